```python
import jax, jax.numpy as jnp
from jax import lax
import numpy as np

D_MODEL = 1024
BATCH = 8
SEQ = 16384
DEPTH = 1

CHUNK = 64
EPS = 1e-6
GDN_HEADS = 4
GDN_DK = 128
GDN_DV = 128
GDN_CONV = 4
ATT_HEADS = 8
ATT_DH = 64
ATT_BAND = 9
REL_CLIP = 128
D_FF = 2816
FFN_CONV = 3

KEY_A = GDN_HEADS * GDN_DK
VAL_A = GDN_HEADS * GDN_DV
WIDTH_B = ATT_HEADS * ATT_DH
IN_SIZES = (KEY_A, KEY_A, VAL_A, VAL_A, GDN_HEADS, GDN_HEADS,
            WIDTH_B, WIDTH_B, WIDTH_B, D_MODEL, D_MODEL)
IN_SPLITS = tuple(sum(IN_SIZES[:i + 1]) for i in range(len(IN_SIZES) - 1))
D_IN = sum(IN_SIZES)
CONV_A = 2 * KEY_A + VAL_A

kernel_name = "hybrid_gdn_bandattn_convffn"


def rmsnorm(x, w):
    xf = x.astype(jnp.float32)
    y = xf * lax.rsqrt(jnp.mean(xf * xf, axis=-1, keepdims=True) + EPS)
    return (y * w.astype(jnp.float32)).astype(x.dtype)


def l2norm(x):
    xf = x.astype(jnp.float32)
    return xf * lax.rsqrt(jnp.sum(xf * xf, axis=-1, keepdims=True) + EPS)


def causal_dwconv(x, w):
    width = w.shape[0]
    return lax.conv_general_dilated(
        x, w[:, None, :].astype(x.dtype), window_strides=(1,), padding=[(width - 1, 0)],
        dimension_numbers=('NWC', 'WIO', 'NWC'), feature_group_count=x.shape[-1])


def gated_delta_rule(q, k, v, g, beta):
    b_, t_, h_, dk = q.shape
    dv = v.shape[-1]
    n = t_ // CHUNK

    def to_chunks(a):
        return a.astype(jnp.float32).reshape(b_, n, CHUNK, h_, -1).transpose(1, 0, 3, 2, 4)

    q = to_chunks(q) * (dk ** -0.5)
    k = to_chunks(k)
    v = to_chunks(v)
    g = g.astype(jnp.float32).reshape(b_, n, CHUNK, h_).transpose(1, 0, 3, 2)
    beta = beta.astype(jnp.float32).reshape(b_, n, CHUNK, h_).transpose(1, 0, 3, 2)

    G = jnp.cumsum(g, axis=-1)
    idx = jnp.arange(CHUNK)
    strict = idx[:, None] > idx[None, :]
    incl = idx[:, None] >= idx[None, :]
    diff = G[..., :, None] - G[..., None, :]
    dec_strict = jnp.exp(jnp.where(strict, diff, -jnp.inf))
    dec_incl = jnp.exp(jnp.where(incl, diff, -jnp.inf))
    gam = jnp.exp(G)

    a_mat = beta[..., :, None] * jnp.einsum('nbhid,nbhjd->nbhij', k, k) * dec_strict
    eye = jnp.eye(CHUNK, dtype=jnp.float32)
    rhs = jnp.concatenate([(beta * gam)[..., None] * k, beta[..., None] * v], axis=-1)
    sol = lax.linalg.triangular_solve(a_mat + eye, rhs, left_side=True, lower=True)
    w_c = sol[..., :dk]
    uv_c = sol[..., dk:]
    p_c = jnp.einsum('nbhid,nbhjd->nbhij', q, k) * dec_incl
    qg_c = q * gam[..., None]
    kd_c = k * jnp.exp(G[..., -1:] - G)[..., None]
    gl_c = gam[..., -1]

    def step(s, xs):
        w_i, uv_i, p_i, qg_i, kd_i, gl_i = xs
        u = uv_i - jnp.einsum('bhid,bhde->bhie', w_i, s)
        o = jnp.einsum('bhid,bhde->bhie', qg_i, s) + jnp.einsum('bhij,bhje->bhie', p_i, u)
        s = gl_i[..., None, None] * s + jnp.einsum('bhid,bhie->bhde', kd_i, u)
        return s, o

    s0 = jnp.zeros((b_, h_, dk, dv), jnp.float32)
    _, o = lax.scan(step, s0, (w_c, uv_c, p_c, qg_c, kd_c, gl_c))
    return o.transpose(1, 0, 3, 2, 4).reshape(b_, t_, h_, dv)


def chunk_band_attention(q, k, v, rel_table):
    b_, t_, h_, dh = q.shape
    n = t_ // CHUNK
    band = ATT_BAND * CHUNK
    lead = (ATT_BAND - 1) * CHUNK
    kp = jnp.pad(k, ((0, 0), (lead, 0), (0, 0), (0, 0)))
    vp = jnp.pad(v, ((0, 0), (lead, 0), (0, 0), (0, 0)))
    r = jnp.arange(CHUNK)
    j = jnp.arange(band)
    dist = lead + r[:, None] - j[None, :]
    bias = rel_table.astype(jnp.float32)[:, jnp.clip(dist, -REL_CLIP, REL_CLIP) + REL_CLIP]
    scale = dh ** -0.5

    def one_chunk(c):
        start = c * CHUNK
        qc = lax.dynamic_slice_in_dim(q, start, CHUNK, axis=1)
        kc = lax.dynamic_slice_in_dim(kp, start, band, axis=1)
        vc = lax.dynamic_slice_in_dim(vp, start, band, axis=1)
        s = jnp.einsum('bqhd,bkhd->bhqk', qc, kc).astype(jnp.float32) * scale + bias
        valid = j >= lead - start
        s = jnp.where(valid[None, None, None, :], s, -jnp.inf)
        p = jax.nn.softmax(s, axis=-1).astype(v.dtype)
        return jnp.einsum('bhqk,bkhd->bqhd', p, vc)

    out = lax.map(one_chunk, jnp.arange(n))
    return out.transpose(1, 0, 2, 3, 4).reshape(b_, t_, h_ * dh)


def _fwd_setup_inputs(seed: int = 0) -> dict:
    key = jax.random.key(seed)
    ks = jax.random.split(key, 20)
    f32 = jnp.float32
    nrm = lambda k, shape, s: jax.random.normal(k, shape, f32) * s
    gain = lambda k, shape: 1.0 + 0.02 * jax.random.normal(k, shape, f32)
    dt = jnp.exp(jax.random.uniform(ks[4], (DEPTH, GDN_HEADS), f32, np.log(1e-3), np.log(1e-1)))
    return {
        "x": nrm(ks[0], (BATCH, SEQ, D_MODEL), 1.0),
        "norm_mix_w": gain(ks[1], (DEPTH, D_MODEL)),
        "w_in": nrm(ks[2], (DEPTH, D_MODEL, D_IN), D_MODEL ** -0.5),
        "conv_qkv_w": nrm(ks[3], (DEPTH, GDN_CONV, CONV_A), GDN_CONV ** -0.5),
        "a_log": jnp.log(jax.random.uniform(ks[5], (DEPTH, GDN_HEADS), f32, 1.0, 16.0)),
        "dt_bias": dt + jnp.log(-jnp.expm1(-dt)),
        "gdn_norm_w": gain(ks[6], (DEPTH, GDN_DV)),
        "w_branch_a": nrm(ks[7], (DEPTH, VAL_A, D_MODEL), VAL_A ** -0.5),
        "w_branch_b": nrm(ks[8], (DEPTH, WIDTH_B, D_MODEL), WIDTH_B ** -0.5),
        "rel_bias": nrm(ks[9], (DEPTH, ATT_HEADS, 2 * REL_CLIP + 1), 0.5),
        "w_out": nrm(ks[10], (DEPTH, D_MODEL, D_MODEL), D_MODEL ** -0.5),
        "norm_ffn_w": gain(ks[11], (DEPTH, D_MODEL)),
        "w_up": nrm(ks[12], (DEPTH, D_MODEL, 2 * D_FF), D_MODEL ** -0.5),
        "conv_ffn_w": nrm(ks[13], (DEPTH, FFN_CONV, 2 * D_FF), FFN_CONV ** -0.5),
        "conv_ffn_b": nrm(ks[14], (DEPTH, 2 * D_FF), 0.02),
        "w_down": nrm(ks[15], (DEPTH, D_FF, D_MODEL), D_FF ** -0.5),
        "norm_final_w": gain(ks[16], (D_MODEL,)),
    }


def _fwd_reference(x, norm_mix_w, w_in, conv_qkv_w, a_log, dt_bias, gdn_norm_w, w_branch_a, w_branch_b,
              rel_bias, w_out, norm_ffn_w, w_up, conv_ffn_w, conv_ffn_b, w_down, norm_final_w):
    b_, t_, _ = x.shape
    for l in range(DEPTH):
        h = rmsnorm(x, norm_mix_w[l])
        proj = h @ w_in[l]
        qa, ka, va, za, ba, aa, qb, kb, vb, ga, gb = jnp.split(proj, IN_SPLITS, axis=-1)

        qkv = jax.nn.silu(causal_dwconv(jnp.concatenate([qa, ka, va], axis=-1), conv_qkv_w[l]))
        qa, ka, va = jnp.split(qkv, (KEY_A, 2 * KEY_A), axis=-1)
        qa = l2norm(qa.reshape(b_, t_, GDN_HEADS, GDN_DK))
        ka = l2norm(ka.reshape(b_, t_, GDN_HEADS, GDN_DK))
        va = va.reshape(b_, t_, GDN_HEADS, GDN_DV)
        beta = jax.nn.sigmoid(ba.astype(jnp.float32))
        g = -jnp.exp(a_log[l].astype(jnp.float32)) * jax.nn.softplus(
            aa.astype(jnp.float32) + dt_bias[l].astype(jnp.float32))
        oa = gated_delta_rule(qa, ka, va, g, beta)
        za = za.reshape(b_, t_, GDN_HEADS, GDN_DV).astype(jnp.float32)
        oa = (rmsnorm(oa, gdn_norm_w[l]) * jax.nn.silu(za)).astype(x.dtype).reshape(b_, t_, VAL_A)

        ob = chunk_band_attention(qb.reshape(b_, t_, ATT_HEADS, ATT_DH),
                                  kb.reshape(b_, t_, ATT_HEADS, ATT_DH),
                                  vb.reshape(b_, t_, ATT_HEADS, ATT_DH), rel_bias[l])

        mix = jax.nn.sigmoid(ga) * (oa @ w_branch_a[l]) + jax.nn.sigmoid(gb) * (ob @ w_branch_b[l])
        x = x + mix @ w_out[l]

        h = rmsnorm(x, norm_ffn_w[l])
        u = causal_dwconv(h @ w_up[l], conv_ffn_w[l]) + conv_ffn_b[l]
        gate, up = jnp.split(u, 2, axis=-1)
        x = x + (jax.nn.silu(gate) * up) @ w_down[l]
    return rmsnorm(x, norm_final_w)


import jax as _jax
import jax.numpy as _jnp

TWIN_FORMAT = 'train_step'
FWD_PARAMS = ['x', 'norm_mix_w', 'w_in', 'conv_qkv_w', 'a_log', 'dt_bias', 'gdn_norm_w', 'w_branch_a', 'w_branch_b', 'rel_bias', 'w_out', 'norm_ffn_w', 'w_up', 'conv_ffn_w', 'conv_ffn_b', 'w_down', 'norm_final_w']
TWIN_WEIGHTS = ['norm_mix_w', 'w_in', 'conv_qkv_w', 'a_log', 'dt_bias', 'gdn_norm_w', 'w_branch_a', 'w_branch_b', 'rel_bias', 'w_out', 'norm_ffn_w', 'w_up', 'conv_ffn_w', 'conv_ffn_b', 'w_down', 'norm_final_w']
TWIN_DIFF_INPUT = 'x'
TWIN_INPUTS = ['x', 'norm_mix_w', 'w_in', 'conv_qkv_w', 'a_log', 'dt_bias', 'gdn_norm_w', 'w_branch_a', 'w_branch_b', 'rel_bias', 'w_out', 'norm_ffn_w', 'w_up', 'conv_ffn_w', 'conv_ffn_b', 'w_down', 'norm_final_w', 'loss_target', 'm_norm_mix_w', 'm_w_in', 'm_conv_qkv_w', 'm_a_log', 'm_dt_bias', 'm_gdn_norm_w', 'm_w_branch_a', 'm_w_branch_b', 'm_rel_bias', 'm_w_out', 'm_norm_ffn_w', 'm_w_up', 'm_conv_ffn_w', 'm_conv_ffn_b', 'm_w_down', 'm_norm_final_w', 'v_norm_mix_w', 'v_w_in', 'v_conv_qkv_w', 'v_a_log', 'v_dt_bias', 'v_gdn_norm_w', 'v_w_branch_a', 'v_w_branch_b', 'v_rel_bias', 'v_w_out', 'v_norm_ffn_w', 'v_w_up', 'v_conv_ffn_w', 'v_conv_ffn_b', 'v_w_down', 'v_norm_final_w']
TWIN_OUTPUTS = ['loss', 'grad_x', 'grad_norm_mix_w', 'grad_w_in', 'grad_conv_qkv_w', 'grad_a_log', 'grad_dt_bias', 'grad_gdn_norm_w', 'grad_w_branch_a', 'grad_w_branch_b', 'grad_rel_bias', 'grad_w_out', 'grad_norm_ffn_w', 'grad_w_up', 'grad_conv_ffn_w', 'grad_conv_ffn_b', 'grad_w_down', 'grad_norm_final_w', 'delta_norm_mix_w', 'delta_w_in', 'delta_conv_qkv_w', 'delta_a_log', 'delta_dt_bias', 'delta_gdn_norm_w', 'delta_w_branch_a', 'delta_w_branch_b', 'delta_rel_bias', 'delta_w_out', 'delta_norm_ffn_w', 'delta_w_up', 'delta_conv_ffn_w', 'delta_conv_ffn_b', 'delta_w_down', 'delta_norm_final_w', 'new_m_norm_mix_w', 'new_m_w_in', 'new_m_conv_qkv_w', 'new_m_a_log', 'new_m_dt_bias', 'new_m_gdn_norm_w', 'new_m_w_branch_a', 'new_m_w_branch_b', 'new_m_rel_bias', 'new_m_w_out', 'new_m_norm_ffn_w', 'new_m_w_up', 'new_m_conv_ffn_w', 'new_m_conv_ffn_b', 'new_m_w_down', 'new_m_norm_final_w', 'new_v_norm_mix_w', 'new_v_w_in', 'new_v_conv_qkv_w', 'new_v_a_log', 'new_v_dt_bias', 'new_v_gdn_norm_w', 'new_v_w_branch_a', 'new_v_w_branch_b', 'new_v_rel_bias', 'new_v_w_out', 'new_v_norm_ffn_w', 'new_v_w_up', 'new_v_conv_ffn_w', 'new_v_conv_ffn_b', 'new_v_w_down', 'new_v_norm_final_w']
TWIN_LEAF_KINDS = {'loss': 'loss', 'grad_x': 'grad_x', 'grad_norm_mix_w': 'grad_w', 'grad_w_in': 'grad_w', 'grad_conv_qkv_w': 'grad_w', 'grad_a_log': 'grad_w', 'grad_dt_bias': 'grad_w', 'grad_gdn_norm_w': 'grad_w', 'grad_w_branch_a': 'grad_w', 'grad_w_branch_b': 'grad_w', 'grad_rel_bias': 'grad_w', 'grad_w_out': 'grad_w', 'grad_norm_ffn_w': 'grad_w', 'grad_w_up': 'grad_w', 'grad_conv_ffn_w': 'grad_w', 'grad_conv_ffn_b': 'grad_w', 'grad_w_down': 'grad_w', 'grad_norm_final_w': 'grad_w', 'delta_norm_mix_w': 'delta_w', 'delta_w_in': 'delta_w', 'delta_conv_qkv_w': 'delta_w', 'delta_a_log': 'delta_w', 'delta_dt_bias': 'delta_w', 'delta_gdn_norm_w': 'delta_w', 'delta_w_branch_a': 'delta_w', 'delta_w_branch_b': 'delta_w', 'delta_rel_bias': 'delta_w', 'delta_w_out': 'delta_w', 'delta_norm_ffn_w': 'delta_w', 'delta_w_up': 'delta_w', 'delta_conv_ffn_w': 'delta_w', 'delta_conv_ffn_b': 'delta_w', 'delta_w_down': 'delta_w', 'delta_norm_final_w': 'delta_w', 'new_m_norm_mix_w': 'new_m', 'new_m_w_in': 'new_m', 'new_m_conv_qkv_w': 'new_m', 'new_m_a_log': 'new_m', 'new_m_dt_bias': 'new_m', 'new_m_gdn_norm_w': 'new_m', 'new_m_w_branch_a': 'new_m', 'new_m_w_branch_b': 'new_m', 'new_m_rel_bias': 'new_m', 'new_m_w_out': 'new_m', 'new_m_norm_ffn_w': 'new_m', 'new_m_w_up': 'new_m', 'new_m_conv_ffn_w': 'new_m', 'new_m_conv_ffn_b': 'new_m', 'new_m_w_down': 'new_m', 'new_m_norm_final_w': 'new_m', 'new_v_norm_mix_w': 'new_v', 'new_v_w_in': 'new_v', 'new_v_conv_qkv_w': 'new_v', 'new_v_a_log': 'new_v', 'new_v_dt_bias': 'new_v', 'new_v_gdn_norm_w': 'new_v', 'new_v_w_branch_a': 'new_v', 'new_v_w_branch_b': 'new_v', 'new_v_rel_bias': 'new_v', 'new_v_w_out': 'new_v', 'new_v_norm_ffn_w': 'new_v', 'new_v_w_up': 'new_v', 'new_v_conv_ffn_w': 'new_v', 'new_v_conv_ffn_b': 'new_v', 'new_v_w_down': 'new_v', 'new_v_norm_final_w': 'new_v'}


def _forward(args):
    return _fwd_reference(*[args[k] for k in FWD_PARAMS])


def _output_shape():
    def fwd():
        inp = _fwd_setup_inputs(0)
        return _fwd_reference(*[inp[k] for k in FWD_PARAMS])
    out = _jax.eval_shape(fwd)
    return out.shape, out.dtype

N_MICROBATCH = 1
ADAM_LR = 0.001
ADAM_B1 = 0.9
ADAM_B2 = 0.999
ADAM_EPS = 1e-08
ADAM_WD = 0.01
ADAM_STEP = 10
PER_EXAMPLE_BATCH_AXIS = {'x': 0, 'loss_target': 0}
SHARED_INPUTS = []
_WEIGHT_DTYPES = {'norm_mix_w': _jnp.float32, 'w_in': _jnp.float32, 'conv_qkv_w': _jnp.float32, 'a_log': _jnp.float32, 'dt_bias': _jnp.float32, 'gdn_norm_w': _jnp.float32, 'w_branch_a': _jnp.float32, 'w_branch_b': _jnp.float32, 'rel_bias': _jnp.float32, 'w_out': _jnp.float32, 'norm_ffn_w': _jnp.float32, 'w_up': _jnp.float32, 'conv_ffn_w': _jnp.float32, 'conv_ffn_b': _jnp.float32, 'w_down': _jnp.float32, 'norm_final_w': _jnp.float32}
MOMENT_SCALE = {'norm_mix_w': 2.234907e-01, 'w_in': 9.037046e-02, 'conv_qkv_w': 1.299316e-01, 'a_log': 2.362204e+00, 'dt_bias': 2.304571e+00, 'gdn_norm_w': 3.776299e-01, 'w_branch_a': 1.237177e-01, 'w_branch_b': 3.359108e-02, 'rel_bias': 2.052591e-02, 'w_out': 1.255600e-01, 'norm_ffn_w': 2.584284e-01, 'w_up': 1.122770e-01, 'conv_ffn_w': 1.113949e-01, 'conv_ffn_b': 1.141588e-01, 'w_down': 1.832588e-01, 'norm_final_w': 1.280240e+02}


def _to_microbatches(a, axis):
    t = _jnp.moveaxis(a, axis, 0)
    t = t.reshape((N_MICROBATCH, t.shape[0] // N_MICROBATCH) + t.shape[1:])
    return _jnp.moveaxis(t, 1, axis + 1)


def setup_inputs(seed: int = 0) -> dict:
    inp = _fwd_setup_inputs(seed)
    key = _jax.random.fold_in(_jax.random.key(seed), 7919)
    shape, _ = _output_shape()
    out = dict(inp)
    out["loss_target"] = _jax.random.normal(_jax.random.fold_in(key, 0), shape, _jnp.float32)
    for i, name in enumerate(TWIN_WEIGHTS):
        w = inp[name].astype(_jnp.float32)
        if MOMENT_SCALE is None:
            s = _jnp.sqrt(_jnp.mean(_jnp.square(w)) + 1e-30)
        else:
            s = MOMENT_SCALE[name]
        km, kv = _jax.random.split(_jax.random.fold_in(key, i + 1))
        out[name] = w
        out["m_" + name] = s * _jax.random.normal(km, w.shape, _jnp.float32)
        out["v_" + name] = (s * s) * _jax.random.uniform(kv, w.shape, _jnp.float32, 0.5, 1.5)
    if N_MICROBATCH > 1:
        for name, axis in PER_EXAMPLE_BATCH_AXIS.items():
            out[name] = _to_microbatches(out[name], axis)
    return {'x': out['x'], 'norm_mix_w': out['norm_mix_w'], 'w_in': out['w_in'], 'conv_qkv_w': out['conv_qkv_w'], 'a_log': out['a_log'], 'dt_bias': out['dt_bias'], 'gdn_norm_w': out['gdn_norm_w'], 'w_branch_a': out['w_branch_a'], 'w_branch_b': out['w_branch_b'], 'rel_bias': out['rel_bias'], 'w_out': out['w_out'], 'norm_ffn_w': out['norm_ffn_w'], 'w_up': out['w_up'], 'conv_ffn_w': out['conv_ffn_w'], 'conv_ffn_b': out['conv_ffn_b'], 'w_down': out['w_down'], 'norm_final_w': out['norm_final_w'], 'loss_target': out['loss_target'], 'm_norm_mix_w': out['m_norm_mix_w'], 'm_w_in': out['m_w_in'], 'm_conv_qkv_w': out['m_conv_qkv_w'], 'm_a_log': out['m_a_log'], 'm_dt_bias': out['m_dt_bias'], 'm_gdn_norm_w': out['m_gdn_norm_w'], 'm_w_branch_a': out['m_w_branch_a'], 'm_w_branch_b': out['m_w_branch_b'], 'm_rel_bias': out['m_rel_bias'], 'm_w_out': out['m_w_out'], 'm_norm_ffn_w': out['m_norm_ffn_w'], 'm_w_up': out['m_w_up'], 'm_conv_ffn_w': out['m_conv_ffn_w'], 'm_conv_ffn_b': out['m_conv_ffn_b'], 'm_w_down': out['m_w_down'], 'm_norm_final_w': out['m_norm_final_w'], 'v_norm_mix_w': out['v_norm_mix_w'], 'v_w_in': out['v_w_in'], 'v_conv_qkv_w': out['v_conv_qkv_w'], 'v_a_log': out['v_a_log'], 'v_dt_bias': out['v_dt_bias'], 'v_gdn_norm_w': out['v_gdn_norm_w'], 'v_w_branch_a': out['v_w_branch_a'], 'v_w_branch_b': out['v_w_branch_b'], 'v_rel_bias': out['v_rel_bias'], 'v_w_out': out['v_w_out'], 'v_norm_ffn_w': out['v_norm_ffn_w'], 'v_w_up': out['v_w_up'], 'v_conv_ffn_w': out['v_conv_ffn_w'], 'v_conv_ffn_b': out['v_conv_ffn_b'], 'v_w_down': out['v_w_down'], 'v_norm_final_w': out['v_norm_final_w']}


def _loss(weights, diff, rest, loss_target):
    with _jax.named_scope("forward"):
        args = {**rest, TWIN_DIFF_INPUT: diff, **{k: w.astype(_WEIGHT_DTYPES[k]) for k, w in weights.items()}}
        y = _forward(args)
    with _jax.named_scope("loss_head"):
        err = _jnp.square(y.astype(_jnp.float32) - loss_target)
        return 0.5 * _jnp.sum(_jnp.mean(err, axis=-1)) if err.ndim else 0.5 * err


def _adamw(w, g, m, v):
    m = ADAM_B1 * m + (1.0 - ADAM_B1) * g
    v = ADAM_B2 * v + (1.0 - ADAM_B2) * _jnp.square(g)
    m_hat = m / (1.0 - ADAM_B1 ** ADAM_STEP)
    v_hat = v / (1.0 - ADAM_B2 ** ADAM_STEP)
    delta = -ADAM_LR * (m_hat / (_jnp.sqrt(v_hat) + ADAM_EPS) + ADAM_WD * w)
    return delta, m, v


def reference(x, norm_mix_w, w_in, conv_qkv_w, a_log, dt_bias, gdn_norm_w, w_branch_a, w_branch_b, rel_bias, w_out, norm_ffn_w, w_up, conv_ffn_w, conv_ffn_b, w_down, norm_final_w, loss_target, m_norm_mix_w, m_w_in, m_conv_qkv_w, m_a_log, m_dt_bias, m_gdn_norm_w, m_w_branch_a, m_w_branch_b, m_rel_bias, m_w_out, m_norm_ffn_w, m_w_up, m_conv_ffn_w, m_conv_ffn_b, m_w_down, m_norm_final_w, v_norm_mix_w, v_w_in, v_conv_qkv_w, v_a_log, v_dt_bias, v_gdn_norm_w, v_w_branch_a, v_w_branch_b, v_rel_bias, v_w_out, v_norm_ffn_w, v_w_up, v_conv_ffn_w, v_conv_ffn_b, v_w_down, v_norm_final_w):
    given = dict(x=x, norm_mix_w=norm_mix_w, w_in=w_in, conv_qkv_w=conv_qkv_w, a_log=a_log, dt_bias=dt_bias, gdn_norm_w=gdn_norm_w, w_branch_a=w_branch_a, w_branch_b=w_branch_b, rel_bias=rel_bias, w_out=w_out, norm_ffn_w=norm_ffn_w, w_up=w_up, conv_ffn_w=conv_ffn_w, conv_ffn_b=conv_ffn_b, w_down=w_down, norm_final_w=norm_final_w, loss_target=loss_target, m_norm_mix_w=m_norm_mix_w, m_w_in=m_w_in, m_conv_qkv_w=m_conv_qkv_w, m_a_log=m_a_log, m_dt_bias=m_dt_bias, m_gdn_norm_w=m_gdn_norm_w, m_w_branch_a=m_w_branch_a, m_w_branch_b=m_w_branch_b, m_rel_bias=m_rel_bias, m_w_out=m_w_out, m_norm_ffn_w=m_norm_ffn_w, m_w_up=m_w_up, m_conv_ffn_w=m_conv_ffn_w, m_conv_ffn_b=m_conv_ffn_b, m_w_down=m_w_down, m_norm_final_w=m_norm_final_w, v_norm_mix_w=v_norm_mix_w, v_w_in=v_w_in, v_conv_qkv_w=v_conv_qkv_w, v_a_log=v_a_log, v_dt_bias=v_dt_bias, v_gdn_norm_w=v_gdn_norm_w, v_w_branch_a=v_w_branch_a, v_w_branch_b=v_w_branch_b, v_rel_bias=v_rel_bias, v_w_out=v_w_out, v_norm_ffn_w=v_norm_ffn_w, v_w_up=v_w_up, v_conv_ffn_w=v_conv_ffn_w, v_conv_ffn_b=v_conv_ffn_b, v_w_down=v_w_down, v_norm_final_w=v_norm_final_w)
    weights = {n: given[n] for n in TWIN_WEIGHTS}
    shared = {n: given[n] for n in SHARED_INPUTS}
    per_example = {n: given[n] for n in ['x']}
    grad_fn = _jax.value_and_grad(_loss, argnums=(0, 1))

    def one_microbatch(ex, loss_target):
        ex = dict(ex)
        diff = ex.pop(TWIN_DIFF_INPUT)
        return grad_fn(weights, diff, {**shared, **ex}, loss_target)

    if N_MICROBATCH == 1:
        loss, (grad_w, grad_x) = one_microbatch(per_example, given["loss_target"])
    else:
        def body(carry, xs):
            loss_sum, grad_sum = carry
            l_k, (gw_k, gx_k) = one_microbatch(xs[0], xs[1])
            with _jax.named_scope("update"):
                return (loss_sum + l_k, _jax.tree.map(_jnp.add, grad_sum, gw_k)), gx_k

        init = (_jnp.zeros((), _jnp.float32), _jax.tree.map(_jnp.zeros_like, weights))
        (loss, grad_w), grad_x = _jax.lax.scan(body, init, (per_example, given["loss_target"]))
    with _jax.named_scope("update"):
        delta_w, new_m, new_v = {}, {}, {}
        for n in TWIN_WEIGHTS:
            delta_w[n], new_m[n], new_v[n] = _adamw(weights[n], grad_w[n], given["m_" + n], given["v_" + n])
    return (loss, grad_x, *[grad_w[n] for n in TWIN_WEIGHTS], *[delta_w[n] for n in TWIN_WEIGHTS],
            *[new_m[n] for n in TWIN_WEIGHTS], *[new_v[n] for n in TWIN_WEIGHTS])
```

```python
import functools
import math

import numpy as np
import jax
import jax.numpy as jnp
from jax import lax
from jax.experimental import pallas as pl
from jax.experimental.pallas import tpu as pltpu

F32, BF16 = jnp.float32, jnp.bfloat16
HIGHEST = lax.Precision.HIGHEST

N_DEV = 8
D_MODEL = 1024
CHUNK = 64
EPS = 1e-6
GDN_HEADS, GDN_DK = 4, 128
KEY_A = GDN_HEADS * GDN_DK
GDN_CONV = 4
ATT_HEADS, ATT_DH = 8, 64
WIDTH_B = ATT_HEADS * ATT_DH
ATT_BAND = 9
REL_CLIP = 128
D_FF = 2816
FFN_CONV = 3
D_IN = 5640
ADAM_LR, ADAM_B1, ADAM_B2, ADAM_EPS, ADAM_WD, ADAM_STEP = 0.001, 0.9, 0.999, 1e-08, 0.01, 10

LANES = 128
SUBLANES = 8
NEG = -1e30

PROJ_W = 5760
PB = 512
CB_GA, CB_GB = 0, 1
CB_QA, CB_KA, CB_VA, CB_ZA, CB_QB, CB_KB, CB_VB = 4, 5, 6, 7, 8, 9, 10
CB_BD = 44

ATT_QB = 256
ATT_KW = 768
ATT_VEC = 1024


def _dot(a, b, precision=None):
    return jnp.dot(a, b, preferred_element_type=F32, precision=precision)


def _dot_nt(a, b, precision=None):
    return lax.dot_general(a, b, (((1,), (1,)), ((), ())), preferred_element_type=F32, precision=precision)


def _dot_tn(a, b):
    return lax.dot_general(a, b, (((0,), (0,)), ((), ())), preferred_element_type=F32)


def _sigmoid(x):
    return 1.0 / (1.0 + jnp.exp(-x))


def _softplus(x):
    return jnp.maximum(x, 0.0) + jnp.log(1.0 + jnp.exp(-jnp.abs(x)))


def _cparams(*sem):
    return pltpu.CompilerParams(dimension_semantics=tuple(sem))


def _rmsnorm_cast(x, w, name, tm=512):
    t, d = x.shape

    def body(x_ref, w_ref, o_ref):
        xv = x_ref[...]
        r = lax.rsqrt(jnp.mean(xv * xv, axis=-1, keepdims=True) + EPS)
        o_ref[...] = (xv * r * w_ref[...]).astype(BF16)

    return pl.pallas_call(
        body, name=name, grid=(t // tm,),
        in_specs=[pl.BlockSpec((tm, d), lambda i: (i, 0)), pl.BlockSpec((1, d), lambda i: (0, 0))],
        out_specs=pl.BlockSpec((tm, d), lambda i: (i, 0)),
        out_shape=jax.ShapeDtypeStruct((t, d), BF16),
        compiler_params=_cparams("parallel"),
    )(x, w)


def _mm_nn(a, b, out_dtype, name, tm, tn, tk):
    m, k = a.shape
    _, n = b.shape
    nk = k // tk
    assert m % tm == 0 and n % tn == 0 and k % tk == 0

    if nk == 1:
        def body(a_ref, b_ref, o_ref):
            o_ref[...] = _dot(a_ref[...], b_ref[...]).astype(out_dtype)
        scratch = []
    else:
        def body(a_ref, b_ref, o_ref, acc_ref):
            kk = pl.program_id(2)

            @pl.when(kk == 0)
            def _():
                acc_ref[...] = jnp.zeros_like(acc_ref)

            acc_ref[...] += _dot(a_ref[...], b_ref[...])

            @pl.when(kk == nk - 1)
            def _():
                o_ref[...] = acc_ref[...].astype(out_dtype)
        scratch = [pltpu.VMEM((tm, tn), F32)]

    return pl.pallas_call(
        body, name=name, grid=(m // tm, n // tn, nk),
        in_specs=[pl.BlockSpec((tm, tk), lambda i, j, kk: (i, kk)),
                  pl.BlockSpec((tk, tn), lambda i, j, kk: (kk, j))],
        out_specs=pl.BlockSpec((tm, tn), lambda i, j, kk: (i, j)),
        out_shape=jax.ShapeDtypeStruct((m, n), out_dtype),
        scratch_shapes=scratch,
        compiler_params=_cparams("parallel", "parallel", "arbitrary"),
    )(a, b)


def _mm_tn(a, b, name, tn, tk=512):
    t, m = a.shape
    _, n = b.shape
    assert t % tk == 0 and n % tn == 0

    def body(a_ref, b_ref, o_ref):
        @pl.when(pl.program_id(1) == 0)
        def _():
            o_ref[...] = jnp.zeros_like(o_ref)

        o_ref[...] += _dot_tn(a_ref[...], b_ref[...])

    return pl.pallas_call(
        body, name=name, grid=(n // tn, t // tk),
        in_specs=[pl.BlockSpec((tk, m), lambda j, s: (s, 0)),
                  pl.BlockSpec((tk, tn), lambda j, s: (s, j))],
        out_specs=pl.BlockSpec((m, tn), lambda j, s: (0, j)),
        out_shape=jax.ShapeDtypeStruct((m, n), F32),
        compiler_params=_cparams("parallel", "arbitrary"),
    )(a, b)


def _rms_bwd(dh, x, w, dres, name, tm=512):
    t, d = x.shape

    def body(dh_ref, x_ref, w_ref, dres_ref, dx_ref, dxb_ref, dw_ref):
        @pl.when(pl.program_id(0) == 0)
        def _():
            dw_ref[...] = jnp.zeros_like(dw_ref)

        xv = x_ref[...]
        r = lax.rsqrt(jnp.mean(xv * xv, axis=-1, keepdims=True) + EPS)
        xh = xv * r
        dhv = dh_ref[...]
        dw_ref[0:1, :] += jnp.sum(dhv * xh, axis=0, keepdims=True)
        dxh = dhv * w_ref[...]
        dx = dres_ref[...] + r * (dxh - xh * jnp.mean(dxh * xh, axis=-1, keepdims=True))
        dx_ref[...] = dx
        dxb_ref[...] = dx.astype(BF16)

    row = pl.BlockSpec((tm, d), lambda i: (i, 0))
    return pl.pallas_call(
        body, name=name, grid=(t // tm,),
        in_specs=[row, row, pl.BlockSpec((1, d), lambda i: (0, 0)), row],
        out_specs=[row, row, pl.BlockSpec((SUBLANES, d), lambda i: (0, 0))],
        out_shape=[jax.ShapeDtypeStruct((t, d), F32), jax.ShapeDtypeStruct((t, d), BF16),
                   jax.ShapeDtypeStruct((SUBLANES, d), F32)],
        compiler_params=_cparams("arbitrary"),
    )(dh, x, w, dres)


def _rel_index(dist):
    return np.clip(dist, -REL_CLIP, REL_CLIP) + REL_CLIP


def _bias_onehots():
    tw = 3 * LANES
    m = np.arange(ATT_VEC)
    dq = np.where(m <= ATT_KW, 512 - m, 512 - (m - ATT_VEC))
    dk = np.where(m < ATT_KW, m, m - ATT_VEC)
    ohq = np.zeros((tw, ATT_VEC), np.float32)
    ohk = np.zeros((tw, ATT_VEC), np.float32)
    ohq[_rel_index(dq), m] = 1.0
    ohk[_rel_index(dk), m] = 1.0
    return ohq, ohk


def _att_bias(table_pad):
    ohq, ohk = _bias_onehots()
    nslab = ATT_QB // SUBLANES

    def body(t_ref, ohq_ref, ohk_ref, bq_ref, bk_ref):
        tv = jnp.broadcast_to(t_ref[...], (SUBLANES, 3 * LANES))
        row = lax.broadcasted_iota(jnp.int32, (ATT_QB, ATT_KW), 0) // CHUNK
        col = lax.broadcasted_iota(jnp.int32, (ATT_QB, ATT_KW), 1) // CHUNK
        band = (col >= row) & (col <= row + ATT_BAND - 1)
        for oh_ref, out_ref in ((ohq_ref, bq_ref), (ohk_ref, bk_ref)):
            vec = _dot(tv, oh_ref[...], HIGHEST)[0:1, :]
            slab = jnp.concatenate([vec if b == 0 else pltpu.roll(vec, b, 1) for b in range(SUBLANES)], axis=0)
            rows = [slab if a == 0 else pltpu.roll(slab, SUBLANES * a, 1) for a in range(nslab)]
            full = jnp.concatenate(rows, axis=0)[:, :ATT_KW]
            out_ref[...] = jnp.where(band, full, NEG)

    h = table_pad.shape[0]
    oh_spec = pl.BlockSpec((3 * LANES, ATT_VEC), lambda i: (0, 0))
    out_spec = pl.BlockSpec((None, ATT_QB, ATT_KW), lambda i: (i, 0, 0))
    return pl.pallas_call(
        body, name="att_bias", grid=(h,),
        in_specs=[pl.BlockSpec((None, 1, 3 * LANES), lambda i: (i, 0, 0)), oh_spec, oh_spec],
        out_specs=[out_spec, out_spec],
        out_shape=[jax.ShapeDtypeStruct((h, ATT_QB, ATT_KW), F32)] * 2,
        compiler_params=_cparams("parallel"),
    )(table_pad, jnp.asarray(ohq), jnp.asarray(ohk))


def _head_masks():
    lane = lax.broadcasted_iota(jnp.int32, (1, LANES), 1)
    return [lane < ATT_DH, lane >= ATT_DH]


def _att_fwd(proj, bias_q):
    t = proj.shape[0]
    nb = t // ATT_QB
    scale = ATT_DH ** -0.5

    def body(q_ref, k0_ref, k1_ref, k2_ref, v0_ref, v1_ref, v2_ref, b_ref, o_ref, lse_ref, lset_ref):
        i = pl.program_id(0)
        q = q_ref[...].astype(BF16)
        kk = jnp.concatenate([k0_ref[...], k1_ref[...], k2_ref[...]], axis=0).astype(BF16)
        vv = jnp.concatenate([v0_ref[...], v1_ref[...], v2_ref[...]], axis=0).astype(BF16)
        kpos = lax.broadcasted_iota(jnp.int32, (1, ATT_KW), 1) + (i - 2) * ATT_QB
        valid = kpos >= 0
        lane = lax.broadcasted_iota(jnp.int32, (1, LANES), 1)
        masks = _head_masks()
        lse_cols = jnp.zeros((ATT_QB, LANES), F32)
        for p in range(ATT_HEADS // 2):
            cs = slice(p * LANES, (p + 1) * LANES)
            qt, kt, vt = q[:, cs], kk[:, cs], vv[:, cs]
            acc = jnp.zeros((ATT_QB, LANES), F32)
            for sub in range(2):
                h = 2 * p + sub
                s = _dot_nt(jnp.where(masks[sub], qt, 0), kt) * scale + b_ref[h]
                s = jnp.where(valid, s, NEG)
                mx = jnp.max(s, axis=-1, keepdims=True)
                e = jnp.exp(s - mx)
                l = jnp.sum(e, axis=-1, keepdims=True)
                pn = (e / l).astype(BF16)
                acc = acc + _dot(pn, jnp.where(masks[sub], vt, 0))
                lse_cols = lse_cols + jnp.where(lane == h, mx + jnp.log(l), 0.0)
            o_ref[:, cs] = acc.astype(BF16)
        lse_ref[...] = lse_cols
        lset_ref[...] = lse_cols.T[0:SUBLANES, :]

    def kv_spec(off, cb):
        return pl.BlockSpec((ATT_QB, PB), lambda i: (jnp.maximum(i + off, 0), cb))

    return pl.pallas_call(
        body, name="att_fwd", grid=(nb,),
        in_specs=[pl.BlockSpec((ATT_QB, PB), lambda i: (i, CB_QB)),
                  kv_spec(-2, CB_KB), kv_spec(-1, CB_KB), kv_spec(0, CB_KB),
                  kv_spec(-2, CB_VB), kv_spec(-1, CB_VB), kv_spec(0, CB_VB),
                  pl.BlockSpec((ATT_HEADS, ATT_QB, ATT_KW), lambda i: (0, 0, 0))],
        out_specs=[pl.BlockSpec((ATT_QB, WIDTH_B), lambda i: (i, 0)),
                   pl.BlockSpec((ATT_QB, LANES), lambda i: (i, 0)),
                   pl.BlockSpec((SUBLANES, ATT_QB), lambda i: (0, i))],
        out_shape=[jax.ShapeDtypeStruct((t, WIDTH_B), BF16), jax.ShapeDtypeStruct((t, LANES), F32),
                   jax.ShapeDtypeStruct((SUBLANES, t), F32)],
        compiler_params=_cparams("parallel"),
    )(proj, proj, proj, proj, proj, proj, proj, bias_q)


def _att_dq(proj, bias_q, lse, d_ob):
    t = proj.shape[0]
    nb = t // ATT_QB
    scale = ATT_DH ** -0.5
    nslab = ATT_QB // SUBLANES

    def body(q_ref, k0_ref, k1_ref, k2_ref, v0_ref, v1_ref, v2_ref, b_ref, lse_ref, do_ref,
             dq_ref, dlt_ref, slab_ref):
        i = pl.program_id(0)

        @pl.when(i == 0)
        def _():
            slab_ref[...] = jnp.zeros_like(slab_ref)

        q = q_ref[...].astype(BF16)
        kk = jnp.concatenate([k0_ref[...], k1_ref[...], k2_ref[...]], axis=0).astype(BF16)
        vv = jnp.concatenate([v0_ref[...], v1_ref[...], v2_ref[...]], axis=0).astype(BF16)
        do = do_ref[...].astype(BF16)
        kpos = lax.broadcasted_iota(jnp.int32, (1, ATT_KW), 1) + (i - 2) * ATT_QB
        valid = kpos >= 0
        lane = lax.broadcasted_iota(jnp.int32, (1, LANES), 1)
        masks = _head_masks()
        lse_all = lse_ref[...]
        dlt_cols = jnp.zeros((ATT_QB, LANES), F32)
        zpad = jnp.zeros((SUBLANES, ATT_VEC - ATT_KW), F32)
        for p in range(ATT_HEADS // 2):
            cs = slice(p * LANES, (p + 1) * LANES)
            qt, kt, vt, dot_ = q[:, cs], kk[:, cs], vv[:, cs], do[:, cs]
            acc = jnp.zeros((ATT_QB, LANES), F32)
            for sub in range(2):
                h = 2 * p + sub
                s = _dot_nt(jnp.where(masks[sub], qt, 0), kt) * scale + b_ref[h]
                s = jnp.where(valid, s, NEG)
                pr = jnp.exp(s - lse_all[:, h:h + 1])
                dp = _dot_nt(jnp.where(masks[sub], dot_, 0), vt)
                dl = jnp.sum(pr * dp, axis=-1, keepdims=True)
                ds = pr * (dp - dl)
                acc = acc + _dot((ds * scale).astype(BF16), jnp.where(masks[sub], kt, 0))
                dlt_cols = dlt_cols + jnp.where(lane == h, dl, 0.0)
                sl = jnp.zeros((SUBLANES, ATT_VEC), F32)
                for a in range(nslab):
                    piece = jnp.concatenate([ds[a * SUBLANES:(a + 1) * SUBLANES, :], zpad], axis=1)
                    sl = sl + (piece if a == 0 else pltpu.roll(piece, ATT_VEC - SUBLANES * a, 1))
                slab_ref[h] += sl
            dq_ref[:, cs] = acc.astype(BF16)
        dlt_ref[...] = dlt_cols.T[0:SUBLANES, :]

    def kv_spec(off, cb):
        return pl.BlockSpec((ATT_QB, PB), lambda i: (jnp.maximum(i + off, 0), cb))

    return pl.pallas_call(
        body, name="att_dq", grid=(nb,),
        in_specs=[pl.BlockSpec((ATT_QB, PB), lambda i: (i, CB_QB)),
                  kv_spec(-2, CB_KB), kv_spec(-1, CB_KB), kv_spec(0, CB_KB),
                  kv_spec(-2, CB_VB), kv_spec(-1, CB_VB), kv_spec(0, CB_VB),
                  pl.BlockSpec((ATT_HEADS, ATT_QB, ATT_KW), lambda i: (0, 0, 0)),
                  pl.BlockSpec((ATT_QB, LANES), lambda i: (i, 0)),
                  pl.BlockSpec((ATT_QB, WIDTH_B), lambda i: (i, 0))],
        out_specs=[pl.BlockSpec((ATT_QB, WIDTH_B), lambda i: (i, 0)),
                   pl.BlockSpec((SUBLANES, ATT_QB), lambda i: (0, i)),
                   pl.BlockSpec((ATT_HEADS, SUBLANES, ATT_VEC), lambda i: (0, 0, 0))],
        out_shape=[jax.ShapeDtypeStruct((t, WIDTH_B), BF16), jax.ShapeDtypeStruct((SUBLANES, t), F32),
                   jax.ShapeDtypeStruct((ATT_HEADS, SUBLANES, ATT_VEC), F32)],
        compiler_params=_cparams("arbitrary"),
    )(proj, proj, proj, proj, proj, proj, proj, bias_q, lse, d_ob)


def _att_dkv(proj, bias_k, lse_t, dlt_t, d_ob):
    t = proj.shape[0]
    nb = t // ATT_QB
    scale = ATT_DH ** -0.5

    def body(k_ref, v_ref, q0_ref, q1_ref, q2_ref, d0_ref, d1_ref, d2_ref, l0_ref, l1_ref, l2_ref,
             e0_ref, e1_ref, e2_ref, b_ref, dk_ref, dv_ref):
        i = pl.program_id(0)
        k = k_ref[...].astype(BF16)
        v = v_ref[...].astype(BF16)
        qq = jnp.concatenate([q0_ref[...], q1_ref[...], q2_ref[...]], axis=0).astype(BF16)
        do = jnp.concatenate([d0_ref[...], d1_ref[...], d2_ref[...]], axis=0).astype(BF16)
        lse = jnp.concatenate([l0_ref[...], l1_ref[...], l2_ref[...]], axis=1)
        dlt = jnp.concatenate([e0_ref[...], e1_ref[...], e2_ref[...]], axis=1)
        qpos = lax.broadcasted_iota(jnp.int32, (1, ATT_KW), 1) + i * ATT_QB
        valid = qpos < t
        masks = _head_masks()
        for p in range(ATT_HEADS // 2):
            cs = slice(p * LANES, (p + 1) * LANES)
            kt, vt, qt, dot_ = k[:, cs], v[:, cs], qq[:, cs], do[:, cs]
            acc_k = jnp.zeros((ATT_QB, LANES), F32)
            acc_v = jnp.zeros((ATT_QB, LANES), F32)
            for sub in range(2):
                h = 2 * p + sub
                st = _dot_nt(jnp.where(masks[sub], kt, 0), qt) * scale + b_ref[h]
                st = jnp.where(valid, st, NEG)
                pt = jnp.exp(st - lse[h:h + 1, :])
                dot_m = jnp.where(masks[sub], dot_, 0)
                acc_v = acc_v + _dot(pt.astype(BF16), dot_m)
                dpt = _dot_nt(jnp.where(masks[sub], vt, 0), dot_)
                dst = pt * (dpt - dlt[h:h + 1, :])
                acc_k = acc_k + _dot((dst * scale).astype(BF16), jnp.where(masks[sub], qt, 0))
            dk_ref[:, cs] = acc_k.astype(BF16)
            dv_ref[:, cs] = acc_v.astype(BF16)

    def q_spec(off, cb):
        return pl.BlockSpec((ATT_QB, PB), lambda i: (jnp.minimum(i + off, nb - 1), cb))

    def d_spec(off):
        return pl.BlockSpec((ATT_QB, WIDTH_B), lambda i: (jnp.minimum(i + off, nb - 1), 0))

    def r_spec(off):
        return pl.BlockSpec((SUBLANES, ATT_QB), lambda i: (0, jnp.minimum(i + off, nb - 1)))

    row = pl.BlockSpec((ATT_QB, WIDTH_B), lambda i: (i, 0))
    return pl.pallas_call(
        body, name="att_dkv", grid=(nb,),
        in_specs=[pl.BlockSpec((ATT_QB, PB), lambda i: (i, CB_KB)), pl.BlockSpec((ATT_QB, PB), lambda i: (i, CB_VB)),
                  q_spec(0, CB_QB), q_spec(1, CB_QB), q_spec(2, CB_QB),
                  d_spec(0), d_spec(1), d_spec(2), r_spec(0), r_spec(1), r_spec(2),
                  r_spec(0), r_spec(1), r_spec(2),
                  pl.BlockSpec((ATT_HEADS, ATT_QB, ATT_KW), lambda i: (0, 0, 0))],
        out_specs=[row, row],
        out_shape=[jax.ShapeDtypeStruct((t, WIDTH_B), BF16)] * 2,
        compiler_params=_cparams("parallel"),
    )(proj, proj, proj, proj, proj, d_ob, d_ob, d_ob, lse_t, lse_t, lse_t, dlt_t, dlt_t, dlt_t, bias_k)


def _relbias_grad(slabs):
    ohq, _ = _bias_onehots()

    def body(s_ref, oh_ref, o_ref):
        sv = s_ref[...]
        vec = sv[0:1, :]
        for b in range(1, SUBLANES):
            vec = vec + pltpu.roll(sv[b:b + 1, :], ATT_VEC - b, 1)
        o_ref[...] = _dot_nt(jnp.broadcast_to(vec, (SUBLANES, ATT_VEC)), oh_ref[...], HIGHEST)[0:1, :]

    h = slabs.shape[0]
    return pl.pallas_call(
        body, name="att_dbias", grid=(h,),
        in_specs=[pl.BlockSpec((None, SUBLANES, ATT_VEC), lambda i: (i, 0, 0)),
                  pl.BlockSpec((3 * LANES, ATT_VEC), lambda i: (0, 0))],
        out_specs=pl.BlockSpec((None, 1, 3 * LANES), lambda i: (i, 0, 0)),
        out_shape=jax.ShapeDtypeStruct((h, 1, 3 * LANES), F32),
        compiler_params=_cparams("parallel"),
    )(slabs, jnp.asarray(ohq))


GDN_TM = 512
GDN_CB = 4
HALO = SUBLANES


def _conv_taps(ext, width, lead, n):
    return [(ext if k == width - 1 else pltpu.roll(ext, width - 1 - k, 0))[lead:lead + n] for k in range(width)]


def _prev_halo_spec(tm, width, cb):
    return pl.BlockSpec((HALO, width), lambda i: (jnp.maximum(i * (tm // HALO) - 1, 0), cb))


def _next_halo_spec(tm, width, cb, t):
    return pl.BlockSpec((HALO, width), lambda i: (jnp.minimum((i + 1) * (tm // HALO), t // HALO - 1), cb))


def _gdn_prep_fwd(proj, conv_w):
    t = proj.shape[0]
    tm = GDN_TM

    def body(q_ref, k_ref, v_ref, hq_ref, hk_ref, hv_ref, w_ref, qn_ref, kn_ref, vo_ref):
        first = pl.program_id(0) == 0
        for idx, (x_ref, h_ref, o_ref) in enumerate(((q_ref, hq_ref, qn_ref), (k_ref, hk_ref, kn_ref),
                                                      (v_ref, hv_ref, vo_ref))):
            halo = jnp.where(first, 0.0, h_ref[...])
            ext = jnp.concatenate([halo, x_ref[...]], axis=0)
            w = w_ref[:, idx * KEY_A:(idx + 1) * KEY_A]
            taps = _conv_taps(ext, GDN_CONV, HALO, tm)
            y = sum(w[k:k + 1, :] * taps[k] for k in range(GDN_CONV))
            a = y * _sigmoid(y)
            if idx < 2:
                for h in range(GDN_HEADS):
                    cs = slice(h * GDN_DK, (h + 1) * GDN_DK)
                    seg = a[:, cs]
                    o_ref[:, cs] = seg * lax.rsqrt(jnp.sum(seg * seg, axis=-1, keepdims=True) + EPS)
            else:
                o_ref[...] = a

    row = pl.BlockSpec((tm, KEY_A), lambda i: (i, 0))
    return pl.pallas_call(
        body, name="gdn_prep_fwd", grid=(t // tm,),
        in_specs=[pl.BlockSpec((tm, PB), lambda i: (i, CB_QA)), pl.BlockSpec((tm, PB), lambda i: (i, CB_KA)),
                  pl.BlockSpec((tm, PB), lambda i: (i, CB_VA)),
                  _prev_halo_spec(tm, PB, CB_QA), _prev_halo_spec(tm, PB, CB_KA), _prev_halo_spec(tm, PB, CB_VA),
                  pl.BlockSpec((GDN_CONV, 3 * KEY_A), lambda i: (0, 0))],
        out_specs=[row, row, row],
        out_shape=[jax.ShapeDtypeStruct((t, KEY_A), F32)] * 3,
        compiler_params=_cparams("parallel"),
    )(proj, proj, proj, proj, proj, proj, conv_w)


def _gdn_prep_bwd(proj, conv_w, dqn, dkn, dv):
    t = proj.shape[0]
    tm = GDN_TM
    nt = t // tm
    n_ext = tm + HALO

    def body(q_ref, k_ref, v_ref, pq_ref, pk_ref, pv_ref, nq_ref, nk_ref, nv_ref,
             dq_ref, dk_ref, dv_ref, ndq_ref, ndk_ref, ndv_ref, w_ref, oq_ref, ok_ref, ov_ref, dw_ref):
        i = pl.program_id(0)
        first, last = i == 0, i == nt - 1

        @pl.when(first)
        def _():
            dw_ref[...] = jnp.zeros_like(dw_ref)

        groups = ((q_ref, pq_ref, nq_ref, dq_ref, ndq_ref, oq_ref), (k_ref, pk_ref, nk_ref, dk_ref, ndk_ref, ok_ref),
                  (v_ref, pv_ref, nv_ref, dv_ref, ndv_ref, ov_ref))
        for idx, (x_ref, p_ref, n_ref, d_ref, nd_ref, o_ref) in enumerate(groups):
            cs_all = slice(idx * KEY_A, (idx + 1) * KEY_A)
            ext = jnp.concatenate([jnp.where(first, 0.0, p_ref[...]), x_ref[...], jnp.where(last, 0.0, n_ref[...])], axis=0)
            w = w_ref[:, cs_all]
            taps = _conv_taps(ext, GDN_CONV, HALO, n_ext)
            y = sum(w[k:k + 1, :] * taps[k] for k in range(GDN_CONV))
            sg = _sigmoid(y)
            a = y * sg
            dup = jnp.concatenate([d_ref[...], jnp.where(last, 0.0, nd_ref[...])], axis=0)
            if idx < 2:
                segs = []
                for h in range(GDN_HEADS):
                    cs = slice(h * GDN_DK, (h + 1) * GDN_DK)
                    seg = a[:, cs]
                    r = lax.rsqrt(jnp.sum(seg * seg, axis=-1, keepdims=True) + EPS)
                    nrm = seg * r
                    dn = dup[:, cs]
                    segs.append(r * (dn - nrm * jnp.sum(dn * nrm, axis=-1, keepdims=True)))
                da = jnp.concatenate(segs, axis=1)
            else:
                da = dup
            dy = da * sg * (1.0 + y * (1.0 - sg))
            dx = sum(w[k:k + 1, :] * (dy if k == GDN_CONV - 1 else pltpu.roll(dy, n_ext - (GDN_CONV - 1 - k), 0))[:tm]
                     for k in range(GDN_CONV))
            o_ref[...] = dx.astype(BF16)
            for k in range(GDN_CONV):
                dw_ref[k:k + 1, cs_all] += jnp.sum(dy[:tm] * taps[k][:tm], axis=0, keepdims=True)

    row = pl.BlockSpec((tm, KEY_A), lambda i: (i, 0))
    nrow = _next_halo_spec(tm, KEY_A, 0, t)
    return pl.pallas_call(
        body, name="gdn_prep_bwd", grid=(nt,),
        in_specs=[pl.BlockSpec((tm, PB), lambda i: (i, CB_QA)), pl.BlockSpec((tm, PB), lambda i: (i, CB_KA)),
                  pl.BlockSpec((tm, PB), lambda i: (i, CB_VA)),
                  _prev_halo_spec(tm, PB, CB_QA), _prev_halo_spec(tm, PB, CB_KA), _prev_halo_spec(tm, PB, CB_VA),
                  _next_halo_spec(tm, PB, CB_QA, t), _next_halo_spec(tm, PB, CB_KA, t), _next_halo_spec(tm, PB, CB_VA, t),
                  row, row, row, nrow, nrow, nrow,
                  pl.BlockSpec((GDN_CONV, 3 * KEY_A), lambda i: (0, 0))],
        out_specs=[row, row, row, pl.BlockSpec((SUBLANES, 3 * KEY_A), lambda i: (0, 0))],
        out_shape=[jax.ShapeDtypeStruct((t, KEY_A), BF16)] * 3 + [jax.ShapeDtypeStruct((SUBLANES, 3 * KEY_A), F32)],
        compiler_params=_cparams("arbitrary"),
    )(proj, proj, proj, proj, proj, proj, proj, proj, proj, dqn, dkn, dv, dqn, dkn, dv, conv_w)


def _gdn_chunk_terms(bd, par, kn, qn, h, rows):
    c = CHUNK
    cs = slice(h * GDN_DK, (h + 1) * GDN_DK)
    ii = lax.broadcasted_iota(jnp.int32, (c, c), 0)
    jj = lax.broadcasted_iota(jnp.int32, (c, c), 1)
    strict, incl = ii > jj, ii >= jj
    bl = bd[rows, h:h + 1]
    dl = bd[rows, GDN_HEADS + h:GDN_HEADS + h + 1]
    beta = _sigmoid(bl)
    ea = jnp.exp(par[0:1, h:h + 1])
    sp_arg = dl + par[1:2, h:h + 1]
    g = -ea * _softplus(sp_arg)
    gb = _dot(incl.astype(F32), jnp.broadcast_to(g, (c, GDN_DK)), HIGHEST)
    gc = gb[:, :c]
    diff = gc - gc.T
    dec_s = jnp.exp(jnp.where(strict, diff, NEG))
    dec_i = jnp.exp(jnp.where(incl, diff, NEG))
    gam = jnp.exp(gb)
    glast = gb[c - 1:c, :]
    e_rest = jnp.exp(glast - gb)
    gl = jnp.exp(glast)
    k = kn[rows, cs]
    q = qn[rows, cs] * (GDN_DK ** -0.5)
    kb, qb = k.astype(BF16), q.astype(BF16)
    kk = _dot_nt(kb, kb)
    p = _dot_nt(qb, kb) * dec_i
    return dict(strict=strict, incl=incl, beta=beta, ea=ea, sp_arg=sp_arg, g=g, gam=gam, e_rest=e_rest, gl=gl,
                dec_s=dec_s, dec_i=dec_i, k=k, q=q, kb=kb, qb=qb, kk=kk, p=p, cs=cs)


def _gdn_fwd(qn, kn, v, proj, par, gnw):
    t = qn.shape[0]
    c = CHUNK
    nc = t // c
    r_ = GDN_CB * c

    def body(qn_ref, kn_ref, v_ref, bd_ref, z_ref, par_ref, gnw_ref,
             oan_ref, o_ref, sp_ref, w_ref, u_ref, tm_ref, s_ref):
        @pl.when(pl.program_id(0) == 0)
        def _():
            s_ref[...] = jnp.zeros_like(s_ref)

        bd, par, gnw_v = bd_ref[...], par_ref[...], gnw_ref[...]
        eye = (lax.broadcasted_iota(jnp.int32, (c, c), 0) == lax.broadcasted_iota(jnp.int32, (c, c), 1)).astype(F32)
        for cc in range(GDN_CB):
            rows = slice(cc * c, (cc + 1) * c)
            for h in range(GDN_HEADS):
                tms = _gdn_chunk_terms(bd, par, kn_ref, qn_ref, h, rows)
                cs = tms["cs"]
                vv = v_ref[rows, cs]
                a_mat = tms["beta"] * tms["kk"] * tms["dec_s"]
                x = -a_mat
                tinv = eye + x
                for _ in range(5):
                    x = _dot(x, x, HIGHEST)
                    tinv = tinv + _dot(tinv, x, HIGHEST)
                wm = _dot(tinv, (tms["beta"] * tms["gam"]) * tms["k"], HIGHEST)
                uv = _dot(tinv, tms["beta"] * vv, HIGHEST)
                sh = s_ref[h]
                sb = sh.astype(BF16)
                sp_ref[cc, h] = sh
                u = uv - _dot(wm.astype(BF16), sb)
                ub = u.astype(BF16)
                o = _dot((tms["q"] * tms["gam"]).astype(BF16), sb) + _dot(tms["p"].astype(BF16), ub)
                s_ref[h] = tms["gl"] * sh + _dot_tn((tms["k"] * tms["e_rest"]).astype(BF16), ub)
                w_ref[rows, cs] = wm
                u_ref[rows, cs] = u
                tm_ref[cc, h] = tinv
                o_ref[rows, cs] = o
                zz = z_ref[rows, cs]
                rr = lax.rsqrt(jnp.mean(o * o, axis=-1, keepdims=True) + EPS)
                oan_ref[rows, cs] = ((o * rr) * gnw_v * (zz * _sigmoid(zz))).astype(BF16)

    row = pl.BlockSpec((r_, KEY_A), lambda i: (i, 0))
    return pl.pallas_call(
        body, name="gdn_fwd", grid=(nc // GDN_CB,),
        in_specs=[row, row, row, pl.BlockSpec((r_, LANES), lambda i: (i, CB_BD)),
                  pl.BlockSpec((r_, PB), lambda i: (i, CB_ZA)),
                  pl.BlockSpec((SUBLANES, LANES), lambda i: (0, 0)), pl.BlockSpec((1, GDN_DK), lambda i: (0, 0))],
        out_specs=[row, row, pl.BlockSpec((GDN_CB, GDN_HEADS, GDN_DK, GDN_DK), lambda i: (i, 0, 0, 0)),
                   row, row, pl.BlockSpec((GDN_CB, GDN_HEADS, c, c), lambda i: (i, 0, 0, 0))],
        out_shape=[jax.ShapeDtypeStruct((t, KEY_A), BF16), jax.ShapeDtypeStruct((t, KEY_A), F32),
                   jax.ShapeDtypeStruct((nc, GDN_HEADS, GDN_DK, GDN_DK), F32),
                   jax.ShapeDtypeStruct((t, KEY_A), F32), jax.ShapeDtypeStruct((t, KEY_A), F32),
                   jax.ShapeDtypeStruct((nc, GDN_HEADS, c, c), F32)],
        scratch_shapes=[pltpu.VMEM((GDN_HEADS, GDN_DK, GDN_DK), F32)],
        compiler_params=_cparams("arbitrary"),
    )(qn, kn, v, proj, proj, par, gnw)


def _gdn_bwd(qn, kn, v, proj, par, gnw, o, sprev, wst, ust, tst, d_oan):
    t = qn.shape[0]
    c = CHUNK
    nc = t // c
    nb = nc // GDN_CB
    r_ = GDN_CB * c

    def body(qn_ref, kn_ref, v_ref, bd_ref, z_ref, par_ref, gnw_ref, o_ref, sp_ref, w_ref, u_ref, tm_ref, do_ref,
             dqn_ref, dkn_ref, dv_ref, dz_ref, dbd_ref, acc_ref, ds_ref):
        @pl.when(pl.program_id(0) == 0)
        def _():
            ds_ref[...] = jnp.zeros_like(ds_ref)
            acc_ref[...] = jnp.zeros_like(acc_ref)

        bd, par, gnw_v = bd_ref[...], par_ref[...], gnw_ref[...]
        lane = lax.broadcasted_iota(jnp.int32, (1, LANES), 1)
        rix = lax.broadcasted_iota(jnp.int32, (c, 1), 0)
        ii = lax.broadcasted_iota(jnp.int32, (c, c), 0)
        jj = lax.broadcasted_iota(jnp.int32, (c, c), 1)
        upper = (jj >= ii).astype(F32)
        acc_a = jnp.zeros((1, LANES), F32)
        acc_d = jnp.zeros((1, LANES), F32)
        acc_g = jnp.zeros((1, LANES), F32)
        for cc in reversed(range(GDN_CB)):
            rows = slice(cc * c, (cc + 1) * c)
            dbd_tile = jnp.zeros((c, LANES), F32)
            for h in range(GDN_HEADS):
                tms = _gdn_chunk_terms(bd, par, kn_ref, qn_ref, h, rows)
                cs = tms["cs"]
                beta, gam, k, q, kb, qb = tms["beta"], tms["gam"], tms["k"], tms["q"], tms["kb"], tms["qb"]
                dec_s, dec_i, kk, p, gl, e_rest = tms["dec_s"], tms["dec_i"], tms["kk"], tms["p"], tms["gl"], tms["e_rest"]
                vv = v_ref[rows, cs]
                sh = sp_ref[cc, h]
                sb = sh.astype(BF16)
                wm, u, tinv = w_ref[rows, cs], u_ref[rows, cs], tm_ref[cc, h]
                wb, ub = wm.astype(BF16), u.astype(BF16)
                ov, zz, dout = o_ref[rows, cs], z_ref[rows, cs], do_ref[rows, cs]
                sg = _sigmoid(zz)
                sil = zz * sg
                rr = lax.rsqrt(jnp.mean(ov * ov, axis=-1, keepdims=True) + EPS)
                on = ov * rr
                dz_ref[rows, cs] = (dout * on * gnw_v * (sg * (1.0 + zz * (1.0 - sg)))).astype(BF16)
                acc_g = acc_g + jnp.sum(dout * on * sil, axis=0, keepdims=True)
                don = dout * gnw_v * sil
                d_o = rr * (don - on * jnp.mean(don * on, axis=-1, keepdims=True))
                dob = d_o.astype(BF16)
                qg = q * gam
                kd = k * e_rest
                dsn = ds_ref[h]
                dsnb = dsn.astype(BF16)
                du = _dot(p.T.astype(BF16), dob) + _dot(kd.astype(BF16), dsnb)
                dub = du.astype(BF16)
                ds_ref[h] = gl * dsn + _dot_tn(qg.astype(BF16), dob) - _dot_tn(wb, dub)
                dqg = _dot_nt(dob, sb)
                dp = _dot_nt(dob, ub)
                dkd = _dot_nt(ub, dsnb)
                dgl = jnp.sum(jnp.sum(dsn * sh, axis=1, keepdims=True), axis=0, keepdims=True)
                dwm = -_dot_nt(dub, sb)
                tt = tinv.T
                dbk = _dot(tt, dwm, HIGHEST)
                dbv = _dot(tt, du, HIGHEST)
                uv = u + _dot(wb, sb)
                d_a = -(_dot_nt(dbk.astype(BF16), wb) + _dot_nt(dbv.astype(BF16), uv.astype(BF16)))
                d_a = jnp.where(tms["strict"], d_a, 0.0)
                dkk = d_a * beta * dec_s
                dbeta = jnp.sum(d_a * kk * dec_s, axis=-1, keepdims=True)
                xa = dkk * kk
                bk = (beta * gam) * k
                dbeta = dbeta + jnp.sum(dbk * k * gam, axis=-1, keepdims=True) + jnp.sum(dbv * vv, axis=-1, keepdims=True)
                dk = dbk * (beta * gam)
                dv_ref[rows, cs] = dbv * beta
                dqk = dp * dec_i
                xp = dp * p
                dq = _dot(dqk.astype(BF16), kb) + dqg * gam
                dk = dk + _dot_tn(dqk.astype(BF16), qb) + _dot((dkk + dkk.T).astype(BF16), kb) + dkd * e_rest
                zc = jnp.sum(dkd * kd, axis=-1, keepdims=True)
                xs = xa + xp
                dgc = (jnp.sum(xs, axis=-1, keepdims=True) - jnp.sum(xs.T, axis=-1, keepdims=True)
                       + jnp.sum(dbk * bk, axis=-1, keepdims=True) + jnp.sum(dqg * qg, axis=-1, keepdims=True) - zc)
                dglast = jnp.sum(zc, axis=0, keepdims=True) + dgl * gl[:, 0:1]
                dgc = dgc + jnp.where(rix == c - 1, dglast, 0.0)
                dg = _dot(upper, jnp.broadcast_to(dgc, (c, GDN_DK)), HIGHEST)[:, 0:1]
                ddl = dg * (-tms["ea"]) * _sigmoid(tms["sp_arg"])
                dbl = dbeta * beta * (1.0 - beta)
                acc_a = acc_a + jnp.where(lane == h, jnp.sum(dg * tms["g"], axis=0, keepdims=True), 0.0)
                acc_d = acc_d + jnp.where(lane == h, jnp.sum(ddl, axis=0, keepdims=True), 0.0)
                dbd_tile = dbd_tile + jnp.where(lane == h, dbl, 0.0) + jnp.where(lane == GDN_HEADS + h, ddl, 0.0)
                dqn_ref[rows, cs] = dq * (GDN_DK ** -0.5)
                dkn_ref[rows, cs] = dk
            dbd_ref[rows, :] = dbd_tile
        acc_ref[0:1, :] += acc_a
        acc_ref[1:2, :] += acc_d
        acc_ref[2:3, :] += acc_g

    def rev(i):
        return nb - 1 - i

    row = pl.BlockSpec((r_, KEY_A), lambda i: (rev(i), 0))
    st = pl.BlockSpec((GDN_CB, GDN_HEADS, GDN_DK, GDN_DK), lambda i: (rev(i), 0, 0, 0))
    tt_spec = pl.BlockSpec((GDN_CB, GDN_HEADS, c, c), lambda i: (rev(i), 0, 0, 0))
    return pl.pallas_call(
        body, name="gdn_bwd", grid=(nb,),
        in_specs=[row, row, row, pl.BlockSpec((r_, LANES), lambda i: (rev(i), CB_BD)),
                  pl.BlockSpec((r_, PB), lambda i: (rev(i), CB_ZA)),
                  pl.BlockSpec((SUBLANES, LANES), lambda i: (0, 0)), pl.BlockSpec((1, GDN_DK), lambda i: (0, 0)),
                  row, st, row, row, tt_spec, row],
        out_specs=[row, row, row, row, pl.BlockSpec((r_, LANES), lambda i: (rev(i), 0)),
                   pl.BlockSpec((SUBLANES, LANES), lambda i: (0, 0))],
        out_shape=[jax.ShapeDtypeStruct((t, KEY_A), F32)] * 3 + [jax.ShapeDtypeStruct((t, KEY_A), BF16),
                   jax.ShapeDtypeStruct((t, LANES), F32), jax.ShapeDtypeStruct((SUBLANES, LANES), F32)],
        scratch_shapes=[pltpu.VMEM((GDN_HEADS, GDN_DK, GDN_DK), F32)],
        compiler_params=_cparams("arbitrary"),
    )(qn, kn, v, proj, proj, par, gnw, o, sprev, wst, ust, tst, d_oan)


def _merge_fwd(oan, ob, proj, x, wba, wbb, wout, tm=512):
    t = x.shape[0]

    def body(oa_ref, ob_ref, ga_ref, gb_ref, x_ref, wba_ref, wbb_ref, wout_ref, x2_ref):
        ya = _dot(oa_ref[...], wba_ref[...])
        yb = _dot(ob_ref[...], wbb_ref[...])
        mix = _sigmoid(ga_ref[...]) * ya + _sigmoid(gb_ref[...]) * yb
        x2_ref[...] = x_ref[...] + _dot(mix.astype(BF16), wout_ref[...])

    half = pl.BlockSpec((tm, KEY_A), lambda i: (i, 0))
    row = pl.BlockSpec((tm, D_MODEL), lambda i: (i, 0))
    wsmall = pl.BlockSpec((KEY_A, D_MODEL), lambda i: (0, 0))
    return pl.pallas_call(
        body, name="merge_fwd", grid=(t // tm,),
        in_specs=[half, half, pl.BlockSpec((tm, D_MODEL), lambda i: (i, CB_GA)),
                  pl.BlockSpec((tm, D_MODEL), lambda i: (i, CB_GB)), row, wsmall, wsmall,
                  pl.BlockSpec((D_MODEL, D_MODEL), lambda i: (0, 0))],
        out_specs=row,
        out_shape=jax.ShapeDtypeStruct((t, D_MODEL), F32),
        compiler_params=_cparams("parallel"),
    )(oan, ob, proj, proj, x, wba, wbb, wout)


def _merge_bwd(dx2b, oan, ob, proj, wba, wbb, wout_t, wba_t, wbb_t, tm=512):
    t = dx2b.shape[0]

    def body(dx_ref, oa_ref, ob_ref, ga_ref, gb_ref, wba_ref, wbb_ref, woutt_ref, wbat_ref, wbbt_ref,
             dga_ref, dgb_ref, doa_ref, dob_ref, mix_ref, dya_ref, dyb_ref):
        dmix = _dot(dx_ref[...], woutt_ref[...])
        ya = _dot(oa_ref[...], wba_ref[...])
        yb = _dot(ob_ref[...], wbb_ref[...])
        sa, sb = _sigmoid(ga_ref[...]), _sigmoid(gb_ref[...])
        mix_ref[...] = (sa * ya + sb * yb).astype(BF16)
        dga_ref[...] = (dmix * ya * sa * (1.0 - sa)).astype(BF16)
        dgb_ref[...] = (dmix * yb * sb * (1.0 - sb)).astype(BF16)
        dya = (dmix * sa).astype(BF16)
        dyb = (dmix * sb).astype(BF16)
        dya_ref[...] = dya
        dyb_ref[...] = dyb
        doa_ref[...] = _dot(dya, wbat_ref[...])
        dob_ref[...] = _dot(dyb, wbbt_ref[...])

    half = pl.BlockSpec((tm, KEY_A), lambda i: (i, 0))
    row = pl.BlockSpec((tm, D_MODEL), lambda i: (i, 0))
    wsmall = pl.BlockSpec((KEY_A, D_MODEL), lambda i: (0, 0))
    wsmall_t = pl.BlockSpec((D_MODEL, KEY_A), lambda i: (0, 0))
    big = jax.ShapeDtypeStruct((t, D_MODEL), BF16)
    return pl.pallas_call(
        body, name="merge_bwd", grid=(t // tm,),
        in_specs=[row, half, half, pl.BlockSpec((tm, D_MODEL), lambda i: (i, CB_GA)),
                  pl.BlockSpec((tm, D_MODEL), lambda i: (i, CB_GB)), wsmall, wsmall,
                  pl.BlockSpec((D_MODEL, D_MODEL), lambda i: (0, 0)), wsmall_t, wsmall_t],
        out_specs=[row, row, half, half, row, row, row],
        out_shape=[big, big, jax.ShapeDtypeStruct((t, KEY_A), F32), jax.ShapeDtypeStruct((t, KEY_A), F32), big, big, big],
        compiler_params=_cparams("parallel"),
    )(dx2b, oan, ob, proj, proj, wba, wbb, wout_t, wba_t, wbb_t)


FFN_TM = 128
FFN_W = 2 * D_FF


def _ffn_conv(up_ref, halo_ref, cw_ref, cb_ref, first):
    ext = jnp.concatenate([jnp.where(first, 0.0, halo_ref[...]), up_ref[...]], axis=0)
    taps = _conv_taps(ext, FFN_CONV, HALO, FFN_TM)
    cw = cw_ref[...]
    u = sum(cw[k:k + 1, :] * taps[k] for k in range(FFN_CONV)) + cb_ref[...]
    return u, taps


def _ffn_tail(up, cw, cb, wdown, x2, tgt, w3):
    t = x2.shape[0]
    tm = FFN_TM

    def body(up_ref, halo_ref, cw_ref, cb_ref, wd_ref, x2_ref, tgt_ref, w3_ref, dx_ref, dxb_ref, act_ref, acc_ref):
        first = pl.program_id(0) == 0

        @pl.when(first)
        def _():
            acc_ref[...] = jnp.zeros_like(acc_ref)

        u, _ = _ffn_conv(up_ref, halo_ref, cw_ref, cb_ref, first)
        gate, upp = u[:, :D_FF], u[:, D_FF:]
        act = (gate * _sigmoid(gate) * upp).astype(BF16)
        act_ref[...] = act
        x3 = x2_ref[...] + _dot(act, wd_ref[...])
        r = lax.rsqrt(jnp.mean(x3 * x3, axis=-1, keepdims=True) + EPS)
        xh = x3 * r
        w3v = w3_ref[...]
        err = xh * w3v - tgt_ref[...]
        loss = 0.5 * jnp.sum(jnp.mean(err * err, axis=-1, keepdims=True), axis=0, keepdims=True)
        dy = err * (1.0 / D_MODEL)
        acc_ref[0:1, :] += jnp.sum(dy * xh, axis=0, keepdims=True)
        acc_ref[1:2, :] += jnp.broadcast_to(loss, (1, D_MODEL))
        dxh = dy * w3v
        dx = r * (dxh - xh * jnp.mean(dxh * xh, axis=-1, keepdims=True))
        dx_ref[...] = dx
        dxb_ref[...] = dx.astype(BF16)

    row = pl.BlockSpec((tm, D_MODEL), lambda i: (i, 0))
    vec = pl.BlockSpec((1, D_MODEL), lambda i: (0, 0))
    return pl.pallas_call(
        body, name="ffn_tail", grid=(t // tm,),
        in_specs=[pl.BlockSpec((tm, FFN_W), lambda i: (i, 0)), _prev_halo_spec(tm, FFN_W, 0),
                  pl.BlockSpec((SUBLANES, FFN_W), lambda i: (0, 0)), pl.BlockSpec((1, FFN_W), lambda i: (0, 0)),
                  pl.BlockSpec((D_FF, D_MODEL), lambda i: (0, 0)), row, row, vec],
        out_specs=[row, row, pl.BlockSpec((tm, D_FF), lambda i: (i, 0)), pl.BlockSpec((SUBLANES, D_MODEL), lambda i: (0, 0))],
        out_shape=[jax.ShapeDtypeStruct((t, D_MODEL), F32), jax.ShapeDtypeStruct((t, D_MODEL), BF16),
                   jax.ShapeDtypeStruct((t, D_FF), BF16), jax.ShapeDtypeStruct((SUBLANES, D_MODEL), F32)],
        compiler_params=_cparams("arbitrary"),
    )(up, up, cw, cb, wdown, x2, tgt, w3)


def _ffn_bwd_act(dx3b, wdown_t, up, cw, cb):
    t = up.shape[0]
    tm = FFN_TM

    def body(dx_ref, wdt_ref, up_ref, halo_ref, cw_ref, cb_ref, du_ref, acc_ref):
        first = pl.program_id(0) == 0

        @pl.when(first)
        def _():
            acc_ref[...] = jnp.zeros_like(acc_ref)

        dact = _dot(dx_ref[...], wdt_ref[...])
        u, taps = _ffn_conv(up_ref, halo_ref, cw_ref, cb_ref, first)
        gate, upp = u[:, :D_FF], u[:, D_FF:]
        sg = _sigmoid(gate)
        du = jnp.concatenate([dact * upp * (sg * (1.0 + gate * (1.0 - sg))), dact * (gate * sg)], axis=1)
        du_ref[...] = du
        for k in range(FFN_CONV):
            acc_ref[k:k + 1, :] += jnp.sum(du * taps[k], axis=0, keepdims=True)
        acc_ref[FFN_CONV:FFN_CONV + 1, :] += jnp.sum(du, axis=0, keepdims=True)

    return pl.pallas_call(
        body, name="ffn_bwd_act", grid=(t // tm,),
        in_specs=[pl.BlockSpec((tm, D_MODEL), lambda i: (i, 0)), pl.BlockSpec((D_MODEL, D_FF), lambda i: (0, 0)),
                  pl.BlockSpec((tm, FFN_W), lambda i: (i, 0)), _prev_halo_spec(tm, FFN_W, 0),
                  pl.BlockSpec((SUBLANES, FFN_W), lambda i: (0, 0)), pl.BlockSpec((1, FFN_W), lambda i: (0, 0))],
        out_specs=[pl.BlockSpec((tm, FFN_W), lambda i: (i, 0)), pl.BlockSpec((SUBLANES, FFN_W), lambda i: (0, 0))],
        out_shape=[jax.ShapeDtypeStruct((t, FFN_W), F32), jax.ShapeDtypeStruct((SUBLANES, FFN_W), F32)],
        compiler_params=_cparams("arbitrary"),
    )(dx3b, wdown_t, up, up, cw, cb)


def _ffn_bwd_conv(du, cw):
    t = du.shape[0]
    tm = FFN_TM
    nt = t // tm
    n_ext = tm + HALO

    def body(du_ref, nxt_ref, cw_ref, o_ref):
        last = pl.program_id(0) == nt - 1
        ext = jnp.concatenate([du_ref[...], jnp.where(last, 0.0, nxt_ref[...])], axis=0)
        cw = cw_ref[...]
        acc = cw[FFN_CONV - 1:FFN_CONV, :] * ext[:tm]
        for k in range(FFN_CONV - 1):
            acc = acc + cw[k:k + 1, :] * pltpu.roll(ext, n_ext - (FFN_CONV - 1 - k), 0)[:tm]
        o_ref[...] = acc.astype(BF16)

    return pl.pallas_call(
        body, name="ffn_bwd_conv", grid=(nt,),
        in_specs=[pl.BlockSpec((tm, FFN_W), lambda i: (i, 0)), _next_halo_spec(tm, FFN_W, 0, t),
                  pl.BlockSpec((SUBLANES, FFN_W), lambda i: (0, 0))],
        out_specs=pl.BlockSpec((tm, FFN_W), lambda i: (i, 0)),
        out_shape=jax.ShapeDtypeStruct((t, FFN_W), BF16),
        compiler_params=_cparams("parallel"),
    )(du, du, cw)


def _adamw(parts, w, m, v, name, tr):
    r = w.shape[0]

    def body(p_ref, w_ref, m_ref, v_ref, g_ref, d_ref, mo_ref, vo_ref):
        g = p_ref[0]
        for s in range(1, N_DEV):
            g = g + p_ref[s]
        mm = ADAM_B1 * m_ref[...] + (1.0 - ADAM_B1) * g
        vv = ADAM_B2 * v_ref[...] + (1.0 - ADAM_B2) * (g * g)
        m_hat = mm / (1.0 - ADAM_B1 ** ADAM_STEP)
        v_hat = vv / (1.0 - ADAM_B2 ** ADAM_STEP)
        g_ref[...] = g
        d_ref[...] = -ADAM_LR * (m_hat / (jnp.sqrt(v_hat) + ADAM_EPS) + ADAM_WD * w_ref[...])
        mo_ref[...] = mm
        vo_ref[...] = vv

    row = pl.BlockSpec((tr, LANES), lambda i: (i, 0))
    return pl.pallas_call(
        body, name=name, grid=(r // tr,),
        in_specs=[pl.BlockSpec((N_DEV, tr, LANES), lambda i: (0, i, 0)), row, row, row],
        out_specs=[row, row, row, row],
        out_shape=[jax.ShapeDtypeStruct((r, LANES), F32)] * 4,
        compiler_params=_cparams("parallel"),
    )(parts, w, m, v)


def _mesh_pos():
    return lax.axis_index("x"), lax.axis_index("y"), lax.axis_index("c")


def _peer(pos, k):
    x, y, c = pos
    return (x ^ ((k >> 2) & 1), y ^ ((k >> 1) & 1), c ^ (k & 1))


def _flat_id(pos):
    return 4 * pos[0] + 2 * pos[1] + pos[2]


def _all_gather(wb, ws):
    any_spec = pl.BlockSpec(memory_space=pl.ANY)

    def body(wb_ref, ws_ref, gb_ref, gs_ref, send_sems, recv_sems, loc_sems):
        pos = _mesh_pos()
        me = _flat_id(pos)
        local = [pltpu.make_async_copy(wb_ref, gb_ref.at[me], loc_sems.at[0]),
                 pltpu.make_async_copy(ws_ref, gs_ref.at[me], loc_sems.at[1])]
        remote = []
        for k in range(1, N_DEV):
            to = _peer(pos, k)
            for j, (src, dst) in enumerate(((wb_ref, gb_ref), (ws_ref, gs_ref))):
                remote.append(pltpu.make_async_remote_copy(
                    src_ref=src, dst_ref=dst.at[me], send_sem=send_sems.at[j, k - 1], recv_sem=recv_sems.at[j, k - 1],
                    device_id=to, device_id_type=pl.DeviceIdType.MESH))
        for cp in local + remote:
            cp.start()
        for cp in remote:
            cp.wait()
        for cp in local:
            cp.wait()

    return pl.pallas_call(
        body, name="all_gather_weights",
        in_specs=[any_spec, any_spec], out_specs=[any_spec, any_spec],
        out_shape=[jax.ShapeDtypeStruct((N_DEV,) + wb.shape, wb.dtype), jax.ShapeDtypeStruct((N_DEV,) + ws.shape, ws.dtype)],
        scratch_shapes=[pltpu.SemaphoreType.DMA((2, N_DEV - 1)), pltpu.SemaphoreType.DMA((2, N_DEV - 1)),
                        pltpu.SemaphoreType.DMA((2,))],
    )(wb, ws)


def _exchange_grads(big, small):
    any_spec = pl.BlockSpec(memory_space=pl.ANY)

    def body(big_ref, small_ref, rb_ref, rs_ref, send_sems, recv_sems, loc_sems):
        pos = _mesh_pos()
        me = _flat_id(pos)
        local = [pltpu.make_async_copy(big_ref.at[me], rb_ref.at[me], loc_sems.at[0]),
                 pltpu.make_async_copy(small_ref, rs_ref.at[me], loc_sems.at[1])]
        remote = []
        for k in range(1, N_DEV):
            to = _peer(pos, k)
            remote.append(pltpu.make_async_remote_copy(
                src_ref=big_ref.at[_flat_id(to)], dst_ref=rb_ref.at[me], send_sem=send_sems.at[0, k - 1],
                recv_sem=recv_sems.at[0, k - 1], device_id=to, device_id_type=pl.DeviceIdType.MESH))
            remote.append(pltpu.make_async_remote_copy(
                src_ref=small_ref, dst_ref=rs_ref.at[me], send_sem=send_sems.at[1, k - 1],
                recv_sem=recv_sems.at[1, k - 1], device_id=to, device_id_type=pl.DeviceIdType.MESH))
        for cp in local + remote:
            cp.start()
        for cp in remote:
            cp.wait()
        for cp in local:
            cp.wait()

    return pl.pallas_call(
        body, name="exchange_grads",
        in_specs=[any_spec, any_spec], out_specs=[any_spec, any_spec],
        out_shape=[jax.ShapeDtypeStruct(big.shape, big.dtype), jax.ShapeDtypeStruct((N_DEV,) + small.shape, small.dtype)],
        scratch_shapes=[pltpu.SemaphoreType.DMA((2, N_DEV - 1)), pltpu.SemaphoreType.DMA((2, N_DEV - 1)),
                        pltpu.SemaphoreType.DMA((2,))],
    )(big, small)


def _pad_rows(a, rows):
    return jnp.pad(a, ((0, rows - a.shape[0]),) + ((0, 0),) * (a.ndim - 1))


PACK_UNIT = SUBLANES * LANES


def _pack_lanes(parts, rows):
    out = []
    for a in parts:
        f = a.reshape(-1)
        out.append(jnp.pad(f, (0, (-f.shape[0]) % PACK_UNIT)).reshape(-1, LANES))
    packed = jnp.concatenate(out, axis=0)
    assert packed.shape[0] == rows, (packed.shape, rows)
    return packed


def _unpack_lanes(buf, shapes):
    out, r0 = [], 0
    for shp in shapes:
        n = math.prod(shp)
        nr = -(-n // PACK_UNIT) * SUBLANES
        out.append(buf[r0:r0 + nr].reshape(-1)[:n].reshape(shp))
        r0 += nr
    return out


def _col_shards(g):
    r, n = g.shape
    return g.reshape(r, N_DEV, n // N_DEV).transpose(1, 0, 2).reshape(N_DEV, -1)


def _row_shards(g):
    return g.reshape(N_DEV, -1)


def _lane_rows(flat):
    n = flat.shape[1]
    return jnp.pad(flat, ((0, 0), (0, (-n) % PACK_UNIT))).reshape(N_DEV, -1, LANES)


BIG_ROWS = 16168
BIG_TILE = 376
SMALL_ROWS = 128
WB_SHAPE = (2032, 1024)
WS_ROWS = 32


def kernel(x, norm_mix_w, w_in, conv_qkv_w, a_log, dt_bias, gdn_norm_w, w_branch_a, w_branch_b, rel_bias, w_out, norm_ffn_w, w_up, conv_ffn_w, conv_ffn_b, w_down, norm_final_w, loss_target, m_norm_mix_w, m_w_in, m_conv_qkv_w, m_a_log, m_dt_bias, m_gdn_norm_w, m_w_branch_a, m_w_branch_b, m_rel_bias, m_w_out, m_norm_ffn_w, m_w_up, m_conv_ffn_w, m_conv_ffn_b, m_w_down, m_norm_final_w, v_norm_mix_w, v_w_in, v_conv_qkv_w, v_a_log, v_dt_bias, v_gdn_norm_w, v_w_branch_a, v_w_branch_b, v_rel_bias, v_w_out, v_norm_ffn_w, v_w_up, v_conv_ffn_w, v_conv_ffn_b, v_w_down, v_norm_final_w):
    big_w = (w_in, w_branch_a, w_branch_b, w_out, w_up, w_down, conv_qkv_w, conv_ffn_w)
    big_m = (m_w_in, m_w_branch_a, m_w_branch_b, m_w_out, m_w_up, m_w_down, m_conv_qkv_w, m_conv_ffn_w)
    big_v = (v_w_in, v_w_branch_a, v_w_branch_b, v_w_out, v_w_up, v_w_down, v_conv_qkv_w, v_conv_ffn_w)
    small_w = (norm_mix_w, a_log, dt_bias, gdn_norm_w, rel_bias, norm_ffn_w, conv_ffn_b, norm_final_w)
    small_m = (m_norm_mix_w, m_a_log, m_dt_bias, m_gdn_norm_w, m_rel_bias, m_norm_ffn_w, m_conv_ffn_b, m_norm_final_w)
    small_v = (v_norm_mix_w, v_a_log, v_dt_bias, v_gdn_norm_w, v_rel_bias, v_norm_ffn_w, v_conv_ffn_b, v_norm_final_w)

    wb = jnp.concatenate([w.reshape(-1).astype(BF16) for w in big_w[:6]])
    wb = jnp.pad(wb, (0, WB_SHAPE[0] * WB_SHAPE[1] - wb.shape[0])).reshape(WB_SHAPE)
    ws = _pack_lanes(big_w[6:], WS_ROWS)
    gb, gs = _all_gather(wb, ws)
    gb = gb.reshape(N_DEV, -1)
    sizes = [math.prod(w.shape) for w in big_w[:6]]
    offs = np.cumsum([0] + sizes)
    sh = [gb[:, offs[i]:offs[i + 1]] for i in range(6)]
    win = sh[0].reshape(N_DEV, D_MODEL, D_IN // N_DEV).transpose(1, 0, 2).reshape(D_MODEL, D_IN)
    wba = sh[1].reshape(N_DEV, KEY_A, D_MODEL // N_DEV).transpose(1, 0, 2).reshape(KEY_A, D_MODEL)
    wbb = sh[2].reshape(N_DEV, WIDTH_B, D_MODEL // N_DEV).transpose(1, 0, 2).reshape(WIDTH_B, D_MODEL)
    wout = sh[3].reshape(D_MODEL, D_MODEL)
    wup = sh[4].reshape(N_DEV, D_MODEL, FFN_W // N_DEV).transpose(1, 0, 2).reshape(D_MODEL, FFN_W)
    wdown = sh[5].reshape(D_FF, D_MODEL)
    gs = gs.reshape(N_DEV, -1)
    cqkv = gs[:, :GDN_CONV * 192].reshape(N_DEV, GDN_CONV, 192).transpose(1, 0, 2).reshape(GDN_CONV, 3 * KEY_A)
    cffn = gs[:, PACK_UNIT:PACK_UNIT + FFN_CONV * 704].reshape(N_DEV, FFN_CONV, 704).transpose(1, 0, 2).reshape(FFN_CONV, FFN_W)
    cffn = _pad_rows(cffn, SUBLANES)
    grads, small_g, grad_x = _local_step(x[0], loss_target[0], win, wba, wbb, wout, wup, wdown, cqkv, cffn, norm_mix_w,
                                         a_log, dt_bias, gdn_norm_w, rel_bias, norm_ffn_w, conv_ffn_b, norm_final_w)
    return _reduce_update(grads, small_g, grad_x, big_w, big_m, big_v, small_w, small_m, small_v)


def _local_step(xs, tgt, win, wba, wbb, wout, wup, wdown, cqkv, cffn, norm_mix_w, a_log, dt_bias, gdn_norm_w, rel_bias,
                norm_ffn_w, conv_ffn_b, norm_final_w):
    w_all = jnp.concatenate([win[:, 3592:5640], win[:, 0:2048], win[:, 2056:3592], win[:, 2048:2056],
                             jnp.zeros((D_MODEL, PROJ_W - D_IN), BF16)], axis=1)
    par = _pad_rows(jnp.pad(jnp.concatenate([a_log, dt_bias], axis=0), ((0, 0), (0, LANES - GDN_HEADS))), SUBLANES)
    table = jnp.pad(rel_bias[0], ((0, 0), (0, 3 * LANES - rel_bias.shape[-1]))).reshape(ATT_HEADS, 1, 3 * LANES)

    h1 = _rmsnorm_cast(xs, norm_mix_w, "norm_mix")
    proj = _mm_nn(h1, w_all, F32, "in_proj", 512, 1152, D_MODEL)
    qn, kn, va = _gdn_prep_fwd(proj, cqkv)
    oan, o_gdn, sprev, wst, ust, tst = _gdn_fwd(qn, kn, va, proj, par, gdn_norm_w)
    bias_q, bias_k = _att_bias(table)
    ob, lse, lse_t = _att_fwd(proj, bias_q)
    x2 = _merge_fwd(oan, ob, proj, xs, wba, wbb, wout)
    h2 = _rmsnorm_cast(x2, norm_ffn_w, "norm_ffn")
    up = _mm_nn(h2, wup, F32, "ffn_up", 512, 1408, D_MODEL)
    dx3, dx3b, act, tail_sums = _ffn_tail(up, cffn, conv_ffn_b, wdown, x2, tgt, norm_final_w.reshape(1, D_MODEL))

    g_wdown = _mm_tn(act, dx3b, "dw_down", 512)
    du, ffn_sums = _ffn_bwd_act(dx3b, wdown.T, up, cffn, conv_ffn_b)
    dup = _ffn_bwd_conv(du, cffn)
    g_wup = _mm_tn(h2, dup, "dw_up", 1408)
    dh2 = _mm_nn(dup, wup.T, F32, "ffn_up_bwd", 512, D_MODEL, 1408)
    dx2, dx2b, nffn_sums = _rms_bwd(dh2, x2, norm_ffn_w, dx3, "norm_ffn_bwd")
    dga, dgb, d_oan, d_ob, mixb, dya, dyb = _merge_bwd(dx2b, oan, ob, proj, wba, wbb, wout.T, wba.T, wbb.T)
    g_wout = _mm_tn(mixb, dx2b, "dw_out", 512)
    g_wba = _mm_tn(oan, dya, "dw_branch_a", 512)
    g_wbb = _mm_tn(ob, dyb, "dw_branch_b", 512)
    dqb, dlt_t, slabs = _att_dq(proj, bias_q, lse, d_ob)
    dkb, dvb = _att_dkv(proj, bias_k, lse_t, dlt_t, d_ob)
    g_rel = _relbias_grad(slabs)[:, 0, :rel_bias.shape[-1]]
    dqn, dkn, dva, dz, dbd, gdn_sums = _gdn_bwd(qn, kn, va, proj, par, gdn_norm_w, o_gdn, sprev, wst, ust, tst, d_oan)
    dqa, dka, dvaa, cq_sums = _gdn_prep_bwd(proj, cqkv, dqn, dkn, dva)
    dproj = jnp.concatenate([dga, dgb, dqa, dka, dvaa, dz, dqb, dkb, dvb, dbd.astype(BF16)], axis=1)
    g_wall = _mm_tn(h1, dproj, "dw_in", 1152)
    dh1 = _mm_nn(dproj, w_all.T, F32, "in_proj_bwd", 512, D_MODEL, 1152)
    grad_x, _, nmix_sums = _rms_bwd(dh1, xs, norm_mix_w, dx2, "norm_mix_bwd")

    g_win = jnp.concatenate([g_wall[:, 2048:4096], g_wall[:, 5632:5640], g_wall[:, 4096:5632], g_wall[:, 0:2048]], axis=1)
    small_g = (nmix_sums[0:1], gdn_sums[0:1, :GDN_HEADS], gdn_sums[1:2, :GDN_HEADS], gdn_sums[2:3], g_rel,
               nffn_sums[0:1], ffn_sums[FFN_CONV:FFN_CONV + 1], tail_sums[0:1], tail_sums[1:2, 0:1])
    return (g_win, g_wba, g_wbb, g_wout, g_wup, g_wdown, cq_sums[:GDN_CONV], ffn_sums[:FFN_CONV]), small_g, grad_x


def _reduce_update(grads, small_g, grad_x, big_w, big_m, big_v, small_w, small_m, small_v):
    g_win, g_wba, g_wbb, g_wout, g_wup, g_wdown, g_cqkv, g_cffn = grads
    send_big = jnp.concatenate(
        [_lane_rows(_col_shards(g_win)), _lane_rows(_col_shards(g_wba)), _lane_rows(_col_shards(g_wbb)),
         _lane_rows(_row_shards(g_wout)), _lane_rows(_col_shards(g_wup)), _lane_rows(_row_shards(g_wdown)),
         _lane_rows(_col_shards(g_cqkv)), _lane_rows(_col_shards(g_cffn))], axis=1)
    assert send_big.shape[1] == BIG_ROWS, send_big.shape
    send_small = _pack_lanes(small_g, SMALL_ROWS)
    recv_big, recv_small = _exchange_grads(send_big, send_small)

    big_shapes = [w.shape for w in big_w]
    gB, dB, mB, vB = _adamw(recv_big, _pack_lanes(big_w, BIG_ROWS), _pack_lanes(big_m, BIG_ROWS),
                            _pack_lanes(big_v, BIG_ROWS), "adamw_sharded", BIG_TILE)
    small_shapes = [w.shape for w in small_w]
    zero = jnp.zeros((1,), F32)
    gS, dS, mS, vS = _adamw(recv_small, _pack_lanes(small_w + (zero,), SMALL_ROWS), _pack_lanes(small_m + (zero,), SMALL_ROWS),
                            _pack_lanes(small_v + (zero,), SMALL_ROWS), "adamw_replicated", SMALL_ROWS)
    loss = _unpack_lanes(gS, small_shapes + [()])[-1]

    names = ("norm_mix_w", "w_in", "conv_qkv_w", "a_log", "dt_bias", "gdn_norm_w", "w_branch_a", "w_branch_b", "rel_bias",
             "w_out", "norm_ffn_w", "w_up", "conv_ffn_w", "conv_ffn_b", "w_down", "norm_final_w")
    big_names = ("w_in", "w_branch_a", "w_branch_b", "w_out", "w_up", "w_down", "conv_qkv_w", "conv_ffn_w")
    small_names = ("norm_mix_w", "a_log", "dt_bias", "gdn_norm_w", "rel_bias", "norm_ffn_w", "conv_ffn_b", "norm_final_w")
    outs = []
    for bufB, bufS in ((gB, gS), (dB, dS), (mB, mS), (vB, vS)):
        vals = dict(zip(big_names, _unpack_lanes(bufB, big_shapes)))
        vals.update(zip(small_names, _unpack_lanes(bufS, small_shapes)))
        outs.extend(vals[n] for n in names)
    return (loss, grad_x[None], *outs)
```

```python
import functools
import math

import numpy as np
import jax
import jax.numpy as jnp
from jax import lax
from jax.experimental import pallas as pl
from jax.experimental.pallas import tpu as pltpu

F32, BF16 = jnp.float32, jnp.bfloat16
HIGHEST = lax.Precision.HIGHEST

N_DEV = 8
D_MODEL = 1024
CHUNK = 64
EPS = 1e-6
GDN_HEADS, GDN_DK = 4, 128
KEY_A = GDN_HEADS * GDN_DK
GDN_CONV = 4
ATT_HEADS, ATT_DH = 8, 64
WIDTH_B = ATT_HEADS * ATT_DH
ATT_BAND = 9
REL_CLIP = 128
D_FF = 2816
FFN_CONV = 3
D_IN = 5640
ADAM_LR, ADAM_B1, ADAM_B2, ADAM_EPS, ADAM_WD, ADAM_STEP = 0.001, 0.9, 0.999, 1e-08, 0.01, 10

LANES = 128
SUBLANES = 8
NEG = -1e30

PROJ_W = 5760
PB = 512
CB_GA, CB_GB = 0, 1
CB_QA, CB_KA, CB_VA, CB_ZA, CB_QB, CB_KB, CB_VB = 4, 5, 6, 7, 8, 9, 10
CB_BD = 44

ATT_QB = 256
ATT_KW = 768
ATT_VEC = 1024


def _dot(a, b, precision=None):
    return jnp.dot(a, b, preferred_element_type=F32, precision=precision)


def _dot_nt(a, b, precision=None):
    return lax.dot_general(a, b, (((1,), (1,)), ((), ())), preferred_element_type=F32, precision=precision)


def _dot_tn(a, b):
    return lax.dot_general(a, b, (((0,), (0,)), ((), ())), preferred_element_type=F32)


def _sigmoid(x):
    return 1.0 / (1.0 + jnp.exp(-x))


def _softplus(x):
    return jnp.maximum(x, 0.0) + jnp.log(1.0 + jnp.exp(-jnp.abs(x)))


def _cparams(*sem):
    return pltpu.CompilerParams(dimension_semantics=tuple(sem))


def _rmsnorm_cast(x, w, name, tm=512):
    t, d = x.shape

    def body(x_ref, w_ref, o_ref):
        xv = x_ref[...]
        r = lax.rsqrt(jnp.mean(xv * xv, axis=-1, keepdims=True) + EPS)
        o_ref[...] = (xv * r * w_ref[...]).astype(BF16)

    return pl.pallas_call(
        body, name=name, grid=(t // tm,),
        in_specs=[pl.BlockSpec((tm, d), lambda i: (i, 0)), pl.BlockSpec((1, d), lambda i: (0, 0))],
        out_specs=pl.BlockSpec((tm, d), lambda i: (i, 0)),
        out_shape=jax.ShapeDtypeStruct((t, d), BF16),
        compiler_params=_cparams("parallel"),
    )(x, w)


def _mm_nn(a, b, out_dtype, name, tm, tn, tk):
    m, k = a.shape
    _, n = b.shape
    nk = k // tk
    assert m % tm == 0 and n % tn == 0 and k % tk == 0

    if nk == 1:
        def body(a_ref, b_ref, o_ref):
            o_ref[...] = _dot(a_ref[...], b_ref[...]).astype(out_dtype)
        scratch = []
    else:
        def body(a_ref, b_ref, o_ref, acc_ref):
            kk = pl.program_id(2)

            @pl.when(kk == 0)
            def _():
                acc_ref[...] = jnp.zeros_like(acc_ref)

            acc_ref[...] += _dot(a_ref[...], b_ref[...])

            @pl.when(kk == nk - 1)
            def _():
                o_ref[...] = acc_ref[...].astype(out_dtype)
        scratch = [pltpu.VMEM((tm, tn), F32)]

    return pl.pallas_call(
        body, name=name, grid=(m // tm, n // tn, nk),
        in_specs=[pl.BlockSpec((tm, tk), lambda i, j, kk: (i, kk)),
                  pl.BlockSpec((tk, tn), lambda i, j, kk: (kk, j))],
        out_specs=pl.BlockSpec((tm, tn), lambda i, j, kk: (i, j)),
        out_shape=jax.ShapeDtypeStruct((m, n), out_dtype),
        scratch_shapes=scratch,
        compiler_params=_cparams("parallel", "parallel", "arbitrary"),
    )(a, b)


def _mm_tn(a, b, name, tn, tk=512):
    t, m = a.shape
    _, n = b.shape
    assert t % tk == 0 and n % tn == 0

    def body(a_ref, b_ref, o_ref):
        @pl.when(pl.program_id(1) == 0)
        def _():
            o_ref[...] = jnp.zeros_like(o_ref)

        o_ref[...] += _dot_tn(a_ref[...], b_ref[...])

    return pl.pallas_call(
        body, name=name, grid=(n // tn, t // tk),
        in_specs=[pl.BlockSpec((tk, m), lambda j, s: (s, 0)),
                  pl.BlockSpec((tk, tn), lambda j, s: (s, j))],
        out_specs=pl.BlockSpec((m, tn), lambda j, s: (0, j)),
        out_shape=jax.ShapeDtypeStruct((m, n), F32),
        compiler_params=_cparams("parallel", "arbitrary"),
    )(a, b)


def _rms_bwd(dh, x, w, dres, name, tm=512):
    t, d = x.shape

    def body(dh_ref, x_ref, w_ref, dres_ref, dx_ref, dxb_ref, dw_ref):
        @pl.when(pl.program_id(0) == 0)
        def _():
            dw_ref[...] = jnp.zeros_like(dw_ref)

        xv = x_ref[...]
        r = lax.rsqrt(jnp.mean(xv * xv, axis=-1, keepdims=True) + EPS)
        xh = xv * r
        dhv = dh_ref[...]
        dw_ref[0:1, :] += jnp.sum(dhv * xh, axis=0, keepdims=True)
        dxh = dhv * w_ref[...]
        dx = dres_ref[...] + r * (dxh - xh * jnp.mean(dxh * xh, axis=-1, keepdims=True))
        dx_ref[...] = dx
        dxb_ref[...] = dx.astype(BF16)

    row = pl.BlockSpec((tm, d), lambda i: (i, 0))
    return pl.pallas_call(
        body, name=name, grid=(t // tm,),
        in_specs=[row, row, pl.BlockSpec((1, d), lambda i: (0, 0)), row],
        out_specs=[row, row, pl.BlockSpec((SUBLANES, d), lambda i: (0, 0))],
        out_shape=[jax.ShapeDtypeStruct((t, d), F32), jax.ShapeDtypeStruct((t, d), BF16),
                   jax.ShapeDtypeStruct((SUBLANES, d), F32)],
        compiler_params=_cparams("arbitrary"),
    )(dh, x, w, dres)


def _rel_index(dist):
    return np.clip(dist, -REL_CLIP, REL_CLIP) + REL_CLIP


def _bias_onehots():
    tw = 3 * LANES
    m = np.arange(ATT_VEC)
    dq = np.where(m <= ATT_KW, 512 - m, 512 - (m - ATT_VEC))
    dk = np.where(m < ATT_KW, m, m - ATT_VEC)
    ohq = np.zeros((tw, ATT_VEC), np.float32)
    ohk = np.zeros((tw, ATT_VEC), np.float32)
    ohq[_rel_index(dq), m] = 1.0
    ohk[_rel_index(dk), m] = 1.0
    return ohq, ohk


def _att_bias(table_pad):
    ohq, ohk = _bias_onehots()
    nslab = ATT_QB // SUBLANES

    def body(t_ref, ohq_ref, ohk_ref, bq_ref, bk_ref):
        tv = jnp.broadcast_to(t_ref[...], (SUBLANES, 3 * LANES))
        row = lax.broadcasted_iota(jnp.int32, (ATT_QB, ATT_KW), 0) // CHUNK
        col = lax.broadcasted_iota(jnp.int32, (ATT_QB, ATT_KW), 1) // CHUNK
        band = (col >= row) & (col <= row + ATT_BAND - 1)
        for oh_ref, out_ref in ((ohq_ref, bq_ref), (ohk_ref, bk_ref)):
            vec = _dot(tv, oh_ref[...], HIGHEST)[0:1, :]
            slab = jnp.concatenate([vec if b == 0 else pltpu.roll(vec, b, 1) for b in range(SUBLANES)], axis=0)
            rows = [slab if a == 0 else pltpu.roll(slab, SUBLANES * a, 1) for a in range(nslab)]
            full = jnp.concatenate(rows, axis=0)[:, :ATT_KW]
            out_ref[...] = jnp.where(band, full, NEG)

    h = table_pad.shape[0]
    oh_spec = pl.BlockSpec((3 * LANES, ATT_VEC), lambda i: (0, 0))
    out_spec = pl.BlockSpec((None, ATT_QB, ATT_KW), lambda i: (i, 0, 0))
    return pl.pallas_call(
        body, name="att_bias", grid=(h,),
        in_specs=[pl.BlockSpec((None, 1, 3 * LANES), lambda i: (i, 0, 0)), oh_spec, oh_spec],
        out_specs=[out_spec, out_spec],
        out_shape=[jax.ShapeDtypeStruct((h, ATT_QB, ATT_KW), F32)] * 2,
        compiler_params=_cparams("parallel"),
    )(table_pad, jnp.asarray(ohq), jnp.asarray(ohk))


def _head_masks():
    lane = lax.broadcasted_iota(jnp.int32, (1, LANES), 1)
    return [lane < ATT_DH, lane >= ATT_DH]


def _att_fwd(proj, bias_q):
    t = proj.shape[0]
    nb = t // ATT_QB
    scale = ATT_DH ** -0.5

    def body(q_ref, k0_ref, k1_ref, k2_ref, v0_ref, v1_ref, v2_ref, b_ref, o_ref, lse_ref, lset_ref):
        i = pl.program_id(0)
        q = q_ref[...].astype(BF16)
        kk = jnp.concatenate([k0_ref[...], k1_ref[...], k2_ref[...]], axis=0).astype(BF16)
        vv = jnp.concatenate([v0_ref[...], v1_ref[...], v2_ref[...]], axis=0).astype(BF16)
        kpos = lax.broadcasted_iota(jnp.int32, (1, ATT_KW), 1) + (i - 2) * ATT_QB
        valid = kpos >= 0
        lane = lax.broadcasted_iota(jnp.int32, (1, LANES), 1)
        masks = _head_masks()
        lse_cols = jnp.zeros((ATT_QB, LANES), F32)
        for p in range(ATT_HEADS // 2):
            cs = slice(p * LANES, (p + 1) * LANES)
            qt, kt, vt = q[:, cs], kk[:, cs], vv[:, cs]
            acc = jnp.zeros((ATT_QB, LANES), F32)
            for sub in range(2):
                h = 2 * p + sub
                s = _dot_nt(jnp.where(masks[sub], qt, 0), kt) * scale + b_ref[h]
                s = jnp.where(valid, s, NEG)
                mx = jnp.max(s, axis=-1, keepdims=True)
                e = jnp.exp(s - mx)
                l = jnp.sum(e, axis=-1, keepdims=True)
                pn = (e / l).astype(BF16)
                acc = acc + _dot(pn, jnp.where(masks[sub], vt, 0))
                lse_cols = lse_cols + jnp.where(lane == h, mx + jnp.log(l), 0.0)
            o_ref[:, cs] = acc.astype(BF16)
        lse_ref[...] = lse_cols
        lset_ref[...] = lse_cols.T[0:SUBLANES, :]

    def kv_spec(off, cb):
        return pl.BlockSpec((ATT_QB, PB), lambda i: (jnp.maximum(i + off, 0), cb))

    return pl.pallas_call(
        body, name="att_fwd", grid=(nb,),
        in_specs=[pl.BlockSpec((ATT_QB, PB), lambda i: (i, CB_QB)),
                  kv_spec(-2, CB_KB), kv_spec(-1, CB_KB), kv_spec(0, CB_KB),
                  kv_spec(-2, CB_VB), kv_spec(-1, CB_VB), kv_spec(0, CB_VB),
                  pl.BlockSpec((ATT_HEADS, ATT_QB, ATT_KW), lambda i: (0, 0, 0))],
        out_specs=[pl.BlockSpec((ATT_QB, WIDTH_B), lambda i: (i, 0)),
                   pl.BlockSpec((ATT_QB, LANES), lambda i: (i, 0)),
                   pl.BlockSpec((SUBLANES, ATT_QB), lambda i: (0, i))],
        out_shape=[jax.ShapeDtypeStruct((t, WIDTH_B), BF16), jax.ShapeDtypeStruct((t, LANES), F32),
                   jax.ShapeDtypeStruct((SUBLANES, t), F32)],
        compiler_params=_cparams("parallel"),
    )(proj, proj, proj, proj, proj, proj, proj, bias_q)


def _att_dq(proj, bias_q, lse, d_ob):
    t = proj.shape[0]
    nb = t // ATT_QB
    scale = ATT_DH ** -0.5
    nslab = ATT_QB // SUBLANES

    def body(q_ref, k0_ref, k1_ref, k2_ref, v0_ref, v1_ref, v2_ref, b_ref, lse_ref, do_ref,
             dq_ref, dlt_ref, slab_ref):
        i = pl.program_id(0)

        @pl.when(i == 0)
        def _():
            slab_ref[...] = jnp.zeros_like(slab_ref)

        q = q_ref[...].astype(BF16)
        kk = jnp.concatenate([k0_ref[...], k1_ref[...], k2_ref[...]], axis=0).astype(BF16)
        vv = jnp.concatenate([v0_ref[...], v1_ref[...], v2_ref[...]], axis=0).astype(BF16)
        do = do_ref[...].astype(BF16)
        kpos = lax.broadcasted_iota(jnp.int32, (1, ATT_KW), 1) + (i - 2) * ATT_QB
        valid = kpos >= 0
        lane = lax.broadcasted_iota(jnp.int32, (1, LANES), 1)
        masks = _head_masks()
        lse_all = lse_ref[...]
        dlt_cols = jnp.zeros((ATT_QB, LANES), F32)
        zpad = jnp.zeros((SUBLANES, ATT_VEC - ATT_KW), F32)
        for p in range(ATT_HEADS // 2):
            cs = slice(p * LANES, (p + 1) * LANES)
            qt, kt, vt, dot_ = q[:, cs], kk[:, cs], vv[:, cs], do[:, cs]
            acc = jnp.zeros((ATT_QB, LANES), F32)
            for sub in range(2):
                h = 2 * p + sub
                s = _dot_nt(jnp.where(masks[sub], qt, 0), kt) * scale + b_ref[h]
                s = jnp.where(valid, s, NEG)
                pr = jnp.exp(s - lse_all[:, h:h + 1])
                dp = _dot_nt(jnp.where(masks[sub], dot_, 0), vt)
                dl = jnp.sum(pr * dp, axis=-1, keepdims=True)
                ds = pr * (dp - dl)
                acc = acc + _dot((ds * scale).astype(BF16), jnp.where(masks[sub], kt, 0))
                dlt_cols = dlt_cols + jnp.where(lane == h, dl, 0.0)
                sl = jnp.zeros((SUBLANES, ATT_VEC), F32)
                for a in range(nslab):
                    piece = jnp.concatenate([ds[a * SUBLANES:(a + 1) * SUBLANES, :], zpad], axis=1)
                    sl = sl + (piece if a == 0 else pltpu.roll(piece, ATT_VEC - SUBLANES * a, 1))
                slab_ref[h] += sl
            dq_ref[:, cs] = acc.astype(BF16)
        dlt_ref[...] = dlt_cols.T[0:SUBLANES, :]

    def kv_spec(off, cb):
        return pl.BlockSpec((ATT_QB, PB), lambda i: (jnp.maximum(i + off, 0), cb))

    return pl.pallas_call(
        body, name="att_dq", grid=(nb,),
        in_specs=[pl.BlockSpec((ATT_QB, PB), lambda i: (i, CB_QB)),
                  kv_spec(-2, CB_KB), kv_spec(-1, CB_KB), kv_spec(0, CB_KB),
                  kv_spec(-2, CB_VB), kv_spec(-1, CB_VB), kv_spec(0, CB_VB),
                  pl.BlockSpec((ATT_HEADS, ATT_QB, ATT_KW), lambda i: (0, 0, 0)),
                  pl.BlockSpec((ATT_QB, LANES), lambda i: (i, 0)),
                  pl.BlockSpec((ATT_QB, WIDTH_B), lambda i: (i, 0))],
        out_specs=[pl.BlockSpec((ATT_QB, WIDTH_B), lambda i: (i, 0)),
                   pl.BlockSpec((SUBLANES, ATT_QB), lambda i: (0, i)),
                   pl.BlockSpec((ATT_HEADS, SUBLANES, ATT_VEC), lambda i: (0, 0, 0))],
        out_shape=[jax.ShapeDtypeStruct((t, WIDTH_B), BF16), jax.ShapeDtypeStruct((SUBLANES, t), F32),
                   jax.ShapeDtypeStruct((ATT_HEADS, SUBLANES, ATT_VEC), F32)],
        compiler_params=_cparams("arbitrary"),
    )(proj, proj, proj, proj, proj, proj, proj, bias_q, lse, d_ob)


def _att_dkv(proj, bias_k, lse_t, dlt_t, d_ob):
    t = proj.shape[0]
    nb = t // ATT_QB
    scale = ATT_DH ** -0.5

    def body(k_ref, v_ref, q0_ref, q1_ref, q2_ref, d0_ref, d1_ref, d2_ref, l0_ref, l1_ref, l2_ref,
             e0_ref, e1_ref, e2_ref, b_ref, dk_ref, dv_ref):
        i = pl.program_id(0)
        k = k_ref[...].astype(BF16)
        v = v_ref[...].astype(BF16)
        qq = jnp.concatenate([q0_ref[...], q1_ref[...], q2_ref[...]], axis=0).astype(BF16)
        do = jnp.concatenate([d0_ref[...], d1_ref[...], d2_ref[...]], axis=0).astype(BF16)
        lse = jnp.concatenate([l0_ref[...], l1_ref[...], l2_ref[...]], axis=1)
        dlt = jnp.concatenate([e0_ref[...], e1_ref[...], e2_ref[...]], axis=1)
        qpos = lax.broadcasted_iota(jnp.int32, (1, ATT_KW), 1) + i * ATT_QB
        valid = qpos < t
        masks = _head_masks()
        for p in range(ATT_HEADS // 2):
            cs = slice(p * LANES, (p + 1) * LANES)
            kt, vt, qt, dot_ = k[:, cs], v[:, cs], qq[:, cs], do[:, cs]
            acc_k = jnp.zeros((ATT_QB, LANES), F32)
            acc_v = jnp.zeros((ATT_QB, LANES), F32)
            for sub in range(2):
                h = 2 * p + sub
                st = _dot_nt(jnp.where(masks[sub], kt, 0), qt) * scale + b_ref[h]
                st = jnp.where(valid, st, NEG)
                pt = jnp.exp(st - lse[h:h + 1, :])
                dot_m = jnp.where(masks[sub], dot_, 0)
                acc_v = acc_v + _dot(pt.astype(BF16), dot_m)
                dpt = _dot_nt(jnp.where(masks[sub], vt, 0), dot_)
                dst = pt * (dpt - dlt[h:h + 1, :])
                acc_k = acc_k + _dot((dst * scale).astype(BF16), jnp.where(masks[sub], qt, 0))
            dk_ref[:, cs] = acc_k.astype(BF16)
            dv_ref[:, cs] = acc_v.astype(BF16)

    def q_spec(off, cb):
        return pl.BlockSpec((ATT_QB, PB), lambda i: (jnp.minimum(i + off, nb - 1), cb))

    def d_spec(off):
        return pl.BlockSpec((ATT_QB, WIDTH_B), lambda i: (jnp.minimum(i + off, nb - 1), 0))

    def r_spec(off):
        return pl.BlockSpec((SUBLANES, ATT_QB), lambda i: (0, jnp.minimum(i + off, nb - 1)))

    row = pl.BlockSpec((ATT_QB, WIDTH_B), lambda i: (i, 0))
    return pl.pallas_call(
        body, name="att_dkv", grid=(nb,),
        in_specs=[pl.BlockSpec((ATT_QB, PB), lambda i: (i, CB_KB)), pl.BlockSpec((ATT_QB, PB), lambda i: (i, CB_VB)),
                  q_spec(0, CB_QB), q_spec(1, CB_QB), q_spec(2, CB_QB),
                  d_spec(0), d_spec(1), d_spec(2), r_spec(0), r_spec(1), r_spec(2),
                  r_spec(0), r_spec(1), r_spec(2),
                  pl.BlockSpec((ATT_HEADS, ATT_QB, ATT_KW), lambda i: (0, 0, 0))],
        out_specs=[row, row],
        out_shape=[jax.ShapeDtypeStruct((t, WIDTH_B), BF16)] * 2,
        compiler_params=_cparams("parallel"),
    )(proj, proj, proj, proj, proj, d_ob, d_ob, d_ob, lse_t, lse_t, lse_t, dlt_t, dlt_t, dlt_t, bias_k)


def _relbias_grad(slabs):
    ohq, _ = _bias_onehots()

    def body(s_ref, oh_ref, o_ref):
        sv = s_ref[...]
        vec = sv[0:1, :]
        for b in range(1, SUBLANES):
            vec = vec + pltpu.roll(sv[b:b + 1, :], ATT_VEC - b, 1)
        o_ref[...] = _dot_nt(jnp.broadcast_to(vec, (SUBLANES, ATT_VEC)), oh_ref[...], HIGHEST)[0:1, :]

    h = slabs.shape[0]
    return pl.pallas_call(
        body, name="att_dbias", grid=(h,),
        in_specs=[pl.BlockSpec((None, SUBLANES, ATT_VEC), lambda i: (i, 0, 0)),
                  pl.BlockSpec((3 * LANES, ATT_VEC), lambda i: (0, 0))],
        out_specs=pl.BlockSpec((None, 1, 3 * LANES), lambda i: (i, 0, 0)),
        out_shape=jax.ShapeDtypeStruct((h, 1, 3 * LANES), F32),
        compiler_params=_cparams("parallel"),
    )(slabs, jnp.asarray(ohq))


GDN_TM = 512
GDN_CB = 4
HALO = SUBLANES


def _conv_taps(ext, width, lead, n):
    return [(ext if k == width - 1 else pltpu.roll(ext, width - 1 - k, 0))[lead:lead + n] for k in range(width)]


def _prev_halo_spec(tm, width, cb):
    return pl.BlockSpec((HALO, width), lambda i: (jnp.maximum(i * (tm // HALO) - 1, 0), cb))


def _next_halo_spec(tm, width, cb, t):
    return pl.BlockSpec((HALO, width), lambda i: (jnp.minimum((i + 1) * (tm // HALO), t // HALO - 1), cb))


def _gdn_prep_fwd(proj, conv_w):
    t = proj.shape[0]
    tm = GDN_TM

    def body(q_ref, k_ref, v_ref, hq_ref, hk_ref, hv_ref, w_ref, qn_ref, kn_ref, vo_ref):
        first = pl.program_id(0) == 0
        for idx, (x_ref, h_ref, o_ref) in enumerate(((q_ref, hq_ref, qn_ref), (k_ref, hk_ref, kn_ref),
                                                      (v_ref, hv_ref, vo_ref))):
            halo = jnp.where(first, 0.0, h_ref[...])
            ext = jnp.concatenate([halo, x_ref[...]], axis=0)
            w = w_ref[:, idx * KEY_A:(idx + 1) * KEY_A]
            taps = _conv_taps(ext, GDN_CONV, HALO, tm)
            y = sum(w[k:k + 1, :] * taps[k] for k in range(GDN_CONV))
            a = y * _sigmoid(y)
            if idx < 2:
                for h in range(GDN_HEADS):
                    cs = slice(h * GDN_DK, (h + 1) * GDN_DK)
                    seg = a[:, cs]
                    o_ref[:, cs] = seg * lax.rsqrt(jnp.sum(seg * seg, axis=-1, keepdims=True) + EPS)
            else:
                o_ref[...] = a

    row = pl.BlockSpec((tm, KEY_A), lambda i: (i, 0))
    return pl.pallas_call(
        body, name="gdn_prep_fwd", grid=(t // tm,),
        in_specs=[pl.BlockSpec((tm, PB), lambda i: (i, CB_QA)), pl.BlockSpec((tm, PB), lambda i: (i, CB_KA)),
                  pl.BlockSpec((tm, PB), lambda i: (i, CB_VA)),
                  _prev_halo_spec(tm, PB, CB_QA), _prev_halo_spec(tm, PB, CB_KA), _prev_halo_spec(tm, PB, CB_VA),
                  pl.BlockSpec((GDN_CONV, 3 * KEY_A), lambda i: (0, 0))],
        out_specs=[row, row, row],
        out_shape=[jax.ShapeDtypeStruct((t, KEY_A), F32)] * 3,
        compiler_params=_cparams("parallel"),
    )(proj, proj, proj, proj, proj, proj, conv_w)


def _gdn_prep_bwd(proj, conv_w, dqn, dkn, dv):
    t = proj.shape[0]
    tm = GDN_TM
    nt = t // tm
    n_ext = tm + HALO

    def body(q_ref, k_ref, v_ref, pq_ref, pk_ref, pv_ref, nq_ref, nk_ref, nv_ref,
             dq_ref, dk_ref, dv_ref, ndq_ref, ndk_ref, ndv_ref, w_ref, oq_ref, ok_ref, ov_ref, dw_ref):
        i = pl.program_id(0)
        first, last = i == 0, i == nt - 1

        @pl.when(first)
        def _():
            dw_ref[...] = jnp.zeros_like(dw_ref)

        groups = ((q_ref, pq_ref, nq_ref, dq_ref, ndq_ref, oq_ref), (k_ref, pk_ref, nk_ref, dk_ref, ndk_ref, ok_ref),
                  (v_ref, pv_ref, nv_ref, dv_ref, ndv_ref, ov_ref))
        for idx, (x_ref, p_ref, n_ref, d_ref, nd_ref, o_ref) in enumerate(groups):
            cs_all = slice(idx * KEY_A, (idx + 1) * KEY_A)
            ext = jnp.concatenate([jnp.where(first, 0.0, p_ref[...]), x_ref[...], jnp.where(last, 0.0, n_ref[...])], axis=0)
            w = w_ref[:, cs_all]
            taps = _conv_taps(ext, GDN_CONV, HALO, n_ext)
            y = sum(w[k:k + 1, :] * taps[k] for k in range(GDN_CONV))
            sg = _sigmoid(y)
            a = y * sg
            dup = jnp.concatenate([d_ref[...], jnp.where(last, 0.0, nd_ref[...])], axis=0)
            if idx < 2:
                segs = []
                for h in range(GDN_HEADS):
                    cs = slice(h * GDN_DK, (h + 1) * GDN_DK)
                    seg = a[:, cs]
                    r = lax.rsqrt(jnp.sum(seg * seg, axis=-1, keepdims=True) + EPS)
                    nrm = seg * r
                    dn = dup[:, cs]
                    segs.append(r * (dn - nrm * jnp.sum(dn * nrm, axis=-1, keepdims=True)))
                da = jnp.concatenate(segs, axis=1)
            else:
                da = dup
            dy = da * sg * (1.0 + y * (1.0 - sg))
            dx = sum(w[k:k + 1, :] * (dy if k == GDN_CONV - 1 else pltpu.roll(dy, n_ext - (GDN_CONV - 1 - k), 0))[:tm]
                     for k in range(GDN_CONV))
            o_ref[...] = dx.astype(BF16)
            for k in range(GDN_CONV):
                dw_ref[k:k + 1, cs_all] += jnp.sum(dy[:tm] * taps[k][:tm], axis=0, keepdims=True)

    row = pl.BlockSpec((tm, KEY_A), lambda i: (i, 0))
    nrow = _next_halo_spec(tm, KEY_A, 0, t)
    return pl.pallas_call(
        body, name="gdn_prep_bwd", grid=(nt,),
        in_specs=[pl.BlockSpec((tm, PB), lambda i: (i, CB_QA)), pl.BlockSpec((tm, PB), lambda i: (i, CB_KA)),
                  pl.BlockSpec((tm, PB), lambda i: (i, CB_VA)),
                  _prev_halo_spec(tm, PB, CB_QA), _prev_halo_spec(tm, PB, CB_KA), _prev_halo_spec(tm, PB, CB_VA),
                  _next_halo_spec(tm, PB, CB_QA, t), _next_halo_spec(tm, PB, CB_KA, t), _next_halo_spec(tm, PB, CB_VA, t),
                  row, row, row, nrow, nrow, nrow,
                  pl.BlockSpec((GDN_CONV, 3 * KEY_A), lambda i: (0, 0))],
        out_specs=[row, row, row, pl.BlockSpec((SUBLANES, 3 * KEY_A), lambda i: (0, 0))],
        out_shape=[jax.ShapeDtypeStruct((t, KEY_A), BF16)] * 3 + [jax.ShapeDtypeStruct((SUBLANES, 3 * KEY_A), F32)],
        compiler_params=_cparams("arbitrary"),
    )(proj, proj, proj, proj, proj, proj, proj, proj, proj, dqn, dkn, dv, dqn, dkn, dv, conv_w)


class _Pair(dict):
    __getattr__ = dict.__getitem__
    __setattr__ = dict.__setitem__


def _gdn_terms(bd, par, kn_ref, qn_ref):
    c = CHUNK
    ii = lax.broadcasted_iota(jnp.int32, (c, c), 0)
    jj = lax.broadcasted_iota(jnp.int32, (c, c), 1)
    strict, incl = ii > jj, ii >= jj
    ltri = incl.astype(F32)
    ts = []
    for cc in range(GDN_CB):
        for h in range(GDN_HEADS):
            t = _Pair(cc=cc, h=h, rows=slice(cc * c, (cc + 1) * c), cs=slice(h * GDN_DK, (h + 1) * GDN_DK),
                      strict=strict, incl=incl)
            t.beta = _sigmoid(bd[t.rows, h:h + 1])
            t.ea = jnp.exp(par[0:1, h:h + 1])
            t.sp_arg = bd[t.rows, GDN_HEADS + h:GDN_HEADS + h + 1] + par[1:2, h:h + 1]
            t.g = -t.ea * _softplus(t.sp_arg)
            t.k = kn_ref[t.rows, t.cs]
            t.q = qn_ref[t.rows, t.cs] * (GDN_DK ** -0.5)
            t.kb, t.qb = t.k.astype(BF16), t.q.astype(BF16)
            ts.append(t)
    for t in ts:
        t.gb = _dot(ltri, jnp.broadcast_to(t.g, (c, GDN_DK)), HIGHEST)
    for t in ts:
        t.kk = _dot_nt(t.kb, t.kb)
        t.qk = _dot_nt(t.qb, t.kb)
    for t in ts:
        gc = t.gb[:, :c]
        diff = gc - gc.T
        t.dec_s = jnp.exp(jnp.where(strict, diff, NEG))
        t.dec_i = jnp.exp(jnp.where(incl, diff, NEG))
        t.gam = jnp.exp(t.gb)
        glast = t.gb[c - 1:c, :]
        t.e_rest = jnp.exp(glast - t.gb)
        t.gl = jnp.exp(glast)
        t.p = t.qk * t.dec_i
    return ts


def _gdn_fwd(qn, kn, v, proj, par, gnw):
    t = qn.shape[0]
    c = CHUNK
    nc = t // c
    r_ = GDN_CB * c

    def body(qn_ref, kn_ref, v_ref, bd_ref, z_ref, par_ref, gnw_ref,
             oan_ref, o_ref, sp_ref, w_ref, u_ref, tm_ref, s_ref):
        @pl.when(pl.program_id(0) == 0)
        def _():
            s_ref[...] = jnp.zeros_like(s_ref)

        bd, par, gnw_v = bd_ref[...], par_ref[...], gnw_ref[...]
        eye = (lax.broadcasted_iota(jnp.int32, (c, c), 0) == lax.broadcasted_iota(jnp.int32, (c, c), 1)).astype(F32)
        ts = _gdn_terms(bd, par, kn_ref, qn_ref)
        for t in ts:
            t.vv = v_ref[t.rows, t.cs]
            t.x = -(t.beta * t.kk * t.dec_s)
            t.tinv = eye + t.x
        for _ in range(5):
            for t in ts:
                t.x = _dot(t.x, t.x, HIGHEST)
            for t in ts:
                t.tinv = t.tinv + _dot(t.tinv, t.x, HIGHEST)
        for t in ts:
            t.wm = _dot(t.tinv, (t.beta * t.gam) * t.k, HIGHEST)
            t.uv = _dot(t.tinv, t.beta * t.vv, HIGHEST)
        for t in ts:
            w_ref[t.rows, t.cs] = t.wm
            tm_ref[t.cc, t.h] = t.tinv
            t.wb = t.wm.astype(BF16)
            t.qgb = (t.q * t.gam).astype(BF16)
            t.kdb = (t.k * t.e_rest).astype(BF16)
            t.pb = t.p.astype(BF16)
        for cc in range(GDN_CB):
            tc = [t for t in ts if t.cc == cc]
            for t in tc:
                t.sh = s_ref[t.h]
                t.sb = t.sh.astype(BF16)
                sp_ref[cc, t.h] = t.sh
            for t in tc:
                t.ws = _dot(t.wb, t.sb)
                t.qs = _dot(t.qgb, t.sb)
            for t in tc:
                t.u = t.uv - t.ws
                t.ub = t.u.astype(BF16)
            for t in tc:
                t.pu = _dot(t.pb, t.ub)
                t.ku = _dot_tn(t.kdb, t.ub)
            for t in tc:
                t.o = t.qs + t.pu
                s_ref[t.h] = t.gl * t.sh + t.ku
                u_ref[t.rows, t.cs] = t.u
                o_ref[t.rows, t.cs] = t.o
        for t in ts:
            zz = z_ref[t.rows, t.cs]
            rr = lax.rsqrt(jnp.mean(t.o * t.o, axis=-1, keepdims=True) + EPS)
            oan_ref[t.rows, t.cs] = ((t.o * rr) * gnw_v * (zz * _sigmoid(zz))).astype(BF16)

    row = pl.BlockSpec((r_, KEY_A), lambda i: (i, 0))
    return pl.pallas_call(
        body, name="gdn_fwd", grid=(nc // GDN_CB,),
        in_specs=[row, row, row, pl.BlockSpec((r_, LANES), lambda i: (i, CB_BD)),
                  pl.BlockSpec((r_, PB), lambda i: (i, CB_ZA)),
                  pl.BlockSpec((SUBLANES, LANES), lambda i: (0, 0)), pl.BlockSpec((1, GDN_DK), lambda i: (0, 0))],
        out_specs=[row, row, pl.BlockSpec((GDN_CB, GDN_HEADS, GDN_DK, GDN_DK), lambda i: (i, 0, 0, 0)),
                   row, row, pl.BlockSpec((GDN_CB, GDN_HEADS, c, c), lambda i: (i, 0, 0, 0))],
        out_shape=[jax.ShapeDtypeStruct((t, KEY_A), BF16), jax.ShapeDtypeStruct((t, KEY_A), F32),
                   jax.ShapeDtypeStruct((nc, GDN_HEADS, GDN_DK, GDN_DK), F32),
                   jax.ShapeDtypeStruct((t, KEY_A), F32), jax.ShapeDtypeStruct((t, KEY_A), F32),
                   jax.ShapeDtypeStruct((nc, GDN_HEADS, c, c), F32)],
        scratch_shapes=[pltpu.VMEM((GDN_HEADS, GDN_DK, GDN_DK), F32)],
        compiler_params=_cparams("arbitrary"),
    )(qn, kn, v, proj, proj, par, gnw)


def _gdn_bwd(qn, kn, v, proj, par, gnw, o, sprev, wst, ust, tst, d_oan):
    t = qn.shape[0]
    c = CHUNK
    nc = t // c
    nb = nc // GDN_CB
    r_ = GDN_CB * c

    def body(qn_ref, kn_ref, v_ref, bd_ref, z_ref, par_ref, gnw_ref, o_ref, sp_ref, w_ref, u_ref, tm_ref, do_ref,
             dqn_ref, dkn_ref, dv_ref, dz_ref, dbd_ref, acc_ref, ds_ref):
        @pl.when(pl.program_id(0) == 0)
        def _():
            ds_ref[...] = jnp.zeros_like(ds_ref)
            acc_ref[...] = jnp.zeros_like(acc_ref)

        bd, par, gnw_v = bd_ref[...], par_ref[...], gnw_ref[...]
        lane = lax.broadcasted_iota(jnp.int32, (1, LANES), 1)
        rix = lax.broadcasted_iota(jnp.int32, (c, 1), 0)
        ii = lax.broadcasted_iota(jnp.int32, (c, c), 0)
        jj = lax.broadcasted_iota(jnp.int32, (c, c), 1)
        upper = (jj >= ii).astype(F32)
        acc_a = jnp.zeros((1, LANES), F32)
        acc_d = jnp.zeros((1, LANES), F32)
        acc_g = jnp.zeros((1, LANES), F32)
        ts = _gdn_terms(bd, par, kn_ref, qn_ref)
        for t in ts:
            t.vv = v_ref[t.rows, t.cs]
            t.sh = sp_ref[t.cc, t.h]
            t.sb = t.sh.astype(BF16)
            t.wm, t.u, t.tinv = w_ref[t.rows, t.cs], u_ref[t.rows, t.cs], tm_ref[t.cc, t.h]
            t.wb, t.ub = t.wm.astype(BF16), t.u.astype(BF16)
            ov, zz, dout = o_ref[t.rows, t.cs], z_ref[t.rows, t.cs], do_ref[t.rows, t.cs]
            sg = _sigmoid(zz)
            sil = zz * sg
            rr = lax.rsqrt(jnp.mean(ov * ov, axis=-1, keepdims=True) + EPS)
            on = ov * rr
            dz_ref[t.rows, t.cs] = (dout * on * gnw_v * (sg * (1.0 + zz * (1.0 - sg)))).astype(BF16)
            acc_g = acc_g + jnp.sum(dout * on * sil, axis=0, keepdims=True)
            don = dout * gnw_v * sil
            t.dob = (rr * (don - on * jnp.mean(don * on, axis=-1, keepdims=True))).astype(BF16)
            t.qg = t.q * t.gam
            t.kd = t.k * t.e_rest
            t.qgb, t.kdb = t.qg.astype(BF16), t.kd.astype(BF16)
            t.ptb = t.p.T.astype(BF16)
        for cc in reversed(range(GDN_CB)):
            tc = [t for t in ts if t.cc == cc]
            for t in tc:
                t.dsn = ds_ref[t.h]
                t.dsnb = t.dsn.astype(BF16)
            for t in tc:
                t.du = _dot(t.ptb, t.dob) + _dot(t.kdb, t.dsnb)
                t.dkd = _dot_nt(t.ub, t.dsnb)
                t.dgl = jnp.sum(jnp.sum(t.dsn * t.sh, axis=1, keepdims=True), axis=0, keepdims=True)
            for t in tc:
                t.dub = t.du.astype(BF16)
            for t in tc:
                ds_ref[t.h] = t.gl * t.dsn + _dot_tn(t.qgb, t.dob) - _dot_tn(t.wb, t.dub)
        for t in ts:
            t.dqg = _dot_nt(t.dob, t.sb)
            t.dp = _dot_nt(t.dob, t.ub)
            t.dwm = -_dot_nt(t.dub, t.sb)
            t.uv = t.u + _dot(t.wb, t.sb)
        for t in ts:
            tt = t.tinv.T
            t.dbk = _dot(tt, t.dwm, HIGHEST)
            t.dbv = _dot(tt, t.du, HIGHEST)
        for t in ts:
            d_a = -(_dot_nt(t.dbk.astype(BF16), t.wb) + _dot_nt(t.dbv.astype(BF16), t.uv.astype(BF16)))
            t.d_a = jnp.where(t.strict, d_a, 0.0)
        for t in ts:
            t.dkk = t.d_a * t.beta * t.dec_s
            t.dqk = t.dp * t.dec_i
            t.dqkb = t.dqk.astype(BF16)
        for t in ts:
            t.dq = _dot(t.dqkb, t.kb) + t.dqg * t.gam
            t.dk = (t.dbk * (t.beta * t.gam) + _dot_tn(t.dqkb, t.qb) + _dot((t.dkk + t.dkk.T).astype(BF16), t.kb)
                    + t.dkd * t.e_rest)
        for t in ts:
            dbeta = (jnp.sum(t.d_a * t.kk * t.dec_s, axis=-1, keepdims=True)
                     + jnp.sum(t.dbk * t.k * t.gam, axis=-1, keepdims=True) + jnp.sum(t.dbv * t.vv, axis=-1, keepdims=True))
            t.dbl = dbeta * t.beta * (1.0 - t.beta)
            dv_ref[t.rows, t.cs] = t.dbv * t.beta
            bk = (t.beta * t.gam) * t.k
            zc = jnp.sum(t.dkd * t.kd, axis=-1, keepdims=True)
            xs = t.dkk * t.kk + t.dp * t.p
            dgc = (jnp.sum(xs, axis=-1, keepdims=True) - jnp.sum(xs.T, axis=-1, keepdims=True)
                   + jnp.sum(t.dbk * bk, axis=-1, keepdims=True) + jnp.sum(t.dqg * t.qg, axis=-1, keepdims=True) - zc)
            dglast = jnp.sum(zc, axis=0, keepdims=True) + t.dgl * t.gl[:, 0:1]
            t.dgc = dgc + jnp.where(rix == c - 1, dglast, 0.0)
        for t in ts:
            t.dg = _dot(upper, jnp.broadcast_to(t.dgc, (c, GDN_DK)), HIGHEST)[:, 0:1]
        dbd_tiles = [jnp.zeros((c, LANES), F32) for _ in range(GDN_CB)]
        for t in ts:
            ddl = t.dg * (-t.ea) * _sigmoid(t.sp_arg)
            acc_a = acc_a + jnp.where(lane == t.h, jnp.sum(t.dg * t.g, axis=0, keepdims=True), 0.0)
            acc_d = acc_d + jnp.where(lane == t.h, jnp.sum(ddl, axis=0, keepdims=True), 0.0)
            dbd_tiles[t.cc] = (dbd_tiles[t.cc] + jnp.where(lane == t.h, t.dbl, 0.0)
                               + jnp.where(lane == GDN_HEADS + t.h, ddl, 0.0))
            dqn_ref[t.rows, t.cs] = t.dq * (GDN_DK ** -0.5)
            dkn_ref[t.rows, t.cs] = t.dk
        for cc in range(GDN_CB):
            dbd_ref[cc * c:(cc + 1) * c, :] = dbd_tiles[cc]
        acc_ref[0:1, :] += acc_a
        acc_ref[1:2, :] += acc_d
        acc_ref[2:3, :] += acc_g

    def rev(i):
        return nb - 1 - i

    row = pl.BlockSpec((r_, KEY_A), lambda i: (rev(i), 0))
    st = pl.BlockSpec((GDN_CB, GDN_HEADS, GDN_DK, GDN_DK), lambda i: (rev(i), 0, 0, 0))
    tt_spec = pl.BlockSpec((GDN_CB, GDN_HEADS, c, c), lambda i: (rev(i), 0, 0, 0))
    return pl.pallas_call(
        body, name="gdn_bwd", grid=(nb,),
        in_specs=[row, row, row, pl.BlockSpec((r_, LANES), lambda i: (rev(i), CB_BD)),
                  pl.BlockSpec((r_, PB), lambda i: (rev(i), CB_ZA)),
                  pl.BlockSpec((SUBLANES, LANES), lambda i: (0, 0)), pl.BlockSpec((1, GDN_DK), lambda i: (0, 0)),
                  row, st, row, row, tt_spec, row],
        out_specs=[row, row, row, row, pl.BlockSpec((r_, LANES), lambda i: (rev(i), 0)),
                   pl.BlockSpec((SUBLANES, LANES), lambda i: (0, 0))],
        out_shape=[jax.ShapeDtypeStruct((t, KEY_A), F32)] * 3 + [jax.ShapeDtypeStruct((t, KEY_A), BF16),
                   jax.ShapeDtypeStruct((t, LANES), F32), jax.ShapeDtypeStruct((SUBLANES, LANES), F32)],
        scratch_shapes=[pltpu.VMEM((GDN_HEADS, GDN_DK, GDN_DK), F32)],
        compiler_params=_cparams("arbitrary"),
    )(qn, kn, v, proj, proj, par, gnw, o, sprev, wst, ust, tst, d_oan)


def _merge_fwd(oan, ob, proj, x, wba, wbb, wout, tm=512):
    t = x.shape[0]

    def body(oa_ref, ob_ref, ga_ref, gb_ref, x_ref, wba_ref, wbb_ref, wout_ref, x2_ref):
        ya = _dot(oa_ref[...], wba_ref[...])
        yb = _dot(ob_ref[...], wbb_ref[...])
        mix = _sigmoid(ga_ref[...]) * ya + _sigmoid(gb_ref[...]) * yb
        x2_ref[...] = x_ref[...] + _dot(mix.astype(BF16), wout_ref[...])

    half = pl.BlockSpec((tm, KEY_A), lambda i: (i, 0))
    row = pl.BlockSpec((tm, D_MODEL), lambda i: (i, 0))
    wsmall = pl.BlockSpec((KEY_A, D_MODEL), lambda i: (0, 0))
    return pl.pallas_call(
        body, name="merge_fwd", grid=(t // tm,),
        in_specs=[half, half, pl.BlockSpec((tm, D_MODEL), lambda i: (i, CB_GA)),
                  pl.BlockSpec((tm, D_MODEL), lambda i: (i, CB_GB)), row, wsmall, wsmall,
                  pl.BlockSpec((D_MODEL, D_MODEL), lambda i: (0, 0))],
        out_specs=row,
        out_shape=jax.ShapeDtypeStruct((t, D_MODEL), F32),
        compiler_params=_cparams("parallel"),
    )(oan, ob, proj, proj, x, wba, wbb, wout)


def _merge_bwd(dx2b, oan, ob, proj, wba, wbb, wout_t, wba_t, wbb_t, tm=512):
    t = dx2b.shape[0]

    def body(dx_ref, oa_ref, ob_ref, ga_ref, gb_ref, wba_ref, wbb_ref, woutt_ref, wbat_ref, wbbt_ref,
             dga_ref, dgb_ref, doa_ref, dob_ref, mix_ref, dya_ref, dyb_ref):
        dmix = _dot(dx_ref[...], woutt_ref[...])
        ya = _dot(oa_ref[...], wba_ref[...])
        yb = _dot(ob_ref[...], wbb_ref[...])
        sa, sb = _sigmoid(ga_ref[...]), _sigmoid(gb_ref[...])
        mix_ref[...] = (sa * ya + sb * yb).astype(BF16)
        dga_ref[...] = (dmix * ya * sa * (1.0 - sa)).astype(BF16)
        dgb_ref[...] = (dmix * yb * sb * (1.0 - sb)).astype(BF16)
        dya = (dmix * sa).astype(BF16)
        dyb = (dmix * sb).astype(BF16)
        dya_ref[...] = dya
        dyb_ref[...] = dyb
        doa_ref[...] = _dot(dya, wbat_ref[...])
        dob_ref[...] = _dot(dyb, wbbt_ref[...])

    half = pl.BlockSpec((tm, KEY_A), lambda i: (i, 0))
    row = pl.BlockSpec((tm, D_MODEL), lambda i: (i, 0))
    wsmall = pl.BlockSpec((KEY_A, D_MODEL), lambda i: (0, 0))
    wsmall_t = pl.BlockSpec((D_MODEL, KEY_A), lambda i: (0, 0))
    big = jax.ShapeDtypeStruct((t, D_MODEL), BF16)
    return pl.pallas_call(
        body, name="merge_bwd", grid=(t // tm,),
        in_specs=[row, half, half, pl.BlockSpec((tm, D_MODEL), lambda i: (i, CB_GA)),
                  pl.BlockSpec((tm, D_MODEL), lambda i: (i, CB_GB)), wsmall, wsmall,
                  pl.BlockSpec((D_MODEL, D_MODEL), lambda i: (0, 0)), wsmall_t, wsmall_t],
        out_specs=[row, row, half, half, row, row, row],
        out_shape=[big, big, jax.ShapeDtypeStruct((t, KEY_A), F32), jax.ShapeDtypeStruct((t, KEY_A), F32), big, big, big],
        compiler_params=_cparams("parallel"),
    )(dx2b, oan, ob, proj, proj, wba, wbb, wout_t, wba_t, wbb_t)


FFN_TM = 128
FFN_W = 2 * D_FF


def _ffn_conv(up_ref, halo_ref, cw_ref, cb_ref, first):
    ext = jnp.concatenate([jnp.where(first, 0.0, halo_ref[...]), up_ref[...]], axis=0)
    taps = _conv_taps(ext, FFN_CONV, HALO, FFN_TM)
    cw = cw_ref[...]
    u = sum(cw[k:k + 1, :] * taps[k] for k in range(FFN_CONV)) + cb_ref[...]
    return u, taps


def _ffn_tail(up, cw, cb, wdown, x2, tgt, w3):
    t = x2.shape[0]
    tm = FFN_TM

    def body(up_ref, halo_ref, cw_ref, cb_ref, wd_ref, x2_ref, tgt_ref, w3_ref, dx_ref, dxb_ref, act_ref, acc_ref):
        first = pl.program_id(0) == 0

        @pl.when(first)
        def _():
            acc_ref[...] = jnp.zeros_like(acc_ref)

        u, _ = _ffn_conv(up_ref, halo_ref, cw_ref, cb_ref, first)
        gate, upp = u[:, :D_FF], u[:, D_FF:]
        act = (gate * _sigmoid(gate) * upp).astype(BF16)
        act_ref[...] = act
        x3 = x2_ref[...] + _dot(act, wd_ref[...])
        r = lax.rsqrt(jnp.mean(x3 * x3, axis=-1, keepdims=True) + EPS)
        xh = x3 * r
        w3v = w3_ref[...]
        err = xh * w3v - tgt_ref[...]
        loss = 0.5 * jnp.sum(jnp.mean(err * err, axis=-1, keepdims=True), axis=0, keepdims=True)
        dy = err * (1.0 / D_MODEL)
        acc_ref[0:1, :] += jnp.sum(dy * xh, axis=0, keepdims=True)
        acc_ref[1:2, :] += jnp.broadcast_to(loss, (1, D_MODEL))
        dxh = dy * w3v
        dx = r * (dxh - xh * jnp.mean(dxh * xh, axis=-1, keepdims=True))
        dx_ref[...] = dx
        dxb_ref[...] = dx.astype(BF16)

    row = pl.BlockSpec((tm, D_MODEL), lambda i: (i, 0))
    vec = pl.BlockSpec((1, D_MODEL), lambda i: (0, 0))
    return pl.pallas_call(
        body, name="ffn_tail", grid=(t // tm,),
        in_specs=[pl.BlockSpec((tm, FFN_W), lambda i: (i, 0)), _prev_halo_spec(tm, FFN_W, 0),
                  pl.BlockSpec((SUBLANES, FFN_W), lambda i: (0, 0)), pl.BlockSpec((1, FFN_W), lambda i: (0, 0)),
                  pl.BlockSpec((D_FF, D_MODEL), lambda i: (0, 0)), row, row, vec],
        out_specs=[row, row, pl.BlockSpec((tm, D_FF), lambda i: (i, 0)), pl.BlockSpec((SUBLANES, D_MODEL), lambda i: (0, 0))],
        out_shape=[jax.ShapeDtypeStruct((t, D_MODEL), F32), jax.ShapeDtypeStruct((t, D_MODEL), BF16),
                   jax.ShapeDtypeStruct((t, D_FF), BF16), jax.ShapeDtypeStruct((SUBLANES, D_MODEL), F32)],
        compiler_params=_cparams("arbitrary"),
    )(up, up, cw, cb, wdown, x2, tgt, w3)


def _ffn_bwd_act(dx3b, wdown_t, up, cw, cb):
    t = up.shape[0]
    tm = FFN_TM

    def body(dx_ref, wdt_ref, up_ref, halo_ref, cw_ref, cb_ref, du_ref, acc_ref):
        first = pl.program_id(0) == 0

        @pl.when(first)
        def _():
            acc_ref[...] = jnp.zeros_like(acc_ref)

        dact = _dot(dx_ref[...], wdt_ref[...])
        u, taps = _ffn_conv(up_ref, halo_ref, cw_ref, cb_ref, first)
        gate, upp = u[:, :D_FF], u[:, D_FF:]
        sg = _sigmoid(gate)
        du = jnp.concatenate([dact * upp * (sg * (1.0 + gate * (1.0 - sg))), dact * (gate * sg)], axis=1)
        du_ref[...] = du
        for k in range(FFN_CONV):
            acc_ref[k:k + 1, :] += jnp.sum(du * taps[k], axis=0, keepdims=True)
        acc_ref[FFN_CONV:FFN_CONV + 1, :] += jnp.sum(du, axis=0, keepdims=True)

    return pl.pallas_call(
        body, name="ffn_bwd_act", grid=(t // tm,),
        in_specs=[pl.BlockSpec((tm, D_MODEL), lambda i: (i, 0)), pl.BlockSpec((D_MODEL, D_FF), lambda i: (0, 0)),
                  pl.BlockSpec((tm, FFN_W), lambda i: (i, 0)), _prev_halo_spec(tm, FFN_W, 0),
                  pl.BlockSpec((SUBLANES, FFN_W), lambda i: (0, 0)), pl.BlockSpec((1, FFN_W), lambda i: (0, 0))],
        out_specs=[pl.BlockSpec((tm, FFN_W), lambda i: (i, 0)), pl.BlockSpec((SUBLANES, FFN_W), lambda i: (0, 0))],
        out_shape=[jax.ShapeDtypeStruct((t, FFN_W), F32), jax.ShapeDtypeStruct((SUBLANES, FFN_W), F32)],
        compiler_params=_cparams("arbitrary"),
    )(dx3b, wdown_t, up, up, cw, cb)


def _ffn_bwd_conv(du, cw):
    t = du.shape[0]
    tm = FFN_TM
    nt = t // tm
    n_ext = tm + HALO

    def body(du_ref, nxt_ref, cw_ref, o_ref):
        last = pl.program_id(0) == nt - 1
        ext = jnp.concatenate([du_ref[...], jnp.where(last, 0.0, nxt_ref[...])], axis=0)
        cw = cw_ref[...]
        acc = cw[FFN_CONV - 1:FFN_CONV, :] * ext[:tm]
        for k in range(FFN_CONV - 1):
            acc = acc + cw[k:k + 1, :] * pltpu.roll(ext, n_ext - (FFN_CONV - 1 - k), 0)[:tm]
        o_ref[...] = acc.astype(BF16)

    return pl.pallas_call(
        body, name="ffn_bwd_conv", grid=(nt,),
        in_specs=[pl.BlockSpec((tm, FFN_W), lambda i: (i, 0)), _next_halo_spec(tm, FFN_W, 0, t),
                  pl.BlockSpec((SUBLANES, FFN_W), lambda i: (0, 0))],
        out_specs=pl.BlockSpec((tm, FFN_W), lambda i: (i, 0)),
        out_shape=jax.ShapeDtypeStruct((t, FFN_W), BF16),
        compiler_params=_cparams("parallel"),
    )(du, du, cw)


def _adamw(parts, w, m, v, name, tr):
    r, cols = w.shape

    def body(p_ref, w_ref, m_ref, v_ref, g_ref, d_ref, mo_ref, vo_ref):
        g = p_ref[0].astype(F32)
        for s in range(1, N_DEV):
            g = g + p_ref[s].astype(F32)
        mm = ADAM_B1 * m_ref[...] + (1.0 - ADAM_B1) * g
        vv = ADAM_B2 * v_ref[...] + (1.0 - ADAM_B2) * (g * g)
        m_hat = mm / (1.0 - ADAM_B1 ** ADAM_STEP)
        v_hat = vv / (1.0 - ADAM_B2 ** ADAM_STEP)
        g_ref[...] = g
        d_ref[...] = -ADAM_LR * (m_hat / (jnp.sqrt(v_hat) + ADAM_EPS) + ADAM_WD * w_ref[...])
        mo_ref[...] = mm
        vo_ref[...] = vv

    assert r % tr == 0
    row = pl.BlockSpec((tr, cols), lambda i: (i, 0))
    return pl.pallas_call(
        body, name=name, grid=(r // tr,),
        in_specs=[pl.BlockSpec((N_DEV, tr, cols), lambda i: (0, i, 0)), row, row, row],
        out_specs=[row, row, row, row],
        out_shape=[jax.ShapeDtypeStruct((r, cols), F32)] * 4,
        compiler_params=_cparams("parallel"),
    )(parts, w, m, v)


def _mesh_pos():
    return lax.axis_index("x"), lax.axis_index("y"), lax.axis_index("c")


def _peer(pos, k):
    x, y, c = pos
    return (x ^ ((k >> 2) & 1), y ^ ((k >> 1) & 1), c ^ (k & 1))


def _flat_id(pos):
    return 4 * pos[0] + 2 * pos[1] + pos[2]


def _exchange_copies(srcs, dsts, scatter, send_sems, recv_sems, loc_sems):
    pos = _mesh_pos()
    me = _flat_id(pos)
    local, remote = [], []
    for j, (src, dst) in enumerate(zip(srcs, dsts)):
        local.append(pltpu.make_async_copy(src.at[me] if scatter[j] else src, dst.at[me], loc_sems.at[j]))
        for k in range(1, N_DEV):
            to = _peer(pos, k)
            remote.append(pltpu.make_async_remote_copy(
                src_ref=src.at[_flat_id(to)] if scatter[j] else src, dst_ref=dst.at[me],
                send_sem=send_sems.at[j, k - 1], recv_sem=recv_sems.at[j, k - 1],
                device_id=to, device_id_type=pl.DeviceIdType.MESH))
    return local, remote


def _exchange_shapes(arrays, scatter):
    return [jax.ShapeDtypeStruct(a.shape if s else (N_DEV,) + a.shape, a.dtype) for a, s in zip(arrays, scatter)]


def _exchange_sems(n):
    return [pltpu.SemaphoreType.DMA((n, N_DEV - 1)), pltpu.SemaphoreType.DMA((n, N_DEV - 1)), pltpu.SemaphoreType.DMA((n,))]


def _exchange(arrays, scatter, name):
    n = len(arrays)
    any_spec = pl.BlockSpec(memory_space=pl.ANY)

    def body(*refs):
        local, remote = _exchange_copies(refs[:n], refs[n:2 * n], scatter, *refs[2 * n:])
        for cp in local + remote:
            cp.start()
        for cp in remote:
            cp.wait()
        for cp in local:
            cp.wait()

    return pl.pallas_call(
        body, name=name, in_specs=[any_spec] * n, out_specs=[any_spec] * n,
        out_shape=_exchange_shapes(arrays, scatter), scratch_shapes=_exchange_sems(n),
    )(*arrays)


def _pad_rows(a, rows):
    return jnp.pad(a, ((0, rows - a.shape[0]),) + ((0, 0),) * (a.ndim - 1))


PACK_UNIT = SUBLANES * LANES


def _pack_lanes(parts, rows):
    out = []
    for a in parts:
        f = a.reshape(-1)
        out.append(jnp.pad(f, (0, (-f.shape[0]) % PACK_UNIT)).reshape(-1, LANES))
    packed = jnp.concatenate(out, axis=0)
    assert packed.shape[0] == rows, (packed.shape, rows)
    return packed


def _unpack_lanes(buf, shapes):
    out, r0 = [], 0
    for shp in shapes:
        n = math.prod(shp)
        nr = -(-n // PACK_UNIT) * SUBLANES
        out.append(buf[r0:r0 + nr].reshape(-1)[:n].reshape(shp))
        r0 += nr
    return out


def _col_shards(g):
    r, n = g.shape
    return g.reshape(r, N_DEV, n // N_DEV).transpose(1, 0, 2)


def _col_unshard(s):
    _, r, w = s.shape
    return s.transpose(1, 0, 2).reshape(r, N_DEV * w)


def _lane_rows(flat):
    n = flat.shape[1]
    return jnp.pad(flat, ((0, 0), (0, (-n) % PACK_UNIT))).reshape(N_DEV, -1, LANES)


SMALL_ROWS = 128
WS_ROWS = 32


def kernel(x, norm_mix_w, w_in, conv_qkv_w, a_log, dt_bias, gdn_norm_w, w_branch_a, w_branch_b, rel_bias, w_out, norm_ffn_w, w_up, conv_ffn_w, conv_ffn_b, w_down, norm_final_w, loss_target, m_norm_mix_w, m_w_in, m_conv_qkv_w, m_a_log, m_dt_bias, m_gdn_norm_w, m_w_branch_a, m_w_branch_b, m_rel_bias, m_w_out, m_norm_ffn_w, m_w_up, m_conv_ffn_w, m_conv_ffn_b, m_w_down, m_norm_final_w, v_norm_mix_w, v_w_in, v_conv_qkv_w, v_a_log, v_dt_bias, v_gdn_norm_w, v_w_branch_a, v_w_branch_b, v_rel_bias, v_w_out, v_norm_ffn_w, v_w_up, v_conv_ffn_w, v_conv_ffn_b, v_w_down, v_norm_final_w):
    big_w = (w_in, w_branch_a, w_branch_b, w_out, w_up, w_down, conv_qkv_w, conv_ffn_w)
    big_m = (m_w_in, m_w_branch_a, m_w_branch_b, m_w_out, m_w_up, m_w_down, m_conv_qkv_w, m_conv_ffn_w)
    big_v = (v_w_in, v_w_branch_a, v_w_branch_b, v_w_out, v_w_up, v_w_down, v_conv_qkv_w, v_conv_ffn_w)
    small_w = (norm_mix_w, a_log, dt_bias, gdn_norm_w, rel_bias, norm_ffn_w, conv_ffn_b, norm_final_w)
    small_m = (m_norm_mix_w, m_a_log, m_dt_bias, m_gdn_norm_w, m_rel_bias, m_norm_ffn_w, m_conv_ffn_b, m_norm_final_w)
    small_v = (v_norm_mix_w, v_a_log, v_dt_bias, v_gdn_norm_w, v_rel_bias, v_norm_ffn_w, v_conv_ffn_b, v_norm_final_w)

    ws = _pack_lanes(big_w[6:], WS_ROWS)
    g_in, g_ba, g_bb, g_out, g_up, g_down, gs = _exchange(
        [w[0].astype(BF16) for w in big_w[:6]] + [ws], (False,) * 7, "all_gather_weights")
    win, wba, wbb, wup = _col_unshard(g_in), _col_unshard(g_ba), _col_unshard(g_bb), _col_unshard(g_up)
    wout = g_out.reshape(D_MODEL, D_MODEL)
    wdown = g_down.reshape(D_FF, D_MODEL)
    gs = gs.reshape(N_DEV, -1)
    cqkv = gs[:, :GDN_CONV * 192].reshape(N_DEV, GDN_CONV, 192).transpose(1, 0, 2).reshape(GDN_CONV, 3 * KEY_A)
    cffn = gs[:, PACK_UNIT:PACK_UNIT + FFN_CONV * 704].reshape(N_DEV, FFN_CONV, 704).transpose(1, 0, 2).reshape(FFN_CONV, FFN_W)
    cffn = _pad_rows(cffn, SUBLANES)
    grads, small_g, grad_x = _local_step(x[0], loss_target[0], win, wba, wbb, wout, wup, wdown, cqkv, cffn, norm_mix_w,
                                         a_log, dt_bias, gdn_norm_w, rel_bias, norm_ffn_w, conv_ffn_b, norm_final_w)
    return _reduce_update(grads, small_g, grad_x, big_w, big_m, big_v, small_w, small_m, small_v)


def _local_step(xs, tgt, win, wba, wbb, wout, wup, wdown, cqkv, cffn, norm_mix_w, a_log, dt_bias, gdn_norm_w, rel_bias,
                norm_ffn_w, conv_ffn_b, norm_final_w):
    w_all = jnp.concatenate([win[:, 3592:5640], win[:, 0:2048], win[:, 2056:3592], win[:, 2048:2056],
                             jnp.zeros((D_MODEL, PROJ_W - D_IN), BF16)], axis=1)
    par = _pad_rows(jnp.pad(jnp.concatenate([a_log, dt_bias], axis=0), ((0, 0), (0, LANES - GDN_HEADS))), SUBLANES)
    table = jnp.pad(rel_bias[0], ((0, 0), (0, 3 * LANES - rel_bias.shape[-1]))).reshape(ATT_HEADS, 1, 3 * LANES)

    h1 = _rmsnorm_cast(xs, norm_mix_w, "norm_mix")
    proj = _mm_nn(h1, w_all, F32, "in_proj", 512, 1152, D_MODEL)
    qn, kn, va = _gdn_prep_fwd(proj, cqkv)
    oan, o_gdn, sprev, wst, ust, tst = _gdn_fwd(qn, kn, va, proj, par, gdn_norm_w)
    bias_q, bias_k = _att_bias(table)
    ob, lse, lse_t = _att_fwd(proj, bias_q)
    x2 = _merge_fwd(oan, ob, proj, xs, wba, wbb, wout)
    h2 = _rmsnorm_cast(x2, norm_ffn_w, "norm_ffn")
    up = _mm_nn(h2, wup, F32, "ffn_up", 512, 1408, D_MODEL)
    dx3, dx3b, act, tail_sums = _ffn_tail(up, cffn, conv_ffn_b, wdown, x2, tgt, norm_final_w.reshape(1, D_MODEL))

    g_wdown = _mm_tn(act, dx3b, "dw_down", 512)
    du, ffn_sums = _ffn_bwd_act(dx3b, wdown.T, up, cffn, conv_ffn_b)
    dup = _ffn_bwd_conv(du, cffn)
    g_wup = _mm_tn(h2, dup, "dw_up", 1408)
    dh2 = _mm_nn(dup, wup.T, F32, "ffn_up_bwd", 512, D_MODEL, 1408)
    dx2, dx2b, nffn_sums = _rms_bwd(dh2, x2, norm_ffn_w, dx3, "norm_ffn_bwd")
    dga, dgb, d_oan, d_ob, mixb, dya, dyb = _merge_bwd(dx2b, oan, ob, proj, wba, wbb, wout.T, wba.T, wbb.T)
    g_wout = _mm_tn(mixb, dx2b, "dw_out", 512)
    g_wba = _mm_tn(oan, dya, "dw_branch_a", 512)
    g_wbb = _mm_tn(ob, dyb, "dw_branch_b", 512)
    dqb, dlt_t, slabs = _att_dq(proj, bias_q, lse, d_ob)
    dkb, dvb = _att_dkv(proj, bias_k, lse_t, dlt_t, d_ob)
    g_rel = _relbias_grad(slabs)[:, 0, :rel_bias.shape[-1]]
    dqn, dkn, dva, dz, dbd, gdn_sums = _gdn_bwd(qn, kn, va, proj, par, gdn_norm_w, o_gdn, sprev, wst, ust, tst, d_oan)
    dqa, dka, dvaa, cq_sums = _gdn_prep_bwd(proj, cqkv, dqn, dkn, dva)
    dproj = jnp.concatenate([dga, dgb, dqa, dka, dvaa, dz, dqb, dkb, dvb, dbd.astype(BF16)], axis=1)
    g_wall = _mm_tn(h1, dproj, "dw_in", 1152)
    dh1 = _mm_nn(dproj, w_all.T, F32, "in_proj_bwd", 512, D_MODEL, 1152)
    grad_x, _, nmix_sums = _rms_bwd(dh1, xs, norm_mix_w, dx2, "norm_mix_bwd")

    g_win = jnp.concatenate([g_wall[:, 2048:4096], g_wall[:, 5632:5640], g_wall[:, 4096:5632], g_wall[:, 0:2048]], axis=1)
    small_g = (nmix_sums[0:1], gdn_sums[0:1, :GDN_HEADS], gdn_sums[1:2, :GDN_HEADS], gdn_sums[2:3], g_rel,
               nffn_sums[0:1], ffn_sums[FFN_CONV:FFN_CONV + 1], tail_sums[0:1], tail_sums[1:2, 0:1])
    return (g_win, g_wba, g_wbb, g_wout, g_wup, g_wdown, cq_sums[:GDN_CONV], ffn_sums[:FFN_CONV]), small_g, grad_x


def _reduce_update(grads, small_g, grad_x, big_w, big_m, big_v, small_w, small_m, small_v):
    g_win, g_wba, g_wbb, g_wout, g_wup, g_wdown, g_cqkv, g_cffn = grads
    send = [_col_shards(g_win).astype(BF16), _col_shards(g_wba).astype(BF16), _col_shards(g_wbb).astype(BF16),
            g_wout.reshape(N_DEV, -1, D_MODEL).astype(BF16), _col_shards(g_wup).astype(BF16),
            g_wdown.reshape(N_DEV, -1, D_MODEL).astype(BF16),
            jnp.concatenate([_lane_rows(_col_shards(g_cqkv).reshape(N_DEV, -1)),
                             _lane_rows(_col_shards(g_cffn).reshape(N_DEV, -1))], axis=1),
            _pack_lanes(small_g, SMALL_ROWS)]
    recv = _exchange(send, (True,) * 7 + (False,), "exchange_grads")

    res = {}
    for i, (nm, tr) in enumerate((("w_in", 128), ("w_branch_a", KEY_A), ("w_branch_b", WIDTH_B), ("w_out", 128),
                                  ("w_up", 128), ("w_down", 176))):
        res[nm] = [o[None] for o in _adamw(recv[i], big_w[i][0], big_m[i][0], big_v[i][0], "adamw_" + nm, tr)]
    conv = _adamw(recv[6], _pack_lanes(big_w[6:], WS_ROWS), _pack_lanes(big_m[6:], WS_ROWS), _pack_lanes(big_v[6:], WS_ROWS),
                  "adamw_conv", WS_ROWS)
    conv = [_unpack_lanes(o, [w.shape for w in big_w[6:]]) for o in conv]
    res["conv_qkv_w"] = [o[0] for o in conv]
    res["conv_ffn_w"] = [o[1] for o in conv]
    small_shapes = [w.shape for w in small_w]
    zero = jnp.zeros((1,), F32)
    small = _adamw(recv[7], _pack_lanes(small_w + (zero,), SMALL_ROWS), _pack_lanes(small_m + (zero,), SMALL_ROWS),
                   _pack_lanes(small_v + (zero,), SMALL_ROWS), "adamw_replicated", SMALL_ROWS)
    small = [_unpack_lanes(o, small_shapes + [()]) for o in small]
    loss = small[0][-1]
    for j, nm in enumerate(("norm_mix_w", "a_log", "dt_bias", "gdn_norm_w", "rel_bias", "norm_ffn_w", "conv_ffn_b",
                            "norm_final_w")):
        res[nm] = [o[j] for o in small]

    names = ("norm_mix_w", "w_in", "conv_qkv_w", "a_log", "dt_bias", "gdn_norm_w", "w_branch_a", "w_branch_b", "rel_bias",
             "w_out", "norm_ffn_w", "w_up", "conv_ffn_w", "conv_ffn_b", "w_down", "norm_final_w")
    outs = [res[n][kind] for kind in range(4) for n in names]
    return (loss, grad_x[None], *outs)
```

```python
import functools
import math

import numpy as np
import jax
import jax.numpy as jnp
from jax import lax
from jax.experimental import pallas as pl
from jax.experimental.pallas import tpu as pltpu

F32, BF16 = jnp.float32, jnp.bfloat16
HIGHEST = lax.Precision.HIGHEST

N_DEV = 8
D_MODEL = 1024
CHUNK = 64
EPS = 1e-6
GDN_HEADS, GDN_DK = 4, 128
KEY_A = GDN_HEADS * GDN_DK
GDN_CONV = 4
ATT_HEADS, ATT_DH = 8, 64
WIDTH_B = ATT_HEADS * ATT_DH
ATT_BAND = 9
REL_CLIP = 128
D_FF = 2816
FFN_CONV = 3
D_IN = 5640
ADAM_LR, ADAM_B1, ADAM_B2, ADAM_EPS, ADAM_WD, ADAM_STEP = 0.001, 0.9, 0.999, 1e-08, 0.01, 10

LANES = 128
SUBLANES = 8
NEG = -1e30

PROJ_W = 5760
PB = 512
CB_GA, CB_GB = 0, 1
CB_QA, CB_KA, CB_VA, CB_ZA, CB_QB, CB_KB, CB_VB = 4, 5, 6, 7, 8, 9, 10
CB_BD = 44

ATT_QB = 256
ATT_KW = 768
ATT_VEC = 1024


def _dot(a, b, precision=None):
    return jnp.dot(a, b, preferred_element_type=F32, precision=precision)


def _dot_nt(a, b, precision=None):
    return lax.dot_general(a, b, (((1,), (1,)), ((), ())), preferred_element_type=F32, precision=precision)


def _dot_tn(a, b):
    return lax.dot_general(a, b, (((0,), (0,)), ((), ())), preferred_element_type=F32)


def _sigmoid(x):
    return 1.0 / (1.0 + jnp.exp(-x))


def _softplus(x):
    return jnp.maximum(x, 0.0) + jnp.log(1.0 + jnp.exp(-jnp.abs(x)))


def _cparams(*sem):
    return pltpu.CompilerParams(dimension_semantics=tuple(sem))


def _rmsnorm_cast(x, w, name, tm=512):
    t, d = x.shape

    def body(x_ref, w_ref, o_ref):
        xv = x_ref[...]
        r = lax.rsqrt(jnp.mean(xv * xv, axis=-1, keepdims=True) + EPS)
        o_ref[...] = (xv * r * w_ref[...]).astype(BF16)

    return pl.pallas_call(
        body, name=name, grid=(t // tm,),
        in_specs=[pl.BlockSpec((tm, d), lambda i: (i, 0)), pl.BlockSpec((1, d), lambda i: (0, 0))],
        out_specs=pl.BlockSpec((tm, d), lambda i: (i, 0)),
        out_shape=jax.ShapeDtypeStruct((t, d), BF16),
        compiler_params=_cparams("parallel"),
    )(x, w)


def _mm_nn(a, b, out_dtype, name, tm, tn, tk, carry=((), ())):
    m, k = a.shape
    _, n = b.shape
    nk = k // tk
    assert m % tm == 0 and n % tn == 0 and k % tk == 0
    arrays, scatter = carry
    nx = len(arrays)
    gm, gn = m // tm, n // tn

    def body(*refs):
        a_ref, b_ref = refs[:2]
        srcs = refs[2:2 + nx]
        o_ref = refs[2 + nx]
        dsts = refs[3 + nx:3 + 2 * nx]
        rest = refs[3 + 2 * nx:]
        i, j, kk = pl.program_id(0), pl.program_id(1), pl.program_id(2)
        if nx:
            local, remote = _exchange_copies(srcs, dsts, scatter, *rest[-3:])

            @pl.when((i == 0) & (j == 0) & (kk == 0))
            def _():
                for cp in local + remote:
                    cp.start()

        if nk == 1:
            o_ref[...] = _dot(a_ref[...], b_ref[...]).astype(out_dtype)
        else:
            acc_ref = rest[0]

            @pl.when(kk == 0)
            def _():
                acc_ref[...] = jnp.zeros_like(acc_ref)

            acc_ref[...] += _dot(a_ref[...], b_ref[...])

            @pl.when(kk == nk - 1)
            def _():
                o_ref[...] = acc_ref[...].astype(out_dtype)

        if nx:
            @pl.when((i == gm - 1) & (j == gn - 1) & (kk == nk - 1))
            def _():
                for cp in remote + local:
                    cp.wait()

    any_spec = pl.BlockSpec(memory_space=pl.ANY)
    scratch = ([pltpu.VMEM((tm, tn), F32)] if nk > 1 else []) + (_exchange_sems(nx) if nx else [])
    out = pl.pallas_call(
        body, name=name, grid=(gm, gn, nk),
        in_specs=[pl.BlockSpec((tm, tk), lambda i, j, kk: (i, kk)),
                  pl.BlockSpec((tk, tn), lambda i, j, kk: (kk, j))] + [any_spec] * nx,
        out_specs=[pl.BlockSpec((tm, tn), lambda i, j, kk: (i, j))] + [any_spec] * nx,
        out_shape=[jax.ShapeDtypeStruct((m, n), out_dtype)] + _exchange_shapes(arrays, scatter),
        scratch_shapes=scratch,
        compiler_params=_cparams(*(("arbitrary",) * 3 if nx else ("parallel", "parallel", "arbitrary"))),
    )(a, b, *arrays)
    return out if nx else out[0]


def _mm_tn(a, b, name, tn, tk=512):
    t, m = a.shape
    _, n = b.shape
    assert t % tk == 0 and n % tn == 0

    def body(a_ref, b_ref, o_ref):
        @pl.when(pl.program_id(1) == 0)
        def _():
            o_ref[...] = jnp.zeros_like(o_ref)

        o_ref[...] += _dot_tn(a_ref[...], b_ref[...])

    return pl.pallas_call(
        body, name=name, grid=(n // tn, t // tk),
        in_specs=[pl.BlockSpec((tk, m), lambda j, s: (s, 0)),
                  pl.BlockSpec((tk, tn), lambda j, s: (s, j))],
        out_specs=pl.BlockSpec((m, tn), lambda j, s: (0, j)),
        out_shape=jax.ShapeDtypeStruct((m, n), F32),
        compiler_params=_cparams("parallel", "arbitrary"),
    )(a, b)


def _rms_bwd(dh, x, w, dres, name, tm=512):
    t, d = x.shape

    def body(dh_ref, x_ref, w_ref, dres_ref, dx_ref, dxb_ref, dw_ref):
        @pl.when(pl.program_id(0) == 0)
        def _():
            dw_ref[...] = jnp.zeros_like(dw_ref)

        xv = x_ref[...]
        r = lax.rsqrt(jnp.mean(xv * xv, axis=-1, keepdims=True) + EPS)
        xh = xv * r
        dhv = dh_ref[...]
        dw_ref[0:1, :] += jnp.sum(dhv * xh, axis=0, keepdims=True)
        dxh = dhv * w_ref[...]
        dx = dres_ref[...] + r * (dxh - xh * jnp.mean(dxh * xh, axis=-1, keepdims=True))
        dx_ref[...] = dx
        dxb_ref[...] = dx.astype(BF16)

    row = pl.BlockSpec((tm, d), lambda i: (i, 0))
    return pl.pallas_call(
        body, name=name, grid=(t // tm,),
        in_specs=[row, row, pl.BlockSpec((1, d), lambda i: (0, 0)), row],
        out_specs=[row, row, pl.BlockSpec((SUBLANES, d), lambda i: (0, 0))],
        out_shape=[jax.ShapeDtypeStruct((t, d), F32), jax.ShapeDtypeStruct((t, d), BF16),
                   jax.ShapeDtypeStruct((SUBLANES, d), F32)],
        compiler_params=_cparams("arbitrary"),
    )(dh, x, w, dres)


def _rel_index(dist):
    return np.clip(dist, -REL_CLIP, REL_CLIP) + REL_CLIP


def _bias_onehots():
    tw = 3 * LANES
    m = np.arange(ATT_VEC)
    dq = np.where(m <= ATT_KW, 512 - m, 512 - (m - ATT_VEC))
    dk = np.where(m < ATT_KW, m, m - ATT_VEC)
    ohq = np.zeros((tw, ATT_VEC), np.float32)
    ohk = np.zeros((tw, ATT_VEC), np.float32)
    ohq[_rel_index(dq), m] = 1.0
    ohk[_rel_index(dk), m] = 1.0
    return ohq, ohk


def _att_bias(table_pad):
    ohq, ohk = _bias_onehots()
    nslab = ATT_QB // SUBLANES

    def body(t_ref, ohq_ref, ohk_ref, bq_ref, bk_ref):
        tv = jnp.broadcast_to(t_ref[...], (SUBLANES, 3 * LANES))
        row = lax.broadcasted_iota(jnp.int32, (ATT_QB, ATT_KW), 0) // CHUNK
        col = lax.broadcasted_iota(jnp.int32, (ATT_QB, ATT_KW), 1) // CHUNK
        band = (col >= row) & (col <= row + ATT_BAND - 1)
        for oh_ref, out_ref in ((ohq_ref, bq_ref), (ohk_ref, bk_ref)):
            vec = _dot(tv, oh_ref[...], HIGHEST)[0:1, :]
            slab = jnp.concatenate([vec if b == 0 else pltpu.roll(vec, b, 1) for b in range(SUBLANES)], axis=0)
            rows = [slab if a == 0 else pltpu.roll(slab, SUBLANES * a, 1) for a in range(nslab)]
            full = jnp.concatenate(rows, axis=0)[:, :ATT_KW]
            out_ref[...] = jnp.where(band, full, NEG)

    h = table_pad.shape[0]
    oh_spec = pl.BlockSpec((3 * LANES, ATT_VEC), lambda i: (0, 0))
    out_spec = pl.BlockSpec((None, ATT_QB, ATT_KW), lambda i: (i, 0, 0))
    return pl.pallas_call(
        body, name="att_bias", grid=(h,),
        in_specs=[pl.BlockSpec((None, 1, 3 * LANES), lambda i: (i, 0, 0)), oh_spec, oh_spec],
        out_specs=[out_spec, out_spec],
        out_shape=[jax.ShapeDtypeStruct((h, ATT_QB, ATT_KW), F32)] * 2,
        compiler_params=_cparams("parallel"),
    )(table_pad, jnp.asarray(ohq), jnp.asarray(ohk))


def _head_masks():
    lane = lax.broadcasted_iota(jnp.int32, (1, LANES), 1)
    return [lane < ATT_DH, lane >= ATT_DH]


def _att_fwd(proj, bias_q):
    t = proj.shape[0]
    nb = t // ATT_QB
    scale = ATT_DH ** -0.5

    def body(q_ref, k0_ref, k1_ref, k2_ref, v0_ref, v1_ref, v2_ref, b_ref, o_ref, lse_ref, lset_ref):
        i = pl.program_id(0)
        q = q_ref[...].astype(BF16)
        kk = jnp.concatenate([k0_ref[...], k1_ref[...], k2_ref[...]], axis=0).astype(BF16)
        vv = jnp.concatenate([v0_ref[...], v1_ref[...], v2_ref[...]], axis=0).astype(BF16)
        kpos = lax.broadcasted_iota(jnp.int32, (1, ATT_KW), 1) + (i - 2) * ATT_QB
        valid = kpos >= 0
        lane = lax.broadcasted_iota(jnp.int32, (1, LANES), 1)
        masks = _head_masks()
        lse_cols = jnp.zeros((ATT_QB, LANES), F32)
        for p in range(ATT_HEADS // 2):
            cs = slice(p * LANES, (p + 1) * LANES)
            qt, kt, vt = q[:, cs], kk[:, cs], vv[:, cs]
            acc = jnp.zeros((ATT_QB, LANES), F32)
            for sub in range(2):
                h = 2 * p + sub
                s = _dot_nt(jnp.where(masks[sub], qt, 0), kt) * scale + b_ref[h]
                s = jnp.where(valid, s, NEG)
                mx = jnp.max(s, axis=-1, keepdims=True)
                e = jnp.exp(s - mx)
                l = jnp.sum(e, axis=-1, keepdims=True)
                pn = (e / l).astype(BF16)
                acc = acc + _dot(pn, jnp.where(masks[sub], vt, 0))
                lse_cols = lse_cols + jnp.where(lane == h, mx + jnp.log(l), 0.0)
            o_ref[:, cs] = acc.astype(BF16)
        lse_ref[...] = lse_cols
        lset_ref[...] = lse_cols.T[0:SUBLANES, :]

    def kv_spec(off, cb):
        return pl.BlockSpec((ATT_QB, PB), lambda i: (jnp.maximum(i + off, 0), cb))

    return pl.pallas_call(
        body, name="att_fwd", grid=(nb,),
        in_specs=[pl.BlockSpec((ATT_QB, PB), lambda i: (i, CB_QB)),
                  kv_spec(-2, CB_KB), kv_spec(-1, CB_KB), kv_spec(0, CB_KB),
                  kv_spec(-2, CB_VB), kv_spec(-1, CB_VB), kv_spec(0, CB_VB),
                  pl.BlockSpec((ATT_HEADS, ATT_QB, ATT_KW), lambda i: (0, 0, 0))],
        out_specs=[pl.BlockSpec((ATT_QB, WIDTH_B), lambda i: (i, 0)),
                   pl.BlockSpec((ATT_QB, LANES), lambda i: (i, 0)),
                   pl.BlockSpec((SUBLANES, ATT_QB), lambda i: (0, i))],
        out_shape=[jax.ShapeDtypeStruct((t, WIDTH_B), BF16), jax.ShapeDtypeStruct((t, LANES), F32),
                   jax.ShapeDtypeStruct((SUBLANES, t), F32)],
        compiler_params=_cparams("parallel"),
    )(proj, proj, proj, proj, proj, proj, proj, bias_q)


def _att_dq(proj, bias_q, lse, d_ob):
    t = proj.shape[0]
    nb = t // ATT_QB
    scale = ATT_DH ** -0.5
    nslab = ATT_QB // SUBLANES

    def body(q_ref, k0_ref, k1_ref, k2_ref, v0_ref, v1_ref, v2_ref, b_ref, lse_ref, do_ref,
             dq_ref, dlt_ref, slab_ref):
        i = pl.program_id(0)

        @pl.when(i == 0)
        def _():
            slab_ref[...] = jnp.zeros_like(slab_ref)

        q = q_ref[...].astype(BF16)
        kk = jnp.concatenate([k0_ref[...], k1_ref[...], k2_ref[...]], axis=0).astype(BF16)
        vv = jnp.concatenate([v0_ref[...], v1_ref[...], v2_ref[...]], axis=0).astype(BF16)
        do = do_ref[...].astype(BF16)
        kpos = lax.broadcasted_iota(jnp.int32, (1, ATT_KW), 1) + (i - 2) * ATT_QB
        valid = kpos >= 0
        lane = lax.broadcasted_iota(jnp.int32, (1, LANES), 1)
        masks = _head_masks()
        lse_all = lse_ref[...]
        dlt_cols = jnp.zeros((ATT_QB, LANES), F32)
        zpad = jnp.zeros((SUBLANES, ATT_VEC - ATT_KW), F32)
        for p in range(ATT_HEADS // 2):
            cs = slice(p * LANES, (p + 1) * LANES)
            qt, kt, vt, dot_ = q[:, cs], kk[:, cs], vv[:, cs], do[:, cs]
            acc = jnp.zeros((ATT_QB, LANES), F32)
            for sub in range(2):
                h = 2 * p + sub
                s = _dot_nt(jnp.where(masks[sub], qt, 0), kt) * scale + b_ref[h]
                s = jnp.where(valid, s, NEG)
                pr = jnp.exp(s - lse_all[:, h:h + 1])
                dp = _dot_nt(jnp.where(masks[sub], dot_, 0), vt)
                dl = jnp.sum(pr * dp, axis=-1, keepdims=True)
                ds = pr * (dp - dl)
                acc = acc + _dot((ds * scale).astype(BF16), jnp.where(masks[sub], kt, 0))
                dlt_cols = dlt_cols + jnp.where(lane == h, dl, 0.0)
                sl = jnp.zeros((SUBLANES, ATT_VEC), F32)
                for a in range(nslab):
                    piece = jnp.concatenate([ds[a * SUBLANES:(a + 1) * SUBLANES, :], zpad], axis=1)
                    sl = sl + (piece if a == 0 else pltpu.roll(piece, ATT_VEC - SUBLANES * a, 1))
                slab_ref[h] += sl
            dq_ref[:, cs] = acc.astype(BF16)
        dlt_ref[...] = dlt_cols.T[0:SUBLANES, :]

    def kv_spec(off, cb):
        return pl.BlockSpec((ATT_QB, PB), lambda i: (jnp.maximum(i + off, 0), cb))

    return pl.pallas_call(
        body, name="att_dq", grid=(nb,),
        in_specs=[pl.BlockSpec((ATT_QB, PB), lambda i: (i, CB_QB)),
                  kv_spec(-2, CB_KB), kv_spec(-1, CB_KB), kv_spec(0, CB_KB),
                  kv_spec(-2, CB_VB), kv_spec(-1, CB_VB), kv_spec(0, CB_VB),
                  pl.BlockSpec((ATT_HEADS, ATT_QB, ATT_KW), lambda i: (0, 0, 0)),
                  pl.BlockSpec((ATT_QB, LANES), lambda i: (i, 0)),
                  pl.BlockSpec((ATT_QB, WIDTH_B), lambda i: (i, 0))],
        out_specs=[pl.BlockSpec((ATT_QB, WIDTH_B), lambda i: (i, 0)),
                   pl.BlockSpec((SUBLANES, ATT_QB), lambda i: (0, i)),
                   pl.BlockSpec((ATT_HEADS, SUBLANES, ATT_VEC), lambda i: (0, 0, 0))],
        out_shape=[jax.ShapeDtypeStruct((t, WIDTH_B), BF16), jax.ShapeDtypeStruct((SUBLANES, t), F32),
                   jax.ShapeDtypeStruct((ATT_HEADS, SUBLANES, ATT_VEC), F32)],
        compiler_params=_cparams("arbitrary"),
    )(proj, proj, proj, proj, proj, proj, proj, bias_q, lse, d_ob)


def _att_dkv(proj, bias_k, lse_t, dlt_t, d_ob):
    t = proj.shape[0]
    nb = t // ATT_QB
    scale = ATT_DH ** -0.5

    def body(k_ref, v_ref, q0_ref, q1_ref, q2_ref, d0_ref, d1_ref, d2_ref, l0_ref, l1_ref, l2_ref,
             e0_ref, e1_ref, e2_ref, b_ref, dk_ref, dv_ref):
        i = pl.program_id(0)
        k = k_ref[...].astype(BF16)
        v = v_ref[...].astype(BF16)
        qq = jnp.concatenate([q0_ref[...], q1_ref[...], q2_ref[...]], axis=0).astype(BF16)
        do = jnp.concatenate([d0_ref[...], d1_ref[...], d2_ref[...]], axis=0).astype(BF16)
        lse = jnp.concatenate([l0_ref[...], l1_ref[...], l2_ref[...]], axis=1)
        dlt = jnp.concatenate([e0_ref[...], e1_ref[...], e2_ref[...]], axis=1)
        qpos = lax.broadcasted_iota(jnp.int32, (1, ATT_KW), 1) + i * ATT_QB
        valid = qpos < t
        masks = _head_masks()
        for p in range(ATT_HEADS // 2):
            cs = slice(p * LANES, (p + 1) * LANES)
            kt, vt, qt, dot_ = k[:, cs], v[:, cs], qq[:, cs], do[:, cs]
            acc_k = jnp.zeros((ATT_QB, LANES), F32)
            acc_v = jnp.zeros((ATT_QB, LANES), F32)
            for sub in range(2):
                h = 2 * p + sub
                st = _dot_nt(jnp.where(masks[sub], kt, 0), qt) * scale + b_ref[h]
                st = jnp.where(valid, st, NEG)
                pt = jnp.exp(st - lse[h:h + 1, :])
                dot_m = jnp.where(masks[sub], dot_, 0)
                acc_v = acc_v + _dot(pt.astype(BF16), dot_m)
                dpt = _dot_nt(jnp.where(masks[sub], vt, 0), dot_)
                dst = pt * (dpt - dlt[h:h + 1, :])
                acc_k = acc_k + _dot((dst * scale).astype(BF16), jnp.where(masks[sub], qt, 0))
            dk_ref[:, cs] = acc_k.astype(BF16)
            dv_ref[:, cs] = acc_v.astype(BF16)

    def q_spec(off, cb):
        return pl.BlockSpec((ATT_QB, PB), lambda i: (jnp.minimum(i + off, nb - 1), cb))

    def d_spec(off):
        return pl.BlockSpec((ATT_QB, WIDTH_B), lambda i: (jnp.minimum(i + off, nb - 1), 0))

    def r_spec(off):
        return pl.BlockSpec((SUBLANES, ATT_QB), lambda i: (0, jnp.minimum(i + off, nb - 1)))

    row = pl.BlockSpec((ATT_QB, WIDTH_B), lambda i: (i, 0))
    return pl.pallas_call(
        body, name="att_dkv", grid=(nb,),
        in_specs=[pl.BlockSpec((ATT_QB, PB), lambda i: (i, CB_KB)), pl.BlockSpec((ATT_QB, PB), lambda i: (i, CB_VB)),
                  q_spec(0, CB_QB), q_spec(1, CB_QB), q_spec(2, CB_QB),
                  d_spec(0), d_spec(1), d_spec(2), r_spec(0), r_spec(1), r_spec(2),
                  r_spec(0), r_spec(1), r_spec(2),
                  pl.BlockSpec((ATT_HEADS, ATT_QB, ATT_KW), lambda i: (0, 0, 0))],
        out_specs=[row, row],
        out_shape=[jax.ShapeDtypeStruct((t, WIDTH_B), BF16)] * 2,
        compiler_params=_cparams("parallel"),
    )(proj, proj, proj, proj, proj, d_ob, d_ob, d_ob, lse_t, lse_t, lse_t, dlt_t, dlt_t, dlt_t, bias_k)


def _relbias_grad(slabs):
    ohq, _ = _bias_onehots()

    def body(s_ref, oh_ref, o_ref):
        sv = s_ref[...]
        vec = sv[0:1, :]
        for b in range(1, SUBLANES):
            vec = vec + pltpu.roll(sv[b:b + 1, :], ATT_VEC - b, 1)
        o_ref[...] = _dot_nt(jnp.broadcast_to(vec, (SUBLANES, ATT_VEC)), oh_ref[...], HIGHEST)[0:1, :]

    h = slabs.shape[0]
    return pl.pallas_call(
        body, name="att_dbias", grid=(h,),
        in_specs=[pl.BlockSpec((None, SUBLANES, ATT_VEC), lambda i: (i, 0, 0)),
                  pl.BlockSpec((3 * LANES, ATT_VEC), lambda i: (0, 0))],
        out_specs=pl.BlockSpec((None, 1, 3 * LANES), lambda i: (i, 0, 0)),
        out_shape=jax.ShapeDtypeStruct((h, 1, 3 * LANES), F32),
        compiler_params=_cparams("parallel"),
    )(slabs, jnp.asarray(ohq))


GDN_TM = 512
GDN_CB = 4
HALO = SUBLANES


def _conv_taps(ext, width, lead, n):
    return [(ext if k == width - 1 else pltpu.roll(ext, width - 1 - k, 0))[lead:lead + n] for k in range(width)]


def _prev_halo_spec(tm, width, cb):
    return pl.BlockSpec((HALO, width), lambda i: (jnp.maximum(i * (tm // HALO) - 1, 0), cb))


def _next_halo_spec(tm, width, cb, t):
    return pl.BlockSpec((HALO, width), lambda i: (jnp.minimum((i + 1) * (tm // HALO), t // HALO - 1), cb))


def _gdn_prep_fwd(proj, conv_w):
    t = proj.shape[0]
    tm = GDN_TM

    def body(q_ref, k_ref, v_ref, hq_ref, hk_ref, hv_ref, w_ref, qn_ref, kn_ref, vo_ref):
        first = pl.program_id(0) == 0
        for idx, (x_ref, h_ref, o_ref) in enumerate(((q_ref, hq_ref, qn_ref), (k_ref, hk_ref, kn_ref),
                                                      (v_ref, hv_ref, vo_ref))):
            halo = jnp.where(first, 0.0, h_ref[...])
            ext = jnp.concatenate([halo, x_ref[...]], axis=0)
            w = w_ref[:, idx * KEY_A:(idx + 1) * KEY_A]
            taps = _conv_taps(ext, GDN_CONV, HALO, tm)
            y = sum(w[k:k + 1, :] * taps[k] for k in range(GDN_CONV))
            a = y * _sigmoid(y)
            if idx < 2:
                for h in range(GDN_HEADS):
                    cs = slice(h * GDN_DK, (h + 1) * GDN_DK)
                    seg = a[:, cs]
                    o_ref[:, cs] = seg * lax.rsqrt(jnp.sum(seg * seg, axis=-1, keepdims=True) + EPS)
            else:
                o_ref[...] = a

    row = pl.BlockSpec((tm, KEY_A), lambda i: (i, 0))
    return pl.pallas_call(
        body, name="gdn_prep_fwd", grid=(t // tm,),
        in_specs=[pl.BlockSpec((tm, PB), lambda i: (i, CB_QA)), pl.BlockSpec((tm, PB), lambda i: (i, CB_KA)),
                  pl.BlockSpec((tm, PB), lambda i: (i, CB_VA)),
                  _prev_halo_spec(tm, PB, CB_QA), _prev_halo_spec(tm, PB, CB_KA), _prev_halo_spec(tm, PB, CB_VA),
                  pl.BlockSpec((GDN_CONV, 3 * KEY_A), lambda i: (0, 0))],
        out_specs=[row, row, row],
        out_shape=[jax.ShapeDtypeStruct((t, KEY_A), F32)] * 3,
        compiler_params=_cparams("parallel"),
    )(proj, proj, proj, proj, proj, proj, conv_w)


def _gdn_prep_bwd(proj, conv_w, dqn, dkn, dv):
    t = proj.shape[0]
    tm = GDN_TM
    nt = t // tm
    n_ext = tm + HALO

    def body(q_ref, k_ref, v_ref, pq_ref, pk_ref, pv_ref, nq_ref, nk_ref, nv_ref,
             dq_ref, dk_ref, dv_ref, ndq_ref, ndk_ref, ndv_ref, w_ref, oq_ref, ok_ref, ov_ref, dw_ref):
        i = pl.program_id(0)
        first, last = i == 0, i == nt - 1

        @pl.when(first)
        def _():
            dw_ref[...] = jnp.zeros_like(dw_ref)

        groups = ((q_ref, pq_ref, nq_ref, dq_ref, ndq_ref, oq_ref), (k_ref, pk_ref, nk_ref, dk_ref, ndk_ref, ok_ref),
                  (v_ref, pv_ref, nv_ref, dv_ref, ndv_ref, ov_ref))
        for idx, (x_ref, p_ref, n_ref, d_ref, nd_ref, o_ref) in enumerate(groups):
            cs_all = slice(idx * KEY_A, (idx + 1) * KEY_A)
            ext = jnp.concatenate([jnp.where(first, 0.0, p_ref[...]), x_ref[...], jnp.where(last, 0.0, n_ref[...])], axis=0)
            w = w_ref[:, cs_all]
            taps = _conv_taps(ext, GDN_CONV, HALO, n_ext)
            y = sum(w[k:k + 1, :] * taps[k] for k in range(GDN_CONV))
            sg = _sigmoid(y)
            a = y * sg
            dup = jnp.concatenate([d_ref[...], jnp.where(last, 0.0, nd_ref[...])], axis=0)
            if idx < 2:
                segs = []
                for h in range(GDN_HEADS):
                    cs = slice(h * GDN_DK, (h + 1) * GDN_DK)
                    seg = a[:, cs]
                    r = lax.rsqrt(jnp.sum(seg * seg, axis=-1, keepdims=True) + EPS)
                    nrm = seg * r
                    dn = dup[:, cs]
                    segs.append(r * (dn - nrm * jnp.sum(dn * nrm, axis=-1, keepdims=True)))
                da = jnp.concatenate(segs, axis=1)
            else:
                da = dup
            dy = da * sg * (1.0 + y * (1.0 - sg))
            dx = sum(w[k:k + 1, :] * (dy if k == GDN_CONV - 1 else pltpu.roll(dy, n_ext - (GDN_CONV - 1 - k), 0))[:tm]
                     for k in range(GDN_CONV))
            o_ref[...] = dx.astype(BF16)
            for k in range(GDN_CONV):
                dw_ref[k:k + 1, cs_all] += jnp.sum(dy[:tm] * taps[k][:tm], axis=0, keepdims=True)

    row = pl.BlockSpec((tm, KEY_A), lambda i: (i, 0))
    nrow = _next_halo_spec(tm, KEY_A, 0, t)
    return pl.pallas_call(
        body, name="gdn_prep_bwd", grid=(nt,),
        in_specs=[pl.BlockSpec((tm, PB), lambda i: (i, CB_QA)), pl.BlockSpec((tm, PB), lambda i: (i, CB_KA)),
                  pl.BlockSpec((tm, PB), lambda i: (i, CB_VA)),
                  _prev_halo_spec(tm, PB, CB_QA), _prev_halo_spec(tm, PB, CB_KA), _prev_halo_spec(tm, PB, CB_VA),
                  _next_halo_spec(tm, PB, CB_QA, t), _next_halo_spec(tm, PB, CB_KA, t), _next_halo_spec(tm, PB, CB_VA, t),
                  row, row, row, nrow, nrow, nrow,
                  pl.BlockSpec((GDN_CONV, 3 * KEY_A), lambda i: (0, 0))],
        out_specs=[row, row, row, pl.BlockSpec((SUBLANES, 3 * KEY_A), lambda i: (0, 0))],
        out_shape=[jax.ShapeDtypeStruct((t, KEY_A), BF16)] * 3 + [jax.ShapeDtypeStruct((SUBLANES, 3 * KEY_A), F32)],
        compiler_params=_cparams("arbitrary"),
    )(proj, proj, proj, proj, proj, proj, proj, proj, proj, dqn, dkn, dv, dqn, dkn, dv, conv_w)


class _Pair(dict):
    __getattr__ = dict.__getitem__
    __setattr__ = dict.__setitem__


def _gdn_terms(bd, par, kn_ref, qn_ref):
    c = CHUNK
    ii = lax.broadcasted_iota(jnp.int32, (c, c), 0)
    jj = lax.broadcasted_iota(jnp.int32, (c, c), 1)
    strict, incl = ii > jj, ii >= jj
    ltri = incl.astype(F32)
    ts = []
    for cc in range(GDN_CB):
        for h in range(GDN_HEADS):
            t = _Pair(cc=cc, h=h, rows=slice(cc * c, (cc + 1) * c), cs=slice(h * GDN_DK, (h + 1) * GDN_DK),
                      strict=strict, incl=incl)
            t.beta = _sigmoid(bd[t.rows, h:h + 1])
            t.ea = jnp.exp(par[0:1, h:h + 1])
            t.sp_arg = bd[t.rows, GDN_HEADS + h:GDN_HEADS + h + 1] + par[1:2, h:h + 1]
            t.g = -t.ea * _softplus(t.sp_arg)
            t.k = kn_ref[t.rows, t.cs]
            t.q = qn_ref[t.rows, t.cs] * (GDN_DK ** -0.5)
            t.kb, t.qb = t.k.astype(BF16), t.q.astype(BF16)
            ts.append(t)
    for t in ts:
        t.gb = _dot(ltri, jnp.broadcast_to(t.g, (c, GDN_DK)), HIGHEST)
    for t in ts:
        t.kk = _dot_nt(t.kb, t.kb)
        t.qk = _dot_nt(t.qb, t.kb)
    for t in ts:
        gc = t.gb[:, :c]
        diff = gc - gc.T
        t.dec_s = jnp.exp(jnp.where(strict, diff, NEG))
        t.dec_i = jnp.exp(jnp.where(incl, diff, NEG))
        t.gam = jnp.exp(t.gb)
        glast = t.gb[c - 1:c, :]
        t.e_rest = jnp.exp(glast - t.gb)
        t.gl = jnp.exp(glast)
        t.p = t.qk * t.dec_i
    return ts


def _gdn_fwd(qn, kn, v, proj, par, gnw):
    t = qn.shape[0]
    c = CHUNK
    nc = t // c
    r_ = GDN_CB * c

    def body(qn_ref, kn_ref, v_ref, bd_ref, z_ref, par_ref, gnw_ref,
             oan_ref, o_ref, sp_ref, w_ref, u_ref, tm_ref, s_ref):
        @pl.when(pl.program_id(0) == 0)
        def _():
            s_ref[...] = jnp.zeros_like(s_ref)

        bd, par, gnw_v = bd_ref[...], par_ref[...], gnw_ref[...]
        eye = (lax.broadcasted_iota(jnp.int32, (c, c), 0) == lax.broadcasted_iota(jnp.int32, (c, c), 1)).astype(F32)
        ts = _gdn_terms(bd, par, kn_ref, qn_ref)
        for t in ts:
            t.vv = v_ref[t.rows, t.cs]
            t.x = -(t.beta * t.kk * t.dec_s)
            t.tinv = eye + t.x
        for _ in range(5):
            for t in ts:
                t.x = _dot(t.x, t.x, HIGHEST)
            for t in ts:
                t.tinv = t.tinv + _dot(t.tinv, t.x, HIGHEST)
        for t in ts:
            t.wm = _dot(t.tinv, (t.beta * t.gam) * t.k, HIGHEST)
            t.uv = _dot(t.tinv, t.beta * t.vv, HIGHEST)
        for t in ts:
            w_ref[t.rows, t.cs] = t.wm
            tm_ref[t.cc, t.h] = t.tinv
            t.wb = t.wm.astype(BF16)
            t.qgb = (t.q * t.gam).astype(BF16)
            t.kdb = (t.k * t.e_rest).astype(BF16)
            t.pb = t.p.astype(BF16)
        for cc in range(GDN_CB):
            tc = [t for t in ts if t.cc == cc]
            for t in tc:
                t.sh = s_ref[t.h]
                t.sb = t.sh.astype(BF16)
                sp_ref[cc, t.h] = t.sh
            for t in tc:
                t.ws = _dot(t.wb, t.sb)
                t.qs = _dot(t.qgb, t.sb)
            for t in tc:
                t.u = t.uv - t.ws
                t.ub = t.u.astype(BF16)
            for t in tc:
                t.pu = _dot(t.pb, t.ub)
                t.ku = _dot_tn(t.kdb, t.ub)
            for t in tc:
                t.o = t.qs + t.pu
                s_ref[t.h] = t.gl * t.sh + t.ku
                u_ref[t.rows, t.cs] = t.u
                o_ref[t.rows, t.cs] = t.o
        for t in ts:
            zz = z_ref[t.rows, t.cs]
            rr = lax.rsqrt(jnp.mean(t.o * t.o, axis=-1, keepdims=True) + EPS)
            oan_ref[t.rows, t.cs] = ((t.o * rr) * gnw_v * (zz * _sigmoid(zz))).astype(BF16)

    row = pl.BlockSpec((r_, KEY_A), lambda i: (i, 0))
    return pl.pallas_call(
        body, name="gdn_fwd", grid=(nc // GDN_CB,),
        in_specs=[row, row, row, pl.BlockSpec((r_, LANES), lambda i: (i, CB_BD)),
                  pl.BlockSpec((r_, PB), lambda i: (i, CB_ZA)),
                  pl.BlockSpec((SUBLANES, LANES), lambda i: (0, 0)), pl.BlockSpec((1, GDN_DK), lambda i: (0, 0))],
        out_specs=[row, row, pl.BlockSpec((GDN_CB, GDN_HEADS, GDN_DK, GDN_DK), lambda i: (i, 0, 0, 0)),
                   row, row, pl.BlockSpec((GDN_CB, GDN_HEADS, c, c), lambda i: (i, 0, 0, 0))],
        out_shape=[jax.ShapeDtypeStruct((t, KEY_A), BF16), jax.ShapeDtypeStruct((t, KEY_A), F32),
                   jax.ShapeDtypeStruct((nc, GDN_HEADS, GDN_DK, GDN_DK), F32),
                   jax.ShapeDtypeStruct((t, KEY_A), F32), jax.ShapeDtypeStruct((t, KEY_A), F32),
                   jax.ShapeDtypeStruct((nc, GDN_HEADS, c, c), F32)],
        scratch_shapes=[pltpu.VMEM((GDN_HEADS, GDN_DK, GDN_DK), F32)],
        compiler_params=_cparams("arbitrary"),
    )(qn, kn, v, proj, proj, par, gnw)


def _gdn_bwd(qn, kn, v, proj, par, gnw, o, sprev, wst, ust, tst, d_oan):
    t = qn.shape[0]
    c = CHUNK
    nc = t // c
    nb = nc // GDN_CB
    r_ = GDN_CB * c

    def body(qn_ref, kn_ref, v_ref, bd_ref, z_ref, par_ref, gnw_ref, o_ref, sp_ref, w_ref, u_ref, tm_ref, do_ref,
             dqn_ref, dkn_ref, dv_ref, dz_ref, dbd_ref, acc_ref, ds_ref):
        @pl.when(pl.program_id(0) == 0)
        def _():
            ds_ref[...] = jnp.zeros_like(ds_ref)
            acc_ref[...] = jnp.zeros_like(acc_ref)

        bd, par, gnw_v = bd_ref[...], par_ref[...], gnw_ref[...]
        lane = lax.broadcasted_iota(jnp.int32, (1, LANES), 1)
        rix = lax.broadcasted_iota(jnp.int32, (c, 1), 0)
        ii = lax.broadcasted_iota(jnp.int32, (c, c), 0)
        jj = lax.broadcasted_iota(jnp.int32, (c, c), 1)
        upper = (jj >= ii).astype(F32)
        acc_a = jnp.zeros((1, LANES), F32)
        acc_d = jnp.zeros((1, LANES), F32)
        acc_g = jnp.zeros((1, LANES), F32)
        ts = _gdn_terms(bd, par, kn_ref, qn_ref)
        for t in ts:
            t.vv = v_ref[t.rows, t.cs]
            t.sh = sp_ref[t.cc, t.h]
            t.sb = t.sh.astype(BF16)
            t.wm, t.u, t.tinv = w_ref[t.rows, t.cs], u_ref[t.rows, t.cs], tm_ref[t.cc, t.h]
            t.wb, t.ub = t.wm.astype(BF16), t.u.astype(BF16)
            ov, zz, dout = o_ref[t.rows, t.cs], z_ref[t.rows, t.cs], do_ref[t.rows, t.cs]
            sg = _sigmoid(zz)
            sil = zz * sg
            rr = lax.rsqrt(jnp.mean(ov * ov, axis=-1, keepdims=True) + EPS)
            on = ov * rr
            dz_ref[t.rows, t.cs] = (dout * on * gnw_v * (sg * (1.0 + zz * (1.0 - sg)))).astype(BF16)
            acc_g = acc_g + jnp.sum(dout * on * sil, axis=0, keepdims=True)
            don = dout * gnw_v * sil
            t.dob = (rr * (don - on * jnp.mean(don * on, axis=-1, keepdims=True))).astype(BF16)
            t.qg = t.q * t.gam
            t.kd = t.k * t.e_rest
            t.qgb, t.kdb = t.qg.astype(BF16), t.kd.astype(BF16)
            t.ptb = t.p.T.astype(BF16)
        for cc in reversed(range(GDN_CB)):
            tc = [t for t in ts if t.cc == cc]
            for t in tc:
                t.dsn = ds_ref[t.h]
                t.dsnb = t.dsn.astype(BF16)
            for t in tc:
                t.du = _dot(t.ptb, t.dob) + _dot(t.kdb, t.dsnb)
                t.dkd = _dot_nt(t.ub, t.dsnb)
                t.dgl = jnp.sum(jnp.sum(t.dsn * t.sh, axis=1, keepdims=True), axis=0, keepdims=True)
            for t in tc:
                t.dub = t.du.astype(BF16)
            for t in tc:
                ds_ref[t.h] = t.gl * t.dsn + _dot_tn(t.qgb, t.dob) - _dot_tn(t.wb, t.dub)
        for t in ts:
            t.dqg = _dot_nt(t.dob, t.sb)
            t.dp = _dot_nt(t.dob, t.ub)
            t.dwm = -_dot_nt(t.dub, t.sb)
            t.uv = t.u + _dot(t.wb, t.sb)
        for t in ts:
            tt = t.tinv.T
            t.dbk = _dot(tt, t.dwm, HIGHEST)
            t.dbv = _dot(tt, t.du, HIGHEST)
        for t in ts:
            d_a = -(_dot_nt(t.dbk.astype(BF16), t.wb) + _dot_nt(t.dbv.astype(BF16), t.uv.astype(BF16)))
            t.d_a = jnp.where(t.strict, d_a, 0.0)
        for t in ts:
            t.dkk = t.d_a * t.beta * t.dec_s
            t.dqk = t.dp * t.dec_i
            t.dqkb = t.dqk.astype(BF16)
        for t in ts:
            t.dq = _dot(t.dqkb, t.kb) + t.dqg * t.gam
            t.dk = (t.dbk * (t.beta * t.gam) + _dot_tn(t.dqkb, t.qb) + _dot((t.dkk + t.dkk.T).astype(BF16), t.kb)
                    + t.dkd * t.e_rest)
        for t in ts:
            dbeta = (jnp.sum(t.d_a * t.kk * t.dec_s, axis=-1, keepdims=True)
                     + jnp.sum(t.dbk * t.k * t.gam, axis=-1, keepdims=True) + jnp.sum(t.dbv * t.vv, axis=-1, keepdims=True))
            t.dbl = dbeta * t.beta * (1.0 - t.beta)
            dv_ref[t.rows, t.cs] = t.dbv * t.beta
            bk = (t.beta * t.gam) * t.k
            zc = jnp.sum(t.dkd * t.kd, axis=-1, keepdims=True)
            xs = t.dkk * t.kk + t.dp * t.p
            dgc = (jnp.sum(xs, axis=-1, keepdims=True) - jnp.sum(xs.T, axis=-1, keepdims=True)
                   + jnp.sum(t.dbk * bk, axis=-1, keepdims=True) + jnp.sum(t.dqg * t.qg, axis=-1, keepdims=True) - zc)
            dglast = jnp.sum(zc, axis=0, keepdims=True) + t.dgl * t.gl[:, 0:1]
            t.dgc = dgc + jnp.where(rix == c - 1, dglast, 0.0)
        for t in ts:
            t.dg = _dot(upper, jnp.broadcast_to(t.dgc, (c, GDN_DK)), HIGHEST)[:, 0:1]
        dbd_tiles = [jnp.zeros((c, LANES), F32) for _ in range(GDN_CB)]
        for t in ts:
            ddl = t.dg * (-t.ea) * _sigmoid(t.sp_arg)
            acc_a = acc_a + jnp.where(lane == t.h, jnp.sum(t.dg * t.g, axis=0, keepdims=True), 0.0)
            acc_d = acc_d + jnp.where(lane == t.h, jnp.sum(ddl, axis=0, keepdims=True), 0.0)
            dbd_tiles[t.cc] = (dbd_tiles[t.cc] + jnp.where(lane == t.h, t.dbl, 0.0)
                               + jnp.where(lane == GDN_HEADS + t.h, ddl, 0.0))
            dqn_ref[t.rows, t.cs] = t.dq * (GDN_DK ** -0.5)
            dkn_ref[t.rows, t.cs] = t.dk
        for cc in range(GDN_CB):
            dbd_ref[cc * c:(cc + 1) * c, :] = dbd_tiles[cc]
        acc_ref[0:1, :] += acc_a
        acc_ref[1:2, :] += acc_d
        acc_ref[2:3, :] += acc_g

    def rev(i):
        return nb - 1 - i

    row = pl.BlockSpec((r_, KEY_A), lambda i: (rev(i), 0))
    st = pl.BlockSpec((GDN_CB, GDN_HEADS, GDN_DK, GDN_DK), lambda i: (rev(i), 0, 0, 0))
    tt_spec = pl.BlockSpec((GDN_CB, GDN_HEADS, c, c), lambda i: (rev(i), 0, 0, 0))
    return pl.pallas_call(
        body, name="gdn_bwd", grid=(nb,),
        in_specs=[row, row, row, pl.BlockSpec((r_, LANES), lambda i: (rev(i), CB_BD)),
                  pl.BlockSpec((r_, PB), lambda i: (rev(i), CB_ZA)),
                  pl.BlockSpec((SUBLANES, LANES), lambda i: (0, 0)), pl.BlockSpec((1, GDN_DK), lambda i: (0, 0)),
                  row, st, row, row, tt_spec, row],
        out_specs=[row, row, row, row, pl.BlockSpec((r_, LANES), lambda i: (rev(i), 0)),
                   pl.BlockSpec((SUBLANES, LANES), lambda i: (0, 0))],
        out_shape=[jax.ShapeDtypeStruct((t, KEY_A), F32)] * 3 + [jax.ShapeDtypeStruct((t, KEY_A), BF16),
                   jax.ShapeDtypeStruct((t, LANES), F32), jax.ShapeDtypeStruct((SUBLANES, LANES), F32)],
        scratch_shapes=[pltpu.VMEM((GDN_HEADS, GDN_DK, GDN_DK), F32)],
        compiler_params=_cparams("arbitrary"),
    )(qn, kn, v, proj, proj, par, gnw, o, sprev, wst, ust, tst, d_oan)


def _merge_fwd(oan, ob, proj, x, wba, wbb, wout, tm=512):
    t = x.shape[0]

    def body(oa_ref, ob_ref, ga_ref, gb_ref, x_ref, wba_ref, wbb_ref, wout_ref, x2_ref):
        ya = _dot(oa_ref[...], wba_ref[...])
        yb = _dot(ob_ref[...], wbb_ref[...])
        mix = _sigmoid(ga_ref[...]) * ya + _sigmoid(gb_ref[...]) * yb
        x2_ref[...] = x_ref[...] + _dot(mix.astype(BF16), wout_ref[...])

    half = pl.BlockSpec((tm, KEY_A), lambda i: (i, 0))
    row = pl.BlockSpec((tm, D_MODEL), lambda i: (i, 0))
    wsmall = pl.BlockSpec((KEY_A, D_MODEL), lambda i: (0, 0))
    return pl.pallas_call(
        body, name="merge_fwd", grid=(t // tm,),
        in_specs=[half, half, pl.BlockSpec((tm, D_MODEL), lambda i: (i, CB_GA)),
                  pl.BlockSpec((tm, D_MODEL), lambda i: (i, CB_GB)), row, wsmall, wsmall,
                  pl.BlockSpec((D_MODEL, D_MODEL), lambda i: (0, 0))],
        out_specs=row,
        out_shape=jax.ShapeDtypeStruct((t, D_MODEL), F32),
        compiler_params=_cparams("parallel"),
    )(oan, ob, proj, proj, x, wba, wbb, wout)


def _merge_bwd(dx2b, oan, ob, proj, wba, wbb, wout_t, wba_t, wbb_t, tm=512):
    t = dx2b.shape[0]

    def body(dx_ref, oa_ref, ob_ref, ga_ref, gb_ref, wba_ref, wbb_ref, woutt_ref, wbat_ref, wbbt_ref,
             dga_ref, dgb_ref, doa_ref, dob_ref, mix_ref, dya_ref, dyb_ref):
        dmix = _dot(dx_ref[...], woutt_ref[...])
        ya = _dot(oa_ref[...], wba_ref[...])
        yb = _dot(ob_ref[...], wbb_ref[...])
        sa, sb = _sigmoid(ga_ref[...]), _sigmoid(gb_ref[...])
        mix_ref[...] = (sa * ya + sb * yb).astype(BF16)
        dga_ref[...] = (dmix * ya * sa * (1.0 - sa)).astype(BF16)
        dgb_ref[...] = (dmix * yb * sb * (1.0 - sb)).astype(BF16)
        dya = (dmix * sa).astype(BF16)
        dyb = (dmix * sb).astype(BF16)
        dya_ref[...] = dya
        dyb_ref[...] = dyb
        doa_ref[...] = _dot(dya, wbat_ref[...])
        dob_ref[...] = _dot(dyb, wbbt_ref[...])

    half = pl.BlockSpec((tm, KEY_A), lambda i: (i, 0))
    row = pl.BlockSpec((tm, D_MODEL), lambda i: (i, 0))
    wsmall = pl.BlockSpec((KEY_A, D_MODEL), lambda i: (0, 0))
    wsmall_t = pl.BlockSpec((D_MODEL, KEY_A), lambda i: (0, 0))
    big = jax.ShapeDtypeStruct((t, D_MODEL), BF16)
    return pl.pallas_call(
        body, name="merge_bwd", grid=(t // tm,),
        in_specs=[row, half, half, pl.BlockSpec((tm, D_MODEL), lambda i: (i, CB_GA)),
                  pl.BlockSpec((tm, D_MODEL), lambda i: (i, CB_GB)), wsmall, wsmall,
                  pl.BlockSpec((D_MODEL, D_MODEL), lambda i: (0, 0)), wsmall_t, wsmall_t],
        out_specs=[row, row, half, half, row, row, row],
        out_shape=[big, big, jax.ShapeDtypeStruct((t, KEY_A), F32), jax.ShapeDtypeStruct((t, KEY_A), F32), big, big, big],
        compiler_params=_cparams("parallel"),
    )(dx2b, oan, ob, proj, proj, wba, wbb, wout_t, wba_t, wbb_t)


FFN_TM = 128
FFN_W = 2 * D_FF


def _ffn_conv(up_ref, halo_ref, cw_ref, cb_ref, first):
    ext = jnp.concatenate([jnp.where(first, 0.0, halo_ref[...]), up_ref[...]], axis=0)
    taps = _conv_taps(ext, FFN_CONV, HALO, FFN_TM)
    cw = cw_ref[...]
    u = sum(cw[k:k + 1, :] * taps[k] for k in range(FFN_CONV)) + cb_ref[...]
    return u, taps


def _ffn_tail(up, cw, cb, wdown, x2, tgt, w3):
    t = x2.shape[0]
    tm = FFN_TM

    def body(up_ref, halo_ref, cw_ref, cb_ref, wd_ref, x2_ref, tgt_ref, w3_ref, dx_ref, dxb_ref, act_ref, acc_ref):
        first = pl.program_id(0) == 0

        @pl.when(first)
        def _():
            acc_ref[...] = jnp.zeros_like(acc_ref)

        u, _ = _ffn_conv(up_ref, halo_ref, cw_ref, cb_ref, first)
        gate, upp = u[:, :D_FF], u[:, D_FF:]
        act = (gate * _sigmoid(gate) * upp).astype(BF16)
        act_ref[...] = act
        x3 = x2_ref[...] + _dot(act, wd_ref[...])
        r = lax.rsqrt(jnp.mean(x3 * x3, axis=-1, keepdims=True) + EPS)
        xh = x3 * r
        w3v = w3_ref[...]
        err = xh * w3v - tgt_ref[...]
        loss = 0.5 * jnp.sum(jnp.mean(err * err, axis=-1, keepdims=True), axis=0, keepdims=True)
        dy = err * (1.0 / D_MODEL)
        acc_ref[0:1, :] += jnp.sum(dy * xh, axis=0, keepdims=True)
        acc_ref[1:2, :] += jnp.broadcast_to(loss, (1, D_MODEL))
        dxh = dy * w3v
        dx = r * (dxh - xh * jnp.mean(dxh * xh, axis=-1, keepdims=True))
        dx_ref[...] = dx
        dxb_ref[...] = dx.astype(BF16)

    row = pl.BlockSpec((tm, D_MODEL), lambda i: (i, 0))
    vec = pl.BlockSpec((1, D_MODEL), lambda i: (0, 0))
    return pl.pallas_call(
        body, name="ffn_tail", grid=(t // tm,),
        in_specs=[pl.BlockSpec((tm, FFN_W), lambda i: (i, 0)), _prev_halo_spec(tm, FFN_W, 0),
                  pl.BlockSpec((SUBLANES, FFN_W), lambda i: (0, 0)), pl.BlockSpec((1, FFN_W), lambda i: (0, 0)),
                  pl.BlockSpec((D_FF, D_MODEL), lambda i: (0, 0)), row, row, vec],
        out_specs=[row, row, pl.BlockSpec((tm, D_FF), lambda i: (i, 0)), pl.BlockSpec((SUBLANES, D_MODEL), lambda i: (0, 0))],
        out_shape=[jax.ShapeDtypeStruct((t, D_MODEL), F32), jax.ShapeDtypeStruct((t, D_MODEL), BF16),
                   jax.ShapeDtypeStruct((t, D_FF), BF16), jax.ShapeDtypeStruct((SUBLANES, D_MODEL), F32)],
        compiler_params=_cparams("arbitrary"),
    )(up, up, cw, cb, wdown, x2, tgt, w3)


def _ffn_bwd_act(dx3b, wdown_t, up, cw, cb):
    t = up.shape[0]
    tm = FFN_TM

    def body(dx_ref, wdt_ref, up_ref, halo_ref, cw_ref, cb_ref, du_ref, acc_ref):
        first = pl.program_id(0) == 0

        @pl.when(first)
        def _():
            acc_ref[...] = jnp.zeros_like(acc_ref)

        dact = _dot(dx_ref[...], wdt_ref[...])
        u, taps = _ffn_conv(up_ref, halo_ref, cw_ref, cb_ref, first)
        gate, upp = u[:, :D_FF], u[:, D_FF:]
        sg = _sigmoid(gate)
        du = jnp.concatenate([dact * upp * (sg * (1.0 + gate * (1.0 - sg))), dact * (gate * sg)], axis=1)
        du_ref[...] = du
        for k in range(FFN_CONV):
            acc_ref[k:k + 1, :] += jnp.sum(du * taps[k], axis=0, keepdims=True)
        acc_ref[FFN_CONV:FFN_CONV + 1, :] += jnp.sum(du, axis=0, keepdims=True)

    return pl.pallas_call(
        body, name="ffn_bwd_act", grid=(t // tm,),
        in_specs=[pl.BlockSpec((tm, D_MODEL), lambda i: (i, 0)), pl.BlockSpec((D_MODEL, D_FF), lambda i: (0, 0)),
                  pl.BlockSpec((tm, FFN_W), lambda i: (i, 0)), _prev_halo_spec(tm, FFN_W, 0),
                  pl.BlockSpec((SUBLANES, FFN_W), lambda i: (0, 0)), pl.BlockSpec((1, FFN_W), lambda i: (0, 0))],
        out_specs=[pl.BlockSpec((tm, FFN_W), lambda i: (i, 0)), pl.BlockSpec((SUBLANES, FFN_W), lambda i: (0, 0))],
        out_shape=[jax.ShapeDtypeStruct((t, FFN_W), F32), jax.ShapeDtypeStruct((SUBLANES, FFN_W), F32)],
        compiler_params=_cparams("arbitrary"),
    )(dx3b, wdown_t, up, up, cw, cb)


def _ffn_bwd_conv(du, cw):
    t = du.shape[0]
    tm = FFN_TM
    nt = t // tm
    n_ext = tm + HALO

    def body(du_ref, nxt_ref, cw_ref, o_ref):
        last = pl.program_id(0) == nt - 1
        ext = jnp.concatenate([du_ref[...], jnp.where(last, 0.0, nxt_ref[...])], axis=0)
        cw = cw_ref[...]
        acc = cw[FFN_CONV - 1:FFN_CONV, :] * ext[:tm]
        for k in range(FFN_CONV - 1):
            acc = acc + cw[k:k + 1, :] * pltpu.roll(ext, n_ext - (FFN_CONV - 1 - k), 0)[:tm]
        o_ref[...] = acc.astype(BF16)

    return pl.pallas_call(
        body, name="ffn_bwd_conv", grid=(nt,),
        in_specs=[pl.BlockSpec((tm, FFN_W), lambda i: (i, 0)), _next_halo_spec(tm, FFN_W, 0, t),
                  pl.BlockSpec((SUBLANES, FFN_W), lambda i: (0, 0))],
        out_specs=pl.BlockSpec((tm, FFN_W), lambda i: (i, 0)),
        out_shape=jax.ShapeDtypeStruct((t, FFN_W), BF16),
        compiler_params=_cparams("parallel"),
    )(du, du, cw)


def _adamw(parts, w, m, v, name, tr):
    r, cols = w.shape

    def body(p_ref, w_ref, m_ref, v_ref, g_ref, d_ref, mo_ref, vo_ref):
        g = p_ref[0].astype(F32)
        for s in range(1, N_DEV):
            g = g + p_ref[s].astype(F32)
        mm = ADAM_B1 * m_ref[...] + (1.0 - ADAM_B1) * g
        vv = ADAM_B2 * v_ref[...] + (1.0 - ADAM_B2) * (g * g)
        m_hat = mm / (1.0 - ADAM_B1 ** ADAM_STEP)
        v_hat = vv / (1.0 - ADAM_B2 ** ADAM_STEP)
        g_ref[...] = g
        d_ref[...] = -ADAM_LR * (m_hat / (jnp.sqrt(v_hat) + ADAM_EPS) + ADAM_WD * w_ref[...])
        mo_ref[...] = mm
        vo_ref[...] = vv

    assert r % tr == 0
    row = pl.BlockSpec((tr, cols), lambda i: (i, 0))
    return pl.pallas_call(
        body, name=name, grid=(r // tr,),
        in_specs=[pl.BlockSpec((N_DEV, tr, cols), lambda i: (0, i, 0)), row, row, row],
        out_specs=[row, row, row, row],
        out_shape=[jax.ShapeDtypeStruct((r, cols), F32)] * 4,
        compiler_params=_cparams("parallel"),
    )(parts, w, m, v)


def _mesh_pos():
    return lax.axis_index("x"), lax.axis_index("y"), lax.axis_index("c")


def _peer(pos, k):
    x, y, c = pos
    return (x ^ ((k >> 2) & 1), y ^ ((k >> 1) & 1), c ^ (k & 1))


def _flat_id(pos):
    return 4 * pos[0] + 2 * pos[1] + pos[2]


def _exchange_copies(srcs, dsts, scatter, send_sems, recv_sems, loc_sems):
    pos = _mesh_pos()
    me = _flat_id(pos)
    local, remote = [], []
    for j, (src, dst) in enumerate(zip(srcs, dsts)):
        local.append(pltpu.make_async_copy(src.at[me] if scatter[j] else src, dst.at[me], loc_sems.at[j]))
        for k in range(1, N_DEV):
            to = _peer(pos, k)
            remote.append(pltpu.make_async_remote_copy(
                src_ref=src.at[_flat_id(to)] if scatter[j] else src, dst_ref=dst.at[me],
                send_sem=send_sems.at[j, k - 1], recv_sem=recv_sems.at[j, k - 1],
                device_id=to, device_id_type=pl.DeviceIdType.MESH))
    return local, remote


def _exchange_shapes(arrays, scatter):
    return [jax.ShapeDtypeStruct(a.shape if s else (N_DEV,) + a.shape, a.dtype) for a, s in zip(arrays, scatter)]


def _exchange_sems(n):
    return [pltpu.SemaphoreType.DMA((n, N_DEV - 1)), pltpu.SemaphoreType.DMA((n, N_DEV - 1)), pltpu.SemaphoreType.DMA((n,))]


def _exchange(arrays, scatter, name):
    n = len(arrays)
    any_spec = pl.BlockSpec(memory_space=pl.ANY)

    def body(*refs):
        local, remote = _exchange_copies(refs[:n], refs[n:2 * n], scatter, *refs[2 * n:])
        for cp in local + remote:
            cp.start()
        for cp in remote:
            cp.wait()
        for cp in local:
            cp.wait()

    return pl.pallas_call(
        body, name=name, in_specs=[any_spec] * n, out_specs=[any_spec] * n,
        out_shape=_exchange_shapes(arrays, scatter), scratch_shapes=_exchange_sems(n),
    )(*arrays)


def _pad_rows(a, rows):
    return jnp.pad(a, ((0, rows - a.shape[0]),) + ((0, 0),) * (a.ndim - 1))


PACK_UNIT = SUBLANES * LANES


def _pack_lanes(parts, rows):
    out = []
    for a in parts:
        f = a.reshape(-1)
        out.append(jnp.pad(f, (0, (-f.shape[0]) % PACK_UNIT)).reshape(-1, LANES))
    packed = jnp.concatenate(out, axis=0)
    assert packed.shape[0] == rows, (packed.shape, rows)
    return packed


def _unpack_lanes(buf, shapes):
    out, r0 = [], 0
    for shp in shapes:
        n = math.prod(shp)
        nr = -(-n // PACK_UNIT) * SUBLANES
        out.append(buf[r0:r0 + nr].reshape(-1)[:n].reshape(shp))
        r0 += nr
    return out


def _col_shards(g):
    r, n = g.shape
    return g.reshape(r, N_DEV, n // N_DEV).transpose(1, 0, 2)


def _col_unshard(s):
    _, r, w = s.shape
    return s.transpose(1, 0, 2).reshape(r, N_DEV * w)


def _lane_rows(flat):
    n = flat.shape[1]
    return jnp.pad(flat, ((0, 0), (0, (-n) % PACK_UNIT))).reshape(N_DEV, -1, LANES)


SMALL_ROWS = 128
WS_ROWS = 32


def kernel(x, norm_mix_w, w_in, conv_qkv_w, a_log, dt_bias, gdn_norm_w, w_branch_a, w_branch_b, rel_bias, w_out, norm_ffn_w, w_up, conv_ffn_w, conv_ffn_b, w_down, norm_final_w, loss_target, m_norm_mix_w, m_w_in, m_conv_qkv_w, m_a_log, m_dt_bias, m_gdn_norm_w, m_w_branch_a, m_w_branch_b, m_rel_bias, m_w_out, m_norm_ffn_w, m_w_up, m_conv_ffn_w, m_conv_ffn_b, m_w_down, m_norm_final_w, v_norm_mix_w, v_w_in, v_conv_qkv_w, v_a_log, v_dt_bias, v_gdn_norm_w, v_w_branch_a, v_w_branch_b, v_rel_bias, v_w_out, v_norm_ffn_w, v_w_up, v_conv_ffn_w, v_conv_ffn_b, v_w_down, v_norm_final_w):
    big_w = (w_in, w_branch_a, w_branch_b, w_out, w_up, w_down, conv_qkv_w, conv_ffn_w)
    big_m = (m_w_in, m_w_branch_a, m_w_branch_b, m_w_out, m_w_up, m_w_down, m_conv_qkv_w, m_conv_ffn_w)
    big_v = (v_w_in, v_w_branch_a, v_w_branch_b, v_w_out, v_w_up, v_w_down, v_conv_qkv_w, v_conv_ffn_w)
    small_w = (norm_mix_w, a_log, dt_bias, gdn_norm_w, rel_bias, norm_ffn_w, conv_ffn_b, norm_final_w)
    small_m = (m_norm_mix_w, m_a_log, m_dt_bias, m_gdn_norm_w, m_rel_bias, m_norm_ffn_w, m_conv_ffn_b, m_norm_final_w)
    small_v = (v_norm_mix_w, v_a_log, v_dt_bias, v_gdn_norm_w, v_rel_bias, v_norm_ffn_w, v_conv_ffn_b, v_norm_final_w)

    xs, tgt = x[0], loss_target[0]
    ws = _pack_lanes(big_w[6:], WS_ROWS)
    g_in, gs = _exchange([w_in[0].astype(BF16), ws], (False, False), "all_gather_in")
    win = _col_unshard(g_in)
    gs = gs.reshape(N_DEV, -1)
    cqkv = gs[:, :GDN_CONV * 192].reshape(N_DEV, GDN_CONV, 192).transpose(1, 0, 2).reshape(GDN_CONV, 3 * KEY_A)
    cffn = gs[:, PACK_UNIT:PACK_UNIT + FFN_CONV * 704].reshape(N_DEV, FFN_CONV, 704).transpose(1, 0, 2).reshape(FFN_CONV, FFN_W)
    cffn = _pad_rows(cffn, SUBLANES)
    w_all = jnp.concatenate([win[:, 3592:5640], win[:, 0:2048], win[:, 2056:3592], win[:, 2048:2056],
                             jnp.zeros((D_MODEL, PROJ_W - D_IN), BF16)], axis=1)
    par = _pad_rows(jnp.pad(jnp.concatenate([a_log, dt_bias], axis=0), ((0, 0), (0, LANES - GDN_HEADS))), SUBLANES)
    table = jnp.pad(rel_bias[0], ((0, 0), (0, 3 * LANES - rel_bias.shape[-1]))).reshape(ATT_HEADS, 1, 3 * LANES)

    h1 = _rmsnorm_cast(xs, norm_mix_w, "norm_mix")
    proj, g_ba, g_bb, g_out, g_up, g_down = _mm_nn(
        h1, w_all, F32, "in_proj", 512, 1152, D_MODEL, carry=([w[0].astype(BF16) for w in big_w[1:6]], (False,) * 5))
    wba, wbb, wup = _col_unshard(g_ba), _col_unshard(g_bb), _col_unshard(g_up)
    wout = g_out.reshape(D_MODEL, D_MODEL)
    wdown = g_down.reshape(D_FF, D_MODEL)
    qn, kn, va = _gdn_prep_fwd(proj, cqkv)
    oan, o_gdn, sprev, wst, ust, tst = _gdn_fwd(qn, kn, va, proj, par, gdn_norm_w)
    bias_q, bias_k = _att_bias(table)
    ob, lse, lse_t = _att_fwd(proj, bias_q)
    x2 = _merge_fwd(oan, ob, proj, xs, wba, wbb, wout)
    h2 = _rmsnorm_cast(x2, norm_ffn_w, "norm_ffn")
    up = _mm_nn(h2, wup, F32, "ffn_up", 512, 1408, D_MODEL)
    dx3, dx3b, act, tail_sums = _ffn_tail(up, cffn, conv_ffn_b, wdown, x2, tgt, norm_final_w.reshape(1, D_MODEL))

    g_wdown = _mm_tn(act, dx3b, "dw_down", 512)
    du, ffn_sums = _ffn_bwd_act(dx3b, wdown.T, up, cffn, conv_ffn_b)
    dup = _ffn_bwd_conv(du, cffn)
    g_wup = _mm_tn(h2, dup, "dw_up", 1408)
    dh2, r_up, r_down = _mm_nn(dup, wup.T, F32, "ffn_up_bwd", 512, D_MODEL, 1408, carry=(
        [_col_shards(g_wup).astype(BF16), g_wdown.reshape(N_DEV, -1, D_MODEL).astype(BF16)], (True, True)))
    dx2, dx2b, nffn_sums = _rms_bwd(dh2, x2, norm_ffn_w, dx3, "norm_ffn_bwd")
    dga, dgb, d_oan, d_ob, mixb, dya, dyb = _merge_bwd(dx2b, oan, ob, proj, wba, wbb, wout.T, wba.T, wbb.T)
    g_wout = _mm_tn(mixb, dx2b, "dw_out", 512)
    g_wba = _mm_tn(oan, dya, "dw_branch_a", 512)
    g_wbb = _mm_tn(ob, dyb, "dw_branch_b", 512)
    dqb, dlt_t, slabs = _att_dq(proj, bias_q, lse, d_ob)
    dkb, dvb = _att_dkv(proj, bias_k, lse_t, dlt_t, d_ob)
    g_rel = _relbias_grad(slabs)[:, 0, :rel_bias.shape[-1]]
    dqn, dkn, dva, dz, dbd, gdn_sums = _gdn_bwd(qn, kn, va, proj, par, gdn_norm_w, o_gdn, sprev, wst, ust, tst, d_oan)
    dqa, dka, dvaa, cq_sums = _gdn_prep_bwd(proj, cqkv, dqn, dkn, dva)
    dproj = jnp.concatenate([dga, dgb, dqa, dka, dvaa, dz, dqb, dkb, dvb, dbd.astype(BF16)], axis=1)
    g_wall = _mm_tn(h1, dproj, "dw_in", 1152)
    g_win = jnp.concatenate([g_wall[:, 2048:4096], g_wall[:, 5632:5640], g_wall[:, 4096:5632], g_wall[:, 0:2048]], axis=1)
    g_conv = jnp.concatenate([_lane_rows(_col_shards(cq_sums[:GDN_CONV]).reshape(N_DEV, -1)),
                              _lane_rows(_col_shards(ffn_sums[:FFN_CONV]).reshape(N_DEV, -1))], axis=1)
    dh1, r_in, r_ba, r_bb, r_out, r_conv = _mm_nn(dproj, w_all.T, F32, "in_proj_bwd", 512, D_MODEL, 1152, carry=(
        [_col_shards(g_win).astype(BF16), _col_shards(g_wba).astype(BF16), _col_shards(g_wbb).astype(BF16),
         g_wout.reshape(N_DEV, -1, D_MODEL).astype(BF16), g_conv], (True,) * 5))
    grad_x, _, nmix_sums = _rms_bwd(dh1, xs, norm_mix_w, dx2, "norm_mix_bwd")

    small_g = (nmix_sums[0:1], gdn_sums[0:1, :GDN_HEADS], gdn_sums[1:2, :GDN_HEADS], gdn_sums[2:3], g_rel,
               nffn_sums[0:1], ffn_sums[FFN_CONV:FFN_CONV + 1], tail_sums[0:1], tail_sums[1:2, 0:1])
    r_small, = _exchange([_pack_lanes(small_g, SMALL_ROWS)], (False,), "all_gather_small_grads")
    recv = (r_in, r_ba, r_bb, r_out, r_up, r_down, r_conv, r_small)

    res = {}
    for i, (nm, tr) in enumerate((("w_in", 128), ("w_branch_a", KEY_A), ("w_branch_b", WIDTH_B), ("w_out", 128),
                                  ("w_up", 128), ("w_down", 176))):
        res[nm] = [o[None] for o in _adamw(recv[i], big_w[i][0], big_m[i][0], big_v[i][0], "adamw_" + nm, tr)]
    conv = _adamw(recv[6], _pack_lanes(big_w[6:], WS_ROWS), _pack_lanes(big_m[6:], WS_ROWS), _pack_lanes(big_v[6:], WS_ROWS),
                  "adamw_conv", WS_ROWS)
    conv = [_unpack_lanes(o, [w.shape for w in big_w[6:]]) for o in conv]
    res["conv_qkv_w"] = [o[0] for o in conv]
    res["conv_ffn_w"] = [o[1] for o in conv]
    small_shapes = [w.shape for w in small_w]
    zero = jnp.zeros((1,), F32)
    small = _adamw(recv[7], _pack_lanes(small_w + (zero,), SMALL_ROWS), _pack_lanes(small_m + (zero,), SMALL_ROWS),
                   _pack_lanes(small_v + (zero,), SMALL_ROWS), "adamw_replicated", SMALL_ROWS)
    small = [_unpack_lanes(o, small_shapes + [()]) for o in small]
    loss = small[0][-1]
    for j, nm in enumerate(("norm_mix_w", "a_log", "dt_bias", "gdn_norm_w", "rel_bias", "norm_ffn_w", "conv_ffn_b",
                            "norm_final_w")):
        res[nm] = [o[j] for o in small]

    names = ("norm_mix_w", "w_in", "conv_qkv_w", "a_log", "dt_bias", "gdn_norm_w", "w_branch_a", "w_branch_b", "rel_bias",
             "w_out", "norm_ffn_w", "w_up", "conv_ffn_w", "conv_ffn_b", "w_down", "norm_final_w")
    outs = [res[n][kind] for kind in range(4) for n in names]
    return (loss, grad_x[None], *outs)
```

```python
import functools
import math

import numpy as np
import jax
import jax.numpy as jnp
from jax import lax
from jax.experimental import pallas as pl
from jax.experimental.pallas import tpu as pltpu

F32, BF16 = jnp.float32, jnp.bfloat16
HIGHEST = lax.Precision.HIGHEST

N_DEV = 8
D_MODEL = 1024
CHUNK = 64
EPS = 1e-6
GDN_HEADS, GDN_DK = 4, 128
KEY_A = GDN_HEADS * GDN_DK
GDN_CONV = 4
ATT_HEADS, ATT_DH = 8, 64
WIDTH_B = ATT_HEADS * ATT_DH
ATT_BAND = 9
REL_CLIP = 128
D_FF = 2816
FFN_CONV = 3
D_IN = 5640
ADAM_LR, ADAM_B1, ADAM_B2, ADAM_EPS, ADAM_WD, ADAM_STEP = 0.001, 0.9, 0.999, 1e-08, 0.01, 10

LANES = 128
SUBLANES = 8
NEG = -1e30

PROJ_W = 5760
PB = 512
CB_GA, CB_GB = 0, 1
CB_QA, CB_KA, CB_VA, CB_ZA, CB_QB, CB_KB, CB_VB = 4, 5, 6, 7, 8, 9, 10
CB_BD = 44

ATT_QB = 256
ATT_KW = 768
ATT_VEC = 1024


def _dot(a, b, precision=None):
    return jnp.dot(a, b, preferred_element_type=F32, precision=precision)


def _dot_nt(a, b, precision=None):
    return lax.dot_general(a, b, (((1,), (1,)), ((), ())), preferred_element_type=F32, precision=precision)


def _dot_tn(a, b):
    return lax.dot_general(a, b, (((0,), (0,)), ((), ())), preferred_element_type=F32)


def _split(a):
    hi = a.astype(BF16)
    return hi, (a - hi.astype(F32)).astype(BF16)


def _dot3s(a, b):
    return _dot(a[0], b[0]) + (_dot(a[0], b[1]) + _dot(a[1], b[0]))


def _sigmoid(x):
    return 0.5 * jnp.tanh(0.5 * x) + 0.5


def _softplus(x):
    return jnp.maximum(x, 0.0) + jnp.log(1.0 + jnp.exp(-jnp.abs(x)))


def _cparams(*sem):
    return pltpu.CompilerParams(dimension_semantics=tuple(sem))


def _rmsnorm_cast(x, w, name, tm=512):
    t, d = x.shape

    def body(x_ref, w_ref, o_ref):
        xv = x_ref[...]
        r = lax.rsqrt(jnp.mean(xv * xv, axis=-1, keepdims=True) + EPS)
        o_ref[...] = (xv * r * w_ref[...]).astype(BF16)

    return pl.pallas_call(
        body, name=name, grid=(t // tm,),
        in_specs=[pl.BlockSpec((tm, d), lambda i: (i, 0)), pl.BlockSpec((1, d), lambda i: (0, 0))],
        out_specs=pl.BlockSpec((tm, d), lambda i: (i, 0)),
        out_shape=jax.ShapeDtypeStruct((t, d), BF16),
        compiler_params=_cparams("parallel"),
    )(x, w)


def _mm_nn(a, b, out_dtype, name, tm, tn, tk, carry=((), ())):
    m, k = a.shape
    _, n = b.shape
    nk = k // tk
    assert m % tm == 0 and n % tn == 0 and k % tk == 0
    arrays, scatter = carry
    nx = len(arrays)
    gm, gn = m // tm, n // tn

    def body(*refs):
        a_ref, b_ref = refs[:2]
        srcs = refs[2:2 + nx]
        o_ref = refs[2 + nx]
        dsts = refs[3 + nx:3 + 2 * nx]
        rest = refs[3 + 2 * nx:]
        i, j, kk = pl.program_id(0), pl.program_id(1), pl.program_id(2)
        if nx:
            local, remote = _exchange_copies(srcs, dsts, scatter, *rest[-3:])

            @pl.when((i == 0) & (j == 0) & (kk == 0))
            def _():
                for cp in local + remote:
                    cp.start()

        if nk == 1:
            o_ref[...] = _dot(a_ref[...], b_ref[...]).astype(out_dtype)
        else:
            acc_ref = rest[0]

            @pl.when(kk == 0)
            def _():
                acc_ref[...] = jnp.zeros_like(acc_ref)

            acc_ref[...] += _dot(a_ref[...], b_ref[...])

            @pl.when(kk == nk - 1)
            def _():
                o_ref[...] = acc_ref[...].astype(out_dtype)

        if nx:
            @pl.when((i == gm - 1) & (j == gn - 1) & (kk == nk - 1))
            def _():
                for cp in remote + local:
                    cp.wait()

    any_spec = pl.BlockSpec(memory_space=pl.ANY)
    scratch = ([pltpu.VMEM((tm, tn), F32)] if nk > 1 else []) + (_exchange_sems(nx) if nx else [])
    out = pl.pallas_call(
        body, name=name, grid=(gm, gn, nk),
        in_specs=[pl.BlockSpec((tm, tk), lambda i, j, kk: (i, kk)),
                  pl.BlockSpec((tk, tn), lambda i, j, kk: (kk, j))] + [any_spec] * nx,
        out_specs=[pl.BlockSpec((tm, tn), lambda i, j, kk: (i, j))] + [any_spec] * nx,
        out_shape=[jax.ShapeDtypeStruct((m, n), out_dtype)] + _exchange_shapes(arrays, scatter),
        scratch_shapes=scratch,
        compiler_params=_cparams(*(("arbitrary",) * 3 if nx else ("parallel", "parallel", "arbitrary"))),
    )(a, b, *arrays)
    return out if nx else out[0]


MM_TM = 1024


def _mm_tn(a, b, name, tn, tk=MM_TM):
    t, m = a.shape
    _, n = b.shape
    assert t % tk == 0 and n % tn == 0

    def body(a_ref, b_ref, o_ref):
        @pl.when(pl.program_id(1) == 0)
        def _():
            o_ref[...] = jnp.zeros_like(o_ref)

        o_ref[...] += _dot_tn(a_ref[...], b_ref[...])

    return pl.pallas_call(
        body, name=name, grid=(n // tn, t // tk),
        in_specs=[pl.BlockSpec((tk, m), lambda j, s: (s, 0)),
                  pl.BlockSpec((tk, tn), lambda j, s: (s, j))],
        out_specs=pl.BlockSpec((m, tn), lambda j, s: (0, j)),
        out_shape=jax.ShapeDtypeStruct((m, n), F32),
        compiler_params=_cparams("parallel", "arbitrary"),
    )(a, b)


def _rms_bwd(dh, x, w, dres, name, tm=512):
    t, d = x.shape

    def body(dh_ref, x_ref, w_ref, dres_ref, dx_ref, dxb_ref, dw_ref):
        @pl.when(pl.program_id(0) == 0)
        def _():
            dw_ref[...] = jnp.zeros_like(dw_ref)

        xv = x_ref[...]
        r = lax.rsqrt(jnp.mean(xv * xv, axis=-1, keepdims=True) + EPS)
        xh = xv * r
        dhv = dh_ref[...]
        dw_ref[0:1, :] += jnp.sum(dhv * xh, axis=0, keepdims=True)
        dxh = dhv * w_ref[...]
        dx = dres_ref[...] + r * (dxh - xh * jnp.mean(dxh * xh, axis=-1, keepdims=True))
        dx_ref[...] = dx
        dxb_ref[...] = dx.astype(BF16)

    row = pl.BlockSpec((tm, d), lambda i: (i, 0))
    return pl.pallas_call(
        body, name=name, grid=(t // tm,),
        in_specs=[row, row, pl.BlockSpec((1, d), lambda i: (0, 0)), row],
        out_specs=[row, row, pl.BlockSpec((SUBLANES, d), lambda i: (0, 0))],
        out_shape=[jax.ShapeDtypeStruct((t, d), F32), jax.ShapeDtypeStruct((t, d), BF16),
                   jax.ShapeDtypeStruct((SUBLANES, d), F32)],
        compiler_params=_cparams("arbitrary"),
    )(dh, x, w, dres)


def _rel_index(dist):
    return np.clip(dist, -REL_CLIP, REL_CLIP) + REL_CLIP


def _bias_onehots():
    tw = 3 * LANES
    m = np.arange(ATT_VEC)
    dq = np.where(m <= ATT_KW, 512 - m, 512 - (m - ATT_VEC))
    dk = np.where(m < ATT_KW, m, m - ATT_VEC)
    ohq = np.zeros((tw, ATT_VEC), np.float32)
    ohk = np.zeros((tw, ATT_VEC), np.float32)
    ohq[_rel_index(dq), m] = 1.0
    ohk[_rel_index(dk), m] = 1.0
    return ohq, ohk


def _att_bias(table_pad):
    ohq, ohk = _bias_onehots()
    nslab = ATT_QB // SUBLANES

    def body(t_ref, ohq_ref, ohk_ref, bq_ref, bk_ref):
        tv = jnp.broadcast_to(t_ref[...], (SUBLANES, 3 * LANES))
        row = lax.broadcasted_iota(jnp.int32, (ATT_QB, ATT_KW), 0) // CHUNK
        col = lax.broadcasted_iota(jnp.int32, (ATT_QB, ATT_KW), 1) // CHUNK
        band = (col >= row) & (col <= row + ATT_BAND - 1)
        for oh_ref, out_ref in ((ohq_ref, bq_ref), (ohk_ref, bk_ref)):
            vec = _dot(tv, oh_ref[...], HIGHEST)[0:1, :]
            slab = jnp.concatenate([vec if b == 0 else pltpu.roll(vec, b, 1) for b in range(SUBLANES)], axis=0)
            rows = [slab if a == 0 else pltpu.roll(slab, SUBLANES * a, 1) for a in range(nslab)]
            full = jnp.concatenate(rows, axis=0)[:, :ATT_KW]
            out_ref[...] = jnp.where(band, full, NEG)

    h = table_pad.shape[0]
    oh_spec = pl.BlockSpec((3 * LANES, ATT_VEC), lambda i: (0, 0))
    out_spec = pl.BlockSpec((None, ATT_QB, ATT_KW), lambda i: (i, 0, 0))
    return pl.pallas_call(
        body, name="att_bias", grid=(h,),
        in_specs=[pl.BlockSpec((None, 1, 3 * LANES), lambda i: (i, 0, 0)), oh_spec, oh_spec],
        out_specs=[out_spec, out_spec],
        out_shape=[jax.ShapeDtypeStruct((h, ATT_QB, ATT_KW), F32)] * 2,
        compiler_params=_cparams("parallel"),
    )(table_pad, jnp.asarray(ohq), jnp.asarray(ohk))


def _head_masks():
    lane = lax.broadcasted_iota(jnp.int32, (1, LANES), 1)
    return [lane < ATT_DH, lane >= ATT_DH]


def _att_fwd(proj, bias_q):
    t = proj.shape[0]
    nb = t // ATT_QB
    scale = ATT_DH ** -0.5

    def body(q_ref, k0_ref, k1_ref, k2_ref, v0_ref, v1_ref, v2_ref, b_ref, o_ref, lse_ref, lset_ref):
        i = pl.program_id(0)
        q = (q_ref[...] * scale).astype(BF16)
        kk = jnp.concatenate([k0_ref[...], k1_ref[...], k2_ref[...]], axis=0).astype(BF16)
        vv = jnp.concatenate([v0_ref[...], v1_ref[...], v2_ref[...]], axis=0).astype(BF16)
        kpos = lax.broadcasted_iota(jnp.int32, (1, ATT_KW), 1) + (i - 2) * ATT_QB
        valid = kpos >= 0
        lane = lax.broadcasted_iota(jnp.int32, (1, LANES), 1)
        masks = _head_masks()
        lse_cols = jnp.zeros((ATT_QB, LANES), F32)
        for p in range(ATT_HEADS // 2):
            cs = slice(p * LANES, (p + 1) * LANES)
            qt, kt, vt = q[:, cs], kk[:, cs], vv[:, cs]
            acc = jnp.zeros((ATT_QB, LANES), F32)
            for sub in range(2):
                h = 2 * p + sub
                s = _dot_nt(jnp.where(masks[sub], qt, 0), kt) + b_ref[h]
                s = jnp.where(valid, s, NEG)
                mx = jnp.max(s, axis=-1, keepdims=True)
                e = jnp.exp(s - mx)
                l = jnp.sum(e, axis=-1, keepdims=True)
                acc = acc + _dot(e.astype(BF16), jnp.where(masks[sub], vt, 0)) * (1.0 / l)
                lse_cols = lse_cols + jnp.where(lane == h, mx + jnp.log(l), 0.0)
            o_ref[:, cs] = acc.astype(BF16)
        lse_ref[...] = lse_cols
        lset_ref[...] = lse_cols.T[0:SUBLANES, :]

    def kv_spec(off, cb):
        return pl.BlockSpec((ATT_QB, PB), lambda i: (jnp.maximum(i + off, 0), cb))

    return pl.pallas_call(
        body, name="att_fwd", grid=(nb,),
        in_specs=[pl.BlockSpec((ATT_QB, PB), lambda i: (i, CB_QB)),
                  kv_spec(-2, CB_KB), kv_spec(-1, CB_KB), kv_spec(0, CB_KB),
                  kv_spec(-2, CB_VB), kv_spec(-1, CB_VB), kv_spec(0, CB_VB),
                  pl.BlockSpec((ATT_HEADS, ATT_QB, ATT_KW), lambda i: (0, 0, 0))],
        out_specs=[pl.BlockSpec((ATT_QB, WIDTH_B), lambda i: (i, 0)),
                   pl.BlockSpec((ATT_QB, LANES), lambda i: (i, 0)),
                   pl.BlockSpec((SUBLANES, ATT_QB), lambda i: (0, i))],
        out_shape=[jax.ShapeDtypeStruct((t, WIDTH_B), BF16), jax.ShapeDtypeStruct((t, LANES), F32),
                   jax.ShapeDtypeStruct((SUBLANES, t), F32)],
        compiler_params=_cparams("parallel"),
    )(proj, proj, proj, proj, proj, proj, proj, bias_q)


def _att_dq(proj, bias_q, lse, d_ob):
    t = proj.shape[0]
    nb = t // ATT_QB
    scale = ATT_DH ** -0.5
    nslab = ATT_QB // SUBLANES

    def body(q_ref, k0_ref, k1_ref, k2_ref, v0_ref, v1_ref, v2_ref, b_ref, lse_ref, do_ref,
             dq_ref, dlt_ref, slab_ref):
        i = pl.program_id(0)

        @pl.when(i == 0)
        def _():
            slab_ref[...] = jnp.zeros_like(slab_ref)

        q = (q_ref[...] * scale).astype(BF16)
        kk = jnp.concatenate([k0_ref[...], k1_ref[...], k2_ref[...]], axis=0).astype(BF16)
        vv = jnp.concatenate([v0_ref[...], v1_ref[...], v2_ref[...]], axis=0).astype(BF16)
        do = do_ref[...].astype(BF16)
        kpos = lax.broadcasted_iota(jnp.int32, (1, ATT_KW), 1) + (i - 2) * ATT_QB
        valid = kpos >= 0
        lane = lax.broadcasted_iota(jnp.int32, (1, LANES), 1)
        masks = _head_masks()
        lse_all = lse_ref[...]
        dlt_cols = jnp.zeros((ATT_QB, LANES), F32)
        zpad = jnp.zeros((SUBLANES, ATT_VEC - ATT_KW), F32)
        for p in range(ATT_HEADS // 2):
            cs = slice(p * LANES, (p + 1) * LANES)
            qt, kt, vt, dot_ = q[:, cs], kk[:, cs], vv[:, cs], do[:, cs]
            acc = jnp.zeros((ATT_QB, LANES), F32)
            for sub in range(2):
                h = 2 * p + sub
                s = _dot_nt(jnp.where(masks[sub], qt, 0), kt) + b_ref[h]
                s = jnp.where(valid, s, NEG)
                pr = jnp.exp(s - lse_all[:, h:h + 1])
                dp = _dot_nt(jnp.where(masks[sub], dot_, 0), vt)
                dl = jnp.sum(pr * dp, axis=-1, keepdims=True)
                ds = pr * (dp - dl)
                acc = acc + _dot(ds.astype(BF16), jnp.where(masks[sub], kt, 0)) * scale
                dlt_cols = dlt_cols + jnp.where(lane == h, dl, 0.0)
                sl = jnp.zeros((SUBLANES, ATT_VEC), F32)
                for a in range(nslab):
                    piece = jnp.concatenate([ds[a * SUBLANES:(a + 1) * SUBLANES, :], zpad], axis=1)
                    sl = sl + (piece if a == 0 else pltpu.roll(piece, ATT_VEC - SUBLANES * a, 1))
                slab_ref[h] += sl
            dq_ref[:, cs] = acc.astype(BF16)
        dlt_ref[...] = dlt_cols.T[0:SUBLANES, :]

    def kv_spec(off, cb):
        return pl.BlockSpec((ATT_QB, PB), lambda i: (jnp.maximum(i + off, 0), cb))

    return pl.pallas_call(
        body, name="att_dq", grid=(nb,),
        in_specs=[pl.BlockSpec((ATT_QB, PB), lambda i: (i, CB_QB)),
                  kv_spec(-2, CB_KB), kv_spec(-1, CB_KB), kv_spec(0, CB_KB),
                  kv_spec(-2, CB_VB), kv_spec(-1, CB_VB), kv_spec(0, CB_VB),
                  pl.BlockSpec((ATT_HEADS, ATT_QB, ATT_KW), lambda i: (0, 0, 0)),
                  pl.BlockSpec((ATT_QB, LANES), lambda i: (i, 0)),
                  pl.BlockSpec((ATT_QB, WIDTH_B), lambda i: (i, 0))],
        out_specs=[pl.BlockSpec((ATT_QB, WIDTH_B), lambda i: (i, 0)),
                   pl.BlockSpec((SUBLANES, ATT_QB), lambda i: (0, i)),
                   pl.BlockSpec((ATT_HEADS, SUBLANES, ATT_VEC), lambda i: (0, 0, 0))],
        out_shape=[jax.ShapeDtypeStruct((t, WIDTH_B), BF16), jax.ShapeDtypeStruct((SUBLANES, t), F32),
                   jax.ShapeDtypeStruct((ATT_HEADS, SUBLANES, ATT_VEC), F32)],
        compiler_params=_cparams("arbitrary"),
    )(proj, proj, proj, proj, proj, proj, proj, bias_q, lse, d_ob)


def _att_dkv(proj, bias_k, lse_t, dlt_t, d_ob):
    t = proj.shape[0]
    nb = t // ATT_QB
    scale = ATT_DH ** -0.5

    def body(k_ref, v_ref, q0_ref, q1_ref, q2_ref, d0_ref, d1_ref, d2_ref, l0_ref, l1_ref, l2_ref,
             e0_ref, e1_ref, e2_ref, b_ref, dk_ref, dv_ref):
        i = pl.program_id(0)
        k = k_ref[...].astype(BF16)
        v = v_ref[...].astype(BF16)
        qq = (jnp.concatenate([q0_ref[...], q1_ref[...], q2_ref[...]], axis=0) * scale).astype(BF16)
        do = jnp.concatenate([d0_ref[...], d1_ref[...], d2_ref[...]], axis=0).astype(BF16)
        lse = jnp.concatenate([l0_ref[...], l1_ref[...], l2_ref[...]], axis=1)
        dlt = jnp.concatenate([e0_ref[...], e1_ref[...], e2_ref[...]], axis=1)
        qpos = lax.broadcasted_iota(jnp.int32, (1, ATT_KW), 1) + i * ATT_QB
        valid = qpos < t
        masks = _head_masks()
        for p in range(ATT_HEADS // 2):
            cs = slice(p * LANES, (p + 1) * LANES)
            kt, vt, qt, dot_ = k[:, cs], v[:, cs], qq[:, cs], do[:, cs]
            acc_k = jnp.zeros((ATT_QB, LANES), F32)
            acc_v = jnp.zeros((ATT_QB, LANES), F32)
            for sub in range(2):
                h = 2 * p + sub
                st = _dot_nt(jnp.where(masks[sub], kt, 0), qt) + b_ref[h]
                st = jnp.where(valid, st, NEG)
                pt = jnp.exp(st - lse[h:h + 1, :])
                dot_m = jnp.where(masks[sub], dot_, 0)
                acc_v = acc_v + _dot(pt.astype(BF16), dot_m)
                dpt = _dot_nt(jnp.where(masks[sub], vt, 0), dot_)
                dst = pt * (dpt - dlt[h:h + 1, :])
                acc_k = acc_k + _dot(dst.astype(BF16), jnp.where(masks[sub], qt, 0))
            dk_ref[:, cs] = acc_k.astype(BF16)
            dv_ref[:, cs] = acc_v.astype(BF16)

    def q_spec(off, cb):
        return pl.BlockSpec((ATT_QB, PB), lambda i: (jnp.minimum(i + off, nb - 1), cb))

    def d_spec(off):
        return pl.BlockSpec((ATT_QB, WIDTH_B), lambda i: (jnp.minimum(i + off, nb - 1), 0))

    def r_spec(off):
        return pl.BlockSpec((SUBLANES, ATT_QB), lambda i: (0, jnp.minimum(i + off, nb - 1)))

    row = pl.BlockSpec((ATT_QB, WIDTH_B), lambda i: (i, 0))
    return pl.pallas_call(
        body, name="att_dkv", grid=(nb,),
        in_specs=[pl.BlockSpec((ATT_QB, PB), lambda i: (i, CB_KB)), pl.BlockSpec((ATT_QB, PB), lambda i: (i, CB_VB)),
                  q_spec(0, CB_QB), q_spec(1, CB_QB), q_spec(2, CB_QB),
                  d_spec(0), d_spec(1), d_spec(2), r_spec(0), r_spec(1), r_spec(2),
                  r_spec(0), r_spec(1), r_spec(2),
                  pl.BlockSpec((ATT_HEADS, ATT_QB, ATT_KW), lambda i: (0, 0, 0))],
        out_specs=[row, row],
        out_shape=[jax.ShapeDtypeStruct((t, WIDTH_B), BF16)] * 2,
        compiler_params=_cparams("parallel"),
    )(proj, proj, proj, proj, proj, d_ob, d_ob, d_ob, lse_t, lse_t, lse_t, dlt_t, dlt_t, dlt_t, bias_k)


def _relbias_grad(slabs):
    ohq, _ = _bias_onehots()

    def body(s_ref, oh_ref, o_ref):
        sv = s_ref[...]
        vec = sv[0:1, :]
        for b in range(1, SUBLANES):
            vec = vec + pltpu.roll(sv[b:b + 1, :], ATT_VEC - b, 1)
        o_ref[...] = _dot_nt(jnp.broadcast_to(vec, (SUBLANES, ATT_VEC)), oh_ref[...], HIGHEST)[0:1, :]

    h = slabs.shape[0]
    return pl.pallas_call(
        body, name="att_dbias", grid=(h,),
        in_specs=[pl.BlockSpec((None, SUBLANES, ATT_VEC), lambda i: (i, 0, 0)),
                  pl.BlockSpec((3 * LANES, ATT_VEC), lambda i: (0, 0))],
        out_specs=pl.BlockSpec((None, 1, 3 * LANES), lambda i: (i, 0, 0)),
        out_shape=jax.ShapeDtypeStruct((h, 1, 3 * LANES), F32),
        compiler_params=_cparams("parallel"),
    )(slabs, jnp.asarray(ohq))


GDN_TM = 512
GDN_CB = 4
HALO = SUBLANES


def _conv_taps(ext, width, lead, n):
    return [(ext if k == width - 1 else pltpu.roll(ext, width - 1 - k, 0))[lead:lead + n] for k in range(width)]


def _prev_halo_spec(tm, width, cb):
    return pl.BlockSpec((HALO, width), lambda i: (jnp.maximum(i * (tm // HALO) - 1, 0), cb))


def _next_halo_spec(tm, width, cb, t):
    return pl.BlockSpec((HALO, width), lambda i: (jnp.minimum((i + 1) * (tm // HALO), t // HALO - 1), cb))


def _gdn_prep_fwd(proj, conv_w):
    t = proj.shape[0]
    tm = GDN_TM

    def body(q_ref, k_ref, v_ref, hq_ref, hk_ref, hv_ref, w_ref, qn_ref, kn_ref, vo_ref):
        first = pl.program_id(0) == 0
        for idx, (x_ref, h_ref, o_ref) in enumerate(((q_ref, hq_ref, qn_ref), (k_ref, hk_ref, kn_ref),
                                                      (v_ref, hv_ref, vo_ref))):
            halo = jnp.where(first, 0.0, h_ref[...])
            ext = jnp.concatenate([halo, x_ref[...]], axis=0)
            w = w_ref[:, idx * KEY_A:(idx + 1) * KEY_A]
            taps = _conv_taps(ext, GDN_CONV, HALO, tm)
            y = sum(w[k:k + 1, :] * taps[k] for k in range(GDN_CONV))
            a = y * _sigmoid(y)
            if idx < 2:
                for h in range(GDN_HEADS):
                    cs = slice(h * GDN_DK, (h + 1) * GDN_DK)
                    seg = a[:, cs]
                    o_ref[:, cs] = seg * lax.rsqrt(jnp.sum(seg * seg, axis=-1, keepdims=True) + EPS)
            else:
                o_ref[...] = a

    row = pl.BlockSpec((tm, KEY_A), lambda i: (i, 0))
    return pl.pallas_call(
        body, name="gdn_prep_fwd", grid=(t // tm,),
        in_specs=[pl.BlockSpec((tm, PB), lambda i: (i, CB_QA)), pl.BlockSpec((tm, PB), lambda i: (i, CB_KA)),
                  pl.BlockSpec((tm, PB), lambda i: (i, CB_VA)),
                  _prev_halo_spec(tm, PB, CB_QA), _prev_halo_spec(tm, PB, CB_KA), _prev_halo_spec(tm, PB, CB_VA),
                  pl.BlockSpec((GDN_CONV, 3 * KEY_A), lambda i: (0, 0))],
        out_specs=[row, row, row],
        out_shape=[jax.ShapeDtypeStruct((t, KEY_A), F32)] * 3,
        compiler_params=_cparams("parallel"),
    )(proj, proj, proj, proj, proj, proj, conv_w)


def _gdn_prep_bwd(proj, conv_w, dqn, dkn, dv):
    t = proj.shape[0]
    tm = GDN_TM
    nt = t // tm
    n_ext = tm + HALO

    def body(q_ref, k_ref, v_ref, pq_ref, pk_ref, pv_ref, nq_ref, nk_ref, nv_ref,
             dq_ref, dk_ref, dv_ref, ndq_ref, ndk_ref, ndv_ref, w_ref, oq_ref, ok_ref, ov_ref, dw_ref):
        i = pl.program_id(0)
        first, last = i == 0, i == nt - 1

        @pl.when(first)
        def _():
            dw_ref[...] = jnp.zeros_like(dw_ref)

        groups = ((q_ref, pq_ref, nq_ref, dq_ref, ndq_ref, oq_ref), (k_ref, pk_ref, nk_ref, dk_ref, ndk_ref, ok_ref),
                  (v_ref, pv_ref, nv_ref, dv_ref, ndv_ref, ov_ref))
        for idx, (x_ref, p_ref, n_ref, d_ref, nd_ref, o_ref) in enumerate(groups):
            cs_all = slice(idx * KEY_A, (idx + 1) * KEY_A)
            ext = jnp.concatenate([jnp.where(first, 0.0, p_ref[...]), x_ref[...], jnp.where(last, 0.0, n_ref[...])], axis=0)
            w = w_ref[:, cs_all]
            taps = _conv_taps(ext, GDN_CONV, HALO, n_ext)
            y = sum(w[k:k + 1, :] * taps[k] for k in range(GDN_CONV))
            sg = _sigmoid(y)
            a = y * sg
            dup = jnp.concatenate([d_ref[...], jnp.where(last, 0.0, nd_ref[...])], axis=0)
            if idx < 2:
                segs = []
                for h in range(GDN_HEADS):
                    cs = slice(h * GDN_DK, (h + 1) * GDN_DK)
                    seg = a[:, cs]
                    r = lax.rsqrt(jnp.sum(seg * seg, axis=-1, keepdims=True) + EPS)
                    nrm = seg * r
                    dn = dup[:, cs]
                    segs.append(r * (dn - nrm * jnp.sum(dn * nrm, axis=-1, keepdims=True)))
                da = jnp.concatenate(segs, axis=1)
            else:
                da = dup
            dy = da * sg * (1.0 + y * (1.0 - sg))
            dx = sum(w[k:k + 1, :] * (dy if k == GDN_CONV - 1 else pltpu.roll(dy, n_ext - (GDN_CONV - 1 - k), 0))[:tm]
                     for k in range(GDN_CONV))
            o_ref[...] = dx.astype(BF16)
            for k in range(GDN_CONV):
                dw_ref[k:k + 1, cs_all] += jnp.sum(dy[:tm] * taps[k][:tm], axis=0, keepdims=True)

    row = pl.BlockSpec((tm, KEY_A), lambda i: (i, 0))
    nrow = _next_halo_spec(tm, KEY_A, 0, t)
    return pl.pallas_call(
        body, name="gdn_prep_bwd", grid=(nt,),
        in_specs=[pl.BlockSpec((tm, PB), lambda i: (i, CB_QA)), pl.BlockSpec((tm, PB), lambda i: (i, CB_KA)),
                  pl.BlockSpec((tm, PB), lambda i: (i, CB_VA)),
                  _prev_halo_spec(tm, PB, CB_QA), _prev_halo_spec(tm, PB, CB_KA), _prev_halo_spec(tm, PB, CB_VA),
                  _next_halo_spec(tm, PB, CB_QA, t), _next_halo_spec(tm, PB, CB_KA, t), _next_halo_spec(tm, PB, CB_VA, t),
                  row, row, row, nrow, nrow, nrow,
                  pl.BlockSpec((GDN_CONV, 3 * KEY_A), lambda i: (0, 0))],
        out_specs=[row, row, row, pl.BlockSpec((SUBLANES, 3 * KEY_A), lambda i: (0, 0))],
        out_shape=[jax.ShapeDtypeStruct((t, KEY_A), BF16)] * 3 + [jax.ShapeDtypeStruct((SUBLANES, 3 * KEY_A), F32)],
        compiler_params=_cparams("arbitrary"),
    )(proj, proj, proj, proj, proj, proj, proj, proj, proj, dqn, dkn, dv, dqn, dkn, dv, conv_w)


class _Pair(dict):
    __getattr__ = dict.__getitem__
    __setattr__ = dict.__setitem__


def _gdn_terms(bd, par, kn_ref, qn_ref):
    c = CHUNK
    ii = lax.broadcasted_iota(jnp.int32, (c, c), 0)
    jj = lax.broadcasted_iota(jnp.int32, (c, c), 1)
    strict, incl = ii > jj, ii >= jj
    ltri = incl.astype(F32)
    ts = []
    for cc in range(GDN_CB):
        for h in range(GDN_HEADS):
            t = _Pair(cc=cc, h=h, rows=slice(cc * c, (cc + 1) * c), cs=slice(h * GDN_DK, (h + 1) * GDN_DK),
                      strict=strict, incl=incl)
            t.beta = _sigmoid(bd[t.rows, h:h + 1])
            t.ea = jnp.exp(par[0:1, h:h + 1])
            t.sp_arg = bd[t.rows, GDN_HEADS + h:GDN_HEADS + h + 1] + par[1:2, h:h + 1]
            t.g = -t.ea * _softplus(t.sp_arg)
            t.k = kn_ref[t.rows, t.cs]
            t.q = qn_ref[t.rows, t.cs] * (GDN_DK ** -0.5)
            t.kb, t.qb = t.k.astype(BF16), t.q.astype(BF16)
            ts.append(t)
    for t in ts:
        t.gb = _dot(ltri, jnp.broadcast_to(t.g, (c, GDN_DK)), HIGHEST)
    for t in ts:
        t.kk = _dot_nt(t.kb, t.kb)
        t.qk = _dot_nt(t.qb, t.kb)
    for t in ts:
        gc = t.gb[:, :c]
        diff = gc - gc.T
        t.dec_s = jnp.exp(jnp.where(strict, diff, NEG))
        t.dec_i = jnp.exp(jnp.where(incl, diff, NEG))
        t.gam = jnp.exp(t.gb)
        glast = t.gb[c - 1:c, :]
        t.e_rest = jnp.exp(glast - t.gb)
        t.gl = jnp.exp(glast)
        t.p = t.qk * t.dec_i
    return ts


def _gdn_fwd(qn, kn, v, proj, par, gnw):
    t = qn.shape[0]
    c = CHUNK
    nc = t // c
    r_ = GDN_CB * c

    def body(qn_ref, kn_ref, v_ref, bd_ref, z_ref, par_ref, gnw_ref,
             oan_ref, o_ref, sp_ref, w_ref, u_ref, tm_ref, s_ref):
        @pl.when(pl.program_id(0) == 0)
        def _():
            s_ref[...] = jnp.zeros_like(s_ref)

        bd, par, gnw_v = bd_ref[...], par_ref[...], gnw_ref[...]
        eye = (lax.broadcasted_iota(jnp.int32, (c, c), 0) == lax.broadcasted_iota(jnp.int32, (c, c), 1)).astype(F32)
        ts = _gdn_terms(bd, par, kn_ref, qn_ref)
        for t in ts:
            t.vv = v_ref[t.rows, t.cs]
            t.x = -(t.beta * t.kk * t.dec_s)
            t.tinv = eye + t.x
        for t in ts:
            t.xs = _split(t.x)
        for _ in range(5):
            for t in ts:
                t.xs = _split(_dot3s(t.xs, t.xs))
            for t in ts:
                t.tinv = t.tinv + _dot3s(_split(t.tinv), t.xs)
        for t in ts:
            tsp = _split(t.tinv)
            t.wm = _dot3s(tsp, _split((t.beta * t.gam) * t.k))
            t.uv = _dot3s(tsp, _split(t.beta * t.vv))
        for t in ts:
            w_ref[t.rows, t.cs] = t.wm
            tm_ref[t.cc, t.h] = t.tinv
            t.wb = t.wm.astype(BF16)
            t.qgb = (t.q * t.gam).astype(BF16)
            t.kdb = (t.k * t.e_rest).astype(BF16)
            t.pb = t.p.astype(BF16)
        for cc in range(GDN_CB):
            tc = [t for t in ts if t.cc == cc]
            for t in tc:
                t.sh = s_ref[t.h]
                t.sb = t.sh.astype(BF16)
                sp_ref[cc, t.h] = t.sh
            for t in tc:
                t.ws = _dot(t.wb, t.sb)
                t.qs = _dot(t.qgb, t.sb)
            for t in tc:
                t.u = t.uv - t.ws
                t.ub = t.u.astype(BF16)
            for t in tc:
                t.pu = _dot(t.pb, t.ub)
                t.ku = _dot_tn(t.kdb, t.ub)
            for t in tc:
                t.o = t.qs + t.pu
                s_ref[t.h] = t.gl * t.sh + t.ku
                u_ref[t.rows, t.cs] = t.u
                o_ref[t.rows, t.cs] = t.o
        for t in ts:
            zz = z_ref[t.rows, t.cs]
            rr = lax.rsqrt(jnp.mean(t.o * t.o, axis=-1, keepdims=True) + EPS)
            oan_ref[t.rows, t.cs] = ((t.o * rr) * gnw_v * (zz * _sigmoid(zz))).astype(BF16)

    row = pl.BlockSpec((r_, KEY_A), lambda i: (i, 0))
    return pl.pallas_call(
        body, name="gdn_fwd", grid=(nc // GDN_CB,),
        in_specs=[row, row, row, pl.BlockSpec((r_, LANES), lambda i: (i, CB_BD)),
                  pl.BlockSpec((r_, PB), lambda i: (i, CB_ZA)),
                  pl.BlockSpec((SUBLANES, LANES), lambda i: (0, 0)), pl.BlockSpec((1, GDN_DK), lambda i: (0, 0))],
        out_specs=[row, row, pl.BlockSpec((GDN_CB, GDN_HEADS, GDN_DK, GDN_DK), lambda i: (i, 0, 0, 0)),
                   row, row, pl.BlockSpec((GDN_CB, GDN_HEADS, c, c), lambda i: (i, 0, 0, 0))],
        out_shape=[jax.ShapeDtypeStruct((t, KEY_A), BF16), jax.ShapeDtypeStruct((t, KEY_A), F32),
                   jax.ShapeDtypeStruct((nc, GDN_HEADS, GDN_DK, GDN_DK), F32),
                   jax.ShapeDtypeStruct((t, KEY_A), F32), jax.ShapeDtypeStruct((t, KEY_A), F32),
                   jax.ShapeDtypeStruct((nc, GDN_HEADS, c, c), F32)],
        scratch_shapes=[pltpu.VMEM((GDN_HEADS, GDN_DK, GDN_DK), F32)],
        compiler_params=_cparams("arbitrary"),
    )(qn, kn, v, proj, proj, par, gnw)


def _gdn_bwd(qn, kn, v, proj, par, gnw, o, sprev, wst, ust, tst, d_oan):
    t = qn.shape[0]
    c = CHUNK
    nc = t // c
    nb = nc // GDN_CB
    r_ = GDN_CB * c

    def body(qn_ref, kn_ref, v_ref, bd_ref, z_ref, par_ref, gnw_ref, o_ref, sp_ref, w_ref, u_ref, tm_ref, do_ref,
             dqn_ref, dkn_ref, dv_ref, dz_ref, dbd_ref, acc_ref, ds_ref):
        @pl.when(pl.program_id(0) == 0)
        def _():
            ds_ref[...] = jnp.zeros_like(ds_ref)
            acc_ref[...] = jnp.zeros_like(acc_ref)

        bd, par, gnw_v = bd_ref[...], par_ref[...], gnw_ref[...]
        lane = lax.broadcasted_iota(jnp.int32, (1, LANES), 1)
        rix = lax.broadcasted_iota(jnp.int32, (c, 1), 0)
        ii = lax.broadcasted_iota(jnp.int32, (c, c), 0)
        jj = lax.broadcasted_iota(jnp.int32, (c, c), 1)
        upper = (jj >= ii).astype(F32)
        acc_a = jnp.zeros((1, LANES), F32)
        acc_d = jnp.zeros((1, LANES), F32)
        acc_g = jnp.zeros((1, LANES), F32)
        ts = _gdn_terms(bd, par, kn_ref, qn_ref)
        for t in ts:
            t.vv = v_ref[t.rows, t.cs]
            t.sh = sp_ref[t.cc, t.h]
            t.sb = t.sh.astype(BF16)
            t.wm, t.u, t.tinv = w_ref[t.rows, t.cs], u_ref[t.rows, t.cs], tm_ref[t.cc, t.h]
            t.wb, t.ub = t.wm.astype(BF16), t.u.astype(BF16)
            ov, zz, dout = o_ref[t.rows, t.cs], z_ref[t.rows, t.cs], do_ref[t.rows, t.cs]
            sg = _sigmoid(zz)
            sil = zz * sg
            rr = lax.rsqrt(jnp.mean(ov * ov, axis=-1, keepdims=True) + EPS)
            on = ov * rr
            dz_ref[t.rows, t.cs] = (dout * on * gnw_v * (sg * (1.0 + zz * (1.0 - sg)))).astype(BF16)
            acc_g = acc_g + jnp.sum(dout * on * sil, axis=0, keepdims=True)
            don = dout * gnw_v * sil
            t.dob = (rr * (don - on * jnp.mean(don * on, axis=-1, keepdims=True))).astype(BF16)
            t.qg = t.q * t.gam
            t.kd = t.k * t.e_rest
            t.qgb, t.kdb = t.qg.astype(BF16), t.kd.astype(BF16)
            t.ptb = t.p.T.astype(BF16)
        for cc in reversed(range(GDN_CB)):
            tc = [t for t in ts if t.cc == cc]
            for t in tc:
                t.dsn = ds_ref[t.h]
                t.dsnb = t.dsn.astype(BF16)
            for t in tc:
                t.du = _dot(t.ptb, t.dob) + _dot(t.kdb, t.dsnb)
                t.dkd = _dot_nt(t.ub, t.dsnb)
                t.dgl = jnp.sum(jnp.sum(t.dsn * t.sh, axis=1, keepdims=True), axis=0, keepdims=True)
            for t in tc:
                t.dub = t.du.astype(BF16)
            for t in tc:
                ds_ref[t.h] = t.gl * t.dsn + _dot_tn(t.qgb, t.dob) - _dot_tn(t.wb, t.dub)
        for t in ts:
            t.dqg = _dot_nt(t.dob, t.sb)
            t.dp = _dot_nt(t.dob, t.ub)
            t.dwm = -_dot_nt(t.dub, t.sb)
            t.uv = t.u + _dot(t.wb, t.sb)
        for t in ts:
            tsp = _split(t.tinv.T)
            t.dbk = _dot3s(tsp, _split(t.dwm))
            t.dbv = _dot3s(tsp, _split(t.du))
        for t in ts:
            d_a = -(_dot_nt(t.dbk.astype(BF16), t.wb) + _dot_nt(t.dbv.astype(BF16), t.uv.astype(BF16)))
            t.d_a = jnp.where(t.strict, d_a, 0.0)
        for t in ts:
            t.dkk = t.d_a * t.beta * t.dec_s
            t.dqk = t.dp * t.dec_i
            t.dqkb = t.dqk.astype(BF16)
        for t in ts:
            t.dq = _dot(t.dqkb, t.kb) + t.dqg * t.gam
            t.dk = (t.dbk * (t.beta * t.gam) + _dot_tn(t.dqkb, t.qb) + _dot((t.dkk + t.dkk.T).astype(BF16), t.kb)
                    + t.dkd * t.e_rest)
        for t in ts:
            dbeta = (jnp.sum(t.d_a * t.kk * t.dec_s, axis=-1, keepdims=True)
                     + jnp.sum(t.dbk * t.k * t.gam, axis=-1, keepdims=True) + jnp.sum(t.dbv * t.vv, axis=-1, keepdims=True))
            t.dbl = dbeta * t.beta * (1.0 - t.beta)
            dv_ref[t.rows, t.cs] = t.dbv * t.beta
            bk = (t.beta * t.gam) * t.k
            zc = jnp.sum(t.dkd * t.kd, axis=-1, keepdims=True)
            xs = t.dkk * t.kk + t.dp * t.p
            dgc = (jnp.sum(xs, axis=-1, keepdims=True) - jnp.sum(xs.T, axis=-1, keepdims=True)
                   + jnp.sum(t.dbk * bk, axis=-1, keepdims=True) + jnp.sum(t.dqg * t.qg, axis=-1, keepdims=True) - zc)
            dglast = jnp.sum(zc, axis=0, keepdims=True) + t.dgl * t.gl[:, 0:1]
            t.dgc = dgc + jnp.where(rix == c - 1, dglast, 0.0)
        for t in ts:
            t.dg = _dot(upper, jnp.broadcast_to(t.dgc, (c, GDN_DK)), HIGHEST)[:, 0:1]
        dbd_tiles = [jnp.zeros((c, LANES), F32) for _ in range(GDN_CB)]
        for t in ts:
            ddl = t.dg * (-t.ea) * _sigmoid(t.sp_arg)
            acc_a = acc_a + jnp.where(lane == t.h, jnp.sum(t.dg * t.g, axis=0, keepdims=True), 0.0)
            acc_d = acc_d + jnp.where(lane == t.h, jnp.sum(ddl, axis=0, keepdims=True), 0.0)
            dbd_tiles[t.cc] = (dbd_tiles[t.cc] + jnp.where(lane == t.h, t.dbl, 0.0)
                               + jnp.where(lane == GDN_HEADS + t.h, ddl, 0.0))
            dqn_ref[t.rows, t.cs] = t.dq * (GDN_DK ** -0.5)
            dkn_ref[t.rows, t.cs] = t.dk
        for cc in range(GDN_CB):
            dbd_ref[cc * c:(cc + 1) * c, :] = dbd_tiles[cc]
        acc_ref[0:1, :] += acc_a
        acc_ref[1:2, :] += acc_d
        acc_ref[2:3, :] += acc_g

    def rev(i):
        return nb - 1 - i

    row = pl.BlockSpec((r_, KEY_A), lambda i: (rev(i), 0))
    st = pl.BlockSpec((GDN_CB, GDN_HEADS, GDN_DK, GDN_DK), lambda i: (rev(i), 0, 0, 0))
    tt_spec = pl.BlockSpec((GDN_CB, GDN_HEADS, c, c), lambda i: (rev(i), 0, 0, 0))
    return pl.pallas_call(
        body, name="gdn_bwd", grid=(nb,),
        in_specs=[row, row, row, pl.BlockSpec((r_, LANES), lambda i: (rev(i), CB_BD)),
                  pl.BlockSpec((r_, PB), lambda i: (rev(i), CB_ZA)),
                  pl.BlockSpec((SUBLANES, LANES), lambda i: (0, 0)), pl.BlockSpec((1, GDN_DK), lambda i: (0, 0)),
                  row, st, row, row, tt_spec, row],
        out_specs=[row, row, row, row, pl.BlockSpec((r_, LANES), lambda i: (rev(i), 0)),
                   pl.BlockSpec((SUBLANES, LANES), lambda i: (0, 0))],
        out_shape=[jax.ShapeDtypeStruct((t, KEY_A), F32)] * 3 + [jax.ShapeDtypeStruct((t, KEY_A), BF16),
                   jax.ShapeDtypeStruct((t, LANES), F32), jax.ShapeDtypeStruct((SUBLANES, LANES), F32)],
        scratch_shapes=[pltpu.VMEM((GDN_HEADS, GDN_DK, GDN_DK), F32)],
        compiler_params=_cparams("arbitrary"),
    )(qn, kn, v, proj, proj, par, gnw, o, sprev, wst, ust, tst, d_oan)


def _merge_fwd(oan, ob, proj, x, wba, wbb, wout, tm=512):
    t = x.shape[0]

    def body(oa_ref, ob_ref, ga_ref, gb_ref, x_ref, wba_ref, wbb_ref, wout_ref, x2_ref):
        ya = _dot(oa_ref[...], wba_ref[...])
        yb = _dot(ob_ref[...], wbb_ref[...])
        mix = _sigmoid(ga_ref[...]) * ya + _sigmoid(gb_ref[...]) * yb
        x2_ref[...] = x_ref[...] + _dot(mix.astype(BF16), wout_ref[...])

    half = pl.BlockSpec((tm, KEY_A), lambda i: (i, 0))
    row = pl.BlockSpec((tm, D_MODEL), lambda i: (i, 0))
    wsmall = pl.BlockSpec((KEY_A, D_MODEL), lambda i: (0, 0))
    return pl.pallas_call(
        body, name="merge_fwd", grid=(t // tm,),
        in_specs=[half, half, pl.BlockSpec((tm, D_MODEL), lambda i: (i, CB_GA)),
                  pl.BlockSpec((tm, D_MODEL), lambda i: (i, CB_GB)), row, wsmall, wsmall,
                  pl.BlockSpec((D_MODEL, D_MODEL), lambda i: (0, 0))],
        out_specs=row,
        out_shape=jax.ShapeDtypeStruct((t, D_MODEL), F32),
        compiler_params=_cparams("parallel"),
    )(oan, ob, proj, proj, x, wba, wbb, wout)


def _merge_bwd(dx2b, oan, ob, proj, wba, wbb, wout_t, wba_t, wbb_t, tm=512):
    t = dx2b.shape[0]

    def body(dx_ref, oa_ref, ob_ref, ga_ref, gb_ref, wba_ref, wbb_ref, woutt_ref, wbat_ref, wbbt_ref,
             dga_ref, dgb_ref, doa_ref, dob_ref, mix_ref, dya_ref, dyb_ref):
        dmix = _dot(dx_ref[...], woutt_ref[...])
        ya = _dot(oa_ref[...], wba_ref[...])
        yb = _dot(ob_ref[...], wbb_ref[...])
        sa, sb = _sigmoid(ga_ref[...]), _sigmoid(gb_ref[...])
        mix_ref[...] = (sa * ya + sb * yb).astype(BF16)
        dga_ref[...] = (dmix * ya * sa * (1.0 - sa)).astype(BF16)
        dgb_ref[...] = (dmix * yb * sb * (1.0 - sb)).astype(BF16)
        dya = (dmix * sa).astype(BF16)
        dyb = (dmix * sb).astype(BF16)
        dya_ref[...] = dya
        dyb_ref[...] = dyb
        doa_ref[...] = _dot(dya, wbat_ref[...])
        dob_ref[...] = _dot(dyb, wbbt_ref[...])

    half = pl.BlockSpec((tm, KEY_A), lambda i: (i, 0))
    row = pl.BlockSpec((tm, D_MODEL), lambda i: (i, 0))
    wsmall = pl.BlockSpec((KEY_A, D_MODEL), lambda i: (0, 0))
    wsmall_t = pl.BlockSpec((D_MODEL, KEY_A), lambda i: (0, 0))
    big = jax.ShapeDtypeStruct((t, D_MODEL), BF16)
    return pl.pallas_call(
        body, name="merge_bwd", grid=(t // tm,),
        in_specs=[row, half, half, pl.BlockSpec((tm, D_MODEL), lambda i: (i, CB_GA)),
                  pl.BlockSpec((tm, D_MODEL), lambda i: (i, CB_GB)), wsmall, wsmall,
                  pl.BlockSpec((D_MODEL, D_MODEL), lambda i: (0, 0)), wsmall_t, wsmall_t],
        out_specs=[row, row, half, half, row, row, row],
        out_shape=[big, big, jax.ShapeDtypeStruct((t, KEY_A), F32), jax.ShapeDtypeStruct((t, KEY_A), F32), big, big, big],
        compiler_params=_cparams("parallel"),
    )(dx2b, oan, ob, proj, proj, wba, wbb, wout_t, wba_t, wbb_t)


FFN_TM = 128
FFN_W = 2 * D_FF


def _ffn_conv(up_ref, halo_ref, cw_ref, cb_ref, first):
    ext = jnp.concatenate([jnp.where(first, 0.0, halo_ref[...]), up_ref[...]], axis=0)
    taps = _conv_taps(ext, FFN_CONV, HALO, FFN_TM)
    cw = cw_ref[...]
    u = sum(cw[k:k + 1, :] * taps[k] for k in range(FFN_CONV)) + cb_ref[...]
    return u, taps


def _ffn_tail(up, cw, cb, wdown, x2, tgt, w3):
    t = x2.shape[0]
    tm = FFN_TM

    def body(up_ref, halo_ref, cw_ref, cb_ref, wd_ref, x2_ref, tgt_ref, w3_ref, dx_ref, dxb_ref, act_ref, acc_ref):
        first = pl.program_id(0) == 0

        @pl.when(first)
        def _():
            acc_ref[...] = jnp.zeros_like(acc_ref)

        u, _ = _ffn_conv(up_ref, halo_ref, cw_ref, cb_ref, first)
        gate, upp = u[:, :D_FF], u[:, D_FF:]
        act = (gate * _sigmoid(gate) * upp).astype(BF16)
        act_ref[...] = act
        x3 = x2_ref[...] + _dot(act, wd_ref[...])
        r = lax.rsqrt(jnp.mean(x3 * x3, axis=-1, keepdims=True) + EPS)
        xh = x3 * r
        w3v = w3_ref[...]
        err = xh * w3v - tgt_ref[...]
        loss = 0.5 * jnp.sum(jnp.mean(err * err, axis=-1, keepdims=True), axis=0, keepdims=True)
        dy = err * (1.0 / D_MODEL)
        acc_ref[0:1, :] += jnp.sum(dy * xh, axis=0, keepdims=True)
        acc_ref[1:2, :] += jnp.broadcast_to(loss, (1, D_MODEL))
        dxh = dy * w3v
        dx = r * (dxh - xh * jnp.mean(dxh * xh, axis=-1, keepdims=True))
        dx_ref[...] = dx
        dxb_ref[...] = dx.astype(BF16)

    row = pl.BlockSpec((tm, D_MODEL), lambda i: (i, 0))
    vec = pl.BlockSpec((1, D_MODEL), lambda i: (0, 0))
    return pl.pallas_call(
        body, name="ffn_tail", grid=(t // tm,),
        in_specs=[pl.BlockSpec((tm, FFN_W), lambda i: (i, 0)), _prev_halo_spec(tm, FFN_W, 0),
                  pl.BlockSpec((SUBLANES, FFN_W), lambda i: (0, 0)), pl.BlockSpec((1, FFN_W), lambda i: (0, 0)),
                  pl.BlockSpec((D_FF, D_MODEL), lambda i: (0, 0)), row, row, vec],
        out_specs=[row, row, pl.BlockSpec((tm, D_FF), lambda i: (i, 0)), pl.BlockSpec((SUBLANES, D_MODEL), lambda i: (0, 0))],
        out_shape=[jax.ShapeDtypeStruct((t, D_MODEL), F32), jax.ShapeDtypeStruct((t, D_MODEL), BF16),
                   jax.ShapeDtypeStruct((t, D_FF), BF16), jax.ShapeDtypeStruct((SUBLANES, D_MODEL), F32)],
        compiler_params=_cparams("arbitrary"),
    )(up, up, cw, cb, wdown, x2, tgt, w3)


def _ffn_bwd_act(dx3b, wdown_t, up, cw, cb):
    t = up.shape[0]
    tm = FFN_TM

    def body(dx_ref, wdt_ref, up_ref, halo_ref, cw_ref, cb_ref, du_ref, acc_ref):
        first = pl.program_id(0) == 0

        @pl.when(first)
        def _():
            acc_ref[...] = jnp.zeros_like(acc_ref)

        dact = _dot(dx_ref[...], wdt_ref[...])
        u, taps = _ffn_conv(up_ref, halo_ref, cw_ref, cb_ref, first)
        gate, upp = u[:, :D_FF], u[:, D_FF:]
        sg = _sigmoid(gate)
        du = jnp.concatenate([dact * upp * (sg * (1.0 + gate * (1.0 - sg))), dact * (gate * sg)], axis=1)
        du_ref[...] = du
        for k in range(FFN_CONV):
            acc_ref[k:k + 1, :] += jnp.sum(du * taps[k], axis=0, keepdims=True)
        acc_ref[FFN_CONV:FFN_CONV + 1, :] += jnp.sum(du, axis=0, keepdims=True)

    return pl.pallas_call(
        body, name="ffn_bwd_act", grid=(t // tm,),
        in_specs=[pl.BlockSpec((tm, D_MODEL), lambda i: (i, 0)), pl.BlockSpec((D_MODEL, D_FF), lambda i: (0, 0)),
                  pl.BlockSpec((tm, FFN_W), lambda i: (i, 0)), _prev_halo_spec(tm, FFN_W, 0),
                  pl.BlockSpec((SUBLANES, FFN_W), lambda i: (0, 0)), pl.BlockSpec((1, FFN_W), lambda i: (0, 0))],
        out_specs=[pl.BlockSpec((tm, FFN_W), lambda i: (i, 0)), pl.BlockSpec((SUBLANES, FFN_W), lambda i: (0, 0))],
        out_shape=[jax.ShapeDtypeStruct((t, FFN_W), F32), jax.ShapeDtypeStruct((SUBLANES, FFN_W), F32)],
        compiler_params=_cparams("arbitrary"),
    )(dx3b, wdown_t, up, up, cw, cb)


def _ffn_bwd_conv(du, cw):
    t = du.shape[0]
    tm = FFN_TM
    nt = t // tm
    n_ext = tm + HALO

    def body(du_ref, nxt_ref, cw_ref, o_ref):
        last = pl.program_id(0) == nt - 1
        ext = jnp.concatenate([du_ref[...], jnp.where(last, 0.0, nxt_ref[...])], axis=0)
        cw = cw_ref[...]
        acc = cw[FFN_CONV - 1:FFN_CONV, :] * ext[:tm]
        for k in range(FFN_CONV - 1):
            acc = acc + cw[k:k + 1, :] * pltpu.roll(ext, n_ext - (FFN_CONV - 1 - k), 0)[:tm]
        o_ref[...] = acc.astype(BF16)

    return pl.pallas_call(
        body, name="ffn_bwd_conv", grid=(nt,),
        in_specs=[pl.BlockSpec((tm, FFN_W), lambda i: (i, 0)), _next_halo_spec(tm, FFN_W, 0, t),
                  pl.BlockSpec((SUBLANES, FFN_W), lambda i: (0, 0))],
        out_specs=pl.BlockSpec((tm, FFN_W), lambda i: (i, 0)),
        out_shape=jax.ShapeDtypeStruct((t, FFN_W), BF16),
        compiler_params=_cparams("parallel"),
    )(du, du, cw)


def _adamw(parts, w, m, v, name, tr):
    r, cols = w.shape

    def body(p_ref, w_ref, m_ref, v_ref, g_ref, d_ref, mo_ref, vo_ref):
        g = p_ref[0].astype(F32)
        for s in range(1, N_DEV):
            g = g + p_ref[s].astype(F32)
        mm = ADAM_B1 * m_ref[...] + (1.0 - ADAM_B1) * g
        vv = ADAM_B2 * v_ref[...] + (1.0 - ADAM_B2) * (g * g)
        m_hat = mm / (1.0 - ADAM_B1 ** ADAM_STEP)
        v_hat = vv / (1.0 - ADAM_B2 ** ADAM_STEP)
        g_ref[...] = g
        d_ref[...] = -ADAM_LR * (m_hat / (jnp.sqrt(v_hat) + ADAM_EPS) + ADAM_WD * w_ref[...])
        mo_ref[...] = mm
        vo_ref[...] = vv

    assert r % tr == 0
    row = pl.BlockSpec((tr, cols), lambda i: (i, 0))
    return pl.pallas_call(
        body, name=name, grid=(r // tr,),
        in_specs=[pl.BlockSpec((N_DEV, tr, cols), lambda i: (0, i, 0)), row, row, row],
        out_specs=[row, row, row, row],
        out_shape=[jax.ShapeDtypeStruct((r, cols), F32)] * 4,
        compiler_params=_cparams("parallel"),
    )(parts, w, m, v)


def _mesh_pos():
    return lax.axis_index("x"), lax.axis_index("y"), lax.axis_index("c")


def _peer(pos, k):
    x, y, c = pos
    return (x ^ ((k >> 2) & 1), y ^ ((k >> 1) & 1), c ^ (k & 1))


def _flat_id(pos):
    return 4 * pos[0] + 2 * pos[1] + pos[2]


def _exchange_copies(srcs, dsts, scatter, send_sems, recv_sems, loc_sems):
    pos = _mesh_pos()
    me = _flat_id(pos)
    local, remote = [], []
    for j, (src, dst) in enumerate(zip(srcs, dsts)):
        local.append(pltpu.make_async_copy(src.at[me] if scatter[j] else src, dst.at[me], loc_sems.at[j]))
        for k in range(1, N_DEV):
            to = _peer(pos, k)
            remote.append(pltpu.make_async_remote_copy(
                src_ref=src.at[_flat_id(to)] if scatter[j] else src, dst_ref=dst.at[me],
                send_sem=send_sems.at[j, k - 1], recv_sem=recv_sems.at[j, k - 1],
                device_id=to, device_id_type=pl.DeviceIdType.MESH))
    return local, remote


def _exchange_shapes(arrays, scatter):
    return [jax.ShapeDtypeStruct(a.shape if s else (N_DEV,) + a.shape, a.dtype) for a, s in zip(arrays, scatter)]


def _exchange_sems(n):
    return [pltpu.SemaphoreType.DMA((n, N_DEV - 1)), pltpu.SemaphoreType.DMA((n, N_DEV - 1)), pltpu.SemaphoreType.DMA((n,))]


def _exchange(arrays, scatter, name):
    n = len(arrays)
    any_spec = pl.BlockSpec(memory_space=pl.ANY)

    def body(*refs):
        local, remote = _exchange_copies(refs[:n], refs[n:2 * n], scatter, *refs[2 * n:])
        for cp in local + remote:
            cp.start()
        for cp in remote:
            cp.wait()
        for cp in local:
            cp.wait()

    return pl.pallas_call(
        body, name=name, in_specs=[any_spec] * n, out_specs=[any_spec] * n,
        out_shape=_exchange_shapes(arrays, scatter), scratch_shapes=_exchange_sems(n),
    )(*arrays)


def _pad_rows(a, rows):
    return jnp.pad(a, ((0, rows - a.shape[0]),) + ((0, 0),) * (a.ndim - 1))


PACK_UNIT = SUBLANES * LANES


def _pack_lanes(parts, rows):
    out = []
    for a in parts:
        f = a.reshape(-1)
        out.append(jnp.pad(f, (0, (-f.shape[0]) % PACK_UNIT)).reshape(-1, LANES))
    packed = jnp.concatenate(out, axis=0)
    assert packed.shape[0] == rows, (packed.shape, rows)
    return packed


def _unpack_lanes(buf, shapes):
    out, r0 = [], 0
    for shp in shapes:
        n = math.prod(shp)
        nr = -(-n // PACK_UNIT) * SUBLANES
        out.append(buf[r0:r0 + nr].reshape(-1)[:n].reshape(shp))
        r0 += nr
    return out


def _col_shards(g):
    r, n = g.shape
    return g.reshape(r, N_DEV, n // N_DEV).transpose(1, 0, 2)


def _col_unshard(s):
    _, r, w = s.shape
    return s.transpose(1, 0, 2).reshape(r, N_DEV * w)


def _lane_rows(flat):
    n = flat.shape[1]
    return jnp.pad(flat, ((0, 0), (0, (-n) % PACK_UNIT))).reshape(N_DEV, -1, LANES)


SMALL_ROWS = 128
WS_ROWS = 32


def kernel(x, norm_mix_w, w_in, conv_qkv_w, a_log, dt_bias, gdn_norm_w, w_branch_a, w_branch_b, rel_bias, w_out, norm_ffn_w, w_up, conv_ffn_w, conv_ffn_b, w_down, norm_final_w, loss_target, m_norm_mix_w, m_w_in, m_conv_qkv_w, m_a_log, m_dt_bias, m_gdn_norm_w, m_w_branch_a, m_w_branch_b, m_rel_bias, m_w_out, m_norm_ffn_w, m_w_up, m_conv_ffn_w, m_conv_ffn_b, m_w_down, m_norm_final_w, v_norm_mix_w, v_w_in, v_conv_qkv_w, v_a_log, v_dt_bias, v_gdn_norm_w, v_w_branch_a, v_w_branch_b, v_rel_bias, v_w_out, v_norm_ffn_w, v_w_up, v_conv_ffn_w, v_conv_ffn_b, v_w_down, v_norm_final_w):
    big_w = (w_in, w_branch_a, w_branch_b, w_out, w_up, w_down, conv_qkv_w, conv_ffn_w)
    big_m = (m_w_in, m_w_branch_a, m_w_branch_b, m_w_out, m_w_up, m_w_down, m_conv_qkv_w, m_conv_ffn_w)
    big_v = (v_w_in, v_w_branch_a, v_w_branch_b, v_w_out, v_w_up, v_w_down, v_conv_qkv_w, v_conv_ffn_w)
    small_w = (norm_mix_w, a_log, dt_bias, gdn_norm_w, rel_bias, norm_ffn_w, conv_ffn_b, norm_final_w)
    small_m = (m_norm_mix_w, m_a_log, m_dt_bias, m_gdn_norm_w, m_rel_bias, m_norm_ffn_w, m_conv_ffn_b, m_norm_final_w)
    small_v = (v_norm_mix_w, v_a_log, v_dt_bias, v_gdn_norm_w, v_rel_bias, v_norm_ffn_w, v_conv_ffn_b, v_norm_final_w)

    xs, tgt = x[0], loss_target[0]
    ws = _pack_lanes(big_w[6:], WS_ROWS)
    g_in, gs = _exchange([w_in[0].astype(BF16), ws], (False, False), "all_gather_in")
    win = _col_unshard(g_in)
    gs = gs.reshape(N_DEV, -1)
    cqkv = gs[:, :GDN_CONV * 192].reshape(N_DEV, GDN_CONV, 192).transpose(1, 0, 2).reshape(GDN_CONV, 3 * KEY_A)
    cffn = gs[:, PACK_UNIT:PACK_UNIT + FFN_CONV * 704].reshape(N_DEV, FFN_CONV, 704).transpose(1, 0, 2).reshape(FFN_CONV, FFN_W)
    cffn = _pad_rows(cffn, SUBLANES)
    w_all = jnp.concatenate([win[:, 3592:5640], win[:, 0:2048], win[:, 2056:3592], win[:, 2048:2056],
                             jnp.zeros((D_MODEL, PROJ_W - D_IN), BF16)], axis=1)
    par = _pad_rows(jnp.pad(jnp.concatenate([a_log, dt_bias], axis=0), ((0, 0), (0, LANES - GDN_HEADS))), SUBLANES)
    table = jnp.pad(rel_bias[0], ((0, 0), (0, 3 * LANES - rel_bias.shape[-1]))).reshape(ATT_HEADS, 1, 3 * LANES)

    h1 = _rmsnorm_cast(xs, norm_mix_w, "norm_mix")
    proj, g_ba, g_bb, g_out, g_up, g_down = _mm_nn(
        h1, w_all, F32, "in_proj", MM_TM, 1152, D_MODEL, carry=([w[0].astype(BF16) for w in big_w[1:6]], (False,) * 5))
    wba, wbb, wup = _col_unshard(g_ba), _col_unshard(g_bb), _col_unshard(g_up)
    wout = g_out.reshape(D_MODEL, D_MODEL)
    wdown = g_down.reshape(D_FF, D_MODEL)
    qn, kn, va = _gdn_prep_fwd(proj, cqkv)
    oan, o_gdn, sprev, wst, ust, tst = _gdn_fwd(qn, kn, va, proj, par, gdn_norm_w)
    bias_q, bias_k = _att_bias(table)
    ob, lse, lse_t = _att_fwd(proj, bias_q)
    x2 = _merge_fwd(oan, ob, proj, xs, wba, wbb, wout)
    h2 = _rmsnorm_cast(x2, norm_ffn_w, "norm_ffn")
    up = _mm_nn(h2, wup, F32, "ffn_up", MM_TM, 1408, D_MODEL)
    dx3, dx3b, act, tail_sums = _ffn_tail(up, cffn, conv_ffn_b, wdown, x2, tgt, norm_final_w.reshape(1, D_MODEL))

    g_wdown = _mm_tn(act, dx3b, "dw_down", 512)
    du, ffn_sums = _ffn_bwd_act(dx3b, wdown.T, up, cffn, conv_ffn_b)
    dup = _ffn_bwd_conv(du, cffn)
    g_wup = _mm_tn(h2, dup, "dw_up", 1408)
    dh2, r_up, r_down = _mm_nn(dup, wup.T, F32, "ffn_up_bwd", MM_TM, D_MODEL, 1408, carry=(
        [_col_shards(g_wup).astype(BF16), g_wdown.reshape(N_DEV, -1, D_MODEL).astype(BF16)], (True, True)))
    dx2, dx2b, nffn_sums = _rms_bwd(dh2, x2, norm_ffn_w, dx3, "norm_ffn_bwd")
    dga, dgb, d_oan, d_ob, mixb, dya, dyb = _merge_bwd(dx2b, oan, ob, proj, wba, wbb, wout.T, wba.T, wbb.T)
    g_wout = _mm_tn(mixb, dx2b, "dw_out", 512)
    g_wba = _mm_tn(oan, dya, "dw_branch_a", 512)
    g_wbb = _mm_tn(ob, dyb, "dw_branch_b", 512)
    dqb, dlt_t, slabs = _att_dq(proj, bias_q, lse, d_ob)
    dkb, dvb = _att_dkv(proj, bias_k, lse_t, dlt_t, d_ob)
    g_rel = _relbias_grad(slabs)[:, 0, :rel_bias.shape[-1]]
    dqn, dkn, dva, dz, dbd, gdn_sums = _gdn_bwd(qn, kn, va, proj, par, gdn_norm_w, o_gdn, sprev, wst, ust, tst, d_oan)
    dqa, dka, dvaa, cq_sums = _gdn_prep_bwd(proj, cqkv, dqn, dkn, dva)
    dproj = jnp.concatenate([dga, dgb, dqa, dka, dvaa, dz, dqb, dkb, dvb, dbd.astype(BF16)], axis=1)
    g_wall = _mm_tn(h1, dproj, "dw_in", 1152)
    g_win = jnp.concatenate([g_wall[:, 2048:4096], g_wall[:, 5632:5640], g_wall[:, 4096:5632], g_wall[:, 0:2048]], axis=1)
    g_conv = jnp.concatenate([_lane_rows(_col_shards(cq_sums[:GDN_CONV]).reshape(N_DEV, -1)),
                              _lane_rows(_col_shards(ffn_sums[:FFN_CONV]).reshape(N_DEV, -1))], axis=1)
    dh1, r_in, r_ba, r_bb, r_out, r_conv = _mm_nn(dproj, w_all.T, F32, "in_proj_bwd", MM_TM, D_MODEL, 1152, carry=(
        [_col_shards(g_win).astype(BF16), _col_shards(g_wba).astype(BF16), _col_shards(g_wbb).astype(BF16),
         g_wout.reshape(N_DEV, -1, D_MODEL).astype(BF16), g_conv], (True,) * 5))
    grad_x, _, nmix_sums = _rms_bwd(dh1, xs, norm_mix_w, dx2, "norm_mix_bwd")

    small_g = (nmix_sums[0:1], gdn_sums[0:1, :GDN_HEADS], gdn_sums[1:2, :GDN_HEADS], gdn_sums[2:3], g_rel,
               nffn_sums[0:1], ffn_sums[FFN_CONV:FFN_CONV + 1], tail_sums[0:1], tail_sums[1:2, 0:1])
    r_small, = _exchange([_pack_lanes(small_g, SMALL_ROWS)], (False,), "all_gather_small_grads")
    recv = (r_in, r_ba, r_bb, r_out, r_up, r_down, r_conv, r_small)

    res = {}
    for i, (nm, tr) in enumerate((("w_in", 128), ("w_branch_a", KEY_A), ("w_branch_b", WIDTH_B), ("w_out", 128),
                                  ("w_up", 128), ("w_down", 176))):
        res[nm] = [o[None] for o in _adamw(recv[i], big_w[i][0], big_m[i][0], big_v[i][0], "adamw_" + nm, tr)]
    conv = _adamw(recv[6], _pack_lanes(big_w[6:], WS_ROWS), _pack_lanes(big_m[6:], WS_ROWS), _pack_lanes(big_v[6:], WS_ROWS),
                  "adamw_conv", WS_ROWS)
    conv = [_unpack_lanes(o, [w.shape for w in big_w[6:]]) for o in conv]
    res["conv_qkv_w"] = [o[0] for o in conv]
    res["conv_ffn_w"] = [o[1] for o in conv]
    small_shapes = [w.shape for w in small_w]
    zero = jnp.zeros((1,), F32)
    small = _adamw(recv[7], _pack_lanes(small_w + (zero,), SMALL_ROWS), _pack_lanes(small_m + (zero,), SMALL_ROWS),
                   _pack_lanes(small_v + (zero,), SMALL_ROWS), "adamw_replicated", SMALL_ROWS)
    small = [_unpack_lanes(o, small_shapes + [()]) for o in small]
    loss = small[0][-1]
    for j, nm in enumerate(("norm_mix_w", "a_log", "dt_bias", "gdn_norm_w", "rel_bias", "norm_ffn_w", "conv_ffn_b",
                            "norm_final_w")):
        res[nm] = [o[j] for o in small]

    names = ("norm_mix_w", "w_in", "conv_qkv_w", "a_log", "dt_bias", "gdn_norm_w", "w_branch_a", "w_branch_b", "rel_bias",
             "w_out", "norm_ffn_w", "w_up", "conv_ffn_w", "conv_ffn_b", "w_down", "norm_final_w")
    outs = [res[n][kind] for kind in range(4) for n in names]
    return (loss, grad_x[None], *outs)
```

```python
import functools
import math

import numpy as np
import jax
import jax.numpy as jnp
from jax import lax
from jax.experimental import pallas as pl
from jax.experimental.pallas import tpu as pltpu

F32, BF16 = jnp.float32, jnp.bfloat16
HIGHEST = lax.Precision.HIGHEST

N_DEV = 8
D_MODEL = 1024
CHUNK = 64
EPS = 1e-6
GDN_HEADS, GDN_DK = 4, 128
KEY_A = GDN_HEADS * GDN_DK
GDN_CONV = 4
ATT_HEADS, ATT_DH = 8, 64
WIDTH_B = ATT_HEADS * ATT_DH
ATT_BAND = 9
REL_CLIP = 128
D_FF = 2816
FFN_CONV = 3
D_IN = 5640
ADAM_LR, ADAM_B1, ADAM_B2, ADAM_EPS, ADAM_WD, ADAM_STEP = 0.001, 0.9, 0.999, 1e-08, 0.01, 10

LANES = 128
SUBLANES = 8
NEG = -1e30

PROJ_W = 5760
PB = 512
CB_GA, CB_GB = 0, 1
CB_KB, CB_VB, CB_QA, CB_KA, CB_VA, CB_QB, CB_ZA = 4, 5, 6, 7, 8, 9, 10
CB_BD = 44
DP_GATES, DP_KVB, DP_QKVA, DP_QB, DP_ZBD = (2048, 0), (1024, 2), (1536, 2), (512, 9), (640, 8)
W_IN_ORDER = ((3592, 5640), (2568, 3592), (0, 1536), (2056, 2568), (1536, 2048), (2048, 2056))

ATT_QB = 256
ATT_KW = 768
ATT_VEC = 1024


def _dot(a, b, precision=None):
    return jnp.dot(a, b, preferred_element_type=F32, precision=precision)


def _dot_nt(a, b, precision=None):
    return lax.dot_general(a, b, (((1,), (1,)), ((), ())), preferred_element_type=F32, precision=precision)


def _dot_tn(a, b):
    return lax.dot_general(a, b, (((0,), (0,)), ((), ())), preferred_element_type=F32)


def _split(a):
    hi = a.astype(BF16)
    return hi, (a - hi.astype(F32)).astype(BF16)


def _dot3s(a, b):
    return _dot(a[0], b[0]) + (_dot(a[0], b[1]) + _dot(a[1], b[0]))


def _sigmoid(x):
    return 0.5 * jnp.tanh(0.5 * x) + 0.5


def _softplus(x):
    return jnp.maximum(x, 0.0) + jnp.log(1.0 + jnp.exp(-jnp.abs(x)))


def _cparams(*sem):
    return pltpu.CompilerParams(dimension_semantics=tuple(sem))


def _dp_spec(tm, region, index=lambda i: i):
    width, cb = region
    return pl.BlockSpec((tm, width), lambda i: (index(i), cb))


def _rmsnorm_cast(x, w, name, tm=512):
    t, d = x.shape

    def body(x_ref, w_ref, o_ref):
        xv = x_ref[...]
        r = lax.rsqrt(jnp.mean(xv * xv, axis=-1, keepdims=True) + EPS)
        o_ref[...] = (xv * r * w_ref[...]).astype(BF16)

    return pl.pallas_call(
        body, name=name, grid=(t // tm,),
        in_specs=[pl.BlockSpec((tm, d), lambda i: (i, 0)), pl.BlockSpec((1, d), lambda i: (0, 0))],
        out_specs=pl.BlockSpec((tm, d), lambda i: (i, 0)),
        out_shape=jax.ShapeDtypeStruct((t, d), BF16),
        compiler_params=_cparams("parallel"),
    )(x, w)


def _mm_nn(a, b, out_dtype, name, tm, tn, tk, carry=((), ())):
    m, k = a.shape
    _, n = b.shape
    nk = k // tk
    assert m % tm == 0 and n % tn == 0 and k % tk == 0
    arrays, scatter = carry
    nx = len(arrays)
    gm, gn = m // tm, n // tn

    def body(*refs):
        a_ref, b_ref = refs[:2]
        srcs = refs[2:2 + nx]
        o_ref = refs[2 + nx]
        dsts = refs[3 + nx:3 + 2 * nx]
        rest = refs[3 + 2 * nx:]
        i, j, kk = pl.program_id(0), pl.program_id(1), pl.program_id(2)
        if nx:
            local, remote = _exchange_copies(srcs, dsts, scatter, *rest[-3:])

            @pl.when((i == 0) & (j == 0) & (kk == 0))
            def _():
                for cp in local + remote:
                    cp.start()

        if nk == 1:
            o_ref[...] = _dot(a_ref[...], b_ref[...]).astype(out_dtype)
        else:
            acc_ref = rest[0]

            @pl.when(kk == 0)
            def _():
                acc_ref[...] = jnp.zeros_like(acc_ref)

            acc_ref[...] += _dot(a_ref[...], b_ref[...])

            @pl.when(kk == nk - 1)
            def _():
                o_ref[...] = acc_ref[...].astype(out_dtype)

        if nx:
            @pl.when((i == gm - 1) & (j == gn - 1) & (kk == nk - 1))
            def _():
                for cp in remote + local:
                    cp.wait()

    any_spec = pl.BlockSpec(memory_space=pl.ANY)
    scratch = ([pltpu.VMEM((tm, tn), F32)] if nk > 1 else []) + (_exchange_sems(nx) if nx else [])
    out = pl.pallas_call(
        body, name=name, grid=(gm, gn, nk),
        in_specs=[pl.BlockSpec((tm, tk), lambda i, j, kk: (i, kk)),
                  pl.BlockSpec((tk, tn), lambda i, j, kk: (kk, j))] + [any_spec] * nx,
        out_specs=[pl.BlockSpec((tm, tn), lambda i, j, kk: (i, j))] + [any_spec] * nx,
        out_shape=[jax.ShapeDtypeStruct((m, n), out_dtype)] + _exchange_shapes(arrays, scatter),
        scratch_shapes=scratch,
        compiler_params=_cparams(*(("arbitrary",) * 3 if nx else ("parallel", "parallel", "arbitrary"))),
    )(a, b, *arrays)
    return out if nx else out[0]


def _mm_rms_bwd(a, b, x, w, dres, name, tm, tk, carry):
    m, k = a.shape
    _, n = b.shape
    nk = k // tk
    gm = m // tm
    assert m % tm == 0 and k % tk == 0
    arrays, scatter = carry
    nx = len(arrays)

    def body(*refs):
        a_ref, b_ref, x_ref, w_ref, dres_ref = refs[:5]
        srcs = refs[5:5 + nx]
        dx_ref, dxb_ref, dw_ref = refs[5 + nx:8 + nx]
        dsts = refs[8 + nx:8 + 2 * nx]
        acc_ref = refs[8 + 2 * nx]
        i, kk = pl.program_id(0), pl.program_id(1)
        local, remote = _exchange_copies(srcs, dsts, scatter, *refs[9 + 2 * nx:])

        @pl.when((i == 0) & (kk == 0))
        def _():
            for cp in local + remote:
                cp.start()
            dw_ref[...] = jnp.zeros_like(dw_ref)

        @pl.when(kk == 0)
        def _():
            acc_ref[...] = jnp.zeros_like(acc_ref)

        acc_ref[...] += _dot(a_ref[...], b_ref[...])

        @pl.when(kk == nk - 1)
        def _():
            dhv = acc_ref[...]
            xv = x_ref[...]
            r = lax.rsqrt(jnp.mean(xv * xv, axis=-1, keepdims=True) + EPS)
            xh = xv * r
            dw_ref[0:1, :] += jnp.sum(dhv * xh, axis=0, keepdims=True)
            dxh = dhv * w_ref[...]
            dx = dres_ref[...] + r * (dxh - xh * jnp.mean(dxh * xh, axis=-1, keepdims=True))
            dx_ref[...] = dx
            dxb_ref[...] = dx.astype(BF16)

        @pl.when((i == gm - 1) & (kk == nk - 1))
        def _():
            for cp in remote + local:
                cp.wait()

    any_spec = pl.BlockSpec(memory_space=pl.ANY)
    row = pl.BlockSpec((tm, n), lambda i, kk: (i, 0))
    return pl.pallas_call(
        body, name=name, grid=(gm, nk),
        in_specs=[pl.BlockSpec((tm, tk), lambda i, kk: (i, kk)), pl.BlockSpec((tk, n), lambda i, kk: (kk, 0)),
                  row, pl.BlockSpec((1, n), lambda i, kk: (0, 0)), row] + [any_spec] * nx,
        out_specs=[row, row, pl.BlockSpec((SUBLANES, n), lambda i, kk: (0, 0))] + [any_spec] * nx,
        out_shape=[jax.ShapeDtypeStruct((m, n), F32), jax.ShapeDtypeStruct((m, n), BF16),
                   jax.ShapeDtypeStruct((SUBLANES, n), F32)] + _exchange_shapes(arrays, scatter),
        scratch_shapes=[pltpu.VMEM((tm, n), F32)] + _exchange_sems(nx),
        compiler_params=_cparams("arbitrary", "arbitrary"),
    )(a, b, x, w, dres, *arrays)


MM_TM = 1024


def _mm_tn(a, b, name, tn, tk=MM_TM):
    t, m = a.shape
    _, n = b.shape
    assert t % tk == 0 and n % tn == 0

    def body(a_ref, b_ref, o_ref):
        @pl.when(pl.program_id(1) == 0)
        def _():
            o_ref[...] = jnp.zeros_like(o_ref)

        o_ref[...] += _dot_tn(a_ref[...], b_ref[...])

    return pl.pallas_call(
        body, name=name, grid=(n // tn, t // tk),
        in_specs=[pl.BlockSpec((tk, m), lambda j, s: (s, 0)),
                  pl.BlockSpec((tk, tn), lambda j, s: (s, j))],
        out_specs=pl.BlockSpec((m, tn), lambda j, s: (0, j)),
        out_shape=jax.ShapeDtypeStruct((m, n), F32),
        compiler_params=_cparams("parallel", "arbitrary"),
    )(a, b)


def _rel_index(dist):
    return np.clip(dist, -REL_CLIP, REL_CLIP) + REL_CLIP


def _bias_onehots():
    tw = 3 * LANES
    m = np.arange(ATT_VEC)
    dq = np.where(m <= ATT_KW, 512 - m, 512 - (m - ATT_VEC))
    dk = np.where(m < ATT_KW, m, m - ATT_VEC)
    ohq = np.zeros((tw, ATT_VEC), np.float32)
    ohk = np.zeros((tw, ATT_VEC), np.float32)
    ohq[_rel_index(dq), m] = 1.0
    ohk[_rel_index(dk), m] = 1.0
    return ohq, ohk


def _att_bias(table_pad):
    ohq, ohk = _bias_onehots()
    nslab = ATT_QB // SUBLANES

    def body(t_ref, ohq_ref, ohk_ref, bq_ref, bk_ref):
        tv = jnp.broadcast_to(t_ref[...], (SUBLANES, 3 * LANES))
        row = lax.broadcasted_iota(jnp.int32, (ATT_QB, ATT_KW), 0) // CHUNK
        col = lax.broadcasted_iota(jnp.int32, (ATT_QB, ATT_KW), 1) // CHUNK
        band = (col >= row) & (col <= row + ATT_BAND - 1)
        for oh_ref, out_ref in ((ohq_ref, bq_ref), (ohk_ref, bk_ref)):
            vec = _dot(tv, oh_ref[...], HIGHEST)[0:1, :]
            slab = jnp.concatenate([vec if b == 0 else pltpu.roll(vec, b, 1) for b in range(SUBLANES)], axis=0)
            rows = [slab if a == 0 else pltpu.roll(slab, SUBLANES * a, 1) for a in range(nslab)]
            full = jnp.concatenate(rows, axis=0)[:, :ATT_KW]
            out_ref[...] = jnp.where(band, full, NEG)

    h = table_pad.shape[0]
    oh_spec = pl.BlockSpec((3 * LANES, ATT_VEC), lambda i: (0, 0))
    out_spec = pl.BlockSpec((None, ATT_QB, ATT_KW), lambda i: (i, 0, 0))
    return pl.pallas_call(
        body, name="att_bias", grid=(h,),
        in_specs=[pl.BlockSpec((None, 1, 3 * LANES), lambda i: (i, 0, 0)), oh_spec, oh_spec],
        out_specs=[out_spec, out_spec],
        out_shape=[jax.ShapeDtypeStruct((h, ATT_QB, ATT_KW), F32)] * 2,
        compiler_params=_cparams("parallel"),
    )(table_pad, jnp.asarray(ohq), jnp.asarray(ohk))


def _head_masks():
    lane = lax.broadcasted_iota(jnp.int32, (1, LANES), 1)
    return [lane < ATT_DH, lane >= ATT_DH]


def _att_fwd(proj, bias_q):
    t = proj.shape[0]
    nb = t // ATT_QB
    scale = ATT_DH ** -0.5

    def body(q_ref, k0_ref, k1_ref, k2_ref, v0_ref, v1_ref, v2_ref, b_ref, o_ref, lse_ref, lset_ref):
        i = pl.program_id(0)
        q = (q_ref[...] * scale).astype(BF16)
        kk = jnp.concatenate([k0_ref[...], k1_ref[...], k2_ref[...]], axis=0).astype(BF16)
        vv = jnp.concatenate([v0_ref[...], v1_ref[...], v2_ref[...]], axis=0).astype(BF16)
        kpos = lax.broadcasted_iota(jnp.int32, (1, ATT_KW), 1) + (i - 2) * ATT_QB
        valid = kpos >= 0
        lane = lax.broadcasted_iota(jnp.int32, (1, LANES), 1)
        masks = _head_masks()
        lse_cols = jnp.zeros((ATT_QB, LANES), F32)
        for p in range(ATT_HEADS // 2):
            cs = slice(p * LANES, (p + 1) * LANES)
            qt, kt, vt = q[:, cs], kk[:, cs], vv[:, cs]
            acc = jnp.zeros((ATT_QB, LANES), F32)
            for sub in range(2):
                h = 2 * p + sub
                s = _dot_nt(jnp.where(masks[sub], qt, 0), kt) + b_ref[h]
                s = jnp.where(valid, s, NEG)
                mx = jnp.max(s, axis=-1, keepdims=True)
                e = jnp.exp(s - mx)
                l = jnp.sum(e, axis=-1, keepdims=True)
                acc = acc + _dot(e.astype(BF16), jnp.where(masks[sub], vt, 0)) * (1.0 / l)
                lse_cols = lse_cols + jnp.where(lane == h, mx + jnp.log(l), 0.0)
            o_ref[:, cs] = acc.astype(BF16)
        lse_ref[...] = lse_cols
        lset_ref[...] = lse_cols.T[0:SUBLANES, :]

    def kv_spec(off, cb):
        return pl.BlockSpec((ATT_QB, PB), lambda i: (jnp.maximum(i + off, 0), cb))

    return pl.pallas_call(
        body, name="att_fwd", grid=(nb,),
        in_specs=[pl.BlockSpec((ATT_QB, PB), lambda i: (i, CB_QB)),
                  kv_spec(-2, CB_KB), kv_spec(-1, CB_KB), kv_spec(0, CB_KB),
                  kv_spec(-2, CB_VB), kv_spec(-1, CB_VB), kv_spec(0, CB_VB),
                  pl.BlockSpec((ATT_HEADS, ATT_QB, ATT_KW), lambda i: (0, 0, 0))],
        out_specs=[pl.BlockSpec((ATT_QB, WIDTH_B), lambda i: (i, 0)),
                   pl.BlockSpec((ATT_QB, LANES), lambda i: (i, 0)),
                   pl.BlockSpec((SUBLANES, ATT_QB), lambda i: (0, i))],
        out_shape=[jax.ShapeDtypeStruct((t, WIDTH_B), BF16), jax.ShapeDtypeStruct((t, LANES), F32),
                   jax.ShapeDtypeStruct((SUBLANES, t), F32)],
        compiler_params=_cparams("parallel"),
    )(proj, proj, proj, proj, proj, proj, proj, bias_q)


def _att_dq(proj, bias_q, lse, d_ob, dproj):
    t = proj.shape[0]
    nb = t // ATT_QB
    scale = ATT_DH ** -0.5
    nslab = ATT_QB // SUBLANES

    def body(q_ref, k0_ref, k1_ref, k2_ref, v0_ref, v1_ref, v2_ref, b_ref, lse_ref, do_ref, dp_in_ref,
             dq_ref, dlt_ref, slab_ref):
        i = pl.program_id(0)

        @pl.when(i == 0)
        def _():
            slab_ref[...] = jnp.zeros_like(slab_ref)

        q = (q_ref[...] * scale).astype(BF16)
        kk = jnp.concatenate([k0_ref[...], k1_ref[...], k2_ref[...]], axis=0).astype(BF16)
        vv = jnp.concatenate([v0_ref[...], v1_ref[...], v2_ref[...]], axis=0).astype(BF16)
        do = do_ref[...].astype(BF16)
        kpos = lax.broadcasted_iota(jnp.int32, (1, ATT_KW), 1) + (i - 2) * ATT_QB
        valid = kpos >= 0
        lane = lax.broadcasted_iota(jnp.int32, (1, LANES), 1)
        masks = _head_masks()
        lse_all = lse_ref[...]
        dlt_cols = jnp.zeros((ATT_QB, LANES), F32)
        zpad = jnp.zeros((SUBLANES, ATT_VEC - ATT_KW), F32)
        for p in range(ATT_HEADS // 2):
            cs = slice(p * LANES, (p + 1) * LANES)
            qt, kt, vt, dot_ = q[:, cs], kk[:, cs], vv[:, cs], do[:, cs]
            acc = jnp.zeros((ATT_QB, LANES), F32)
            for sub in range(2):
                h = 2 * p + sub
                s = _dot_nt(jnp.where(masks[sub], qt, 0), kt) + b_ref[h]
                s = jnp.where(valid, s, NEG)
                pr = jnp.exp(s - lse_all[:, h:h + 1])
                dp = _dot_nt(jnp.where(masks[sub], dot_, 0), vt)
                dl = jnp.sum(pr * dp, axis=-1, keepdims=True)
                ds = pr * (dp - dl)
                acc = acc + _dot(ds.astype(BF16), jnp.where(masks[sub], kt, 0)) * scale
                dlt_cols = dlt_cols + jnp.where(lane == h, dl, 0.0)
                sl = jnp.zeros((SUBLANES, ATT_VEC), F32)
                for a in range(nslab):
                    piece = jnp.concatenate([ds[a * SUBLANES:(a + 1) * SUBLANES, :], zpad], axis=1)
                    sl = sl + (piece if a == 0 else pltpu.roll(piece, ATT_VEC - SUBLANES * a, 1))
                slab_ref[h] += sl
            dq_ref[:, cs] = acc.astype(BF16)
        dlt_ref[...] = dlt_cols.T[0:SUBLANES, :]

    def kv_spec(off, cb):
        return pl.BlockSpec((ATT_QB, PB), lambda i: (jnp.maximum(i + off, 0), cb))

    return pl.pallas_call(
        body, name="att_dq", grid=(nb,),
        in_specs=[pl.BlockSpec((ATT_QB, PB), lambda i: (i, CB_QB)),
                  kv_spec(-2, CB_KB), kv_spec(-1, CB_KB), kv_spec(0, CB_KB),
                  kv_spec(-2, CB_VB), kv_spec(-1, CB_VB), kv_spec(0, CB_VB),
                  pl.BlockSpec((ATT_HEADS, ATT_QB, ATT_KW), lambda i: (0, 0, 0)),
                  pl.BlockSpec((ATT_QB, LANES), lambda i: (i, 0)),
                  pl.BlockSpec((ATT_QB, WIDTH_B), lambda i: (i, 0)), pl.BlockSpec(memory_space=pl.ANY)],
        out_specs=[_dp_spec(ATT_QB, DP_QB),
                   pl.BlockSpec((SUBLANES, ATT_QB), lambda i: (0, i)),
                   pl.BlockSpec((ATT_HEADS, SUBLANES, ATT_VEC), lambda i: (0, 0, 0))],
        out_shape=[jax.ShapeDtypeStruct(dproj.shape, dproj.dtype), jax.ShapeDtypeStruct((SUBLANES, t), F32),
                   jax.ShapeDtypeStruct((ATT_HEADS, SUBLANES, ATT_VEC), F32)],
        input_output_aliases={10: 0},
        compiler_params=_cparams("arbitrary"),
    )(proj, proj, proj, proj, proj, proj, proj, bias_q, lse, d_ob, dproj)


def _att_dkv(proj, bias_k, lse_t, dlt_t, d_ob, dproj):
    t = proj.shape[0]
    nb = t // ATT_QB
    scale = ATT_DH ** -0.5

    def body(k_ref, v_ref, q0_ref, q1_ref, q2_ref, d0_ref, d1_ref, d2_ref, l0_ref, l1_ref, l2_ref,
             e0_ref, e1_ref, e2_ref, b_ref, dp_in_ref, dkv_ref):
        i = pl.program_id(0)
        k = k_ref[...].astype(BF16)
        v = v_ref[...].astype(BF16)
        qq = (jnp.concatenate([q0_ref[...], q1_ref[...], q2_ref[...]], axis=0) * scale).astype(BF16)
        do = jnp.concatenate([d0_ref[...], d1_ref[...], d2_ref[...]], axis=0).astype(BF16)
        lse = jnp.concatenate([l0_ref[...], l1_ref[...], l2_ref[...]], axis=1)
        dlt = jnp.concatenate([e0_ref[...], e1_ref[...], e2_ref[...]], axis=1)
        qpos = lax.broadcasted_iota(jnp.int32, (1, ATT_KW), 1) + i * ATT_QB
        valid = qpos < t
        masks = _head_masks()
        for p in range(ATT_HEADS // 2):
            cs = slice(p * LANES, (p + 1) * LANES)
            kt, vt, qt, dot_ = k[:, cs], v[:, cs], qq[:, cs], do[:, cs]
            acc_k = jnp.zeros((ATT_QB, LANES), F32)
            acc_v = jnp.zeros((ATT_QB, LANES), F32)
            for sub in range(2):
                h = 2 * p + sub
                st = _dot_nt(jnp.where(masks[sub], kt, 0), qt) + b_ref[h]
                st = jnp.where(valid, st, NEG)
                pt = jnp.exp(st - lse[h:h + 1, :])
                dot_m = jnp.where(masks[sub], dot_, 0)
                acc_v = acc_v + _dot(pt.astype(BF16), dot_m)
                dpt = _dot_nt(jnp.where(masks[sub], vt, 0), dot_)
                dst = pt * (dpt - dlt[h:h + 1, :])
                acc_k = acc_k + _dot(dst.astype(BF16), jnp.where(masks[sub], qt, 0))
            dkv_ref[:, cs] = acc_k.astype(BF16)
            dkv_ref[:, WIDTH_B + p * LANES:WIDTH_B + (p + 1) * LANES] = acc_v.astype(BF16)

    def q_spec(off, cb):
        return pl.BlockSpec((ATT_QB, PB), lambda i: (jnp.minimum(i + off, nb - 1), cb))

    def d_spec(off):
        return pl.BlockSpec((ATT_QB, WIDTH_B), lambda i: (jnp.minimum(i + off, nb - 1), 0))

    def r_spec(off):
        return pl.BlockSpec((SUBLANES, ATT_QB), lambda i: (0, jnp.minimum(i + off, nb - 1)))

    row = pl.BlockSpec((ATT_QB, WIDTH_B), lambda i: (i, 0))
    return pl.pallas_call(
        body, name="att_dkv", grid=(nb,),
        in_specs=[pl.BlockSpec((ATT_QB, PB), lambda i: (i, CB_KB)), pl.BlockSpec((ATT_QB, PB), lambda i: (i, CB_VB)),
                  q_spec(0, CB_QB), q_spec(1, CB_QB), q_spec(2, CB_QB),
                  d_spec(0), d_spec(1), d_spec(2), r_spec(0), r_spec(1), r_spec(2),
                  r_spec(0), r_spec(1), r_spec(2),
                  pl.BlockSpec((ATT_HEADS, ATT_QB, ATT_KW), lambda i: (0, 0, 0)), pl.BlockSpec(memory_space=pl.ANY)],
        out_specs=_dp_spec(ATT_QB, DP_KVB),
        out_shape=jax.ShapeDtypeStruct(dproj.shape, dproj.dtype),
        input_output_aliases={15: 0},
        compiler_params=_cparams("parallel"),
    )(proj, proj, proj, proj, proj, d_ob, d_ob, d_ob, lse_t, lse_t, lse_t, dlt_t, dlt_t, dlt_t, bias_k, dproj)


def _relbias_grad(slabs):
    ohq, _ = _bias_onehots()

    def body(s_ref, oh_ref, o_ref):
        sv = s_ref[...]
        vec = sv[0:1, :]
        for b in range(1, SUBLANES):
            vec = vec + pltpu.roll(sv[b:b + 1, :], ATT_VEC - b, 1)
        o_ref[...] = _dot_nt(jnp.broadcast_to(vec, (SUBLANES, ATT_VEC)), oh_ref[...], HIGHEST)[0:1, :]

    h = slabs.shape[0]
    return pl.pallas_call(
        body, name="att_dbias", grid=(h,),
        in_specs=[pl.BlockSpec((None, SUBLANES, ATT_VEC), lambda i: (i, 0, 0)),
                  pl.BlockSpec((3 * LANES, ATT_VEC), lambda i: (0, 0))],
        out_specs=pl.BlockSpec((None, 1, 3 * LANES), lambda i: (i, 0, 0)),
        out_shape=jax.ShapeDtypeStruct((h, 1, 3 * LANES), F32),
        compiler_params=_cparams("parallel"),
    )(slabs, jnp.asarray(ohq))


GDN_TM = 512
GDN_CB = 4
HALO = SUBLANES


def _conv_taps(ext, width, lead, n):
    return [(ext if k == width - 1 else pltpu.roll(ext, width - 1 - k, 0))[lead:lead + n] for k in range(width)]


def _prev_halo_spec(tm, width, cb):
    return pl.BlockSpec((HALO, width), lambda i: (jnp.maximum(i * (tm // HALO) - 1, 0), cb))


def _next_halo_spec(tm, width, cb, t):
    return pl.BlockSpec((HALO, width), lambda i: (jnp.minimum((i + 1) * (tm // HALO), t // HALO - 1), cb))


def _gdn_prep_fwd(proj, conv_w):
    t = proj.shape[0]
    tm = GDN_TM

    def body(q_ref, k_ref, v_ref, hq_ref, hk_ref, hv_ref, w_ref, qn_ref, kn_ref, vo_ref):
        first = pl.program_id(0) == 0
        for idx, (x_ref, h_ref, o_ref) in enumerate(((q_ref, hq_ref, qn_ref), (k_ref, hk_ref, kn_ref),
                                                      (v_ref, hv_ref, vo_ref))):
            halo = jnp.where(first, 0.0, h_ref[...])
            ext = jnp.concatenate([halo, x_ref[...]], axis=0)
            w = w_ref[:, idx * KEY_A:(idx + 1) * KEY_A]
            taps = _conv_taps(ext, GDN_CONV, HALO, tm)
            y = sum(w[k:k + 1, :] * taps[k] for k in range(GDN_CONV))
            a = y * _sigmoid(y)
            if idx < 2:
                for h in range(GDN_HEADS):
                    cs = slice(h * GDN_DK, (h + 1) * GDN_DK)
                    seg = a[:, cs]
                    o_ref[:, cs] = seg * lax.rsqrt(jnp.sum(seg * seg, axis=-1, keepdims=True) + EPS)
            else:
                o_ref[...] = a

    row = pl.BlockSpec((tm, KEY_A), lambda i: (i, 0))
    return pl.pallas_call(
        body, name="gdn_prep_fwd", grid=(t // tm,),
        in_specs=[pl.BlockSpec((tm, PB), lambda i: (i, CB_QA)), pl.BlockSpec((tm, PB), lambda i: (i, CB_KA)),
                  pl.BlockSpec((tm, PB), lambda i: (i, CB_VA)),
                  _prev_halo_spec(tm, PB, CB_QA), _prev_halo_spec(tm, PB, CB_KA), _prev_halo_spec(tm, PB, CB_VA),
                  pl.BlockSpec((GDN_CONV, 3 * KEY_A), lambda i: (0, 0))],
        out_specs=[row, row, row],
        out_shape=[jax.ShapeDtypeStruct((t, KEY_A), F32)] * 3,
        compiler_params=_cparams("parallel"),
    )(proj, proj, proj, proj, proj, proj, conv_w)


def _gdn_prep_bwd(proj, conv_w, dqn, dkn, dv, dproj):
    t = proj.shape[0]
    tm = GDN_TM
    nt = t // tm
    n_ext = tm + HALO

    def body(q_ref, k_ref, v_ref, pq_ref, pk_ref, pv_ref, nq_ref, nk_ref, nv_ref,
             dq_ref, dk_ref, dv_ref, ndq_ref, ndk_ref, ndv_ref, w_ref, dp_in_ref, out_ref, dw_ref):
        i = pl.program_id(0)
        first, last = i == 0, i == nt - 1

        @pl.when(first)
        def _():
            dw_ref[...] = jnp.zeros_like(dw_ref)

        groups = ((q_ref, pq_ref, nq_ref, dq_ref, ndq_ref), (k_ref, pk_ref, nk_ref, dk_ref, ndk_ref),
                  (v_ref, pv_ref, nv_ref, dv_ref, ndv_ref))
        for idx, (x_ref, p_ref, n_ref, d_ref, nd_ref) in enumerate(groups):
            cs_all = slice(idx * KEY_A, (idx + 1) * KEY_A)
            ext = jnp.concatenate([jnp.where(first, 0.0, p_ref[...]), x_ref[...], jnp.where(last, 0.0, n_ref[...])], axis=0)
            w = w_ref[:, cs_all]
            taps = _conv_taps(ext, GDN_CONV, HALO, n_ext)
            y = sum(w[k:k + 1, :] * taps[k] for k in range(GDN_CONV))
            sg = _sigmoid(y)
            a = y * sg
            dup = jnp.concatenate([d_ref[...], jnp.where(last, 0.0, nd_ref[...])], axis=0)
            if idx < 2:
                segs = []
                for h in range(GDN_HEADS):
                    cs = slice(h * GDN_DK, (h + 1) * GDN_DK)
                    seg = a[:, cs]
                    r = lax.rsqrt(jnp.sum(seg * seg, axis=-1, keepdims=True) + EPS)
                    nrm = seg * r
                    dn = dup[:, cs]
                    segs.append(r * (dn - nrm * jnp.sum(dn * nrm, axis=-1, keepdims=True)))
                da = jnp.concatenate(segs, axis=1)
            else:
                da = dup
            dy = da * sg * (1.0 + y * (1.0 - sg))
            dx = sum(w[k:k + 1, :] * (dy if k == GDN_CONV - 1 else pltpu.roll(dy, n_ext - (GDN_CONV - 1 - k), 0))[:tm]
                     for k in range(GDN_CONV))
            out_ref[:, cs_all] = dx.astype(BF16)
            for k in range(GDN_CONV):
                dw_ref[k:k + 1, cs_all] += jnp.sum(dy[:tm] * taps[k][:tm], axis=0, keepdims=True)

    row = pl.BlockSpec((tm, KEY_A), lambda i: (i, 0))
    nrow = _next_halo_spec(tm, KEY_A, 0, t)
    return pl.pallas_call(
        body, name="gdn_prep_bwd", grid=(nt,),
        in_specs=[pl.BlockSpec((tm, PB), lambda i: (i, CB_QA)), pl.BlockSpec((tm, PB), lambda i: (i, CB_KA)),
                  pl.BlockSpec((tm, PB), lambda i: (i, CB_VA)),
                  _prev_halo_spec(tm, PB, CB_QA), _prev_halo_spec(tm, PB, CB_KA), _prev_halo_spec(tm, PB, CB_VA),
                  _next_halo_spec(tm, PB, CB_QA, t), _next_halo_spec(tm, PB, CB_KA, t), _next_halo_spec(tm, PB, CB_VA, t),
                  row, row, row, nrow, nrow, nrow,
                  pl.BlockSpec((GDN_CONV, 3 * KEY_A), lambda i: (0, 0)), pl.BlockSpec(memory_space=pl.ANY)],
        out_specs=[_dp_spec(tm, DP_QKVA), pl.BlockSpec((SUBLANES, 3 * KEY_A), lambda i: (0, 0))],
        out_shape=[jax.ShapeDtypeStruct(dproj.shape, dproj.dtype), jax.ShapeDtypeStruct((SUBLANES, 3 * KEY_A), F32)],
        input_output_aliases={16: 0},
        compiler_params=_cparams("arbitrary"),
    )(proj, proj, proj, proj, proj, proj, proj, proj, proj, dqn, dkn, dv, dqn, dkn, dv, conv_w, dproj)


class _Pair(dict):
    __getattr__ = dict.__getitem__
    __setattr__ = dict.__setitem__


def _gdn_terms(bd, par, kn_ref, qn_ref):
    c = CHUNK
    ii = lax.broadcasted_iota(jnp.int32, (c, c), 0)
    jj = lax.broadcasted_iota(jnp.int32, (c, c), 1)
    strict, incl = ii > jj, ii >= jj
    ltri = incl.astype(F32)
    ts = []
    for cc in range(GDN_CB):
        for h in range(GDN_HEADS):
            t = _Pair(cc=cc, h=h, rows=slice(cc * c, (cc + 1) * c), cs=slice(h * GDN_DK, (h + 1) * GDN_DK),
                      strict=strict, incl=incl)
            t.beta = _sigmoid(bd[t.rows, h:h + 1])
            t.ea = jnp.exp(par[0:1, h:h + 1])
            t.sp_arg = bd[t.rows, GDN_HEADS + h:GDN_HEADS + h + 1] + par[1:2, h:h + 1]
            t.g = -t.ea * _softplus(t.sp_arg)
            t.k = kn_ref[t.rows, t.cs]
            t.q = qn_ref[t.rows, t.cs] * (GDN_DK ** -0.5)
            t.kb, t.qb = t.k.astype(BF16), t.q.astype(BF16)
            ts.append(t)
    for t in ts:
        t.gb = _dot(ltri, jnp.broadcast_to(t.g, (c, GDN_DK)), HIGHEST)
    for t in ts:
        t.kk = _dot_nt(t.kb, t.kb)
        t.qk = _dot_nt(t.qb, t.kb)
    for t in ts:
        gc = t.gb[:, :c]
        diff = gc - gc.T
        t.dec_s = jnp.exp(jnp.where(strict, diff, NEG))
        t.dec_i = jnp.exp(jnp.where(incl, diff, NEG))
        t.gam = jnp.exp(t.gb)
        glast = t.gb[c - 1:c, :]
        t.e_rest = jnp.exp(glast - t.gb)
        t.gl = jnp.exp(glast)
        t.p = t.qk * t.dec_i
    return ts


def _gdn_fwd(qn, kn, v, proj, par, gnw):
    t = qn.shape[0]
    c = CHUNK
    nc = t // c
    r_ = GDN_CB * c

    def body(qn_ref, kn_ref, v_ref, bd_ref, z_ref, par_ref, gnw_ref,
             oan_ref, o_ref, sp_ref, w_ref, u_ref, tm_ref, s_ref):
        @pl.when(pl.program_id(0) == 0)
        def _():
            s_ref[...] = jnp.zeros_like(s_ref)

        bd, par, gnw_v = bd_ref[...], par_ref[...], gnw_ref[...]
        eye = (lax.broadcasted_iota(jnp.int32, (c, c), 0) == lax.broadcasted_iota(jnp.int32, (c, c), 1)).astype(F32)
        ts = _gdn_terms(bd, par, kn_ref, qn_ref)
        for t in ts:
            t.vv = v_ref[t.rows, t.cs]
            t.x = -(t.beta * t.kk * t.dec_s)
            t.tinv = eye + t.x
        for t in ts:
            t.xs = _split(t.x)
        for _ in range(5):
            for t in ts:
                t.xs = _split(_dot3s(t.xs, t.xs))
            for t in ts:
                t.tinv = t.tinv + _dot3s(_split(t.tinv), t.xs)
        for t in ts:
            tsp = _split(t.tinv)
            t.wm = _dot3s(tsp, _split((t.beta * t.gam) * t.k))
            t.uv = _dot3s(tsp, _split(t.beta * t.vv))
        for t in ts:
            w_ref[t.rows, t.cs] = t.wm
            tm_ref[t.cc, t.h] = t.tinv
            t.wb = t.wm.astype(BF16)
            t.qgb = (t.q * t.gam).astype(BF16)
            t.kdb = (t.k * t.e_rest).astype(BF16)
            t.pb = t.p.astype(BF16)
        for cc in range(GDN_CB):
            tc = [t for t in ts if t.cc == cc]
            for t in tc:
                t.sh = s_ref[t.h]
                t.sb = t.sh.astype(BF16)
                sp_ref[cc, t.h] = t.sh
            for t in tc:
                t.ws = _dot(t.wb, t.sb)
                t.qs = _dot(t.qgb, t.sb)
            for t in tc:
                t.u = t.uv - t.ws
                t.ub = t.u.astype(BF16)
            for t in tc:
                t.pu = _dot(t.pb, t.ub)
                t.ku = _dot_tn(t.kdb, t.ub)
            for t in tc:
                t.o = t.qs + t.pu
                s_ref[t.h] = t.gl * t.sh + t.ku
                u_ref[t.rows, t.cs] = t.u
                o_ref[t.rows, t.cs] = t.o
        for t in ts:
            zz = z_ref[t.rows, t.cs]
            rr = lax.rsqrt(jnp.mean(t.o * t.o, axis=-1, keepdims=True) + EPS)
            oan_ref[t.rows, t.cs] = ((t.o * rr) * gnw_v * (zz * _sigmoid(zz))).astype(BF16)

    row = pl.BlockSpec((r_, KEY_A), lambda i: (i, 0))
    return pl.pallas_call(
        body, name="gdn_fwd", grid=(nc // GDN_CB,),
        in_specs=[row, row, row, pl.BlockSpec((r_, LANES), lambda i: (i, CB_BD)),
                  pl.BlockSpec((r_, PB), lambda i: (i, CB_ZA)),
                  pl.BlockSpec((SUBLANES, LANES), lambda i: (0, 0)), pl.BlockSpec((1, GDN_DK), lambda i: (0, 0))],
        out_specs=[row, row, pl.BlockSpec((GDN_CB, GDN_HEADS, GDN_DK, GDN_DK), lambda i: (i, 0, 0, 0)),
                   row, row, pl.BlockSpec((GDN_CB, GDN_HEADS, c, c), lambda i: (i, 0, 0, 0))],
        out_shape=[jax.ShapeDtypeStruct((t, KEY_A), BF16), jax.ShapeDtypeStruct((t, KEY_A), F32),
                   jax.ShapeDtypeStruct((nc, GDN_HEADS, GDN_DK, GDN_DK), F32),
                   jax.ShapeDtypeStruct((t, KEY_A), F32), jax.ShapeDtypeStruct((t, KEY_A), F32),
                   jax.ShapeDtypeStruct((nc, GDN_HEADS, c, c), F32)],
        scratch_shapes=[pltpu.VMEM((GDN_HEADS, GDN_DK, GDN_DK), F32)],
        compiler_params=_cparams("arbitrary"),
    )(qn, kn, v, proj, proj, par, gnw)


def _gdn_bwd(qn, kn, v, proj, par, gnw, o, sprev, wst, ust, tst, d_oan, dproj):
    t = qn.shape[0]
    c = CHUNK
    nc = t // c
    nb = nc // GDN_CB
    r_ = GDN_CB * c

    def body(qn_ref, kn_ref, v_ref, bd_ref, z_ref, par_ref, gnw_ref, o_ref, sp_ref, w_ref, u_ref, tm_ref, do_ref,
             dp_in_ref, dqn_ref, dkn_ref, dv_ref, dzb_ref, acc_ref, ds_ref):
        @pl.when(pl.program_id(0) == 0)
        def _():
            ds_ref[...] = jnp.zeros_like(ds_ref)
            acc_ref[...] = jnp.zeros_like(acc_ref)

        bd, par, gnw_v = bd_ref[...], par_ref[...], gnw_ref[...]
        lane = lax.broadcasted_iota(jnp.int32, (1, LANES), 1)
        rix = lax.broadcasted_iota(jnp.int32, (c, 1), 0)
        ii = lax.broadcasted_iota(jnp.int32, (c, c), 0)
        jj = lax.broadcasted_iota(jnp.int32, (c, c), 1)
        upper = (jj >= ii).astype(F32)
        acc_a = jnp.zeros((1, LANES), F32)
        acc_d = jnp.zeros((1, LANES), F32)
        acc_g = jnp.zeros((1, LANES), F32)
        ts = _gdn_terms(bd, par, kn_ref, qn_ref)
        for t in ts:
            t.vv = v_ref[t.rows, t.cs]
            t.sh = sp_ref[t.cc, t.h]
            t.sb = t.sh.astype(BF16)
            t.wm, t.u, t.tinv = w_ref[t.rows, t.cs], u_ref[t.rows, t.cs], tm_ref[t.cc, t.h]
            t.wb, t.ub = t.wm.astype(BF16), t.u.astype(BF16)
            ov, zz, dout = o_ref[t.rows, t.cs], z_ref[t.rows, t.cs], do_ref[t.rows, t.cs]
            sg = _sigmoid(zz)
            sil = zz * sg
            rr = lax.rsqrt(jnp.mean(ov * ov, axis=-1, keepdims=True) + EPS)
            on = ov * rr
            dzb_ref[t.rows, t.cs] = (dout * on * gnw_v * (sg * (1.0 + zz * (1.0 - sg)))).astype(BF16)
            acc_g = acc_g + jnp.sum(dout * on * sil, axis=0, keepdims=True)
            don = dout * gnw_v * sil
            t.dob = (rr * (don - on * jnp.mean(don * on, axis=-1, keepdims=True))).astype(BF16)
            t.qg = t.q * t.gam
            t.kd = t.k * t.e_rest
            t.qgb, t.kdb = t.qg.astype(BF16), t.kd.astype(BF16)
            t.ptb = t.p.T.astype(BF16)
        for cc in reversed(range(GDN_CB)):
            tc = [t for t in ts if t.cc == cc]
            for t in tc:
                t.dsn = ds_ref[t.h]
                t.dsnb = t.dsn.astype(BF16)
            for t in tc:
                t.du = _dot(t.ptb, t.dob) + _dot(t.kdb, t.dsnb)
                t.dkd = _dot_nt(t.ub, t.dsnb)
                t.dgl = jnp.sum(jnp.sum(t.dsn * t.sh, axis=1, keepdims=True), axis=0, keepdims=True)
            for t in tc:
                t.dub = t.du.astype(BF16)
            for t in tc:
                ds_ref[t.h] = t.gl * t.dsn + _dot_tn(t.qgb, t.dob) - _dot_tn(t.wb, t.dub)
        for t in ts:
            t.dqg = _dot_nt(t.dob, t.sb)
            t.dp = _dot_nt(t.dob, t.ub)
            t.dwm = -_dot_nt(t.dub, t.sb)
            t.uv = t.u + _dot(t.wb, t.sb)
        for t in ts:
            tsp = _split(t.tinv.T)
            t.dbk = _dot3s(tsp, _split(t.dwm))
            t.dbv = _dot3s(tsp, _split(t.du))
        for t in ts:
            d_a = -(_dot_nt(t.dbk.astype(BF16), t.wb) + _dot_nt(t.dbv.astype(BF16), t.uv.astype(BF16)))
            t.d_a = jnp.where(t.strict, d_a, 0.0)
        for t in ts:
            t.dkk = t.d_a * t.beta * t.dec_s
            t.dqk = t.dp * t.dec_i
            t.dqkb = t.dqk.astype(BF16)
        for t in ts:
            t.dq = _dot(t.dqkb, t.kb) + t.dqg * t.gam
            t.dk = (t.dbk * (t.beta * t.gam) + _dot_tn(t.dqkb, t.qb) + _dot((t.dkk + t.dkk.T).astype(BF16), t.kb)
                    + t.dkd * t.e_rest)
        for t in ts:
            dbeta = (jnp.sum(t.d_a * t.kk * t.dec_s, axis=-1, keepdims=True)
                     + jnp.sum(t.dbk * t.k * t.gam, axis=-1, keepdims=True) + jnp.sum(t.dbv * t.vv, axis=-1, keepdims=True))
            t.dbl = dbeta * t.beta * (1.0 - t.beta)
            dv_ref[t.rows, t.cs] = t.dbv * t.beta
            bk = (t.beta * t.gam) * t.k
            zc = jnp.sum(t.dkd * t.kd, axis=-1, keepdims=True)
            xs = t.dkk * t.kk + t.dp * t.p
            dgc = (jnp.sum(xs, axis=-1, keepdims=True) - jnp.sum(xs.T, axis=-1, keepdims=True)
                   + jnp.sum(t.dbk * bk, axis=-1, keepdims=True) + jnp.sum(t.dqg * t.qg, axis=-1, keepdims=True) - zc)
            dglast = jnp.sum(zc, axis=0, keepdims=True) + t.dgl * t.gl[:, 0:1]
            t.dgc = dgc + jnp.where(rix == c - 1, dglast, 0.0)
        for t in ts:
            t.dg = _dot(upper, jnp.broadcast_to(t.dgc, (c, GDN_DK)), HIGHEST)[:, 0:1]
        dbd_tiles = [jnp.zeros((c, LANES), F32) for _ in range(GDN_CB)]
        for t in ts:
            ddl = t.dg * (-t.ea) * _sigmoid(t.sp_arg)
            acc_a = acc_a + jnp.where(lane == t.h, jnp.sum(t.dg * t.g, axis=0, keepdims=True), 0.0)
            acc_d = acc_d + jnp.where(lane == t.h, jnp.sum(ddl, axis=0, keepdims=True), 0.0)
            dbd_tiles[t.cc] = (dbd_tiles[t.cc] + jnp.where(lane == t.h, t.dbl, 0.0)
                               + jnp.where(lane == GDN_HEADS + t.h, ddl, 0.0))
            dqn_ref[t.rows, t.cs] = t.dq * (GDN_DK ** -0.5)
            dkn_ref[t.rows, t.cs] = t.dk
        for cc in range(GDN_CB):
            dzb_ref[cc * c:(cc + 1) * c, KEY_A:KEY_A + LANES] = dbd_tiles[cc].astype(BF16)
        acc_ref[0:1, :] += acc_a
        acc_ref[1:2, :] += acc_d
        acc_ref[2:3, :] += acc_g

    def rev(i):
        return nb - 1 - i

    row = pl.BlockSpec((r_, KEY_A), lambda i: (rev(i), 0))
    st = pl.BlockSpec((GDN_CB, GDN_HEADS, GDN_DK, GDN_DK), lambda i: (rev(i), 0, 0, 0))
    tt_spec = pl.BlockSpec((GDN_CB, GDN_HEADS, c, c), lambda i: (rev(i), 0, 0, 0))
    return pl.pallas_call(
        body, name="gdn_bwd", grid=(nb,),
        in_specs=[row, row, row, pl.BlockSpec((r_, LANES), lambda i: (rev(i), CB_BD)),
                  pl.BlockSpec((r_, PB), lambda i: (rev(i), CB_ZA)),
                  pl.BlockSpec((SUBLANES, LANES), lambda i: (0, 0)), pl.BlockSpec((1, GDN_DK), lambda i: (0, 0)),
                  row, st, row, row, tt_spec, row, pl.BlockSpec(memory_space=pl.ANY)],
        out_specs=[row, row, row, _dp_spec(r_, DP_ZBD, rev), pl.BlockSpec((SUBLANES, LANES), lambda i: (0, 0))],
        out_shape=[jax.ShapeDtypeStruct((t, KEY_A), F32)] * 3 + [jax.ShapeDtypeStruct(dproj.shape, dproj.dtype),
                                                                jax.ShapeDtypeStruct((SUBLANES, LANES), F32)],
        input_output_aliases={13: 3},
        scratch_shapes=[pltpu.VMEM((GDN_HEADS, GDN_DK, GDN_DK), F32)],
        compiler_params=_cparams("arbitrary"),
    )(qn, kn, v, proj, proj, par, gnw, o, sprev, wst, ust, tst, d_oan, dproj)


def _merge_fwd(oan, ob, proj, x, wba, wbb, wout, tm=512):
    t = x.shape[0]

    def body(oa_ref, ob_ref, ga_ref, gb_ref, x_ref, wba_ref, wbb_ref, wout_ref, x2_ref):
        ya = _dot(oa_ref[...], wba_ref[...])
        yb = _dot(ob_ref[...], wbb_ref[...])
        mix = _sigmoid(ga_ref[...]) * ya + _sigmoid(gb_ref[...]) * yb
        x2_ref[...] = x_ref[...] + _dot(mix.astype(BF16), wout_ref[...])

    half = pl.BlockSpec((tm, KEY_A), lambda i: (i, 0))
    row = pl.BlockSpec((tm, D_MODEL), lambda i: (i, 0))
    wsmall = pl.BlockSpec((KEY_A, D_MODEL), lambda i: (0, 0))
    return pl.pallas_call(
        body, name="merge_fwd", grid=(t // tm,),
        in_specs=[half, half, pl.BlockSpec((tm, D_MODEL), lambda i: (i, CB_GA)),
                  pl.BlockSpec((tm, D_MODEL), lambda i: (i, CB_GB)), row, wsmall, wsmall,
                  pl.BlockSpec((D_MODEL, D_MODEL), lambda i: (0, 0))],
        out_specs=row,
        out_shape=jax.ShapeDtypeStruct((t, D_MODEL), F32),
        compiler_params=_cparams("parallel"),
    )(oan, ob, proj, proj, x, wba, wbb, wout)


def _merge_bwd(dx2b, oan, ob, proj, wba, wbb, wout_t, wba_t, wbb_t, tm=512):
    t = dx2b.shape[0]

    def body(dx_ref, oa_ref, ob_ref, ga_ref, gb_ref, wba_ref, wbb_ref, woutt_ref, wbat_ref, wbbt_ref,
             dg_ref, doa_ref, dob_ref, mix_ref, dya_ref, dyb_ref):
        dmix = _dot(dx_ref[...], woutt_ref[...])
        ya = _dot(oa_ref[...], wba_ref[...])
        yb = _dot(ob_ref[...], wbb_ref[...])
        sa, sb = _sigmoid(ga_ref[...]), _sigmoid(gb_ref[...])
        mix_ref[...] = (sa * ya + sb * yb).astype(BF16)
        dg_ref[:, :D_MODEL] = (dmix * ya * sa * (1.0 - sa)).astype(BF16)
        dg_ref[:, D_MODEL:] = (dmix * yb * sb * (1.0 - sb)).astype(BF16)
        dya = (dmix * sa).astype(BF16)
        dyb = (dmix * sb).astype(BF16)
        dya_ref[...] = dya
        dyb_ref[...] = dyb
        doa_ref[...] = _dot(dya, wbat_ref[...])
        dob_ref[...] = _dot(dyb, wbbt_ref[...])

    half = pl.BlockSpec((tm, KEY_A), lambda i: (i, 0))
    row = pl.BlockSpec((tm, D_MODEL), lambda i: (i, 0))
    wsmall = pl.BlockSpec((KEY_A, D_MODEL), lambda i: (0, 0))
    wsmall_t = pl.BlockSpec((D_MODEL, KEY_A), lambda i: (0, 0))
    big = jax.ShapeDtypeStruct((t, D_MODEL), BF16)
    return pl.pallas_call(
        body, name="merge_bwd", grid=(t // tm,),
        in_specs=[row, half, half, pl.BlockSpec((tm, D_MODEL), lambda i: (i, CB_GA)),
                  pl.BlockSpec((tm, D_MODEL), lambda i: (i, CB_GB)), wsmall, wsmall,
                  pl.BlockSpec((D_MODEL, D_MODEL), lambda i: (0, 0)), wsmall_t, wsmall_t],
        out_specs=[_dp_spec(tm, DP_GATES), half, half, row, row, row],
        out_shape=[jax.ShapeDtypeStruct((t, PROJ_W), BF16), jax.ShapeDtypeStruct((t, KEY_A), F32),
                   jax.ShapeDtypeStruct((t, KEY_A), F32), big, big, big],
        compiler_params=_cparams("parallel"),
    )(dx2b, oan, ob, proj, proj, wba, wbb, wout_t, wba_t, wbb_t)


FFN_TM = 128
FFN_W = 2 * D_FF


def _ffn_conv(up_ref, halo_ref, cw_ref, cb_ref, first):
    ext = jnp.concatenate([jnp.where(first, 0.0, halo_ref[...]), up_ref[...]], axis=0)
    taps = _conv_taps(ext, FFN_CONV, HALO, FFN_TM)
    cw = cw_ref[...]
    u = sum(cw[k:k + 1, :] * taps[k] for k in range(FFN_CONV)) + cb_ref[...]
    return u, taps


def _ffn_tail(up, cw, cb, wdown, x2, tgt, w3):
    t = x2.shape[0]
    tm = FFN_TM

    def body(up_ref, halo_ref, cw_ref, cb_ref, wd_ref, x2_ref, tgt_ref, w3_ref, dx_ref, dxb_ref, act_ref, acc_ref):
        first = pl.program_id(0) == 0

        @pl.when(first)
        def _():
            acc_ref[...] = jnp.zeros_like(acc_ref)

        u, _ = _ffn_conv(up_ref, halo_ref, cw_ref, cb_ref, first)
        gate, upp = u[:, :D_FF], u[:, D_FF:]
        act = (gate * _sigmoid(gate) * upp).astype(BF16)
        act_ref[...] = act
        x3 = x2_ref[...] + _dot(act, wd_ref[...])
        r = lax.rsqrt(jnp.mean(x3 * x3, axis=-1, keepdims=True) + EPS)
        xh = x3 * r
        w3v = w3_ref[...]
        err = xh * w3v - tgt_ref[...]
        loss = 0.5 * jnp.sum(jnp.mean(err * err, axis=-1, keepdims=True), axis=0, keepdims=True)
        dy = err * (1.0 / D_MODEL)
        acc_ref[0:1, :] += jnp.sum(dy * xh, axis=0, keepdims=True)
        acc_ref[1:2, :] += jnp.broadcast_to(loss, (1, D_MODEL))
        dxh = dy * w3v
        dx = r * (dxh - xh * jnp.mean(dxh * xh, axis=-1, keepdims=True))
        dx_ref[...] = dx
        dxb_ref[...] = dx.astype(BF16)

    row = pl.BlockSpec((tm, D_MODEL), lambda i: (i, 0))
    vec = pl.BlockSpec((1, D_MODEL), lambda i: (0, 0))
    return pl.pallas_call(
        body, name="ffn_tail", grid=(t // tm,),
        in_specs=[pl.BlockSpec((tm, FFN_W), lambda i: (i, 0)), _prev_halo_spec(tm, FFN_W, 0),
                  pl.BlockSpec((SUBLANES, FFN_W), lambda i: (0, 0)), pl.BlockSpec((1, FFN_W), lambda i: (0, 0)),
                  pl.BlockSpec((D_FF, D_MODEL), lambda i: (0, 0)), row, row, vec],
        out_specs=[row, row, pl.BlockSpec((tm, D_FF), lambda i: (i, 0)), pl.BlockSpec((SUBLANES, D_MODEL), lambda i: (0, 0))],
        out_shape=[jax.ShapeDtypeStruct((t, D_MODEL), F32), jax.ShapeDtypeStruct((t, D_MODEL), BF16),
                   jax.ShapeDtypeStruct((t, D_FF), BF16), jax.ShapeDtypeStruct((SUBLANES, D_MODEL), F32)],
        compiler_params=_cparams("arbitrary"),
    )(up, up, cw, cb, wdown, x2, tgt, w3)


def _ffn_bwd_act(dx3b, wdown_t, up, cw, cb):
    t = up.shape[0]
    tm = FFN_TM

    def body(dx_ref, wdt_ref, up_ref, halo_ref, cw_ref, cb_ref, du_ref, acc_ref):
        first = pl.program_id(0) == 0

        @pl.when(first)
        def _():
            acc_ref[...] = jnp.zeros_like(acc_ref)

        dact = _dot(dx_ref[...], wdt_ref[...])
        u, taps = _ffn_conv(up_ref, halo_ref, cw_ref, cb_ref, first)
        gate, upp = u[:, :D_FF], u[:, D_FF:]
        sg = _sigmoid(gate)
        du = jnp.concatenate([dact * upp * (sg * (1.0 + gate * (1.0 - sg))), dact * (gate * sg)], axis=1)
        du_ref[...] = du
        for k in range(FFN_CONV):
            acc_ref[k:k + 1, :] += jnp.sum(du * taps[k], axis=0, keepdims=True)
        acc_ref[FFN_CONV:FFN_CONV + 1, :] += jnp.sum(du, axis=0, keepdims=True)

    return pl.pallas_call(
        body, name="ffn_bwd_act", grid=(t // tm,),
        in_specs=[pl.BlockSpec((tm, D_MODEL), lambda i: (i, 0)), pl.BlockSpec((D_MODEL, D_FF), lambda i: (0, 0)),
                  pl.BlockSpec((tm, FFN_W), lambda i: (i, 0)), _prev_halo_spec(tm, FFN_W, 0),
                  pl.BlockSpec((SUBLANES, FFN_W), lambda i: (0, 0)), pl.BlockSpec((1, FFN_W), lambda i: (0, 0))],
        out_specs=[pl.BlockSpec((tm, FFN_W), lambda i: (i, 0)), pl.BlockSpec((SUBLANES, FFN_W), lambda i: (0, 0))],
        out_shape=[jax.ShapeDtypeStruct((t, FFN_W), F32), jax.ShapeDtypeStruct((SUBLANES, FFN_W), F32)],
        compiler_params=_cparams("arbitrary"),
    )(dx3b, wdown_t, up, up, cw, cb)


def _ffn_bwd_conv(du, cw):
    t = du.shape[0]
    tm = FFN_TM
    nt = t // tm
    n_ext = tm + HALO

    def body(du_ref, nxt_ref, cw_ref, o_ref):
        last = pl.program_id(0) == nt - 1
        ext = jnp.concatenate([du_ref[...], jnp.where(last, 0.0, nxt_ref[...])], axis=0)
        cw = cw_ref[...]
        acc = cw[FFN_CONV - 1:FFN_CONV, :] * ext[:tm]
        for k in range(FFN_CONV - 1):
            acc = acc + cw[k:k + 1, :] * pltpu.roll(ext, n_ext - (FFN_CONV - 1 - k), 0)[:tm]
        o_ref[...] = acc.astype(BF16)

    return pl.pallas_call(
        body, name="ffn_bwd_conv", grid=(nt,),
        in_specs=[pl.BlockSpec((tm, FFN_W), lambda i: (i, 0)), _next_halo_spec(tm, FFN_W, 0, t),
                  pl.BlockSpec((SUBLANES, FFN_W), lambda i: (0, 0))],
        out_specs=pl.BlockSpec((tm, FFN_W), lambda i: (i, 0)),
        out_shape=jax.ShapeDtypeStruct((t, FFN_W), BF16),
        compiler_params=_cparams("parallel"),
    )(du, du, cw)


def _adamw(parts, w, m, v, name, tr):
    r, cols = w.shape

    def body(p_ref, w_ref, m_ref, v_ref, g_ref, d_ref, mo_ref, vo_ref):
        g = p_ref[0].astype(F32)
        for s in range(1, N_DEV):
            g = g + p_ref[s].astype(F32)
        mm = ADAM_B1 * m_ref[...] + (1.0 - ADAM_B1) * g
        vv = ADAM_B2 * v_ref[...] + (1.0 - ADAM_B2) * (g * g)
        m_hat = mm / (1.0 - ADAM_B1 ** ADAM_STEP)
        v_hat = vv / (1.0 - ADAM_B2 ** ADAM_STEP)
        g_ref[...] = g
        d_ref[...] = -ADAM_LR * (m_hat / (jnp.sqrt(v_hat) + ADAM_EPS) + ADAM_WD * w_ref[...])
        mo_ref[...] = mm
        vo_ref[...] = vv

    assert r % tr == 0
    row = pl.BlockSpec((tr, cols), lambda i: (i, 0))
    return pl.pallas_call(
        body, name=name, grid=(r // tr,),
        in_specs=[pl.BlockSpec((N_DEV, tr, cols), lambda i: (0, i, 0)), row, row, row],
        out_specs=[row, row, row, row],
        out_shape=[jax.ShapeDtypeStruct((r, cols), F32)] * 4,
        compiler_params=_cparams("parallel"),
    )(parts, w, m, v)


def _mesh_pos():
    return lax.axis_index("x"), lax.axis_index("y"), lax.axis_index("c")


def _peer(pos, k):
    x, y, c = pos
    return (x ^ ((k >> 2) & 1), y ^ ((k >> 1) & 1), c ^ (k & 1))


def _flat_id(pos):
    return 4 * pos[0] + 2 * pos[1] + pos[2]


def _exchange_copies(srcs, dsts, scatter, send_sems, recv_sems, loc_sems):
    pos = _mesh_pos()
    me = _flat_id(pos)
    local, remote = [], []
    for j, (src, dst) in enumerate(zip(srcs, dsts)):
        local.append(pltpu.make_async_copy(src.at[me] if scatter[j] else src, dst.at[me], loc_sems.at[j]))
        for k in range(1, N_DEV):
            to = _peer(pos, k)
            remote.append(pltpu.make_async_remote_copy(
                src_ref=src.at[_flat_id(to)] if scatter[j] else src, dst_ref=dst.at[me],
                send_sem=send_sems.at[j, k - 1], recv_sem=recv_sems.at[j, k - 1],
                device_id=to, device_id_type=pl.DeviceIdType.MESH))
    return local, remote


def _exchange_shapes(arrays, scatter):
    return [jax.ShapeDtypeStruct(a.shape if s else (N_DEV,) + a.shape, a.dtype) for a, s in zip(arrays, scatter)]


def _exchange_sems(n):
    return [pltpu.SemaphoreType.DMA((n, N_DEV - 1)), pltpu.SemaphoreType.DMA((n, N_DEV - 1)), pltpu.SemaphoreType.DMA((n,))]


def _exchange(arrays, scatter, name):
    n = len(arrays)
    any_spec = pl.BlockSpec(memory_space=pl.ANY)

    def body(*refs):
        local, remote = _exchange_copies(refs[:n], refs[n:2 * n], scatter, *refs[2 * n:])
        for cp in local + remote:
            cp.start()
        for cp in remote:
            cp.wait()
        for cp in local:
            cp.wait()

    return pl.pallas_call(
        body, name=name, in_specs=[any_spec] * n, out_specs=[any_spec] * n,
        out_shape=_exchange_shapes(arrays, scatter), scratch_shapes=_exchange_sems(n),
    )(*arrays)


def _pad_rows(a, rows):
    return jnp.pad(a, ((0, rows - a.shape[0]),) + ((0, 0),) * (a.ndim - 1))


PACK_UNIT = SUBLANES * LANES


def _pack_lanes(parts, rows):
    out = []
    for a in parts:
        f = a.reshape(-1)
        out.append(jnp.pad(f, (0, (-f.shape[0]) % PACK_UNIT)).reshape(-1, LANES))
    packed = jnp.concatenate(out, axis=0)
    assert packed.shape[0] == rows, (packed.shape, rows)
    return packed


def _unpack_lanes(buf, shapes):
    out, r0 = [], 0
    for shp in shapes:
        n = math.prod(shp)
        nr = -(-n // PACK_UNIT) * SUBLANES
        out.append(buf[r0:r0 + nr].reshape(-1)[:n].reshape(shp))
        r0 += nr
    return out


def _col_shards(g):
    r, n = g.shape
    return g.reshape(r, N_DEV, n // N_DEV).transpose(1, 0, 2)


def _col_unshard(s):
    _, r, w = s.shape
    return s.transpose(1, 0, 2).reshape(r, N_DEV * w)


def _lane_rows(flat):
    n = flat.shape[1]
    return jnp.pad(flat, ((0, 0), (0, (-n) % PACK_UNIT))).reshape(N_DEV, -1, LANES)


SMALL_ROWS = 128
WS_ROWS = 32


def kernel(x, norm_mix_w, w_in, conv_qkv_w, a_log, dt_bias, gdn_norm_w, w_branch_a, w_branch_b, rel_bias, w_out, norm_ffn_w, w_up, conv_ffn_w, conv_ffn_b, w_down, norm_final_w, loss_target, m_norm_mix_w, m_w_in, m_conv_qkv_w, m_a_log, m_dt_bias, m_gdn_norm_w, m_w_branch_a, m_w_branch_b, m_rel_bias, m_w_out, m_norm_ffn_w, m_w_up, m_conv_ffn_w, m_conv_ffn_b, m_w_down, m_norm_final_w, v_norm_mix_w, v_w_in, v_conv_qkv_w, v_a_log, v_dt_bias, v_gdn_norm_w, v_w_branch_a, v_w_branch_b, v_rel_bias, v_w_out, v_norm_ffn_w, v_w_up, v_conv_ffn_w, v_conv_ffn_b, v_w_down, v_norm_final_w):
    big_w = (w_in, w_branch_a, w_branch_b, w_out, w_up, w_down, conv_qkv_w, conv_ffn_w)
    big_m = (m_w_in, m_w_branch_a, m_w_branch_b, m_w_out, m_w_up, m_w_down, m_conv_qkv_w, m_conv_ffn_w)
    big_v = (v_w_in, v_w_branch_a, v_w_branch_b, v_w_out, v_w_up, v_w_down, v_conv_qkv_w, v_conv_ffn_w)
    small_w = (norm_mix_w, a_log, dt_bias, gdn_norm_w, rel_bias, norm_ffn_w, conv_ffn_b, norm_final_w)
    small_m = (m_norm_mix_w, m_a_log, m_dt_bias, m_gdn_norm_w, m_rel_bias, m_norm_ffn_w, m_conv_ffn_b, m_norm_final_w)
    small_v = (v_norm_mix_w, v_a_log, v_dt_bias, v_gdn_norm_w, v_rel_bias, v_norm_ffn_w, v_conv_ffn_b, v_norm_final_w)

    xs, tgt = x[0], loss_target[0]
    ws = _pack_lanes(big_w[6:], WS_ROWS)
    g_in, gs = _exchange([w_in[0].astype(BF16), ws], (False, False), "all_gather_in")
    win = _col_unshard(g_in)
    gs = gs.reshape(N_DEV, -1)
    cqkv = gs[:, :GDN_CONV * 192].reshape(N_DEV, GDN_CONV, 192).transpose(1, 0, 2).reshape(GDN_CONV, 3 * KEY_A)
    cffn = gs[:, PACK_UNIT:PACK_UNIT + FFN_CONV * 704].reshape(N_DEV, FFN_CONV, 704).transpose(1, 0, 2).reshape(FFN_CONV, FFN_W)
    cffn = _pad_rows(cffn, SUBLANES)
    w_all = jnp.concatenate([win[:, a:b] for a, b in W_IN_ORDER] + [jnp.zeros((D_MODEL, PROJ_W - D_IN), BF16)], axis=1)
    par = _pad_rows(jnp.pad(jnp.concatenate([a_log, dt_bias], axis=0), ((0, 0), (0, LANES - GDN_HEADS))), SUBLANES)
    table = jnp.pad(rel_bias[0], ((0, 0), (0, 3 * LANES - rel_bias.shape[-1]))).reshape(ATT_HEADS, 1, 3 * LANES)

    h1 = _rmsnorm_cast(xs, norm_mix_w, "norm_mix")
    proj, g_ba, g_bb, g_out, g_up, g_down = _mm_nn(
        h1, w_all, F32, "in_proj", MM_TM, 1152, D_MODEL, carry=([w[0].astype(BF16) for w in big_w[1:6]], (False,) * 5))
    wba, wbb, wup = _col_unshard(g_ba), _col_unshard(g_bb), _col_unshard(g_up)
    wout = g_out.reshape(D_MODEL, D_MODEL)
    wdown = g_down.reshape(D_FF, D_MODEL)
    qn, kn, va = _gdn_prep_fwd(proj, cqkv)
    oan, o_gdn, sprev, wst, ust, tst = _gdn_fwd(qn, kn, va, proj, par, gdn_norm_w)
    bias_q, bias_k = _att_bias(table)
    ob, lse, lse_t = _att_fwd(proj, bias_q)
    x2 = _merge_fwd(oan, ob, proj, xs, wba, wbb, wout)
    h2 = _rmsnorm_cast(x2, norm_ffn_w, "norm_ffn")
    up = _mm_nn(h2, wup, F32, "ffn_up", MM_TM, 1408, D_MODEL)
    dx3, dx3b, act, tail_sums = _ffn_tail(up, cffn, conv_ffn_b, wdown, x2, tgt, norm_final_w.reshape(1, D_MODEL))

    g_wdown = _mm_tn(act, dx3b, "dw_down", 512)
    du, ffn_sums = _ffn_bwd_act(dx3b, wdown.T, up, cffn, conv_ffn_b)
    dup = _ffn_bwd_conv(du, cffn)
    g_wup = _mm_tn(h2, dup, "dw_up", 1408)
    dx2, dx2b, nffn_sums, r_up, r_down = _mm_rms_bwd(dup, wup.T, x2, norm_ffn_w, dx3, "ffn_up_bwd", MM_TM, 1408, carry=(
        [_col_shards(g_wup).astype(BF16), g_wdown.reshape(N_DEV, -1, D_MODEL).astype(BF16)], (True, True)))
    dproj, d_oan, d_ob, mixb, dya, dyb = _merge_bwd(dx2b, oan, ob, proj, wba, wbb, wout.T, wba.T, wbb.T)
    g_wout = _mm_tn(mixb, dx2b, "dw_out", 512)
    g_wba = _mm_tn(oan, dya, "dw_branch_a", 512)
    g_wbb = _mm_tn(ob, dyb, "dw_branch_b", 512)
    dproj, dlt_t, slabs = _att_dq(proj, bias_q, lse, d_ob, dproj)
    dproj = _att_dkv(proj, bias_k, lse_t, dlt_t, d_ob, dproj)
    g_rel = _relbias_grad(slabs)[:, 0, :rel_bias.shape[-1]]
    dqn, dkn, dva, dproj, gdn_sums = _gdn_bwd(qn, kn, va, proj, par, gdn_norm_w, o_gdn, sprev, wst, ust, tst, d_oan, dproj)
    dproj, cq_sums = _gdn_prep_bwd(proj, cqkv, dqn, dkn, dva, dproj)
    g_wall = _mm_tn(h1, dproj, "dw_in", 1152)
    starts = np.cumsum([0] + [b - a for a, b in W_IN_ORDER])
    g_win = jnp.concatenate([g_wall[:, starts[i]:starts[i + 1]] for i in np.argsort([a for a, _ in W_IN_ORDER])], axis=1)
    g_conv = jnp.concatenate([_lane_rows(_col_shards(cq_sums[:GDN_CONV]).reshape(N_DEV, -1)),
                              _lane_rows(_col_shards(ffn_sums[:FFN_CONV]).reshape(N_DEV, -1))], axis=1)
    grad_x, _, nmix_sums, r_in, r_ba, r_bb, r_out, r_conv = _mm_rms_bwd(
        dproj, w_all.T, xs, norm_mix_w, dx2, "in_proj_bwd", MM_TM, 1152, carry=(
            [_col_shards(g_win).astype(BF16), _col_shards(g_wba).astype(BF16), _col_shards(g_wbb).astype(BF16),
             g_wout.reshape(N_DEV, -1, D_MODEL).astype(BF16), g_conv], (True,) * 5))

    small_g = (nmix_sums[0:1], gdn_sums[0:1, :GDN_HEADS], gdn_sums[1:2, :GDN_HEADS], gdn_sums[2:3], g_rel,
               nffn_sums[0:1], ffn_sums[FFN_CONV:FFN_CONV + 1], tail_sums[0:1], tail_sums[1:2, 0:1])
    r_small, = _exchange([_pack_lanes(small_g, SMALL_ROWS)], (False,), "all_gather_small_grads")
    recv = (r_in, r_ba, r_bb, r_out, r_up, r_down, r_conv, r_small)

    res = {}
    for i, (nm, tr) in enumerate((("w_in", 128), ("w_branch_a", KEY_A), ("w_branch_b", WIDTH_B), ("w_out", 128),
                                  ("w_up", 128), ("w_down", 176))):
        res[nm] = [o[None] for o in _adamw(recv[i], big_w[i][0], big_m[i][0], big_v[i][0], "adamw_" + nm, tr)]
    conv = _adamw(recv[6], _pack_lanes(big_w[6:], WS_ROWS), _pack_lanes(big_m[6:], WS_ROWS), _pack_lanes(big_v[6:], WS_ROWS),
                  "adamw_conv", WS_ROWS)
    conv = [_unpack_lanes(o, [w.shape for w in big_w[6:]]) for o in conv]
    res["conv_qkv_w"] = [o[0] for o in conv]
    res["conv_ffn_w"] = [o[1] for o in conv]
    small_shapes = [w.shape for w in small_w]
    zero = jnp.zeros((1,), F32)
    small = _adamw(recv[7], _pack_lanes(small_w + (zero,), SMALL_ROWS), _pack_lanes(small_m + (zero,), SMALL_ROWS),
                   _pack_lanes(small_v + (zero,), SMALL_ROWS), "adamw_replicated", SMALL_ROWS)
    small = [_unpack_lanes(o, small_shapes + [()]) for o in small]
    loss = small[0][-1]
    for j, nm in enumerate(("norm_mix_w", "a_log", "dt_bias", "gdn_norm_w", "rel_bias", "norm_ffn_w", "conv_ffn_b",
                            "norm_final_w")):
        res[nm] = [o[j] for o in small]

    names = ("norm_mix_w", "w_in", "conv_qkv_w", "a_log", "dt_bias", "gdn_norm_w", "w_branch_a", "w_branch_b", "rel_bias",
             "w_out", "norm_ffn_w", "w_up", "conv_ffn_w", "conv_ffn_b", "w_down", "norm_final_w")
    outs = [res[n][kind] for kind in range(4) for n in names]
    return (loss, grad_x[None], *outs)
```

```python
import functools
import math

import numpy as np
import jax
import jax.numpy as jnp
from jax import lax
from jax.experimental import pallas as pl
from jax.experimental.pallas import tpu as pltpu

F32, BF16 = jnp.float32, jnp.bfloat16
HIGHEST = lax.Precision.HIGHEST

N_DEV = 8
D_MODEL = 1024
CHUNK = 64
EPS = 1e-6
GDN_HEADS, GDN_DK = 4, 128
KEY_A = GDN_HEADS * GDN_DK
GDN_CONV = 4
ATT_HEADS, ATT_DH = 8, 64
WIDTH_B = ATT_HEADS * ATT_DH
ATT_BAND = 9
REL_CLIP = 128
D_FF = 2816
FFN_CONV = 3
D_IN = 5640
ADAM_LR, ADAM_B1, ADAM_B2, ADAM_EPS, ADAM_WD, ADAM_STEP = 0.001, 0.9, 0.999, 1e-08, 0.01, 10

LANES = 128
SUBLANES = 8
NEG = -1e30

PROJ_W = 5760
PB = 512
CB_GA, CB_GB = 0, 1
CB_KB, CB_VB, CB_QA, CB_KA, CB_VA, CB_QB, CB_ZA = 4, 5, 6, 7, 8, 9, 10
CB_BD = 44
DP_GATES, DP_KVB, DP_QKVA, DP_QB, DP_ZBD = (2048, 0), (1024, 2), (1536, 2), (512, 9), (640, 8)
W_IN_ORDER = ((3592, 5640), (2568, 3592), (0, 1536), (2056, 2568), (1536, 2048), (2048, 2056))

ATT_QB = 256
ATT_KW = 768
ATT_VEC = 1024


def _dot(a, b, precision=None):
    return jnp.dot(a, b, preferred_element_type=F32, precision=precision)


def _dot_nt(a, b, precision=None):
    return lax.dot_general(a, b, (((1,), (1,)), ((), ())), preferred_element_type=F32, precision=precision)


def _dot_tn(a, b):
    return lax.dot_general(a, b, (((0,), (0,)), ((), ())), preferred_element_type=F32)


def _split(a):
    hi = a.astype(BF16)
    return hi, (a - hi.astype(F32)).astype(BF16)


def _dot3s(a, b):
    return _dot(a[0], b[0]) + (_dot(a[0], b[1]) + _dot(a[1], b[0]))


def _sigmoid(x):
    return 0.5 * jnp.tanh(0.5 * x) + 0.5


def _softplus(x):
    return jnp.maximum(x, 0.0) + jnp.log(1.0 + jnp.exp(-jnp.abs(x)))


def _cparams(*sem):
    return pltpu.CompilerParams(dimension_semantics=tuple(sem))


def _dp_spec(tm, region, index=lambda i: i):
    width, cb = region
    return pl.BlockSpec((tm, width), lambda i: (index(i), cb))


def _rmsnorm_cast(x, w, name, tm=512, carry=((), ())):
    t, d = x.shape
    nt = t // tm
    arrays, scatter = carry
    nx = len(arrays)

    def body(*refs):
        x_ref, w_ref = refs[:2]
        o_ref = refs[2 + nx]
        i = pl.program_id(0)
        if nx:
            local, remote = _exchange_copies(refs[2:2 + nx], refs[3 + nx:3 + 2 * nx], scatter, *refs[3 + 2 * nx:])

            @pl.when(i == 0)
            def _():
                for cp in local + remote:
                    cp.start()

        xv = x_ref[...]
        r = lax.rsqrt(jnp.mean(xv * xv, axis=-1, keepdims=True) + EPS)
        o_ref[...] = (xv * r * w_ref[...]).astype(BF16)

        if nx:
            @pl.when(i == nt - 1)
            def _():
                for cp in remote + local:
                    cp.wait()

    any_spec = pl.BlockSpec(memory_space=pl.ANY)
    out = pl.pallas_call(
        body, name=name, grid=(nt,),
        in_specs=[pl.BlockSpec((tm, d), lambda i: (i, 0)), pl.BlockSpec((1, d), lambda i: (0, 0))] + [any_spec] * nx,
        out_specs=[pl.BlockSpec((tm, d), lambda i: (i, 0))] + [any_spec] * nx,
        out_shape=[jax.ShapeDtypeStruct((t, d), BF16)] + _exchange_shapes(arrays, scatter),
        scratch_shapes=_exchange_sems(nx) if nx else [],
        compiler_params=_cparams("arbitrary" if nx else "parallel"),
    )(x, w, *arrays)
    return out if nx else out[0]


def _mm_nn(a, b, out_dtype, name, tm, tn, tk, carry=((), ())):
    m, k = a.shape
    _, n = b.shape
    tm = min(tm, m)
    nk = k // tk
    assert m % tm == 0 and n % tn == 0 and k % tk == 0
    arrays, scatter = carry
    nx = len(arrays)
    gm, gn = m // tm, n // tn

    def body(*refs):
        a_ref, b_ref = refs[:2]
        srcs = refs[2:2 + nx]
        o_ref = refs[2 + nx]
        dsts = refs[3 + nx:3 + 2 * nx]
        rest = refs[3 + 2 * nx:]
        i, j, kk = pl.program_id(0), pl.program_id(1), pl.program_id(2)
        if nx:
            local, remote = _exchange_copies(srcs, dsts, scatter, *rest[-3:])

            @pl.when((i == 0) & (j == 0) & (kk == 0))
            def _():
                for cp in local + remote:
                    cp.start()

        if nk == 1:
            o_ref[...] = _dot(a_ref[...], b_ref[...]).astype(out_dtype)
        else:
            acc_ref = rest[0]

            @pl.when(kk == 0)
            def _():
                acc_ref[...] = jnp.zeros_like(acc_ref)

            acc_ref[...] += _dot(a_ref[...], b_ref[...])

            @pl.when(kk == nk - 1)
            def _():
                o_ref[...] = acc_ref[...].astype(out_dtype)

        if nx:
            @pl.when((i == gm - 1) & (j == gn - 1) & (kk == nk - 1))
            def _():
                for cp in remote + local:
                    cp.wait()

    any_spec = pl.BlockSpec(memory_space=pl.ANY)
    scratch = ([pltpu.VMEM((tm, tn), F32)] if nk > 1 else []) + (_exchange_sems(nx) if nx else [])
    out = pl.pallas_call(
        body, name=name, grid=(gm, gn, nk),
        in_specs=[pl.BlockSpec((tm, tk), lambda i, j, kk: (i, kk)),
                  pl.BlockSpec((tk, tn), lambda i, j, kk: (kk, j))] + [any_spec] * nx,
        out_specs=[pl.BlockSpec((tm, tn), lambda i, j, kk: (i, j))] + [any_spec] * nx,
        out_shape=[jax.ShapeDtypeStruct((m, n), out_dtype)] + _exchange_shapes(arrays, scatter),
        scratch_shapes=scratch,
        compiler_params=_cparams(*(("arbitrary",) * 3 if nx else ("parallel", "parallel", "arbitrary"))),
    )(a, b, *arrays)
    return out if nx else out[0]


def _mm_rms_bwd(a, b, x, w, dres, name, tm, tk, carry):
    m, k = a.shape
    _, n = b.shape
    nk = k // tk
    gm = m // tm
    assert m % tm == 0 and k % tk == 0
    arrays, scatter = carry
    nx = len(arrays)

    def body(*refs):
        a_ref, b_ref, x_ref, w_ref, dres_ref = refs[:5]
        srcs = refs[5:5 + nx]
        dx_ref, dxb_ref, dw_ref = refs[5 + nx:8 + nx]
        dsts = refs[8 + nx:8 + 2 * nx]
        acc_ref = refs[8 + 2 * nx]
        i, kk = pl.program_id(0), pl.program_id(1)
        local, remote = _exchange_copies(srcs, dsts, scatter, *refs[9 + 2 * nx:])

        @pl.when((i == 0) & (kk == 0))
        def _():
            for cp in local + remote:
                cp.start()
            dw_ref[...] = jnp.zeros_like(dw_ref)

        @pl.when(kk == 0)
        def _():
            acc_ref[...] = jnp.zeros_like(acc_ref)

        acc_ref[...] += _dot(a_ref[...], b_ref[...])

        @pl.when(kk == nk - 1)
        def _():
            dhv = acc_ref[...]
            xv = x_ref[...]
            r = lax.rsqrt(jnp.mean(xv * xv, axis=-1, keepdims=True) + EPS)
            xh = xv * r
            dw_ref[0:1, :] += jnp.sum(dhv * xh, axis=0, keepdims=True)
            dxh = dhv * w_ref[...]
            dx = dres_ref[...] + r * (dxh - xh * jnp.mean(dxh * xh, axis=-1, keepdims=True))
            dx_ref[...] = dx
            dxb_ref[...] = dx.astype(BF16)

        @pl.when((i == gm - 1) & (kk == nk - 1))
        def _():
            for cp in remote + local:
                cp.wait()

    any_spec = pl.BlockSpec(memory_space=pl.ANY)
    row = pl.BlockSpec((tm, n), lambda i, kk: (i, 0))
    return pl.pallas_call(
        body, name=name, grid=(gm, nk),
        in_specs=[pl.BlockSpec((tm, tk), lambda i, kk: (i, kk)), pl.BlockSpec((tk, n), lambda i, kk: (kk, 0)),
                  row, pl.BlockSpec((1, n), lambda i, kk: (0, 0)), row] + [any_spec] * nx,
        out_specs=[row, row, pl.BlockSpec((SUBLANES, n), lambda i, kk: (0, 0))] + [any_spec] * nx,
        out_shape=[jax.ShapeDtypeStruct((m, n), F32), jax.ShapeDtypeStruct((m, n), BF16),
                   jax.ShapeDtypeStruct((SUBLANES, n), F32)] + _exchange_shapes(arrays, scatter),
        scratch_shapes=[pltpu.VMEM((tm, n), F32)] + _exchange_sems(nx),
        compiler_params=_cparams("arbitrary", "arbitrary"),
    )(a, b, x, w, dres, *arrays)


MM_TM = 1024


def _mm_tn(a, b, name, tn, tk=2 * MM_TM):
    t, m = a.shape
    _, n = b.shape
    tk = min(tk, t)
    assert t % tk == 0 and n % tn == 0

    def body(a_ref, b_ref, o_ref):
        @pl.when(pl.program_id(1) == 0)
        def _():
            o_ref[...] = jnp.zeros_like(o_ref)

        o_ref[...] += _dot_tn(a_ref[...], b_ref[...])

    return pl.pallas_call(
        body, name=name, grid=(n // tn, t // tk),
        in_specs=[pl.BlockSpec((tk, m), lambda j, s: (s, 0)),
                  pl.BlockSpec((tk, tn), lambda j, s: (s, j))],
        out_specs=pl.BlockSpec((m, tn), lambda j, s: (0, j)),
        out_shape=jax.ShapeDtypeStruct((m, n), F32),
        compiler_params=_cparams("parallel", "arbitrary"),
    )(a, b)


def _rel_index(dist):
    return np.clip(dist, -REL_CLIP, REL_CLIP) + REL_CLIP


def _bias_onehots():
    tw = 3 * LANES
    m = np.arange(ATT_VEC)
    dq = np.where(m <= ATT_KW, 512 - m, 512 - (m - ATT_VEC))
    dk = np.where(m < ATT_KW, m, m - ATT_VEC)
    ohq = np.zeros((tw, ATT_VEC), np.float32)
    ohk = np.zeros((tw, ATT_VEC), np.float32)
    ohq[_rel_index(dq), m] = 1.0
    ohk[_rel_index(dk), m] = 1.0
    return ohq, ohk


def _att_bias(table_pad):
    ohq, ohk = _bias_onehots()
    nslab = ATT_QB // SUBLANES

    def body(t_ref, ohq_ref, ohk_ref, bq_ref, bk_ref):
        tv = jnp.broadcast_to(t_ref[...], (SUBLANES, 3 * LANES))
        row = lax.broadcasted_iota(jnp.int32, (ATT_QB, ATT_KW), 0) // CHUNK
        col = lax.broadcasted_iota(jnp.int32, (ATT_QB, ATT_KW), 1) // CHUNK
        band = (col >= row) & (col <= row + ATT_BAND - 1)
        for oh_ref, out_ref in ((ohq_ref, bq_ref), (ohk_ref, bk_ref)):
            vec = _dot(tv, oh_ref[...], HIGHEST)[0:1, :]
            slab = jnp.concatenate([vec if b == 0 else pltpu.roll(vec, b, 1) for b in range(SUBLANES)], axis=0)
            rows = [slab if a == 0 else pltpu.roll(slab, SUBLANES * a, 1) for a in range(nslab)]
            full = jnp.concatenate(rows, axis=0)[:, :ATT_KW]
            out_ref[...] = jnp.where(band, full, NEG)

    h = table_pad.shape[0]
    oh_spec = pl.BlockSpec((3 * LANES, ATT_VEC), lambda i: (0, 0))
    out_spec = pl.BlockSpec((None, ATT_QB, ATT_KW), lambda i: (i, 0, 0))
    return pl.pallas_call(
        body, name="att_bias", grid=(h,),
        in_specs=[pl.BlockSpec((None, 1, 3 * LANES), lambda i: (i, 0, 0)), oh_spec, oh_spec],
        out_specs=[out_spec, out_spec],
        out_shape=[jax.ShapeDtypeStruct((h, ATT_QB, ATT_KW), F32)] * 2,
        compiler_params=_cparams("parallel"),
    )(table_pad, jnp.asarray(ohq), jnp.asarray(ohk))


def _head_masks():
    lane = lax.broadcasted_iota(jnp.int32, (1, LANES), 1)
    return [lane < ATT_DH, lane >= ATT_DH]


def _att_fwd(proj, bias_q):
    t = proj.shape[0]
    nb = t // ATT_QB
    scale = ATT_DH ** -0.5

    def body(q_ref, k0_ref, k1_ref, k2_ref, v0_ref, v1_ref, v2_ref, b_ref, o_ref, lse_ref, lset_ref):
        i = pl.program_id(0)
        q = (q_ref[...] * scale).astype(BF16)
        kk = jnp.concatenate([k0_ref[...], k1_ref[...], k2_ref[...]], axis=0).astype(BF16)
        vv = jnp.concatenate([v0_ref[...], v1_ref[...], v2_ref[...]], axis=0).astype(BF16)
        kpos = lax.broadcasted_iota(jnp.int32, (1, ATT_KW), 1) + (i - 2) * ATT_QB
        valid = kpos >= 0
        lane = lax.broadcasted_iota(jnp.int32, (1, LANES), 1)
        masks = _head_masks()
        lse_cols = jnp.zeros((ATT_QB, LANES), F32)
        for p in range(ATT_HEADS // 2):
            cs = slice(p * LANES, (p + 1) * LANES)
            qt, kt, vt = q[:, cs], kk[:, cs], vv[:, cs]
            acc = jnp.zeros((ATT_QB, LANES), F32)
            for sub in range(2):
                h = 2 * p + sub
                s = _dot_nt(jnp.where(masks[sub], qt, 0), kt) + b_ref[h]
                s = jnp.where(valid, s, NEG)
                mx = jnp.max(s, axis=-1, keepdims=True)
                e = jnp.exp(s - mx)
                l = jnp.sum(e, axis=-1, keepdims=True)
                acc = acc + _dot(e.astype(BF16), jnp.where(masks[sub], vt, 0)) * (1.0 / l)
                lse_cols = lse_cols + jnp.where(lane == h, mx + jnp.log(l), 0.0)
            o_ref[:, cs] = acc.astype(BF16)
        lse_ref[...] = lse_cols
        lset_ref[...] = lse_cols.T[0:SUBLANES, :]

    def kv_spec(off, cb):
        return pl.BlockSpec((ATT_QB, PB), lambda i: (jnp.maximum(i + off, 0), cb))

    return pl.pallas_call(
        body, name="att_fwd", grid=(nb,),
        in_specs=[pl.BlockSpec((ATT_QB, PB), lambda i: (i, CB_QB)),
                  kv_spec(-2, CB_KB), kv_spec(-1, CB_KB), kv_spec(0, CB_KB),
                  kv_spec(-2, CB_VB), kv_spec(-1, CB_VB), kv_spec(0, CB_VB),
                  pl.BlockSpec((ATT_HEADS, ATT_QB, ATT_KW), lambda i: (0, 0, 0))],
        out_specs=[pl.BlockSpec((ATT_QB, WIDTH_B), lambda i: (i, 0)),
                   pl.BlockSpec((ATT_QB, LANES), lambda i: (i, 0)),
                   pl.BlockSpec((SUBLANES, ATT_QB), lambda i: (0, i))],
        out_shape=[jax.ShapeDtypeStruct((t, WIDTH_B), BF16), jax.ShapeDtypeStruct((t, LANES), F32),
                   jax.ShapeDtypeStruct((SUBLANES, t), F32)],
        compiler_params=_cparams("parallel"),
    )(proj, proj, proj, proj, proj, proj, proj, bias_q)


def _att_dq(proj, bias_q, lse, d_ob, dproj):
    t = proj.shape[0]
    nb = t // ATT_QB
    scale = ATT_DH ** -0.5
    nslab = ATT_QB // SUBLANES

    def body(q_ref, k0_ref, k1_ref, k2_ref, v0_ref, v1_ref, v2_ref, b_ref, lse_ref, do_ref, dp_in_ref,
             dq_ref, dlt_ref, slab_ref):
        i = pl.program_id(0)

        @pl.when(i == 0)
        def _():
            slab_ref[...] = jnp.zeros_like(slab_ref)

        q = (q_ref[...] * scale).astype(BF16)
        kk = jnp.concatenate([k0_ref[...], k1_ref[...], k2_ref[...]], axis=0).astype(BF16)
        vv = jnp.concatenate([v0_ref[...], v1_ref[...], v2_ref[...]], axis=0).astype(BF16)
        do = do_ref[...].astype(BF16)
        kpos = lax.broadcasted_iota(jnp.int32, (1, ATT_KW), 1) + (i - 2) * ATT_QB
        valid = kpos >= 0
        lane = lax.broadcasted_iota(jnp.int32, (1, LANES), 1)
        masks = _head_masks()
        lse_all = lse_ref[...]
        dlt_cols = jnp.zeros((ATT_QB, LANES), F32)
        zpad = jnp.zeros((SUBLANES, ATT_VEC - ATT_KW), F32)
        for p in range(ATT_HEADS // 2):
            cs = slice(p * LANES, (p + 1) * LANES)
            qt, kt, vt, dot_ = q[:, cs], kk[:, cs], vv[:, cs], do[:, cs]
            acc = jnp.zeros((ATT_QB, LANES), F32)
            for sub in range(2):
                h = 2 * p + sub
                s = _dot_nt(jnp.where(masks[sub], qt, 0), kt) + b_ref[h]
                s = jnp.where(valid, s, NEG)
                pr = jnp.exp(s - lse_all[:, h:h + 1])
                dp = _dot_nt(jnp.where(masks[sub], dot_, 0), vt)
                dl = jnp.sum(pr * dp, axis=-1, keepdims=True)
                ds = pr * (dp - dl)
                acc = acc + _dot(ds.astype(BF16), jnp.where(masks[sub], kt, 0)) * scale
                dlt_cols = dlt_cols + jnp.where(lane == h, dl, 0.0)
                sl = jnp.zeros((SUBLANES, ATT_VEC), F32)
                for a in range(nslab):
                    piece = jnp.concatenate([ds[a * SUBLANES:(a + 1) * SUBLANES, :], zpad], axis=1)
                    sl = sl + (piece if a == 0 else pltpu.roll(piece, ATT_VEC - SUBLANES * a, 1))
                slab_ref[h] += sl
            dq_ref[:, cs] = acc.astype(BF16)
        dlt_ref[...] = dlt_cols.T[0:SUBLANES, :]

    def kv_spec(off, cb):
        return pl.BlockSpec((ATT_QB, PB), lambda i: (jnp.maximum(i + off, 0), cb))

    return pl.pallas_call(
        body, name="att_dq", grid=(nb,),
        in_specs=[pl.BlockSpec((ATT_QB, PB), lambda i: (i, CB_QB)),
                  kv_spec(-2, CB_KB), kv_spec(-1, CB_KB), kv_spec(0, CB_KB),
                  kv_spec(-2, CB_VB), kv_spec(-1, CB_VB), kv_spec(0, CB_VB),
                  pl.BlockSpec((ATT_HEADS, ATT_QB, ATT_KW), lambda i: (0, 0, 0)),
                  pl.BlockSpec((ATT_QB, LANES), lambda i: (i, 0)),
                  pl.BlockSpec((ATT_QB, WIDTH_B), lambda i: (i, 0)), pl.BlockSpec(memory_space=pl.ANY)],
        out_specs=[_dp_spec(ATT_QB, DP_QB),
                   pl.BlockSpec((SUBLANES, ATT_QB), lambda i: (0, i)),
                   pl.BlockSpec((ATT_HEADS, SUBLANES, ATT_VEC), lambda i: (0, 0, 0))],
        out_shape=[jax.ShapeDtypeStruct(dproj.shape, dproj.dtype), jax.ShapeDtypeStruct((SUBLANES, t), F32),
                   jax.ShapeDtypeStruct((ATT_HEADS, SUBLANES, ATT_VEC), F32)],
        input_output_aliases={10: 0},
        compiler_params=_cparams("arbitrary"),
    )(proj, proj, proj, proj, proj, proj, proj, bias_q, lse, d_ob, dproj)


def _att_dkv(proj, bias_k, lse_t, dlt_t, d_ob, dproj):
    t = proj.shape[0]
    nb = t // ATT_QB
    scale = ATT_DH ** -0.5

    def body(k_ref, v_ref, q0_ref, q1_ref, q2_ref, d0_ref, d1_ref, d2_ref, l0_ref, l1_ref, l2_ref,
             e0_ref, e1_ref, e2_ref, b_ref, dp_in_ref, dkv_ref):
        i = pl.program_id(0)
        k = k_ref[...].astype(BF16)
        v = v_ref[...].astype(BF16)
        qq = (jnp.concatenate([q0_ref[...], q1_ref[...], q2_ref[...]], axis=0) * scale).astype(BF16)
        do = jnp.concatenate([d0_ref[...], d1_ref[...], d2_ref[...]], axis=0).astype(BF16)
        lse = jnp.concatenate([l0_ref[...], l1_ref[...], l2_ref[...]], axis=1)
        dlt = jnp.concatenate([e0_ref[...], e1_ref[...], e2_ref[...]], axis=1)
        qpos = lax.broadcasted_iota(jnp.int32, (1, ATT_KW), 1) + i * ATT_QB
        valid = qpos < t
        masks = _head_masks()
        for p in range(ATT_HEADS // 2):
            cs = slice(p * LANES, (p + 1) * LANES)
            kt, vt, qt, dot_ = k[:, cs], v[:, cs], qq[:, cs], do[:, cs]
            acc_k = jnp.zeros((ATT_QB, LANES), F32)
            acc_v = jnp.zeros((ATT_QB, LANES), F32)
            for sub in range(2):
                h = 2 * p + sub
                st = _dot_nt(jnp.where(masks[sub], kt, 0), qt) + b_ref[h]
                st = jnp.where(valid, st, NEG)
                pt = jnp.exp(st - lse[h:h + 1, :])
                dot_m = jnp.where(masks[sub], dot_, 0)
                acc_v = acc_v + _dot(pt.astype(BF16), dot_m)
                dpt = _dot_nt(jnp.where(masks[sub], vt, 0), dot_)
                dst = pt * (dpt - dlt[h:h + 1, :])
                acc_k = acc_k + _dot(dst.astype(BF16), jnp.where(masks[sub], qt, 0))
            dkv_ref[:, cs] = acc_k.astype(BF16)
            dkv_ref[:, WIDTH_B + p * LANES:WIDTH_B + (p + 1) * LANES] = acc_v.astype(BF16)

    def q_spec(off, cb):
        return pl.BlockSpec((ATT_QB, PB), lambda i: (jnp.minimum(i + off, nb - 1), cb))

    def d_spec(off):
        return pl.BlockSpec((ATT_QB, WIDTH_B), lambda i: (jnp.minimum(i + off, nb - 1), 0))

    def r_spec(off):
        return pl.BlockSpec((SUBLANES, ATT_QB), lambda i: (0, jnp.minimum(i + off, nb - 1)))

    row = pl.BlockSpec((ATT_QB, WIDTH_B), lambda i: (i, 0))
    return pl.pallas_call(
        body, name="att_dkv", grid=(nb,),
        in_specs=[pl.BlockSpec((ATT_QB, PB), lambda i: (i, CB_KB)), pl.BlockSpec((ATT_QB, PB), lambda i: (i, CB_VB)),
                  q_spec(0, CB_QB), q_spec(1, CB_QB), q_spec(2, CB_QB),
                  d_spec(0), d_spec(1), d_spec(2), r_spec(0), r_spec(1), r_spec(2),
                  r_spec(0), r_spec(1), r_spec(2),
                  pl.BlockSpec((ATT_HEADS, ATT_QB, ATT_KW), lambda i: (0, 0, 0)), pl.BlockSpec(memory_space=pl.ANY)],
        out_specs=_dp_spec(ATT_QB, DP_KVB),
        out_shape=jax.ShapeDtypeStruct(dproj.shape, dproj.dtype),
        input_output_aliases={15: 0},
        compiler_params=_cparams("parallel"),
    )(proj, proj, proj, proj, proj, d_ob, d_ob, d_ob, lse_t, lse_t, lse_t, dlt_t, dlt_t, dlt_t, bias_k, dproj)


def _relbias_grad(slabs):
    ohq, _ = _bias_onehots()

    def body(s_ref, oh_ref, o_ref):
        sv = s_ref[...]
        vec = sv[0:1, :]
        for b in range(1, SUBLANES):
            vec = vec + pltpu.roll(sv[b:b + 1, :], ATT_VEC - b, 1)
        o_ref[...] = _dot_nt(jnp.broadcast_to(vec, (SUBLANES, ATT_VEC)), oh_ref[...], HIGHEST)[0:1, :]

    h = slabs.shape[0]
    return pl.pallas_call(
        body, name="att_dbias", grid=(h,),
        in_specs=[pl.BlockSpec((None, SUBLANES, ATT_VEC), lambda i: (i, 0, 0)),
                  pl.BlockSpec((3 * LANES, ATT_VEC), lambda i: (0, 0))],
        out_specs=pl.BlockSpec((None, 1, 3 * LANES), lambda i: (i, 0, 0)),
        out_shape=jax.ShapeDtypeStruct((h, 1, 3 * LANES), F32),
        compiler_params=_cparams("parallel"),
    )(slabs, jnp.asarray(ohq))


GDN_TM = 512
GDN_CB = 4
HALO = SUBLANES


def _conv_taps(ext, width, lead, n):
    return [(ext if k == width - 1 else pltpu.roll(ext, width - 1 - k, 0))[lead:lead + n] for k in range(width)]


def _prev_halo_spec(tm, width, cb):
    return pl.BlockSpec((HALO, width), lambda i: (jnp.maximum(i * (tm // HALO) - 1, 0), cb))


def _next_halo_spec(tm, width, cb, t):
    return pl.BlockSpec((HALO, width), lambda i: (jnp.minimum((i + 1) * (tm // HALO), t // HALO - 1), cb))


def _gdn_prep_fwd(proj, conv_w):
    t = proj.shape[0]
    tm = GDN_TM

    def body(q_ref, k_ref, v_ref, hq_ref, hk_ref, hv_ref, w_ref, qn_ref, kn_ref, vo_ref):
        first = pl.program_id(0) == 0
        for idx, (x_ref, h_ref, o_ref) in enumerate(((q_ref, hq_ref, qn_ref), (k_ref, hk_ref, kn_ref),
                                                      (v_ref, hv_ref, vo_ref))):
            halo = jnp.where(first, 0.0, h_ref[...])
            ext = jnp.concatenate([halo, x_ref[...]], axis=0)
            w = w_ref[:, idx * KEY_A:(idx + 1) * KEY_A]
            taps = _conv_taps(ext, GDN_CONV, HALO, tm)
            y = sum(w[k:k + 1, :] * taps[k] for k in range(GDN_CONV))
            a = y * _sigmoid(y)
            if idx < 2:
                for h in range(GDN_HEADS):
                    cs = slice(h * GDN_DK, (h + 1) * GDN_DK)
                    seg = a[:, cs]
                    o_ref[:, cs] = seg * lax.rsqrt(jnp.sum(seg * seg, axis=-1, keepdims=True) + EPS)
            else:
                o_ref[...] = a

    row = pl.BlockSpec((tm, KEY_A), lambda i: (i, 0))
    return pl.pallas_call(
        body, name="gdn_prep_fwd", grid=(t // tm,),
        in_specs=[pl.BlockSpec((tm, PB), lambda i: (i, CB_QA)), pl.BlockSpec((tm, PB), lambda i: (i, CB_KA)),
                  pl.BlockSpec((tm, PB), lambda i: (i, CB_VA)),
                  _prev_halo_spec(tm, PB, CB_QA), _prev_halo_spec(tm, PB, CB_KA), _prev_halo_spec(tm, PB, CB_VA),
                  pl.BlockSpec((GDN_CONV, 3 * KEY_A), lambda i: (0, 0))],
        out_specs=[row, row, row],
        out_shape=[jax.ShapeDtypeStruct((t, KEY_A), F32)] * 3,
        compiler_params=_cparams("parallel"),
    )(proj, proj, proj, proj, proj, proj, conv_w)


def _gdn_prep_bwd(proj, conv_w, dqn, dkn, dv, dproj):
    t = proj.shape[0]
    tm = GDN_TM
    nt = t // tm
    n_ext = tm + HALO

    def body(q_ref, k_ref, v_ref, pq_ref, pk_ref, pv_ref, nq_ref, nk_ref, nv_ref,
             dq_ref, dk_ref, dv_ref, ndq_ref, ndk_ref, ndv_ref, w_ref, dp_in_ref, out_ref, dw_ref):
        i = pl.program_id(0)
        first, last = i == 0, i == nt - 1

        @pl.when(first)
        def _():
            dw_ref[...] = jnp.zeros_like(dw_ref)

        groups = ((q_ref, pq_ref, nq_ref, dq_ref, ndq_ref), (k_ref, pk_ref, nk_ref, dk_ref, ndk_ref),
                  (v_ref, pv_ref, nv_ref, dv_ref, ndv_ref))
        for idx, (x_ref, p_ref, n_ref, d_ref, nd_ref) in enumerate(groups):
            cs_all = slice(idx * KEY_A, (idx + 1) * KEY_A)
            ext = jnp.concatenate([jnp.where(first, 0.0, p_ref[...]), x_ref[...], jnp.where(last, 0.0, n_ref[...])], axis=0)
            w = w_ref[:, cs_all]
            taps = _conv_taps(ext, GDN_CONV, HALO, n_ext)
            y = sum(w[k:k + 1, :] * taps[k] for k in range(GDN_CONV))
            sg = _sigmoid(y)
            a = y * sg
            dup = jnp.concatenate([d_ref[...], jnp.where(last, 0.0, nd_ref[...])], axis=0)
            if idx < 2:
                segs = []
                for h in range(GDN_HEADS):
                    cs = slice(h * GDN_DK, (h + 1) * GDN_DK)
                    seg = a[:, cs]
                    r = lax.rsqrt(jnp.sum(seg * seg, axis=-1, keepdims=True) + EPS)
                    nrm = seg * r
                    dn = dup[:, cs]
                    segs.append(r * (dn - nrm * jnp.sum(dn * nrm, axis=-1, keepdims=True)))
                da = jnp.concatenate(segs, axis=1)
            else:
                da = dup
            dy = da * sg * (1.0 + y * (1.0 - sg))
            dx = sum(w[k:k + 1, :] * (dy if k == GDN_CONV - 1 else pltpu.roll(dy, n_ext - (GDN_CONV - 1 - k), 0))[:tm]
                     for k in range(GDN_CONV))
            out_ref[:, cs_all] = dx.astype(BF16)
            for k in range(GDN_CONV):
                dw_ref[k:k + 1, cs_all] += jnp.sum(dy[:tm] * taps[k][:tm], axis=0, keepdims=True)

    row = pl.BlockSpec((tm, KEY_A), lambda i: (i, 0))
    nrow = _next_halo_spec(tm, KEY_A, 0, t)
    return pl.pallas_call(
        body, name="gdn_prep_bwd", grid=(nt,),
        in_specs=[pl.BlockSpec((tm, PB), lambda i: (i, CB_QA)), pl.BlockSpec((tm, PB), lambda i: (i, CB_KA)),
                  pl.BlockSpec((tm, PB), lambda i: (i, CB_VA)),
                  _prev_halo_spec(tm, PB, CB_QA), _prev_halo_spec(tm, PB, CB_KA), _prev_halo_spec(tm, PB, CB_VA),
                  _next_halo_spec(tm, PB, CB_QA, t), _next_halo_spec(tm, PB, CB_KA, t), _next_halo_spec(tm, PB, CB_VA, t),
                  row, row, row, nrow, nrow, nrow,
                  pl.BlockSpec((GDN_CONV, 3 * KEY_A), lambda i: (0, 0)), pl.BlockSpec(memory_space=pl.ANY)],
        out_specs=[_dp_spec(tm, DP_QKVA), pl.BlockSpec((SUBLANES, 3 * KEY_A), lambda i: (0, 0))],
        out_shape=[jax.ShapeDtypeStruct(dproj.shape, dproj.dtype), jax.ShapeDtypeStruct((SUBLANES, 3 * KEY_A), F32)],
        input_output_aliases={16: 0},
        compiler_params=_cparams("arbitrary"),
    )(proj, proj, proj, proj, proj, proj, proj, proj, proj, dqn, dkn, dv, dqn, dkn, dv, conv_w, dproj)


class _Pair(dict):
    __getattr__ = dict.__getitem__
    __setattr__ = dict.__setitem__


def _pairs_to_lanes(cols):
    lane = lax.broadcasted_iota(jnp.int32, (1, LANES), 1)
    out = jnp.zeros((cols[0].shape[0], LANES), F32)
    for p, col in enumerate(cols):
        out = out + jnp.where(lane == p, col, 0.0)
    return out


def _gdn_terms(bd, par, kn_ref, qn_ref):
    c = CHUNK
    ii = lax.broadcasted_iota(jnp.int32, (c, c), 0)
    jj = lax.broadcasted_iota(jnp.int32, (c, c), 1)
    strict, incl = ii > jj, ii >= jj
    ltri = incl.astype(F32)
    ts = []
    for cc in range(GDN_CB):
        for h in range(GDN_HEADS):
            t = _Pair(cc=cc, h=h, rows=slice(cc * c, (cc + 1) * c), cs=slice(h * GDN_DK, (h + 1) * GDN_DK),
                      strict=strict, incl=incl)
            t.beta = _sigmoid(bd[t.rows, h:h + 1])
            t.ea = jnp.exp(par[0:1, h:h + 1])
            t.sp_arg = bd[t.rows, GDN_HEADS + h:GDN_HEADS + h + 1] + par[1:2, h:h + 1]
            t.g = -t.ea * _softplus(t.sp_arg)
            t.k = kn_ref[t.rows, t.cs]
            t.q = qn_ref[t.rows, t.cs] * (GDN_DK ** -0.5)
            t.kb, t.qb = t.k.astype(BF16), t.q.astype(BF16)
            ts.append(t)
    gall = _dot(ltri, _pairs_to_lanes([t.g for t in ts]), HIGHEST)
    for p, t in enumerate(ts):
        t.gb = jnp.broadcast_to(gall[:, p:p + 1], (c, GDN_DK))
    for t in ts:
        t.kk = _dot_nt(t.kb, t.kb)
        t.qk = _dot_nt(t.qb, t.kb)
    for t in ts:
        gc = t.gb[:, :c]
        diff = gc - gc.T
        t.dec_s = jnp.exp(jnp.where(strict, diff, NEG))
        t.dec_i = jnp.exp(jnp.where(incl, diff, NEG))
        t.gam = jnp.exp(t.gb)
        glast = t.gb[c - 1:c, :]
        t.e_rest = jnp.exp(glast - t.gb)
        t.gl = jnp.exp(glast)
        t.p = t.qk * t.dec_i
    return ts


def _gdn_fwd(qn, kn, v, proj, par, gnw):
    t = qn.shape[0]
    c = CHUNK
    nc = t // c
    r_ = GDN_CB * c

    def body(qn_ref, kn_ref, v_ref, bd_ref, z_ref, par_ref, gnw_ref,
             oan_ref, o_ref, sp_ref, w_ref, u_ref, tm_ref, s_ref):
        @pl.when(pl.program_id(0) == 0)
        def _():
            s_ref[...] = jnp.zeros_like(s_ref)

        bd, par, gnw_v = bd_ref[...], par_ref[...], gnw_ref[...]
        eye = (lax.broadcasted_iota(jnp.int32, (c, c), 0) == lax.broadcasted_iota(jnp.int32, (c, c), 1)).astype(F32)
        ts = _gdn_terms(bd, par, kn_ref, qn_ref)
        for t in ts:
            t.vv = v_ref[t.rows, t.cs]
            t.x = -(t.beta * t.kk * t.dec_s)
            t.tinv = eye + t.x
        for t in ts:
            t.xs = _split(t.x)
        for _ in range(5):
            for t in ts:
                t.xs = _split(_dot3s(t.xs, t.xs))
            for t in ts:
                t.tinv = t.tinv + _dot3s(_split(t.tinv), t.xs)
        for t in ts:
            tsp = _split(t.tinv)
            t.wm = _dot3s(tsp, _split((t.beta * t.gam) * t.k))
            t.uv = _dot3s(tsp, _split(t.beta * t.vv))
        for t in ts:
            w_ref[t.rows, t.cs] = t.wm
            tm_ref[t.cc, t.h] = t.tinv
            t.wb = t.wm.astype(BF16)
            t.qgb = (t.q * t.gam).astype(BF16)
            t.kdb = (t.k * t.e_rest).astype(BF16)
            t.pb = t.p.astype(BF16)
        for cc in range(GDN_CB):
            tc = [t for t in ts if t.cc == cc]
            for t in tc:
                t.sh = s_ref[t.h]
                t.sb = t.sh.astype(BF16)
                sp_ref[cc, t.h] = t.sh
            for t in tc:
                t.ws = _dot(t.wb, t.sb)
                t.qs = _dot(t.qgb, t.sb)
            for t in tc:
                t.u = t.uv - t.ws
                t.ub = t.u.astype(BF16)
            for t in tc:
                t.pu = _dot(t.pb, t.ub)
                t.ku = _dot_tn(t.kdb, t.ub)
            for t in tc:
                t.o = t.qs + t.pu
                s_ref[t.h] = t.gl * t.sh + t.ku
                u_ref[t.rows, t.cs] = t.u
                o_ref[t.rows, t.cs] = t.o
        for t in ts:
            zz = z_ref[t.rows, t.cs]
            rr = lax.rsqrt(jnp.mean(t.o * t.o, axis=-1, keepdims=True) + EPS)
            oan_ref[t.rows, t.cs] = ((t.o * rr) * gnw_v * (zz * _sigmoid(zz))).astype(BF16)

    row = pl.BlockSpec((r_, KEY_A), lambda i: (i, 0))
    return pl.pallas_call(
        body, name="gdn_fwd", grid=(nc // GDN_CB,),
        in_specs=[row, row, row, pl.BlockSpec((r_, LANES), lambda i: (i, CB_BD)),
                  pl.BlockSpec((r_, PB), lambda i: (i, CB_ZA)),
                  pl.BlockSpec((SUBLANES, LANES), lambda i: (0, 0)), pl.BlockSpec((1, GDN_DK), lambda i: (0, 0))],
        out_specs=[row, row, pl.BlockSpec((GDN_CB, GDN_HEADS, GDN_DK, GDN_DK), lambda i: (i, 0, 0, 0)),
                   row, row, pl.BlockSpec((GDN_CB, GDN_HEADS, c, c), lambda i: (i, 0, 0, 0))],
        out_shape=[jax.ShapeDtypeStruct((t, KEY_A), BF16), jax.ShapeDtypeStruct((t, KEY_A), F32),
                   jax.ShapeDtypeStruct((nc, GDN_HEADS, GDN_DK, GDN_DK), F32),
                   jax.ShapeDtypeStruct((t, KEY_A), F32), jax.ShapeDtypeStruct((t, KEY_A), F32),
                   jax.ShapeDtypeStruct((nc, GDN_HEADS, c, c), F32)],
        scratch_shapes=[pltpu.VMEM((GDN_HEADS, GDN_DK, GDN_DK), F32)],
        compiler_params=_cparams("arbitrary"),
    )(qn, kn, v, proj, proj, par, gnw)


def _gdn_bwd(qn, kn, v, proj, par, gnw, o, sprev, wst, ust, tst, d_oan, dproj):
    t = qn.shape[0]
    c = CHUNK
    nc = t // c
    nb = nc // GDN_CB
    r_ = GDN_CB * c

    def body(qn_ref, kn_ref, v_ref, bd_ref, z_ref, par_ref, gnw_ref, o_ref, sp_ref, w_ref, u_ref, tm_ref, do_ref,
             dp_in_ref, dqn_ref, dkn_ref, dv_ref, dzb_ref, acc_ref, ds_ref):
        @pl.when(pl.program_id(0) == 0)
        def _():
            ds_ref[...] = jnp.zeros_like(ds_ref)
            acc_ref[...] = jnp.zeros_like(acc_ref)

        bd, par, gnw_v = bd_ref[...], par_ref[...], gnw_ref[...]
        lane = lax.broadcasted_iota(jnp.int32, (1, LANES), 1)
        rix = lax.broadcasted_iota(jnp.int32, (c, 1), 0)
        ii = lax.broadcasted_iota(jnp.int32, (c, c), 0)
        jj = lax.broadcasted_iota(jnp.int32, (c, c), 1)
        upper = (jj >= ii).astype(F32)
        acc_a = jnp.zeros((1, LANES), F32)
        acc_d = jnp.zeros((1, LANES), F32)
        acc_g = jnp.zeros((1, LANES), F32)
        ts = _gdn_terms(bd, par, kn_ref, qn_ref)
        for t in ts:
            t.vv = v_ref[t.rows, t.cs]
            t.sh = sp_ref[t.cc, t.h]
            t.sb = t.sh.astype(BF16)
            t.wm, t.u, t.tinv = w_ref[t.rows, t.cs], u_ref[t.rows, t.cs], tm_ref[t.cc, t.h]
            t.wb, t.ub = t.wm.astype(BF16), t.u.astype(BF16)
            ov, zz, dout = o_ref[t.rows, t.cs], z_ref[t.rows, t.cs], do_ref[t.rows, t.cs]
            sg = _sigmoid(zz)
            sil = zz * sg
            rr = lax.rsqrt(jnp.mean(ov * ov, axis=-1, keepdims=True) + EPS)
            on = ov * rr
            dzb_ref[t.rows, t.cs] = (dout * on * gnw_v * (sg * (1.0 + zz * (1.0 - sg)))).astype(BF16)
            acc_g = acc_g + jnp.sum(dout * on * sil, axis=0, keepdims=True)
            don = dout * gnw_v * sil
            t.dob = (rr * (don - on * jnp.mean(don * on, axis=-1, keepdims=True))).astype(BF16)
            t.qg = t.q * t.gam
            t.kd = t.k * t.e_rest
            t.qgb, t.kdb = t.qg.astype(BF16), t.kd.astype(BF16)
            t.ptb = t.p.T.astype(BF16)
        for cc in reversed(range(GDN_CB)):
            tc = [t for t in ts if t.cc == cc]
            for t in tc:
                t.dsn = ds_ref[t.h]
                t.dsnb = t.dsn.astype(BF16)
            for t in tc:
                t.du = _dot(t.ptb, t.dob) + _dot(t.kdb, t.dsnb)
                t.dkd = _dot_nt(t.ub, t.dsnb)
                t.dgl = jnp.sum(jnp.sum(t.dsn * t.sh, axis=1, keepdims=True), axis=0, keepdims=True)
            for t in tc:
                t.dub = t.du.astype(BF16)
            for t in tc:
                ds_ref[t.h] = t.gl * t.dsn + _dot_tn(t.qgb, t.dob) - _dot_tn(t.wb, t.dub)
        for t in ts:
            t.dqg = _dot_nt(t.dob, t.sb)
            t.dp = _dot_nt(t.dob, t.ub)
            t.dwm = -_dot_nt(t.dub, t.sb)
            t.uv = t.u + _dot(t.wb, t.sb)
        for t in ts:
            tsp = _split(t.tinv.T)
            t.dbk = _dot3s(tsp, _split(t.dwm))
            t.dbv = _dot3s(tsp, _split(t.du))
        for t in ts:
            d_a = -(_dot_nt(t.dbk.astype(BF16), t.wb) + _dot_nt(t.dbv.astype(BF16), t.uv.astype(BF16)))
            t.d_a = jnp.where(t.strict, d_a, 0.0)
        for t in ts:
            t.dkk = t.d_a * t.beta * t.dec_s
            t.dqk = t.dp * t.dec_i
            t.dqkb = t.dqk.astype(BF16)
        for t in ts:
            t.dq = _dot(t.dqkb, t.kb) + t.dqg * t.gam
            t.dk = (t.dbk * (t.beta * t.gam) + _dot_tn(t.dqkb, t.qb) + _dot((t.dkk + t.dkk.T).astype(BF16), t.kb)
                    + t.dkd * t.e_rest)
        for t in ts:
            dbeta = (jnp.sum(t.d_a * t.kk * t.dec_s, axis=-1, keepdims=True)
                     + jnp.sum(t.dbk * t.k * t.gam, axis=-1, keepdims=True) + jnp.sum(t.dbv * t.vv, axis=-1, keepdims=True))
            t.dbl = dbeta * t.beta * (1.0 - t.beta)
            dv_ref[t.rows, t.cs] = t.dbv * t.beta
            bk = (t.beta * t.gam) * t.k
            zc = jnp.sum(t.dkd * t.kd, axis=-1, keepdims=True)
            xs = t.dkk * t.kk + t.dp * t.p
            dgc = (jnp.sum(xs, axis=-1, keepdims=True) - jnp.sum(xs.T, axis=-1, keepdims=True)
                   + jnp.sum(t.dbk * bk, axis=-1, keepdims=True) + jnp.sum(t.dqg * t.qg, axis=-1, keepdims=True) - zc)
            dglast = jnp.sum(zc, axis=0, keepdims=True) + t.dgl * t.gl[:, 0:1]
            t.dgc = dgc + jnp.where(rix == c - 1, dglast, 0.0)
        dgall = _dot(upper, _pairs_to_lanes([t.dgc for t in ts]), HIGHEST)
        for p, t in enumerate(ts):
            t.dg = dgall[:, p:p + 1]
        dbd_tiles = [jnp.zeros((c, LANES), F32) for _ in range(GDN_CB)]
        for t in ts:
            ddl = t.dg * (-t.ea) * _sigmoid(t.sp_arg)
            acc_a = acc_a + jnp.where(lane == t.h, jnp.sum(t.dg * t.g, axis=0, keepdims=True), 0.0)
            acc_d = acc_d + jnp.where(lane == t.h, jnp.sum(ddl, axis=0, keepdims=True), 0.0)
            dbd_tiles[t.cc] = (dbd_tiles[t.cc] + jnp.where(lane == t.h, t.dbl, 0.0)
                               + jnp.where(lane == GDN_HEADS + t.h, ddl, 0.0))
            dqn_ref[t.rows, t.cs] = t.dq * (GDN_DK ** -0.5)
            dkn_ref[t.rows, t.cs] = t.dk
        for cc in range(GDN_CB):
            dzb_ref[cc * c:(cc + 1) * c, KEY_A:KEY_A + LANES] = dbd_tiles[cc].astype(BF16)
        acc_ref[0:1, :] += acc_a
        acc_ref[1:2, :] += acc_d
        acc_ref[2:3, :] += acc_g

    def rev(i):
        return nb - 1 - i

    row = pl.BlockSpec((r_, KEY_A), lambda i: (rev(i), 0))
    st = pl.BlockSpec((GDN_CB, GDN_HEADS, GDN_DK, GDN_DK), lambda i: (rev(i), 0, 0, 0))
    tt_spec = pl.BlockSpec((GDN_CB, GDN_HEADS, c, c), lambda i: (rev(i), 0, 0, 0))
    return pl.pallas_call(
        body, name="gdn_bwd", grid=(nb,),
        in_specs=[row, row, row, pl.BlockSpec((r_, LANES), lambda i: (rev(i), CB_BD)),
                  pl.BlockSpec((r_, PB), lambda i: (rev(i), CB_ZA)),
                  pl.BlockSpec((SUBLANES, LANES), lambda i: (0, 0)), pl.BlockSpec((1, GDN_DK), lambda i: (0, 0)),
                  row, st, row, row, tt_spec, row, pl.BlockSpec(memory_space=pl.ANY)],
        out_specs=[row, row, row, _dp_spec(r_, DP_ZBD, rev), pl.BlockSpec((SUBLANES, LANES), lambda i: (0, 0))],
        out_shape=[jax.ShapeDtypeStruct((t, KEY_A), F32)] * 3 + [jax.ShapeDtypeStruct(dproj.shape, dproj.dtype),
                                                                jax.ShapeDtypeStruct((SUBLANES, LANES), F32)],
        input_output_aliases={13: 3},
        scratch_shapes=[pltpu.VMEM((GDN_HEADS, GDN_DK, GDN_DK), F32)],
        compiler_params=_cparams("arbitrary"),
    )(qn, kn, v, proj, proj, par, gnw, o, sprev, wst, ust, tst, d_oan, dproj)


def _merge_fwd(oan, ob, proj, x, wba, wbb, wout, tm=512):
    t = x.shape[0]

    def body(oa_ref, ob_ref, ga_ref, gb_ref, x_ref, wba_ref, wbb_ref, wout_ref, x2_ref):
        ya = _dot(oa_ref[...], wba_ref[...])
        yb = _dot(ob_ref[...], wbb_ref[...])
        mix = _sigmoid(ga_ref[...]) * ya + _sigmoid(gb_ref[...]) * yb
        x2_ref[...] = x_ref[...] + _dot(mix.astype(BF16), wout_ref[...])

    half = pl.BlockSpec((tm, KEY_A), lambda i: (i, 0))
    row = pl.BlockSpec((tm, D_MODEL), lambda i: (i, 0))
    wsmall = pl.BlockSpec((KEY_A, D_MODEL), lambda i: (0, 0))
    return pl.pallas_call(
        body, name="merge_fwd", grid=(t // tm,),
        in_specs=[half, half, pl.BlockSpec((tm, D_MODEL), lambda i: (i, CB_GA)),
                  pl.BlockSpec((tm, D_MODEL), lambda i: (i, CB_GB)), row, wsmall, wsmall,
                  pl.BlockSpec((D_MODEL, D_MODEL), lambda i: (0, 0))],
        out_specs=row,
        out_shape=jax.ShapeDtypeStruct((t, D_MODEL), F32),
        compiler_params=_cparams("parallel"),
    )(oan, ob, proj, proj, x, wba, wbb, wout)


def _merge_bwd(dx2b, oan, ob, proj, wba, wbb, wout_t, wba_t, wbb_t, tm=512):
    t = dx2b.shape[0]

    def body(dx_ref, oa_ref, ob_ref, ga_ref, gb_ref, wba_ref, wbb_ref, woutt_ref, wbat_ref, wbbt_ref,
             dg_ref, doa_ref, dob_ref, mix_ref, dya_ref, dyb_ref):
        dmix = _dot(dx_ref[...], woutt_ref[...])
        ya = _dot(oa_ref[...], wba_ref[...])
        yb = _dot(ob_ref[...], wbb_ref[...])
        sa, sb = _sigmoid(ga_ref[...]), _sigmoid(gb_ref[...])
        mix_ref[...] = (sa * ya + sb * yb).astype(BF16)
        dg_ref[:, :D_MODEL] = (dmix * ya * sa * (1.0 - sa)).astype(BF16)
        dg_ref[:, D_MODEL:] = (dmix * yb * sb * (1.0 - sb)).astype(BF16)
        dya = (dmix * sa).astype(BF16)
        dyb = (dmix * sb).astype(BF16)
        dya_ref[...] = dya
        dyb_ref[...] = dyb
        doa_ref[...] = _dot(dya, wbat_ref[...])
        dob_ref[...] = _dot(dyb, wbbt_ref[...])

    half = pl.BlockSpec((tm, KEY_A), lambda i: (i, 0))
    row = pl.BlockSpec((tm, D_MODEL), lambda i: (i, 0))
    wsmall = pl.BlockSpec((KEY_A, D_MODEL), lambda i: (0, 0))
    wsmall_t = pl.BlockSpec((D_MODEL, KEY_A), lambda i: (0, 0))
    big = jax.ShapeDtypeStruct((t, D_MODEL), BF16)
    return pl.pallas_call(
        body, name="merge_bwd", grid=(t // tm,),
        in_specs=[row, half, half, pl.BlockSpec((tm, D_MODEL), lambda i: (i, CB_GA)),
                  pl.BlockSpec((tm, D_MODEL), lambda i: (i, CB_GB)), wsmall, wsmall,
                  pl.BlockSpec((D_MODEL, D_MODEL), lambda i: (0, 0)), wsmall_t, wsmall_t],
        out_specs=[_dp_spec(tm, DP_GATES), half, half, row, row, row],
        out_shape=[jax.ShapeDtypeStruct((t, PROJ_W), BF16), jax.ShapeDtypeStruct((t, KEY_A), F32),
                   jax.ShapeDtypeStruct((t, KEY_A), F32), big, big, big],
        compiler_params=_cparams("parallel"),
    )(dx2b, oan, ob, proj, proj, wba, wbb, wout_t, wba_t, wbb_t)


FFN_TM = 128
FFN_W = 2 * D_FF


def _ffn_conv(up_ref, halo_ref, cw_ref, cb_ref, first):
    ext = jnp.concatenate([jnp.where(first, 0.0, halo_ref[...]), up_ref[...]], axis=0)
    taps = _conv_taps(ext, FFN_CONV, HALO, FFN_TM)
    cw = cw_ref[...]
    u = sum(cw[k:k + 1, :] * taps[k] for k in range(FFN_CONV)) + cb_ref[...]
    return u, taps


def _ffn_tail(up, cw, cb, wdown, x2, tgt, w3):
    t = x2.shape[0]
    tm = FFN_TM

    def body(up_ref, halo_ref, cw_ref, cb_ref, wd_ref, x2_ref, tgt_ref, w3_ref, dx_ref, dxb_ref, act_ref, acc_ref):
        first = pl.program_id(0) == 0

        @pl.when(first)
        def _():
            acc_ref[...] = jnp.zeros_like(acc_ref)

        u, _ = _ffn_conv(up_ref, halo_ref, cw_ref, cb_ref, first)
        gate, upp = u[:, :D_FF], u[:, D_FF:]
        act = (gate * _sigmoid(gate) * upp).astype(BF16)
        act_ref[...] = act
        x3 = x2_ref[...] + _dot(act, wd_ref[...])
        r = lax.rsqrt(jnp.mean(x3 * x3, axis=-1, keepdims=True) + EPS)
        xh = x3 * r
        w3v = w3_ref[...]
        err = xh * w3v - tgt_ref[...]
        loss = 0.5 * jnp.sum(jnp.mean(err * err, axis=-1, keepdims=True), axis=0, keepdims=True)
        dy = err * (1.0 / D_MODEL)
        acc_ref[0:1, :] += jnp.sum(dy * xh, axis=0, keepdims=True)
        acc_ref[1:2, :] += jnp.broadcast_to(loss, (1, D_MODEL))
        dxh = dy * w3v
        dx = r * (dxh - xh * jnp.mean(dxh * xh, axis=-1, keepdims=True))
        dx_ref[...] = dx
        dxb_ref[...] = dx.astype(BF16)

    row = pl.BlockSpec((tm, D_MODEL), lambda i: (i, 0))
    vec = pl.BlockSpec((1, D_MODEL), lambda i: (0, 0))
    return pl.pallas_call(
        body, name="ffn_tail", grid=(t // tm,),
        in_specs=[pl.BlockSpec((tm, FFN_W), lambda i: (i, 0)), _prev_halo_spec(tm, FFN_W, 0),
                  pl.BlockSpec((SUBLANES, FFN_W), lambda i: (0, 0)), pl.BlockSpec((1, FFN_W), lambda i: (0, 0)),
                  pl.BlockSpec((D_FF, D_MODEL), lambda i: (0, 0)), row, row, vec],
        out_specs=[row, row, pl.BlockSpec((tm, D_FF), lambda i: (i, 0)), pl.BlockSpec((SUBLANES, D_MODEL), lambda i: (0, 0))],
        out_shape=[jax.ShapeDtypeStruct((t, D_MODEL), F32), jax.ShapeDtypeStruct((t, D_MODEL), BF16),
                   jax.ShapeDtypeStruct((t, D_FF), BF16), jax.ShapeDtypeStruct((SUBLANES, D_MODEL), F32)],
        compiler_params=_cparams("arbitrary"),
    )(up, up, cw, cb, wdown, x2, tgt, w3)


def _ffn_bwd_act(dx3b, wdown_t, up, cw, cb):
    t = up.shape[0]
    tm = FFN_TM

    def body(dx_ref, wdt_ref, up_ref, halo_ref, cw_ref, cb_ref, du_ref, acc_ref):
        first = pl.program_id(0) == 0

        @pl.when(first)
        def _():
            acc_ref[...] = jnp.zeros_like(acc_ref)

        dact = _dot(dx_ref[...], wdt_ref[...])
        u, taps = _ffn_conv(up_ref, halo_ref, cw_ref, cb_ref, first)
        gate, upp = u[:, :D_FF], u[:, D_FF:]
        sg = _sigmoid(gate)
        du = jnp.concatenate([dact * upp * (sg * (1.0 + gate * (1.0 - sg))), dact * (gate * sg)], axis=1)
        du_ref[...] = du
        for k in range(FFN_CONV):
            acc_ref[k:k + 1, :] += jnp.sum(du * taps[k], axis=0, keepdims=True)
        acc_ref[FFN_CONV:FFN_CONV + 1, :] += jnp.sum(du, axis=0, keepdims=True)

    return pl.pallas_call(
        body, name="ffn_bwd_act", grid=(t // tm,),
        in_specs=[pl.BlockSpec((tm, D_MODEL), lambda i: (i, 0)), pl.BlockSpec((D_MODEL, D_FF), lambda i: (0, 0)),
                  pl.BlockSpec((tm, FFN_W), lambda i: (i, 0)), _prev_halo_spec(tm, FFN_W, 0),
                  pl.BlockSpec((SUBLANES, FFN_W), lambda i: (0, 0)), pl.BlockSpec((1, FFN_W), lambda i: (0, 0))],
        out_specs=[pl.BlockSpec((tm, FFN_W), lambda i: (i, 0)), pl.BlockSpec((SUBLANES, FFN_W), lambda i: (0, 0))],
        out_shape=[jax.ShapeDtypeStruct((t, FFN_W), F32), jax.ShapeDtypeStruct((SUBLANES, FFN_W), F32)],
        compiler_params=_cparams("arbitrary"),
    )(dx3b, wdown_t, up, up, cw, cb)


def _ffn_bwd_conv(du, cw):
    t = du.shape[0]
    tm = FFN_TM
    nt = t // tm
    n_ext = tm + HALO

    def body(du_ref, nxt_ref, cw_ref, o_ref):
        last = pl.program_id(0) == nt - 1
        ext = jnp.concatenate([du_ref[...], jnp.where(last, 0.0, nxt_ref[...])], axis=0)
        cw = cw_ref[...]
        acc = cw[FFN_CONV - 1:FFN_CONV, :] * ext[:tm]
        for k in range(FFN_CONV - 1):
            acc = acc + cw[k:k + 1, :] * pltpu.roll(ext, n_ext - (FFN_CONV - 1 - k), 0)[:tm]
        o_ref[...] = acc.astype(BF16)

    return pl.pallas_call(
        body, name="ffn_bwd_conv", grid=(nt,),
        in_specs=[pl.BlockSpec((tm, FFN_W), lambda i: (i, 0)), _next_halo_spec(tm, FFN_W, 0, t),
                  pl.BlockSpec((SUBLANES, FFN_W), lambda i: (0, 0))],
        out_specs=pl.BlockSpec((tm, FFN_W), lambda i: (i, 0)),
        out_shape=jax.ShapeDtypeStruct((t, FFN_W), BF16),
        compiler_params=_cparams("parallel"),
    )(du, du, cw)


def _adamw(parts, w, m, v, name, tr):
    r, cols = w.shape

    def body(p_ref, w_ref, m_ref, v_ref, g_ref, d_ref, mo_ref, vo_ref):
        g = p_ref[0].astype(F32)
        for s in range(1, N_DEV):
            g = g + p_ref[s].astype(F32)
        mm = ADAM_B1 * m_ref[...] + (1.0 - ADAM_B1) * g
        vv = ADAM_B2 * v_ref[...] + (1.0 - ADAM_B2) * (g * g)
        m_hat = mm / (1.0 - ADAM_B1 ** ADAM_STEP)
        v_hat = vv / (1.0 - ADAM_B2 ** ADAM_STEP)
        g_ref[...] = g
        d_ref[...] = -ADAM_LR * (m_hat / (jnp.sqrt(v_hat) + ADAM_EPS) + ADAM_WD * w_ref[...])
        mo_ref[...] = mm
        vo_ref[...] = vv

    assert r % tr == 0
    row = pl.BlockSpec((tr, cols), lambda i: (i, 0))
    return pl.pallas_call(
        body, name=name, grid=(r // tr,),
        in_specs=[pl.BlockSpec((N_DEV, tr, cols), lambda i: (0, i, 0)), row, row, row],
        out_specs=[row, row, row, row],
        out_shape=[jax.ShapeDtypeStruct((r, cols), F32)] * 4,
        compiler_params=_cparams("parallel"),
    )(parts, w, m, v)


def _mesh_pos():
    return lax.axis_index("x"), lax.axis_index("y"), lax.axis_index("c")


def _peer(pos, k):
    x, y, c = pos
    return (x ^ ((k >> 2) & 1), y ^ ((k >> 1) & 1), c ^ (k & 1))


def _flat_id(pos):
    return 4 * pos[0] + 2 * pos[1] + pos[2]


def _exchange_copies(srcs, dsts, scatter, send_sems, recv_sems, loc_sems):
    pos = _mesh_pos()
    me = _flat_id(pos)
    local, remote = [], []
    for j, (src, dst) in enumerate(zip(srcs, dsts)):
        local.append(pltpu.make_async_copy(src.at[me] if scatter[j] else src, dst.at[me], loc_sems.at[j]))
        for k in range(1, N_DEV):
            to = _peer(pos, k)
            remote.append(pltpu.make_async_remote_copy(
                src_ref=src.at[_flat_id(to)] if scatter[j] else src, dst_ref=dst.at[me],
                send_sem=send_sems.at[j, k - 1], recv_sem=recv_sems.at[j, k - 1],
                device_id=to, device_id_type=pl.DeviceIdType.MESH))
    return local, remote


def _exchange_shapes(arrays, scatter):
    return [jax.ShapeDtypeStruct(a.shape if s else (N_DEV,) + a.shape, a.dtype) for a, s in zip(arrays, scatter)]


def _exchange_sems(n):
    return [pltpu.SemaphoreType.DMA((n, N_DEV - 1)), pltpu.SemaphoreType.DMA((n, N_DEV - 1)), pltpu.SemaphoreType.DMA((n,))]


def _exchange(arrays, scatter, name):
    n = len(arrays)
    any_spec = pl.BlockSpec(memory_space=pl.ANY)

    def body(*refs):
        local, remote = _exchange_copies(refs[:n], refs[n:2 * n], scatter, *refs[2 * n:])
        for cp in local + remote:
            cp.start()
        for cp in remote:
            cp.wait()
        for cp in local:
            cp.wait()

    return pl.pallas_call(
        body, name=name, in_specs=[any_spec] * n, out_specs=[any_spec] * n,
        out_shape=_exchange_shapes(arrays, scatter), scratch_shapes=_exchange_sems(n),
    )(*arrays)


def _pad_rows(a, rows):
    return jnp.pad(a, ((0, rows - a.shape[0]),) + ((0, 0),) * (a.ndim - 1))


PACK_UNIT = SUBLANES * LANES


def _pack_lanes(parts, rows):
    out = []
    for a in parts:
        f = a.reshape(-1)
        out.append(jnp.pad(f, (0, (-f.shape[0]) % PACK_UNIT)).reshape(-1, LANES))
    packed = jnp.concatenate(out, axis=0)
    assert packed.shape[0] == rows, (packed.shape, rows)
    return packed


def _unpack_lanes(buf, shapes):
    out, r0 = [], 0
    for shp in shapes:
        n = math.prod(shp)
        nr = -(-n // PACK_UNIT) * SUBLANES
        out.append(buf[r0:r0 + nr].reshape(-1)[:n].reshape(shp))
        r0 += nr
    return out


def _col_shards(g):
    r, n = g.shape
    return g.reshape(r, N_DEV, n // N_DEV).transpose(1, 0, 2)


def _col_unshard(s):
    _, r, w = s.shape
    return s.transpose(1, 0, 2).reshape(r, N_DEV * w)


def _lane_rows(flat):
    n = flat.shape[1]
    return jnp.pad(flat, ((0, 0), (0, (-n) % PACK_UNIT))).reshape(N_DEV, -1, LANES)


SMALL_ROWS = 128
WS_ROWS = 32


def kernel(x, norm_mix_w, w_in, conv_qkv_w, a_log, dt_bias, gdn_norm_w, w_branch_a, w_branch_b, rel_bias, w_out, norm_ffn_w, w_up, conv_ffn_w, conv_ffn_b, w_down, norm_final_w, loss_target, m_norm_mix_w, m_w_in, m_conv_qkv_w, m_a_log, m_dt_bias, m_gdn_norm_w, m_w_branch_a, m_w_branch_b, m_rel_bias, m_w_out, m_norm_ffn_w, m_w_up, m_conv_ffn_w, m_conv_ffn_b, m_w_down, m_norm_final_w, v_norm_mix_w, v_w_in, v_conv_qkv_w, v_a_log, v_dt_bias, v_gdn_norm_w, v_w_branch_a, v_w_branch_b, v_rel_bias, v_w_out, v_norm_ffn_w, v_w_up, v_conv_ffn_w, v_conv_ffn_b, v_w_down, v_norm_final_w):
    big_w = (w_in, w_branch_a, w_branch_b, w_out, w_up, w_down, conv_qkv_w, conv_ffn_w)
    big_m = (m_w_in, m_w_branch_a, m_w_branch_b, m_w_out, m_w_up, m_w_down, m_conv_qkv_w, m_conv_ffn_w)
    big_v = (v_w_in, v_w_branch_a, v_w_branch_b, v_w_out, v_w_up, v_w_down, v_conv_qkv_w, v_conv_ffn_w)
    small_w = (norm_mix_w, a_log, dt_bias, gdn_norm_w, rel_bias, norm_ffn_w, conv_ffn_b, norm_final_w)
    small_m = (m_norm_mix_w, m_a_log, m_dt_bias, m_gdn_norm_w, m_rel_bias, m_norm_ffn_w, m_conv_ffn_b, m_norm_final_w)
    small_v = (v_norm_mix_w, v_a_log, v_dt_bias, v_gdn_norm_w, v_rel_bias, v_norm_ffn_w, v_conv_ffn_b, v_norm_final_w)

    xs, tgt = x[0], loss_target[0]
    ws = _pack_lanes(big_w[6:], WS_ROWS)
    h1, g_in, gs = _rmsnorm_cast(xs, norm_mix_w, "norm_mix", carry=([w_in[0].astype(BF16), ws], (False, False)))
    win = _col_unshard(g_in)
    gs = gs.reshape(N_DEV, -1)
    cqkv = gs[:, :GDN_CONV * 192].reshape(N_DEV, GDN_CONV, 192).transpose(1, 0, 2).reshape(GDN_CONV, 3 * KEY_A)
    cffn = gs[:, PACK_UNIT:PACK_UNIT + FFN_CONV * 704].reshape(N_DEV, FFN_CONV, 704).transpose(1, 0, 2).reshape(FFN_CONV, FFN_W)
    cffn = _pad_rows(cffn, SUBLANES)
    w_all = jnp.concatenate([win[:, a:b] for a, b in W_IN_ORDER] + [jnp.zeros((D_MODEL, PROJ_W - D_IN), BF16)], axis=1)
    par = _pad_rows(jnp.pad(jnp.concatenate([a_log, dt_bias], axis=0), ((0, 0), (0, LANES - GDN_HEADS))), SUBLANES)
    table = jnp.pad(rel_bias[0], ((0, 0), (0, 3 * LANES - rel_bias.shape[-1]))).reshape(ATT_HEADS, 1, 3 * LANES)

    proj, g_ba, g_bb, g_out, g_up, g_down = _mm_nn(
        h1, w_all, F32, "in_proj", 2 * MM_TM, 1152, D_MODEL, carry=([w[0].astype(BF16) for w in big_w[1:6]], (False,) * 5))
    wba, wbb, wup = _col_unshard(g_ba), _col_unshard(g_bb), _col_unshard(g_up)
    wout = g_out.reshape(D_MODEL, D_MODEL)
    wdown = g_down.reshape(D_FF, D_MODEL)
    qn, kn, va = _gdn_prep_fwd(proj, cqkv)
    oan, o_gdn, sprev, wst, ust, tst = _gdn_fwd(qn, kn, va, proj, par, gdn_norm_w)
    bias_q, bias_k = _att_bias(table)
    ob, lse, lse_t = _att_fwd(proj, bias_q)
    x2 = _merge_fwd(oan, ob, proj, xs, wba, wbb, wout)
    h2 = _rmsnorm_cast(x2, norm_ffn_w, "norm_ffn")
    up = _mm_nn(h2, wup, F32, "ffn_up", 2 * MM_TM, 1408, D_MODEL)
    dx3, dx3b, act, tail_sums = _ffn_tail(up, cffn, conv_ffn_b, wdown, x2, tgt, norm_final_w.reshape(1, D_MODEL))

    g_wdown = _mm_tn(act, dx3b, "dw_down", 512)
    du, ffn_sums = _ffn_bwd_act(dx3b, wdown.T, up, cffn, conv_ffn_b)
    dup = _ffn_bwd_conv(du, cffn)
    g_wup = _mm_tn(h2, dup, "dw_up", 1408)
    dx2, dx2b, nffn_sums, r_up, r_down = _mm_rms_bwd(dup, wup.T, x2, norm_ffn_w, dx3, "ffn_up_bwd", MM_TM, 1408, carry=(
        [_col_shards(g_wup).astype(BF16), g_wdown.reshape(N_DEV, -1, D_MODEL).astype(BF16)], (True, True)))
    dproj, d_oan, d_ob, mixb, dya, dyb = _merge_bwd(dx2b, oan, ob, proj, wba, wbb, wout.T, wba.T, wbb.T)
    g_wout = _mm_tn(mixb, dx2b, "dw_out", 512)
    g_wba = _mm_tn(oan, dya, "dw_branch_a", 512)
    g_wbb = _mm_tn(ob, dyb, "dw_branch_b", 512)
    dproj, dlt_t, slabs = _att_dq(proj, bias_q, lse, d_ob, dproj)
    dproj = _att_dkv(proj, bias_k, lse_t, dlt_t, d_ob, dproj)
    g_rel = _relbias_grad(slabs)[:, 0, :rel_bias.shape[-1]]
    dqn, dkn, dva, dproj, gdn_sums = _gdn_bwd(qn, kn, va, proj, par, gdn_norm_w, o_gdn, sprev, wst, ust, tst, d_oan, dproj)
    dproj, cq_sums = _gdn_prep_bwd(proj, cqkv, dqn, dkn, dva, dproj)
    g_wall = _mm_tn(h1, dproj, "dw_in", 1152)
    starts = np.cumsum([0] + [b - a for a, b in W_IN_ORDER])
    g_win = jnp.concatenate([g_wall[:, starts[i]:starts[i + 1]] for i in np.argsort([a for a, _ in W_IN_ORDER])], axis=1)
    g_conv = jnp.concatenate([_lane_rows(_col_shards(cq_sums[:GDN_CONV]).reshape(N_DEV, -1)),
                              _lane_rows(_col_shards(ffn_sums[:FFN_CONV]).reshape(N_DEV, -1))], axis=1)
    grad_x, _, nmix_sums, r_in, r_ba, r_bb, r_out, r_conv = _mm_rms_bwd(
        dproj, w_all.T, xs, norm_mix_w, dx2, "in_proj_bwd", MM_TM, 1152, carry=(
            [_col_shards(g_win).astype(BF16), _col_shards(g_wba).astype(BF16), _col_shards(g_wbb).astype(BF16),
             g_wout.reshape(N_DEV, -1, D_MODEL).astype(BF16), g_conv], (True,) * 5))

    small_g = (nmix_sums[0:1], gdn_sums[0:1, :GDN_HEADS], gdn_sums[1:2, :GDN_HEADS], gdn_sums[2:3], g_rel,
               nffn_sums[0:1], ffn_sums[FFN_CONV:FFN_CONV + 1], tail_sums[0:1], tail_sums[1:2, 0:1])
    r_small, = _exchange([_pack_lanes(small_g, SMALL_ROWS)], (False,), "all_gather_small_grads")
    recv = (r_in, r_ba, r_bb, r_out, r_up, r_down, r_conv, r_small)

    res = {}
    for i, (nm, tr) in enumerate((("w_in", 128), ("w_branch_a", KEY_A), ("w_branch_b", WIDTH_B), ("w_out", 128),
                                  ("w_up", 128), ("w_down", 176))):
        res[nm] = [o[None] for o in _adamw(recv[i], big_w[i][0], big_m[i][0], big_v[i][0], "adamw_" + nm, tr)]
    conv = _adamw(recv[6], _pack_lanes(big_w[6:], WS_ROWS), _pack_lanes(big_m[6:], WS_ROWS), _pack_lanes(big_v[6:], WS_ROWS),
                  "adamw_conv", WS_ROWS)
    conv = [_unpack_lanes(o, [w.shape for w in big_w[6:]]) for o in conv]
    res["conv_qkv_w"] = [o[0] for o in conv]
    res["conv_ffn_w"] = [o[1] for o in conv]
    small_shapes = [w.shape for w in small_w]
    zero = jnp.zeros((1,), F32)
    small = _adamw(recv[7], _pack_lanes(small_w + (zero,), SMALL_ROWS), _pack_lanes(small_m + (zero,), SMALL_ROWS),
                   _pack_lanes(small_v + (zero,), SMALL_ROWS), "adamw_replicated", SMALL_ROWS)
    small = [_unpack_lanes(o, small_shapes + [()]) for o in small]
    loss = small[0][-1]
    for j, nm in enumerate(("norm_mix_w", "a_log", "dt_bias", "gdn_norm_w", "rel_bias", "norm_ffn_w", "conv_ffn_b",
                            "norm_final_w")):
        res[nm] = [o[j] for o in small]

    names = ("norm_mix_w", "w_in", "conv_qkv_w", "a_log", "dt_bias", "gdn_norm_w", "w_branch_a", "w_branch_b", "rel_bias",
             "w_out", "norm_ffn_w", "w_up", "conv_ffn_w", "conv_ffn_b", "w_down", "norm_final_w")
    outs = [res[n][kind] for kind in range(4) for n in names]
    return (loss, grad_x[None], *outs)
```

```python
import functools
import math

import numpy as np
import jax
import jax.numpy as jnp
from jax import lax
from jax.experimental import pallas as pl
from jax.experimental.pallas import tpu as pltpu

F32, BF16 = jnp.float32, jnp.bfloat16
HIGHEST = lax.Precision.HIGHEST

N_DEV = 8
D_MODEL = 1024
CHUNK = 64
EPS = 1e-6
GDN_HEADS, GDN_DK = 4, 128
KEY_A = GDN_HEADS * GDN_DK
GDN_CONV = 4
ATT_HEADS, ATT_DH = 8, 64
WIDTH_B = ATT_HEADS * ATT_DH
ATT_BAND = 9
REL_CLIP = 128
D_FF = 2816
FFN_CONV = 3
D_IN = 5640
ADAM_LR, ADAM_B1, ADAM_B2, ADAM_EPS, ADAM_WD, ADAM_STEP = 0.001, 0.9, 0.999, 1e-08, 0.01, 10

LANES = 128
SUBLANES = 8
NEG = -1e30

PROJ_W = 5760
PB = 512
CB_GA, CB_GB = 0, 1
CB_KB, CB_VB, CB_QA, CB_KA, CB_VA, CB_QB, CB_ZA = 4, 5, 6, 7, 8, 9, 10
CB_BD = 44
DP_GATES, DP_KVB, DP_QKVA, DP_QB, DP_ZBD = (2048, 0), (1024, 2), (1536, 2), (512, 9), (640, 8)
W_IN_ORDER = ((3592, 5640), (2568, 3592), (0, 1536), (2056, 2568), (1536, 2048), (2048, 2056))

ATT_QB = 256
ATT_KW = 768
ATT_VEC = 1024


def _dot(a, b, precision=None):
    return jnp.dot(a, b, preferred_element_type=F32, precision=precision)


def _dot_nt(a, b, precision=None):
    return lax.dot_general(a, b, (((1,), (1,)), ((), ())), preferred_element_type=F32, precision=precision)


def _dot_tn(a, b):
    return lax.dot_general(a, b, (((0,), (0,)), ((), ())), preferred_element_type=F32)


def _split(a):
    hi = a.astype(BF16)
    return hi, (a - hi.astype(F32)).astype(BF16)


def _dot3s(a, b):
    return _dot(a[0], b[0]) + (_dot(a[0], b[1]) + _dot(a[1], b[0]))


def _sigmoid(x):
    return 0.5 * jnp.tanh(0.5 * x) + 0.5


def _softplus(x):
    return jnp.maximum(x, 0.0) + jnp.log(1.0 + jnp.exp(-jnp.abs(x)))


def _cparams(*sem):
    return pltpu.CompilerParams(dimension_semantics=tuple(sem))


def _dp_spec(tm, region, index=lambda i: i):
    width, cb = region
    return pl.BlockSpec((tm, width), lambda i: (index(i), cb))


def _rmsnorm_cast(x, w, name, tm=512, carry=((), ())):
    t, d = x.shape
    nt = t // tm
    arrays, scatter = carry
    nx = len(arrays)

    def body(*refs):
        x_ref, w_ref = refs[:2]
        o_ref = refs[2 + nx]
        i = pl.program_id(0)
        if nx:
            local, remote = _exchange_copies(refs[2:2 + nx], refs[3 + nx:3 + 2 * nx], scatter, *refs[3 + 2 * nx:])

            @pl.when(i == 0)
            def _():
                for cp in local + remote:
                    cp.start()

        xv = x_ref[...]
        r = lax.rsqrt(jnp.mean(xv * xv, axis=-1, keepdims=True) + EPS)
        o_ref[...] = (xv * r * w_ref[...]).astype(BF16)

        if nx:
            @pl.when(i == nt - 1)
            def _():
                for cp in remote + local:
                    cp.wait()

    any_spec = pl.BlockSpec(memory_space=pl.ANY)
    out = pl.pallas_call(
        body, name=name, grid=(nt,),
        in_specs=[pl.BlockSpec((tm, d), lambda i: (i, 0)), pl.BlockSpec((1, d), lambda i: (0, 0))] + [any_spec] * nx,
        out_specs=[pl.BlockSpec((tm, d), lambda i: (i, 0))] + [any_spec] * nx,
        out_shape=[jax.ShapeDtypeStruct((t, d), BF16)] + _exchange_shapes(arrays, scatter),
        scratch_shapes=_exchange_sems(nx) if nx else [],
        compiler_params=_cparams("arbitrary" if nx else "parallel"),
    )(x, w, *arrays)
    return out if nx else out[0]


def _mm_nn(a, b, out_dtype, name, tm, tn, tk, carry=((), ())):
    m, k = a.shape
    _, n = b.shape
    tm = min(tm, m)
    nk = k // tk
    assert m % tm == 0 and n % tn == 0 and k % tk == 0
    arrays, scatter = carry
    nx = len(arrays)
    gm, gn = m // tm, n // tn

    def body(*refs):
        a_ref, b_ref = refs[:2]
        srcs = refs[2:2 + nx]
        o_ref = refs[2 + nx]
        dsts = refs[3 + nx:3 + 2 * nx]
        rest = refs[3 + 2 * nx:]
        i, j, kk = pl.program_id(0), pl.program_id(1), pl.program_id(2)
        if nx:
            local, remote = _exchange_copies(srcs, dsts, scatter, *rest[-3:])

            @pl.when((i == 0) & (j == 0) & (kk == 0))
            def _():
                for cp in local + remote:
                    cp.start()

        if nk == 1:
            o_ref[...] = _dot(a_ref[...], b_ref[...]).astype(out_dtype)
        else:
            acc_ref = rest[0]

            @pl.when(kk == 0)
            def _():
                acc_ref[...] = jnp.zeros_like(acc_ref)

            acc_ref[...] += _dot(a_ref[...], b_ref[...])

            @pl.when(kk == nk - 1)
            def _():
                o_ref[...] = acc_ref[...].astype(out_dtype)

        if nx:
            @pl.when((i == gm - 1) & (j == gn - 1) & (kk == nk - 1))
            def _():
                for cp in remote + local:
                    cp.wait()

    any_spec = pl.BlockSpec(memory_space=pl.ANY)
    scratch = ([pltpu.VMEM((tm, tn), F32)] if nk > 1 else []) + (_exchange_sems(nx) if nx else [])
    out = pl.pallas_call(
        body, name=name, grid=(gm, gn, nk),
        in_specs=[pl.BlockSpec((tm, tk), lambda i, j, kk: (i, kk)),
                  pl.BlockSpec((tk, tn), lambda i, j, kk: (kk, j))] + [any_spec] * nx,
        out_specs=[pl.BlockSpec((tm, tn), lambda i, j, kk: (i, j))] + [any_spec] * nx,
        out_shape=[jax.ShapeDtypeStruct((m, n), out_dtype)] + _exchange_shapes(arrays, scatter),
        scratch_shapes=scratch,
        compiler_params=_cparams(*(("arbitrary",) * 3 if nx else ("parallel", "parallel", "arbitrary"))),
    )(a, b, *arrays)
    return out if nx else out[0]


def _mm_rms_bwd(a, b, x, w, dres, name, tm, tk, carry):
    m, k = a.shape
    _, n = b.shape
    nk = k // tk
    gm = m // tm
    assert m % tm == 0 and k % tk == 0
    arrays, scatter = carry
    nx = len(arrays)

    def body(*refs):
        a_ref, b_ref, x_ref, w_ref, dres_ref = refs[:5]
        srcs = refs[5:5 + nx]
        dx_ref, dxb_ref, dw_ref = refs[5 + nx:8 + nx]
        dsts = refs[8 + nx:8 + 2 * nx]
        acc_ref = refs[8 + 2 * nx]
        i, kk = pl.program_id(0), pl.program_id(1)
        local, remote = _exchange_copies(srcs, dsts, scatter, *refs[9 + 2 * nx:])

        @pl.when((i == 0) & (kk == 0))
        def _():
            for cp in local + remote:
                cp.start()
            dw_ref[...] = jnp.zeros_like(dw_ref)

        @pl.when(kk == 0)
        def _():
            acc_ref[...] = jnp.zeros_like(acc_ref)

        acc_ref[...] += _dot(a_ref[...], b_ref[...])

        @pl.when(kk == nk - 1)
        def _():
            dhv = acc_ref[...]
            xv = x_ref[...]
            r = lax.rsqrt(jnp.mean(xv * xv, axis=-1, keepdims=True) + EPS)
            xh = xv * r
            dw_ref[0:1, :] += jnp.sum(dhv * xh, axis=0, keepdims=True)
            dxh = dhv * w_ref[...]
            dx = dres_ref[...] + r * (dxh - xh * jnp.mean(dxh * xh, axis=-1, keepdims=True))
            dx_ref[...] = dx
            dxb_ref[...] = dx.astype(BF16)

        @pl.when((i == gm - 1) & (kk == nk - 1))
        def _():
            for cp in remote + local:
                cp.wait()

    any_spec = pl.BlockSpec(memory_space=pl.ANY)
    row = pl.BlockSpec((tm, n), lambda i, kk: (i, 0))
    return pl.pallas_call(
        body, name=name, grid=(gm, nk),
        in_specs=[pl.BlockSpec((tm, tk), lambda i, kk: (i, kk)), pl.BlockSpec((tk, n), lambda i, kk: (kk, 0)),
                  row, pl.BlockSpec((1, n), lambda i, kk: (0, 0)), row] + [any_spec] * nx,
        out_specs=[row, row, pl.BlockSpec((SUBLANES, n), lambda i, kk: (0, 0))] + [any_spec] * nx,
        out_shape=[jax.ShapeDtypeStruct((m, n), F32), jax.ShapeDtypeStruct((m, n), BF16),
                   jax.ShapeDtypeStruct((SUBLANES, n), F32)] + _exchange_shapes(arrays, scatter),
        scratch_shapes=[pltpu.VMEM((tm, n), F32)] + _exchange_sems(nx),
        compiler_params=_cparams("arbitrary", "arbitrary"),
    )(a, b, x, w, dres, *arrays)


MM_TM = 1024


def _mm_tn(a, b, name, tn, tk=2 * MM_TM):
    t, m = a.shape
    _, n = b.shape
    tk = min(tk, t)
    assert t % tk == 0 and n % tn == 0

    def body(a_ref, b_ref, o_ref):
        @pl.when(pl.program_id(1) == 0)
        def _():
            o_ref[...] = jnp.zeros_like(o_ref)

        o_ref[...] += _dot_tn(a_ref[...], b_ref[...])

    return pl.pallas_call(
        body, name=name, grid=(n // tn, t // tk),
        in_specs=[pl.BlockSpec((tk, m), lambda j, s: (s, 0)),
                  pl.BlockSpec((tk, tn), lambda j, s: (s, j))],
        out_specs=pl.BlockSpec((m, tn), lambda j, s: (0, j)),
        out_shape=jax.ShapeDtypeStruct((m, n), F32),
        compiler_params=_cparams("parallel", "arbitrary"),
    )(a, b)


def _rel_index(dist):
    return np.clip(dist, -REL_CLIP, REL_CLIP) + REL_CLIP


def _bias_onehots():
    tw = 3 * LANES
    m = np.arange(ATT_VEC)
    dq = np.where(m <= ATT_KW, 512 - m, 512 - (m - ATT_VEC))
    dk = np.where(m < ATT_KW, m, m - ATT_VEC)
    ohq = np.zeros((tw, ATT_VEC), np.float32)
    ohk = np.zeros((tw, ATT_VEC), np.float32)
    ohq[_rel_index(dq), m] = 1.0
    ohk[_rel_index(dk), m] = 1.0
    return ohq, ohk


def _att_bias(table_pad):
    ohq, ohk = _bias_onehots()
    nslab = ATT_QB // SUBLANES

    def body(t_ref, ohq_ref, ohk_ref, bq_ref, bk_ref):
        tv = jnp.broadcast_to(t_ref[...], (SUBLANES, 3 * LANES))
        row = lax.broadcasted_iota(jnp.int32, (ATT_QB, ATT_KW), 0) // CHUNK
        col = lax.broadcasted_iota(jnp.int32, (ATT_QB, ATT_KW), 1) // CHUNK
        band = (col >= row) & (col <= row + ATT_BAND - 1)
        for oh_ref, out_ref in ((ohq_ref, bq_ref), (ohk_ref, bk_ref)):
            vec = _dot(tv, oh_ref[...], HIGHEST)[0:1, :]
            slab = jnp.concatenate([vec if b == 0 else pltpu.roll(vec, b, 1) for b in range(SUBLANES)], axis=0)
            rows = [slab if a == 0 else pltpu.roll(slab, SUBLANES * a, 1) for a in range(nslab)]
            full = jnp.concatenate(rows, axis=0)[:, :ATT_KW]
            out_ref[...] = jnp.where(band, full, NEG)

    h = table_pad.shape[0]
    oh_spec = pl.BlockSpec((3 * LANES, ATT_VEC), lambda i: (0, 0))
    out_spec = pl.BlockSpec((None, ATT_QB, ATT_KW), lambda i: (i, 0, 0))
    return pl.pallas_call(
        body, name="att_bias", grid=(h,),
        in_specs=[pl.BlockSpec((None, 1, 3 * LANES), lambda i: (i, 0, 0)), oh_spec, oh_spec],
        out_specs=[out_spec, out_spec],
        out_shape=[jax.ShapeDtypeStruct((h, ATT_QB, ATT_KW), F32)] * 2,
        compiler_params=_cparams("parallel"),
    )(table_pad, jnp.asarray(ohq), jnp.asarray(ohk))


def _head_masks():
    lane = lax.broadcasted_iota(jnp.int32, (1, LANES), 1)
    return [lane < ATT_DH, lane >= ATT_DH]


def _att_fwd(proj, bias_q):
    t = proj.shape[0]
    nb = t // ATT_QB
    scale = ATT_DH ** -0.5

    def body(q_ref, k0_ref, k1_ref, k2_ref, v0_ref, v1_ref, v2_ref, b_ref, o_ref, lse_ref, lset_ref):
        i = pl.program_id(0)
        q = (q_ref[...] * scale).astype(BF16)
        kk = jnp.concatenate([k0_ref[...], k1_ref[...], k2_ref[...]], axis=0).astype(BF16)
        vv = jnp.concatenate([v0_ref[...], v1_ref[...], v2_ref[...]], axis=0).astype(BF16)
        kpos = lax.broadcasted_iota(jnp.int32, (1, ATT_KW), 1) + (i - 2) * ATT_QB
        valid = kpos >= 0
        lane = lax.broadcasted_iota(jnp.int32, (1, LANES), 1)
        masks = _head_masks()
        lse_cols = jnp.zeros((ATT_QB, LANES), F32)
        for p in range(ATT_HEADS // 2):
            cs = slice(p * LANES, (p + 1) * LANES)
            qt, kt, vt = q[:, cs], kk[:, cs], vv[:, cs]
            acc = jnp.zeros((ATT_QB, LANES), F32)
            for sub in range(2):
                h = 2 * p + sub
                s = _dot_nt(jnp.where(masks[sub], qt, 0), kt) + b_ref[h]
                s = jnp.where(valid, s, NEG)
                mx = jnp.max(s, axis=-1, keepdims=True)
                e = jnp.exp(s - mx)
                l = jnp.sum(e, axis=-1, keepdims=True)
                acc = acc + _dot(e.astype(BF16), jnp.where(masks[sub], vt, 0)) * (1.0 / l)
                lse_cols = lse_cols + jnp.where(lane == h, mx + jnp.log(l), 0.0)
            o_ref[:, cs] = acc.astype(BF16)
        lse_ref[...] = lse_cols
        lset_ref[...] = lse_cols.T[0:SUBLANES, :]

    def kv_spec(off, cb):
        return pl.BlockSpec((ATT_QB, PB), lambda i: (jnp.maximum(i + off, 0), cb))

    return pl.pallas_call(
        body, name="att_fwd", grid=(nb,),
        in_specs=[pl.BlockSpec((ATT_QB, PB), lambda i: (i, CB_QB)),
                  kv_spec(-2, CB_KB), kv_spec(-1, CB_KB), kv_spec(0, CB_KB),
                  kv_spec(-2, CB_VB), kv_spec(-1, CB_VB), kv_spec(0, CB_VB),
                  pl.BlockSpec((ATT_HEADS, ATT_QB, ATT_KW), lambda i: (0, 0, 0))],
        out_specs=[pl.BlockSpec((ATT_QB, WIDTH_B), lambda i: (i, 0)),
                   pl.BlockSpec((ATT_QB, LANES), lambda i: (i, 0)),
                   pl.BlockSpec((SUBLANES, ATT_QB), lambda i: (0, i))],
        out_shape=[jax.ShapeDtypeStruct((t, WIDTH_B), BF16), jax.ShapeDtypeStruct((t, LANES), F32),
                   jax.ShapeDtypeStruct((SUBLANES, t), F32)],
        compiler_params=_cparams("parallel"),
    )(proj, proj, proj, proj, proj, proj, proj, bias_q)


def _att_dq(proj, bias_q, lse, d_ob, dproj):
    t = proj.shape[0]
    nb = t // ATT_QB
    scale = ATT_DH ** -0.5
    nslab = ATT_QB // SUBLANES

    def body(q_ref, k0_ref, k1_ref, k2_ref, v0_ref, v1_ref, v2_ref, b_ref, lse_ref, do_ref, dp_in_ref,
             dq_ref, dlt_ref, slab_ref):
        i = pl.program_id(0)

        @pl.when(i == 0)
        def _():
            slab_ref[...] = jnp.zeros_like(slab_ref)

        q = (q_ref[...] * scale).astype(BF16)
        kk = jnp.concatenate([k0_ref[...], k1_ref[...], k2_ref[...]], axis=0).astype(BF16)
        vv = jnp.concatenate([v0_ref[...], v1_ref[...], v2_ref[...]], axis=0).astype(BF16)
        do = do_ref[...].astype(BF16)
        kpos = lax.broadcasted_iota(jnp.int32, (1, ATT_KW), 1) + (i - 2) * ATT_QB
        valid = kpos >= 0
        lane = lax.broadcasted_iota(jnp.int32, (1, LANES), 1)
        masks = _head_masks()
        lse_all = lse_ref[...]
        dlt_cols = jnp.zeros((ATT_QB, LANES), F32)
        zpad = jnp.zeros((SUBLANES, ATT_VEC - ATT_KW), F32)
        for p in range(ATT_HEADS // 2):
            cs = slice(p * LANES, (p + 1) * LANES)
            qt, kt, vt, dot_ = q[:, cs], kk[:, cs], vv[:, cs], do[:, cs]
            acc = jnp.zeros((ATT_QB, LANES), F32)
            for sub in range(2):
                h = 2 * p + sub
                s = _dot_nt(jnp.where(masks[sub], qt, 0), kt) + b_ref[h]
                s = jnp.where(valid, s, NEG)
                pr = jnp.exp(s - lse_all[:, h:h + 1])
                dp = _dot_nt(jnp.where(masks[sub], dot_, 0), vt)
                dl = jnp.sum(pr * dp, axis=-1, keepdims=True)
                ds = pr * (dp - dl)
                acc = acc + _dot(ds.astype(BF16), jnp.where(masks[sub], kt, 0)) * scale
                dlt_cols = dlt_cols + jnp.where(lane == h, dl, 0.0)
                sl = jnp.zeros((SUBLANES, ATT_VEC), F32)
                for a in range(nslab):
                    piece = jnp.concatenate([ds[a * SUBLANES:(a + 1) * SUBLANES, :], zpad], axis=1)
                    sl = sl + (piece if a == 0 else pltpu.roll(piece, ATT_VEC - SUBLANES * a, 1))
                slab_ref[h] += sl
            dq_ref[:, cs] = acc.astype(BF16)
        dlt_ref[...] = dlt_cols.T[0:SUBLANES, :]

    def kv_spec(off, cb):
        return pl.BlockSpec((ATT_QB, PB), lambda i: (jnp.maximum(i + off, 0), cb))

    return pl.pallas_call(
        body, name="att_dq", grid=(nb,),
        in_specs=[pl.BlockSpec((ATT_QB, PB), lambda i: (i, CB_QB)),
                  kv_spec(-2, CB_KB), kv_spec(-1, CB_KB), kv_spec(0, CB_KB),
                  kv_spec(-2, CB_VB), kv_spec(-1, CB_VB), kv_spec(0, CB_VB),
                  pl.BlockSpec((ATT_HEADS, ATT_QB, ATT_KW), lambda i: (0, 0, 0)),
                  pl.BlockSpec((ATT_QB, LANES), lambda i: (i, 0)),
                  pl.BlockSpec((ATT_QB, WIDTH_B), lambda i: (i, 0)), pl.BlockSpec(memory_space=pl.ANY)],
        out_specs=[_dp_spec(ATT_QB, DP_QB),
                   pl.BlockSpec((SUBLANES, ATT_QB), lambda i: (0, i)),
                   pl.BlockSpec((ATT_HEADS, SUBLANES, ATT_VEC), lambda i: (0, 0, 0))],
        out_shape=[jax.ShapeDtypeStruct(dproj.shape, dproj.dtype), jax.ShapeDtypeStruct((SUBLANES, t), F32),
                   jax.ShapeDtypeStruct((ATT_HEADS, SUBLANES, ATT_VEC), F32)],
        input_output_aliases={10: 0},
        compiler_params=_cparams("arbitrary"),
    )(proj, proj, proj, proj, proj, proj, proj, bias_q, lse, d_ob, dproj)


def _att_dkv(proj, bias_k, lse_t, dlt_t, d_ob, dproj):
    t = proj.shape[0]
    nb = t // ATT_QB
    scale = ATT_DH ** -0.5

    def body(k_ref, v_ref, q0_ref, q1_ref, q2_ref, d0_ref, d1_ref, d2_ref, l0_ref, l1_ref, l2_ref,
             e0_ref, e1_ref, e2_ref, b_ref, dp_in_ref, dkv_ref):
        i = pl.program_id(0)
        k = k_ref[...].astype(BF16)
        v = v_ref[...].astype(BF16)
        qq = (jnp.concatenate([q0_ref[...], q1_ref[...], q2_ref[...]], axis=0) * scale).astype(BF16)
        do = jnp.concatenate([d0_ref[...], d1_ref[...], d2_ref[...]], axis=0).astype(BF16)
        lse = jnp.concatenate([l0_ref[...], l1_ref[...], l2_ref[...]], axis=1)
        dlt = jnp.concatenate([e0_ref[...], e1_ref[...], e2_ref[...]], axis=1)
        qpos = lax.broadcasted_iota(jnp.int32, (1, ATT_KW), 1) + i * ATT_QB
        valid = qpos < t
        masks = _head_masks()
        for p in range(ATT_HEADS // 2):
            cs = slice(p * LANES, (p + 1) * LANES)
            kt, vt, qt, dot_ = k[:, cs], v[:, cs], qq[:, cs], do[:, cs]
            acc_k = jnp.zeros((ATT_QB, LANES), F32)
            acc_v = jnp.zeros((ATT_QB, LANES), F32)
            for sub in range(2):
                h = 2 * p + sub
                st = _dot_nt(jnp.where(masks[sub], kt, 0), qt) + b_ref[h]
                st = jnp.where(valid, st, NEG)
                pt = jnp.exp(st - lse[h:h + 1, :])
                dot_m = jnp.where(masks[sub], dot_, 0)
                acc_v = acc_v + _dot(pt.astype(BF16), dot_m)
                dpt = _dot_nt(jnp.where(masks[sub], vt, 0), dot_)
                dst = pt * (dpt - dlt[h:h + 1, :])
                acc_k = acc_k + _dot(dst.astype(BF16), jnp.where(masks[sub], qt, 0))
            dkv_ref[:, cs] = acc_k.astype(BF16)
            dkv_ref[:, WIDTH_B + p * LANES:WIDTH_B + (p + 1) * LANES] = acc_v.astype(BF16)

    def q_spec(off, cb):
        return pl.BlockSpec((ATT_QB, PB), lambda i: (jnp.minimum(i + off, nb - 1), cb))

    def d_spec(off):
        return pl.BlockSpec((ATT_QB, WIDTH_B), lambda i: (jnp.minimum(i + off, nb - 1), 0))

    def r_spec(off):
        return pl.BlockSpec((SUBLANES, ATT_QB), lambda i: (0, jnp.minimum(i + off, nb - 1)))

    row = pl.BlockSpec((ATT_QB, WIDTH_B), lambda i: (i, 0))
    return pl.pallas_call(
        body, name="att_dkv", grid=(nb,),
        in_specs=[pl.BlockSpec((ATT_QB, PB), lambda i: (i, CB_KB)), pl.BlockSpec((ATT_QB, PB), lambda i: (i, CB_VB)),
                  q_spec(0, CB_QB), q_spec(1, CB_QB), q_spec(2, CB_QB),
                  d_spec(0), d_spec(1), d_spec(2), r_spec(0), r_spec(1), r_spec(2),
                  r_spec(0), r_spec(1), r_spec(2),
                  pl.BlockSpec((ATT_HEADS, ATT_QB, ATT_KW), lambda i: (0, 0, 0)), pl.BlockSpec(memory_space=pl.ANY)],
        out_specs=_dp_spec(ATT_QB, DP_KVB),
        out_shape=jax.ShapeDtypeStruct(dproj.shape, dproj.dtype),
        input_output_aliases={15: 0},
        compiler_params=_cparams("parallel"),
    )(proj, proj, proj, proj, proj, d_ob, d_ob, d_ob, lse_t, lse_t, lse_t, dlt_t, dlt_t, dlt_t, bias_k, dproj)


def _relbias_grad(slabs):
    ohq, _ = _bias_onehots()

    def body(s_ref, oh_ref, o_ref):
        sv = s_ref[...]
        vec = sv[0:1, :]
        for b in range(1, SUBLANES):
            vec = vec + pltpu.roll(sv[b:b + 1, :], ATT_VEC - b, 1)
        o_ref[...] = _dot_nt(jnp.broadcast_to(vec, (SUBLANES, ATT_VEC)), oh_ref[...], HIGHEST)[0:1, :]

    h = slabs.shape[0]
    return pl.pallas_call(
        body, name="att_dbias", grid=(h,),
        in_specs=[pl.BlockSpec((None, SUBLANES, ATT_VEC), lambda i: (i, 0, 0)),
                  pl.BlockSpec((3 * LANES, ATT_VEC), lambda i: (0, 0))],
        out_specs=pl.BlockSpec((None, 1, 3 * LANES), lambda i: (i, 0, 0)),
        out_shape=jax.ShapeDtypeStruct((h, 1, 3 * LANES), F32),
        compiler_params=_cparams("parallel"),
    )(slabs, jnp.asarray(ohq))


GDN_TM = 512
GDN_CB = 4
HALO = SUBLANES


def _conv_taps(ext, width, lead, n):
    return [(ext if k == width - 1 else pltpu.roll(ext, width - 1 - k, 0))[lead:lead + n] for k in range(width)]


def _prev_halo_spec(tm, width, cb):
    return pl.BlockSpec((HALO, width), lambda i: (jnp.maximum(i * (tm // HALO) - 1, 0), cb))


def _next_halo_spec(tm, width, cb, t):
    return pl.BlockSpec((HALO, width), lambda i: (jnp.minimum((i + 1) * (tm // HALO), t // HALO - 1), cb))


def _gdn_prep_fwd(proj, conv_w):
    t = proj.shape[0]
    tm = GDN_TM

    def body(q_ref, k_ref, v_ref, hq_ref, hk_ref, hv_ref, w_ref, qn_ref, kn_ref, vo_ref):
        first = pl.program_id(0) == 0
        for idx, (x_ref, h_ref, o_ref) in enumerate(((q_ref, hq_ref, qn_ref), (k_ref, hk_ref, kn_ref),
                                                      (v_ref, hv_ref, vo_ref))):
            halo = jnp.where(first, 0.0, h_ref[...])
            ext = jnp.concatenate([halo, x_ref[...]], axis=0)
            w = w_ref[:, idx * KEY_A:(idx + 1) * KEY_A]
            taps = _conv_taps(ext, GDN_CONV, HALO, tm)
            y = sum(w[k:k + 1, :] * taps[k] for k in range(GDN_CONV))
            a = y * _sigmoid(y)
            if idx < 2:
                for h in range(GDN_HEADS):
                    cs = slice(h * GDN_DK, (h + 1) * GDN_DK)
                    seg = a[:, cs]
                    o_ref[:, cs] = seg * lax.rsqrt(jnp.sum(seg * seg, axis=-1, keepdims=True) + EPS)
            else:
                o_ref[...] = a

    row = pl.BlockSpec((tm, KEY_A), lambda i: (i, 0))
    return pl.pallas_call(
        body, name="gdn_prep_fwd", grid=(t // tm,),
        in_specs=[pl.BlockSpec((tm, PB), lambda i: (i, CB_QA)), pl.BlockSpec((tm, PB), lambda i: (i, CB_KA)),
                  pl.BlockSpec((tm, PB), lambda i: (i, CB_VA)),
                  _prev_halo_spec(tm, PB, CB_QA), _prev_halo_spec(tm, PB, CB_KA), _prev_halo_spec(tm, PB, CB_VA),
                  pl.BlockSpec((GDN_CONV, 3 * KEY_A), lambda i: (0, 0))],
        out_specs=[row, row, row],
        out_shape=[jax.ShapeDtypeStruct((t, KEY_A), F32)] * 3,
        compiler_params=_cparams("parallel"),
    )(proj, proj, proj, proj, proj, proj, conv_w)


def _gdn_prep_bwd(proj, conv_w, dqn, dkn, dv, dproj):
    t = proj.shape[0]
    tm = GDN_TM
    nt = t // tm
    n_ext = tm + HALO

    def body(q_ref, k_ref, v_ref, pq_ref, pk_ref, pv_ref, nq_ref, nk_ref, nv_ref,
             dq_ref, dk_ref, dv_ref, ndq_ref, ndk_ref, ndv_ref, w_ref, dp_in_ref, out_ref, dw_ref):
        i = pl.program_id(0)
        first, last = i == 0, i == nt - 1

        @pl.when(first)
        def _():
            dw_ref[...] = jnp.zeros_like(dw_ref)

        groups = ((q_ref, pq_ref, nq_ref, dq_ref, ndq_ref), (k_ref, pk_ref, nk_ref, dk_ref, ndk_ref),
                  (v_ref, pv_ref, nv_ref, dv_ref, ndv_ref))
        for idx, (x_ref, p_ref, n_ref, d_ref, nd_ref) in enumerate(groups):
            cs_all = slice(idx * KEY_A, (idx + 1) * KEY_A)
            ext = jnp.concatenate([jnp.where(first, 0.0, p_ref[...]), x_ref[...], jnp.where(last, 0.0, n_ref[...])], axis=0)
            w = w_ref[:, cs_all]
            taps = _conv_taps(ext, GDN_CONV, HALO, n_ext)
            y = sum(w[k:k + 1, :] * taps[k] for k in range(GDN_CONV))
            sg = _sigmoid(y)
            a = y * sg
            dup = jnp.concatenate([d_ref[...], jnp.where(last, 0.0, nd_ref[...])], axis=0)
            if idx < 2:
                segs = []
                for h in range(GDN_HEADS):
                    cs = slice(h * GDN_DK, (h + 1) * GDN_DK)
                    seg = a[:, cs]
                    r = lax.rsqrt(jnp.sum(seg * seg, axis=-1, keepdims=True) + EPS)
                    nrm = seg * r
                    dn = dup[:, cs]
                    segs.append(r * (dn - nrm * jnp.sum(dn * nrm, axis=-1, keepdims=True)))
                da = jnp.concatenate(segs, axis=1)
            else:
                da = dup
            dy = da * sg * (1.0 + y * (1.0 - sg))
            dx = sum(w[k:k + 1, :] * (dy if k == GDN_CONV - 1 else pltpu.roll(dy, n_ext - (GDN_CONV - 1 - k), 0))[:tm]
                     for k in range(GDN_CONV))
            out_ref[:, cs_all] = dx.astype(BF16)
            for k in range(GDN_CONV):
                dw_ref[k:k + 1, cs_all] += jnp.sum(dy[:tm] * taps[k][:tm], axis=0, keepdims=True)

    row = pl.BlockSpec((tm, KEY_A), lambda i: (i, 0))
    nrow = _next_halo_spec(tm, KEY_A, 0, t)
    return pl.pallas_call(
        body, name="gdn_prep_bwd", grid=(nt,),
        in_specs=[pl.BlockSpec((tm, PB), lambda i: (i, CB_QA)), pl.BlockSpec((tm, PB), lambda i: (i, CB_KA)),
                  pl.BlockSpec((tm, PB), lambda i: (i, CB_VA)),
                  _prev_halo_spec(tm, PB, CB_QA), _prev_halo_spec(tm, PB, CB_KA), _prev_halo_spec(tm, PB, CB_VA),
                  _next_halo_spec(tm, PB, CB_QA, t), _next_halo_spec(tm, PB, CB_KA, t), _next_halo_spec(tm, PB, CB_VA, t),
                  row, row, row, nrow, nrow, nrow,
                  pl.BlockSpec((GDN_CONV, 3 * KEY_A), lambda i: (0, 0)), pl.BlockSpec(memory_space=pl.ANY)],
        out_specs=[_dp_spec(tm, DP_QKVA), pl.BlockSpec((SUBLANES, 3 * KEY_A), lambda i: (0, 0))],
        out_shape=[jax.ShapeDtypeStruct(dproj.shape, dproj.dtype), jax.ShapeDtypeStruct((SUBLANES, 3 * KEY_A), F32)],
        input_output_aliases={16: 0},
        compiler_params=_cparams("arbitrary"),
    )(proj, proj, proj, proj, proj, proj, proj, proj, proj, dqn, dkn, dv, dqn, dkn, dv, conv_w, dproj)


class _Pair(dict):
    __getattr__ = dict.__getitem__
    __setattr__ = dict.__setitem__


def _pairs_to_lanes(cols):
    lane = lax.broadcasted_iota(jnp.int32, (1, LANES), 1)
    out = jnp.zeros((cols[0].shape[0], LANES), F32)
    for p, col in enumerate(cols):
        out = out + jnp.where(lane == p, col, 0.0)
    return out


def _gdn_terms(bd, par, kn_ref, qn_ref):
    c = CHUNK
    ii = lax.broadcasted_iota(jnp.int32, (c, c), 0)
    jj = lax.broadcasted_iota(jnp.int32, (c, c), 1)
    strict, incl = ii > jj, ii >= jj
    ltri = incl.astype(F32)
    ts = []
    for cc in range(GDN_CB):
        for h in range(GDN_HEADS):
            t = _Pair(cc=cc, h=h, rows=slice(cc * c, (cc + 1) * c), cs=slice(h * GDN_DK, (h + 1) * GDN_DK),
                      strict=strict, incl=incl)
            t.beta = _sigmoid(bd[t.rows, h:h + 1])
            t.ea = jnp.exp(par[0:1, h:h + 1])
            t.sp_arg = bd[t.rows, GDN_HEADS + h:GDN_HEADS + h + 1] + par[1:2, h:h + 1]
            t.g = -t.ea * _softplus(t.sp_arg)
            t.k = kn_ref[t.rows, t.cs]
            t.q = qn_ref[t.rows, t.cs] * (GDN_DK ** -0.5)
            t.kb, t.qb = t.k.astype(BF16), t.q.astype(BF16)
            ts.append(t)
    gall = _dot(ltri, _pairs_to_lanes([t.g for t in ts]), HIGHEST)
    for p, t in enumerate(ts):
        t.gb = jnp.broadcast_to(gall[:, p:p + 1], (c, GDN_DK))
    for t in ts:
        t.kk = _dot_nt(t.kb, t.kb)
        t.qk = _dot_nt(t.qb, t.kb)
    for t in ts:
        gc = t.gb[:, :c]
        diff = gc - gc.T
        t.dec_s = jnp.exp(jnp.where(strict, diff, NEG))
        t.dec_i = jnp.exp(jnp.where(incl, diff, NEG))
        t.gam = jnp.exp(t.gb)
        glast = t.gb[c - 1:c, :]
        t.e_rest = jnp.exp(glast - t.gb)
        t.gl = jnp.exp(glast)
        t.p = t.qk * t.dec_i
    return ts


def _gdn_fwd(qn, kn, v, proj, par, gnw):
    t = qn.shape[0]
    c = CHUNK
    nc = t // c
    r_ = GDN_CB * c

    def body(qn_ref, kn_ref, v_ref, bd_ref, z_ref, par_ref, gnw_ref,
             oan_ref, o_ref, sp_ref, w_ref, u_ref, tm_ref, s_ref):
        @pl.when(pl.program_id(0) == 0)
        def _():
            s_ref[...] = jnp.zeros_like(s_ref)

        bd, par, gnw_v = bd_ref[...], par_ref[...], gnw_ref[...]
        eye = (lax.broadcasted_iota(jnp.int32, (c, c), 0) == lax.broadcasted_iota(jnp.int32, (c, c), 1)).astype(F32)
        ts = _gdn_terms(bd, par, kn_ref, qn_ref)
        for t in ts:
            t.vv = v_ref[t.rows, t.cs]
            t.x = -(t.beta * t.kk * t.dec_s)
            t.tinv = eye + t.x
        for t in ts:
            t.xs = _split(t.x)
        for _ in range(5):
            for t in ts:
                t.xs = _split(_dot3s(t.xs, t.xs))
            for t in ts:
                t.tinv = t.tinv + _dot3s(_split(t.tinv), t.xs)
        for t in ts:
            tsp = _split(t.tinv)
            t.wm = _dot3s(tsp, _split((t.beta * t.gam) * t.k))
            t.uv = _dot3s(tsp, _split(t.beta * t.vv))
        for t in ts:
            w_ref[t.rows, t.cs] = t.wm
            tm_ref[t.cc, t.h] = t.tinv
            t.wb = t.wm.astype(BF16)
            t.qgb = (t.q * t.gam).astype(BF16)
            t.kdb = (t.k * t.e_rest).astype(BF16)
            t.pb = t.p.astype(BF16)
        for cc in range(GDN_CB):
            tc = [t for t in ts if t.cc == cc]
            for t in tc:
                t.sh = s_ref[t.h]
                t.sb = t.sh.astype(BF16)
                sp_ref[cc, t.h] = t.sh
            for t in tc:
                t.ws = _dot(t.wb, t.sb)
                t.qs = _dot(t.qgb, t.sb)
            for t in tc:
                t.u = t.uv - t.ws
                t.ub = t.u.astype(BF16)
            for t in tc:
                t.pu = _dot(t.pb, t.ub)
                t.ku = _dot_tn(t.kdb, t.ub)
            for t in tc:
                t.o = t.qs + t.pu
                s_ref[t.h] = t.gl * t.sh + t.ku
                u_ref[t.rows, t.cs] = t.u
                o_ref[t.rows, t.cs] = t.o
        for t in ts:
            zz = z_ref[t.rows, t.cs]
            rr = lax.rsqrt(jnp.mean(t.o * t.o, axis=-1, keepdims=True) + EPS)
            oan_ref[t.rows, t.cs] = ((t.o * rr) * gnw_v * (zz * _sigmoid(zz))).astype(BF16)

    row = pl.BlockSpec((r_, KEY_A), lambda i: (i, 0))
    return pl.pallas_call(
        body, name="gdn_fwd", grid=(nc // GDN_CB,),
        in_specs=[row, row, row, pl.BlockSpec((r_, LANES), lambda i: (i, CB_BD)),
                  pl.BlockSpec((r_, PB), lambda i: (i, CB_ZA)),
                  pl.BlockSpec((SUBLANES, LANES), lambda i: (0, 0)), pl.BlockSpec((1, GDN_DK), lambda i: (0, 0))],
        out_specs=[row, row, pl.BlockSpec((GDN_CB, GDN_HEADS, GDN_DK, GDN_DK), lambda i: (i, 0, 0, 0)),
                   row, row, pl.BlockSpec((GDN_CB, GDN_HEADS, c, c), lambda i: (i, 0, 0, 0))],
        out_shape=[jax.ShapeDtypeStruct((t, KEY_A), BF16), jax.ShapeDtypeStruct((t, KEY_A), F32),
                   jax.ShapeDtypeStruct((nc, GDN_HEADS, GDN_DK, GDN_DK), F32),
                   jax.ShapeDtypeStruct((t, KEY_A), F32), jax.ShapeDtypeStruct((t, KEY_A), F32),
                   jax.ShapeDtypeStruct((nc, GDN_HEADS, c, c), F32)],
        scratch_shapes=[pltpu.VMEM((GDN_HEADS, GDN_DK, GDN_DK), F32)],
        compiler_params=_cparams("arbitrary"),
    )(qn, kn, v, proj, proj, par, gnw)


def _gdn_bwd(qn, kn, v, proj, par, gnw, o, sprev, wst, ust, tst, d_oan, dproj):
    t = qn.shape[0]
    c = CHUNK
    nc = t // c
    nb = nc // GDN_CB
    r_ = GDN_CB * c

    def body(qn_ref, kn_ref, v_ref, bd_ref, z_ref, par_ref, gnw_ref, o_ref, sp_ref, w_ref, u_ref, tm_ref, do_ref,
             dp_in_ref, dqn_ref, dkn_ref, dv_ref, dzb_ref, acc_ref, ds_ref):
        @pl.when(pl.program_id(0) == 0)
        def _():
            ds_ref[...] = jnp.zeros_like(ds_ref)
            acc_ref[...] = jnp.zeros_like(acc_ref)

        bd, par, gnw_v = bd_ref[...], par_ref[...], gnw_ref[...]
        lane = lax.broadcasted_iota(jnp.int32, (1, LANES), 1)
        rix = lax.broadcasted_iota(jnp.int32, (c, 1), 0)
        ii = lax.broadcasted_iota(jnp.int32, (c, c), 0)
        jj = lax.broadcasted_iota(jnp.int32, (c, c), 1)
        upper = (jj >= ii).astype(F32)
        acc_a = jnp.zeros((1, LANES), F32)
        acc_d = jnp.zeros((1, LANES), F32)
        acc_g = jnp.zeros((1, LANES), F32)
        ts = _gdn_terms(bd, par, kn_ref, qn_ref)
        for t in ts:
            t.vv = v_ref[t.rows, t.cs]
            t.sh = sp_ref[t.cc, t.h]
            t.sb = t.sh.astype(BF16)
            t.wm, t.u, t.tinv = w_ref[t.rows, t.cs], u_ref[t.rows, t.cs], tm_ref[t.cc, t.h]
            t.wb, t.ub = t.wm.astype(BF16), t.u.astype(BF16)
            ov, zz, dout = o_ref[t.rows, t.cs], z_ref[t.rows, t.cs], do_ref[t.rows, t.cs]
            sg = _sigmoid(zz)
            sil = zz * sg
            rr = lax.rsqrt(jnp.mean(ov * ov, axis=-1, keepdims=True) + EPS)
            on = ov * rr
            dzb_ref[t.rows, t.cs] = (dout * on * gnw_v * (sg * (1.0 + zz * (1.0 - sg)))).astype(BF16)
            acc_g = acc_g + jnp.sum(dout * on * sil, axis=0, keepdims=True)
            don = dout * gnw_v * sil
            t.dob = (rr * (don - on * jnp.mean(don * on, axis=-1, keepdims=True))).astype(BF16)
            t.qg = t.q * t.gam
            t.kd = t.k * t.e_rest
            t.qgb, t.kdb = t.qg.astype(BF16), t.kd.astype(BF16)
            t.ptb = t.p.T.astype(BF16)
        for cc in reversed(range(GDN_CB)):
            tc = [t for t in ts if t.cc == cc]
            for t in tc:
                t.dsn = ds_ref[t.h]
                t.dsnb = t.dsn.astype(BF16)
            for t in tc:
                t.du = _dot(t.ptb, t.dob) + _dot(t.kdb, t.dsnb)
                t.dkd = _dot_nt(t.ub, t.dsnb)
                t.dgl = jnp.sum(jnp.sum(t.dsn * t.sh, axis=1, keepdims=True), axis=0, keepdims=True)
            for t in tc:
                t.dub = t.du.astype(BF16)
            for t in tc:
                ds_ref[t.h] = t.gl * t.dsn + _dot_tn(t.qgb, t.dob) - _dot_tn(t.wb, t.dub)
        for t in ts:
            t.dqg = _dot_nt(t.dob, t.sb)
            t.dp = _dot_nt(t.dob, t.ub)
            t.dwm = -_dot_nt(t.dub, t.sb)
            t.uv = t.u + _dot(t.wb, t.sb)
        for t in ts:
            tsp = _split(t.tinv.T)
            t.dbk = _dot3s(tsp, _split(t.dwm))
            t.dbv = _dot3s(tsp, _split(t.du))
        for t in ts:
            d_a = -(_dot_nt(t.dbk.astype(BF16), t.wb) + _dot_nt(t.dbv.astype(BF16), t.uv.astype(BF16)))
            t.d_a = jnp.where(t.strict, d_a, 0.0)
        for t in ts:
            t.dkk = t.d_a * t.beta * t.dec_s
            t.dqk = t.dp * t.dec_i
            t.dqkb = t.dqk.astype(BF16)
        for t in ts:
            t.dq = _dot(t.dqkb, t.kb) + t.dqg * t.gam
            t.dk = (t.dbk * (t.beta * t.gam) + _dot_tn(t.dqkb, t.qb) + _dot((t.dkk + t.dkk.T).astype(BF16), t.kb)
                    + t.dkd * t.e_rest)
        for t in ts:
            dbeta = (jnp.sum(t.d_a * t.kk * t.dec_s, axis=-1, keepdims=True)
                     + jnp.sum(t.dbk * t.k * t.gam, axis=-1, keepdims=True) + jnp.sum(t.dbv * t.vv, axis=-1, keepdims=True))
            t.dbl = dbeta * t.beta * (1.0 - t.beta)
            dv_ref[t.rows, t.cs] = t.dbv * t.beta
            bk = (t.beta * t.gam) * t.k
            zc = jnp.sum(t.dkd * t.kd, axis=-1, keepdims=True)
            xs = t.dkk * t.kk + t.dp * t.p
            dgc = (jnp.sum(xs, axis=-1, keepdims=True) - jnp.sum(xs.T, axis=-1, keepdims=True)
                   + jnp.sum(t.dbk * bk, axis=-1, keepdims=True) + jnp.sum(t.dqg * t.qg, axis=-1, keepdims=True) - zc)
            dglast = jnp.sum(zc, axis=0, keepdims=True) + t.dgl * t.gl[:, 0:1]
            t.dgc = dgc + jnp.where(rix == c - 1, dglast, 0.0)
        dgall = _dot(upper, _pairs_to_lanes([t.dgc for t in ts]), HIGHEST)
        for p, t in enumerate(ts):
            t.dg = dgall[:, p:p + 1]
        dbd_tiles = [jnp.zeros((c, LANES), F32) for _ in range(GDN_CB)]
        for t in ts:
            ddl = t.dg * (-t.ea) * _sigmoid(t.sp_arg)
            acc_a = acc_a + jnp.where(lane == t.h, jnp.sum(t.dg * t.g, axis=0, keepdims=True), 0.0)
            acc_d = acc_d + jnp.where(lane == t.h, jnp.sum(ddl, axis=0, keepdims=True), 0.0)
            dbd_tiles[t.cc] = (dbd_tiles[t.cc] + jnp.where(lane == t.h, t.dbl, 0.0)
                               + jnp.where(lane == GDN_HEADS + t.h, ddl, 0.0))
            dqn_ref[t.rows, t.cs] = t.dq * (GDN_DK ** -0.5)
            dkn_ref[t.rows, t.cs] = t.dk
        for cc in range(GDN_CB):
            dzb_ref[cc * c:(cc + 1) * c, KEY_A:KEY_A + LANES] = dbd_tiles[cc].astype(BF16)
        acc_ref[0:1, :] += acc_a
        acc_ref[1:2, :] += acc_d
        acc_ref[2:3, :] += acc_g

    def rev(i):
        return nb - 1 - i

    row = pl.BlockSpec((r_, KEY_A), lambda i: (rev(i), 0))
    st = pl.BlockSpec((GDN_CB, GDN_HEADS, GDN_DK, GDN_DK), lambda i: (rev(i), 0, 0, 0))
    tt_spec = pl.BlockSpec((GDN_CB, GDN_HEADS, c, c), lambda i: (rev(i), 0, 0, 0))
    return pl.pallas_call(
        body, name="gdn_bwd", grid=(nb,),
        in_specs=[row, row, row, pl.BlockSpec((r_, LANES), lambda i: (rev(i), CB_BD)),
                  pl.BlockSpec((r_, PB), lambda i: (rev(i), CB_ZA)),
                  pl.BlockSpec((SUBLANES, LANES), lambda i: (0, 0)), pl.BlockSpec((1, GDN_DK), lambda i: (0, 0)),
                  row, st, row, row, tt_spec, row, pl.BlockSpec(memory_space=pl.ANY)],
        out_specs=[row, row, row, _dp_spec(r_, DP_ZBD, rev), pl.BlockSpec((SUBLANES, LANES), lambda i: (0, 0))],
        out_shape=[jax.ShapeDtypeStruct((t, KEY_A), F32)] * 3 + [jax.ShapeDtypeStruct(dproj.shape, dproj.dtype),
                                                                jax.ShapeDtypeStruct((SUBLANES, LANES), F32)],
        input_output_aliases={13: 3},
        scratch_shapes=[pltpu.VMEM((GDN_HEADS, GDN_DK, GDN_DK), F32)],
        compiler_params=_cparams("arbitrary"),
    )(qn, kn, v, proj, proj, par, gnw, o, sprev, wst, ust, tst, d_oan, dproj)


def _merge_fwd(oan, ob, proj, x, wba, wbb, wout, tm=512):
    t = x.shape[0]

    def body(oa_ref, ob_ref, ga_ref, gb_ref, x_ref, wba_ref, wbb_ref, wout_ref, x2_ref):
        ya = _dot(oa_ref[...], wba_ref[...])
        yb = _dot(ob_ref[...], wbb_ref[...])
        mix = _sigmoid(ga_ref[...]) * ya + _sigmoid(gb_ref[...]) * yb
        x2_ref[...] = x_ref[...] + _dot(mix.astype(BF16), wout_ref[...])

    half = pl.BlockSpec((tm, KEY_A), lambda i: (i, 0))
    row = pl.BlockSpec((tm, D_MODEL), lambda i: (i, 0))
    wsmall = pl.BlockSpec((KEY_A, D_MODEL), lambda i: (0, 0))
    return pl.pallas_call(
        body, name="merge_fwd", grid=(t // tm,),
        in_specs=[half, half, pl.BlockSpec((tm, D_MODEL), lambda i: (i, CB_GA)),
                  pl.BlockSpec((tm, D_MODEL), lambda i: (i, CB_GB)), row, wsmall, wsmall,
                  pl.BlockSpec((D_MODEL, D_MODEL), lambda i: (0, 0))],
        out_specs=row,
        out_shape=jax.ShapeDtypeStruct((t, D_MODEL), F32),
        compiler_params=_cparams("parallel"),
    )(oan, ob, proj, proj, x, wba, wbb, wout)


def _merge_bwd(dx2b, oan, ob, proj, wba, wbb, wout_t, wba_t, wbb_t, tm=512):
    t = dx2b.shape[0]

    def body(dx_ref, oa_ref, ob_ref, ga_ref, gb_ref, wba_ref, wbb_ref, woutt_ref, wbat_ref, wbbt_ref,
             dg_ref, doa_ref, dob_ref, mix_ref, dya_ref, dyb_ref):
        dmix = _dot(dx_ref[...], woutt_ref[...])
        ya = _dot(oa_ref[...], wba_ref[...])
        yb = _dot(ob_ref[...], wbb_ref[...])
        sa, sb = _sigmoid(ga_ref[...]), _sigmoid(gb_ref[...])
        mix_ref[...] = (sa * ya + sb * yb).astype(BF16)
        dg_ref[:, :D_MODEL] = (dmix * ya * sa * (1.0 - sa)).astype(BF16)
        dg_ref[:, D_MODEL:] = (dmix * yb * sb * (1.0 - sb)).astype(BF16)
        dya = (dmix * sa).astype(BF16)
        dyb = (dmix * sb).astype(BF16)
        dya_ref[...] = dya
        dyb_ref[...] = dyb
        doa_ref[...] = _dot(dya, wbat_ref[...])
        dob_ref[...] = _dot(dyb, wbbt_ref[...])

    half = pl.BlockSpec((tm, KEY_A), lambda i: (i, 0))
    row = pl.BlockSpec((tm, D_MODEL), lambda i: (i, 0))
    wsmall = pl.BlockSpec((KEY_A, D_MODEL), lambda i: (0, 0))
    wsmall_t = pl.BlockSpec((D_MODEL, KEY_A), lambda i: (0, 0))
    big = jax.ShapeDtypeStruct((t, D_MODEL), BF16)
    return pl.pallas_call(
        body, name="merge_bwd", grid=(t // tm,),
        in_specs=[row, half, half, pl.BlockSpec((tm, D_MODEL), lambda i: (i, CB_GA)),
                  pl.BlockSpec((tm, D_MODEL), lambda i: (i, CB_GB)), wsmall, wsmall,
                  pl.BlockSpec((D_MODEL, D_MODEL), lambda i: (0, 0)), wsmall_t, wsmall_t],
        out_specs=[_dp_spec(tm, DP_GATES), half, half, row, row, row],
        out_shape=[jax.ShapeDtypeStruct((t, PROJ_W), BF16), jax.ShapeDtypeStruct((t, KEY_A), F32),
                   jax.ShapeDtypeStruct((t, KEY_A), F32), big, big, big],
        compiler_params=_cparams("parallel"),
    )(dx2b, oan, ob, proj, proj, wba, wbb, wout_t, wba_t, wbb_t)


FFN_TM = 128
FFN_W = 2 * D_FF


def _ffn_tail(up, cw, cb, wdown, x2, tgt, w3):
    t = x2.shape[0]
    tm = FFN_TM

    def body(up_ref, halo_ref, cw_ref, cb_ref, wd_ref, x2_ref, tgt_ref, w3_ref, dx_ref, dxb_ref, act_ref, u_ref, acc_ref):
        first = pl.program_id(0) == 0

        @pl.when(first)
        def _():
            acc_ref[...] = jnp.zeros_like(acc_ref)

        ext = jnp.concatenate([jnp.where(first, 0.0, halo_ref[...]), up_ref[...]], axis=0)
        taps = _conv_taps(ext, FFN_CONV, HALO, tm)
        cw_v = cw_ref[...]
        u = sum(cw_v[k:k + 1, :] * taps[k] for k in range(FFN_CONV)) + cb_ref[...]
        u_ref[...] = u
        gate, upp = u[:, :D_FF], u[:, D_FF:]
        act = (gate * _sigmoid(gate) * upp).astype(BF16)
        act_ref[...] = act
        x3 = x2_ref[...] + _dot(act, wd_ref[...])
        r = lax.rsqrt(jnp.mean(x3 * x3, axis=-1, keepdims=True) + EPS)
        xh = x3 * r
        w3v = w3_ref[...]
        err = xh * w3v - tgt_ref[...]
        loss = 0.5 * jnp.sum(jnp.mean(err * err, axis=-1, keepdims=True), axis=0, keepdims=True)
        dy = err * (1.0 / D_MODEL)
        acc_ref[0:1, :] += jnp.sum(dy * xh, axis=0, keepdims=True)
        acc_ref[1:2, :] += jnp.broadcast_to(loss, (1, D_MODEL))
        dxh = dy * w3v
        dx = r * (dxh - xh * jnp.mean(dxh * xh, axis=-1, keepdims=True))
        dx_ref[...] = dx
        dxb_ref[...] = dx.astype(BF16)

    row = pl.BlockSpec((tm, D_MODEL), lambda i: (i, 0))
    wide = pl.BlockSpec((tm, FFN_W), lambda i: (i, 0))
    vec = pl.BlockSpec((1, D_MODEL), lambda i: (0, 0))
    return pl.pallas_call(
        body, name="ffn_tail", grid=(t // tm,),
        in_specs=[wide, _prev_halo_spec(tm, FFN_W, 0),
                  pl.BlockSpec((SUBLANES, FFN_W), lambda i: (0, 0)), pl.BlockSpec((1, FFN_W), lambda i: (0, 0)),
                  pl.BlockSpec((D_FF, D_MODEL), lambda i: (0, 0)), row, row, vec],
        out_specs=[row, row, pl.BlockSpec((tm, D_FF), lambda i: (i, 0)), wide,
                   pl.BlockSpec((SUBLANES, D_MODEL), lambda i: (0, 0))],
        out_shape=[jax.ShapeDtypeStruct((t, D_MODEL), F32), jax.ShapeDtypeStruct((t, D_MODEL), BF16),
                   jax.ShapeDtypeStruct((t, D_FF), BF16), jax.ShapeDtypeStruct((t, FFN_W), F32),
                   jax.ShapeDtypeStruct((SUBLANES, D_MODEL), F32)],
        compiler_params=_cparams("arbitrary"),
    )(up, up, cw, cb, wdown, x2, tgt, w3)


def _ffn_bwd_act(dx3b, wdown_t, u):
    t = u.shape[0]
    tm = FFN_TM

    def body(dx_ref, wdt_ref, u_ref, du_ref, acc_ref):
        @pl.when(pl.program_id(0) == 0)
        def _():
            acc_ref[...] = jnp.zeros_like(acc_ref)

        dact = _dot(dx_ref[...], wdt_ref[...])
        gate, upp = u_ref[:, :D_FF], u_ref[:, D_FF:]
        sg = _sigmoid(gate)
        du = jnp.concatenate([dact * upp * (sg * (1.0 + gate * (1.0 - sg))), dact * (gate * sg)], axis=1)
        du_ref[...] = du
        acc_ref[0:1, :] += jnp.sum(du, axis=0, keepdims=True)

    wide = pl.BlockSpec((tm, FFN_W), lambda i: (i, 0))
    return pl.pallas_call(
        body, name="ffn_bwd_act", grid=(t // tm,),
        in_specs=[pl.BlockSpec((tm, D_MODEL), lambda i: (i, 0)), pl.BlockSpec((D_MODEL, D_FF), lambda i: (0, 0)), wide],
        out_specs=[wide, pl.BlockSpec((SUBLANES, FFN_W), lambda i: (0, 0))],
        out_shape=[jax.ShapeDtypeStruct((t, FFN_W), F32), jax.ShapeDtypeStruct((SUBLANES, FFN_W), F32)],
        compiler_params=_cparams("arbitrary"),
    )(dx3b, wdown_t, u)


def _ffn_bwd_conv(du, up, cw):
    t = du.shape[0]
    tm = FFN_TM
    nt = t // tm
    n_ext = tm + HALO

    def body(du_ref, nxt_ref, up_ref, cw_ref, o_ref, acc_ref):
        i = pl.program_id(0)

        @pl.when(i == 0)
        def _():
            acc_ref[...] = jnp.zeros_like(acc_ref)

        ext = jnp.concatenate([du_ref[...], jnp.where(i == nt - 1, 0.0, nxt_ref[...])], axis=0)
        cw_v = cw_ref[...]
        upv = up_ref[...]
        acc = None
        for k in range(FFN_CONV):
            shift = FFN_CONV - 1 - k
            tap = (ext if shift == 0 else pltpu.roll(ext, n_ext - shift, 0))[:tm]
            term = cw_v[k:k + 1, :] * tap
            acc = term if acc is None else acc + term
            acc_ref[k:k + 1, :] += jnp.sum(tap * upv, axis=0, keepdims=True)
        o_ref[...] = acc.astype(BF16)

    wide = pl.BlockSpec((tm, FFN_W), lambda i: (i, 0))
    sums = pl.BlockSpec((SUBLANES, FFN_W), lambda i: (0, 0))
    return pl.pallas_call(
        body, name="ffn_bwd_conv", grid=(nt,),
        in_specs=[wide, _next_halo_spec(tm, FFN_W, 0, t), wide, sums],
        out_specs=[wide, sums],
        out_shape=[jax.ShapeDtypeStruct((t, FFN_W), BF16), jax.ShapeDtypeStruct((SUBLANES, FFN_W), F32)],
        compiler_params=_cparams("arbitrary"),
    )(du, du, up, cw)


def _adamw(parts, w, m, v, name, tr):
    r, cols = w.shape

    def body(p_ref, w_ref, m_ref, v_ref, g_ref, d_ref, mo_ref, vo_ref):
        g = p_ref[0].astype(F32)
        for s in range(1, N_DEV):
            g = g + p_ref[s].astype(F32)
        mm = ADAM_B1 * m_ref[...] + (1.0 - ADAM_B1) * g
        vv = ADAM_B2 * v_ref[...] + (1.0 - ADAM_B2) * (g * g)
        m_hat = mm / (1.0 - ADAM_B1 ** ADAM_STEP)
        v_hat = vv / (1.0 - ADAM_B2 ** ADAM_STEP)
        g_ref[...] = g
        d_ref[...] = -ADAM_LR * (m_hat / (jnp.sqrt(v_hat) + ADAM_EPS) + ADAM_WD * w_ref[...])
        mo_ref[...] = mm
        vo_ref[...] = vv

    assert r % tr == 0
    row = pl.BlockSpec((tr, cols), lambda i: (i, 0))
    return pl.pallas_call(
        body, name=name, grid=(r // tr,),
        in_specs=[pl.BlockSpec((N_DEV, tr, cols), lambda i: (0, i, 0)), row, row, row],
        out_specs=[row, row, row, row],
        out_shape=[jax.ShapeDtypeStruct((r, cols), F32)] * 4,
        compiler_params=_cparams("parallel"),
    )(parts, w, m, v)


def _mesh_pos():
    return lax.axis_index("x"), lax.axis_index("y"), lax.axis_index("c")


def _peer(pos, k):
    x, y, c = pos
    return (x ^ ((k >> 2) & 1), y ^ ((k >> 1) & 1), c ^ (k & 1))


def _flat_id(pos):
    return 4 * pos[0] + 2 * pos[1] + pos[2]


def _exchange_copies(srcs, dsts, scatter, send_sems, recv_sems, loc_sems):
    pos = _mesh_pos()
    me = _flat_id(pos)
    local, remote = [], []
    for j, (src, dst) in enumerate(zip(srcs, dsts)):
        local.append(pltpu.make_async_copy(src.at[me] if scatter[j] else src, dst.at[me], loc_sems.at[j]))
        for k in range(1, N_DEV):
            to = _peer(pos, k)
            remote.append(pltpu.make_async_remote_copy(
                src_ref=src.at[_flat_id(to)] if scatter[j] else src, dst_ref=dst.at[me],
                send_sem=send_sems.at[j, k - 1], recv_sem=recv_sems.at[j, k - 1],
                device_id=to, device_id_type=pl.DeviceIdType.MESH))
    return local, remote


def _exchange_shapes(arrays, scatter):
    return [jax.ShapeDtypeStruct(a.shape if s else (N_DEV,) + a.shape, a.dtype) for a, s in zip(arrays, scatter)]


def _exchange_sems(n):
    return [pltpu.SemaphoreType.DMA((n, N_DEV - 1)), pltpu.SemaphoreType.DMA((n, N_DEV - 1)), pltpu.SemaphoreType.DMA((n,))]


def _exchange(arrays, scatter, name):
    n = len(arrays)
    any_spec = pl.BlockSpec(memory_space=pl.ANY)

    def body(*refs):
        local, remote = _exchange_copies(refs[:n], refs[n:2 * n], scatter, *refs[2 * n:])
        for cp in local + remote:
            cp.start()
        for cp in remote:
            cp.wait()
        for cp in local:
            cp.wait()

    return pl.pallas_call(
        body, name=name, in_specs=[any_spec] * n, out_specs=[any_spec] * n,
        out_shape=_exchange_shapes(arrays, scatter), scratch_shapes=_exchange_sems(n),
    )(*arrays)


def _pad_rows(a, rows):
    return jnp.pad(a, ((0, rows - a.shape[0]),) + ((0, 0),) * (a.ndim - 1))


PACK_UNIT = SUBLANES * LANES


def _pack_lanes(parts, rows):
    out = []
    for a in parts:
        f = a.reshape(-1)
        out.append(jnp.pad(f, (0, (-f.shape[0]) % PACK_UNIT)).reshape(-1, LANES))
    packed = jnp.concatenate(out, axis=0)
    assert packed.shape[0] == rows, (packed.shape, rows)
    return packed


def _unpack_lanes(buf, shapes):
    out, r0 = [], 0
    for shp in shapes:
        n = math.prod(shp)
        nr = -(-n // PACK_UNIT) * SUBLANES
        out.append(buf[r0:r0 + nr].reshape(-1)[:n].reshape(shp))
        r0 += nr
    return out


def _col_shards(g):
    r, n = g.shape
    return g.reshape(r, N_DEV, n // N_DEV).transpose(1, 0, 2)


def _col_unshard(s):
    _, r, w = s.shape
    return s.transpose(1, 0, 2).reshape(r, N_DEV * w)


def _lane_rows(flat):
    n = flat.shape[1]
    return jnp.pad(flat, ((0, 0), (0, (-n) % PACK_UNIT))).reshape(N_DEV, -1, LANES)


SMALL_ROWS = 128
WS_ROWS = 32


def kernel(x, norm_mix_w, w_in, conv_qkv_w, a_log, dt_bias, gdn_norm_w, w_branch_a, w_branch_b, rel_bias, w_out, norm_ffn_w, w_up, conv_ffn_w, conv_ffn_b, w_down, norm_final_w, loss_target, m_norm_mix_w, m_w_in, m_conv_qkv_w, m_a_log, m_dt_bias, m_gdn_norm_w, m_w_branch_a, m_w_branch_b, m_rel_bias, m_w_out, m_norm_ffn_w, m_w_up, m_conv_ffn_w, m_conv_ffn_b, m_w_down, m_norm_final_w, v_norm_mix_w, v_w_in, v_conv_qkv_w, v_a_log, v_dt_bias, v_gdn_norm_w, v_w_branch_a, v_w_branch_b, v_rel_bias, v_w_out, v_norm_ffn_w, v_w_up, v_conv_ffn_w, v_conv_ffn_b, v_w_down, v_norm_final_w):
    big_w = (w_in, w_branch_a, w_branch_b, w_out, w_up, w_down, conv_qkv_w, conv_ffn_w)
    big_m = (m_w_in, m_w_branch_a, m_w_branch_b, m_w_out, m_w_up, m_w_down, m_conv_qkv_w, m_conv_ffn_w)
    big_v = (v_w_in, v_w_branch_a, v_w_branch_b, v_w_out, v_w_up, v_w_down, v_conv_qkv_w, v_conv_ffn_w)
    small_w = (norm_mix_w, a_log, dt_bias, gdn_norm_w, rel_bias, norm_ffn_w, conv_ffn_b, norm_final_w)
    small_m = (m_norm_mix_w, m_a_log, m_dt_bias, m_gdn_norm_w, m_rel_bias, m_norm_ffn_w, m_conv_ffn_b, m_norm_final_w)
    small_v = (v_norm_mix_w, v_a_log, v_dt_bias, v_gdn_norm_w, v_rel_bias, v_norm_ffn_w, v_conv_ffn_b, v_norm_final_w)

    xs, tgt = x[0], loss_target[0]
    ws = _pack_lanes(big_w[6:], WS_ROWS)
    h1, g_in, gs = _rmsnorm_cast(xs, norm_mix_w, "norm_mix", carry=([w_in[0].astype(BF16), ws], (False, False)))
    win = _col_unshard(g_in)
    gs = gs.reshape(N_DEV, -1)
    cqkv = gs[:, :GDN_CONV * 192].reshape(N_DEV, GDN_CONV, 192).transpose(1, 0, 2).reshape(GDN_CONV, 3 * KEY_A)
    cffn = gs[:, PACK_UNIT:PACK_UNIT + FFN_CONV * 704].reshape(N_DEV, FFN_CONV, 704).transpose(1, 0, 2).reshape(FFN_CONV, FFN_W)
    cffn = _pad_rows(cffn, SUBLANES)
    w_all = jnp.concatenate([win[:, a:b] for a, b in W_IN_ORDER] + [jnp.zeros((D_MODEL, PROJ_W - D_IN), BF16)], axis=1)
    par = _pad_rows(jnp.pad(jnp.concatenate([a_log, dt_bias], axis=0), ((0, 0), (0, LANES - GDN_HEADS))), SUBLANES)
    table = jnp.pad(rel_bias[0], ((0, 0), (0, 3 * LANES - rel_bias.shape[-1]))).reshape(ATT_HEADS, 1, 3 * LANES)

    proj, g_ba, g_bb, g_out, g_up, g_down = _mm_nn(
        h1, w_all, F32, "in_proj", 2 * MM_TM, 1152, D_MODEL, carry=([w[0].astype(BF16) for w in big_w[1:6]], (False,) * 5))
    wba, wbb, wup = _col_unshard(g_ba), _col_unshard(g_bb), _col_unshard(g_up)
    wout = g_out.reshape(D_MODEL, D_MODEL)
    wdown = g_down.reshape(D_FF, D_MODEL)
    qn, kn, va = _gdn_prep_fwd(proj, cqkv)
    oan, o_gdn, sprev, wst, ust, tst = _gdn_fwd(qn, kn, va, proj, par, gdn_norm_w)
    bias_q, bias_k = _att_bias(table)
    ob, lse, lse_t = _att_fwd(proj, bias_q)
    x2 = _merge_fwd(oan, ob, proj, xs, wba, wbb, wout)
    h2 = _rmsnorm_cast(x2, norm_ffn_w, "norm_ffn")
    up = _mm_nn(h2, wup, F32, "ffn_up", 2 * MM_TM, 1408, D_MODEL)
    dx3, dx3b, act, u_ffn, tail_sums = _ffn_tail(up, cffn, conv_ffn_b, wdown, x2, tgt, norm_final_w.reshape(1, D_MODEL))

    g_wdown = _mm_tn(act, dx3b, "dw_down", 512)
    du, cb_sums = _ffn_bwd_act(dx3b, wdown.T, u_ffn)
    dup, cw_sums = _ffn_bwd_conv(du, up, cffn)
    g_wup = _mm_tn(h2, dup, "dw_up", 1408)
    dx2, dx2b, nffn_sums, r_up, r_down = _mm_rms_bwd(dup, wup.T, x2, norm_ffn_w, dx3, "ffn_up_bwd", MM_TM, 1408, carry=(
        [_col_shards(g_wup).astype(BF16), g_wdown.reshape(N_DEV, -1, D_MODEL).astype(BF16)], (True, True)))
    dproj, d_oan, d_ob, mixb, dya, dyb = _merge_bwd(dx2b, oan, ob, proj, wba, wbb, wout.T, wba.T, wbb.T)
    g_wout = _mm_tn(mixb, dx2b, "dw_out", 512)
    g_wba = _mm_tn(oan, dya, "dw_branch_a", 512)
    g_wbb = _mm_tn(ob, dyb, "dw_branch_b", 512)
    dproj, dlt_t, slabs = _att_dq(proj, bias_q, lse, d_ob, dproj)
    dproj = _att_dkv(proj, bias_k, lse_t, dlt_t, d_ob, dproj)
    g_rel = _relbias_grad(slabs)[:, 0, :rel_bias.shape[-1]]
    dqn, dkn, dva, dproj, gdn_sums = _gdn_bwd(qn, kn, va, proj, par, gdn_norm_w, o_gdn, sprev, wst, ust, tst, d_oan, dproj)
    dproj, cq_sums = _gdn_prep_bwd(proj, cqkv, dqn, dkn, dva, dproj)
    g_wall = _mm_tn(h1, dproj, "dw_in", 1152)
    starts = np.cumsum([0] + [b - a for a, b in W_IN_ORDER])
    g_win = jnp.concatenate([g_wall[:, starts[i]:starts[i + 1]] for i in np.argsort([a for a, _ in W_IN_ORDER])], axis=1)
    g_conv = jnp.concatenate([_lane_rows(_col_shards(cq_sums[:GDN_CONV]).reshape(N_DEV, -1)),
                              _lane_rows(_col_shards(cw_sums[:FFN_CONV]).reshape(N_DEV, -1))], axis=1)
    grad_x, _, nmix_sums, r_in, r_ba, r_bb, r_out, r_conv = _mm_rms_bwd(
        dproj, w_all.T, xs, norm_mix_w, dx2, "in_proj_bwd", MM_TM, 1152, carry=(
            [_col_shards(g_win).astype(BF16), _col_shards(g_wba).astype(BF16), _col_shards(g_wbb).astype(BF16),
             g_wout.reshape(N_DEV, -1, D_MODEL).astype(BF16), g_conv], (True,) * 5))

    small_g = (nmix_sums[0:1], gdn_sums[0:1, :GDN_HEADS], gdn_sums[1:2, :GDN_HEADS], gdn_sums[2:3], g_rel,
               nffn_sums[0:1], cb_sums[0:1], tail_sums[0:1], tail_sums[1:2, 0:1])
    r_small, = _exchange([_pack_lanes(small_g, SMALL_ROWS)], (False,), "all_gather_small_grads")
    recv = (r_in, r_ba, r_bb, r_out, r_up, r_down, r_conv, r_small)

    res = {}
    for i, (nm, tr) in enumerate((("w_in", 128), ("w_branch_a", KEY_A), ("w_branch_b", WIDTH_B), ("w_out", 128),
                                  ("w_up", 128), ("w_down", 176))):
        res[nm] = [o[None] for o in _adamw(recv[i], big_w[i][0], big_m[i][0], big_v[i][0], "adamw_" + nm, tr)]
    conv = _adamw(recv[6], _pack_lanes(big_w[6:], WS_ROWS), _pack_lanes(big_m[6:], WS_ROWS), _pack_lanes(big_v[6:], WS_ROWS),
                  "adamw_conv", WS_ROWS)
    conv = [_unpack_lanes(o, [w.shape for w in big_w[6:]]) for o in conv]
    res["conv_qkv_w"] = [o[0] for o in conv]
    res["conv_ffn_w"] = [o[1] for o in conv]
    small_shapes = [w.shape for w in small_w]
    zero = jnp.zeros((1,), F32)
    small = _adamw(recv[7], _pack_lanes(small_w + (zero,), SMALL_ROWS), _pack_lanes(small_m + (zero,), SMALL_ROWS),
                   _pack_lanes(small_v + (zero,), SMALL_ROWS), "adamw_replicated", SMALL_ROWS)
    small = [_unpack_lanes(o, small_shapes + [()]) for o in small]
    loss = small[0][-1]
    for j, nm in enumerate(("norm_mix_w", "a_log", "dt_bias", "gdn_norm_w", "rel_bias", "norm_ffn_w", "conv_ffn_b",
                            "norm_final_w")):
        res[nm] = [o[j] for o in small]

    names = ("norm_mix_w", "w_in", "conv_qkv_w", "a_log", "dt_bias", "gdn_norm_w", "w_branch_a", "w_branch_b", "rel_bias",
             "w_out", "norm_ffn_w", "w_up", "conv_ffn_w", "conv_ffn_b", "w_down", "norm_final_w")
    outs = [res[n][kind] for kind in range(4) for n in names]
    return (loss, grad_x[None], *outs)
```

```python
import functools
import math

import numpy as np
import jax
import jax.numpy as jnp
from jax import lax
from jax.experimental import pallas as pl
from jax.experimental.pallas import tpu as pltpu

F32, BF16 = jnp.float32, jnp.bfloat16
HIGHEST = lax.Precision.HIGHEST

N_DEV = 8
D_MODEL = 1024
CHUNK = 64
EPS = 1e-6
GDN_HEADS, GDN_DK = 4, 128
KEY_A = GDN_HEADS * GDN_DK
GDN_CONV = 4
ATT_HEADS, ATT_DH = 8, 64
WIDTH_B = ATT_HEADS * ATT_DH
ATT_BAND = 9
REL_CLIP = 128
D_FF = 2816
FFN_CONV = 3
D_IN = 5640
ADAM_LR, ADAM_B1, ADAM_B2, ADAM_EPS, ADAM_WD, ADAM_STEP = 0.001, 0.9, 0.999, 1e-08, 0.01, 10

LANES = 128
SUBLANES = 8
NEG = -1e30

PROJ_W = 5760
PB = 512
CB_GA, CB_GB = 0, 1
CB_KB, CB_VB, CB_QA, CB_KA, CB_VA, CB_QB, CB_ZA = 4, 5, 6, 7, 8, 9, 10
CB_BD = 44
DP_GATES, DP_KVB, DP_QKVA, DP_QB, DP_ZBD = (2048, 0), (1024, 2), (1536, 2), (512, 9), (640, 8)
W_IN_ORDER = ((3592, 5640), (2568, 3592), (0, 1536), (2056, 2568), (1536, 2048), (2048, 2056))

ATT_QB = 256
ATT_KW = 768
ATT_VEC = 1024


def _dot(a, b, precision=None):
    return jnp.dot(a, b, preferred_element_type=F32, precision=precision)


def _dot_nt(a, b, precision=None):
    return lax.dot_general(a, b, (((1,), (1,)), ((), ())), preferred_element_type=F32, precision=precision)


def _dot_tn(a, b):
    return lax.dot_general(a, b, (((0,), (0,)), ((), ())), preferred_element_type=F32)


def _split(a):
    hi = a.astype(BF16)
    return hi, (a - hi.astype(F32)).astype(BF16)


def _dot3s(a, b):
    return _dot(a[0], b[0]) + (_dot(a[0], b[1]) + _dot(a[1], b[0]))


def _sigmoid(x):
    return 0.5 * jnp.tanh(0.5 * x) + 0.5


def _softplus(x):
    return jnp.maximum(x, 0.0) + jnp.log(1.0 + jnp.exp(-jnp.abs(x)))


def _cparams(*sem):
    return pltpu.CompilerParams(dimension_semantics=tuple(sem))


def _dp_spec(tm, region, index=lambda i: i):
    width, cb = region
    return pl.BlockSpec((tm, width), lambda i: (index(i), cb))


def _rmsnorm_cast(x, w, name, tm=512, carry=((), ())):
    t, d = x.shape
    nt = t // tm
    arrays, scatter = carry
    nx = len(arrays)

    def body(*refs):
        x_ref, w_ref = refs[:2]
        o_ref = refs[2 + nx]
        i = pl.program_id(0)
        if nx:
            local, remote = _exchange_copies(refs[2:2 + nx], refs[3 + nx:3 + 2 * nx], scatter, *refs[3 + 2 * nx:])

            @pl.when(i == 0)
            def _():
                for cp in local + remote:
                    cp.start()

        xv = x_ref[...]
        r = lax.rsqrt(jnp.mean(xv * xv, axis=-1, keepdims=True) + EPS)
        o_ref[...] = (xv * r * w_ref[...]).astype(BF16)

        if nx:
            @pl.when(i == nt - 1)
            def _():
                for cp in remote + local:
                    cp.wait()

    any_spec = pl.BlockSpec(memory_space=pl.ANY)
    out = pl.pallas_call(
        body, name=name, grid=(nt,),
        in_specs=[pl.BlockSpec((tm, d), lambda i: (i, 0)), pl.BlockSpec((1, d), lambda i: (0, 0))] + [any_spec] * nx,
        out_specs=[pl.BlockSpec((tm, d), lambda i: (i, 0))] + [any_spec] * nx,
        out_shape=[jax.ShapeDtypeStruct((t, d), BF16)] + _exchange_shapes(arrays, scatter),
        scratch_shapes=_exchange_sems(nx) if nx else [],
        compiler_params=_cparams("arbitrary" if nx else "parallel"),
    )(x, w, *arrays)
    return out if nx else out[0]


def _mm_nn(a, b, out_dtype, name, tm, tn, tk, carry=((), ())):
    m, k = a.shape
    _, n = b.shape
    tm = min(tm, m)
    nk = k // tk
    assert m % tm == 0 and n % tn == 0 and k % tk == 0
    arrays, scatter = carry
    nx = len(arrays)
    gm, gn = m // tm, n // tn

    def body(*refs):
        a_ref, b_ref = refs[:2]
        srcs = refs[2:2 + nx]
        o_ref = refs[2 + nx]
        dsts = refs[3 + nx:3 + 2 * nx]
        rest = refs[3 + 2 * nx:]
        i, j, kk = pl.program_id(0), pl.program_id(1), pl.program_id(2)
        if nx:
            local, remote = _exchange_copies(srcs, dsts, scatter, *rest[-3:])

            @pl.when((i == 0) & (j == 0) & (kk == 0))
            def _():
                for cp in local + remote:
                    cp.start()

        if nk == 1:
            o_ref[...] = _dot(a_ref[...], b_ref[...]).astype(out_dtype)
        else:
            acc_ref = rest[0]

            @pl.when(kk == 0)
            def _():
                acc_ref[...] = jnp.zeros_like(acc_ref)

            acc_ref[...] += _dot(a_ref[...], b_ref[...])

            @pl.when(kk == nk - 1)
            def _():
                o_ref[...] = acc_ref[...].astype(out_dtype)

        if nx:
            @pl.when((i == gm - 1) & (j == gn - 1) & (kk == nk - 1))
            def _():
                for cp in remote + local:
                    cp.wait()

    any_spec = pl.BlockSpec(memory_space=pl.ANY)
    scratch = ([pltpu.VMEM((tm, tn), F32)] if nk > 1 else []) + (_exchange_sems(nx) if nx else [])
    out = pl.pallas_call(
        body, name=name, grid=(gm, gn, nk),
        in_specs=[pl.BlockSpec((tm, tk), lambda i, j, kk: (i, kk)),
                  pl.BlockSpec((tk, tn), lambda i, j, kk: (kk, j))] + [any_spec] * nx,
        out_specs=[pl.BlockSpec((tm, tn), lambda i, j, kk: (i, j))] + [any_spec] * nx,
        out_shape=[jax.ShapeDtypeStruct((m, n), out_dtype)] + _exchange_shapes(arrays, scatter),
        scratch_shapes=scratch,
        compiler_params=_cparams(*(("arbitrary",) * 3 if nx else ("parallel", "parallel", "arbitrary"))),
    )(a, b, *arrays)
    return out if nx else out[0]


def _mm_rms_bwd(a, b, x, w, dres, name, tm, tk, carry):
    m, k = a.shape
    _, n = b.shape
    nk = k // tk
    gm = m // tm
    assert m % tm == 0 and k % tk == 0
    arrays, scatter = carry
    nx = len(arrays)

    def body(*refs):
        a_ref, b_ref, x_ref, w_ref, dres_ref = refs[:5]
        srcs = refs[5:5 + nx]
        dx_ref, dxb_ref, dw_ref = refs[5 + nx:8 + nx]
        dsts = refs[8 + nx:8 + 2 * nx]
        acc_ref = refs[8 + 2 * nx]
        i, kk = pl.program_id(0), pl.program_id(1)
        local, remote = _exchange_copies(srcs, dsts, scatter, *refs[9 + 2 * nx:])

        @pl.when((i == 0) & (kk == 0))
        def _():
            for cp in local + remote:
                cp.start()
            dw_ref[...] = jnp.zeros_like(dw_ref)

        @pl.when(kk == 0)
        def _():
            acc_ref[...] = jnp.zeros_like(acc_ref)

        acc_ref[...] += _dot(a_ref[...], b_ref[...])

        @pl.when(kk == nk - 1)
        def _():
            dhv = acc_ref[...]
            xv = x_ref[...]
            r = lax.rsqrt(jnp.mean(xv * xv, axis=-1, keepdims=True) + EPS)
            xh = xv * r
            dw_ref[0:1, :] += jnp.sum(dhv * xh, axis=0, keepdims=True)
            dxh = dhv * w_ref[...]
            dx = dres_ref[...] + r * (dxh - xh * jnp.mean(dxh * xh, axis=-1, keepdims=True))
            dx_ref[...] = dx
            dxb_ref[...] = dx.astype(BF16)

        @pl.when((i == gm - 1) & (kk == nk - 1))
        def _():
            for cp in remote + local:
                cp.wait()

    any_spec = pl.BlockSpec(memory_space=pl.ANY)
    row = pl.BlockSpec((tm, n), lambda i, kk: (i, 0))
    return pl.pallas_call(
        body, name=name, grid=(gm, nk),
        in_specs=[pl.BlockSpec((tm, tk), lambda i, kk: (i, kk)), pl.BlockSpec((tk, n), lambda i, kk: (kk, 0)),
                  row, pl.BlockSpec((1, n), lambda i, kk: (0, 0)), row] + [any_spec] * nx,
        out_specs=[row, row, pl.BlockSpec((SUBLANES, n), lambda i, kk: (0, 0))] + [any_spec] * nx,
        out_shape=[jax.ShapeDtypeStruct((m, n), F32), jax.ShapeDtypeStruct((m, n), BF16),
                   jax.ShapeDtypeStruct((SUBLANES, n), F32)] + _exchange_shapes(arrays, scatter),
        scratch_shapes=[pltpu.VMEM((tm, n), F32)] + _exchange_sems(nx),
        compiler_params=_cparams("arbitrary", "arbitrary"),
    )(a, b, x, w, dres, *arrays)


MM_TM = 1024


def _mm_tn(a, b, name, tn, tk=2 * MM_TM):
    t, m = a.shape
    _, n = b.shape
    tk = min(tk, t)
    assert t % tk == 0 and n % tn == 0

    def body(a_ref, b_ref, o_ref):
        @pl.when(pl.program_id(1) == 0)
        def _():
            o_ref[...] = jnp.zeros_like(o_ref)

        o_ref[...] += _dot_tn(a_ref[...], b_ref[...])

    return pl.pallas_call(
        body, name=name, grid=(n // tn, t // tk),
        in_specs=[pl.BlockSpec((tk, m), lambda j, s: (s, 0)),
                  pl.BlockSpec((tk, tn), lambda j, s: (s, j))],
        out_specs=pl.BlockSpec((m, tn), lambda j, s: (0, j)),
        out_shape=jax.ShapeDtypeStruct((m, n), F32),
        compiler_params=_cparams("parallel", "arbitrary"),
    )(a, b)


def _rel_index(dist):
    return np.clip(dist, -REL_CLIP, REL_CLIP) + REL_CLIP


def _bias_onehots():
    tw = 3 * LANES
    m = np.arange(ATT_VEC)
    dq = np.where(m <= ATT_KW, 512 - m, 512 - (m - ATT_VEC))
    dk = np.where(m < ATT_KW, m, m - ATT_VEC)
    ohq = np.zeros((tw, ATT_VEC), np.float32)
    ohk = np.zeros((tw, ATT_VEC), np.float32)
    ohq[_rel_index(dq), m] = 1.0
    ohk[_rel_index(dk), m] = 1.0
    return ohq, ohk


def _att_bias(table_pad):
    ohq, ohk = _bias_onehots()
    nslab = ATT_QB // SUBLANES

    def body(t_ref, ohq_ref, ohk_ref, bq_ref, bk_ref):
        tv = jnp.broadcast_to(t_ref[...], (SUBLANES, 3 * LANES))
        row = lax.broadcasted_iota(jnp.int32, (ATT_QB, ATT_KW), 0) // CHUNK
        col = lax.broadcasted_iota(jnp.int32, (ATT_QB, ATT_KW), 1) // CHUNK
        band = (col >= row) & (col <= row + ATT_BAND - 1)
        for oh_ref, out_ref in ((ohq_ref, bq_ref), (ohk_ref, bk_ref)):
            vec = _dot(tv, oh_ref[...], HIGHEST)[0:1, :]
            slab = jnp.concatenate([vec if b == 0 else pltpu.roll(vec, b, 1) for b in range(SUBLANES)], axis=0)
            rows = [slab if a == 0 else pltpu.roll(slab, SUBLANES * a, 1) for a in range(nslab)]
            full = jnp.concatenate(rows, axis=0)[:, :ATT_KW]
            out_ref[...] = jnp.where(band, full, NEG)

    h = table_pad.shape[0]
    oh_spec = pl.BlockSpec((3 * LANES, ATT_VEC), lambda i: (0, 0))
    out_spec = pl.BlockSpec((None, ATT_QB, ATT_KW), lambda i: (i, 0, 0))
    return pl.pallas_call(
        body, name="att_bias", grid=(h,),
        in_specs=[pl.BlockSpec((None, 1, 3 * LANES), lambda i: (i, 0, 0)), oh_spec, oh_spec],
        out_specs=[out_spec, out_spec],
        out_shape=[jax.ShapeDtypeStruct((h, ATT_QB, ATT_KW), F32)] * 2,
        compiler_params=_cparams("parallel"),
    )(table_pad, jnp.asarray(ohq), jnp.asarray(ohk))


def _head_masks():
    lane = lax.broadcasted_iota(jnp.int32, (1, LANES), 1)
    return [lane < ATT_DH, lane >= ATT_DH]


def _att_fwd(proj, bias_q):
    t = proj.shape[0]
    nb = t // ATT_QB
    scale = ATT_DH ** -0.5

    def body(q_ref, k0_ref, k1_ref, k2_ref, v0_ref, v1_ref, v2_ref, b_ref, o_ref, lse_ref, lset_ref):
        i = pl.program_id(0)
        q = (q_ref[...] * scale).astype(BF16)
        kk = jnp.concatenate([k0_ref[...], k1_ref[...], k2_ref[...]], axis=0).astype(BF16)
        vv = jnp.concatenate([v0_ref[...], v1_ref[...], v2_ref[...]], axis=0).astype(BF16)
        kpos = lax.broadcasted_iota(jnp.int32, (1, ATT_KW), 1) + (i - 2) * ATT_QB
        valid = kpos >= 0
        lane = lax.broadcasted_iota(jnp.int32, (1, LANES), 1)
        masks = _head_masks()
        lse_cols = jnp.zeros((ATT_QB, LANES), F32)
        for p in range(ATT_HEADS // 2):
            cs = slice(p * LANES, (p + 1) * LANES)
            qt, kt, vt = q[:, cs], kk[:, cs], vv[:, cs]
            acc = jnp.zeros((ATT_QB, LANES), F32)
            for sub in range(2):
                h = 2 * p + sub
                s = _dot_nt(jnp.where(masks[sub], qt, 0), kt) + b_ref[h]
                s = jnp.where(valid, s, NEG)
                mx = jnp.max(s, axis=-1, keepdims=True)
                e = jnp.exp(s - mx)
                l = jnp.sum(e, axis=-1, keepdims=True)
                acc = acc + _dot(e.astype(BF16), jnp.where(masks[sub], vt, 0)) * (1.0 / l)
                lse_cols = lse_cols + jnp.where(lane == h, mx + jnp.log(l), 0.0)
            o_ref[:, cs] = acc.astype(BF16)
        lse_ref[...] = lse_cols
        lset_ref[...] = lse_cols.T[0:SUBLANES, :]

    def kv_spec(off, cb):
        return pl.BlockSpec((ATT_QB, PB), lambda i: (jnp.maximum(i + off, 0), cb))

    return pl.pallas_call(
        body, name="att_fwd", grid=(nb,),
        in_specs=[pl.BlockSpec((ATT_QB, PB), lambda i: (i, CB_QB)),
                  kv_spec(-2, CB_KB), kv_spec(-1, CB_KB), kv_spec(0, CB_KB),
                  kv_spec(-2, CB_VB), kv_spec(-1, CB_VB), kv_spec(0, CB_VB),
                  pl.BlockSpec((ATT_HEADS, ATT_QB, ATT_KW), lambda i: (0, 0, 0))],
        out_specs=[pl.BlockSpec((ATT_QB, WIDTH_B), lambda i: (i, 0)),
                   pl.BlockSpec((ATT_QB, LANES), lambda i: (i, 0)),
                   pl.BlockSpec((SUBLANES, ATT_QB), lambda i: (0, i))],
        out_shape=[jax.ShapeDtypeStruct((t, WIDTH_B), BF16), jax.ShapeDtypeStruct((t, LANES), F32),
                   jax.ShapeDtypeStruct((SUBLANES, t), F32)],
        compiler_params=_cparams("parallel"),
    )(proj, proj, proj, proj, proj, proj, proj, bias_q)


def _att_dq(proj, bias_q, lse, d_ob, dproj):
    t = proj.shape[0]
    nb = t // ATT_QB
    scale = ATT_DH ** -0.5
    nslab = ATT_QB // SUBLANES

    def body(q_ref, k0_ref, k1_ref, k2_ref, v0_ref, v1_ref, v2_ref, b_ref, lse_ref, do_ref, dp_in_ref,
             dq_ref, dlt_ref, slab_ref):
        i = pl.program_id(0)

        @pl.when(i == 0)
        def _():
            slab_ref[...] = jnp.zeros_like(slab_ref)

        q = (q_ref[...] * scale).astype(BF16)
        kk = jnp.concatenate([k0_ref[...], k1_ref[...], k2_ref[...]], axis=0).astype(BF16)
        vv = jnp.concatenate([v0_ref[...], v1_ref[...], v2_ref[...]], axis=0).astype(BF16)
        do = do_ref[...].astype(BF16)
        kpos = lax.broadcasted_iota(jnp.int32, (1, ATT_KW), 1) + (i - 2) * ATT_QB
        valid = kpos >= 0
        lane = lax.broadcasted_iota(jnp.int32, (1, LANES), 1)
        masks = _head_masks()
        lse_all = lse_ref[...]
        dlt_cols = jnp.zeros((ATT_QB, LANES), F32)
        zpad = jnp.zeros((SUBLANES, ATT_VEC - ATT_KW), F32)
        for p in range(ATT_HEADS // 2):
            cs = slice(p * LANES, (p + 1) * LANES)
            qt, kt, vt, dot_ = q[:, cs], kk[:, cs], vv[:, cs], do[:, cs]
            acc = jnp.zeros((ATT_QB, LANES), F32)
            for sub in range(2):
                h = 2 * p + sub
                s = _dot_nt(jnp.where(masks[sub], qt, 0), kt) + b_ref[h]
                s = jnp.where(valid, s, NEG)
                pr = jnp.exp(s - lse_all[:, h:h + 1])
                dp = _dot_nt(jnp.where(masks[sub], dot_, 0), vt)
                dl = jnp.sum(pr * dp, axis=-1, keepdims=True)
                ds = pr * (dp - dl)
                acc = acc + _dot(ds.astype(BF16), jnp.where(masks[sub], kt, 0)) * scale
                dlt_cols = dlt_cols + jnp.where(lane == h, dl, 0.0)
                sl = jnp.zeros((SUBLANES, ATT_VEC), F32)
                for a in range(nslab):
                    piece = jnp.concatenate([ds[a * SUBLANES:(a + 1) * SUBLANES, :], zpad], axis=1)
                    sl = sl + (piece if a == 0 else pltpu.roll(piece, ATT_VEC - SUBLANES * a, 1))
                slab_ref[h] += sl
            dq_ref[:, cs] = acc.astype(BF16)
        dlt_ref[...] = dlt_cols.T[0:SUBLANES, :]

    def kv_spec(off, cb):
        return pl.BlockSpec((ATT_QB, PB), lambda i: (jnp.maximum(i + off, 0), cb))

    return pl.pallas_call(
        body, name="att_dq", grid=(nb,),
        in_specs=[pl.BlockSpec((ATT_QB, PB), lambda i: (i, CB_QB)),
                  kv_spec(-2, CB_KB), kv_spec(-1, CB_KB), kv_spec(0, CB_KB),
                  kv_spec(-2, CB_VB), kv_spec(-1, CB_VB), kv_spec(0, CB_VB),
                  pl.BlockSpec((ATT_HEADS, ATT_QB, ATT_KW), lambda i: (0, 0, 0)),
                  pl.BlockSpec((ATT_QB, LANES), lambda i: (i, 0)),
                  pl.BlockSpec((ATT_QB, WIDTH_B), lambda i: (i, 0)), pl.BlockSpec(memory_space=pl.ANY)],
        out_specs=[_dp_spec(ATT_QB, DP_QB),
                   pl.BlockSpec((SUBLANES, ATT_QB), lambda i: (0, i)),
                   pl.BlockSpec((ATT_HEADS, SUBLANES, ATT_VEC), lambda i: (0, 0, 0))],
        out_shape=[jax.ShapeDtypeStruct(dproj.shape, dproj.dtype), jax.ShapeDtypeStruct((SUBLANES, t), F32),
                   jax.ShapeDtypeStruct((ATT_HEADS, SUBLANES, ATT_VEC), F32)],
        input_output_aliases={10: 0},
        compiler_params=_cparams("arbitrary"),
    )(proj, proj, proj, proj, proj, proj, proj, bias_q, lse, d_ob, dproj)


def _att_dkv(proj, bias_k, lse_t, dlt_t, d_ob, dproj):
    t = proj.shape[0]
    nb = t // ATT_QB
    scale = ATT_DH ** -0.5

    def body(k_ref, v_ref, q0_ref, q1_ref, q2_ref, d0_ref, d1_ref, d2_ref, l0_ref, l1_ref, l2_ref,
             e0_ref, e1_ref, e2_ref, b_ref, dp_in_ref, dkv_ref):
        i = pl.program_id(0)
        k = k_ref[...].astype(BF16)
        v = v_ref[...].astype(BF16)
        qq = (jnp.concatenate([q0_ref[...], q1_ref[...], q2_ref[...]], axis=0) * scale).astype(BF16)
        do = jnp.concatenate([d0_ref[...], d1_ref[...], d2_ref[...]], axis=0).astype(BF16)
        lse = jnp.concatenate([l0_ref[...], l1_ref[...], l2_ref[...]], axis=1)
        dlt = jnp.concatenate([e0_ref[...], e1_ref[...], e2_ref[...]], axis=1)
        qpos = lax.broadcasted_iota(jnp.int32, (1, ATT_KW), 1) + i * ATT_QB
        valid = qpos < t
        masks = _head_masks()
        for p in range(ATT_HEADS // 2):
            cs = slice(p * LANES, (p + 1) * LANES)
            kt, vt, qt, dot_ = k[:, cs], v[:, cs], qq[:, cs], do[:, cs]
            acc_k = jnp.zeros((ATT_QB, LANES), F32)
            acc_v = jnp.zeros((ATT_QB, LANES), F32)
            for sub in range(2):
                h = 2 * p + sub
                st = _dot_nt(jnp.where(masks[sub], kt, 0), qt) + b_ref[h]
                st = jnp.where(valid, st, NEG)
                pt = jnp.exp(st - lse[h:h + 1, :])
                dot_m = jnp.where(masks[sub], dot_, 0)
                acc_v = acc_v + _dot(pt.astype(BF16), dot_m)
                dpt = _dot_nt(jnp.where(masks[sub], vt, 0), dot_)
                dst = pt * (dpt - dlt[h:h + 1, :])
                acc_k = acc_k + _dot(dst.astype(BF16), jnp.where(masks[sub], qt, 0))
            dkv_ref[:, cs] = acc_k.astype(BF16)
            dkv_ref[:, WIDTH_B + p * LANES:WIDTH_B + (p + 1) * LANES] = acc_v.astype(BF16)

    def q_spec(off, cb):
        return pl.BlockSpec((ATT_QB, PB), lambda i: (jnp.minimum(i + off, nb - 1), cb))

    def d_spec(off):
        return pl.BlockSpec((ATT_QB, WIDTH_B), lambda i: (jnp.minimum(i + off, nb - 1), 0))

    def r_spec(off):
        return pl.BlockSpec((SUBLANES, ATT_QB), lambda i: (0, jnp.minimum(i + off, nb - 1)))

    row = pl.BlockSpec((ATT_QB, WIDTH_B), lambda i: (i, 0))
    return pl.pallas_call(
        body, name="att_dkv", grid=(nb,),
        in_specs=[pl.BlockSpec((ATT_QB, PB), lambda i: (i, CB_KB)), pl.BlockSpec((ATT_QB, PB), lambda i: (i, CB_VB)),
                  q_spec(0, CB_QB), q_spec(1, CB_QB), q_spec(2, CB_QB),
                  d_spec(0), d_spec(1), d_spec(2), r_spec(0), r_spec(1), r_spec(2),
                  r_spec(0), r_spec(1), r_spec(2),
                  pl.BlockSpec((ATT_HEADS, ATT_QB, ATT_KW), lambda i: (0, 0, 0)), pl.BlockSpec(memory_space=pl.ANY)],
        out_specs=_dp_spec(ATT_QB, DP_KVB),
        out_shape=jax.ShapeDtypeStruct(dproj.shape, dproj.dtype),
        input_output_aliases={15: 0},
        compiler_params=_cparams("parallel"),
    )(proj, proj, proj, proj, proj, d_ob, d_ob, d_ob, lse_t, lse_t, lse_t, dlt_t, dlt_t, dlt_t, bias_k, dproj)


def _relbias_grad(slabs):
    ohq, _ = _bias_onehots()

    def body(s_ref, oh_ref, o_ref):
        sv = s_ref[...]
        vec = sv[0:1, :]
        for b in range(1, SUBLANES):
            vec = vec + pltpu.roll(sv[b:b + 1, :], ATT_VEC - b, 1)
        o_ref[...] = _dot_nt(jnp.broadcast_to(vec, (SUBLANES, ATT_VEC)), oh_ref[...], HIGHEST)[0:1, :]

    h = slabs.shape[0]
    return pl.pallas_call(
        body, name="att_dbias", grid=(h,),
        in_specs=[pl.BlockSpec((None, SUBLANES, ATT_VEC), lambda i: (i, 0, 0)),
                  pl.BlockSpec((3 * LANES, ATT_VEC), lambda i: (0, 0))],
        out_specs=pl.BlockSpec((None, 1, 3 * LANES), lambda i: (i, 0, 0)),
        out_shape=jax.ShapeDtypeStruct((h, 1, 3 * LANES), F32),
        compiler_params=_cparams("parallel"),
    )(slabs, jnp.asarray(ohq))


GDN_TM = 512
GDN_CB = 4
HALO = SUBLANES


def _conv_taps(ext, width, lead, n):
    return [(ext if k == width - 1 else pltpu.roll(ext, width - 1 - k, 0))[lead:lead + n] for k in range(width)]


def _prev_halo_spec(tm, width, cb):
    return pl.BlockSpec((HALO, width), lambda i: (jnp.maximum(i * (tm // HALO) - 1, 0), cb))


def _next_halo_spec(tm, width, cb, t):
    return pl.BlockSpec((HALO, width), lambda i: (jnp.minimum((i + 1) * (tm // HALO), t // HALO - 1), cb))


def _gdn_prep_fwd(proj, conv_w):
    t = proj.shape[0]
    tm = GDN_TM

    def body(q_ref, k_ref, v_ref, hq_ref, hk_ref, hv_ref, w_ref, qn_ref, kn_ref, vo_ref):
        first = pl.program_id(0) == 0
        for idx, (x_ref, h_ref, o_ref) in enumerate(((q_ref, hq_ref, qn_ref), (k_ref, hk_ref, kn_ref),
                                                      (v_ref, hv_ref, vo_ref))):
            halo = jnp.where(first, 0.0, h_ref[...])
            ext = jnp.concatenate([halo, x_ref[...]], axis=0)
            w = w_ref[:, idx * KEY_A:(idx + 1) * KEY_A]
            taps = _conv_taps(ext, GDN_CONV, HALO, tm)
            y = sum(w[k:k + 1, :] * taps[k] for k in range(GDN_CONV))
            a = y * _sigmoid(y)
            if idx < 2:
                for h in range(GDN_HEADS):
                    cs = slice(h * GDN_DK, (h + 1) * GDN_DK)
                    seg = a[:, cs]
                    o_ref[:, cs] = seg * lax.rsqrt(jnp.sum(seg * seg, axis=-1, keepdims=True) + EPS)
            else:
                o_ref[...] = a

    row = pl.BlockSpec((tm, KEY_A), lambda i: (i, 0))
    return pl.pallas_call(
        body, name="gdn_prep_fwd", grid=(t // tm,),
        in_specs=[pl.BlockSpec((tm, PB), lambda i: (i, CB_QA)), pl.BlockSpec((tm, PB), lambda i: (i, CB_KA)),
                  pl.BlockSpec((tm, PB), lambda i: (i, CB_VA)),
                  _prev_halo_spec(tm, PB, CB_QA), _prev_halo_spec(tm, PB, CB_KA), _prev_halo_spec(tm, PB, CB_VA),
                  pl.BlockSpec((GDN_CONV, 3 * KEY_A), lambda i: (0, 0))],
        out_specs=[row, row, row],
        out_shape=[jax.ShapeDtypeStruct((t, KEY_A), F32)] * 3,
        compiler_params=_cparams("parallel"),
    )(proj, proj, proj, proj, proj, proj, conv_w)


def _gdn_prep_bwd(proj, conv_w, dqn, dkn, dv, dproj):
    t = proj.shape[0]
    tm = GDN_TM
    nt = t // tm
    n_ext = tm + HALO

    def body(q_ref, k_ref, v_ref, pq_ref, pk_ref, pv_ref, nq_ref, nk_ref, nv_ref,
             dq_ref, dk_ref, dv_ref, ndq_ref, ndk_ref, ndv_ref, w_ref, dp_in_ref, out_ref, dw_ref):
        i = pl.program_id(0)
        first, last = i == 0, i == nt - 1

        @pl.when(first)
        def _():
            dw_ref[...] = jnp.zeros_like(dw_ref)

        groups = ((q_ref, pq_ref, nq_ref, dq_ref, ndq_ref), (k_ref, pk_ref, nk_ref, dk_ref, ndk_ref),
                  (v_ref, pv_ref, nv_ref, dv_ref, ndv_ref))
        for idx, (x_ref, p_ref, n_ref, d_ref, nd_ref) in enumerate(groups):
            cs_all = slice(idx * KEY_A, (idx + 1) * KEY_A)
            ext = jnp.concatenate([jnp.where(first, 0.0, p_ref[...]), x_ref[...], jnp.where(last, 0.0, n_ref[...])], axis=0)
            w = w_ref[:, cs_all]
            taps = _conv_taps(ext, GDN_CONV, HALO, n_ext)
            y = sum(w[k:k + 1, :] * taps[k] for k in range(GDN_CONV))
            sg = _sigmoid(y)
            a = y * sg
            dup = jnp.concatenate([d_ref[...], jnp.where(last, 0.0, nd_ref[...])], axis=0)
            if idx < 2:
                segs = []
                for h in range(GDN_HEADS):
                    cs = slice(h * GDN_DK, (h + 1) * GDN_DK)
                    seg = a[:, cs]
                    r = lax.rsqrt(jnp.sum(seg * seg, axis=-1, keepdims=True) + EPS)
                    nrm = seg * r
                    dn = dup[:, cs]
                    segs.append(r * (dn - nrm * jnp.sum(dn * nrm, axis=-1, keepdims=True)))
                da = jnp.concatenate(segs, axis=1)
            else:
                da = dup
            dy = da * sg * (1.0 + y * (1.0 - sg))
            dx = sum(w[k:k + 1, :] * (dy if k == GDN_CONV - 1 else pltpu.roll(dy, n_ext - (GDN_CONV - 1 - k), 0))[:tm]
                     for k in range(GDN_CONV))
            out_ref[:, cs_all] = dx.astype(BF16)
            for k in range(GDN_CONV):
                dw_ref[k:k + 1, cs_all] += jnp.sum(dy[:tm] * taps[k][:tm], axis=0, keepdims=True)

    row = pl.BlockSpec((tm, KEY_A), lambda i: (i, 0))
    nrow = _next_halo_spec(tm, KEY_A, 0, t)
    return pl.pallas_call(
        body, name="gdn_prep_bwd", grid=(nt,),
        in_specs=[pl.BlockSpec((tm, PB), lambda i: (i, CB_QA)), pl.BlockSpec((tm, PB), lambda i: (i, CB_KA)),
                  pl.BlockSpec((tm, PB), lambda i: (i, CB_VA)),
                  _prev_halo_spec(tm, PB, CB_QA), _prev_halo_spec(tm, PB, CB_KA), _prev_halo_spec(tm, PB, CB_VA),
                  _next_halo_spec(tm, PB, CB_QA, t), _next_halo_spec(tm, PB, CB_KA, t), _next_halo_spec(tm, PB, CB_VA, t),
                  row, row, row, nrow, nrow, nrow,
                  pl.BlockSpec((GDN_CONV, 3 * KEY_A), lambda i: (0, 0)), pl.BlockSpec(memory_space=pl.ANY)],
        out_specs=[_dp_spec(tm, DP_QKVA), pl.BlockSpec((SUBLANES, 3 * KEY_A), lambda i: (0, 0))],
        out_shape=[jax.ShapeDtypeStruct(dproj.shape, dproj.dtype), jax.ShapeDtypeStruct((SUBLANES, 3 * KEY_A), F32)],
        input_output_aliases={16: 0},
        compiler_params=_cparams("arbitrary"),
    )(proj, proj, proj, proj, proj, proj, proj, proj, proj, dqn, dkn, dv, dqn, dkn, dv, conv_w, dproj)


class _Pair(dict):
    __getattr__ = dict.__getitem__
    __setattr__ = dict.__setitem__


def _pairs_to_lanes(cols):
    lane = lax.broadcasted_iota(jnp.int32, (1, LANES), 1)
    out = jnp.zeros((cols[0].shape[0], LANES), F32)
    for p, col in enumerate(cols):
        out = out + jnp.where(lane == p, col, 0.0)
    return out


def _gdn_terms(bd, par, kn_ref, qn_ref):
    c = CHUNK
    ii = lax.broadcasted_iota(jnp.int32, (c, c), 0)
    jj = lax.broadcasted_iota(jnp.int32, (c, c), 1)
    strict, incl = ii > jj, ii >= jj
    ltri = incl.astype(F32)
    ts = []
    for cc in range(GDN_CB):
        for h in range(GDN_HEADS):
            t = _Pair(cc=cc, h=h, rows=slice(cc * c, (cc + 1) * c), cs=slice(h * GDN_DK, (h + 1) * GDN_DK),
                      strict=strict, incl=incl)
            t.beta = _sigmoid(bd[t.rows, h:h + 1])
            t.ea = jnp.exp(par[0:1, h:h + 1])
            t.sp_arg = bd[t.rows, GDN_HEADS + h:GDN_HEADS + h + 1] + par[1:2, h:h + 1]
            t.g = -t.ea * _softplus(t.sp_arg)
            t.k = kn_ref[t.rows, t.cs]
            t.q = qn_ref[t.rows, t.cs] * (GDN_DK ** -0.5)
            t.kb, t.qb = t.k.astype(BF16), t.q.astype(BF16)
            ts.append(t)
    gall = _dot(ltri, _pairs_to_lanes([t.g for t in ts]), HIGHEST)
    gall_t = gall.T
    for p, t in enumerate(ts):
        t.gb = jnp.broadcast_to(gall[:, p:p + 1], (c, GDN_DK))
    for t in ts:
        t.kk = _dot_nt(t.kb, t.kb)
        t.qk = _dot_nt(t.qb, t.kb)
    for p, t in enumerate(ts):
        diff = t.gb[:, :c] - gall_t[p:p + 1, :]
        t.dec_s = jnp.exp(jnp.where(strict, diff, NEG))
        t.dec_i = jnp.exp(jnp.where(incl, diff, NEG))
        t.gam = jnp.exp(t.gb)
        glast = t.gb[c - 1:c, :]
        t.e_rest = jnp.exp(glast - t.gb)
        t.gl = jnp.exp(glast)
        t.p = t.qk * t.dec_i
    return ts


def _gdn_fwd(qn, kn, v, proj, par, gnw):
    t = qn.shape[0]
    c = CHUNK
    nc = t // c
    r_ = GDN_CB * c

    def body(qn_ref, kn_ref, v_ref, bd_ref, z_ref, par_ref, gnw_ref,
             oan_ref, o_ref, sp_ref, w_ref, u_ref, tm_ref, s_ref):
        @pl.when(pl.program_id(0) == 0)
        def _():
            s_ref[...] = jnp.zeros_like(s_ref)

        bd, par, gnw_v = bd_ref[...], par_ref[...], gnw_ref[...]
        eye = (lax.broadcasted_iota(jnp.int32, (c, c), 0) == lax.broadcasted_iota(jnp.int32, (c, c), 1)).astype(F32)
        ts = _gdn_terms(bd, par, kn_ref, qn_ref)
        for t in ts:
            t.vv = v_ref[t.rows, t.cs]
            t.x = -(t.beta * t.kk * t.dec_s)
            t.tinv = eye + t.x
        for t in ts:
            t.xs = _split(t.x)
        for _ in range(5):
            for t in ts:
                t.xs = _split(_dot3s(t.xs, t.xs))
            for t in ts:
                t.tinv = t.tinv + _dot3s(_split(t.tinv), t.xs)
        for t in ts:
            tsp = _split(t.tinv)
            t.wm = _dot3s(tsp, _split((t.beta * t.gam) * t.k))
            t.uv = _dot3s(tsp, _split(t.beta * t.vv))
        for t in ts:
            w_ref[t.rows, t.cs] = t.wm
            tm_ref[t.cc, t.h] = t.tinv.T
            t.wb = t.wm.astype(BF16)
            t.qgb = (t.q * t.gam).astype(BF16)
            t.kdb = (t.k * t.e_rest).astype(BF16)
            t.pb = t.p.astype(BF16)
        state = [s_ref[h] for h in range(GDN_HEADS)]
        for cc in range(GDN_CB):
            tc = [t for t in ts if t.cc == cc]
            for t in tc:
                t.sh = state[t.h]
                t.sb = t.sh.astype(BF16)
            for t in tc:
                t.ws = _dot(t.wb, t.sb)
            for t in tc:
                t.u = t.uv - t.ws
                t.ub = t.u.astype(BF16)
            for t in tc:
                state[t.h] = t.gl * t.sh + _dot_tn(t.kdb, t.ub)
            for t in tc:
                t.o = _dot(t.qgb, t.sb) + _dot(t.pb, t.ub)
                sp_ref[cc, t.h] = t.sh
                u_ref[t.rows, t.cs] = t.u
                o_ref[t.rows, t.cs] = t.o
        for h in range(GDN_HEADS):
            s_ref[h] = state[h]
        for t in ts:
            zz = z_ref[t.rows, t.cs]
            rr = lax.rsqrt(jnp.mean(t.o * t.o, axis=-1, keepdims=True) + EPS)
            oan_ref[t.rows, t.cs] = ((t.o * rr) * gnw_v * (zz * _sigmoid(zz))).astype(BF16)

    row = pl.BlockSpec((r_, KEY_A), lambda i: (i, 0))
    return pl.pallas_call(
        body, name="gdn_fwd", grid=(nc // GDN_CB,),
        in_specs=[row, row, row, pl.BlockSpec((r_, LANES), lambda i: (i, CB_BD)),
                  pl.BlockSpec((r_, PB), lambda i: (i, CB_ZA)),
                  pl.BlockSpec((SUBLANES, LANES), lambda i: (0, 0)), pl.BlockSpec((1, GDN_DK), lambda i: (0, 0))],
        out_specs=[row, row, pl.BlockSpec((GDN_CB, GDN_HEADS, GDN_DK, GDN_DK), lambda i: (i, 0, 0, 0)),
                   row, row, pl.BlockSpec((GDN_CB, GDN_HEADS, c, c), lambda i: (i, 0, 0, 0))],
        out_shape=[jax.ShapeDtypeStruct((t, KEY_A), BF16), jax.ShapeDtypeStruct((t, KEY_A), F32),
                   jax.ShapeDtypeStruct((nc, GDN_HEADS, GDN_DK, GDN_DK), F32),
                   jax.ShapeDtypeStruct((t, KEY_A), F32), jax.ShapeDtypeStruct((t, KEY_A), F32),
                   jax.ShapeDtypeStruct((nc, GDN_HEADS, c, c), F32)],
        scratch_shapes=[pltpu.VMEM((GDN_HEADS, GDN_DK, GDN_DK), F32)],
        compiler_params=_cparams("arbitrary"),
    )(qn, kn, v, proj, proj, par, gnw)


def _gdn_bwd(qn, kn, v, proj, par, gnw, o, sprev, wst, ust, tst, d_oan, dproj):
    t = qn.shape[0]
    c = CHUNK
    nc = t // c
    nb = nc // GDN_CB
    r_ = GDN_CB * c

    def body(qn_ref, kn_ref, v_ref, bd_ref, z_ref, par_ref, gnw_ref, o_ref, sp_ref, w_ref, u_ref, tm_ref, do_ref,
             dp_in_ref, dqn_ref, dkn_ref, dv_ref, dzb_ref, acc_ref, ds_ref):
        @pl.when(pl.program_id(0) == 0)
        def _():
            ds_ref[...] = jnp.zeros_like(ds_ref)
            acc_ref[...] = jnp.zeros_like(acc_ref)

        bd, par, gnw_v = bd_ref[...], par_ref[...], gnw_ref[...]
        lane = lax.broadcasted_iota(jnp.int32, (1, LANES), 1)
        rix = lax.broadcasted_iota(jnp.int32, (c, 1), 0)
        ii = lax.broadcasted_iota(jnp.int32, (c, c), 0)
        jj = lax.broadcasted_iota(jnp.int32, (c, c), 1)
        upper = (jj >= ii).astype(F32)
        acc_a = jnp.zeros((1, LANES), F32)
        acc_d = jnp.zeros((1, LANES), F32)
        acc_g = jnp.zeros((1, LANES), F32)
        ts = _gdn_terms(bd, par, kn_ref, qn_ref)
        for t in ts:
            t.vv = v_ref[t.rows, t.cs]
            t.sh = sp_ref[t.cc, t.h]
            t.sb = t.sh.astype(BF16)
            t.wm, t.u, t.tinv_t = w_ref[t.rows, t.cs], u_ref[t.rows, t.cs], tm_ref[t.cc, t.h]
            t.wb, t.ub = t.wm.astype(BF16), t.u.astype(BF16)
            ov, zz, dout = o_ref[t.rows, t.cs], z_ref[t.rows, t.cs], do_ref[t.rows, t.cs]
            sg = _sigmoid(zz)
            sil = zz * sg
            rr = lax.rsqrt(jnp.mean(ov * ov, axis=-1, keepdims=True) + EPS)
            on = ov * rr
            dzb_ref[t.rows, t.cs] = (dout * on * gnw_v * (sg * (1.0 + zz * (1.0 - sg)))).astype(BF16)
            acc_g = acc_g + jnp.sum(dout * on * sil, axis=0, keepdims=True)
            don = dout * gnw_v * sil
            t.dob = (rr * (don - on * jnp.mean(don * on, axis=-1, keepdims=True))).astype(BF16)
            t.qg = t.q * t.gam
            t.kd = t.k * t.e_rest
            t.qgb, t.kdb = t.qg.astype(BF16), t.kd.astype(BF16)
            t.ptb = t.p.T.astype(BF16)
        for t in ts:
            t.du0 = _dot(t.ptb, t.dob)
            t.ds0 = _dot_tn(t.qgb, t.dob)
            t.dqg = _dot_nt(t.dob, t.sb)
            t.dp = _dot_nt(t.dob, t.ub)
            t.uv = t.u + _dot(t.wb, t.sb)
        dstate = [ds_ref[h] for h in range(GDN_HEADS)]
        for cc in reversed(range(GDN_CB)):
            tc = [t for t in ts if t.cc == cc]
            for t in tc:
                t.dsn = dstate[t.h]
                t.dsnb = t.dsn.astype(BF16)
            for t in tc:
                t.du = t.du0 + _dot(t.kdb, t.dsnb)
            for t in tc:
                t.dub = t.du.astype(BF16)
            for t in tc:
                dstate[t.h] = t.gl * t.dsn + t.ds0 - _dot_tn(t.wb, t.dub)
            for t in tc:
                t.dkd = _dot_nt(t.ub, t.dsnb)
                t.dgl = jnp.sum(jnp.sum(t.dsn * t.sh, axis=1, keepdims=True), axis=0, keepdims=True)
                t.dwm = -_dot_nt(t.dub, t.sb)
        for h in range(GDN_HEADS):
            ds_ref[h] = dstate[h]
        for t in ts:
            tsp = _split(t.tinv_t)
            t.dbk = _dot3s(tsp, _split(t.dwm))
            t.dbv = _dot3s(tsp, _split(t.du))
        for t in ts:
            d_a = -(_dot_nt(t.dbk.astype(BF16), t.wb) + _dot_nt(t.dbv.astype(BF16), t.uv.astype(BF16)))
            t.d_a = jnp.where(t.strict, d_a, 0.0)
        for t in ts:
            t.dkk = t.d_a * t.beta * t.dec_s
            t.dqk = t.dp * t.dec_i
            t.dqkb = t.dqk.astype(BF16)
        for t in ts:
            t.dq = _dot(t.dqkb, t.kb) + t.dqg * t.gam
            t.dk = (t.dbk * (t.beta * t.gam) + _dot_tn(t.dqkb, t.qb) + _dot((t.dkk + t.dkk.T).astype(BF16), t.kb)
                    + t.dkd * t.e_rest)
        for t in ts:
            dbeta = (jnp.sum(t.d_a * t.kk * t.dec_s, axis=-1, keepdims=True)
                     + jnp.sum(t.dbk * t.k * t.gam, axis=-1, keepdims=True) + jnp.sum(t.dbv * t.vv, axis=-1, keepdims=True))
            t.dbl = dbeta * t.beta * (1.0 - t.beta)
            dv_ref[t.rows, t.cs] = t.dbv * t.beta
            bk = (t.beta * t.gam) * t.k
            zc = jnp.sum(t.dkd * t.kd, axis=-1, keepdims=True)
            xs = t.dkk * t.kk + t.dp * t.p
            dgc = (jnp.sum(xs, axis=-1, keepdims=True) - jnp.sum(xs.T, axis=-1, keepdims=True)
                   + jnp.sum(t.dbk * bk, axis=-1, keepdims=True) + jnp.sum(t.dqg * t.qg, axis=-1, keepdims=True) - zc)
            dglast = jnp.sum(zc, axis=0, keepdims=True) + t.dgl * t.gl[:, 0:1]
            t.dgc = dgc + jnp.where(rix == c - 1, dglast, 0.0)
        dgall = _dot(upper, _pairs_to_lanes([t.dgc for t in ts]), HIGHEST)
        for p, t in enumerate(ts):
            t.dg = dgall[:, p:p + 1]
        dbd_tiles = [jnp.zeros((c, LANES), F32) for _ in range(GDN_CB)]
        for t in ts:
            ddl = t.dg * (-t.ea) * _sigmoid(t.sp_arg)
            acc_a = acc_a + jnp.where(lane == t.h, jnp.sum(t.dg * t.g, axis=0, keepdims=True), 0.0)
            acc_d = acc_d + jnp.where(lane == t.h, jnp.sum(ddl, axis=0, keepdims=True), 0.0)
            dbd_tiles[t.cc] = (dbd_tiles[t.cc] + jnp.where(lane == t.h, t.dbl, 0.0)
                               + jnp.where(lane == GDN_HEADS + t.h, ddl, 0.0))
            dqn_ref[t.rows, t.cs] = t.dq * (GDN_DK ** -0.5)
            dkn_ref[t.rows, t.cs] = t.dk
        for cc in range(GDN_CB):
            dzb_ref[cc * c:(cc + 1) * c, KEY_A:KEY_A + LANES] = dbd_tiles[cc].astype(BF16)
        acc_ref[0:1, :] += acc_a
        acc_ref[1:2, :] += acc_d
        acc_ref[2:3, :] += acc_g

    def rev(i):
        return nb - 1 - i

    row = pl.BlockSpec((r_, KEY_A), lambda i: (rev(i), 0))
    st = pl.BlockSpec((GDN_CB, GDN_HEADS, GDN_DK, GDN_DK), lambda i: (rev(i), 0, 0, 0))
    tt_spec = pl.BlockSpec((GDN_CB, GDN_HEADS, c, c), lambda i: (rev(i), 0, 0, 0))
    return pl.pallas_call(
        body, name="gdn_bwd", grid=(nb,),
        in_specs=[row, row, row, pl.BlockSpec((r_, LANES), lambda i: (rev(i), CB_BD)),
                  pl.BlockSpec((r_, PB), lambda i: (rev(i), CB_ZA)),
                  pl.BlockSpec((SUBLANES, LANES), lambda i: (0, 0)), pl.BlockSpec((1, GDN_DK), lambda i: (0, 0)),
                  row, st, row, row, tt_spec, row, pl.BlockSpec(memory_space=pl.ANY)],
        out_specs=[row, row, row, _dp_spec(r_, DP_ZBD, rev), pl.BlockSpec((SUBLANES, LANES), lambda i: (0, 0))],
        out_shape=[jax.ShapeDtypeStruct((t, KEY_A), F32)] * 3 + [jax.ShapeDtypeStruct(dproj.shape, dproj.dtype),
                                                                jax.ShapeDtypeStruct((SUBLANES, LANES), F32)],
        input_output_aliases={13: 3},
        scratch_shapes=[pltpu.VMEM((GDN_HEADS, GDN_DK, GDN_DK), F32)],
        compiler_params=_cparams("arbitrary"),
    )(qn, kn, v, proj, proj, par, gnw, o, sprev, wst, ust, tst, d_oan, dproj)


def _merge_fwd(oan, ob, proj, x, wba, wbb, wout, tm=512):
    t = x.shape[0]

    def body(oa_ref, ob_ref, ga_ref, gb_ref, x_ref, wba_ref, wbb_ref, wout_ref, x2_ref):
        ya = _dot(oa_ref[...], wba_ref[...])
        yb = _dot(ob_ref[...], wbb_ref[...])
        mix = _sigmoid(ga_ref[...]) * ya + _sigmoid(gb_ref[...]) * yb
        x2_ref[...] = x_ref[...] + _dot(mix.astype(BF16), wout_ref[...])

    half = pl.BlockSpec((tm, KEY_A), lambda i: (i, 0))
    row = pl.BlockSpec((tm, D_MODEL), lambda i: (i, 0))
    wsmall = pl.BlockSpec((KEY_A, D_MODEL), lambda i: (0, 0))
    return pl.pallas_call(
        body, name="merge_fwd", grid=(t // tm,),
        in_specs=[half, half, pl.BlockSpec((tm, D_MODEL), lambda i: (i, CB_GA)),
                  pl.BlockSpec((tm, D_MODEL), lambda i: (i, CB_GB)), row, wsmall, wsmall,
                  pl.BlockSpec((D_MODEL, D_MODEL), lambda i: (0, 0))],
        out_specs=row,
        out_shape=jax.ShapeDtypeStruct((t, D_MODEL), F32),
        compiler_params=_cparams("parallel"),
    )(oan, ob, proj, proj, x, wba, wbb, wout)


def _merge_bwd(dx2b, oan, ob, proj, wba, wbb, wout_t, wba_t, wbb_t, tm=512):
    t = dx2b.shape[0]

    def body(dx_ref, oa_ref, ob_ref, ga_ref, gb_ref, wba_ref, wbb_ref, woutt_ref, wbat_ref, wbbt_ref,
             dg_ref, doa_ref, dob_ref, gout_ref, gba_ref, gbb_ref):
        @pl.when(pl.program_id(0) == 0)
        def _():
            gout_ref[...] = jnp.zeros_like(gout_ref)
            gba_ref[...] = jnp.zeros_like(gba_ref)
            gbb_ref[...] = jnp.zeros_like(gbb_ref)

        dx, oa, ob = dx_ref[...], oa_ref[...], ob_ref[...]
        dmix = _dot(dx, woutt_ref[...])
        ya = _dot(oa, wba_ref[...])
        yb = _dot(ob, wbb_ref[...])
        sa, sb = _sigmoid(ga_ref[...]), _sigmoid(gb_ref[...])
        gout_ref[...] += _dot_tn((sa * ya + sb * yb).astype(BF16), dx)
        dg_ref[:, :D_MODEL] = (dmix * ya * sa * (1.0 - sa)).astype(BF16)
        dg_ref[:, D_MODEL:] = (dmix * yb * sb * (1.0 - sb)).astype(BF16)
        dya = (dmix * sa).astype(BF16)
        dyb = (dmix * sb).astype(BF16)
        gba_ref[...] += _dot_tn(oa, dya)
        gbb_ref[...] += _dot_tn(ob, dyb)
        doa_ref[...] = _dot(dya, wbat_ref[...])
        dob_ref[...] = _dot(dyb, wbbt_ref[...])

    half = pl.BlockSpec((tm, KEY_A), lambda i: (i, 0))
    row = pl.BlockSpec((tm, D_MODEL), lambda i: (i, 0))
    wsmall = pl.BlockSpec((KEY_A, D_MODEL), lambda i: (0, 0))
    wsmall_t = pl.BlockSpec((D_MODEL, KEY_A), lambda i: (0, 0))
    wfull = pl.BlockSpec((D_MODEL, D_MODEL), lambda i: (0, 0))
    return pl.pallas_call(
        body, name="merge_bwd", grid=(t // tm,),
        in_specs=[row, half, half, pl.BlockSpec((tm, D_MODEL), lambda i: (i, CB_GA)),
                  pl.BlockSpec((tm, D_MODEL), lambda i: (i, CB_GB)), wsmall, wsmall, wfull, wsmall_t, wsmall_t],
        out_specs=[_dp_spec(tm, DP_GATES), half, half, wfull, wsmall, wsmall],
        out_shape=[jax.ShapeDtypeStruct((t, PROJ_W), BF16), jax.ShapeDtypeStruct((t, KEY_A), F32),
                   jax.ShapeDtypeStruct((t, KEY_A), F32), jax.ShapeDtypeStruct((D_MODEL, D_MODEL), F32),
                   jax.ShapeDtypeStruct((KEY_A, D_MODEL), F32), jax.ShapeDtypeStruct((WIDTH_B, D_MODEL), F32)],
        compiler_params=_cparams("arbitrary"),
    )(dx2b, oan, ob, proj, proj, wba, wbb, wout_t, wba_t, wbb_t)


FFN_TM = 128
FFN_W = 2 * D_FF


def _ffn_conv(up_ref, halo_ref, cw_ref, cb_ref, first):
    ext = jnp.concatenate([jnp.where(first, 0.0, halo_ref[...]), up_ref[...]], axis=0)
    taps = _conv_taps(ext, FFN_CONV, HALO, FFN_TM)
    cw = cw_ref[...]
    u = sum(cw[k:k + 1, :] * taps[k] for k in range(FFN_CONV)) + cb_ref[...]
    return u, taps


def _ffn_tail(up, cw, cb, wdown, x2, tgt, w3):
    t = x2.shape[0]
    tm = FFN_TM

    def body(up_ref, halo_ref, cw_ref, cb_ref, wd_ref, x2_ref, tgt_ref, w3_ref, dx_ref, dxb_ref, act_ref, acc_ref):
        first = pl.program_id(0) == 0

        @pl.when(first)
        def _():
            acc_ref[...] = jnp.zeros_like(acc_ref)

        u, _ = _ffn_conv(up_ref, halo_ref, cw_ref, cb_ref, first)
        gate, upp = u[:, :D_FF], u[:, D_FF:]
        act = (gate * _sigmoid(gate) * upp).astype(BF16)
        act_ref[...] = act
        x3 = x2_ref[...] + _dot(act, wd_ref[...])
        r = lax.rsqrt(jnp.mean(x3 * x3, axis=-1, keepdims=True) + EPS)
        xh = x3 * r
        w3v = w3_ref[...]
        err = xh * w3v - tgt_ref[...]
        loss = 0.5 * jnp.sum(jnp.mean(err * err, axis=-1, keepdims=True), axis=0, keepdims=True)
        dy = err * (1.0 / D_MODEL)
        acc_ref[0:1, :] += jnp.sum(dy * xh, axis=0, keepdims=True)
        acc_ref[1:2, :] += jnp.broadcast_to(loss, (1, D_MODEL))
        dxh = dy * w3v
        dx = r * (dxh - xh * jnp.mean(dxh * xh, axis=-1, keepdims=True))
        dx_ref[...] = dx
        dxb_ref[...] = dx.astype(BF16)

    row = pl.BlockSpec((tm, D_MODEL), lambda i: (i, 0))
    vec = pl.BlockSpec((1, D_MODEL), lambda i: (0, 0))
    return pl.pallas_call(
        body, name="ffn_tail", grid=(t // tm,),
        in_specs=[pl.BlockSpec((tm, FFN_W), lambda i: (i, 0)), _prev_halo_spec(tm, FFN_W, 0),
                  pl.BlockSpec((SUBLANES, FFN_W), lambda i: (0, 0)), pl.BlockSpec((1, FFN_W), lambda i: (0, 0)),
                  pl.BlockSpec((D_FF, D_MODEL), lambda i: (0, 0)), row, row, vec],
        out_specs=[row, row, pl.BlockSpec((tm, D_FF), lambda i: (i, 0)), pl.BlockSpec((SUBLANES, D_MODEL), lambda i: (0, 0))],
        out_shape=[jax.ShapeDtypeStruct((t, D_MODEL), F32), jax.ShapeDtypeStruct((t, D_MODEL), BF16),
                   jax.ShapeDtypeStruct((t, D_FF), BF16), jax.ShapeDtypeStruct((SUBLANES, D_MODEL), F32)],
        compiler_params=_cparams("arbitrary"),
    )(up, up, cw, cb, wdown, x2, tgt, w3)


def _ffn_bwd_act(dx3b, wdown_t, up, cw, cb):
    t = up.shape[0]
    tm = FFN_TM

    def body(dx_ref, wdt_ref, up_ref, halo_ref, cw_ref, cb_ref, du_ref, acc_ref):
        first = pl.program_id(0) == 0

        @pl.when(first)
        def _():
            acc_ref[...] = jnp.zeros_like(acc_ref)

        dact = _dot(dx_ref[...], wdt_ref[...])
        u, taps = _ffn_conv(up_ref, halo_ref, cw_ref, cb_ref, first)
        gate, upp = u[:, :D_FF], u[:, D_FF:]
        sg = _sigmoid(gate)
        du = jnp.concatenate([dact * upp * (sg * (1.0 + gate * (1.0 - sg))), dact * (gate * sg)], axis=1)
        du_ref[...] = du
        for k in range(FFN_CONV):
            acc_ref[k:k + 1, :] += jnp.sum(du * taps[k], axis=0, keepdims=True)
        acc_ref[FFN_CONV:FFN_CONV + 1, :] += jnp.sum(du, axis=0, keepdims=True)

    return pl.pallas_call(
        body, name="ffn_bwd_act", grid=(t // tm,),
        in_specs=[pl.BlockSpec((tm, D_MODEL), lambda i: (i, 0)), pl.BlockSpec((D_MODEL, D_FF), lambda i: (0, 0)),
                  pl.BlockSpec((tm, FFN_W), lambda i: (i, 0)), _prev_halo_spec(tm, FFN_W, 0),
                  pl.BlockSpec((SUBLANES, FFN_W), lambda i: (0, 0)), pl.BlockSpec((1, FFN_W), lambda i: (0, 0))],
        out_specs=[pl.BlockSpec((tm, FFN_W), lambda i: (i, 0)), pl.BlockSpec((SUBLANES, FFN_W), lambda i: (0, 0))],
        out_shape=[jax.ShapeDtypeStruct((t, FFN_W), F32), jax.ShapeDtypeStruct((SUBLANES, FFN_W), F32)],
        compiler_params=_cparams("arbitrary"),
    )(dx3b, wdown_t, up, up, cw, cb)


def _ffn_bwd_conv(du, cw):
    t = du.shape[0]
    tm = FFN_TM
    nt = t // tm
    n_ext = tm + HALO

    def body(du_ref, nxt_ref, cw_ref, o_ref):
        last = pl.program_id(0) == nt - 1
        ext = jnp.concatenate([du_ref[...], jnp.where(last, 0.0, nxt_ref[...])], axis=0)
        cw = cw_ref[...]
        acc = cw[FFN_CONV - 1:FFN_CONV, :] * ext[:tm]
        for k in range(FFN_CONV - 1):
            acc = acc + cw[k:k + 1, :] * pltpu.roll(ext, n_ext - (FFN_CONV - 1 - k), 0)[:tm]
        o_ref[...] = acc.astype(BF16)

    return pl.pallas_call(
        body, name="ffn_bwd_conv", grid=(nt,),
        in_specs=[pl.BlockSpec((tm, FFN_W), lambda i: (i, 0)), _next_halo_spec(tm, FFN_W, 0, t),
                  pl.BlockSpec((SUBLANES, FFN_W), lambda i: (0, 0))],
        out_specs=pl.BlockSpec((tm, FFN_W), lambda i: (i, 0)),
        out_shape=jax.ShapeDtypeStruct((t, FFN_W), BF16),
        compiler_params=_cparams("parallel"),
    )(du, du, cw)


def _adamw(parts, w, m, v, name, tr):
    r, cols = w.shape

    def body(p_ref, w_ref, m_ref, v_ref, g_ref, d_ref, mo_ref, vo_ref):
        g = p_ref[0].astype(F32)
        for s in range(1, N_DEV):
            g = g + p_ref[s].astype(F32)
        mm = ADAM_B1 * m_ref[...] + (1.0 - ADAM_B1) * g
        vv = ADAM_B2 * v_ref[...] + (1.0 - ADAM_B2) * (g * g)
        m_hat = mm / (1.0 - ADAM_B1 ** ADAM_STEP)
        v_hat = vv / (1.0 - ADAM_B2 ** ADAM_STEP)
        g_ref[...] = g
        d_ref[...] = -ADAM_LR * (m_hat / (jnp.sqrt(v_hat) + ADAM_EPS) + ADAM_WD * w_ref[...])
        mo_ref[...] = mm
        vo_ref[...] = vv

    assert r % tr == 0
    row = pl.BlockSpec((tr, cols), lambda i: (i, 0))
    return pl.pallas_call(
        body, name=name, grid=(r // tr,),
        in_specs=[pl.BlockSpec((N_DEV, tr, cols), lambda i: (0, i, 0)), row, row, row],
        out_specs=[row, row, row, row],
        out_shape=[jax.ShapeDtypeStruct((r, cols), F32)] * 4,
        compiler_params=_cparams("parallel"),
    )(parts, w, m, v)


def _mesh_pos():
    return lax.axis_index("x"), lax.axis_index("y"), lax.axis_index("c")


def _peer(pos, k):
    x, y, c = pos
    return (x ^ ((k >> 2) & 1), y ^ ((k >> 1) & 1), c ^ (k & 1))


def _flat_id(pos):
    return 4 * pos[0] + 2 * pos[1] + pos[2]


def _exchange_copies(srcs, dsts, scatter, send_sems, recv_sems, loc_sems):
    pos = _mesh_pos()
    me = _flat_id(pos)
    local, remote = [], []
    for j, (src, dst) in enumerate(zip(srcs, dsts)):
        local.append(pltpu.make_async_copy(src.at[me] if scatter[j] else src, dst.at[me], loc_sems.at[j]))
        for k in range(1, N_DEV):
            to = _peer(pos, k)
            remote.append(pltpu.make_async_remote_copy(
                src_ref=src.at[_flat_id(to)] if scatter[j] else src, dst_ref=dst.at[me],
                send_sem=send_sems.at[j, k - 1], recv_sem=recv_sems.at[j, k - 1],
                device_id=to, device_id_type=pl.DeviceIdType.MESH))
    return local, remote


def _exchange_shapes(arrays, scatter):
    return [jax.ShapeDtypeStruct(a.shape if s else (N_DEV,) + a.shape, a.dtype) for a, s in zip(arrays, scatter)]


def _exchange_sems(n):
    return [pltpu.SemaphoreType.DMA((n, N_DEV - 1)), pltpu.SemaphoreType.DMA((n, N_DEV - 1)), pltpu.SemaphoreType.DMA((n,))]


def _exchange(arrays, scatter, name):
    n = len(arrays)
    any_spec = pl.BlockSpec(memory_space=pl.ANY)

    def body(*refs):
        local, remote = _exchange_copies(refs[:n], refs[n:2 * n], scatter, *refs[2 * n:])
        for cp in local + remote:
            cp.start()
        for cp in remote:
            cp.wait()
        for cp in local:
            cp.wait()

    return pl.pallas_call(
        body, name=name, in_specs=[any_spec] * n, out_specs=[any_spec] * n,
        out_shape=_exchange_shapes(arrays, scatter), scratch_shapes=_exchange_sems(n),
    )(*arrays)


def _pad_rows(a, rows):
    return jnp.pad(a, ((0, rows - a.shape[0]),) + ((0, 0),) * (a.ndim - 1))


PACK_UNIT = SUBLANES * LANES


def _pack_lanes(parts, rows):
    out = []
    for a in parts:
        f = a.reshape(-1)
        out.append(jnp.pad(f, (0, (-f.shape[0]) % PACK_UNIT)).reshape(-1, LANES))
    packed = jnp.concatenate(out, axis=0)
    assert packed.shape[0] == rows, (packed.shape, rows)
    return packed


def _unpack_lanes(buf, shapes):
    out, r0 = [], 0
    for shp in shapes:
        n = math.prod(shp)
        nr = -(-n // PACK_UNIT) * SUBLANES
        out.append(buf[r0:r0 + nr].reshape(-1)[:n].reshape(shp))
        r0 += nr
    return out


def _col_shards(g):
    r, n = g.shape
    return g.reshape(r, N_DEV, n // N_DEV).transpose(1, 0, 2)


def _col_unshard(s):
    _, r, w = s.shape
    return s.transpose(1, 0, 2).reshape(r, N_DEV * w)


def _lane_rows(flat):
    n = flat.shape[1]
    return jnp.pad(flat, ((0, 0), (0, (-n) % PACK_UNIT))).reshape(N_DEV, -1, LANES)


SMALL_ROWS = 128
WS_ROWS = 32


def kernel(x, norm_mix_w, w_in, conv_qkv_w, a_log, dt_bias, gdn_norm_w, w_branch_a, w_branch_b, rel_bias, w_out, norm_ffn_w, w_up, conv_ffn_w, conv_ffn_b, w_down, norm_final_w, loss_target, m_norm_mix_w, m_w_in, m_conv_qkv_w, m_a_log, m_dt_bias, m_gdn_norm_w, m_w_branch_a, m_w_branch_b, m_rel_bias, m_w_out, m_norm_ffn_w, m_w_up, m_conv_ffn_w, m_conv_ffn_b, m_w_down, m_norm_final_w, v_norm_mix_w, v_w_in, v_conv_qkv_w, v_a_log, v_dt_bias, v_gdn_norm_w, v_w_branch_a, v_w_branch_b, v_rel_bias, v_w_out, v_norm_ffn_w, v_w_up, v_conv_ffn_w, v_conv_ffn_b, v_w_down, v_norm_final_w):
    big_w = (w_in, w_branch_a, w_branch_b, w_out, w_up, w_down, conv_qkv_w, conv_ffn_w)
    big_m = (m_w_in, m_w_branch_a, m_w_branch_b, m_w_out, m_w_up, m_w_down, m_conv_qkv_w, m_conv_ffn_w)
    big_v = (v_w_in, v_w_branch_a, v_w_branch_b, v_w_out, v_w_up, v_w_down, v_conv_qkv_w, v_conv_ffn_w)
    small_w = (norm_mix_w, a_log, dt_bias, gdn_norm_w, rel_bias, norm_ffn_w, conv_ffn_b, norm_final_w)
    small_m = (m_norm_mix_w, m_a_log, m_dt_bias, m_gdn_norm_w, m_rel_bias, m_norm_ffn_w, m_conv_ffn_b, m_norm_final_w)
    small_v = (v_norm_mix_w, v_a_log, v_dt_bias, v_gdn_norm_w, v_rel_bias, v_norm_ffn_w, v_conv_ffn_b, v_norm_final_w)

    xs, tgt = x[0], loss_target[0]
    ws = _pack_lanes(big_w[6:], WS_ROWS)
    h1, g_in, gs = _rmsnorm_cast(xs, norm_mix_w, "norm_mix", carry=([w_in[0].astype(BF16), ws], (False, False)))
    win = _col_unshard(g_in)
    gs = gs.reshape(N_DEV, -1)
    cqkv = gs[:, :GDN_CONV * 192].reshape(N_DEV, GDN_CONV, 192).transpose(1, 0, 2).reshape(GDN_CONV, 3 * KEY_A)
    cffn = gs[:, PACK_UNIT:PACK_UNIT + FFN_CONV * 704].reshape(N_DEV, FFN_CONV, 704).transpose(1, 0, 2).reshape(FFN_CONV, FFN_W)
    cffn = _pad_rows(cffn, SUBLANES)
    w_all = jnp.concatenate([win[:, a:b] for a, b in W_IN_ORDER] + [jnp.zeros((D_MODEL, PROJ_W - D_IN), BF16)], axis=1)
    par = _pad_rows(jnp.pad(jnp.concatenate([a_log, dt_bias], axis=0), ((0, 0), (0, LANES - GDN_HEADS))), SUBLANES)
    table = jnp.pad(rel_bias[0], ((0, 0), (0, 3 * LANES - rel_bias.shape[-1]))).reshape(ATT_HEADS, 1, 3 * LANES)

    proj, g_ba, g_bb, g_out, g_up, g_down = _mm_nn(
        h1, w_all, F32, "in_proj", 2 * MM_TM, 1152, D_MODEL, carry=([w[0].astype(BF16) for w in big_w[1:6]], (False,) * 5))
    wba, wbb, wup = _col_unshard(g_ba), _col_unshard(g_bb), _col_unshard(g_up)
    wout = g_out.reshape(D_MODEL, D_MODEL)
    wdown = g_down.reshape(D_FF, D_MODEL)
    qn, kn, va = _gdn_prep_fwd(proj, cqkv)
    oan, o_gdn, sprev, wst, ust, tst = _gdn_fwd(qn, kn, va, proj, par, gdn_norm_w)
    bias_q, bias_k = _att_bias(table)
    ob, lse, lse_t = _att_fwd(proj, bias_q)
    x2 = _merge_fwd(oan, ob, proj, xs, wba, wbb, wout)
    h2 = _rmsnorm_cast(x2, norm_ffn_w, "norm_ffn")
    up = _mm_nn(h2, wup, F32, "ffn_up", 2 * MM_TM, 1408, D_MODEL)
    dx3, dx3b, act, tail_sums = _ffn_tail(up, cffn, conv_ffn_b, wdown, x2, tgt, norm_final_w.reshape(1, D_MODEL))

    g_wdown = _mm_tn(act, dx3b, "dw_down", 512)
    du, ffn_sums = _ffn_bwd_act(dx3b, wdown.T, up, cffn, conv_ffn_b)
    dup = _ffn_bwd_conv(du, cffn)
    g_wup = _mm_tn(h2, dup, "dw_up", 1408)
    dx2, dx2b, nffn_sums, r_up, r_down = _mm_rms_bwd(dup, wup.T, x2, norm_ffn_w, dx3, "ffn_up_bwd", MM_TM, 1408, carry=(
        [_col_shards(g_wup).astype(BF16), g_wdown.reshape(N_DEV, -1, D_MODEL).astype(BF16)], (True, True)))
    dproj, d_oan, d_ob, g_wout, g_wba, g_wbb = _merge_bwd(dx2b, oan, ob, proj, wba, wbb, wout.T, wba.T, wbb.T)
    dproj, dlt_t, slabs = _att_dq(proj, bias_q, lse, d_ob, dproj)
    dproj = _att_dkv(proj, bias_k, lse_t, dlt_t, d_ob, dproj)
    g_rel = _relbias_grad(slabs)[:, 0, :rel_bias.shape[-1]]
    dqn, dkn, dva, dproj, gdn_sums = _gdn_bwd(qn, kn, va, proj, par, gdn_norm_w, o_gdn, sprev, wst, ust, tst, d_oan, dproj)
    dproj, cq_sums = _gdn_prep_bwd(proj, cqkv, dqn, dkn, dva, dproj)
    g_wall = _mm_tn(h1, dproj, "dw_in", 1152)
    starts = np.cumsum([0] + [b - a for a, b in W_IN_ORDER])
    g_win = jnp.concatenate([g_wall[:, starts[i]:starts[i + 1]] for i in np.argsort([a for a, _ in W_IN_ORDER])], axis=1)
    g_conv = jnp.concatenate([_lane_rows(_col_shards(cq_sums[:GDN_CONV]).reshape(N_DEV, -1)),
                              _lane_rows(_col_shards(ffn_sums[:FFN_CONV]).reshape(N_DEV, -1))], axis=1)
    grad_x, _, nmix_sums, r_in, r_ba, r_bb, r_out, r_conv = _mm_rms_bwd(
        dproj, w_all.T, xs, norm_mix_w, dx2, "in_proj_bwd", MM_TM, 1152, carry=(
            [_col_shards(g_win).astype(BF16), _col_shards(g_wba).astype(BF16), _col_shards(g_wbb).astype(BF16),
             g_wout.reshape(N_DEV, -1, D_MODEL).astype(BF16), g_conv], (True,) * 5))

    small_g = (nmix_sums[0:1], gdn_sums[0:1, :GDN_HEADS], gdn_sums[1:2, :GDN_HEADS], gdn_sums[2:3], g_rel,
               nffn_sums[0:1], ffn_sums[FFN_CONV:FFN_CONV + 1], tail_sums[0:1], tail_sums[1:2, 0:1])
    r_small, = _exchange([_pack_lanes(small_g, SMALL_ROWS)], (False,), "all_gather_small_grads")
    recv = (r_in, r_ba, r_bb, r_out, r_up, r_down, r_conv, r_small)

    res = {}
    for i, (nm, tr) in enumerate((("w_in", 128), ("w_branch_a", KEY_A), ("w_branch_b", WIDTH_B), ("w_out", 128),
                                  ("w_up", 128), ("w_down", 176))):
        res[nm] = [o[None] for o in _adamw(recv[i], big_w[i][0], big_m[i][0], big_v[i][0], "adamw_" + nm, tr)]
    conv = _adamw(recv[6], _pack_lanes(big_w[6:], WS_ROWS), _pack_lanes(big_m[6:], WS_ROWS), _pack_lanes(big_v[6:], WS_ROWS),
                  "adamw_conv", WS_ROWS)
    conv = [_unpack_lanes(o, [w.shape for w in big_w[6:]]) for o in conv]
    res["conv_qkv_w"] = [o[0] for o in conv]
    res["conv_ffn_w"] = [o[1] for o in conv]
    small_shapes = [w.shape for w in small_w]
    zero = jnp.zeros((1,), F32)
    small = _adamw(recv[7], _pack_lanes(small_w + (zero,), SMALL_ROWS), _pack_lanes(small_m + (zero,), SMALL_ROWS),
                   _pack_lanes(small_v + (zero,), SMALL_ROWS), "adamw_replicated", SMALL_ROWS)
    small = [_unpack_lanes(o, small_shapes + [()]) for o in small]
    loss = small[0][-1]
    for j, nm in enumerate(("norm_mix_w", "a_log", "dt_bias", "gdn_norm_w", "rel_bias", "norm_ffn_w", "conv_ffn_b",
                            "norm_final_w")):
        res[nm] = [o[j] for o in small]

    names = ("norm_mix_w", "w_in", "conv_qkv_w", "a_log", "dt_bias", "gdn_norm_w", "w_branch_a", "w_branch_b", "rel_bias",
             "w_out", "norm_ffn_w", "w_up", "conv_ffn_w", "conv_ffn_b", "w_down", "norm_final_w")
    outs = [res[n][kind] for kind in range(4) for n in names]
    return (loss, grad_x[None], *outs)
```

```python
import functools
import math

import numpy as np
import jax
import jax.numpy as jnp
from jax import lax
from jax.experimental import pallas as pl
from jax.experimental.pallas import tpu as pltpu

F32, BF16 = jnp.float32, jnp.bfloat16
HIGHEST = lax.Precision.HIGHEST

N_DEV = 8
D_MODEL = 1024
CHUNK = 64
EPS = 1e-6
GDN_HEADS, GDN_DK = 4, 128
KEY_A = GDN_HEADS * GDN_DK
GDN_CONV = 4
ATT_HEADS, ATT_DH = 8, 64
WIDTH_B = ATT_HEADS * ATT_DH
ATT_BAND = 9
REL_CLIP = 128
D_FF = 2816
FFN_CONV = 3
D_IN = 5640
ADAM_LR, ADAM_B1, ADAM_B2, ADAM_EPS, ADAM_WD, ADAM_STEP = 0.001, 0.9, 0.999, 1e-08, 0.01, 10

LANES = 128
SUBLANES = 8
NEG = -1e30

PROJ_W = 5760
PB = 512
CB_GA, CB_GB = 0, 1
CB_KB, CB_VB, CB_QA, CB_KA, CB_VA, CB_QB, CB_ZA = 4, 5, 6, 7, 8, 9, 10
CB_BD = 44
DP_GATES, DP_KVB, DP_QKVA, DP_QB, DP_ZBD = (2048, 0), (1024, 2), (1536, 2), (512, 9), (640, 8)
W_IN_ORDER = ((3592, 5640), (2568, 3592), (0, 1536), (2056, 2568), (1536, 2048), (2048, 2056))

ATT_QB = 256
ATT_KW = 768
ATT_VEC = 1024


def _dot(a, b, precision=None):
    return jnp.dot(a, b, preferred_element_type=F32, precision=precision)


def _dot_nt(a, b, precision=None):
    return lax.dot_general(a, b, (((1,), (1,)), ((), ())), preferred_element_type=F32, precision=precision)


def _dot_tn(a, b):
    return lax.dot_general(a, b, (((0,), (0,)), ((), ())), preferred_element_type=F32)


def _split(a):
    hi = a.astype(BF16)
    return hi, (a - hi.astype(F32)).astype(BF16)


def _dot3s(a, b):
    return _dot(a[0], b[0]) + (_dot(a[0], b[1]) + _dot(a[1], b[0]))


def _sigmoid(x):
    return 0.5 * jnp.tanh(0.5 * x) + 0.5


def _softplus(x):
    return jnp.maximum(x, 0.0) + jnp.log(1.0 + jnp.exp(-jnp.abs(x)))


def _cparams(*sem):
    return pltpu.CompilerParams(dimension_semantics=tuple(sem))


def _dp_spec(tm, region, index=lambda i: i):
    width, cb = region
    return pl.BlockSpec((tm, width), lambda i: (index(i), cb))


def _rmsnorm_cast(x, w, name, tm=512, carry=((), ())):
    t, d = x.shape
    nt = t // tm
    arrays, scatter = carry
    nx = len(arrays)

    def body(*refs):
        x_ref, w_ref = refs[:2]
        o_ref = refs[2 + nx]
        i = pl.program_id(0)
        if nx:
            local, remote = _exchange_copies(refs[2:2 + nx], refs[3 + nx:3 + 2 * nx], scatter, *refs[3 + 2 * nx:])

            @pl.when(i == 0)
            def _():
                for cp in local + remote:
                    cp.start()

        xv = x_ref[...]
        r = lax.rsqrt(jnp.mean(xv * xv, axis=-1, keepdims=True) + EPS)
        o_ref[...] = (xv * r * w_ref[...]).astype(BF16)

        if nx:
            @pl.when(i == nt - 1)
            def _():
                for cp in remote + local:
                    cp.wait()

    any_spec = pl.BlockSpec(memory_space=pl.ANY)
    out = pl.pallas_call(
        body, name=name, grid=(nt,),
        in_specs=[pl.BlockSpec((tm, d), lambda i: (i, 0)), pl.BlockSpec((1, d), lambda i: (0, 0))] + [any_spec] * nx,
        out_specs=[pl.BlockSpec((tm, d), lambda i: (i, 0))] + [any_spec] * nx,
        out_shape=[jax.ShapeDtypeStruct((t, d), BF16)] + _exchange_shapes(arrays, scatter),
        scratch_shapes=_exchange_sems(nx) if nx else [],
        compiler_params=_cparams("arbitrary" if nx else "parallel"),
    )(x, w, *arrays)
    return out if nx else out[0]


def _mm_nn(a, b, out_dtype, name, tm, tn, tk, carry=((), ())):
    m, k = a.shape
    _, n = b.shape
    tm = min(tm, m)
    nk = k // tk
    assert m % tm == 0 and n % tn == 0 and k % tk == 0
    arrays, scatter = carry
    nx = len(arrays)
    gm, gn = m // tm, n // tn

    def body(*refs):
        a_ref, b_ref = refs[:2]
        srcs = refs[2:2 + nx]
        o_ref = refs[2 + nx]
        dsts = refs[3 + nx:3 + 2 * nx]
        rest = refs[3 + 2 * nx:]
        i, j, kk = pl.program_id(0), pl.program_id(1), pl.program_id(2)
        if nx:
            local, remote = _exchange_copies(srcs, dsts, scatter, *rest[-3:])

            @pl.when((i == 0) & (j == 0) & (kk == 0))
            def _():
                for cp in local + remote:
                    cp.start()

        if nk == 1:
            o_ref[...] = _dot(a_ref[...], b_ref[...]).astype(out_dtype)
        else:
            acc_ref = rest[0]

            @pl.when(kk == 0)
            def _():
                acc_ref[...] = jnp.zeros_like(acc_ref)

            acc_ref[...] += _dot(a_ref[...], b_ref[...])

            @pl.when(kk == nk - 1)
            def _():
                o_ref[...] = acc_ref[...].astype(out_dtype)

        if nx:
            @pl.when((i == gm - 1) & (j == gn - 1) & (kk == nk - 1))
            def _():
                for cp in remote + local:
                    cp.wait()

    any_spec = pl.BlockSpec(memory_space=pl.ANY)
    scratch = ([pltpu.VMEM((tm, tn), F32)] if nk > 1 else []) + (_exchange_sems(nx) if nx else [])
    out = pl.pallas_call(
        body, name=name, grid=(gm, gn, nk),
        in_specs=[pl.BlockSpec((tm, tk), lambda i, j, kk: (i, kk)),
                  pl.BlockSpec((tk, tn), lambda i, j, kk: (kk, j))] + [any_spec] * nx,
        out_specs=[pl.BlockSpec((tm, tn), lambda i, j, kk: (i, j))] + [any_spec] * nx,
        out_shape=[jax.ShapeDtypeStruct((m, n), out_dtype)] + _exchange_shapes(arrays, scatter),
        scratch_shapes=scratch,
        compiler_params=_cparams(*(("arbitrary",) * 3 if nx else ("parallel", "parallel", "arbitrary"))),
    )(a, b, *arrays)
    return out if nx else out[0]


def _mm_rms_bwd(a, b, x, w, dres, name, tm, tk, carry):
    m, k = a.shape
    _, n = b.shape
    nk = k // tk
    gm = m // tm
    assert m % tm == 0 and k % tk == 0
    arrays, scatter = carry
    nx = len(arrays)

    def body(*refs):
        a_ref, b_ref, x_ref, w_ref, dres_ref = refs[:5]
        srcs = refs[5:5 + nx]
        dx_ref, dxb_ref, dw_ref = refs[5 + nx:8 + nx]
        dsts = refs[8 + nx:8 + 2 * nx]
        acc_ref = refs[8 + 2 * nx]
        i, kk = pl.program_id(0), pl.program_id(1)
        local, remote = _exchange_copies(srcs, dsts, scatter, *refs[9 + 2 * nx:])

        @pl.when((i == 0) & (kk == 0))
        def _():
            for cp in local + remote:
                cp.start()
            dw_ref[...] = jnp.zeros_like(dw_ref)

        @pl.when(kk == 0)
        def _():
            acc_ref[...] = jnp.zeros_like(acc_ref)

        acc_ref[...] += _dot(a_ref[...], b_ref[...])

        @pl.when(kk == nk - 1)
        def _():
            dhv = acc_ref[...]
            xv = x_ref[...]
            r = lax.rsqrt(jnp.mean(xv * xv, axis=-1, keepdims=True) + EPS)
            xh = xv * r
            dw_ref[0:1, :] += jnp.sum(dhv * xh, axis=0, keepdims=True)
            dxh = dhv * w_ref[...]
            dx = dres_ref[...] + r * (dxh - xh * jnp.mean(dxh * xh, axis=-1, keepdims=True))
            dx_ref[...] = dx
            dxb_ref[...] = dx.astype(BF16)

        @pl.when((i == gm - 1) & (kk == nk - 1))
        def _():
            for cp in remote + local:
                cp.wait()

    any_spec = pl.BlockSpec(memory_space=pl.ANY)
    row = pl.BlockSpec((tm, n), lambda i, kk: (i, 0))
    return pl.pallas_call(
        body, name=name, grid=(gm, nk),
        in_specs=[pl.BlockSpec((tm, tk), lambda i, kk: (i, kk)), pl.BlockSpec((tk, n), lambda i, kk: (kk, 0)),
                  row, pl.BlockSpec((1, n), lambda i, kk: (0, 0)), row] + [any_spec] * nx,
        out_specs=[row, row, pl.BlockSpec((SUBLANES, n), lambda i, kk: (0, 0))] + [any_spec] * nx,
        out_shape=[jax.ShapeDtypeStruct((m, n), F32), jax.ShapeDtypeStruct((m, n), BF16),
                   jax.ShapeDtypeStruct((SUBLANES, n), F32)] + _exchange_shapes(arrays, scatter),
        scratch_shapes=[pltpu.VMEM((tm, n), F32)] + _exchange_sems(nx),
        compiler_params=_cparams("arbitrary", "arbitrary"),
    )(a, b, x, w, dres, *arrays)


MM_TM = 1024


def _mm_tn(a, b, name, tn, tk=2 * MM_TM):
    t, m = a.shape
    _, n = b.shape
    tk = min(tk, t)
    assert t % tk == 0 and n % tn == 0

    def body(a_ref, b_ref, o_ref):
        @pl.when(pl.program_id(1) == 0)
        def _():
            o_ref[...] = jnp.zeros_like(o_ref)

        o_ref[...] += _dot_tn(a_ref[...], b_ref[...])

    return pl.pallas_call(
        body, name=name, grid=(n // tn, t // tk),
        in_specs=[pl.BlockSpec((tk, m), lambda j, s: (s, 0)),
                  pl.BlockSpec((tk, tn), lambda j, s: (s, j))],
        out_specs=pl.BlockSpec((m, tn), lambda j, s: (0, j)),
        out_shape=jax.ShapeDtypeStruct((m, n), F32),
        compiler_params=_cparams("parallel", "arbitrary"),
    )(a, b)


def _rel_index(dist):
    return np.clip(dist, -REL_CLIP, REL_CLIP) + REL_CLIP


def _bias_onehots():
    tw = 3 * LANES
    m = np.arange(ATT_VEC)
    dq = np.where(m <= ATT_KW, 512 - m, 512 - (m - ATT_VEC))
    dk = np.where(m < ATT_KW, m, m - ATT_VEC)
    ohq = np.zeros((tw, ATT_VEC), np.float32)
    ohk = np.zeros((tw, ATT_VEC), np.float32)
    ohq[_rel_index(dq), m] = 1.0
    ohk[_rel_index(dk), m] = 1.0
    return ohq, ohk


def _att_bias(table_pad):
    ohq, ohk = _bias_onehots()
    nslab = ATT_QB // SUBLANES

    def body(t_ref, ohq_ref, ohk_ref, bq_ref, bk_ref):
        tv = jnp.broadcast_to(t_ref[...], (SUBLANES, 3 * LANES))
        lane = lax.broadcasted_iota(jnp.int32, (ATT_QB, ATT_KW), 1)
        row = lax.broadcasted_iota(jnp.int32, (ATT_QB, ATT_KW), 0) // CHUNK
        col = lane // CHUNK
        band = (col >= row) & (col <= row + ATT_BAND - 1)
        for which, (oh_ref, out_ref) in enumerate(((ohq_ref, bq_ref), (ohk_ref, bk_ref))):
            vec = _dot(tv, oh_ref[...], HIGHEST)[0:1, :]
            slab = jnp.concatenate([vec if b == 0 else pltpu.roll(vec, b, 1) for b in range(SUBLANES)], axis=0)
            rows = [slab if a == 0 else pltpu.roll(slab, SUBLANES * a, 1) for a in range(nslab)]
            full = jnp.concatenate(rows, axis=0)[:, :ATT_KW]
            for v in range(3):
                inside = (lane >= (2 - v) * ATT_QB) if which == 0 else (lane < (v + 1) * ATT_QB)
                out_ref[v] = jnp.where(band & inside, full, NEG)

    h = table_pad.shape[0]
    oh_spec = pl.BlockSpec((3 * LANES, ATT_VEC), lambda i: (0, 0))
    out_spec = pl.BlockSpec((3, None, ATT_QB, ATT_KW), lambda i: (0, i, 0, 0))
    return pl.pallas_call(
        body, name="att_bias", grid=(h,),
        in_specs=[pl.BlockSpec((None, 1, 3 * LANES), lambda i: (i, 0, 0)), oh_spec, oh_spec],
        out_specs=[out_spec, out_spec],
        out_shape=[jax.ShapeDtypeStruct((3, h, ATT_QB, ATT_KW), F32)] * 2,
        compiler_params=_cparams("parallel"),
    )(table_pad, jnp.asarray(ohq), jnp.asarray(ohk))


def _head_masks():
    lane = lax.broadcasted_iota(jnp.int32, (1, LANES), 1)
    return [lane < ATT_DH, lane >= ATT_DH]


def _att_fwd(proj, bias_q):
    t = proj.shape[0]
    nb = t // ATT_QB
    scale = ATT_DH ** -0.5

    def body(q_ref, k0_ref, k1_ref, k2_ref, v0_ref, v1_ref, v2_ref, b_ref, o_ref, lse_ref, lset_ref):
        i = pl.program_id(0)
        q = (q_ref[...] * scale).astype(BF16)
        kk = jnp.concatenate([k0_ref[...], k1_ref[...], k2_ref[...]], axis=0).astype(BF16)
        vv = jnp.concatenate([v0_ref[...], v1_ref[...], v2_ref[...]], axis=0).astype(BF16)
        lane = lax.broadcasted_iota(jnp.int32, (1, LANES), 1)
        masks = _head_masks()
        lse_cols = jnp.zeros((ATT_QB, LANES), F32)
        for p in range(ATT_HEADS // 2):
            cs = slice(p * LANES, (p + 1) * LANES)
            qt, kt, vt = q[:, cs], kk[:, cs], vv[:, cs]
            acc = jnp.zeros((ATT_QB, LANES), F32)
            for sub in range(2):
                h = 2 * p + sub
                s = _dot_nt(jnp.where(masks[sub], qt, 0), kt) + b_ref[h]
                mx = jnp.max(s, axis=-1, keepdims=True)
                e = jnp.exp(s - mx)
                l = jnp.sum(e, axis=-1, keepdims=True)
                acc = acc + _dot(e.astype(BF16), jnp.where(masks[sub], vt, 0)) * (1.0 / l)
                lse_cols = lse_cols + jnp.where(lane == h, mx + jnp.log(l), 0.0)
            o_ref[:, cs] = acc.astype(BF16)
        lse_ref[...] = lse_cols
        lset_ref[...] = lse_cols.T[0:SUBLANES, :]

    def kv_spec(off, cb):
        return pl.BlockSpec((ATT_QB, PB), lambda i: (jnp.maximum(i + off, 0), cb))

    return pl.pallas_call(
        body, name="att_fwd", grid=(nb,),
        in_specs=[pl.BlockSpec((ATT_QB, PB), lambda i: (i, CB_QB)),
                  kv_spec(-2, CB_KB), kv_spec(-1, CB_KB), kv_spec(0, CB_KB),
                  kv_spec(-2, CB_VB), kv_spec(-1, CB_VB), kv_spec(0, CB_VB),
                  pl.BlockSpec((None, ATT_HEADS, ATT_QB, ATT_KW), lambda i: (jnp.minimum(i, 2), 0, 0, 0))],
        out_specs=[pl.BlockSpec((ATT_QB, WIDTH_B), lambda i: (i, 0)),
                   pl.BlockSpec((ATT_QB, LANES), lambda i: (i, 0)),
                   pl.BlockSpec((SUBLANES, ATT_QB), lambda i: (0, i))],
        out_shape=[jax.ShapeDtypeStruct((t, WIDTH_B), BF16), jax.ShapeDtypeStruct((t, LANES), F32),
                   jax.ShapeDtypeStruct((SUBLANES, t), F32)],
        compiler_params=_cparams("parallel"),
    )(proj, proj, proj, proj, proj, proj, proj, bias_q)


def _att_dq(proj, bias_q, lse, d_ob, dproj):
    t = proj.shape[0]
    nb = t // ATT_QB
    scale = ATT_DH ** -0.5
    nslab = ATT_QB // SUBLANES

    def body(q_ref, k0_ref, k1_ref, k2_ref, v0_ref, v1_ref, v2_ref, b_ref, lse_ref, do_ref, dp_in_ref,
             dq_ref, dlt_ref, slab_ref):
        i = pl.program_id(0)

        @pl.when(i == 0)
        def _():
            slab_ref[...] = jnp.zeros_like(slab_ref)

        q = (q_ref[...] * scale).astype(BF16)
        kk = jnp.concatenate([k0_ref[...], k1_ref[...], k2_ref[...]], axis=0).astype(BF16)
        vv = jnp.concatenate([v0_ref[...], v1_ref[...], v2_ref[...]], axis=0).astype(BF16)
        do = do_ref[...].astype(BF16)
        lane = lax.broadcasted_iota(jnp.int32, (1, LANES), 1)
        masks = _head_masks()
        lse_all = lse_ref[...]
        dlt_cols = jnp.zeros((ATT_QB, LANES), F32)
        zpad = jnp.zeros((SUBLANES, ATT_VEC - ATT_KW), F32)
        for p in range(ATT_HEADS // 2):
            cs = slice(p * LANES, (p + 1) * LANES)
            qt, kt, vt, dot_ = q[:, cs], kk[:, cs], vv[:, cs], do[:, cs]
            acc = jnp.zeros((ATT_QB, LANES), F32)
            for sub in range(2):
                h = 2 * p + sub
                s = _dot_nt(jnp.where(masks[sub], qt, 0), kt) + b_ref[h]
                pr = jnp.exp(s - lse_all[:, h:h + 1])
                dp = _dot_nt(jnp.where(masks[sub], dot_, 0), vt)
                dl = jnp.sum(pr * dp, axis=-1, keepdims=True)
                ds = pr * (dp - dl)
                acc = acc + _dot(ds.astype(BF16), jnp.where(masks[sub], kt, 0)) * scale
                dlt_cols = dlt_cols + jnp.where(lane == h, dl, 0.0)
                sl = jnp.zeros((SUBLANES, ATT_VEC), F32)
                for a in range(nslab):
                    piece = jnp.concatenate([ds[a * SUBLANES:(a + 1) * SUBLANES, :], zpad], axis=1)
                    sl = sl + (piece if a == 0 else pltpu.roll(piece, ATT_VEC - SUBLANES * a, 1))
                slab_ref[h] += sl
            dq_ref[:, cs] = acc.astype(BF16)
        dlt_ref[...] = dlt_cols.T[0:SUBLANES, :]

    def kv_spec(off, cb):
        return pl.BlockSpec((ATT_QB, PB), lambda i: (jnp.maximum(i + off, 0), cb))

    return pl.pallas_call(
        body, name="att_dq", grid=(nb,),
        in_specs=[pl.BlockSpec((ATT_QB, PB), lambda i: (i, CB_QB)),
                  kv_spec(-2, CB_KB), kv_spec(-1, CB_KB), kv_spec(0, CB_KB),
                  kv_spec(-2, CB_VB), kv_spec(-1, CB_VB), kv_spec(0, CB_VB),
                  pl.BlockSpec((None, ATT_HEADS, ATT_QB, ATT_KW), lambda i: (jnp.minimum(i, 2), 0, 0, 0)),
                  pl.BlockSpec((ATT_QB, LANES), lambda i: (i, 0)),
                  pl.BlockSpec((ATT_QB, WIDTH_B), lambda i: (i, 0)), pl.BlockSpec(memory_space=pl.ANY)],
        out_specs=[_dp_spec(ATT_QB, DP_QB),
                   pl.BlockSpec((SUBLANES, ATT_QB), lambda i: (0, i)),
                   pl.BlockSpec((ATT_HEADS, SUBLANES, ATT_VEC), lambda i: (0, 0, 0))],
        out_shape=[jax.ShapeDtypeStruct(dproj.shape, dproj.dtype), jax.ShapeDtypeStruct((SUBLANES, t), F32),
                   jax.ShapeDtypeStruct((ATT_HEADS, SUBLANES, ATT_VEC), F32)],
        input_output_aliases={10: 0},
        compiler_params=_cparams("arbitrary"),
    )(proj, proj, proj, proj, proj, proj, proj, bias_q, lse, d_ob, dproj)


def _att_dkv(proj, bias_k, lse_t, dlt_t, d_ob, dproj):
    t = proj.shape[0]
    nb = t // ATT_QB
    scale = ATT_DH ** -0.5

    def body(k_ref, v_ref, q0_ref, q1_ref, q2_ref, d0_ref, d1_ref, d2_ref, l0_ref, l1_ref, l2_ref,
             e0_ref, e1_ref, e2_ref, b_ref, dp_in_ref, dkv_ref):
        i = pl.program_id(0)
        k = k_ref[...].astype(BF16)
        v = v_ref[...].astype(BF16)
        qq = (jnp.concatenate([q0_ref[...], q1_ref[...], q2_ref[...]], axis=0) * scale).astype(BF16)
        do = jnp.concatenate([d0_ref[...], d1_ref[...], d2_ref[...]], axis=0).astype(BF16)
        lse = jnp.concatenate([l0_ref[...], l1_ref[...], l2_ref[...]], axis=1)
        dlt = jnp.concatenate([e0_ref[...], e1_ref[...], e2_ref[...]], axis=1)
        masks = _head_masks()
        for p in range(ATT_HEADS // 2):
            cs = slice(p * LANES, (p + 1) * LANES)
            kt, vt, qt, dot_ = k[:, cs], v[:, cs], qq[:, cs], do[:, cs]
            acc_k = jnp.zeros((ATT_QB, LANES), F32)
            acc_v = jnp.zeros((ATT_QB, LANES), F32)
            for sub in range(2):
                h = 2 * p + sub
                st = _dot_nt(jnp.where(masks[sub], kt, 0), qt) + b_ref[h]
                pt = jnp.exp(st - lse[h:h + 1, :])
                dot_m = jnp.where(masks[sub], dot_, 0)
                acc_v = acc_v + _dot(pt.astype(BF16), dot_m)
                dpt = _dot_nt(jnp.where(masks[sub], vt, 0), dot_)
                dst = pt * (dpt - dlt[h:h + 1, :])
                acc_k = acc_k + _dot(dst.astype(BF16), jnp.where(masks[sub], qt, 0))
            dkv_ref[:, cs] = acc_k.astype(BF16)
            dkv_ref[:, WIDTH_B + p * LANES:WIDTH_B + (p + 1) * LANES] = acc_v.astype(BF16)

    def q_spec(off, cb):
        return pl.BlockSpec((ATT_QB, PB), lambda i: (jnp.minimum(i + off, nb - 1), cb))

    def d_spec(off):
        return pl.BlockSpec((ATT_QB, WIDTH_B), lambda i: (jnp.minimum(i + off, nb - 1), 0))

    def r_spec(off):
        return pl.BlockSpec((SUBLANES, ATT_QB), lambda i: (0, jnp.minimum(i + off, nb - 1)))

    row = pl.BlockSpec((ATT_QB, WIDTH_B), lambda i: (i, 0))
    return pl.pallas_call(
        body, name="att_dkv", grid=(nb,),
        in_specs=[pl.BlockSpec((ATT_QB, PB), lambda i: (i, CB_KB)), pl.BlockSpec((ATT_QB, PB), lambda i: (i, CB_VB)),
                  q_spec(0, CB_QB), q_spec(1, CB_QB), q_spec(2, CB_QB),
                  d_spec(0), d_spec(1), d_spec(2), r_spec(0), r_spec(1), r_spec(2),
                  r_spec(0), r_spec(1), r_spec(2),
                  pl.BlockSpec((None, ATT_HEADS, ATT_QB, ATT_KW), lambda i: (jnp.minimum(nb - 1 - i, 2), 0, 0, 0)),
                  pl.BlockSpec(memory_space=pl.ANY)],
        out_specs=_dp_spec(ATT_QB, DP_KVB),
        out_shape=jax.ShapeDtypeStruct(dproj.shape, dproj.dtype),
        input_output_aliases={15: 0},
        compiler_params=_cparams("parallel"),
    )(proj, proj, proj, proj, proj, d_ob, d_ob, d_ob, lse_t, lse_t, lse_t, dlt_t, dlt_t, dlt_t, bias_k, dproj)


def _relbias_grad(slabs):
    ohq, _ = _bias_onehots()

    def body(s_ref, oh_ref, o_ref):
        sv = s_ref[...]
        vec = sv[0:1, :]
        for b in range(1, SUBLANES):
            vec = vec + pltpu.roll(sv[b:b + 1, :], ATT_VEC - b, 1)
        o_ref[...] = _dot_nt(jnp.broadcast_to(vec, (SUBLANES, ATT_VEC)), oh_ref[...], HIGHEST)[0:1, :]

    h = slabs.shape[0]
    return pl.pallas_call(
        body, name="att_dbias", grid=(h,),
        in_specs=[pl.BlockSpec((None, SUBLANES, ATT_VEC), lambda i: (i, 0, 0)),
                  pl.BlockSpec((3 * LANES, ATT_VEC), lambda i: (0, 0))],
        out_specs=pl.BlockSpec((None, 1, 3 * LANES), lambda i: (i, 0, 0)),
        out_shape=jax.ShapeDtypeStruct((h, 1, 3 * LANES), F32),
        compiler_params=_cparams("parallel"),
    )(slabs, jnp.asarray(ohq))


GDN_TM = 512
GDN_CB = 4
HALO = SUBLANES


def _conv_taps(ext, width, lead, n):
    return [(ext if k == width - 1 else pltpu.roll(ext, width - 1 - k, 0))[lead:lead + n] for k in range(width)]


def _prev_halo_spec(tm, width, cb):
    return pl.BlockSpec((HALO, width), lambda i: (jnp.maximum(i * (tm // HALO) - 1, 0), cb))


def _next_halo_spec(tm, width, cb, t):
    return pl.BlockSpec((HALO, width), lambda i: (jnp.minimum((i + 1) * (tm // HALO), t // HALO - 1), cb))


def _gdn_prep_fwd(proj, conv_w):
    t = proj.shape[0]
    tm = GDN_TM

    def body(q_ref, k_ref, v_ref, hq_ref, hk_ref, hv_ref, w_ref, qn_ref, kn_ref, vo_ref):
        first = pl.program_id(0) == 0
        for idx, (x_ref, h_ref, o_ref) in enumerate(((q_ref, hq_ref, qn_ref), (k_ref, hk_ref, kn_ref),
                                                      (v_ref, hv_ref, vo_ref))):
            halo = jnp.where(first, 0.0, h_ref[...])
            ext = jnp.concatenate([halo, x_ref[...]], axis=0)
            w = w_ref[:, idx * KEY_A:(idx + 1) * KEY_A]
            taps = _conv_taps(ext, GDN_CONV, HALO, tm)
            y = sum(w[k:k + 1, :] * taps[k] for k in range(GDN_CONV))
            a = y * _sigmoid(y)
            if idx < 2:
                for h in range(GDN_HEADS):
                    cs = slice(h * GDN_DK, (h + 1) * GDN_DK)
                    seg = a[:, cs]
                    o_ref[:, cs] = seg * lax.rsqrt(jnp.sum(seg * seg, axis=-1, keepdims=True) + EPS)
            else:
                o_ref[...] = a

    row = pl.BlockSpec((tm, KEY_A), lambda i: (i, 0))
    return pl.pallas_call(
        body, name="gdn_prep_fwd", grid=(t // tm,),
        in_specs=[pl.BlockSpec((tm, PB), lambda i: (i, CB_QA)), pl.BlockSpec((tm, PB), lambda i: (i, CB_KA)),
                  pl.BlockSpec((tm, PB), lambda i: (i, CB_VA)),
                  _prev_halo_spec(tm, PB, CB_QA), _prev_halo_spec(tm, PB, CB_KA), _prev_halo_spec(tm, PB, CB_VA),
                  pl.BlockSpec((GDN_CONV, 3 * KEY_A), lambda i: (0, 0))],
        out_specs=[row, row, row],
        out_shape=[jax.ShapeDtypeStruct((t, KEY_A), F32)] * 3,
        compiler_params=_cparams("parallel"),
    )(proj, proj, proj, proj, proj, proj, conv_w)


def _gdn_prep_bwd(proj, conv_w, dqn, dkn, dv, dproj):
    t = proj.shape[0]
    tm = GDN_TM
    nt = t // tm
    n_ext = tm + HALO

    def body(q_ref, k_ref, v_ref, pq_ref, pk_ref, pv_ref, nq_ref, nk_ref, nv_ref,
             dq_ref, dk_ref, dv_ref, ndq_ref, ndk_ref, ndv_ref, w_ref, dp_in_ref, out_ref, dw_ref):
        i = pl.program_id(0)
        first, last = i == 0, i == nt - 1

        @pl.when(first)
        def _():
            dw_ref[...] = jnp.zeros_like(dw_ref)

        groups = ((q_ref, pq_ref, nq_ref, dq_ref, ndq_ref), (k_ref, pk_ref, nk_ref, dk_ref, ndk_ref),
                  (v_ref, pv_ref, nv_ref, dv_ref, ndv_ref))
        for idx, (x_ref, p_ref, n_ref, d_ref, nd_ref) in enumerate(groups):
            cs_all = slice(idx * KEY_A, (idx + 1) * KEY_A)
            ext = jnp.concatenate([jnp.where(first, 0.0, p_ref[...]), x_ref[...], jnp.where(last, 0.0, n_ref[...])], axis=0)
            w = w_ref[:, cs_all]
            taps = _conv_taps(ext, GDN_CONV, HALO, n_ext)
            y = sum(w[k:k + 1, :] * taps[k] for k in range(GDN_CONV))
            sg = _sigmoid(y)
            a = y * sg
            dup = jnp.concatenate([d_ref[...], jnp.where(last, 0.0, nd_ref[...])], axis=0)
            if idx < 2:
                segs = []
                for h in range(GDN_HEADS):
                    cs = slice(h * GDN_DK, (h + 1) * GDN_DK)
                    seg = a[:, cs]
                    r = lax.rsqrt(jnp.sum(seg * seg, axis=-1, keepdims=True) + EPS)
                    nrm = seg * r
                    dn = dup[:, cs]
                    segs.append(r * (dn - nrm * jnp.sum(dn * nrm, axis=-1, keepdims=True)))
                da = jnp.concatenate(segs, axis=1)
            else:
                da = dup
            dy = da * sg * (1.0 + y * (1.0 - sg))
            dx = sum(w[k:k + 1, :] * (dy if k == GDN_CONV - 1 else pltpu.roll(dy, n_ext - (GDN_CONV - 1 - k), 0))[:tm]
                     for k in range(GDN_CONV))
            out_ref[:, cs_all] = dx.astype(BF16)
            for k in range(GDN_CONV):
                dw_ref[k:k + 1, cs_all] += jnp.sum(dy[:tm] * taps[k][:tm], axis=0, keepdims=True)

    row = pl.BlockSpec((tm, KEY_A), lambda i: (i, 0))
    nrow = _next_halo_spec(tm, KEY_A, 0, t)
    return pl.pallas_call(
        body, name="gdn_prep_bwd", grid=(nt,),
        in_specs=[pl.BlockSpec((tm, PB), lambda i: (i, CB_QA)), pl.BlockSpec((tm, PB), lambda i: (i, CB_KA)),
                  pl.BlockSpec((tm, PB), lambda i: (i, CB_VA)),
                  _prev_halo_spec(tm, PB, CB_QA), _prev_halo_spec(tm, PB, CB_KA), _prev_halo_spec(tm, PB, CB_VA),
                  _next_halo_spec(tm, PB, CB_QA, t), _next_halo_spec(tm, PB, CB_KA, t), _next_halo_spec(tm, PB, CB_VA, t),
                  row, row, row, nrow, nrow, nrow,
                  pl.BlockSpec((GDN_CONV, 3 * KEY_A), lambda i: (0, 0)), pl.BlockSpec(memory_space=pl.ANY)],
        out_specs=[_dp_spec(tm, DP_QKVA), pl.BlockSpec((SUBLANES, 3 * KEY_A), lambda i: (0, 0))],
        out_shape=[jax.ShapeDtypeStruct(dproj.shape, dproj.dtype), jax.ShapeDtypeStruct((SUBLANES, 3 * KEY_A), F32)],
        input_output_aliases={16: 0},
        compiler_params=_cparams("arbitrary"),
    )(proj, proj, proj, proj, proj, proj, proj, proj, proj, dqn, dkn, dv, dqn, dkn, dv, conv_w, dproj)


class _Pair(dict):
    __getattr__ = dict.__getitem__
    __setattr__ = dict.__setitem__


def _pairs_to_lanes(cols):
    lane = lax.broadcasted_iota(jnp.int32, (1, LANES), 1)
    out = jnp.zeros((cols[0].shape[0], LANES), F32)
    for p, col in enumerate(cols):
        out = out + jnp.where(lane == p, col, 0.0)
    return out


def _gdn_terms(bd, par, kn_ref, qn_ref):
    c = CHUNK
    ii = lax.broadcasted_iota(jnp.int32, (c, c), 0)
    jj = lax.broadcasted_iota(jnp.int32, (c, c), 1)
    strict, incl = ii > jj, ii >= jj
    ltri = incl.astype(F32)
    ts = []
    for cc in range(GDN_CB):
        for h in range(GDN_HEADS):
            t = _Pair(cc=cc, h=h, rows=slice(cc * c, (cc + 1) * c), cs=slice(h * GDN_DK, (h + 1) * GDN_DK),
                      strict=strict, incl=incl)
            t.beta = _sigmoid(bd[t.rows, h:h + 1])
            t.ea = jnp.exp(par[0:1, h:h + 1])
            t.sp_arg = bd[t.rows, GDN_HEADS + h:GDN_HEADS + h + 1] + par[1:2, h:h + 1]
            t.g = -t.ea * _softplus(t.sp_arg)
            t.k = kn_ref[t.rows, t.cs]
            t.q = qn_ref[t.rows, t.cs] * (GDN_DK ** -0.5)
            t.kb, t.qb = t.k.astype(BF16), t.q.astype(BF16)
            ts.append(t)
    gall = _dot(ltri, _pairs_to_lanes([t.g for t in ts]), HIGHEST)
    gall_t = gall.T
    for p, t in enumerate(ts):
        t.gb = jnp.broadcast_to(gall[:, p:p + 1], (c, GDN_DK))
    for t in ts:
        t.kk = _dot_nt(t.kb, t.kb)
        t.qk = _dot_nt(t.qb, t.kb)
    for p, t in enumerate(ts):
        diff = t.gb[:, :c] - gall_t[p:p + 1, :]
        t.dec_s = jnp.exp(jnp.where(strict, diff, NEG))
        t.dec_i = jnp.exp(jnp.where(incl, diff, NEG))
        t.gam = jnp.exp(t.gb)
        glast = t.gb[c - 1:c, :]
        t.e_rest = jnp.exp(glast - t.gb)
        t.gl = jnp.exp(glast)
        t.p = t.qk * t.dec_i
    return ts


def _gdn_fwd(qn, kn, v, proj, par, gnw):
    t = qn.shape[0]
    c = CHUNK
    nc = t // c
    r_ = GDN_CB * c

    def body(qn_ref, kn_ref, v_ref, bd_ref, z_ref, par_ref, gnw_ref,
             oan_ref, o_ref, sp_ref, w_ref, u_ref, tm_ref, s_ref):
        @pl.when(pl.program_id(0) == 0)
        def _():
            s_ref[...] = jnp.zeros_like(s_ref)

        bd, par, gnw_v = bd_ref[...], par_ref[...], gnw_ref[...]
        eye = (lax.broadcasted_iota(jnp.int32, (c, c), 0) == lax.broadcasted_iota(jnp.int32, (c, c), 1)).astype(F32)
        ts = _gdn_terms(bd, par, kn_ref, qn_ref)
        for t in ts:
            t.vv = v_ref[t.rows, t.cs]
            t.x = -(t.beta * t.kk * t.dec_s)
            t.tinv = eye + t.x
        for t in ts:
            t.xs = _split(t.x)
        for _ in range(5):
            for t in ts:
                t.xs = _split(_dot3s(t.xs, t.xs))
            for t in ts:
                t.tinv = t.tinv + _dot3s(_split(t.tinv), t.xs)
        for t in ts:
            tsp = _split(t.tinv)
            t.wm = _dot3s(tsp, _split((t.beta * t.gam) * t.k))
            t.uv = _dot3s(tsp, _split(t.beta * t.vv))
        for t in ts:
            w_ref[t.rows, t.cs] = t.wm
            tm_ref[t.cc, t.h] = t.tinv.T
            t.wb = t.wm.astype(BF16)
            t.qgb = (t.q * t.gam).astype(BF16)
            t.kdb = (t.k * t.e_rest).astype(BF16)
            t.pb = t.p.astype(BF16)
        state = [s_ref[h] for h in range(GDN_HEADS)]
        for cc in range(GDN_CB):
            tc = [t for t in ts if t.cc == cc]
            for t in tc:
                t.sh = state[t.h]
                t.sb = t.sh.astype(BF16)
            for t in tc:
                t.ws = _dot(t.wb, t.sb)
            for t in tc:
                t.u = t.uv - t.ws
                t.ub = t.u.astype(BF16)
            for t in tc:
                state[t.h] = t.gl * t.sh + _dot_tn(t.kdb, t.ub)
            for t in tc:
                t.o = _dot(t.qgb, t.sb) + _dot(t.pb, t.ub)
                sp_ref[cc, t.h] = t.sh
                u_ref[t.rows, t.cs] = t.u
                o_ref[t.rows, t.cs] = t.o
        for h in range(GDN_HEADS):
            s_ref[h] = state[h]
        for t in ts:
            zz = z_ref[t.rows, t.cs]
            rr = lax.rsqrt(jnp.mean(t.o * t.o, axis=-1, keepdims=True) + EPS)
            oan_ref[t.rows, t.cs] = ((t.o * rr) * gnw_v * (zz * _sigmoid(zz))).astype(BF16)

    row = pl.BlockSpec((r_, KEY_A), lambda i: (i, 0))
    return pl.pallas_call(
        body, name="gdn_fwd", grid=(nc // GDN_CB,),
        in_specs=[row, row, row, pl.BlockSpec((r_, LANES), lambda i: (i, CB_BD)),
                  pl.BlockSpec((r_, PB), lambda i: (i, CB_ZA)),
                  pl.BlockSpec((SUBLANES, LANES), lambda i: (0, 0)), pl.BlockSpec((1, GDN_DK), lambda i: (0, 0))],
        out_specs=[row, row, pl.BlockSpec((GDN_CB, GDN_HEADS, GDN_DK, GDN_DK), lambda i: (i, 0, 0, 0)),
                   row, row, pl.BlockSpec((GDN_CB, GDN_HEADS, c, c), lambda i: (i, 0, 0, 0))],
        out_shape=[jax.ShapeDtypeStruct((t, KEY_A), BF16), jax.ShapeDtypeStruct((t, KEY_A), F32),
                   jax.ShapeDtypeStruct((nc, GDN_HEADS, GDN_DK, GDN_DK), F32),
                   jax.ShapeDtypeStruct((t, KEY_A), F32), jax.ShapeDtypeStruct((t, KEY_A), F32),
                   jax.ShapeDtypeStruct((nc, GDN_HEADS, c, c), F32)],
        scratch_shapes=[pltpu.VMEM((GDN_HEADS, GDN_DK, GDN_DK), F32)],
        compiler_params=_cparams("arbitrary"),
    )(qn, kn, v, proj, proj, par, gnw)


def _gdn_bwd(qn, kn, v, proj, par, gnw, o, sprev, wst, ust, tst, d_oan, dproj):
    t = qn.shape[0]
    c = CHUNK
    nc = t // c
    nb = nc // GDN_CB
    r_ = GDN_CB * c

    def body(qn_ref, kn_ref, v_ref, bd_ref, z_ref, par_ref, gnw_ref, o_ref, sp_ref, w_ref, u_ref, tm_ref, do_ref,
             dp_in_ref, dqn_ref, dkn_ref, dv_ref, dzb_ref, acc_ref, ds_ref):
        @pl.when(pl.program_id(0) == 0)
        def _():
            ds_ref[...] = jnp.zeros_like(ds_ref)
            acc_ref[...] = jnp.zeros_like(acc_ref)

        bd, par, gnw_v = bd_ref[...], par_ref[...], gnw_ref[...]
        lane = lax.broadcasted_iota(jnp.int32, (1, LANES), 1)
        rix = lax.broadcasted_iota(jnp.int32, (c, 1), 0)
        ii = lax.broadcasted_iota(jnp.int32, (c, c), 0)
        jj = lax.broadcasted_iota(jnp.int32, (c, c), 1)
        upper = (jj >= ii).astype(F32)
        acc_a = jnp.zeros((1, LANES), F32)
        acc_d = jnp.zeros((1, LANES), F32)
        acc_g = jnp.zeros((1, LANES), F32)
        ts = _gdn_terms(bd, par, kn_ref, qn_ref)
        for t in ts:
            t.vv = v_ref[t.rows, t.cs]
            t.sh = sp_ref[t.cc, t.h]
            t.sb = t.sh.astype(BF16)
            t.wm, t.u, t.tinv_t = w_ref[t.rows, t.cs], u_ref[t.rows, t.cs], tm_ref[t.cc, t.h]
            t.wb, t.ub = t.wm.astype(BF16), t.u.astype(BF16)
            ov, zz, dout = o_ref[t.rows, t.cs], z_ref[t.rows, t.cs], do_ref[t.rows, t.cs]
            sg = _sigmoid(zz)
            sil = zz * sg
            rr = lax.rsqrt(jnp.mean(ov * ov, axis=-1, keepdims=True) + EPS)
            on = ov * rr
            dzb_ref[t.rows, t.cs] = (dout * on * gnw_v * (sg * (1.0 + zz * (1.0 - sg)))).astype(BF16)
            acc_g = acc_g + jnp.sum(dout * on * sil, axis=0, keepdims=True)
            don = dout * gnw_v * sil
            t.dob = (rr * (don - on * jnp.mean(don * on, axis=-1, keepdims=True))).astype(BF16)
            t.qg = t.q * t.gam
            t.kd = t.k * t.e_rest
            t.qgb, t.kdb = t.qg.astype(BF16), t.kd.astype(BF16)
            t.ptb = t.p.T.astype(BF16)
        for t in ts:
            t.du0 = _dot(t.ptb, t.dob)
            t.ds0 = _dot_tn(t.qgb, t.dob)
            t.dqg = _dot_nt(t.dob, t.sb)
            t.dp = _dot_nt(t.dob, t.ub)
            t.uv = t.u + _dot(t.wb, t.sb)
        dstate = [ds_ref[h] for h in range(GDN_HEADS)]
        for cc in reversed(range(GDN_CB)):
            tc = [t for t in ts if t.cc == cc]
            for t in tc:
                t.dsn = dstate[t.h]
                t.dsnb = t.dsn.astype(BF16)
            for t in tc:
                t.du = t.du0 + _dot(t.kdb, t.dsnb)
            for t in tc:
                t.dub = t.du.astype(BF16)
            for t in tc:
                dstate[t.h] = t.gl * t.dsn + t.ds0 - _dot_tn(t.wb, t.dub)
            for t in tc:
                t.dkd = _dot_nt(t.ub, t.dsnb)
                t.dgl = jnp.sum(jnp.sum(t.dsn * t.sh, axis=1, keepdims=True), axis=0, keepdims=True)
                t.dwm = -_dot_nt(t.dub, t.sb)
        for h in range(GDN_HEADS):
            ds_ref[h] = dstate[h]
        for t in ts:
            tsp = _split(t.tinv_t)
            t.dbk = _dot3s(tsp, _split(t.dwm))
            t.dbv = _dot3s(tsp, _split(t.du))
        for t in ts:
            d_a = -(_dot_nt(t.dbk.astype(BF16), t.wb) + _dot_nt(t.dbv.astype(BF16), t.uv.astype(BF16)))
            t.d_a = jnp.where(t.strict, d_a, 0.0)
        for t in ts:
            t.dkk = t.d_a * t.beta * t.dec_s
            t.dqk = t.dp * t.dec_i
            t.dqkb = t.dqk.astype(BF16)
        for t in ts:
            t.dq = _dot(t.dqkb, t.kb) + t.dqg * t.gam
            t.dk = (t.dbk * (t.beta * t.gam) + _dot_tn(t.dqkb, t.qb) + _dot((t.dkk + t.dkk.T).astype(BF16), t.kb)
                    + t.dkd * t.e_rest)
        for t in ts:
            dbeta = (jnp.sum(t.d_a * t.kk * t.dec_s, axis=-1, keepdims=True)
                     + jnp.sum(t.dbk * t.k * t.gam, axis=-1, keepdims=True) + jnp.sum(t.dbv * t.vv, axis=-1, keepdims=True))
            t.dbl = dbeta * t.beta * (1.0 - t.beta)
            dv_ref[t.rows, t.cs] = t.dbv * t.beta
            bk = (t.beta * t.gam) * t.k
            zc = jnp.sum(t.dkd * t.kd, axis=-1, keepdims=True)
            xs = t.dkk * t.kk + t.dp * t.p
            dgc = (jnp.sum(xs, axis=-1, keepdims=True) - jnp.sum(xs.T, axis=-1, keepdims=True)
                   + jnp.sum(t.dbk * bk, axis=-1, keepdims=True) + jnp.sum(t.dqg * t.qg, axis=-1, keepdims=True) - zc)
            dglast = jnp.sum(zc, axis=0, keepdims=True) + t.dgl * t.gl[:, 0:1]
            t.dgc = dgc + jnp.where(rix == c - 1, dglast, 0.0)
        dgall = _dot(upper, _pairs_to_lanes([t.dgc for t in ts]), HIGHEST)
        for p, t in enumerate(ts):
            t.dg = dgall[:, p:p + 1]
        dbd_tiles = [jnp.zeros((c, LANES), F32) for _ in range(GDN_CB)]
        for t in ts:
            ddl = t.dg * (-t.ea) * _sigmoid(t.sp_arg)
            acc_a = acc_a + jnp.where(lane == t.h, jnp.sum(t.dg * t.g, axis=0, keepdims=True), 0.0)
            acc_d = acc_d + jnp.where(lane == t.h, jnp.sum(ddl, axis=0, keepdims=True), 0.0)
            dbd_tiles[t.cc] = (dbd_tiles[t.cc] + jnp.where(lane == t.h, t.dbl, 0.0)
                               + jnp.where(lane == GDN_HEADS + t.h, ddl, 0.0))
            dqn_ref[t.rows, t.cs] = t.dq * (GDN_DK ** -0.5)
            dkn_ref[t.rows, t.cs] = t.dk
        for cc in range(GDN_CB):
            dzb_ref[cc * c:(cc + 1) * c, KEY_A:KEY_A + LANES] = dbd_tiles[cc].astype(BF16)
        acc_ref[0:1, :] += acc_a
        acc_ref[1:2, :] += acc_d
        acc_ref[2:3, :] += acc_g

    def rev(i):
        return nb - 1 - i

    row = pl.BlockSpec((r_, KEY_A), lambda i: (rev(i), 0))
    st = pl.BlockSpec((GDN_CB, GDN_HEADS, GDN_DK, GDN_DK), lambda i: (rev(i), 0, 0, 0))
    tt_spec = pl.BlockSpec((GDN_CB, GDN_HEADS, c, c), lambda i: (rev(i), 0, 0, 0))
    return pl.pallas_call(
        body, name="gdn_bwd", grid=(nb,),
        in_specs=[row, row, row, pl.BlockSpec((r_, LANES), lambda i: (rev(i), CB_BD)),
                  pl.BlockSpec((r_, PB), lambda i: (rev(i), CB_ZA)),
                  pl.BlockSpec((SUBLANES, LANES), lambda i: (0, 0)), pl.BlockSpec((1, GDN_DK), lambda i: (0, 0)),
                  row, st, row, row, tt_spec, row, pl.BlockSpec(memory_space=pl.ANY)],
        out_specs=[row, row, row, _dp_spec(r_, DP_ZBD, rev), pl.BlockSpec((SUBLANES, LANES), lambda i: (0, 0))],
        out_shape=[jax.ShapeDtypeStruct((t, KEY_A), F32)] * 3 + [jax.ShapeDtypeStruct(dproj.shape, dproj.dtype),
                                                                jax.ShapeDtypeStruct((SUBLANES, LANES), F32)],
        input_output_aliases={13: 3},
        scratch_shapes=[pltpu.VMEM((GDN_HEADS, GDN_DK, GDN_DK), F32)],
        compiler_params=_cparams("arbitrary"),
    )(qn, kn, v, proj, proj, par, gnw, o, sprev, wst, ust, tst, d_oan, dproj)


def _merge_fwd(oan, ob, proj, x, wba, wbb, wout, tm=512):
    t = x.shape[0]

    def body(oa_ref, ob_ref, ga_ref, gb_ref, x_ref, wba_ref, wbb_ref, wout_ref, x2_ref):
        ya = _dot(oa_ref[...], wba_ref[...])
        yb = _dot(ob_ref[...], wbb_ref[...])
        mix = _sigmoid(ga_ref[...]) * ya + _sigmoid(gb_ref[...]) * yb
        x2_ref[...] = x_ref[...] + _dot(mix.astype(BF16), wout_ref[...])

    half = pl.BlockSpec((tm, KEY_A), lambda i: (i, 0))
    row = pl.BlockSpec((tm, D_MODEL), lambda i: (i, 0))
    wsmall = pl.BlockSpec((KEY_A, D_MODEL), lambda i: (0, 0))
    return pl.pallas_call(
        body, name="merge_fwd", grid=(t // tm,),
        in_specs=[half, half, pl.BlockSpec((tm, D_MODEL), lambda i: (i, CB_GA)),
                  pl.BlockSpec((tm, D_MODEL), lambda i: (i, CB_GB)), row, wsmall, wsmall,
                  pl.BlockSpec((D_MODEL, D_MODEL), lambda i: (0, 0))],
        out_specs=row,
        out_shape=jax.ShapeDtypeStruct((t, D_MODEL), F32),
        compiler_params=_cparams("parallel"),
    )(oan, ob, proj, proj, x, wba, wbb, wout)


def _merge_bwd(dx2b, oan, ob, proj, wba, wbb, wout_t, wba_t, wbb_t, tm=512):
    t = dx2b.shape[0]

    def body(dx_ref, oa_ref, ob_ref, ga_ref, gb_ref, wba_ref, wbb_ref, woutt_ref, wbat_ref, wbbt_ref,
             dg_ref, doa_ref, dob_ref, gout_ref, gba_ref, gbb_ref):
        @pl.when(pl.program_id(0) == 0)
        def _():
            gout_ref[...] = jnp.zeros_like(gout_ref)
            gba_ref[...] = jnp.zeros_like(gba_ref)
            gbb_ref[...] = jnp.zeros_like(gbb_ref)

        dx, oa, ob = dx_ref[...], oa_ref[...], ob_ref[...]
        dmix = _dot(dx, woutt_ref[...])
        ya = _dot(oa, wba_ref[...])
        yb = _dot(ob, wbb_ref[...])
        sa, sb = _sigmoid(ga_ref[...]), _sigmoid(gb_ref[...])
        gout_ref[...] += _dot_tn((sa * ya + sb * yb).astype(BF16), dx)
        dg_ref[:, :D_MODEL] = (dmix * ya * sa * (1.0 - sa)).astype(BF16)
        dg_ref[:, D_MODEL:] = (dmix * yb * sb * (1.0 - sb)).astype(BF16)
        dya = (dmix * sa).astype(BF16)
        dyb = (dmix * sb).astype(BF16)
        gba_ref[...] += _dot_tn(oa, dya)
        gbb_ref[...] += _dot_tn(ob, dyb)
        doa_ref[...] = _dot(dya, wbat_ref[...])
        dob_ref[...] = _dot(dyb, wbbt_ref[...])

    half = pl.BlockSpec((tm, KEY_A), lambda i: (i, 0))
    row = pl.BlockSpec((tm, D_MODEL), lambda i: (i, 0))
    wsmall = pl.BlockSpec((KEY_A, D_MODEL), lambda i: (0, 0))
    wsmall_t = pl.BlockSpec((D_MODEL, KEY_A), lambda i: (0, 0))
    wfull = pl.BlockSpec((D_MODEL, D_MODEL), lambda i: (0, 0))
    return pl.pallas_call(
        body, name="merge_bwd", grid=(t // tm,),
        in_specs=[row, half, half, pl.BlockSpec((tm, D_MODEL), lambda i: (i, CB_GA)),
                  pl.BlockSpec((tm, D_MODEL), lambda i: (i, CB_GB)), wsmall, wsmall, wfull, wsmall_t, wsmall_t],
        out_specs=[_dp_spec(tm, DP_GATES), half, half, wfull, wsmall, wsmall],
        out_shape=[jax.ShapeDtypeStruct((t, PROJ_W), BF16), jax.ShapeDtypeStruct((t, KEY_A), F32),
                   jax.ShapeDtypeStruct((t, KEY_A), F32), jax.ShapeDtypeStruct((D_MODEL, D_MODEL), F32),
                   jax.ShapeDtypeStruct((KEY_A, D_MODEL), F32), jax.ShapeDtypeStruct((WIDTH_B, D_MODEL), F32)],
        compiler_params=_cparams("arbitrary"),
    )(dx2b, oan, ob, proj, proj, wba, wbb, wout_t, wba_t, wbb_t)


FFN_TM = 128
FFN_W = 2 * D_FF


def _ffn_conv(up_ref, halo_ref, cw_ref, cb_ref, first):
    ext = jnp.concatenate([jnp.where(first, 0.0, halo_ref[...]), up_ref[...]], axis=0)
    taps = _conv_taps(ext, FFN_CONV, HALO, FFN_TM)
    cw = cw_ref[...]
    u = sum(cw[k:k + 1, :] * taps[k] for k in range(FFN_CONV)) + cb_ref[...]
    return u, taps


def _ffn_tail(up, cw, cb, wdown, x2, tgt, w3):
    t = x2.shape[0]
    tm = FFN_TM

    def body(up_ref, halo_ref, cw_ref, cb_ref, wd_ref, x2_ref, tgt_ref, w3_ref, dx_ref, dxb_ref, act_ref, acc_ref):
        first = pl.program_id(0) == 0

        @pl.when(first)
        def _():
            acc_ref[...] = jnp.zeros_like(acc_ref)

        u, _ = _ffn_conv(up_ref, halo_ref, cw_ref, cb_ref, first)
        gate, upp = u[:, :D_FF], u[:, D_FF:]
        act = (gate * _sigmoid(gate) * upp).astype(BF16)
        act_ref[...] = act
        x3 = x2_ref[...] + _dot(act, wd_ref[...])
        r = lax.rsqrt(jnp.mean(x3 * x3, axis=-1, keepdims=True) + EPS)
        xh = x3 * r
        w3v = w3_ref[...]
        err = xh * w3v - tgt_ref[...]
        loss = 0.5 * jnp.sum(jnp.mean(err * err, axis=-1, keepdims=True), axis=0, keepdims=True)
        dy = err * (1.0 / D_MODEL)
        acc_ref[0:1, :] += jnp.sum(dy * xh, axis=0, keepdims=True)
        acc_ref[1:2, :] += jnp.broadcast_to(loss, (1, D_MODEL))
        dxh = dy * w3v
        dx = r * (dxh - xh * jnp.mean(dxh * xh, axis=-1, keepdims=True))
        dx_ref[...] = dx
        dxb_ref[...] = dx.astype(BF16)

    row = pl.BlockSpec((tm, D_MODEL), lambda i: (i, 0))
    vec = pl.BlockSpec((1, D_MODEL), lambda i: (0, 0))
    return pl.pallas_call(
        body, name="ffn_tail", grid=(t // tm,),
        in_specs=[pl.BlockSpec((tm, FFN_W), lambda i: (i, 0)), _prev_halo_spec(tm, FFN_W, 0),
                  pl.BlockSpec((SUBLANES, FFN_W), lambda i: (0, 0)), pl.BlockSpec((1, FFN_W), lambda i: (0, 0)),
                  pl.BlockSpec((D_FF, D_MODEL), lambda i: (0, 0)), row, row, vec],
        out_specs=[row, row, pl.BlockSpec((tm, D_FF), lambda i: (i, 0)), pl.BlockSpec((SUBLANES, D_MODEL), lambda i: (0, 0))],
        out_shape=[jax.ShapeDtypeStruct((t, D_MODEL), F32), jax.ShapeDtypeStruct((t, D_MODEL), BF16),
                   jax.ShapeDtypeStruct((t, D_FF), BF16), jax.ShapeDtypeStruct((SUBLANES, D_MODEL), F32)],
        compiler_params=_cparams("arbitrary"),
    )(up, up, cw, cb, wdown, x2, tgt, w3)


def _ffn_bwd(dx3b, wdown_t, up, cw, cb):
    t = up.shape[0]
    tm = FFN_TM
    nt = t // tm
    n_ext = tm + HALO

    def rev(i):
        return nt - 1 - i

    def body(dx_ref, wdt_ref, up_ref, halo_ref, cw_ref, cb_ref, dup_ref, acc_ref, nxt_ref):
        i = pl.program_id(0)

        @pl.when(i == 0)
        def _():
            acc_ref[...] = jnp.zeros_like(acc_ref)
            nxt_ref[...] = jnp.zeros_like(nxt_ref)

        dact = _dot(dx_ref[...], wdt_ref[...])
        u, taps = _ffn_conv(up_ref, halo_ref, cw_ref, cb_ref, i == nt - 1)
        gate, upp = u[:, :D_FF], u[:, D_FF:]
        sg = _sigmoid(gate)
        du = jnp.concatenate([dact * upp * (sg * (1.0 + gate * (1.0 - sg))), dact * (gate * sg)], axis=1)
        for k in range(FFN_CONV):
            acc_ref[k:k + 1, :] += jnp.sum(du * taps[k], axis=0, keepdims=True)
        acc_ref[FFN_CONV:FFN_CONV + 1, :] += jnp.sum(du, axis=0, keepdims=True)
        ext = jnp.concatenate([du, nxt_ref[...]], axis=0)
        cw_v = cw_ref[...]
        dup = cw_v[FFN_CONV - 1:FFN_CONV, :] * du
        for k in range(FFN_CONV - 1):
            dup = dup + cw_v[k:k + 1, :] * pltpu.roll(ext, n_ext - (FFN_CONV - 1 - k), 0)[:tm]
        dup_ref[...] = dup.astype(BF16)
        nxt_ref[...] = du[0:HALO]

    wide = pl.BlockSpec((tm, FFN_W), lambda i: (rev(i), 0))
    return pl.pallas_call(
        body, name="ffn_bwd", grid=(nt,),
        in_specs=[pl.BlockSpec((tm, D_MODEL), lambda i: (rev(i), 0)), pl.BlockSpec((D_MODEL, D_FF), lambda i: (0, 0)),
                  wide, pl.BlockSpec((HALO, FFN_W), lambda i: (jnp.maximum(rev(i) * (tm // HALO) - 1, 0), 0)),
                  pl.BlockSpec((SUBLANES, FFN_W), lambda i: (0, 0)), pl.BlockSpec((1, FFN_W), lambda i: (0, 0))],
        out_specs=[wide, pl.BlockSpec((SUBLANES, FFN_W), lambda i: (0, 0))],
        out_shape=[jax.ShapeDtypeStruct((t, FFN_W), BF16), jax.ShapeDtypeStruct((SUBLANES, FFN_W), F32)],
        scratch_shapes=[pltpu.VMEM((HALO, FFN_W), F32)],
        compiler_params=_cparams("arbitrary"),
    )(dx3b, wdown_t, up, up, cw, cb)


def _adamw(parts, w, m, v, name, tr):
    r, cols = w.shape

    def body(p_ref, w_ref, m_ref, v_ref, g_ref, d_ref, mo_ref, vo_ref):
        g = p_ref[0].astype(F32)
        for s in range(1, N_DEV):
            g = g + p_ref[s].astype(F32)
        mm = ADAM_B1 * m_ref[...] + (1.0 - ADAM_B1) * g
        vv = ADAM_B2 * v_ref[...] + (1.0 - ADAM_B2) * (g * g)
        m_hat = mm / (1.0 - ADAM_B1 ** ADAM_STEP)
        v_hat = vv / (1.0 - ADAM_B2 ** ADAM_STEP)
        g_ref[...] = g
        d_ref[...] = -ADAM_LR * (m_hat / (jnp.sqrt(v_hat) + ADAM_EPS) + ADAM_WD * w_ref[...])
        mo_ref[...] = mm
        vo_ref[...] = vv

    assert r % tr == 0
    row = pl.BlockSpec((tr, cols), lambda i: (i, 0))
    return pl.pallas_call(
        body, name=name, grid=(r // tr,),
        in_specs=[pl.BlockSpec((N_DEV, tr, cols), lambda i: (0, i, 0)), row, row, row],
        out_specs=[row, row, row, row],
        out_shape=[jax.ShapeDtypeStruct((r, cols), F32)] * 4,
        compiler_params=_cparams("parallel"),
    )(parts, w, m, v)


def _mesh_pos():
    return lax.axis_index("x"), lax.axis_index("y"), lax.axis_index("c")


def _peer(pos, k):
    x, y, c = pos
    return (x ^ ((k >> 2) & 1), y ^ ((k >> 1) & 1), c ^ (k & 1))


def _flat_id(pos):
    return 4 * pos[0] + 2 * pos[1] + pos[2]


def _exchange_copies(srcs, dsts, scatter, send_sems, recv_sems, loc_sems):
    pos = _mesh_pos()
    me = _flat_id(pos)
    local, remote = [], []
    for j, (src, dst) in enumerate(zip(srcs, dsts)):
        local.append(pltpu.make_async_copy(src.at[me] if scatter[j] else src, dst.at[me], loc_sems.at[j]))
        for k in range(1, N_DEV):
            to = _peer(pos, k)
            remote.append(pltpu.make_async_remote_copy(
                src_ref=src.at[_flat_id(to)] if scatter[j] else src, dst_ref=dst.at[me],
                send_sem=send_sems.at[j, k - 1], recv_sem=recv_sems.at[j, k - 1],
                device_id=to, device_id_type=pl.DeviceIdType.MESH))
    return local, remote


def _exchange_shapes(arrays, scatter):
    return [jax.ShapeDtypeStruct(a.shape if s else (N_DEV,) + a.shape, a.dtype) for a, s in zip(arrays, scatter)]


def _exchange_sems(n):
    return [pltpu.SemaphoreType.DMA((n, N_DEV - 1)), pltpu.SemaphoreType.DMA((n, N_DEV - 1)), pltpu.SemaphoreType.DMA((n,))]


def _exchange(arrays, scatter, name):
    n = len(arrays)
    any_spec = pl.BlockSpec(memory_space=pl.ANY)

    def body(*refs):
        local, remote = _exchange_copies(refs[:n], refs[n:2 * n], scatter, *refs[2 * n:])
        for cp in local + remote:
            cp.start()
        for cp in remote:
            cp.wait()
        for cp in local:
            cp.wait()

    return pl.pallas_call(
        body, name=name, in_specs=[any_spec] * n, out_specs=[any_spec] * n,
        out_shape=_exchange_shapes(arrays, scatter), scratch_shapes=_exchange_sems(n),
    )(*arrays)


def _pad_rows(a, rows):
    return jnp.pad(a, ((0, rows - a.shape[0]),) + ((0, 0),) * (a.ndim - 1))


PACK_UNIT = SUBLANES * LANES


def _pack_lanes(parts, rows):
    out = []
    for a in parts:
        f = a.reshape(-1)
        out.append(jnp.pad(f, (0, (-f.shape[0]) % PACK_UNIT)).reshape(-1, LANES))
    packed = jnp.concatenate(out, axis=0)
    assert packed.shape[0] == rows, (packed.shape, rows)
    return packed


def _unpack_lanes(buf, shapes):
    out, r0 = [], 0
    for shp in shapes:
        n = math.prod(shp)
        nr = -(-n // PACK_UNIT) * SUBLANES
        out.append(buf[r0:r0 + nr].reshape(-1)[:n].reshape(shp))
        r0 += nr
    return out


def _col_shards(g):
    r, n = g.shape
    return g.reshape(r, N_DEV, n // N_DEV).transpose(1, 0, 2)


def _col_unshard(s):
    _, r, w = s.shape
    return s.transpose(1, 0, 2).reshape(r, N_DEV * w)


def _lane_rows(flat):
    n = flat.shape[1]
    return jnp.pad(flat, ((0, 0), (0, (-n) % PACK_UNIT))).reshape(N_DEV, -1, LANES)


SMALL_ROWS = 128
WS_ROWS = 32


def kernel(x, norm_mix_w, w_in, conv_qkv_w, a_log, dt_bias, gdn_norm_w, w_branch_a, w_branch_b, rel_bias, w_out, norm_ffn_w, w_up, conv_ffn_w, conv_ffn_b, w_down, norm_final_w, loss_target, m_norm_mix_w, m_w_in, m_conv_qkv_w, m_a_log, m_dt_bias, m_gdn_norm_w, m_w_branch_a, m_w_branch_b, m_rel_bias, m_w_out, m_norm_ffn_w, m_w_up, m_conv_ffn_w, m_conv_ffn_b, m_w_down, m_norm_final_w, v_norm_mix_w, v_w_in, v_conv_qkv_w, v_a_log, v_dt_bias, v_gdn_norm_w, v_w_branch_a, v_w_branch_b, v_rel_bias, v_w_out, v_norm_ffn_w, v_w_up, v_conv_ffn_w, v_conv_ffn_b, v_w_down, v_norm_final_w):
    big_w = (w_in, w_branch_a, w_branch_b, w_out, w_up, w_down, conv_qkv_w, conv_ffn_w)
    big_m = (m_w_in, m_w_branch_a, m_w_branch_b, m_w_out, m_w_up, m_w_down, m_conv_qkv_w, m_conv_ffn_w)
    big_v = (v_w_in, v_w_branch_a, v_w_branch_b, v_w_out, v_w_up, v_w_down, v_conv_qkv_w, v_conv_ffn_w)
    small_w = (norm_mix_w, a_log, dt_bias, gdn_norm_w, rel_bias, norm_ffn_w, conv_ffn_b, norm_final_w)
    small_m = (m_norm_mix_w, m_a_log, m_dt_bias, m_gdn_norm_w, m_rel_bias, m_norm_ffn_w, m_conv_ffn_b, m_norm_final_w)
    small_v = (v_norm_mix_w, v_a_log, v_dt_bias, v_gdn_norm_w, v_rel_bias, v_norm_ffn_w, v_conv_ffn_b, v_norm_final_w)

    xs, tgt = x[0], loss_target[0]
    ws = _pack_lanes(big_w[6:], WS_ROWS)
    h1, g_in, gs = _rmsnorm_cast(xs, norm_mix_w, "norm_mix", carry=([w_in[0].astype(BF16), ws], (False, False)))
    win = _col_unshard(g_in)
    gs = gs.reshape(N_DEV, -1)
    cqkv = gs[:, :GDN_CONV * 192].reshape(N_DEV, GDN_CONV, 192).transpose(1, 0, 2).reshape(GDN_CONV, 3 * KEY_A)
    cffn = gs[:, PACK_UNIT:PACK_UNIT + FFN_CONV * 704].reshape(N_DEV, FFN_CONV, 704).transpose(1, 0, 2).reshape(FFN_CONV, FFN_W)
    cffn = _pad_rows(cffn, SUBLANES)
    w_all = jnp.concatenate([win[:, a:b] for a, b in W_IN_ORDER] + [jnp.zeros((D_MODEL, PROJ_W - D_IN), BF16)], axis=1)
    par = _pad_rows(jnp.pad(jnp.concatenate([a_log, dt_bias], axis=0), ((0, 0), (0, LANES - GDN_HEADS))), SUBLANES)
    table = jnp.pad(rel_bias[0], ((0, 0), (0, 3 * LANES - rel_bias.shape[-1]))).reshape(ATT_HEADS, 1, 3 * LANES)

    proj, g_ba, g_bb, g_out, g_up, g_down = _mm_nn(
        h1, w_all, F32, "in_proj", 2 * MM_TM, 1152, D_MODEL, carry=([w[0].astype(BF16) for w in big_w[1:6]], (False,) * 5))
    wba, wbb, wup = _col_unshard(g_ba), _col_unshard(g_bb), _col_unshard(g_up)
    wout = g_out.reshape(D_MODEL, D_MODEL)
    wdown = g_down.reshape(D_FF, D_MODEL)
    qn, kn, va = _gdn_prep_fwd(proj, cqkv)
    oan, o_gdn, sprev, wst, ust, tst = _gdn_fwd(qn, kn, va, proj, par, gdn_norm_w)
    bias_q, bias_k = _att_bias(table)
    ob, lse, lse_t = _att_fwd(proj, bias_q)
    x2 = _merge_fwd(oan, ob, proj, xs, wba, wbb, wout)
    h2 = _rmsnorm_cast(x2, norm_ffn_w, "norm_ffn")
    up = _mm_nn(h2, wup, F32, "ffn_up", 2 * MM_TM, 1408, D_MODEL)
    dx3, dx3b, act, tail_sums = _ffn_tail(up, cffn, conv_ffn_b, wdown, x2, tgt, norm_final_w.reshape(1, D_MODEL))

    g_wdown = _mm_tn(act, dx3b, "dw_down", 512)
    dup, ffn_sums = _ffn_bwd(dx3b, wdown.T, up, cffn, conv_ffn_b)
    g_wup = _mm_tn(h2, dup, "dw_up", 1408)
    dx2, dx2b, nffn_sums, r_up, r_down = _mm_rms_bwd(dup, wup.T, x2, norm_ffn_w, dx3, "ffn_up_bwd", MM_TM, 1408, carry=(
        [_col_shards(g_wup).astype(BF16), g_wdown.reshape(N_DEV, -1, D_MODEL).astype(BF16)], (True, True)))
    dproj, d_oan, d_ob, g_wout, g_wba, g_wbb = _merge_bwd(dx2b, oan, ob, proj, wba, wbb, wout.T, wba.T, wbb.T)
    dproj, dlt_t, slabs = _att_dq(proj, bias_q, lse, d_ob, dproj)
    dproj = _att_dkv(proj, bias_k, lse_t, dlt_t, d_ob, dproj)
    g_rel = _relbias_grad(slabs)[:, 0, :rel_bias.shape[-1]]
    dqn, dkn, dva, dproj, gdn_sums = _gdn_bwd(qn, kn, va, proj, par, gdn_norm_w, o_gdn, sprev, wst, ust, tst, d_oan, dproj)
    dproj, cq_sums = _gdn_prep_bwd(proj, cqkv, dqn, dkn, dva, dproj)
    g_wall = _mm_tn(h1, dproj, "dw_in", 1152)
    starts = np.cumsum([0] + [b - a for a, b in W_IN_ORDER])
    g_win = jnp.concatenate([g_wall[:, starts[i]:starts[i + 1]] for i in np.argsort([a for a, _ in W_IN_ORDER])], axis=1)
    g_conv = jnp.concatenate([_lane_rows(_col_shards(cq_sums[:GDN_CONV]).reshape(N_DEV, -1)),
                              _lane_rows(_col_shards(ffn_sums[:FFN_CONV]).reshape(N_DEV, -1))], axis=1)
    grad_x, _, nmix_sums, r_in, r_ba, r_bb, r_out, r_conv = _mm_rms_bwd(
        dproj, w_all.T, xs, norm_mix_w, dx2, "in_proj_bwd", MM_TM, 1152, carry=(
            [_col_shards(g_win).astype(BF16), _col_shards(g_wba).astype(BF16), _col_shards(g_wbb).astype(BF16),
             g_wout.reshape(N_DEV, -1, D_MODEL).astype(BF16), g_conv], (True,) * 5))

    small_g = (nmix_sums[0:1], gdn_sums[0:1, :GDN_HEADS], gdn_sums[1:2, :GDN_HEADS], gdn_sums[2:3], g_rel,
               nffn_sums[0:1], ffn_sums[FFN_CONV:FFN_CONV + 1], tail_sums[0:1], tail_sums[1:2, 0:1])
    r_small, = _exchange([_pack_lanes(small_g, SMALL_ROWS)], (False,), "all_gather_small_grads")
    recv = (r_in, r_ba, r_bb, r_out, r_up, r_down, r_conv, r_small)

    res = {}
    for i, (nm, tr) in enumerate((("w_in", 128), ("w_branch_a", KEY_A), ("w_branch_b", WIDTH_B), ("w_out", 128),
                                  ("w_up", 128), ("w_down", 176))):
        res[nm] = [o[None] for o in _adamw(recv[i], big_w[i][0], big_m[i][0], big_v[i][0], "adamw_" + nm, tr)]
    conv = _adamw(recv[6], _pack_lanes(big_w[6:], WS_ROWS), _pack_lanes(big_m[6:], WS_ROWS), _pack_lanes(big_v[6:], WS_ROWS),
                  "adamw_conv", WS_ROWS)
    conv = [_unpack_lanes(o, [w.shape for w in big_w[6:]]) for o in conv]
    res["conv_qkv_w"] = [o[0] for o in conv]
    res["conv_ffn_w"] = [o[1] for o in conv]
    small_shapes = [w.shape for w in small_w]
    zero = jnp.zeros((1,), F32)
    small = _adamw(recv[7], _pack_lanes(small_w + (zero,), SMALL_ROWS), _pack_lanes(small_m + (zero,), SMALL_ROWS),
                   _pack_lanes(small_v + (zero,), SMALL_ROWS), "adamw_replicated", SMALL_ROWS)
    small = [_unpack_lanes(o, small_shapes + [()]) for o in small]
    loss = small[0][-1]
    for j, nm in enumerate(("norm_mix_w", "a_log", "dt_bias", "gdn_norm_w", "rel_bias", "norm_ffn_w", "conv_ffn_b",
                            "norm_final_w")):
        res[nm] = [o[j] for o in small]

    names = ("norm_mix_w", "w_in", "conv_qkv_w", "a_log", "dt_bias", "gdn_norm_w", "w_branch_a", "w_branch_b", "rel_bias",
             "w_out", "norm_ffn_w", "w_up", "conv_ffn_w", "conv_ffn_b", "w_down", "norm_final_w")
    outs = [res[n][kind] for kind in range(4) for n in names]
    return (loss, grad_x[None], *outs)
```

```python
import functools
import math

import numpy as np
import jax
import jax.numpy as jnp
from jax import lax
from jax.experimental import pallas as pl
from jax.experimental.pallas import tpu as pltpu

F32, BF16 = jnp.float32, jnp.bfloat16
HIGHEST = lax.Precision.HIGHEST

N_DEV = 8
D_MODEL = 1024
CHUNK = 64
EPS = 1e-6
GDN_HEADS, GDN_DK = 4, 128
KEY_A = GDN_HEADS * GDN_DK
GDN_CONV = 4
ATT_HEADS, ATT_DH = 8, 64
WIDTH_B = ATT_HEADS * ATT_DH
ATT_BAND = 9
REL_CLIP = 128
D_FF = 2816
FFN_CONV = 3
D_IN = 5640
ADAM_LR, ADAM_B1, ADAM_B2, ADAM_EPS, ADAM_WD, ADAM_STEP = 0.001, 0.9, 0.999, 1e-08, 0.01, 10

LANES = 128
SUBLANES = 8
NEG = -1e30

PROJ_W = 5760
PB = 512
CB_GA, CB_GB = 0, 1
CB_KB, CB_VB, CB_QA, CB_KA, CB_VA, CB_QB, CB_ZA = 4, 5, 6, 7, 8, 9, 10
CB_BD = 44
DP_GATES, DP_KVB, DP_QKVA, DP_QB, DP_ZBD = (2048, 0), (1024, 2), (1536, 2), (512, 9), (640, 8)
W_IN_ORDER = ((3592, 5640), (2568, 3592), (0, 1536), (2056, 2568), (1536, 2048), (2048, 2056))

ATT_QB = 256
ATT_KW = 768
ATT_VEC = 1024


def _dot(a, b, precision=None):
    return jnp.dot(a, b, preferred_element_type=F32, precision=precision)


def _dot_nt(a, b, precision=None):
    return lax.dot_general(a, b, (((1,), (1,)), ((), ())), preferred_element_type=F32, precision=precision)


def _dot_tn(a, b):
    return lax.dot_general(a, b, (((0,), (0,)), ((), ())), preferred_element_type=F32)


def _split(a):
    hi = a.astype(BF16)
    return hi, (a - hi.astype(F32)).astype(BF16)


def _dot3s(a, b):
    return _dot(a[0], b[0]) + (_dot(a[0], b[1]) + _dot(a[1], b[0]))


def _sigmoid(x):
    return 0.5 * jnp.tanh(0.5 * x) + 0.5


def _softplus(x):
    return jnp.maximum(x, 0.0) + jnp.log(1.0 + jnp.exp(-jnp.abs(x)))


def _cparams(*sem):
    return pltpu.CompilerParams(dimension_semantics=tuple(sem))


def _dp_spec(tm, region, index=lambda i: i):
    width, cb = region
    return pl.BlockSpec((tm, width), lambda i: (index(i), cb))


def _rmsnorm_cast(x, w, name, tm=512, carry=((), ())):
    t, d = x.shape
    nt = t // tm
    arrays, scatter = carry
    nx = len(arrays)

    def body(*refs):
        x_ref, w_ref = refs[:2]
        o_ref = refs[2 + nx]
        i = pl.program_id(0)
        if nx:
            local, remote = _exchange_copies(refs[2:2 + nx], refs[3 + nx:3 + 2 * nx], scatter, *refs[3 + 2 * nx:])

            @pl.when(i == 0)
            def _():
                for cp in local + remote:
                    cp.start()

        xv = x_ref[...]
        r = lax.rsqrt(jnp.mean(xv * xv, axis=-1, keepdims=True) + EPS)
        o_ref[...] = (xv * r * w_ref[...]).astype(BF16)

        if nx:
            @pl.when(i == nt - 1)
            def _():
                for cp in remote + local:
                    cp.wait()

    any_spec = pl.BlockSpec(memory_space=pl.ANY)
    out = pl.pallas_call(
        body, name=name, grid=(nt,),
        in_specs=[pl.BlockSpec((tm, d), lambda i: (i, 0)), pl.BlockSpec((1, d), lambda i: (0, 0))] + [any_spec] * nx,
        out_specs=[pl.BlockSpec((tm, d), lambda i: (i, 0))] + [any_spec] * nx,
        out_shape=[jax.ShapeDtypeStruct((t, d), BF16)] + _exchange_shapes(arrays, scatter),
        scratch_shapes=_exchange_sems(nx) if nx else [],
        compiler_params=_cparams("arbitrary" if nx else "parallel"),
    )(x, w, *arrays)
    return out if nx else out[0]


def _mm_nn(a, b, out_dtype, name, tm, tn, tk, carry=((), ())):
    m, k = a.shape
    _, n = b.shape
    tm = min(tm, m)
    nk = k // tk
    assert m % tm == 0 and n % tn == 0 and k % tk == 0
    arrays, scatter = carry
    nx = len(arrays)
    gm, gn = m // tm, n // tn

    def body(*refs):
        a_ref, b_ref = refs[:2]
        srcs = refs[2:2 + nx]
        o_ref = refs[2 + nx]
        dsts = refs[3 + nx:3 + 2 * nx]
        rest = refs[3 + 2 * nx:]
        i, j, kk = pl.program_id(0), pl.program_id(1), pl.program_id(2)
        if nx:
            local, remote = _exchange_copies(srcs, dsts, scatter, *rest[-3:])

            @pl.when((i == 0) & (j == 0) & (kk == 0))
            def _():
                for cp in local + remote:
                    cp.start()

        if nk == 1:
            o_ref[...] = _dot(a_ref[...], b_ref[...]).astype(out_dtype)
        else:
            acc_ref = rest[0]

            @pl.when(kk == 0)
            def _():
                acc_ref[...] = jnp.zeros_like(acc_ref)

            acc_ref[...] += _dot(a_ref[...], b_ref[...])

            @pl.when(kk == nk - 1)
            def _():
                o_ref[...] = acc_ref[...].astype(out_dtype)

        if nx:
            @pl.when((i == gm - 1) & (j == gn - 1) & (kk == nk - 1))
            def _():
                for cp in remote + local:
                    cp.wait()

    any_spec = pl.BlockSpec(memory_space=pl.ANY)
    scratch = ([pltpu.VMEM((tm, tn), F32)] if nk > 1 else []) + (_exchange_sems(nx) if nx else [])
    out = pl.pallas_call(
        body, name=name, grid=(gm, gn, nk),
        in_specs=[pl.BlockSpec((tm, tk), lambda i, j, kk: (i, kk)),
                  pl.BlockSpec((tk, tn), lambda i, j, kk: (kk, j))] + [any_spec] * nx,
        out_specs=[pl.BlockSpec((tm, tn), lambda i, j, kk: (i, j))] + [any_spec] * nx,
        out_shape=[jax.ShapeDtypeStruct((m, n), out_dtype)] + _exchange_shapes(arrays, scatter),
        scratch_shapes=scratch,
        compiler_params=_cparams(*(("arbitrary",) * 3 if nx else ("parallel", "parallel", "arbitrary"))),
    )(a, b, *arrays)
    return out if nx else out[0]


def _mm_rms_bwd(a, b, x, w, dres, name, tm, tk, carry):
    m, k = a.shape
    _, n = b.shape
    nk = k // tk
    gm = m // tm
    assert m % tm == 0 and k % tk == 0
    arrays, scatter = carry
    nx = len(arrays)

    def body(*refs):
        a_ref, b_ref, x_ref, w_ref, dres_ref = refs[:5]
        srcs = refs[5:5 + nx]
        dx_ref, dxb_ref, dw_ref = refs[5 + nx:8 + nx]
        dsts = refs[8 + nx:8 + 2 * nx]
        acc_ref = refs[8 + 2 * nx]
        i, kk = pl.program_id(0), pl.program_id(1)
        local, remote = _exchange_copies(srcs, dsts, scatter, *refs[9 + 2 * nx:])

        @pl.when((i == 0) & (kk == 0))
        def _():
            for cp in local + remote:
                cp.start()
            dw_ref[...] = jnp.zeros_like(dw_ref)

        @pl.when(kk == 0)
        def _():
            acc_ref[...] = jnp.zeros_like(acc_ref)

        acc_ref[...] += _dot(a_ref[...], b_ref[...])

        @pl.when(kk == nk - 1)
        def _():
            dhv = acc_ref[...]
            xv = x_ref[...]
            r = lax.rsqrt(jnp.mean(xv * xv, axis=-1, keepdims=True) + EPS)
            xh = xv * r
            dw_ref[0:1, :] += jnp.sum(dhv * xh, axis=0, keepdims=True)
            dxh = dhv * w_ref[...]
            dx = dres_ref[...] + r * (dxh - xh * jnp.mean(dxh * xh, axis=-1, keepdims=True))
            dx_ref[...] = dx
            dxb_ref[...] = dx.astype(BF16)

        @pl.when((i == gm - 1) & (kk == nk - 1))
        def _():
            for cp in remote + local:
                cp.wait()

    any_spec = pl.BlockSpec(memory_space=pl.ANY)
    row = pl.BlockSpec((tm, n), lambda i, kk: (i, 0))
    return pl.pallas_call(
        body, name=name, grid=(gm, nk),
        in_specs=[pl.BlockSpec((tm, tk), lambda i, kk: (i, kk)), pl.BlockSpec((tk, n), lambda i, kk: (kk, 0)),
                  row, pl.BlockSpec((1, n), lambda i, kk: (0, 0)), row] + [any_spec] * nx,
        out_specs=[row, row, pl.BlockSpec((SUBLANES, n), lambda i, kk: (0, 0))] + [any_spec] * nx,
        out_shape=[jax.ShapeDtypeStruct((m, n), F32), jax.ShapeDtypeStruct((m, n), BF16),
                   jax.ShapeDtypeStruct((SUBLANES, n), F32)] + _exchange_shapes(arrays, scatter),
        scratch_shapes=[pltpu.VMEM((tm, n), F32)] + _exchange_sems(nx),
        compiler_params=_cparams("arbitrary", "arbitrary"),
    )(a, b, x, w, dres, *arrays)


MM_TM = 1024


def _mm_tn(a, b, name, tn, tk=2 * MM_TM):
    t, m = a.shape
    _, n = b.shape
    tk = min(tk, t)
    assert t % tk == 0 and n % tn == 0

    def body(a_ref, b_ref, o_ref):
        @pl.when(pl.program_id(1) == 0)
        def _():
            o_ref[...] = jnp.zeros_like(o_ref)

        o_ref[...] += _dot_tn(a_ref[...], b_ref[...])

    return pl.pallas_call(
        body, name=name, grid=(n // tn, t // tk),
        in_specs=[pl.BlockSpec((tk, m), lambda j, s: (s, 0)),
                  pl.BlockSpec((tk, tn), lambda j, s: (s, j))],
        out_specs=pl.BlockSpec((m, tn), lambda j, s: (0, j)),
        out_shape=jax.ShapeDtypeStruct((m, n), F32),
        compiler_params=_cparams("parallel", "arbitrary"),
    )(a, b)


def _rel_index(dist):
    return np.clip(dist, -REL_CLIP, REL_CLIP) + REL_CLIP


def _bias_onehots():
    tw = 3 * LANES
    m = np.arange(ATT_VEC)
    dq = np.where(m <= ATT_KW, 512 - m, 512 - (m - ATT_VEC))
    dk = np.where(m < ATT_KW, m, m - ATT_VEC)
    ohq = np.zeros((tw, ATT_VEC), np.float32)
    ohk = np.zeros((tw, ATT_VEC), np.float32)
    ohq[_rel_index(dq), m] = 1.0
    ohk[_rel_index(dk), m] = 1.0
    return ohq, ohk


def _att_bias(table_pad):
    ohq, ohk = _bias_onehots()
    nslab = ATT_QB // SUBLANES

    def body(t_ref, ohq_ref, ohk_ref, bq_ref, bk_ref):
        tv = jnp.broadcast_to(t_ref[...], (SUBLANES, 3 * LANES))
        lane = lax.broadcasted_iota(jnp.int32, (ATT_QB, ATT_KW), 1)
        row = lax.broadcasted_iota(jnp.int32, (ATT_QB, ATT_KW), 0) // CHUNK
        col = lane // CHUNK
        band = (col >= row) & (col <= row + ATT_BAND - 1)
        for which, (oh_ref, out_ref) in enumerate(((ohq_ref, bq_ref), (ohk_ref, bk_ref))):
            vec = _dot(tv, oh_ref[...], HIGHEST)[0:1, :]
            slab = jnp.concatenate([vec if b == 0 else pltpu.roll(vec, b, 1) for b in range(SUBLANES)], axis=0)
            rows = [slab if a == 0 else pltpu.roll(slab, SUBLANES * a, 1) for a in range(nslab)]
            full = jnp.concatenate(rows, axis=0)[:, :ATT_KW]
            for v in range(3):
                inside = (lane >= (2 - v) * ATT_QB) if which == 0 else (lane < (v + 1) * ATT_QB)
                out_ref[v] = jnp.where(band & inside, full, NEG)

    h = table_pad.shape[0]
    oh_spec = pl.BlockSpec((3 * LANES, ATT_VEC), lambda i: (0, 0))
    out_spec = pl.BlockSpec((3, None, ATT_QB, ATT_KW), lambda i: (0, i, 0, 0))
    return pl.pallas_call(
        body, name="att_bias", grid=(h,),
        in_specs=[pl.BlockSpec((None, 1, 3 * LANES), lambda i: (i, 0, 0)), oh_spec, oh_spec],
        out_specs=[out_spec, out_spec],
        out_shape=[jax.ShapeDtypeStruct((3, h, ATT_QB, ATT_KW), F32)] * 2,
        compiler_params=_cparams("parallel"),
    )(table_pad, jnp.asarray(ohq), jnp.asarray(ohk))


def _head_masks():
    lane = lax.broadcasted_iota(jnp.int32, (1, LANES), 1)
    return [lane < ATT_DH, lane >= ATT_DH]


def _att_fwd(proj, bias_q):
    t = proj.shape[0]
    nb = t // ATT_QB
    scale = ATT_DH ** -0.5

    def body(q_ref, k0_ref, k1_ref, k2_ref, v0_ref, v1_ref, v2_ref, b_ref, o_ref, lse_ref, lset_ref):
        i = pl.program_id(0)
        q = (q_ref[...] * scale).astype(BF16)
        kk = jnp.concatenate([k0_ref[...], k1_ref[...], k2_ref[...]], axis=0).astype(BF16)
        vv = jnp.concatenate([v0_ref[...], v1_ref[...], v2_ref[...]], axis=0).astype(BF16)
        lane = lax.broadcasted_iota(jnp.int32, (1, LANES), 1)
        masks = _head_masks()
        lse_cols = jnp.zeros((ATT_QB, LANES), F32)
        for p in range(ATT_HEADS // 2):
            cs = slice(p * LANES, (p + 1) * LANES)
            qt, kt, vt = q[:, cs], kk[:, cs], vv[:, cs]
            acc = jnp.zeros((ATT_QB, LANES), F32)
            for sub in range(2):
                h = 2 * p + sub
                s = _dot_nt(jnp.where(masks[sub], qt, 0), kt) + b_ref[h]
                mx = jnp.max(s, axis=-1, keepdims=True)
                e = jnp.exp(s - mx)
                l = jnp.sum(e, axis=-1, keepdims=True)
                acc = acc + _dot(e.astype(BF16), jnp.where(masks[sub], vt, 0)) * (1.0 / l)
                lse_cols = lse_cols + jnp.where(lane == h, mx + jnp.log(l), 0.0)
            o_ref[:, cs] = acc.astype(BF16)
        lse_ref[...] = lse_cols
        lset_ref[...] = lse_cols.T[0:SUBLANES, :]

    def kv_spec(off, cb):
        return pl.BlockSpec((ATT_QB, PB), lambda i: (jnp.maximum(i + off, 0), cb))

    return pl.pallas_call(
        body, name="att_fwd", grid=(nb,),
        in_specs=[pl.BlockSpec((ATT_QB, PB), lambda i: (i, CB_QB)),
                  kv_spec(-2, CB_KB), kv_spec(-1, CB_KB), kv_spec(0, CB_KB),
                  kv_spec(-2, CB_VB), kv_spec(-1, CB_VB), kv_spec(0, CB_VB),
                  pl.BlockSpec((None, ATT_HEADS, ATT_QB, ATT_KW), lambda i: (jnp.minimum(i, 2), 0, 0, 0))],
        out_specs=[pl.BlockSpec((ATT_QB, WIDTH_B), lambda i: (i, 0)),
                   pl.BlockSpec((ATT_QB, LANES), lambda i: (i, 0)),
                   pl.BlockSpec((SUBLANES, ATT_QB), lambda i: (0, i))],
        out_shape=[jax.ShapeDtypeStruct((t, WIDTH_B), BF16), jax.ShapeDtypeStruct((t, LANES), F32),
                   jax.ShapeDtypeStruct((SUBLANES, t), F32)],
        compiler_params=_cparams("parallel"),
    )(proj, proj, proj, proj, proj, proj, proj, bias_q)


def _att_dq(proj, bias_q, lse, d_ob, dproj):
    t = proj.shape[0]
    nb = t // ATT_QB
    scale = ATT_DH ** -0.5
    nslab = ATT_QB // SUBLANES

    def body(q_ref, k0_ref, k1_ref, k2_ref, v0_ref, v1_ref, v2_ref, b_ref, lse_ref, do_ref, dp_in_ref,
             dq_ref, dlt_ref, slab_ref):
        i = pl.program_id(0)

        @pl.when(i == 0)
        def _():
            slab_ref[...] = jnp.zeros_like(slab_ref)

        q = (q_ref[...] * scale).astype(BF16)
        kk = jnp.concatenate([k0_ref[...], k1_ref[...], k2_ref[...]], axis=0).astype(BF16)
        vv = jnp.concatenate([v0_ref[...], v1_ref[...], v2_ref[...]], axis=0).astype(BF16)
        do = do_ref[...].astype(BF16)
        lane = lax.broadcasted_iota(jnp.int32, (1, LANES), 1)
        masks = _head_masks()
        lse_all = lse_ref[...]
        dlt_cols = jnp.zeros((ATT_QB, LANES), F32)
        zpad = jnp.zeros((SUBLANES, ATT_VEC - ATT_KW), F32)
        for p in range(ATT_HEADS // 2):
            cs = slice(p * LANES, (p + 1) * LANES)
            qt, kt, vt, dot_ = q[:, cs], kk[:, cs], vv[:, cs], do[:, cs]
            acc = jnp.zeros((ATT_QB, LANES), F32)
            for sub in range(2):
                h = 2 * p + sub
                s = _dot_nt(jnp.where(masks[sub], qt, 0), kt) + b_ref[h]
                pr = jnp.exp(s - lse_all[:, h:h + 1])
                dp = _dot_nt(jnp.where(masks[sub], dot_, 0), vt)
                dl = jnp.sum(pr * dp, axis=-1, keepdims=True)
                ds = pr * (dp - dl)
                acc = acc + _dot(ds.astype(BF16), jnp.where(masks[sub], kt, 0)) * scale
                dlt_cols = dlt_cols + jnp.where(lane == h, dl, 0.0)
                sl = jnp.zeros((SUBLANES, ATT_VEC), F32)
                for a in range(nslab):
                    piece = jnp.concatenate([ds[a * SUBLANES:(a + 1) * SUBLANES, :], zpad], axis=1)
                    sl = sl + (piece if a == 0 else pltpu.roll(piece, ATT_VEC - SUBLANES * a, 1))
                slab_ref[h] += sl
            dq_ref[:, cs] = acc.astype(BF16)
        dlt_ref[...] = dlt_cols.T[0:SUBLANES, :]

    def kv_spec(off, cb):
        return pl.BlockSpec((ATT_QB, PB), lambda i: (jnp.maximum(i + off, 0), cb))

    return pl.pallas_call(
        body, name="att_dq", grid=(nb,),
        in_specs=[pl.BlockSpec((ATT_QB, PB), lambda i: (i, CB_QB)),
                  kv_spec(-2, CB_KB), kv_spec(-1, CB_KB), kv_spec(0, CB_KB),
                  kv_spec(-2, CB_VB), kv_spec(-1, CB_VB), kv_spec(0, CB_VB),
                  pl.BlockSpec((None, ATT_HEADS, ATT_QB, ATT_KW), lambda i: (jnp.minimum(i, 2), 0, 0, 0)),
                  pl.BlockSpec((ATT_QB, LANES), lambda i: (i, 0)),
                  pl.BlockSpec((ATT_QB, WIDTH_B), lambda i: (i, 0)), pl.BlockSpec(memory_space=pl.ANY)],
        out_specs=[_dp_spec(ATT_QB, DP_QB),
                   pl.BlockSpec((SUBLANES, ATT_QB), lambda i: (0, i)),
                   pl.BlockSpec((ATT_HEADS, SUBLANES, ATT_VEC), lambda i: (0, 0, 0))],
        out_shape=[jax.ShapeDtypeStruct(dproj.shape, dproj.dtype), jax.ShapeDtypeStruct((SUBLANES, t), F32),
                   jax.ShapeDtypeStruct((ATT_HEADS, SUBLANES, ATT_VEC), F32)],
        input_output_aliases={10: 0},
        compiler_params=_cparams("arbitrary"),
    )(proj, proj, proj, proj, proj, proj, proj, bias_q, lse, d_ob, dproj)


def _att_dkv(proj, bias_k, lse_t, dlt_t, d_ob, dproj):
    t = proj.shape[0]
    nb = t // ATT_QB
    scale = ATT_DH ** -0.5

    def body(k_ref, v_ref, q0_ref, q1_ref, q2_ref, d0_ref, d1_ref, d2_ref, l0_ref, l1_ref, l2_ref,
             e0_ref, e1_ref, e2_ref, b_ref, dp_in_ref, dkv_ref):
        i = pl.program_id(0)
        k = k_ref[...].astype(BF16)
        v = v_ref[...].astype(BF16)
        qq = (jnp.concatenate([q0_ref[...], q1_ref[...], q2_ref[...]], axis=0) * scale).astype(BF16)
        do = jnp.concatenate([d0_ref[...], d1_ref[...], d2_ref[...]], axis=0).astype(BF16)
        lse = jnp.concatenate([l0_ref[...], l1_ref[...], l2_ref[...]], axis=1)
        dlt = jnp.concatenate([e0_ref[...], e1_ref[...], e2_ref[...]], axis=1)
        masks = _head_masks()
        for p in range(ATT_HEADS // 2):
            cs = slice(p * LANES, (p + 1) * LANES)
            kt, vt, qt, dot_ = k[:, cs], v[:, cs], qq[:, cs], do[:, cs]
            acc_k = jnp.zeros((ATT_QB, LANES), F32)
            acc_v = jnp.zeros((ATT_QB, LANES), F32)
            for sub in range(2):
                h = 2 * p + sub
                st = _dot_nt(jnp.where(masks[sub], kt, 0), qt) + b_ref[h]
                pt = jnp.exp(st - lse[h:h + 1, :])
                dot_m = jnp.where(masks[sub], dot_, 0)
                acc_v = acc_v + _dot(pt.astype(BF16), dot_m)
                dpt = _dot_nt(jnp.where(masks[sub], vt, 0), dot_)
                dst = pt * (dpt - dlt[h:h + 1, :])
                acc_k = acc_k + _dot(dst.astype(BF16), jnp.where(masks[sub], qt, 0))
            dkv_ref[:, cs] = acc_k.astype(BF16)
            dkv_ref[:, WIDTH_B + p * LANES:WIDTH_B + (p + 1) * LANES] = acc_v.astype(BF16)

    def q_spec(off, cb):
        return pl.BlockSpec((ATT_QB, PB), lambda i: (jnp.minimum(i + off, nb - 1), cb))

    def d_spec(off):
        return pl.BlockSpec((ATT_QB, WIDTH_B), lambda i: (jnp.minimum(i + off, nb - 1), 0))

    def r_spec(off):
        return pl.BlockSpec((SUBLANES, ATT_QB), lambda i: (0, jnp.minimum(i + off, nb - 1)))

    row = pl.BlockSpec((ATT_QB, WIDTH_B), lambda i: (i, 0))
    return pl.pallas_call(
        body, name="att_dkv", grid=(nb,),
        in_specs=[pl.BlockSpec((ATT_QB, PB), lambda i: (i, CB_KB)), pl.BlockSpec((ATT_QB, PB), lambda i: (i, CB_VB)),
                  q_spec(0, CB_QB), q_spec(1, CB_QB), q_spec(2, CB_QB),
                  d_spec(0), d_spec(1), d_spec(2), r_spec(0), r_spec(1), r_spec(2),
                  r_spec(0), r_spec(1), r_spec(2),
                  pl.BlockSpec((None, ATT_HEADS, ATT_QB, ATT_KW), lambda i: (jnp.minimum(nb - 1 - i, 2), 0, 0, 0)),
                  pl.BlockSpec(memory_space=pl.ANY)],
        out_specs=_dp_spec(ATT_QB, DP_KVB),
        out_shape=jax.ShapeDtypeStruct(dproj.shape, dproj.dtype),
        input_output_aliases={15: 0},
        compiler_params=_cparams("parallel"),
    )(proj, proj, proj, proj, proj, d_ob, d_ob, d_ob, lse_t, lse_t, lse_t, dlt_t, dlt_t, dlt_t, bias_k, dproj)


def _relbias_grad(slabs):
    ohq, _ = _bias_onehots()

    def body(s_ref, oh_ref, o_ref):
        sv = s_ref[...]
        vec = sv[0:1, :]
        for b in range(1, SUBLANES):
            vec = vec + pltpu.roll(sv[b:b + 1, :], ATT_VEC - b, 1)
        o_ref[...] = _dot_nt(jnp.broadcast_to(vec, (SUBLANES, ATT_VEC)), oh_ref[...], HIGHEST)[0:1, :]

    h = slabs.shape[0]
    return pl.pallas_call(
        body, name="att_dbias", grid=(h,),
        in_specs=[pl.BlockSpec((None, SUBLANES, ATT_VEC), lambda i: (i, 0, 0)),
                  pl.BlockSpec((3 * LANES, ATT_VEC), lambda i: (0, 0))],
        out_specs=pl.BlockSpec((None, 1, 3 * LANES), lambda i: (i, 0, 0)),
        out_shape=jax.ShapeDtypeStruct((h, 1, 3 * LANES), F32),
        compiler_params=_cparams("parallel"),
    )(slabs, jnp.asarray(ohq))


GDN_TM = 512
GDN_CB = 4
HALO = SUBLANES


def _conv_taps(ext, width, lead, n):
    return [(ext if k == width - 1 else pltpu.roll(ext, width - 1 - k, 0))[lead:lead + n] for k in range(width)]


def _prev_halo_spec(tm, width, cb):
    return pl.BlockSpec((HALO, width), lambda i: (jnp.maximum(i * (tm // HALO) - 1, 0), cb))


def _next_halo_spec(tm, width, cb, t):
    return pl.BlockSpec((HALO, width), lambda i: (jnp.minimum((i + 1) * (tm // HALO), t // HALO - 1), cb))


def _gdn_prep_fwd(proj, conv_w):
    t = proj.shape[0]
    tm = GDN_TM

    def body(q_ref, k_ref, v_ref, hq_ref, hk_ref, hv_ref, w_ref, qn_ref, kn_ref, vo_ref):
        first = pl.program_id(0) == 0
        for idx, (x_ref, h_ref, o_ref) in enumerate(((q_ref, hq_ref, qn_ref), (k_ref, hk_ref, kn_ref),
                                                      (v_ref, hv_ref, vo_ref))):
            halo = jnp.where(first, 0.0, h_ref[...])
            ext = jnp.concatenate([halo, x_ref[...]], axis=0)
            w = w_ref[:, idx * KEY_A:(idx + 1) * KEY_A]
            taps = _conv_taps(ext, GDN_CONV, HALO, tm)
            y = sum(w[k:k + 1, :] * taps[k] for k in range(GDN_CONV))
            a = y * _sigmoid(y)
            if idx < 2:
                for h in range(GDN_HEADS):
                    cs = slice(h * GDN_DK, (h + 1) * GDN_DK)
                    seg = a[:, cs]
                    o_ref[:, cs] = seg * lax.rsqrt(jnp.sum(seg * seg, axis=-1, keepdims=True) + EPS)
            else:
                o_ref[...] = a

    row = pl.BlockSpec((tm, KEY_A), lambda i: (i, 0))
    return pl.pallas_call(
        body, name="gdn_prep_fwd", grid=(t // tm,),
        in_specs=[pl.BlockSpec((tm, PB), lambda i: (i, CB_QA)), pl.BlockSpec((tm, PB), lambda i: (i, CB_KA)),
                  pl.BlockSpec((tm, PB), lambda i: (i, CB_VA)),
                  _prev_halo_spec(tm, PB, CB_QA), _prev_halo_spec(tm, PB, CB_KA), _prev_halo_spec(tm, PB, CB_VA),
                  pl.BlockSpec((GDN_CONV, 3 * KEY_A), lambda i: (0, 0))],
        out_specs=[row, row, row],
        out_shape=[jax.ShapeDtypeStruct((t, KEY_A), F32)] * 3,
        compiler_params=_cparams("parallel"),
    )(proj, proj, proj, proj, proj, proj, conv_w)


def _gdn_prep_bwd(proj, conv_w, dqn, dkn, dv, dproj):
    t = proj.shape[0]
    tm = GDN_TM
    nt = t // tm
    n_ext = tm + HALO

    def body(q_ref, k_ref, v_ref, pq_ref, pk_ref, pv_ref, nq_ref, nk_ref, nv_ref,
             dq_ref, dk_ref, dv_ref, ndq_ref, ndk_ref, ndv_ref, w_ref, dp_in_ref, out_ref, dw_ref):
        i = pl.program_id(0)
        first, last = i == 0, i == nt - 1

        @pl.when(first)
        def _():
            dw_ref[...] = jnp.zeros_like(dw_ref)

        groups = ((q_ref, pq_ref, nq_ref, dq_ref, ndq_ref), (k_ref, pk_ref, nk_ref, dk_ref, ndk_ref),
                  (v_ref, pv_ref, nv_ref, dv_ref, ndv_ref))
        for idx, (x_ref, p_ref, n_ref, d_ref, nd_ref) in enumerate(groups):
            cs_all = slice(idx * KEY_A, (idx + 1) * KEY_A)
            ext = jnp.concatenate([jnp.where(first, 0.0, p_ref[...]), x_ref[...], jnp.where(last, 0.0, n_ref[...])], axis=0)
            w = w_ref[:, cs_all]
            taps = _conv_taps(ext, GDN_CONV, HALO, n_ext)
            y = sum(w[k:k + 1, :] * taps[k] for k in range(GDN_CONV))
            sg = _sigmoid(y)
            a = y * sg
            dup = jnp.concatenate([d_ref[...], jnp.where(last, 0.0, nd_ref[...])], axis=0)
            if idx < 2:
                segs = []
                for h in range(GDN_HEADS):
                    cs = slice(h * GDN_DK, (h + 1) * GDN_DK)
                    seg = a[:, cs]
                    r = lax.rsqrt(jnp.sum(seg * seg, axis=-1, keepdims=True) + EPS)
                    nrm = seg * r
                    dn = dup[:, cs]
                    segs.append(r * (dn - nrm * jnp.sum(dn * nrm, axis=-1, keepdims=True)))
                da = jnp.concatenate(segs, axis=1)
            else:
                da = dup
            dy = da * sg * (1.0 + y * (1.0 - sg))
            dx = sum(w[k:k + 1, :] * (dy if k == GDN_CONV - 1 else pltpu.roll(dy, n_ext - (GDN_CONV - 1 - k), 0))[:tm]
                     for k in range(GDN_CONV))
            out_ref[:, cs_all] = dx.astype(BF16)
            for k in range(GDN_CONV):
                dw_ref[k:k + 1, cs_all] += jnp.sum(dy[:tm] * taps[k][:tm], axis=0, keepdims=True)

    row = pl.BlockSpec((tm, KEY_A), lambda i: (i, 0))
    nrow = _next_halo_spec(tm, KEY_A, 0, t)
    return pl.pallas_call(
        body, name="gdn_prep_bwd", grid=(nt,),
        in_specs=[pl.BlockSpec((tm, PB), lambda i: (i, CB_QA)), pl.BlockSpec((tm, PB), lambda i: (i, CB_KA)),
                  pl.BlockSpec((tm, PB), lambda i: (i, CB_VA)),
                  _prev_halo_spec(tm, PB, CB_QA), _prev_halo_spec(tm, PB, CB_KA), _prev_halo_spec(tm, PB, CB_VA),
                  _next_halo_spec(tm, PB, CB_QA, t), _next_halo_spec(tm, PB, CB_KA, t), _next_halo_spec(tm, PB, CB_VA, t),
                  row, row, row, nrow, nrow, nrow,
                  pl.BlockSpec((GDN_CONV, 3 * KEY_A), lambda i: (0, 0)), pl.BlockSpec(memory_space=pl.ANY)],
        out_specs=[_dp_spec(tm, DP_QKVA), pl.BlockSpec((SUBLANES, 3 * KEY_A), lambda i: (0, 0))],
        out_shape=[jax.ShapeDtypeStruct(dproj.shape, dproj.dtype), jax.ShapeDtypeStruct((SUBLANES, 3 * KEY_A), F32)],
        input_output_aliases={16: 0},
        compiler_params=_cparams("arbitrary"),
    )(proj, proj, proj, proj, proj, proj, proj, proj, proj, dqn, dkn, dv, dqn, dkn, dv, conv_w, dproj)


class _Pair(dict):
    __getattr__ = dict.__getitem__
    __setattr__ = dict.__setitem__


def _pairs_to_lanes(cols):
    lane = lax.broadcasted_iota(jnp.int32, (1, LANES), 1)
    out = jnp.zeros((cols[0].shape[0], LANES), F32)
    for p, col in enumerate(cols):
        out = out + jnp.where(lane == p, col, 0.0)
    return out


def _gdn_terms(bd, par, kn_ref, qn_ref):
    c = CHUNK
    ii = lax.broadcasted_iota(jnp.int32, (c, c), 0)
    jj = lax.broadcasted_iota(jnp.int32, (c, c), 1)
    strict, incl = ii > jj, ii >= jj
    ltri = incl.astype(F32)
    ts = []
    for cc in range(GDN_CB):
        for h in range(GDN_HEADS):
            t = _Pair(cc=cc, h=h, rows=slice(cc * c, (cc + 1) * c), cs=slice(h * GDN_DK, (h + 1) * GDN_DK),
                      strict=strict, incl=incl)
            t.beta = _sigmoid(bd[t.rows, h:h + 1])
            t.ea = jnp.exp(par[0:1, h:h + 1])
            t.sp_arg = bd[t.rows, GDN_HEADS + h:GDN_HEADS + h + 1] + par[1:2, h:h + 1]
            t.g = -t.ea * _softplus(t.sp_arg)
            t.k = kn_ref[t.rows, t.cs]
            t.q = qn_ref[t.rows, t.cs] * (GDN_DK ** -0.5)
            t.kb, t.qb = t.k.astype(BF16), t.q.astype(BF16)
            ts.append(t)
    gall = _dot(ltri, _pairs_to_lanes([t.g for t in ts]), HIGHEST)
    gall_t = gall.T
    for p, t in enumerate(ts):
        t.gb = jnp.broadcast_to(gall[:, p:p + 1], (c, GDN_DK))
    for t in ts:
        t.kk = _dot_nt(t.kb, t.kb)
        t.qk = _dot_nt(t.qb, t.kb)
    for p, t in enumerate(ts):
        diff = t.gb[:, :c] - gall_t[p:p + 1, :]
        t.dec_s = jnp.exp(jnp.where(strict, diff, NEG))
        t.dec_i = jnp.exp(jnp.where(incl, diff, NEG))
        t.gam = jnp.exp(t.gb)
        glast = t.gb[c - 1:c, :]
        t.e_rest = jnp.exp(glast - t.gb)
        t.gl = jnp.exp(glast)
        t.p = t.qk * t.dec_i
    return ts


def _gdn_fwd(qn, kn, v, proj, par, gnw):
    t = qn.shape[0]
    c = CHUNK
    nc = t // c
    r_ = GDN_CB * c

    def body(qn_ref, kn_ref, v_ref, bd_ref, z_ref, par_ref, gnw_ref,
             oan_ref, o_ref, sp_ref, w_ref, u_ref, tm_ref, s_ref):
        @pl.when(pl.program_id(0) == 0)
        def _():
            s_ref[...] = jnp.zeros_like(s_ref)

        bd, par, gnw_v = bd_ref[...], par_ref[...], gnw_ref[...]
        eye = (lax.broadcasted_iota(jnp.int32, (c, c), 0) == lax.broadcasted_iota(jnp.int32, (c, c), 1)).astype(F32)
        ts = _gdn_terms(bd, par, kn_ref, qn_ref)
        for t in ts:
            t.vv = v_ref[t.rows, t.cs]
            t.x = -(t.beta * t.kk * t.dec_s)
            t.tinv = eye + t.x
        for t in ts:
            t.xs = _split(t.x)
        for _ in range(5):
            for t in ts:
                t.xs = _split(_dot3s(t.xs, t.xs))
            for t in ts:
                t.tinv = t.tinv + _dot3s(_split(t.tinv), t.xs)
        for t in ts:
            tsp = _split(t.tinv)
            t.wm = _dot3s(tsp, _split((t.beta * t.gam) * t.k))
            t.uv = _dot3s(tsp, _split(t.beta * t.vv))
        for t in ts:
            w_ref[t.rows, t.cs] = t.wm
            tm_ref[t.cc, t.h] = t.tinv.T
            t.wb = t.wm.astype(BF16)
            t.qgb = (t.q * t.gam).astype(BF16)
            t.kdb = (t.k * t.e_rest).astype(BF16)
            t.pb = t.p.astype(BF16)
        state = [s_ref[h] for h in range(GDN_HEADS)]
        for cc in range(GDN_CB):
            tc = [t for t in ts if t.cc == cc]
            for t in tc:
                t.sh = state[t.h]
                t.sb = t.sh.astype(BF16)
            for t in tc:
                t.ws = _dot(t.wb, t.sb)
            for t in tc:
                t.u = t.uv - t.ws
                t.ub = t.u.astype(BF16)
            for t in tc:
                state[t.h] = t.gl * t.sh + _dot_tn(t.kdb, t.ub)
            for t in tc:
                t.o = _dot(t.qgb, t.sb) + _dot(t.pb, t.ub)
                sp_ref[cc, t.h] = t.sh
                u_ref[t.rows, t.cs] = t.u
                o_ref[t.rows, t.cs] = t.o
        for h in range(GDN_HEADS):
            s_ref[h] = state[h]
        for t in ts:
            zz = z_ref[t.rows, t.cs]
            rr = lax.rsqrt(jnp.mean(t.o * t.o, axis=-1, keepdims=True) + EPS)
            oan_ref[t.rows, t.cs] = ((t.o * rr) * gnw_v * (zz * _sigmoid(zz))).astype(BF16)

    row = pl.BlockSpec((r_, KEY_A), lambda i: (i, 0))
    return pl.pallas_call(
        body, name="gdn_fwd", grid=(nc // GDN_CB,),
        in_specs=[row, row, row, pl.BlockSpec((r_, LANES), lambda i: (i, CB_BD)),
                  pl.BlockSpec((r_, PB), lambda i: (i, CB_ZA)),
                  pl.BlockSpec((SUBLANES, LANES), lambda i: (0, 0)), pl.BlockSpec((1, GDN_DK), lambda i: (0, 0))],
        out_specs=[row, row, pl.BlockSpec((GDN_CB, GDN_HEADS, GDN_DK, GDN_DK), lambda i: (i, 0, 0, 0)),
                   row, row, pl.BlockSpec((GDN_CB, GDN_HEADS, c, c), lambda i: (i, 0, 0, 0))],
        out_shape=[jax.ShapeDtypeStruct((t, KEY_A), BF16), jax.ShapeDtypeStruct((t, KEY_A), F32),
                   jax.ShapeDtypeStruct((nc, GDN_HEADS, GDN_DK, GDN_DK), F32),
                   jax.ShapeDtypeStruct((t, KEY_A), F32), jax.ShapeDtypeStruct((t, KEY_A), F32),
                   jax.ShapeDtypeStruct((nc, GDN_HEADS, c, c), F32)],
        scratch_shapes=[pltpu.VMEM((GDN_HEADS, GDN_DK, GDN_DK), F32)],
        compiler_params=_cparams("arbitrary"),
    )(qn, kn, v, proj, proj, par, gnw)


def _gdn_bwd(qn, kn, v, proj, par, gnw, o, sprev, wst, ust, tst, d_oan, dproj):
    t = qn.shape[0]
    c = CHUNK
    nc = t // c
    nb = nc // GDN_CB
    r_ = GDN_CB * c

    def body(qn_ref, kn_ref, v_ref, bd_ref, z_ref, par_ref, gnw_ref, o_ref, sp_ref, w_ref, u_ref, tm_ref, do_ref,
             dp_in_ref, dqn_ref, dkn_ref, dv_ref, dzb_ref, acc_ref, ds_ref):
        @pl.when(pl.program_id(0) == 0)
        def _():
            ds_ref[...] = jnp.zeros_like(ds_ref)
            acc_ref[...] = jnp.zeros_like(acc_ref)

        bd, par, gnw_v = bd_ref[...], par_ref[...], gnw_ref[...]
        lane = lax.broadcasted_iota(jnp.int32, (1, LANES), 1)
        rix = lax.broadcasted_iota(jnp.int32, (c, 1), 0)
        ii = lax.broadcasted_iota(jnp.int32, (c, c), 0)
        jj = lax.broadcasted_iota(jnp.int32, (c, c), 1)
        upper = (jj >= ii).astype(F32)
        acc_a = jnp.zeros((1, LANES), F32)
        acc_d = jnp.zeros((1, LANES), F32)
        acc_g = jnp.zeros((1, LANES), F32)
        ts = _gdn_terms(bd, par, kn_ref, qn_ref)
        for t in ts:
            t.vv = v_ref[t.rows, t.cs]
            t.sh = sp_ref[t.cc, t.h]
            t.sb = t.sh.astype(BF16)
            t.wm, t.u, t.tinv_t = w_ref[t.rows, t.cs], u_ref[t.rows, t.cs], tm_ref[t.cc, t.h]
            t.wb, t.ub = t.wm.astype(BF16), t.u.astype(BF16)
            ov, zz, dout = o_ref[t.rows, t.cs], z_ref[t.rows, t.cs], do_ref[t.rows, t.cs]
            sg = _sigmoid(zz)
            sil = zz * sg
            rr = lax.rsqrt(jnp.mean(ov * ov, axis=-1, keepdims=True) + EPS)
            on = ov * rr
            dzb_ref[t.rows, t.cs] = (dout * on * gnw_v * (sg * (1.0 + zz * (1.0 - sg)))).astype(BF16)
            acc_g = acc_g + jnp.sum(dout * on * sil, axis=0, keepdims=True)
            don = dout * gnw_v * sil
            t.dob = (rr * (don - on * jnp.mean(don * on, axis=-1, keepdims=True))).astype(BF16)
            t.qg = t.q * t.gam
            t.kd = t.k * t.e_rest
            t.qgb, t.kdb = t.qg.astype(BF16), t.kd.astype(BF16)
            t.ptb = t.p.T.astype(BF16)
        for t in ts:
            t.du0 = _dot(t.ptb, t.dob)
            t.ds0 = _dot_tn(t.qgb, t.dob)
            t.dqg = _dot_nt(t.dob, t.sb)
            t.dp = _dot_nt(t.dob, t.ub)
            t.uv = t.u + _dot(t.wb, t.sb)
        dstate = [ds_ref[h] for h in range(GDN_HEADS)]
        for cc in reversed(range(GDN_CB)):
            tc = [t for t in ts if t.cc == cc]
            for t in tc:
                t.dsn = dstate[t.h]
                t.dsnb = t.dsn.astype(BF16)
            for t in tc:
                t.du = t.du0 + _dot(t.kdb, t.dsnb)
            for t in tc:
                t.dub = t.du.astype(BF16)
            for t in tc:
                dstate[t.h] = t.gl * t.dsn + t.ds0 - _dot_tn(t.wb, t.dub)
            for t in tc:
                t.dkd = _dot_nt(t.ub, t.dsnb)
                t.dgl = jnp.sum(jnp.sum(t.dsn * t.sh, axis=1, keepdims=True), axis=0, keepdims=True)
                t.dwm = -_dot_nt(t.dub, t.sb)
        for h in range(GDN_HEADS):
            ds_ref[h] = dstate[h]
        for t in ts:
            tsp = _split(t.tinv_t)
            t.dbk = _dot3s(tsp, _split(t.dwm))
            t.dbv = _dot3s(tsp, _split(t.du))
        for t in ts:
            d_a = -(_dot_nt(t.dbk.astype(BF16), t.wb) + _dot_nt(t.dbv.astype(BF16), t.uv.astype(BF16)))
            t.d_a = jnp.where(t.strict, d_a, 0.0)
        for t in ts:
            t.dkk = t.d_a * t.beta * t.dec_s
            t.dqk = t.dp * t.dec_i
            t.dqkb = t.dqk.astype(BF16)
        for t in ts:
            t.dq = _dot(t.dqkb, t.kb) + t.dqg * t.gam
            t.dk = (t.dbk * (t.beta * t.gam) + _dot_tn(t.dqkb, t.qb) + _dot((t.dkk + t.dkk.T).astype(BF16), t.kb)
                    + t.dkd * t.e_rest)
        for t in ts:
            dbeta = (jnp.sum(t.d_a * t.kk * t.dec_s, axis=-1, keepdims=True)
                     + jnp.sum(t.dbk * t.k * t.gam, axis=-1, keepdims=True) + jnp.sum(t.dbv * t.vv, axis=-1, keepdims=True))
            t.dbl = dbeta * t.beta * (1.0 - t.beta)
            dv_ref[t.rows, t.cs] = t.dbv * t.beta
            bk = (t.beta * t.gam) * t.k
            zc = jnp.sum(t.dkd * t.kd, axis=-1, keepdims=True)
            xs = t.dkk * t.kk + t.dp * t.p
            dgc = (jnp.sum(xs, axis=-1, keepdims=True) - jnp.sum(xs.T, axis=-1, keepdims=True)
                   + jnp.sum(t.dbk * bk, axis=-1, keepdims=True) + jnp.sum(t.dqg * t.qg, axis=-1, keepdims=True) - zc)
            dglast = jnp.sum(zc, axis=0, keepdims=True) + t.dgl * t.gl[:, 0:1]
            t.dgc = dgc + jnp.where(rix == c - 1, dglast, 0.0)
        dgall = _dot(upper, _pairs_to_lanes([t.dgc for t in ts]), HIGHEST)
        for p, t in enumerate(ts):
            t.dg = dgall[:, p:p + 1]
        dbd_tiles = [jnp.zeros((c, LANES), F32) for _ in range(GDN_CB)]
        for t in ts:
            ddl = t.dg * (-t.ea) * _sigmoid(t.sp_arg)
            acc_a = acc_a + jnp.where(lane == t.h, jnp.sum(t.dg * t.g, axis=0, keepdims=True), 0.0)
            acc_d = acc_d + jnp.where(lane == t.h, jnp.sum(ddl, axis=0, keepdims=True), 0.0)
            dbd_tiles[t.cc] = (dbd_tiles[t.cc] + jnp.where(lane == t.h, t.dbl, 0.0)
                               + jnp.where(lane == GDN_HEADS + t.h, ddl, 0.0))
            dqn_ref[t.rows, t.cs] = t.dq * (GDN_DK ** -0.5)
            dkn_ref[t.rows, t.cs] = t.dk
        for cc in range(GDN_CB):
            dzb_ref[cc * c:(cc + 1) * c, KEY_A:KEY_A + LANES] = dbd_tiles[cc].astype(BF16)
        acc_ref[0:1, :] += acc_a
        acc_ref[1:2, :] += acc_d
        acc_ref[2:3, :] += acc_g

    def rev(i):
        return nb - 1 - i

    row = pl.BlockSpec((r_, KEY_A), lambda i: (rev(i), 0))
    st = pl.BlockSpec((GDN_CB, GDN_HEADS, GDN_DK, GDN_DK), lambda i: (rev(i), 0, 0, 0))
    tt_spec = pl.BlockSpec((GDN_CB, GDN_HEADS, c, c), lambda i: (rev(i), 0, 0, 0))
    return pl.pallas_call(
        body, name="gdn_bwd", grid=(nb,),
        in_specs=[row, row, row, pl.BlockSpec((r_, LANES), lambda i: (rev(i), CB_BD)),
                  pl.BlockSpec((r_, PB), lambda i: (rev(i), CB_ZA)),
                  pl.BlockSpec((SUBLANES, LANES), lambda i: (0, 0)), pl.BlockSpec((1, GDN_DK), lambda i: (0, 0)),
                  row, st, row, row, tt_spec, row, pl.BlockSpec(memory_space=pl.ANY)],
        out_specs=[row, row, row, _dp_spec(r_, DP_ZBD, rev), pl.BlockSpec((SUBLANES, LANES), lambda i: (0, 0))],
        out_shape=[jax.ShapeDtypeStruct((t, KEY_A), F32)] * 3 + [jax.ShapeDtypeStruct(dproj.shape, dproj.dtype),
                                                                jax.ShapeDtypeStruct((SUBLANES, LANES), F32)],
        input_output_aliases={13: 3},
        scratch_shapes=[pltpu.VMEM((GDN_HEADS, GDN_DK, GDN_DK), F32)],
        compiler_params=_cparams("arbitrary"),
    )(qn, kn, v, proj, proj, par, gnw, o, sprev, wst, ust, tst, d_oan, dproj)


def _merge_fwd(oan, ob, proj, x, wba, wbb, wout, tm=512):
    t = x.shape[0]

    def body(oa_ref, ob_ref, ga_ref, gb_ref, x_ref, wba_ref, wbb_ref, wout_ref, x2_ref):
        ya = _dot(oa_ref[...], wba_ref[...])
        yb = _dot(ob_ref[...], wbb_ref[...])
        mix = _sigmoid(ga_ref[...]) * ya + _sigmoid(gb_ref[...]) * yb
        x2_ref[...] = x_ref[...] + _dot(mix.astype(BF16), wout_ref[...])

    half = pl.BlockSpec((tm, KEY_A), lambda i: (i, 0))
    row = pl.BlockSpec((tm, D_MODEL), lambda i: (i, 0))
    wsmall = pl.BlockSpec((KEY_A, D_MODEL), lambda i: (0, 0))
    return pl.pallas_call(
        body, name="merge_fwd", grid=(t // tm,),
        in_specs=[half, half, pl.BlockSpec((tm, D_MODEL), lambda i: (i, CB_GA)),
                  pl.BlockSpec((tm, D_MODEL), lambda i: (i, CB_GB)), row, wsmall, wsmall,
                  pl.BlockSpec((D_MODEL, D_MODEL), lambda i: (0, 0))],
        out_specs=row,
        out_shape=jax.ShapeDtypeStruct((t, D_MODEL), F32),
        compiler_params=_cparams("parallel"),
    )(oan, ob, proj, proj, x, wba, wbb, wout)


def _merge_bwd(dx2b, oan, ob, proj, wba, wbb, wout_t, wba_t, wbb_t, tm=512):
    t = dx2b.shape[0]

    def body(dx_ref, oa_ref, ob_ref, ga_ref, gb_ref, wba_ref, wbb_ref, woutt_ref, wbat_ref, wbbt_ref,
             dg_ref, doa_ref, dob_ref, gout_ref, gba_ref, gbb_ref):
        @pl.when(pl.program_id(0) == 0)
        def _():
            gout_ref[...] = jnp.zeros_like(gout_ref)
            gba_ref[...] = jnp.zeros_like(gba_ref)
            gbb_ref[...] = jnp.zeros_like(gbb_ref)

        dx, oa, ob = dx_ref[...], oa_ref[...], ob_ref[...]
        dmix = _dot(dx, woutt_ref[...])
        ya = _dot(oa, wba_ref[...])
        yb = _dot(ob, wbb_ref[...])
        sa, sb = _sigmoid(ga_ref[...]), _sigmoid(gb_ref[...])
        gout_ref[...] += _dot_tn((sa * ya + sb * yb).astype(BF16), dx)
        dg_ref[:, :D_MODEL] = (dmix * ya * sa * (1.0 - sa)).astype(BF16)
        dg_ref[:, D_MODEL:] = (dmix * yb * sb * (1.0 - sb)).astype(BF16)
        dya = (dmix * sa).astype(BF16)
        dyb = (dmix * sb).astype(BF16)
        gba_ref[...] += _dot_tn(oa, dya)
        gbb_ref[...] += _dot_tn(ob, dyb)
        doa_ref[...] = _dot(dya, wbat_ref[...])
        dob_ref[...] = _dot(dyb, wbbt_ref[...])

    half = pl.BlockSpec((tm, KEY_A), lambda i: (i, 0))
    row = pl.BlockSpec((tm, D_MODEL), lambda i: (i, 0))
    wsmall = pl.BlockSpec((KEY_A, D_MODEL), lambda i: (0, 0))
    wsmall_t = pl.BlockSpec((D_MODEL, KEY_A), lambda i: (0, 0))
    wfull = pl.BlockSpec((D_MODEL, D_MODEL), lambda i: (0, 0))
    return pl.pallas_call(
        body, name="merge_bwd", grid=(t // tm,),
        in_specs=[row, half, half, pl.BlockSpec((tm, D_MODEL), lambda i: (i, CB_GA)),
                  pl.BlockSpec((tm, D_MODEL), lambda i: (i, CB_GB)), wsmall, wsmall, wfull, wsmall_t, wsmall_t],
        out_specs=[_dp_spec(tm, DP_GATES), half, half, wfull, wsmall, wsmall],
        out_shape=[jax.ShapeDtypeStruct((t, PROJ_W), BF16), jax.ShapeDtypeStruct((t, KEY_A), F32),
                   jax.ShapeDtypeStruct((t, KEY_A), F32), jax.ShapeDtypeStruct((D_MODEL, D_MODEL), F32),
                   jax.ShapeDtypeStruct((KEY_A, D_MODEL), F32), jax.ShapeDtypeStruct((WIDTH_B, D_MODEL), F32)],
        compiler_params=_cparams("arbitrary"),
    )(dx2b, oan, ob, proj, proj, wba, wbb, wout_t, wba_t, wbb_t)


FFN_TM = 128
FFN_W = 2 * D_FF


def _ffn_conv(up_ref, halo_ref, cw_ref, cb_ref, first):
    ext = jnp.concatenate([jnp.where(first, 0.0, halo_ref[...]), up_ref[...]], axis=0)
    taps = _conv_taps(ext, FFN_CONV, HALO, FFN_TM)
    cw = cw_ref[...]
    u = sum(cw[k:k + 1, :] * taps[k] for k in range(FFN_CONV)) + cb_ref[...]
    return u, taps


def _resident(shape):
    return pl.BlockSpec(shape, lambda i: (0,) * len(shape), pipeline_mode=pl.Buffered(1))


def _ffn_fwd(h2, wup, cw, cb, wdown, x2, tgt, w3):
    t = x2.shape[0]
    tm = FFN_TM

    def body(h2_ref, wup_ref, cw_ref, cb_ref, wd_ref, x2_ref, tgt_ref, w3_ref, up_ref, dx_ref, dxb_ref, act_ref, acc_ref,
             prev_ref):
        @pl.when(pl.program_id(0) == 0)
        def _():
            acc_ref[...] = jnp.zeros_like(acc_ref)
            prev_ref[...] = jnp.zeros_like(prev_ref)

        up = _dot(h2_ref[...], wup_ref[...])
        up_ref[...] = up
        ext = jnp.concatenate([prev_ref[...], up], axis=0)
        prev_ref[...] = up[tm - HALO:tm]
        taps = _conv_taps(ext, FFN_CONV, HALO, tm)
        cw_v = cw_ref[...]
        u = sum(cw_v[k:k + 1, :] * taps[k] for k in range(FFN_CONV)) + cb_ref[...]
        gate, upp = u[:, :D_FF], u[:, D_FF:]
        act = (gate * _sigmoid(gate) * upp).astype(BF16)
        act_ref[...] = act
        x3 = x2_ref[...] + _dot(act, wd_ref[...])
        r = lax.rsqrt(jnp.mean(x3 * x3, axis=-1, keepdims=True) + EPS)
        xh = x3 * r
        w3v = w3_ref[...]
        err = xh * w3v - tgt_ref[...]
        loss = 0.5 * jnp.sum(jnp.mean(err * err, axis=-1, keepdims=True), axis=0, keepdims=True)
        dy = err * (1.0 / D_MODEL)
        acc_ref[0:1, :] += jnp.sum(dy * xh, axis=0, keepdims=True)
        acc_ref[1:2, :] += jnp.broadcast_to(loss, (1, D_MODEL))
        dxh = dy * w3v
        dx = r * (dxh - xh * jnp.mean(dxh * xh, axis=-1, keepdims=True))
        dx_ref[...] = dx
        dxb_ref[...] = dx.astype(BF16)

    row = pl.BlockSpec((tm, D_MODEL), lambda i: (i, 0))
    return pl.pallas_call(
        body, name="ffn_fwd", grid=(t // tm,),
        in_specs=[row, _resident((D_MODEL, FFN_W)), _resident((SUBLANES, FFN_W)), _resident((1, FFN_W)),
                  _resident((D_FF, D_MODEL)), row, row, _resident((1, D_MODEL))],
        out_specs=[pl.BlockSpec((tm, FFN_W), lambda i: (i, 0)), row, row, pl.BlockSpec((tm, D_FF), lambda i: (i, 0)),
                   pl.BlockSpec((SUBLANES, D_MODEL), lambda i: (0, 0))],
        out_shape=[jax.ShapeDtypeStruct((t, FFN_W), F32), jax.ShapeDtypeStruct((t, D_MODEL), F32),
                   jax.ShapeDtypeStruct((t, D_MODEL), BF16), jax.ShapeDtypeStruct((t, D_FF), BF16),
                   jax.ShapeDtypeStruct((SUBLANES, D_MODEL), F32)],
        scratch_shapes=[pltpu.VMEM((HALO, FFN_W), F32)],
        compiler_params=_cparams("arbitrary"),
    )(h2, wup, cw, cb, wdown, x2, tgt, w3)


def _ffn_bwd(dx3b, wdown_t, up, cw, cb):
    t = up.shape[0]
    tm = FFN_TM
    nt = t // tm
    n_ext = tm + HALO

    def rev(i):
        return nt - 1 - i

    def body(dx_ref, wdt_ref, up_ref, halo_ref, cw_ref, cb_ref, dup_ref, acc_ref, nxt_ref):
        i = pl.program_id(0)

        @pl.when(i == 0)
        def _():
            acc_ref[...] = jnp.zeros_like(acc_ref)
            nxt_ref[...] = jnp.zeros_like(nxt_ref)

        dact = _dot(dx_ref[...], wdt_ref[...])
        u, taps = _ffn_conv(up_ref, halo_ref, cw_ref, cb_ref, i == nt - 1)
        gate, upp = u[:, :D_FF], u[:, D_FF:]
        sg = _sigmoid(gate)
        du = jnp.concatenate([dact * upp * (sg * (1.0 + gate * (1.0 - sg))), dact * (gate * sg)], axis=1)
        for k in range(FFN_CONV):
            acc_ref[k:k + 1, :] += jnp.sum(du * taps[k], axis=0, keepdims=True)
        acc_ref[FFN_CONV:FFN_CONV + 1, :] += jnp.sum(du, axis=0, keepdims=True)
        ext = jnp.concatenate([du, nxt_ref[...]], axis=0)
        cw_v = cw_ref[...]
        dup = cw_v[FFN_CONV - 1:FFN_CONV, :] * du
        for k in range(FFN_CONV - 1):
            dup = dup + cw_v[k:k + 1, :] * pltpu.roll(ext, n_ext - (FFN_CONV - 1 - k), 0)[:tm]
        dup_ref[...] = dup.astype(BF16)
        nxt_ref[...] = du[0:HALO]

    wide = pl.BlockSpec((tm, FFN_W), lambda i: (rev(i), 0))
    return pl.pallas_call(
        body, name="ffn_bwd", grid=(nt,),
        in_specs=[pl.BlockSpec((tm, D_MODEL), lambda i: (rev(i), 0)), pl.BlockSpec((D_MODEL, D_FF), lambda i: (0, 0)),
                  wide, pl.BlockSpec((HALO, FFN_W), lambda i: (jnp.maximum(rev(i) * (tm // HALO) - 1, 0), 0)),
                  pl.BlockSpec((SUBLANES, FFN_W), lambda i: (0, 0)), pl.BlockSpec((1, FFN_W), lambda i: (0, 0))],
        out_specs=[wide, pl.BlockSpec((SUBLANES, FFN_W), lambda i: (0, 0))],
        out_shape=[jax.ShapeDtypeStruct((t, FFN_W), BF16), jax.ShapeDtypeStruct((SUBLANES, FFN_W), F32)],
        scratch_shapes=[pltpu.VMEM((HALO, FFN_W), F32)],
        compiler_params=_cparams("arbitrary"),
    )(dx3b, wdown_t, up, up, cw, cb)


def _adamw(parts, w, m, v, name, tr):
    r, cols = w.shape

    def body(p_ref, w_ref, m_ref, v_ref, g_ref, d_ref, mo_ref, vo_ref):
        g = p_ref[0].astype(F32)
        for s in range(1, N_DEV):
            g = g + p_ref[s].astype(F32)
        mm = ADAM_B1 * m_ref[...] + (1.0 - ADAM_B1) * g
        vv = ADAM_B2 * v_ref[...] + (1.0 - ADAM_B2) * (g * g)
        m_hat = mm / (1.0 - ADAM_B1 ** ADAM_STEP)
        v_hat = vv / (1.0 - ADAM_B2 ** ADAM_STEP)
        g_ref[...] = g
        d_ref[...] = -ADAM_LR * (m_hat / (jnp.sqrt(v_hat) + ADAM_EPS) + ADAM_WD * w_ref[...])
        mo_ref[...] = mm
        vo_ref[...] = vv

    assert r % tr == 0
    row = pl.BlockSpec((tr, cols), lambda i: (i, 0))
    return pl.pallas_call(
        body, name=name, grid=(r // tr,),
        in_specs=[pl.BlockSpec((N_DEV, tr, cols), lambda i: (0, i, 0)), row, row, row],
        out_specs=[row, row, row, row],
        out_shape=[jax.ShapeDtypeStruct((r, cols), F32)] * 4,
        compiler_params=_cparams("parallel"),
    )(parts, w, m, v)


def _mesh_pos():
    return lax.axis_index("x"), lax.axis_index("y"), lax.axis_index("c")


def _peer(pos, k):
    x, y, c = pos
    return (x ^ ((k >> 2) & 1), y ^ ((k >> 1) & 1), c ^ (k & 1))


def _flat_id(pos):
    return 4 * pos[0] + 2 * pos[1] + pos[2]


def _exchange_copies(srcs, dsts, scatter, send_sems, recv_sems, loc_sems):
    pos = _mesh_pos()
    me = _flat_id(pos)
    local, remote = [], []
    for j, (src, dst) in enumerate(zip(srcs, dsts)):
        local.append(pltpu.make_async_copy(src.at[me] if scatter[j] else src, dst.at[me], loc_sems.at[j]))
        for k in range(1, N_DEV):
            to = _peer(pos, k)
            remote.append(pltpu.make_async_remote_copy(
                src_ref=src.at[_flat_id(to)] if scatter[j] else src, dst_ref=dst.at[me],
                send_sem=send_sems.at[j, k - 1], recv_sem=recv_sems.at[j, k - 1],
                device_id=to, device_id_type=pl.DeviceIdType.MESH))
    return local, remote


def _exchange_shapes(arrays, scatter):
    return [jax.ShapeDtypeStruct(a.shape if s else (N_DEV,) + a.shape, a.dtype) for a, s in zip(arrays, scatter)]


def _exchange_sems(n):
    return [pltpu.SemaphoreType.DMA((n, N_DEV - 1)), pltpu.SemaphoreType.DMA((n, N_DEV - 1)), pltpu.SemaphoreType.DMA((n,))]


def _exchange(arrays, scatter, name):
    n = len(arrays)
    any_spec = pl.BlockSpec(memory_space=pl.ANY)

    def body(*refs):
        local, remote = _exchange_copies(refs[:n], refs[n:2 * n], scatter, *refs[2 * n:])
        for cp in local + remote:
            cp.start()
        for cp in remote:
            cp.wait()
        for cp in local:
            cp.wait()

    return pl.pallas_call(
        body, name=name, in_specs=[any_spec] * n, out_specs=[any_spec] * n,
        out_shape=_exchange_shapes(arrays, scatter), scratch_shapes=_exchange_sems(n),
    )(*arrays)


def _pad_rows(a, rows):
    return jnp.pad(a, ((0, rows - a.shape[0]),) + ((0, 0),) * (a.ndim - 1))


PACK_UNIT = SUBLANES * LANES


def _pack_lanes(parts, rows):
    out = []
    for a in parts:
        f = a.reshape(-1)
        out.append(jnp.pad(f, (0, (-f.shape[0]) % PACK_UNIT)).reshape(-1, LANES))
    packed = jnp.concatenate(out, axis=0)
    assert packed.shape[0] == rows, (packed.shape, rows)
    return packed


def _unpack_lanes(buf, shapes):
    out, r0 = [], 0
    for shp in shapes:
        n = math.prod(shp)
        nr = -(-n // PACK_UNIT) * SUBLANES
        out.append(buf[r0:r0 + nr].reshape(-1)[:n].reshape(shp))
        r0 += nr
    return out


def _col_shards(g):
    r, n = g.shape
    return g.reshape(r, N_DEV, n // N_DEV).transpose(1, 0, 2)


def _col_unshard(s):
    _, r, w = s.shape
    return s.transpose(1, 0, 2).reshape(r, N_DEV * w)


def _lane_rows(flat):
    n = flat.shape[1]
    return jnp.pad(flat, ((0, 0), (0, (-n) % PACK_UNIT))).reshape(N_DEV, -1, LANES)


SMALL_ROWS = 128
WS_ROWS = 32


def kernel(x, norm_mix_w, w_in, conv_qkv_w, a_log, dt_bias, gdn_norm_w, w_branch_a, w_branch_b, rel_bias, w_out, norm_ffn_w, w_up, conv_ffn_w, conv_ffn_b, w_down, norm_final_w, loss_target, m_norm_mix_w, m_w_in, m_conv_qkv_w, m_a_log, m_dt_bias, m_gdn_norm_w, m_w_branch_a, m_w_branch_b, m_rel_bias, m_w_out, m_norm_ffn_w, m_w_up, m_conv_ffn_w, m_conv_ffn_b, m_w_down, m_norm_final_w, v_norm_mix_w, v_w_in, v_conv_qkv_w, v_a_log, v_dt_bias, v_gdn_norm_w, v_w_branch_a, v_w_branch_b, v_rel_bias, v_w_out, v_norm_ffn_w, v_w_up, v_conv_ffn_w, v_conv_ffn_b, v_w_down, v_norm_final_w):
    big_w = (w_in, w_branch_a, w_branch_b, w_out, w_up, w_down, conv_qkv_w, conv_ffn_w)
    big_m = (m_w_in, m_w_branch_a, m_w_branch_b, m_w_out, m_w_up, m_w_down, m_conv_qkv_w, m_conv_ffn_w)
    big_v = (v_w_in, v_w_branch_a, v_w_branch_b, v_w_out, v_w_up, v_w_down, v_conv_qkv_w, v_conv_ffn_w)
    small_w = (norm_mix_w, a_log, dt_bias, gdn_norm_w, rel_bias, norm_ffn_w, conv_ffn_b, norm_final_w)
    small_m = (m_norm_mix_w, m_a_log, m_dt_bias, m_gdn_norm_w, m_rel_bias, m_norm_ffn_w, m_conv_ffn_b, m_norm_final_w)
    small_v = (v_norm_mix_w, v_a_log, v_dt_bias, v_gdn_norm_w, v_rel_bias, v_norm_ffn_w, v_conv_ffn_b, v_norm_final_w)

    xs, tgt = x[0], loss_target[0]
    ws = _pack_lanes(big_w[6:], WS_ROWS)
    h1, g_in, gs = _rmsnorm_cast(xs, norm_mix_w, "norm_mix", carry=([w_in[0].astype(BF16), ws], (False, False)))
    win = _col_unshard(g_in)
    gs = gs.reshape(N_DEV, -1)
    cqkv = gs[:, :GDN_CONV * 192].reshape(N_DEV, GDN_CONV, 192).transpose(1, 0, 2).reshape(GDN_CONV, 3 * KEY_A)
    cffn = gs[:, PACK_UNIT:PACK_UNIT + FFN_CONV * 704].reshape(N_DEV, FFN_CONV, 704).transpose(1, 0, 2).reshape(FFN_CONV, FFN_W)
    cffn = _pad_rows(cffn, SUBLANES)
    w_all = jnp.concatenate([win[:, a:b] for a, b in W_IN_ORDER] + [jnp.zeros((D_MODEL, PROJ_W - D_IN), BF16)], axis=1)
    par = _pad_rows(jnp.pad(jnp.concatenate([a_log, dt_bias], axis=0), ((0, 0), (0, LANES - GDN_HEADS))), SUBLANES)
    table = jnp.pad(rel_bias[0], ((0, 0), (0, 3 * LANES - rel_bias.shape[-1]))).reshape(ATT_HEADS, 1, 3 * LANES)

    proj, g_ba, g_bb, g_out, g_up, g_down = _mm_nn(
        h1, w_all, F32, "in_proj", 2 * MM_TM, 1152, D_MODEL, carry=([w[0].astype(BF16) for w in big_w[1:6]], (False,) * 5))
    wba, wbb, wup = _col_unshard(g_ba), _col_unshard(g_bb), _col_unshard(g_up)
    wout = g_out.reshape(D_MODEL, D_MODEL)
    wdown = g_down.reshape(D_FF, D_MODEL)
    qn, kn, va = _gdn_prep_fwd(proj, cqkv)
    oan, o_gdn, sprev, wst, ust, tst = _gdn_fwd(qn, kn, va, proj, par, gdn_norm_w)
    bias_q, bias_k = _att_bias(table)
    ob, lse, lse_t = _att_fwd(proj, bias_q)
    x2 = _merge_fwd(oan, ob, proj, xs, wba, wbb, wout)
    h2 = _rmsnorm_cast(x2, norm_ffn_w, "norm_ffn")
    up, dx3, dx3b, act, tail_sums = _ffn_fwd(h2, wup, cffn, conv_ffn_b, wdown, x2, tgt, norm_final_w.reshape(1, D_MODEL))

    g_wdown = _mm_tn(act, dx3b, "dw_down", 512)
    dup, ffn_sums = _ffn_bwd(dx3b, wdown.T, up, cffn, conv_ffn_b)
    g_wup = _mm_tn(h2, dup, "dw_up", 1408)
    dx2, dx2b, nffn_sums, r_up, r_down = _mm_rms_bwd(dup, wup.T, x2, norm_ffn_w, dx3, "ffn_up_bwd", MM_TM, 1408, carry=(
        [_col_shards(g_wup).astype(BF16), g_wdown.reshape(N_DEV, -1, D_MODEL).astype(BF16)], (True, True)))
    dproj, d_oan, d_ob, g_wout, g_wba, g_wbb = _merge_bwd(dx2b, oan, ob, proj, wba, wbb, wout.T, wba.T, wbb.T)
    dproj, dlt_t, slabs = _att_dq(proj, bias_q, lse, d_ob, dproj)
    dproj = _att_dkv(proj, bias_k, lse_t, dlt_t, d_ob, dproj)
    g_rel = _relbias_grad(slabs)[:, 0, :rel_bias.shape[-1]]
    dqn, dkn, dva, dproj, gdn_sums = _gdn_bwd(qn, kn, va, proj, par, gdn_norm_w, o_gdn, sprev, wst, ust, tst, d_oan, dproj)
    dproj, cq_sums = _gdn_prep_bwd(proj, cqkv, dqn, dkn, dva, dproj)
    g_wall = _mm_tn(h1, dproj, "dw_in", 1152)
    starts = np.cumsum([0] + [b - a for a, b in W_IN_ORDER])
    g_win = jnp.concatenate([g_wall[:, starts[i]:starts[i + 1]] for i in np.argsort([a for a, _ in W_IN_ORDER])], axis=1)
    g_conv = jnp.concatenate([_lane_rows(_col_shards(cq_sums[:GDN_CONV]).reshape(N_DEV, -1)),
                              _lane_rows(_col_shards(ffn_sums[:FFN_CONV]).reshape(N_DEV, -1))], axis=1)
    grad_x, _, nmix_sums, r_in, r_ba, r_bb, r_out, r_conv = _mm_rms_bwd(
        dproj, w_all.T, xs, norm_mix_w, dx2, "in_proj_bwd", MM_TM, 1152, carry=(
            [_col_shards(g_win).astype(BF16), _col_shards(g_wba).astype(BF16), _col_shards(g_wbb).astype(BF16),
             g_wout.reshape(N_DEV, -1, D_MODEL).astype(BF16), g_conv], (True,) * 5))

    small_g = (nmix_sums[0:1], gdn_sums[0:1, :GDN_HEADS], gdn_sums[1:2, :GDN_HEADS], gdn_sums[2:3], g_rel,
               nffn_sums[0:1], ffn_sums[FFN_CONV:FFN_CONV + 1], tail_sums[0:1], tail_sums[1:2, 0:1])
    r_small, = _exchange([_pack_lanes(small_g, SMALL_ROWS)], (False,), "all_gather_small_grads")
    recv = (r_in, r_ba, r_bb, r_out, r_up, r_down, r_conv, r_small)

    res = {}
    for i, (nm, tr) in enumerate((("w_in", 128), ("w_branch_a", KEY_A), ("w_branch_b", WIDTH_B), ("w_out", 128),
                                  ("w_up", 128), ("w_down", 176))):
        res[nm] = [o[None] for o in _adamw(recv[i], big_w[i][0], big_m[i][0], big_v[i][0], "adamw_" + nm, tr)]
    conv = _adamw(recv[6], _pack_lanes(big_w[6:], WS_ROWS), _pack_lanes(big_m[6:], WS_ROWS), _pack_lanes(big_v[6:], WS_ROWS),
                  "adamw_conv", WS_ROWS)
    conv = [_unpack_lanes(o, [w.shape for w in big_w[6:]]) for o in conv]
    res["conv_qkv_w"] = [o[0] for o in conv]
    res["conv_ffn_w"] = [o[1] for o in conv]
    small_shapes = [w.shape for w in small_w]
    zero = jnp.zeros((1,), F32)
    small = _adamw(recv[7], _pack_lanes(small_w + (zero,), SMALL_ROWS), _pack_lanes(small_m + (zero,), SMALL_ROWS),
                   _pack_lanes(small_v + (zero,), SMALL_ROWS), "adamw_replicated", SMALL_ROWS)
    small = [_unpack_lanes(o, small_shapes + [()]) for o in small]
    loss = small[0][-1]
    for j, nm in enumerate(("norm_mix_w", "a_log", "dt_bias", "gdn_norm_w", "rel_bias", "norm_ffn_w", "conv_ffn_b",
                            "norm_final_w")):
        res[nm] = [o[j] for o in small]

    names = ("norm_mix_w", "w_in", "conv_qkv_w", "a_log", "dt_bias", "gdn_norm_w", "w_branch_a", "w_branch_b", "rel_bias",
             "w_out", "norm_ffn_w", "w_up", "conv_ffn_w", "conv_ffn_b", "w_down", "norm_final_w")
    outs = [res[n][kind] for kind in range(4) for n in names]
    return (loss, grad_x[None], *outs)
```

```python
import functools
import math

import numpy as np
import jax
import jax.numpy as jnp
from jax import lax
from jax.experimental import pallas as pl
from jax.experimental.pallas import tpu as pltpu

F32, BF16 = jnp.float32, jnp.bfloat16
HIGHEST = lax.Precision.HIGHEST

N_DEV = 8
D_MODEL = 1024
CHUNK = 64
EPS = 1e-6
GDN_HEADS, GDN_DK = 4, 128
KEY_A = GDN_HEADS * GDN_DK
GDN_CONV = 4
ATT_HEADS, ATT_DH = 8, 64
WIDTH_B = ATT_HEADS * ATT_DH
ATT_BAND = 9
REL_CLIP = 128
D_FF = 2816
FFN_CONV = 3
D_IN = 5640
ADAM_LR, ADAM_B1, ADAM_B2, ADAM_EPS, ADAM_WD, ADAM_STEP = 0.001, 0.9, 0.999, 1e-08, 0.01, 10

LANES = 128
SUBLANES = 8
NEG = -1e30

PROJ_W = 5760
PB = 512
CB_GA, CB_GB = 0, 1
CB_KB, CB_VB, CB_QA, CB_KA, CB_VA, CB_QB, CB_ZA = 4, 5, 6, 7, 8, 9, 10
CB_BD = 44
DP_GATES, DP_KVB, DP_QKVA, DP_QB, DP_ZBD = (2048, 0), (1024, 2), (1536, 2), (512, 9), (640, 8)
W_IN_ORDER = ((3592, 5640), (2568, 3592), (0, 1536), (2056, 2568), (1536, 2048), (2048, 2056))

ATT_QB = 256
ATT_KW = 768
ATT_VEC = 1024


def _dot(a, b, precision=None):
    return jnp.dot(a, b, preferred_element_type=F32, precision=precision)


def _dot_nt(a, b, precision=None):
    return lax.dot_general(a, b, (((1,), (1,)), ((), ())), preferred_element_type=F32, precision=precision)


def _dot_tn(a, b):
    return lax.dot_general(a, b, (((0,), (0,)), ((), ())), preferred_element_type=F32)


def _split(a):
    hi = a.astype(BF16)
    return hi, (a - hi.astype(F32)).astype(BF16)


def _dot3s(a, b):
    return _dot(a[0], b[0]) + (_dot(a[0], b[1]) + _dot(a[1], b[0]))


def _sigmoid(x):
    return 0.5 * jnp.tanh(0.5 * x) + 0.5


def _softplus(x):
    return jnp.maximum(x, 0.0) + jnp.log(1.0 + jnp.exp(-jnp.abs(x)))


def _cparams(*sem):
    return pltpu.CompilerParams(dimension_semantics=tuple(sem))


def _dp_spec(tm, region, index=lambda i: i):
    width, cb = region
    return pl.BlockSpec((tm, width), lambda i: (index(i), cb))


def _rmsnorm_cast(x, w, name, tm=512, carry=((), ())):
    t, d = x.shape
    nt = t // tm
    arrays, scatter = carry
    nx = len(arrays)

    def body(*refs):
        x_ref, w_ref = refs[:2]
        o_ref = refs[2 + nx]
        i = pl.program_id(0)
        if nx:
            local, remote = _exchange_copies(refs[2:2 + nx], refs[3 + nx:3 + 2 * nx], scatter, *refs[3 + 2 * nx:])

            @pl.when(i == 0)
            def _():
                for cp in local + remote:
                    cp.start()

        xv = x_ref[...]
        r = lax.rsqrt(jnp.mean(xv * xv, axis=-1, keepdims=True) + EPS)
        o_ref[...] = (xv * r * w_ref[...]).astype(BF16)

        if nx:
            @pl.when(i == nt - 1)
            def _():
                for cp in remote + local:
                    cp.wait()

    any_spec = pl.BlockSpec(memory_space=pl.ANY)
    out = pl.pallas_call(
        body, name=name, grid=(nt,),
        in_specs=[pl.BlockSpec((tm, d), lambda i: (i, 0)), pl.BlockSpec((1, d), lambda i: (0, 0))] + [any_spec] * nx,
        out_specs=[pl.BlockSpec((tm, d), lambda i: (i, 0))] + [any_spec] * nx,
        out_shape=[jax.ShapeDtypeStruct((t, d), BF16)] + _exchange_shapes(arrays, scatter),
        scratch_shapes=_exchange_sems(nx) if nx else [],
        compiler_params=_cparams("arbitrary" if nx else "parallel"),
    )(x, w, *arrays)
    return out if nx else out[0]


def _mm_nn(a, b, out_dtype, name, tm, tn, tk, carry=((), ())):
    m, k = a.shape
    _, n = b.shape
    tm = min(tm, m)
    nk = k // tk
    assert m % tm == 0 and n % tn == 0 and k % tk == 0
    arrays, scatter = carry
    nx = len(arrays)
    gm, gn = m // tm, n // tn

    def body(*refs):
        a_ref, b_ref = refs[:2]
        srcs = refs[2:2 + nx]
        o_ref = refs[2 + nx]
        dsts = refs[3 + nx:3 + 2 * nx]
        rest = refs[3 + 2 * nx:]
        i, j, kk = pl.program_id(0), pl.program_id(1), pl.program_id(2)
        if nx:
            local, remote = _exchange_copies(srcs, dsts, scatter, *rest[-3:])

            @pl.when((i == 0) & (j == 0) & (kk == 0))
            def _():
                for cp in local + remote:
                    cp.start()

        if nk == 1:
            o_ref[...] = _dot(a_ref[...], b_ref[...]).astype(out_dtype)
        else:
            acc_ref = rest[0]

            @pl.when(kk == 0)
            def _():
                acc_ref[...] = jnp.zeros_like(acc_ref)

            acc_ref[...] += _dot(a_ref[...], b_ref[...])

            @pl.when(kk == nk - 1)
            def _():
                o_ref[...] = acc_ref[...].astype(out_dtype)

        if nx:
            @pl.when((i == gm - 1) & (j == gn - 1) & (kk == nk - 1))
            def _():
                for cp in remote + local:
                    cp.wait()

    any_spec = pl.BlockSpec(memory_space=pl.ANY)
    scratch = ([pltpu.VMEM((tm, tn), F32)] if nk > 1 else []) + (_exchange_sems(nx) if nx else [])
    out = pl.pallas_call(
        body, name=name, grid=(gm, gn, nk),
        in_specs=[pl.BlockSpec((tm, tk), lambda i, j, kk: (i, kk)),
                  pl.BlockSpec((tk, tn), lambda i, j, kk: (kk, j))] + [any_spec] * nx,
        out_specs=[pl.BlockSpec((tm, tn), lambda i, j, kk: (i, j))] + [any_spec] * nx,
        out_shape=[jax.ShapeDtypeStruct((m, n), out_dtype)] + _exchange_shapes(arrays, scatter),
        scratch_shapes=scratch,
        compiler_params=_cparams(*(("arbitrary",) * 3 if nx else ("parallel", "parallel", "arbitrary"))),
    )(a, b, *arrays)
    return out if nx else out[0]


def _mm_rms_bwd(a, b, x, w, dres, name, tm, tk, carry):
    m, k = a.shape
    _, n = b.shape
    nk = k // tk
    gm = m // tm
    assert m % tm == 0 and k % tk == 0
    arrays, scatter = carry
    nx = len(arrays)

    def body(*refs):
        a_ref, b_ref, x_ref, w_ref, dres_ref = refs[:5]
        srcs = refs[5:5 + nx]
        dx_ref, dxb_ref, dw_ref = refs[5 + nx:8 + nx]
        dsts = refs[8 + nx:8 + 2 * nx]
        acc_ref = refs[8 + 2 * nx]
        i, kk = pl.program_id(0), pl.program_id(1)
        local, remote = _exchange_copies(srcs, dsts, scatter, *refs[9 + 2 * nx:])

        @pl.when((i == 0) & (kk == 0))
        def _():
            for cp in local + remote:
                cp.start()
            dw_ref[...] = jnp.zeros_like(dw_ref)

        @pl.when(kk == 0)
        def _():
            acc_ref[...] = jnp.zeros_like(acc_ref)

        acc_ref[...] += _dot(a_ref[...], b_ref[...])

        @pl.when(kk == nk - 1)
        def _():
            dhv = acc_ref[...]
            xv = x_ref[...]
            r = lax.rsqrt(jnp.mean(xv * xv, axis=-1, keepdims=True) + EPS)
            xh = xv * r
            dw_ref[0:1, :] += jnp.sum(dhv * xh, axis=0, keepdims=True)
            dxh = dhv * w_ref[...]
            dx = dres_ref[...] + r * (dxh - xh * jnp.mean(dxh * xh, axis=-1, keepdims=True))
            dx_ref[...] = dx
            dxb_ref[...] = dx.astype(BF16)

        @pl.when((i == gm - 1) & (kk == nk - 1))
        def _():
            for cp in remote + local:
                cp.wait()

    any_spec = pl.BlockSpec(memory_space=pl.ANY)
    row = pl.BlockSpec((tm, n), lambda i, kk: (i, 0))
    return pl.pallas_call(
        body, name=name, grid=(gm, nk),
        in_specs=[pl.BlockSpec((tm, tk), lambda i, kk: (i, kk)), pl.BlockSpec((tk, n), lambda i, kk: (kk, 0)),
                  row, pl.BlockSpec((1, n), lambda i, kk: (0, 0)), row] + [any_spec] * nx,
        out_specs=[row, row, pl.BlockSpec((SUBLANES, n), lambda i, kk: (0, 0))] + [any_spec] * nx,
        out_shape=[jax.ShapeDtypeStruct((m, n), F32), jax.ShapeDtypeStruct((m, n), BF16),
                   jax.ShapeDtypeStruct((SUBLANES, n), F32)] + _exchange_shapes(arrays, scatter),
        scratch_shapes=[pltpu.VMEM((tm, n), F32)] + _exchange_sems(nx),
        compiler_params=_cparams("arbitrary", "arbitrary"),
    )(a, b, x, w, dres, *arrays)


MM_TM = 1024


def _mm_tn(a, b, name, tn, tk=2 * MM_TM, carry=((), ())):
    t, m = a.shape
    _, n = b.shape
    tk = min(tk, t)
    assert t % tk == 0 and n % tn == 0
    gn, gs = n // tn, t // tk
    arrays, scatter = carry
    nx = len(arrays)

    def body(*refs):
        a_ref, b_ref = refs[:2]
        o_ref = refs[2 + nx]
        j, s = pl.program_id(0), pl.program_id(1)
        if nx:
            local, remote = _exchange_copies(refs[2:2 + nx], refs[3 + nx:3 + 2 * nx], scatter, *refs[3 + 2 * nx:])

            @pl.when((j == 0) & (s == 0))
            def _():
                for cp in local + remote:
                    cp.start()

        @pl.when(s == 0)
        def _():
            o_ref[...] = jnp.zeros_like(o_ref)

        o_ref[...] += _dot_tn(a_ref[...], b_ref[...])

        if nx:
            @pl.when((j == gn - 1) & (s == gs - 1))
            def _():
                for cp in remote + local:
                    cp.wait()

    any_spec = pl.BlockSpec(memory_space=pl.ANY)
    out = pl.pallas_call(
        body, name=name, grid=(gn, gs),
        in_specs=[pl.BlockSpec((tk, m), lambda j, s: (s, 0)),
                  pl.BlockSpec((tk, tn), lambda j, s: (s, j))] + [any_spec] * nx,
        out_specs=[pl.BlockSpec((m, tn), lambda j, s: (0, j))] + [any_spec] * nx,
        out_shape=[jax.ShapeDtypeStruct((m, n), F32)] + _exchange_shapes(arrays, scatter),
        scratch_shapes=_exchange_sems(nx) if nx else [],
        compiler_params=_cparams(*(("arbitrary", "arbitrary") if nx else ("parallel", "arbitrary"))),
    )(a, b, *arrays)
    return out if nx else out[0]


def _rel_index(dist):
    return np.clip(dist, -REL_CLIP, REL_CLIP) + REL_CLIP


def _bias_onehots():
    tw = 3 * LANES
    m = np.arange(ATT_VEC)
    dq = np.where(m <= ATT_KW, 512 - m, 512 - (m - ATT_VEC))
    dk = np.where(m < ATT_KW, m, m - ATT_VEC)
    ohq = np.zeros((tw, ATT_VEC), np.float32)
    ohk = np.zeros((tw, ATT_VEC), np.float32)
    ohq[_rel_index(dq), m] = 1.0
    ohk[_rel_index(dk), m] = 1.0
    return ohq, ohk


def _att_bias(table_pad):
    ohq, ohk = _bias_onehots()
    nslab = ATT_QB // SUBLANES

    def body(t_ref, ohq_ref, ohk_ref, bq_ref, bk_ref):
        tv = jnp.broadcast_to(t_ref[...], (SUBLANES, 3 * LANES))
        lane = lax.broadcasted_iota(jnp.int32, (ATT_QB, ATT_KW), 1)
        row = lax.broadcasted_iota(jnp.int32, (ATT_QB, ATT_KW), 0) // CHUNK
        col = lane // CHUNK
        band = (col >= row) & (col <= row + ATT_BAND - 1)
        for which, (oh_ref, out_ref) in enumerate(((ohq_ref, bq_ref), (ohk_ref, bk_ref))):
            vec = _dot(tv, oh_ref[...], HIGHEST)[0:1, :]
            slab = jnp.concatenate([vec if b == 0 else pltpu.roll(vec, b, 1) for b in range(SUBLANES)], axis=0)
            rows = [slab if a == 0 else pltpu.roll(slab, SUBLANES * a, 1) for a in range(nslab)]
            full = jnp.concatenate(rows, axis=0)[:, :ATT_KW]
            for v in range(3):
                inside = (lane >= (2 - v) * ATT_QB) if which == 0 else (lane < (v + 1) * ATT_QB)
                out_ref[v] = jnp.where(band & inside, full, NEG)

    h = table_pad.shape[0]
    oh_spec = pl.BlockSpec((3 * LANES, ATT_VEC), lambda i: (0, 0))
    out_spec = pl.BlockSpec((3, None, ATT_QB, ATT_KW), lambda i: (0, i, 0, 0))
    return pl.pallas_call(
        body, name="att_bias", grid=(h,),
        in_specs=[pl.BlockSpec((None, 1, 3 * LANES), lambda i: (i, 0, 0)), oh_spec, oh_spec],
        out_specs=[out_spec, out_spec],
        out_shape=[jax.ShapeDtypeStruct((3, h, ATT_QB, ATT_KW), F32)] * 2,
        compiler_params=_cparams("parallel"),
    )(table_pad, jnp.asarray(ohq), jnp.asarray(ohk))


def _head_masks():
    lane = lax.broadcasted_iota(jnp.int32, (1, LANES), 1)
    return [lane < ATT_DH, lane >= ATT_DH]


def _att_fwd(proj, bias_q):
    t = proj.shape[0]
    nb = t // ATT_QB
    scale = ATT_DH ** -0.5

    def body(q_ref, k0_ref, k1_ref, k2_ref, v0_ref, v1_ref, v2_ref, b_ref, o_ref, lse_ref, lset_ref):
        i = pl.program_id(0)
        q = (q_ref[...] * scale).astype(BF16)
        kk = jnp.concatenate([k0_ref[...], k1_ref[...], k2_ref[...]], axis=0).astype(BF16)
        vv = jnp.concatenate([v0_ref[...], v1_ref[...], v2_ref[...]], axis=0).astype(BF16)
        lane = lax.broadcasted_iota(jnp.int32, (1, LANES), 1)
        masks = _head_masks()
        lse_cols = jnp.zeros((ATT_QB, LANES), F32)
        for p in range(ATT_HEADS // 2):
            cs = slice(p * LANES, (p + 1) * LANES)
            qt, kt, vt = q[:, cs], kk[:, cs], vv[:, cs]
            acc = jnp.zeros((ATT_QB, LANES), F32)
            for sub in range(2):
                h = 2 * p + sub
                s = _dot_nt(jnp.where(masks[sub], qt, 0), kt) + b_ref[h]
                mx = jnp.max(s, axis=-1, keepdims=True)
                e = jnp.exp(s - mx)
                l = jnp.sum(e, axis=-1, keepdims=True)
                acc = acc + _dot(e.astype(BF16), jnp.where(masks[sub], vt, 0)) * (1.0 / l)
                lse_cols = lse_cols + jnp.where(lane == h, mx + jnp.log(l), 0.0)
            o_ref[:, cs] = acc.astype(BF16)
        lse_ref[...] = lse_cols
        lset_ref[...] = lse_cols.T[0:SUBLANES, :]

    def kv_spec(off, cb):
        return pl.BlockSpec((ATT_QB, PB), lambda i: (jnp.maximum(i + off, 0), cb))

    return pl.pallas_call(
        body, name="att_fwd", grid=(nb,),
        in_specs=[pl.BlockSpec((ATT_QB, PB), lambda i: (i, CB_QB)),
                  kv_spec(-2, CB_KB), kv_spec(-1, CB_KB), kv_spec(0, CB_KB),
                  kv_spec(-2, CB_VB), kv_spec(-1, CB_VB), kv_spec(0, CB_VB),
                  pl.BlockSpec((None, ATT_HEADS, ATT_QB, ATT_KW), lambda i: (jnp.minimum(i, 2), 0, 0, 0))],
        out_specs=[pl.BlockSpec((ATT_QB, WIDTH_B), lambda i: (i, 0)),
                   pl.BlockSpec((ATT_QB, LANES), lambda i: (i, 0)),
                   pl.BlockSpec((SUBLANES, ATT_QB), lambda i: (0, i))],
        out_shape=[jax.ShapeDtypeStruct((t, WIDTH_B), BF16), jax.ShapeDtypeStruct((t, LANES), F32),
                   jax.ShapeDtypeStruct((SUBLANES, t), F32)],
        compiler_params=_cparams("parallel"),
    )(proj, proj, proj, proj, proj, proj, proj, bias_q)


def _att_dq(proj, bias_q, lse, d_ob, dproj):
    t = proj.shape[0]
    nb = t // ATT_QB
    scale = ATT_DH ** -0.5
    nslab = ATT_QB // SUBLANES

    def body(q_ref, k0_ref, k1_ref, k2_ref, v0_ref, v1_ref, v2_ref, b_ref, lse_ref, do_ref, dp_in_ref,
             dq_ref, dlt_ref, slab_ref):
        i = pl.program_id(0)

        @pl.when(i == 0)
        def _():
            slab_ref[...] = jnp.zeros_like(slab_ref)

        q = (q_ref[...] * scale).astype(BF16)
        kk = jnp.concatenate([k0_ref[...], k1_ref[...], k2_ref[...]], axis=0).astype(BF16)
        vv = jnp.concatenate([v0_ref[...], v1_ref[...], v2_ref[...]], axis=0).astype(BF16)
        do = do_ref[...].astype(BF16)
        lane = lax.broadcasted_iota(jnp.int32, (1, LANES), 1)
        masks = _head_masks()
        lse_all = lse_ref[...]
        dlt_cols = jnp.zeros((ATT_QB, LANES), F32)
        zpad = jnp.zeros((SUBLANES, ATT_VEC - ATT_KW), F32)
        for p in range(ATT_HEADS // 2):
            cs = slice(p * LANES, (p + 1) * LANES)
            qt, kt, vt, dot_ = q[:, cs], kk[:, cs], vv[:, cs], do[:, cs]
            acc = jnp.zeros((ATT_QB, LANES), F32)
            for sub in range(2):
                h = 2 * p + sub
                s = _dot_nt(jnp.where(masks[sub], qt, 0), kt) + b_ref[h]
                pr = jnp.exp(s - lse_all[:, h:h + 1])
                dp = _dot_nt(jnp.where(masks[sub], dot_, 0), vt)
                dl = jnp.sum(pr * dp, axis=-1, keepdims=True)
                ds = pr * (dp - dl)
                acc = acc + _dot(ds.astype(BF16), jnp.where(masks[sub], kt, 0)) * scale
                dlt_cols = dlt_cols + jnp.where(lane == h, dl, 0.0)
                sl = jnp.zeros((SUBLANES, ATT_VEC), F32)
                for a in range(nslab):
                    piece = jnp.concatenate([ds[a * SUBLANES:(a + 1) * SUBLANES, :], zpad], axis=1)
                    sl = sl + (piece if a == 0 else pltpu.roll(piece, ATT_VEC - SUBLANES * a, 1))
                slab_ref[h] += sl
            dq_ref[:, cs] = acc.astype(BF16)
        dlt_ref[...] = dlt_cols.T[0:SUBLANES, :]

    def kv_spec(off, cb):
        return pl.BlockSpec((ATT_QB, PB), lambda i: (jnp.maximum(i + off, 0), cb))

    return pl.pallas_call(
        body, name="att_dq", grid=(nb,),
        in_specs=[pl.BlockSpec((ATT_QB, PB), lambda i: (i, CB_QB)),
                  kv_spec(-2, CB_KB), kv_spec(-1, CB_KB), kv_spec(0, CB_KB),
                  kv_spec(-2, CB_VB), kv_spec(-1, CB_VB), kv_spec(0, CB_VB),
                  pl.BlockSpec((None, ATT_HEADS, ATT_QB, ATT_KW), lambda i: (jnp.minimum(i, 2), 0, 0, 0)),
                  pl.BlockSpec((ATT_QB, LANES), lambda i: (i, 0)),
                  pl.BlockSpec((ATT_QB, WIDTH_B), lambda i: (i, 0)), pl.BlockSpec(memory_space=pl.ANY)],
        out_specs=[_dp_spec(ATT_QB, DP_QB),
                   pl.BlockSpec((SUBLANES, ATT_QB), lambda i: (0, i)),
                   pl.BlockSpec((ATT_HEADS, SUBLANES, ATT_VEC), lambda i: (0, 0, 0))],
        out_shape=[jax.ShapeDtypeStruct(dproj.shape, dproj.dtype), jax.ShapeDtypeStruct((SUBLANES, t), F32),
                   jax.ShapeDtypeStruct((ATT_HEADS, SUBLANES, ATT_VEC), F32)],
        input_output_aliases={10: 0},
        compiler_params=_cparams("arbitrary"),
    )(proj, proj, proj, proj, proj, proj, proj, bias_q, lse, d_ob, dproj)


def _att_dkv(proj, bias_k, lse_t, dlt_t, d_ob, dproj):
    t = proj.shape[0]
    nb = t // ATT_QB
    scale = ATT_DH ** -0.5

    def body(k_ref, v_ref, q0_ref, q1_ref, q2_ref, d0_ref, d1_ref, d2_ref, l0_ref, l1_ref, l2_ref,
             e0_ref, e1_ref, e2_ref, b_ref, dp_in_ref, dkv_ref):
        i = pl.program_id(0)
        k = k_ref[...].astype(BF16)
        v = v_ref[...].astype(BF16)
        qq = (jnp.concatenate([q0_ref[...], q1_ref[...], q2_ref[...]], axis=0) * scale).astype(BF16)
        do = jnp.concatenate([d0_ref[...], d1_ref[...], d2_ref[...]], axis=0).astype(BF16)
        lse = jnp.concatenate([l0_ref[...], l1_ref[...], l2_ref[...]], axis=1)
        dlt = jnp.concatenate([e0_ref[...], e1_ref[...], e2_ref[...]], axis=1)
        masks = _head_masks()
        for p in range(ATT_HEADS // 2):
            cs = slice(p * LANES, (p + 1) * LANES)
            kt, vt, qt, dot_ = k[:, cs], v[:, cs], qq[:, cs], do[:, cs]
            acc_k = jnp.zeros((ATT_QB, LANES), F32)
            acc_v = jnp.zeros((ATT_QB, LANES), F32)
            for sub in range(2):
                h = 2 * p + sub
                st = _dot_nt(jnp.where(masks[sub], kt, 0), qt) + b_ref[h]
                pt = jnp.exp(st - lse[h:h + 1, :])
                dot_m = jnp.where(masks[sub], dot_, 0)
                acc_v = acc_v + _dot(pt.astype(BF16), dot_m)
                dpt = _dot_nt(jnp.where(masks[sub], vt, 0), dot_)
                dst = pt * (dpt - dlt[h:h + 1, :])
                acc_k = acc_k + _dot(dst.astype(BF16), jnp.where(masks[sub], qt, 0))
            dkv_ref[:, cs] = acc_k.astype(BF16)
            dkv_ref[:, WIDTH_B + p * LANES:WIDTH_B + (p + 1) * LANES] = acc_v.astype(BF16)

    def q_spec(off, cb):
        return pl.BlockSpec((ATT_QB, PB), lambda i: (jnp.minimum(i + off, nb - 1), cb))

    def d_spec(off):
        return pl.BlockSpec((ATT_QB, WIDTH_B), lambda i: (jnp.minimum(i + off, nb - 1), 0))

    def r_spec(off):
        return pl.BlockSpec((SUBLANES, ATT_QB), lambda i: (0, jnp.minimum(i + off, nb - 1)))

    row = pl.BlockSpec((ATT_QB, WIDTH_B), lambda i: (i, 0))
    return pl.pallas_call(
        body, name="att_dkv", grid=(nb,),
        in_specs=[pl.BlockSpec((ATT_QB, PB), lambda i: (i, CB_KB)), pl.BlockSpec((ATT_QB, PB), lambda i: (i, CB_VB)),
                  q_spec(0, CB_QB), q_spec(1, CB_QB), q_spec(2, CB_QB),
                  d_spec(0), d_spec(1), d_spec(2), r_spec(0), r_spec(1), r_spec(2),
                  r_spec(0), r_spec(1), r_spec(2),
                  pl.BlockSpec((None, ATT_HEADS, ATT_QB, ATT_KW), lambda i: (jnp.minimum(nb - 1 - i, 2), 0, 0, 0)),
                  pl.BlockSpec(memory_space=pl.ANY)],
        out_specs=_dp_spec(ATT_QB, DP_KVB),
        out_shape=jax.ShapeDtypeStruct(dproj.shape, dproj.dtype),
        input_output_aliases={15: 0},
        compiler_params=_cparams("parallel"),
    )(proj, proj, proj, proj, proj, d_ob, d_ob, d_ob, lse_t, lse_t, lse_t, dlt_t, dlt_t, dlt_t, bias_k, dproj)


def _relbias_grad(slabs):
    ohq, _ = _bias_onehots()

    def body(s_ref, oh_ref, o_ref):
        sv = s_ref[...]
        vec = sv[0:1, :]
        for b in range(1, SUBLANES):
            vec = vec + pltpu.roll(sv[b:b + 1, :], ATT_VEC - b, 1)
        o_ref[...] = _dot_nt(jnp.broadcast_to(vec, (SUBLANES, ATT_VEC)), oh_ref[...], HIGHEST)[0:1, :]

    h = slabs.shape[0]
    return pl.pallas_call(
        body, name="att_dbias", grid=(h,),
        in_specs=[pl.BlockSpec((None, SUBLANES, ATT_VEC), lambda i: (i, 0, 0)),
                  pl.BlockSpec((3 * LANES, ATT_VEC), lambda i: (0, 0))],
        out_specs=pl.BlockSpec((None, 1, 3 * LANES), lambda i: (i, 0, 0)),
        out_shape=jax.ShapeDtypeStruct((h, 1, 3 * LANES), F32),
        compiler_params=_cparams("parallel"),
    )(slabs, jnp.asarray(ohq))


GDN_TM = 512
GDN_CB = 4
HALO = SUBLANES


def _conv_taps(ext, width, lead, n):
    return [(ext if k == width - 1 else pltpu.roll(ext, width - 1 - k, 0))[lead:lead + n] for k in range(width)]


def _prev_halo_spec(tm, width, cb):
    return pl.BlockSpec((HALO, width), lambda i: (jnp.maximum(i * (tm // HALO) - 1, 0), cb))


def _next_halo_spec(tm, width, cb, t):
    return pl.BlockSpec((HALO, width), lambda i: (jnp.minimum((i + 1) * (tm // HALO), t // HALO - 1), cb))


def _gdn_prep_fwd(proj, conv_w):
    t = proj.shape[0]
    tm = GDN_TM

    def body(q_ref, k_ref, v_ref, hq_ref, hk_ref, hv_ref, w_ref, qn_ref, kn_ref, vo_ref):
        first = pl.program_id(0) == 0
        for idx, (x_ref, h_ref, o_ref) in enumerate(((q_ref, hq_ref, qn_ref), (k_ref, hk_ref, kn_ref),
                                                      (v_ref, hv_ref, vo_ref))):
            halo = jnp.where(first, 0.0, h_ref[...])
            ext = jnp.concatenate([halo, x_ref[...]], axis=0)
            w = w_ref[:, idx * KEY_A:(idx + 1) * KEY_A]
            taps = _conv_taps(ext, GDN_CONV, HALO, tm)
            y = sum(w[k:k + 1, :] * taps[k] for k in range(GDN_CONV))
            a = y * _sigmoid(y)
            if idx < 2:
                for h in range(GDN_HEADS):
                    cs = slice(h * GDN_DK, (h + 1) * GDN_DK)
                    seg = a[:, cs]
                    o_ref[:, cs] = seg * lax.rsqrt(jnp.sum(seg * seg, axis=-1, keepdims=True) + EPS)
            else:
                o_ref[...] = a

    row = pl.BlockSpec((tm, KEY_A), lambda i: (i, 0))
    return pl.pallas_call(
        body, name="gdn_prep_fwd", grid=(t // tm,),
        in_specs=[pl.BlockSpec((tm, PB), lambda i: (i, CB_QA)), pl.BlockSpec((tm, PB), lambda i: (i, CB_KA)),
                  pl.BlockSpec((tm, PB), lambda i: (i, CB_VA)),
                  _prev_halo_spec(tm, PB, CB_QA), _prev_halo_spec(tm, PB, CB_KA), _prev_halo_spec(tm, PB, CB_VA),
                  pl.BlockSpec((GDN_CONV, 3 * KEY_A), lambda i: (0, 0))],
        out_specs=[row, row, row],
        out_shape=[jax.ShapeDtypeStruct((t, KEY_A), F32)] * 3,
        compiler_params=_cparams("parallel"),
    )(proj, proj, proj, proj, proj, proj, conv_w)


def _gdn_prep_bwd(proj, conv_w, dqn, dkn, dv, dproj):
    t = proj.shape[0]
    tm = GDN_TM
    nt = t // tm
    n_ext = tm + HALO

    def body(q_ref, k_ref, v_ref, pq_ref, pk_ref, pv_ref, nq_ref, nk_ref, nv_ref,
             dq_ref, dk_ref, dv_ref, ndq_ref, ndk_ref, ndv_ref, w_ref, dp_in_ref, out_ref, dw_ref):
        i = pl.program_id(0)
        first, last = i == 0, i == nt - 1

        @pl.when(first)
        def _():
            dw_ref[...] = jnp.zeros_like(dw_ref)

        groups = ((q_ref, pq_ref, nq_ref, dq_ref, ndq_ref), (k_ref, pk_ref, nk_ref, dk_ref, ndk_ref),
                  (v_ref, pv_ref, nv_ref, dv_ref, ndv_ref))
        for idx, (x_ref, p_ref, n_ref, d_ref, nd_ref) in enumerate(groups):
            cs_all = slice(idx * KEY_A, (idx + 1) * KEY_A)
            ext = jnp.concatenate([jnp.where(first, 0.0, p_ref[...]), x_ref[...], jnp.where(last, 0.0, n_ref[...])], axis=0)
            w = w_ref[:, cs_all]
            taps = _conv_taps(ext, GDN_CONV, HALO, n_ext)
            y = sum(w[k:k + 1, :] * taps[k] for k in range(GDN_CONV))
            sg = _sigmoid(y)
            a = y * sg
            dup = jnp.concatenate([d_ref[...], jnp.where(last, 0.0, nd_ref[...])], axis=0)
            if idx < 2:
                segs = []
                for h in range(GDN_HEADS):
                    cs = slice(h * GDN_DK, (h + 1) * GDN_DK)
                    seg = a[:, cs]
                    r = lax.rsqrt(jnp.sum(seg * seg, axis=-1, keepdims=True) + EPS)
                    nrm = seg * r
                    dn = dup[:, cs]
                    segs.append(r * (dn - nrm * jnp.sum(dn * nrm, axis=-1, keepdims=True)))
                da = jnp.concatenate(segs, axis=1)
            else:
                da = dup
            dy = da * sg * (1.0 + y * (1.0 - sg))
            dx = sum(w[k:k + 1, :] * (dy if k == GDN_CONV - 1 else pltpu.roll(dy, n_ext - (GDN_CONV - 1 - k), 0))[:tm]
                     for k in range(GDN_CONV))
            out_ref[:, cs_all] = dx.astype(BF16)
            for k in range(GDN_CONV):
                dw_ref[k:k + 1, cs_all] += jnp.sum(dy[:tm] * taps[k][:tm], axis=0, keepdims=True)

    row = pl.BlockSpec((tm, KEY_A), lambda i: (i, 0))
    nrow = _next_halo_spec(tm, KEY_A, 0, t)
    return pl.pallas_call(
        body, name="gdn_prep_bwd", grid=(nt,),
        in_specs=[pl.BlockSpec((tm, PB), lambda i: (i, CB_QA)), pl.BlockSpec((tm, PB), lambda i: (i, CB_KA)),
                  pl.BlockSpec((tm, PB), lambda i: (i, CB_VA)),
                  _prev_halo_spec(tm, PB, CB_QA), _prev_halo_spec(tm, PB, CB_KA), _prev_halo_spec(tm, PB, CB_VA),
                  _next_halo_spec(tm, PB, CB_QA, t), _next_halo_spec(tm, PB, CB_KA, t), _next_halo_spec(tm, PB, CB_VA, t),
                  row, row, row, nrow, nrow, nrow,
                  pl.BlockSpec((GDN_CONV, 3 * KEY_A), lambda i: (0, 0)), pl.BlockSpec(memory_space=pl.ANY)],
        out_specs=[_dp_spec(tm, DP_QKVA), pl.BlockSpec((SUBLANES, 3 * KEY_A), lambda i: (0, 0))],
        out_shape=[jax.ShapeDtypeStruct(dproj.shape, dproj.dtype), jax.ShapeDtypeStruct((SUBLANES, 3 * KEY_A), F32)],
        input_output_aliases={16: 0},
        compiler_params=_cparams("arbitrary"),
    )(proj, proj, proj, proj, proj, proj, proj, proj, proj, dqn, dkn, dv, dqn, dkn, dv, conv_w, dproj)


class _Pair(dict):
    __getattr__ = dict.__getitem__
    __setattr__ = dict.__setitem__


def _pairs_to_lanes(cols):
    lane = lax.broadcasted_iota(jnp.int32, (1, LANES), 1)
    out = jnp.zeros((cols[0].shape[0], LANES), F32)
    for p, col in enumerate(cols):
        out = out + jnp.where(lane == p, col, 0.0)
    return out


def _gdn_terms(bd, par, kn_ref, qn_ref):
    c = CHUNK
    ii = lax.broadcasted_iota(jnp.int32, (c, c), 0)
    jj = lax.broadcasted_iota(jnp.int32, (c, c), 1)
    strict, incl = ii > jj, ii >= jj
    ltri = incl.astype(F32)
    ts = []
    for cc in range(GDN_CB):
        for h in range(GDN_HEADS):
            t = _Pair(cc=cc, h=h, rows=slice(cc * c, (cc + 1) * c), cs=slice(h * GDN_DK, (h + 1) * GDN_DK),
                      strict=strict, incl=incl)
            t.beta = _sigmoid(bd[t.rows, h:h + 1])
            t.ea = jnp.exp(par[0:1, h:h + 1])
            t.sp_arg = bd[t.rows, GDN_HEADS + h:GDN_HEADS + h + 1] + par[1:2, h:h + 1]
            t.g = -t.ea * _softplus(t.sp_arg)
            t.k = kn_ref[t.rows, t.cs]
            t.q = qn_ref[t.rows, t.cs] * (GDN_DK ** -0.5)
            t.kb, t.qb = t.k.astype(BF16), t.q.astype(BF16)
            ts.append(t)
    gall = _dot(ltri, _pairs_to_lanes([t.g for t in ts]), HIGHEST)
    gall_t = gall.T
    for p, t in enumerate(ts):
        t.gb = jnp.broadcast_to(gall[:, p:p + 1], (c, GDN_DK))
    for t in ts:
        t.kk = _dot_nt(t.kb, t.kb)
        t.qk = _dot_nt(t.qb, t.kb)
    for p, t in enumerate(ts):
        diff = t.gb[:, :c] - gall_t[p:p + 1, :]
        t.dec_s = jnp.exp(jnp.where(strict, diff, NEG))
        t.dec_i = jnp.exp(jnp.where(incl, diff, NEG))
        t.gam = jnp.exp(t.gb)
        glast = t.gb[c - 1:c, :]
        t.e_rest = jnp.exp(glast - t.gb)
        t.gl = jnp.exp(glast)
        t.p = t.qk * t.dec_i
    return ts


def _gdn_fwd(qn, kn, v, proj, par, gnw):
    t = qn.shape[0]
    c = CHUNK
    nc = t // c
    r_ = GDN_CB * c

    def body(qn_ref, kn_ref, v_ref, bd_ref, z_ref, par_ref, gnw_ref,
             oan_ref, o_ref, sp_ref, w_ref, u_ref, tm_ref, s_ref):
        @pl.when(pl.program_id(0) == 0)
        def _():
            s_ref[...] = jnp.zeros_like(s_ref)

        bd, par, gnw_v = bd_ref[...], par_ref[...], gnw_ref[...]
        eye = (lax.broadcasted_iota(jnp.int32, (c, c), 0) == lax.broadcasted_iota(jnp.int32, (c, c), 1)).astype(F32)
        ts = _gdn_terms(bd, par, kn_ref, qn_ref)
        for t in ts:
            t.vv = v_ref[t.rows, t.cs]
            t.x = -(t.beta * t.kk * t.dec_s)
            t.tinv = eye + t.x
        for t in ts:
            t.xs = _split(t.x)
        for _ in range(5):
            for t in ts:
                t.xs = _split(_dot3s(t.xs, t.xs))
            for t in ts:
                t.tinv = t.tinv + _dot3s(_split(t.tinv), t.xs)
        for t in ts:
            tsp = _split(t.tinv)
            t.wm = _dot3s(tsp, _split((t.beta * t.gam) * t.k))
            t.uv = _dot3s(tsp, _split(t.beta * t.vv))
        for t in ts:
            w_ref[t.rows, t.cs] = t.wm
            tm_ref[t.cc, t.h] = t.tinv.T
            t.wb = t.wm.astype(BF16)
            t.qgb = (t.q * t.gam).astype(BF16)
            t.kdb = (t.k * t.e_rest).astype(BF16)
            t.pb = t.p.astype(BF16)
        state = [s_ref[h] for h in range(GDN_HEADS)]
        for cc in range(GDN_CB):
            tc = [t for t in ts if t.cc == cc]
            for t in tc:
                t.sh = state[t.h]
                t.sb = t.sh.astype(BF16)
            for t in tc:
                t.ws = _dot(t.wb, t.sb)
            for t in tc:
                t.u = t.uv - t.ws
                t.ub = t.u.astype(BF16)
            for t in tc:
                state[t.h] = t.gl * t.sh + _dot_tn(t.kdb, t.ub)
            for t in tc:
                t.o = _dot(t.qgb, t.sb) + _dot(t.pb, t.ub)
                sp_ref[cc, t.h] = t.sh
                u_ref[t.rows, t.cs] = t.u
                o_ref[t.rows, t.cs] = t.o
        for h in range(GDN_HEADS):
            s_ref[h] = state[h]
        for t in ts:
            zz = z_ref[t.rows, t.cs]
            rr = lax.rsqrt(jnp.mean(t.o * t.o, axis=-1, keepdims=True) + EPS)
            oan_ref[t.rows, t.cs] = ((t.o * rr) * gnw_v * (zz * _sigmoid(zz))).astype(BF16)

    row = pl.BlockSpec((r_, KEY_A), lambda i: (i, 0))
    return pl.pallas_call(
        body, name="gdn_fwd", grid=(nc // GDN_CB,),
        in_specs=[row, row, row, pl.BlockSpec((r_, LANES), lambda i: (i, CB_BD)),
                  pl.BlockSpec((r_, PB), lambda i: (i, CB_ZA)),
                  pl.BlockSpec((SUBLANES, LANES), lambda i: (0, 0)), pl.BlockSpec((1, GDN_DK), lambda i: (0, 0))],
        out_specs=[row, row, pl.BlockSpec((GDN_CB, GDN_HEADS, GDN_DK, GDN_DK), lambda i: (i, 0, 0, 0)),
                   row, row, pl.BlockSpec((GDN_CB, GDN_HEADS, c, c), lambda i: (i, 0, 0, 0))],
        out_shape=[jax.ShapeDtypeStruct((t, KEY_A), BF16), jax.ShapeDtypeStruct((t, KEY_A), F32),
                   jax.ShapeDtypeStruct((nc, GDN_HEADS, GDN_DK, GDN_DK), F32),
                   jax.ShapeDtypeStruct((t, KEY_A), F32), jax.ShapeDtypeStruct((t, KEY_A), F32),
                   jax.ShapeDtypeStruct((nc, GDN_HEADS, c, c), F32)],
        scratch_shapes=[pltpu.VMEM((GDN_HEADS, GDN_DK, GDN_DK), F32)],
        compiler_params=_cparams("arbitrary"),
    )(qn, kn, v, proj, proj, par, gnw)


def _gdn_bwd(qn, kn, v, proj, par, gnw, o, sprev, wst, ust, tst, d_oan, dproj):
    t = qn.shape[0]
    c = CHUNK
    nc = t // c
    nb = nc // GDN_CB
    r_ = GDN_CB * c

    def body(qn_ref, kn_ref, v_ref, bd_ref, z_ref, par_ref, gnw_ref, o_ref, sp_ref, w_ref, u_ref, tm_ref, do_ref,
             dp_in_ref, dqn_ref, dkn_ref, dv_ref, dzb_ref, acc_ref, ds_ref):
        @pl.when(pl.program_id(0) == 0)
        def _():
            ds_ref[...] = jnp.zeros_like(ds_ref)
            acc_ref[...] = jnp.zeros_like(acc_ref)

        bd, par, gnw_v = bd_ref[...], par_ref[...], gnw_ref[...]
        lane = lax.broadcasted_iota(jnp.int32, (1, LANES), 1)
        rix = lax.broadcasted_iota(jnp.int32, (c, 1), 0)
        ii = lax.broadcasted_iota(jnp.int32, (c, c), 0)
        jj = lax.broadcasted_iota(jnp.int32, (c, c), 1)
        upper = (jj >= ii).astype(F32)
        acc_a = jnp.zeros((1, LANES), F32)
        acc_d = jnp.zeros((1, LANES), F32)
        acc_g = jnp.zeros((1, LANES), F32)
        ts = _gdn_terms(bd, par, kn_ref, qn_ref)
        for t in ts:
            t.vv = v_ref[t.rows, t.cs]
            t.sh = sp_ref[t.cc, t.h]
            t.sb = t.sh.astype(BF16)
            t.wm, t.u, t.tinv_t = w_ref[t.rows, t.cs], u_ref[t.rows, t.cs], tm_ref[t.cc, t.h]
            t.wb, t.ub = t.wm.astype(BF16), t.u.astype(BF16)
            ov, zz, dout = o_ref[t.rows, t.cs], z_ref[t.rows, t.cs], do_ref[t.rows, t.cs]
            sg = _sigmoid(zz)
            sil = zz * sg
            rr = lax.rsqrt(jnp.mean(ov * ov, axis=-1, keepdims=True) + EPS)
            on = ov * rr
            dzb_ref[t.rows, t.cs] = (dout * on * gnw_v * (sg * (1.0 + zz * (1.0 - sg)))).astype(BF16)
            acc_g = acc_g + jnp.sum(dout * on * sil, axis=0, keepdims=True)
            don = dout * gnw_v * sil
            t.dob = (rr * (don - on * jnp.mean(don * on, axis=-1, keepdims=True))).astype(BF16)
            t.qg = t.q * t.gam
            t.kd = t.k * t.e_rest
            t.qgb, t.kdb = t.qg.astype(BF16), t.kd.astype(BF16)
            t.ptb = t.p.T.astype(BF16)
        for t in ts:
            t.du0 = _dot(t.ptb, t.dob)
            t.ds0 = _dot_tn(t.qgb, t.dob)
            t.dqg = _dot_nt(t.dob, t.sb)
            t.dp = _dot_nt(t.dob, t.ub)
            t.uv = t.u + _dot(t.wb, t.sb)
        dstate = [ds_ref[h] for h in range(GDN_HEADS)]
        for cc in reversed(range(GDN_CB)):
            tc = [t for t in ts if t.cc == cc]
            for t in tc:
                t.dsn = dstate[t.h]
                t.dsnb = t.dsn.astype(BF16)
            for t in tc:
                t.du = t.du0 + _dot(t.kdb, t.dsnb)
            for t in tc:
                t.dub = t.du.astype(BF16)
            for t in tc:
                dstate[t.h] = t.gl * t.dsn + t.ds0 - _dot_tn(t.wb, t.dub)
            for t in tc:
                t.dkd = _dot_nt(t.ub, t.dsnb)
                t.dgl = jnp.sum(jnp.sum(t.dsn * t.sh, axis=1, keepdims=True), axis=0, keepdims=True)
                t.dwm = -_dot_nt(t.dub, t.sb)
        for h in range(GDN_HEADS):
            ds_ref[h] = dstate[h]
        for t in ts:
            tsp = _split(t.tinv_t)
            t.dbk = _dot3s(tsp, _split(t.dwm))
            t.dbv = _dot3s(tsp, _split(t.du))
        for t in ts:
            d_a = -(_dot_nt(t.dbk.astype(BF16), t.wb) + _dot_nt(t.dbv.astype(BF16), t.uv.astype(BF16)))
            t.d_a = jnp.where(t.strict, d_a, 0.0)
        for t in ts:
            t.dkk = t.d_a * t.beta * t.dec_s
            t.dqk = t.dp * t.dec_i
            t.dqkb = t.dqk.astype(BF16)
        for t in ts:
            t.dq = _dot(t.dqkb, t.kb) + t.dqg * t.gam
            t.dk = (t.dbk * (t.beta * t.gam) + _dot_tn(t.dqkb, t.qb) + _dot((t.dkk + t.dkk.T).astype(BF16), t.kb)
                    + t.dkd * t.e_rest)
        for t in ts:
            dbeta = (jnp.sum(t.d_a * t.kk * t.dec_s, axis=-1, keepdims=True)
                     + jnp.sum(t.dbk * t.k * t.gam, axis=-1, keepdims=True) + jnp.sum(t.dbv * t.vv, axis=-1, keepdims=True))
            t.dbl = dbeta * t.beta * (1.0 - t.beta)
            dv_ref[t.rows, t.cs] = t.dbv * t.beta
            bk = (t.beta * t.gam) * t.k
            zc = jnp.sum(t.dkd * t.kd, axis=-1, keepdims=True)
            xs = t.dkk * t.kk + t.dp * t.p
            dgc = (jnp.sum(xs, axis=-1, keepdims=True) - jnp.sum(xs.T, axis=-1, keepdims=True)
                   + jnp.sum(t.dbk * bk, axis=-1, keepdims=True) + jnp.sum(t.dqg * t.qg, axis=-1, keepdims=True) - zc)
            dglast = jnp.sum(zc, axis=0, keepdims=True) + t.dgl * t.gl[:, 0:1]
            t.dgc = dgc + jnp.where(rix == c - 1, dglast, 0.0)
        dgall = _dot(upper, _pairs_to_lanes([t.dgc for t in ts]), HIGHEST)
        for p, t in enumerate(ts):
            t.dg = dgall[:, p:p + 1]
        dbd_tiles = [jnp.zeros((c, LANES), F32) for _ in range(GDN_CB)]
        for t in ts:
            ddl = t.dg * (-t.ea) * _sigmoid(t.sp_arg)
            acc_a = acc_a + jnp.where(lane == t.h, jnp.sum(t.dg * t.g, axis=0, keepdims=True), 0.0)
            acc_d = acc_d + jnp.where(lane == t.h, jnp.sum(ddl, axis=0, keepdims=True), 0.0)
            dbd_tiles[t.cc] = (dbd_tiles[t.cc] + jnp.where(lane == t.h, t.dbl, 0.0)
                               + jnp.where(lane == GDN_HEADS + t.h, ddl, 0.0))
            dqn_ref[t.rows, t.cs] = t.dq * (GDN_DK ** -0.5)
            dkn_ref[t.rows, t.cs] = t.dk
        for cc in range(GDN_CB):
            dzb_ref[cc * c:(cc + 1) * c, KEY_A:KEY_A + LANES] = dbd_tiles[cc].astype(BF16)
        acc_ref[0:1, :] += acc_a
        acc_ref[1:2, :] += acc_d
        acc_ref[2:3, :] += acc_g

    def rev(i):
        return nb - 1 - i

    row = pl.BlockSpec((r_, KEY_A), lambda i: (rev(i), 0))
    st = pl.BlockSpec((GDN_CB, GDN_HEADS, GDN_DK, GDN_DK), lambda i: (rev(i), 0, 0, 0))
    tt_spec = pl.BlockSpec((GDN_CB, GDN_HEADS, c, c), lambda i: (rev(i), 0, 0, 0))
    return pl.pallas_call(
        body, name="gdn_bwd", grid=(nb,),
        in_specs=[row, row, row, pl.BlockSpec((r_, LANES), lambda i: (rev(i), CB_BD)),
                  pl.BlockSpec((r_, PB), lambda i: (rev(i), CB_ZA)),
                  pl.BlockSpec((SUBLANES, LANES), lambda i: (0, 0)), pl.BlockSpec((1, GDN_DK), lambda i: (0, 0)),
                  row, st, row, row, tt_spec, row, pl.BlockSpec(memory_space=pl.ANY)],
        out_specs=[row, row, row, _dp_spec(r_, DP_ZBD, rev), pl.BlockSpec((SUBLANES, LANES), lambda i: (0, 0))],
        out_shape=[jax.ShapeDtypeStruct((t, KEY_A), F32)] * 3 + [jax.ShapeDtypeStruct(dproj.shape, dproj.dtype),
                                                                jax.ShapeDtypeStruct((SUBLANES, LANES), F32)],
        input_output_aliases={13: 3},
        scratch_shapes=[pltpu.VMEM((GDN_HEADS, GDN_DK, GDN_DK), F32)],
        compiler_params=_cparams("arbitrary"),
    )(qn, kn, v, proj, proj, par, gnw, o, sprev, wst, ust, tst, d_oan, dproj)


def _merge_fwd(oan, ob, proj, x, wba, wbb, wout, tm=512):
    t = x.shape[0]

    def body(oa_ref, ob_ref, ga_ref, gb_ref, x_ref, wba_ref, wbb_ref, wout_ref, x2_ref):
        ya = _dot(oa_ref[...], wba_ref[...])
        yb = _dot(ob_ref[...], wbb_ref[...])
        mix = _sigmoid(ga_ref[...]) * ya + _sigmoid(gb_ref[...]) * yb
        x2_ref[...] = x_ref[...] + _dot(mix.astype(BF16), wout_ref[...])

    half = pl.BlockSpec((tm, KEY_A), lambda i: (i, 0))
    row = pl.BlockSpec((tm, D_MODEL), lambda i: (i, 0))
    wsmall = pl.BlockSpec((KEY_A, D_MODEL), lambda i: (0, 0))
    return pl.pallas_call(
        body, name="merge_fwd", grid=(t // tm,),
        in_specs=[half, half, pl.BlockSpec((tm, D_MODEL), lambda i: (i, CB_GA)),
                  pl.BlockSpec((tm, D_MODEL), lambda i: (i, CB_GB)), row, wsmall, wsmall,
                  pl.BlockSpec((D_MODEL, D_MODEL), lambda i: (0, 0))],
        out_specs=row,
        out_shape=jax.ShapeDtypeStruct((t, D_MODEL), F32),
        compiler_params=_cparams("parallel"),
    )(oan, ob, proj, proj, x, wba, wbb, wout)


def _merge_bwd(dx2b, oan, ob, proj, wba, wbb, wout_t, wba_t, wbb_t, tm=512):
    t = dx2b.shape[0]

    def body(dx_ref, oa_ref, ob_ref, ga_ref, gb_ref, wba_ref, wbb_ref, woutt_ref, wbat_ref, wbbt_ref,
             dg_ref, doa_ref, dob_ref, gout_ref, gba_ref, gbb_ref):
        @pl.when(pl.program_id(0) == 0)
        def _():
            gout_ref[...] = jnp.zeros_like(gout_ref)
            gba_ref[...] = jnp.zeros_like(gba_ref)
            gbb_ref[...] = jnp.zeros_like(gbb_ref)

        dx, oa, ob = dx_ref[...], oa_ref[...], ob_ref[...]
        dmix = _dot(dx, woutt_ref[...])
        ya = _dot(oa, wba_ref[...])
        yb = _dot(ob, wbb_ref[...])
        sa, sb = _sigmoid(ga_ref[...]), _sigmoid(gb_ref[...])
        gout_ref[...] += _dot_tn((sa * ya + sb * yb).astype(BF16), dx)
        dg_ref[:, :D_MODEL] = (dmix * ya * sa * (1.0 - sa)).astype(BF16)
        dg_ref[:, D_MODEL:] = (dmix * yb * sb * (1.0 - sb)).astype(BF16)
        dya = (dmix * sa).astype(BF16)
        dyb = (dmix * sb).astype(BF16)
        gba_ref[...] += _dot_tn(oa, dya)
        gbb_ref[...] += _dot_tn(ob, dyb)
        doa_ref[...] = _dot(dya, wbat_ref[...])
        dob_ref[...] = _dot(dyb, wbbt_ref[...])

    half = pl.BlockSpec((tm, KEY_A), lambda i: (i, 0))
    row = pl.BlockSpec((tm, D_MODEL), lambda i: (i, 0))
    wsmall = pl.BlockSpec((KEY_A, D_MODEL), lambda i: (0, 0))
    wsmall_t = pl.BlockSpec((D_MODEL, KEY_A), lambda i: (0, 0))
    wfull = pl.BlockSpec((D_MODEL, D_MODEL), lambda i: (0, 0))
    return pl.pallas_call(
        body, name="merge_bwd", grid=(t // tm,),
        in_specs=[row, half, half, pl.BlockSpec((tm, D_MODEL), lambda i: (i, CB_GA)),
                  pl.BlockSpec((tm, D_MODEL), lambda i: (i, CB_GB)), wsmall, wsmall, wfull, wsmall_t, wsmall_t],
        out_specs=[_dp_spec(tm, DP_GATES), half, half, wfull, wsmall, wsmall],
        out_shape=[jax.ShapeDtypeStruct((t, PROJ_W), BF16), jax.ShapeDtypeStruct((t, KEY_A), F32),
                   jax.ShapeDtypeStruct((t, KEY_A), F32), jax.ShapeDtypeStruct((D_MODEL, D_MODEL), F32),
                   jax.ShapeDtypeStruct((KEY_A, D_MODEL), F32), jax.ShapeDtypeStruct((WIDTH_B, D_MODEL), F32)],
        compiler_params=_cparams("arbitrary"),
    )(dx2b, oan, ob, proj, proj, wba, wbb, wout_t, wba_t, wbb_t)


FFN_TM = 128
FFN_W = 2 * D_FF


def _ffn_conv(up_ref, halo_ref, cw_ref, cb_ref, first):
    ext = jnp.concatenate([jnp.where(first, 0.0, halo_ref[...]), up_ref[...]], axis=0)
    taps = _conv_taps(ext, FFN_CONV, HALO, FFN_TM)
    cw = cw_ref[...]
    u = sum(cw[k:k + 1, :] * taps[k] for k in range(FFN_CONV)) + cb_ref[...]
    return u, taps


def _resident(shape):
    return pl.BlockSpec(shape, lambda i: (0,) * len(shape), pipeline_mode=pl.Buffered(1))


def _ffn_fwd(h2, wup, cw, cb, wdown, x2, tgt, w3):
    t = x2.shape[0]
    tm = FFN_TM

    def body(h2_ref, wup_ref, cw_ref, cb_ref, wd_ref, x2_ref, tgt_ref, w3_ref, up_ref, dx_ref, dxb_ref, act_ref, acc_ref,
             prev_ref):
        @pl.when(pl.program_id(0) == 0)
        def _():
            acc_ref[...] = jnp.zeros_like(acc_ref)
            prev_ref[...] = jnp.zeros_like(prev_ref)

        up = _dot(h2_ref[...], wup_ref[...])
        up_ref[...] = up
        ext = jnp.concatenate([prev_ref[...], up], axis=0)
        prev_ref[...] = up[tm - HALO:tm]
        taps = _conv_taps(ext, FFN_CONV, HALO, tm)
        cw_v = cw_ref[...]
        u = sum(cw_v[k:k + 1, :] * taps[k] for k in range(FFN_CONV)) + cb_ref[...]
        gate, upp = u[:, :D_FF], u[:, D_FF:]
        act = (gate * _sigmoid(gate) * upp).astype(BF16)
        act_ref[...] = act
        x3 = x2_ref[...] + _dot(act, wd_ref[...])
        r = lax.rsqrt(jnp.mean(x3 * x3, axis=-1, keepdims=True) + EPS)
        xh = x3 * r
        w3v = w3_ref[...]
        err = xh * w3v - tgt_ref[...]
        loss = 0.5 * jnp.sum(jnp.mean(err * err, axis=-1, keepdims=True), axis=0, keepdims=True)
        dy = err * (1.0 / D_MODEL)
        acc_ref[0:1, :] += jnp.sum(dy * xh, axis=0, keepdims=True)
        acc_ref[1:2, :] += jnp.broadcast_to(loss, (1, D_MODEL))
        dxh = dy * w3v
        dx = r * (dxh - xh * jnp.mean(dxh * xh, axis=-1, keepdims=True))
        dx_ref[...] = dx
        dxb_ref[...] = dx.astype(BF16)

    row = pl.BlockSpec((tm, D_MODEL), lambda i: (i, 0))
    return pl.pallas_call(
        body, name="ffn_fwd", grid=(t // tm,),
        in_specs=[row, _resident((D_MODEL, FFN_W)), _resident((SUBLANES, FFN_W)), _resident((1, FFN_W)),
                  _resident((D_FF, D_MODEL)), row, row, _resident((1, D_MODEL))],
        out_specs=[pl.BlockSpec((tm, FFN_W), lambda i: (i, 0)), row, row, pl.BlockSpec((tm, D_FF), lambda i: (i, 0)),
                   pl.BlockSpec((SUBLANES, D_MODEL), lambda i: (0, 0))],
        out_shape=[jax.ShapeDtypeStruct((t, FFN_W), F32), jax.ShapeDtypeStruct((t, D_MODEL), F32),
                   jax.ShapeDtypeStruct((t, D_MODEL), BF16), jax.ShapeDtypeStruct((t, D_FF), BF16),
                   jax.ShapeDtypeStruct((SUBLANES, D_MODEL), F32)],
        scratch_shapes=[pltpu.VMEM((HALO, FFN_W), F32)],
        compiler_params=_cparams("arbitrary"),
    )(h2, wup, cw, cb, wdown, x2, tgt, w3)


def _ffn_bwd(dx3b, wdown_t, up, cw, cb, wup_t, x2, w2, dx3, carry):
    t = up.shape[0]
    tm = FFN_TM
    nt = t // tm
    n_ext = tm + HALO
    arrays, scatter = carry
    nx = len(arrays)

    def rev(i):
        return nt - 1 - i

    def body(*refs):
        dx_ref, wdt_ref, up_ref, halo_ref, cw_ref, cb_ref, wupt_ref, x2_ref, w2_ref, dres_ref = refs[:10]
        srcs = refs[10:10 + nx]
        dup_ref, acc_ref, dx2_ref, dx2b_ref, dw2_ref = refs[10 + nx:15 + nx]
        dsts = refs[15 + nx:15 + 2 * nx]
        nxt_ref = refs[15 + 2 * nx]
        i = pl.program_id(0)
        local, remote = _exchange_copies(srcs, dsts, scatter, *refs[16 + 2 * nx:])

        @pl.when(i == 0)
        def _():
            for cp in local + remote:
                cp.start()
            acc_ref[...] = jnp.zeros_like(acc_ref)
            dw2_ref[...] = jnp.zeros_like(dw2_ref)
            nxt_ref[...] = jnp.zeros_like(nxt_ref)

        dact = _dot(dx_ref[...], wdt_ref[...])
        u, taps = _ffn_conv(up_ref, halo_ref, cw_ref, cb_ref, i == nt - 1)
        gate, upp = u[:, :D_FF], u[:, D_FF:]
        sg = _sigmoid(gate)
        du = jnp.concatenate([dact * upp * (sg * (1.0 + gate * (1.0 - sg))), dact * (gate * sg)], axis=1)
        for k in range(FFN_CONV):
            acc_ref[k:k + 1, :] += jnp.sum(du * taps[k], axis=0, keepdims=True)
        acc_ref[FFN_CONV:FFN_CONV + 1, :] += jnp.sum(du, axis=0, keepdims=True)
        ext = jnp.concatenate([du, nxt_ref[...]], axis=0)
        cw_v = cw_ref[...]
        dup = cw_v[FFN_CONV - 1:FFN_CONV, :] * du
        for k in range(FFN_CONV - 1):
            dup = dup + cw_v[k:k + 1, :] * pltpu.roll(ext, n_ext - (FFN_CONV - 1 - k), 0)[:tm]
        dupb = dup.astype(BF16)
        dup_ref[...] = dupb
        nxt_ref[...] = du[0:HALO]
        dhv = _dot(dupb, wupt_ref[...])
        xv = x2_ref[...]
        r = lax.rsqrt(jnp.mean(xv * xv, axis=-1, keepdims=True) + EPS)
        xh = xv * r
        dw2_ref[0:1, :] += jnp.sum(dhv * xh, axis=0, keepdims=True)
        dxh = dhv * w2_ref[...]
        dx2 = dres_ref[...] + r * (dxh - xh * jnp.mean(dxh * xh, axis=-1, keepdims=True))
        dx2_ref[...] = dx2
        dx2b_ref[...] = dx2.astype(BF16)

        @pl.when(i == nt - 1)
        def _():
            for cp in remote + local:
                cp.wait()

    wide = pl.BlockSpec((tm, FFN_W), lambda i: (rev(i), 0))
    row = pl.BlockSpec((tm, D_MODEL), lambda i: (rev(i), 0))
    any_spec = pl.BlockSpec(memory_space=pl.ANY)
    return pl.pallas_call(
        body, name="ffn_bwd", grid=(nt,),
        in_specs=[row, _resident((D_MODEL, D_FF)),
                  wide, pl.BlockSpec((HALO, FFN_W), lambda i: (jnp.maximum(rev(i) * (tm // HALO) - 1, 0), 0)),
                  _resident((SUBLANES, FFN_W)), _resident((1, FFN_W)), _resident((FFN_W, D_MODEL)), row,
                  _resident((1, D_MODEL)), row] + [any_spec] * nx,
        out_specs=[wide, pl.BlockSpec((SUBLANES, FFN_W), lambda i: (0, 0)), row, row,
                   pl.BlockSpec((SUBLANES, D_MODEL), lambda i: (0, 0))] + [any_spec] * nx,
        out_shape=[jax.ShapeDtypeStruct((t, FFN_W), BF16), jax.ShapeDtypeStruct((SUBLANES, FFN_W), F32),
                   jax.ShapeDtypeStruct((t, D_MODEL), F32), jax.ShapeDtypeStruct((t, D_MODEL), BF16),
                   jax.ShapeDtypeStruct((SUBLANES, D_MODEL), F32)] + _exchange_shapes(arrays, scatter),
        scratch_shapes=[pltpu.VMEM((HALO, FFN_W), F32)] + _exchange_sems(nx),
        compiler_params=_cparams("arbitrary"),
    )(dx3b, wdown_t, up, up, cw, cb, wup_t, x2, w2, dx3, *arrays)


def _adamw(parts, w, m, v, name, tr):
    r, cols = w.shape

    def body(p_ref, w_ref, m_ref, v_ref, g_ref, d_ref, mo_ref, vo_ref):
        g = p_ref[0].astype(F32)
        for s in range(1, N_DEV):
            g = g + p_ref[s].astype(F32)
        mm = ADAM_B1 * m_ref[...] + (1.0 - ADAM_B1) * g
        vv = ADAM_B2 * v_ref[...] + (1.0 - ADAM_B2) * (g * g)
        m_hat = mm / (1.0 - ADAM_B1 ** ADAM_STEP)
        v_hat = vv / (1.0 - ADAM_B2 ** ADAM_STEP)
        g_ref[...] = g
        d_ref[...] = -ADAM_LR * (m_hat / (jnp.sqrt(v_hat) + ADAM_EPS) + ADAM_WD * w_ref[...])
        mo_ref[...] = mm
        vo_ref[...] = vv

    assert r % tr == 0
    row = pl.BlockSpec((tr, cols), lambda i: (i, 0))
    return pl.pallas_call(
        body, name=name, grid=(r // tr,),
        in_specs=[pl.BlockSpec((N_DEV, tr, cols), lambda i: (0, i, 0)), row, row, row],
        out_specs=[row, row, row, row],
        out_shape=[jax.ShapeDtypeStruct((r, cols), F32)] * 4,
        compiler_params=_cparams("parallel"),
    )(parts, w, m, v)


def _mesh_pos():
    return lax.axis_index("x"), lax.axis_index("y"), lax.axis_index("c")


def _peer(pos, k):
    x, y, c = pos
    return (x ^ ((k >> 2) & 1), y ^ ((k >> 1) & 1), c ^ (k & 1))


def _flat_id(pos):
    return 4 * pos[0] + 2 * pos[1] + pos[2]


def _exchange_copies(srcs, dsts, scatter, send_sems, recv_sems, loc_sems):
    pos = _mesh_pos()
    me = _flat_id(pos)
    local, remote = [], []
    for j, (src, dst) in enumerate(zip(srcs, dsts)):
        local.append(pltpu.make_async_copy(src.at[me] if scatter[j] else src, dst.at[me], loc_sems.at[j]))
        for k in range(1, N_DEV):
            to = _peer(pos, k)
            remote.append(pltpu.make_async_remote_copy(
                src_ref=src.at[_flat_id(to)] if scatter[j] else src, dst_ref=dst.at[me],
                send_sem=send_sems.at[j, k - 1], recv_sem=recv_sems.at[j, k - 1],
                device_id=to, device_id_type=pl.DeviceIdType.MESH))
    return local, remote


def _exchange_shapes(arrays, scatter):
    return [jax.ShapeDtypeStruct(a.shape if s else (N_DEV,) + a.shape, a.dtype) for a, s in zip(arrays, scatter)]


def _exchange_sems(n):
    return [pltpu.SemaphoreType.DMA((n, N_DEV - 1)), pltpu.SemaphoreType.DMA((n, N_DEV - 1)), pltpu.SemaphoreType.DMA((n,))]


def _exchange(arrays, scatter, name):
    n = len(arrays)
    any_spec = pl.BlockSpec(memory_space=pl.ANY)

    def body(*refs):
        local, remote = _exchange_copies(refs[:n], refs[n:2 * n], scatter, *refs[2 * n:])
        for cp in local + remote:
            cp.start()
        for cp in remote:
            cp.wait()
        for cp in local:
            cp.wait()

    return pl.pallas_call(
        body, name=name, in_specs=[any_spec] * n, out_specs=[any_spec] * n,
        out_shape=_exchange_shapes(arrays, scatter), scratch_shapes=_exchange_sems(n),
    )(*arrays)


def _pad_rows(a, rows):
    return jnp.pad(a, ((0, rows - a.shape[0]),) + ((0, 0),) * (a.ndim - 1))


PACK_UNIT = SUBLANES * LANES


def _pack_lanes(parts, rows):
    out = []
    for a in parts:
        f = a.reshape(-1)
        out.append(jnp.pad(f, (0, (-f.shape[0]) % PACK_UNIT)).reshape(-1, LANES))
    packed = jnp.concatenate(out, axis=0)
    assert packed.shape[0] == rows, (packed.shape, rows)
    return packed


def _unpack_lanes(buf, shapes):
    out, r0 = [], 0
    for shp in shapes:
        n = math.prod(shp)
        nr = -(-n // PACK_UNIT) * SUBLANES
        out.append(buf[r0:r0 + nr].reshape(-1)[:n].reshape(shp))
        r0 += nr
    return out


def _col_shards(g):
    r, n = g.shape
    return g.reshape(r, N_DEV, n // N_DEV).transpose(1, 0, 2)


def _col_unshard(s):
    _, r, w = s.shape
    return s.transpose(1, 0, 2).reshape(r, N_DEV * w)


def _lane_rows(flat):
    n = flat.shape[1]
    return jnp.pad(flat, ((0, 0), (0, (-n) % PACK_UNIT))).reshape(N_DEV, -1, LANES)


SMALL_ROWS = 128
WS_ROWS = 32


def kernel(x, norm_mix_w, w_in, conv_qkv_w, a_log, dt_bias, gdn_norm_w, w_branch_a, w_branch_b, rel_bias, w_out, norm_ffn_w, w_up, conv_ffn_w, conv_ffn_b, w_down, norm_final_w, loss_target, m_norm_mix_w, m_w_in, m_conv_qkv_w, m_a_log, m_dt_bias, m_gdn_norm_w, m_w_branch_a, m_w_branch_b, m_rel_bias, m_w_out, m_norm_ffn_w, m_w_up, m_conv_ffn_w, m_conv_ffn_b, m_w_down, m_norm_final_w, v_norm_mix_w, v_w_in, v_conv_qkv_w, v_a_log, v_dt_bias, v_gdn_norm_w, v_w_branch_a, v_w_branch_b, v_rel_bias, v_w_out, v_norm_ffn_w, v_w_up, v_conv_ffn_w, v_conv_ffn_b, v_w_down, v_norm_final_w):
    big_w = (w_in, w_branch_a, w_branch_b, w_out, w_up, w_down, conv_qkv_w, conv_ffn_w)
    big_m = (m_w_in, m_w_branch_a, m_w_branch_b, m_w_out, m_w_up, m_w_down, m_conv_qkv_w, m_conv_ffn_w)
    big_v = (v_w_in, v_w_branch_a, v_w_branch_b, v_w_out, v_w_up, v_w_down, v_conv_qkv_w, v_conv_ffn_w)
    small_w = (norm_mix_w, a_log, dt_bias, gdn_norm_w, rel_bias, norm_ffn_w, conv_ffn_b, norm_final_w)
    small_m = (m_norm_mix_w, m_a_log, m_dt_bias, m_gdn_norm_w, m_rel_bias, m_norm_ffn_w, m_conv_ffn_b, m_norm_final_w)
    small_v = (v_norm_mix_w, v_a_log, v_dt_bias, v_gdn_norm_w, v_rel_bias, v_norm_ffn_w, v_conv_ffn_b, v_norm_final_w)

    xs, tgt = x[0], loss_target[0]
    ws = _pack_lanes(big_w[6:], WS_ROWS)
    h1, g_in, gs = _rmsnorm_cast(xs, norm_mix_w, "norm_mix", carry=([w_in[0].astype(BF16), ws], (False, False)))
    win = _col_unshard(g_in)
    gs = gs.reshape(N_DEV, -1)
    cqkv = gs[:, :GDN_CONV * 192].reshape(N_DEV, GDN_CONV, 192).transpose(1, 0, 2).reshape(GDN_CONV, 3 * KEY_A)
    cffn = gs[:, PACK_UNIT:PACK_UNIT + FFN_CONV * 704].reshape(N_DEV, FFN_CONV, 704).transpose(1, 0, 2).reshape(FFN_CONV, FFN_W)
    cffn = _pad_rows(cffn, SUBLANES)
    w_all = jnp.concatenate([win[:, a:b] for a, b in W_IN_ORDER] + [jnp.zeros((D_MODEL, PROJ_W - D_IN), BF16)], axis=1)
    par = _pad_rows(jnp.pad(jnp.concatenate([a_log, dt_bias], axis=0), ((0, 0), (0, LANES - GDN_HEADS))), SUBLANES)
    table = jnp.pad(rel_bias[0], ((0, 0), (0, 3 * LANES - rel_bias.shape[-1]))).reshape(ATT_HEADS, 1, 3 * LANES)

    proj, g_ba, g_bb, g_out, g_up, g_down = _mm_nn(
        h1, w_all, F32, "in_proj", 2 * MM_TM, 1152, D_MODEL, carry=([w[0].astype(BF16) for w in big_w[1:6]], (False,) * 5))
    wba, wbb, wup = _col_unshard(g_ba), _col_unshard(g_bb), _col_unshard(g_up)
    wout = g_out.reshape(D_MODEL, D_MODEL)
    wdown = g_down.reshape(D_FF, D_MODEL)
    qn, kn, va = _gdn_prep_fwd(proj, cqkv)
    oan, o_gdn, sprev, wst, ust, tst = _gdn_fwd(qn, kn, va, proj, par, gdn_norm_w)
    bias_q, bias_k = _att_bias(table)
    ob, lse, lse_t = _att_fwd(proj, bias_q)
    x2 = _merge_fwd(oan, ob, proj, xs, wba, wbb, wout)
    h2 = _rmsnorm_cast(x2, norm_ffn_w, "norm_ffn")
    up, dx3, dx3b, act, tail_sums = _ffn_fwd(h2, wup, cffn, conv_ffn_b, wdown, x2, tgt, norm_final_w.reshape(1, D_MODEL))

    g_wdown = _mm_tn(act, dx3b, "dw_down", 512)
    dup, ffn_sums, dx2, dx2b, nffn_sums, r_down = _ffn_bwd(
        dx3b, wdown.T, up, cffn, conv_ffn_b, wup.T, x2, norm_ffn_w, dx3,
        carry=([g_wdown.reshape(N_DEV, -1, D_MODEL).astype(BF16)], (True,)))
    g_wup = _mm_tn(h2, dup, "dw_up", 1408)
    dproj, d_oan, d_ob, g_wout, g_wba, g_wbb = _merge_bwd(dx2b, oan, ob, proj, wba, wbb, wout.T, wba.T, wbb.T)
    dproj, dlt_t, slabs = _att_dq(proj, bias_q, lse, d_ob, dproj)
    dproj = _att_dkv(proj, bias_k, lse_t, dlt_t, d_ob, dproj)
    g_rel = _relbias_grad(slabs)[:, 0, :rel_bias.shape[-1]]
    dqn, dkn, dva, dproj, gdn_sums = _gdn_bwd(qn, kn, va, proj, par, gdn_norm_w, o_gdn, sprev, wst, ust, tst, d_oan, dproj)
    dproj, cq_sums = _gdn_prep_bwd(proj, cqkv, dqn, dkn, dva, dproj)
    g_wall, r_up = _mm_tn(h1, dproj, "dw_in", 1152, carry=([_col_shards(g_wup).astype(BF16)], (True,)))
    starts = np.cumsum([0] + [b - a for a, b in W_IN_ORDER])
    g_win = jnp.concatenate([g_wall[:, starts[i]:starts[i + 1]] for i in np.argsort([a for a, _ in W_IN_ORDER])], axis=1)
    g_conv = jnp.concatenate([_lane_rows(_col_shards(cq_sums[:GDN_CONV]).reshape(N_DEV, -1)),
                              _lane_rows(_col_shards(ffn_sums[:FFN_CONV]).reshape(N_DEV, -1))], axis=1)
    grad_x, _, nmix_sums, r_in, r_ba, r_bb, r_out, r_conv = _mm_rms_bwd(
        dproj, w_all.T, xs, norm_mix_w, dx2, "in_proj_bwd", MM_TM, 1152, carry=(
            [_col_shards(g_win).astype(BF16), _col_shards(g_wba).astype(BF16), _col_shards(g_wbb).astype(BF16),
             g_wout.reshape(N_DEV, -1, D_MODEL).astype(BF16), g_conv], (True,) * 5))

    small_g = (nmix_sums[0:1], gdn_sums[0:1, :GDN_HEADS], gdn_sums[1:2, :GDN_HEADS], gdn_sums[2:3], g_rel,
               nffn_sums[0:1], ffn_sums[FFN_CONV:FFN_CONV + 1], tail_sums[0:1], tail_sums[1:2, 0:1])
    r_small, = _exchange([_pack_lanes(small_g, SMALL_ROWS)], (False,), "all_gather_small_grads")
    recv = (r_in, r_ba, r_bb, r_out, r_up, r_down, r_conv, r_small)

    res = {}
    for i, (nm, tr) in enumerate((("w_in", 128), ("w_branch_a", KEY_A), ("w_branch_b", WIDTH_B), ("w_out", 128),
                                  ("w_up", 128), ("w_down", 176))):
        res[nm] = [o[None] for o in _adamw(recv[i], big_w[i][0], big_m[i][0], big_v[i][0], "adamw_" + nm, tr)]
    conv = _adamw(recv[6], _pack_lanes(big_w[6:], WS_ROWS), _pack_lanes(big_m[6:], WS_ROWS), _pack_lanes(big_v[6:], WS_ROWS),
                  "adamw_conv", WS_ROWS)
    conv = [_unpack_lanes(o, [w.shape for w in big_w[6:]]) for o in conv]
    res["conv_qkv_w"] = [o[0] for o in conv]
    res["conv_ffn_w"] = [o[1] for o in conv]
    small_shapes = [w.shape for w in small_w]
    zero = jnp.zeros((1,), F32)
    small = _adamw(recv[7], _pack_lanes(small_w + (zero,), SMALL_ROWS), _pack_lanes(small_m + (zero,), SMALL_ROWS),
                   _pack_lanes(small_v + (zero,), SMALL_ROWS), "adamw_replicated", SMALL_ROWS)
    small = [_unpack_lanes(o, small_shapes + [()]) for o in small]
    loss = small[0][-1]
    for j, nm in enumerate(("norm_mix_w", "a_log", "dt_bias", "gdn_norm_w", "rel_bias", "norm_ffn_w", "conv_ffn_b",
                            "norm_final_w")):
        res[nm] = [o[j] for o in small]

    names = ("norm_mix_w", "w_in", "conv_qkv_w", "a_log", "dt_bias", "gdn_norm_w", "w_branch_a", "w_branch_b", "rel_bias",
             "w_out", "norm_ffn_w", "w_up", "conv_ffn_w", "conv_ffn_b", "w_down", "norm_final_w")
    outs = [res[n][kind] for kind in range(4) for n in names]
    return (loss, grad_x[None], *outs)
```

```python
import functools
import math

import numpy as np
import jax
import jax.numpy as jnp
from jax import lax
from jax.experimental import pallas as pl
from jax.experimental.pallas import tpu as pltpu

F32, BF16 = jnp.float32, jnp.bfloat16
HIGHEST = lax.Precision.HIGHEST

N_DEV = 8
D_MODEL = 1024
CHUNK = 64
EPS = 1e-6
GDN_HEADS, GDN_DK = 4, 128
KEY_A = GDN_HEADS * GDN_DK
GDN_CONV = 4
ATT_HEADS, ATT_DH = 8, 64
WIDTH_B = ATT_HEADS * ATT_DH
ATT_BAND = 9
REL_CLIP = 128
D_FF = 2816
FFN_CONV = 3
D_IN = 5640
ADAM_LR, ADAM_B1, ADAM_B2, ADAM_EPS, ADAM_WD, ADAM_STEP = 0.001, 0.9, 0.999, 1e-08, 0.01, 10

LANES = 128
SUBLANES = 8
NEG = -1e30

PROJ_W = 5760
PB = 512
CB_GA, CB_GB = 0, 1
CB_KB, CB_VB, CB_QA, CB_KA, CB_VA, CB_QB, CB_ZA = 4, 5, 6, 7, 8, 9, 10
CB_BD = 44
DP_GATES, DP_KVB, DP_QKVA, DP_QB, DP_ZBD = (2048, 0), (1024, 2), (1536, 2), (512, 9), (640, 8)
W_IN_ORDER = ((3592, 5640), (2568, 3592), (0, 1536), (2056, 2568), (1536, 2048), (2048, 2056))

ATT_QB = 256
ATT_KW = 768
ATT_VEC = 1024


def _dot(a, b, precision=None):
    return jnp.dot(a, b, preferred_element_type=F32, precision=precision)


def _dot_nt(a, b, precision=None):
    return lax.dot_general(a, b, (((1,), (1,)), ((), ())), preferred_element_type=F32, precision=precision)


def _dot_tn(a, b):
    return lax.dot_general(a, b, (((0,), (0,)), ((), ())), preferred_element_type=F32)


def _split(a):
    hi = a.astype(BF16)
    return hi, (a - hi.astype(F32)).astype(BF16)


def _dot3s(a, b):
    return _dot(a[0], b[0]) + (_dot(a[0], b[1]) + _dot(a[1], b[0]))


def _sigmoid(x):
    return 0.5 * jnp.tanh(0.5 * x) + 0.5


def _softplus(x):
    return jnp.maximum(x, 0.0) + jnp.log(1.0 + jnp.exp(-jnp.abs(x)))


def _cparams(*sem):
    return pltpu.CompilerParams(dimension_semantics=tuple(sem))


def _dp_spec(tm, region, index=lambda i: i):
    width, cb = region
    return pl.BlockSpec((tm, width), lambda i: (index(i), cb))


def _rmsnorm_cast(x, w, name, tm=512, carry=((), ())):
    t, d = x.shape
    nt = t // tm
    arrays, scatter = carry
    nx = len(arrays)

    def body(*refs):
        x_ref, w_ref = refs[:2]
        o_ref = refs[2 + nx]
        i = pl.program_id(0)
        if nx:
            local, remote = _exchange_copies(refs[2:2 + nx], refs[3 + nx:3 + 2 * nx], scatter, *refs[3 + 2 * nx:])

            @pl.when(i == 0)
            def _():
                for cp in local + remote:
                    cp.start()

        xv = x_ref[...]
        r = lax.rsqrt(jnp.mean(xv * xv, axis=-1, keepdims=True) + EPS)
        o_ref[...] = (xv * r * w_ref[...]).astype(BF16)

        if nx:
            @pl.when(i == nt - 1)
            def _():
                for cp in remote + local:
                    cp.wait()

    any_spec = pl.BlockSpec(memory_space=pl.ANY)
    out = pl.pallas_call(
        body, name=name, grid=(nt,),
        in_specs=[pl.BlockSpec((tm, d), lambda i: (i, 0)), pl.BlockSpec((1, d), lambda i: (0, 0))] + [any_spec] * nx,
        out_specs=[pl.BlockSpec((tm, d), lambda i: (i, 0))] + [any_spec] * nx,
        out_shape=[jax.ShapeDtypeStruct((t, d), BF16)] + _exchange_shapes(arrays, scatter),
        scratch_shapes=_exchange_sems(nx) if nx else [],
        compiler_params=_cparams("arbitrary" if nx else "parallel"),
    )(x, w, *arrays)
    return out if nx else out[0]


def _mm_nn(a, b, out_dtype, name, tm, tn, tk, carry=((), ())):
    m, k = a.shape
    _, n = b.shape
    tm = min(tm, m)
    nk = k // tk
    assert m % tm == 0 and n % tn == 0 and k % tk == 0
    arrays, scatter = carry
    nx = len(arrays)
    gm, gn = m // tm, n // tn

    def body(*refs):
        a_ref, b_ref = refs[:2]
        srcs = refs[2:2 + nx]
        o_ref = refs[2 + nx]
        dsts = refs[3 + nx:3 + 2 * nx]
        rest = refs[3 + 2 * nx:]
        i, j, kk = pl.program_id(0), pl.program_id(1), pl.program_id(2)
        if nx:
            local, remote = _exchange_copies(srcs, dsts, scatter, *rest[-3:])

            @pl.when((i == 0) & (j == 0) & (kk == 0))
            def _():
                for cp in local + remote:
                    cp.start()

        if nk == 1:
            o_ref[...] = _dot(a_ref[...], b_ref[...]).astype(out_dtype)
        else:
            acc_ref = rest[0]

            @pl.when(kk == 0)
            def _():
                acc_ref[...] = jnp.zeros_like(acc_ref)

            acc_ref[...] += _dot(a_ref[...], b_ref[...])

            @pl.when(kk == nk - 1)
            def _():
                o_ref[...] = acc_ref[...].astype(out_dtype)

        if nx:
            @pl.when((i == gm - 1) & (j == gn - 1) & (kk == nk - 1))
            def _():
                for cp in remote + local:
                    cp.wait()

    any_spec = pl.BlockSpec(memory_space=pl.ANY)
    scratch = ([pltpu.VMEM((tm, tn), F32)] if nk > 1 else []) + (_exchange_sems(nx) if nx else [])
    out = pl.pallas_call(
        body, name=name, grid=(gm, gn, nk),
        in_specs=[pl.BlockSpec((tm, tk), lambda i, j, kk: (i, kk)),
                  pl.BlockSpec((tk, tn), lambda i, j, kk: (kk, j))] + [any_spec] * nx,
        out_specs=[pl.BlockSpec((tm, tn), lambda i, j, kk: (i, j))] + [any_spec] * nx,
        out_shape=[jax.ShapeDtypeStruct((m, n), out_dtype)] + _exchange_shapes(arrays, scatter),
        scratch_shapes=scratch,
        compiler_params=_cparams(*(("arbitrary",) * 3 if nx else ("parallel", "parallel", "arbitrary"))),
    )(a, b, *arrays)
    return out if nx else out[0]


def _mm_rms_bwd(a, b, x, w, dres, name, tm, tk, carry):
    m, k = a.shape
    _, n = b.shape
    nk = k // tk
    gm = m // tm
    assert m % tm == 0 and k % tk == 0
    arrays, scatter = carry
    nx = len(arrays)

    def body(*refs):
        a_ref, b_ref, x_ref, w_ref, dres_ref = refs[:5]
        srcs = refs[5:5 + nx]
        dx_ref, dxb_ref, dw_ref = refs[5 + nx:8 + nx]
        dsts = refs[8 + nx:8 + 2 * nx]
        acc_ref = refs[8 + 2 * nx]
        i, kk = pl.program_id(0), pl.program_id(1)
        local, remote = _exchange_copies(srcs, dsts, scatter, *refs[9 + 2 * nx:])

        @pl.when((i == 0) & (kk == 0))
        def _():
            for cp in local + remote:
                cp.start()
            dw_ref[...] = jnp.zeros_like(dw_ref)

        @pl.when(kk == 0)
        def _():
            acc_ref[...] = jnp.zeros_like(acc_ref)

        acc_ref[...] += _dot(a_ref[...], b_ref[...])

        @pl.when(kk == nk - 1)
        def _():
            dhv = acc_ref[...]
            xv = x_ref[...]
            r = lax.rsqrt(jnp.mean(xv * xv, axis=-1, keepdims=True) + EPS)
            xh = xv * r
            dw_ref[0:1, :] += jnp.sum(dhv * xh, axis=0, keepdims=True)
            dxh = dhv * w_ref[...]
            dx = dres_ref[...] + r * (dxh - xh * jnp.mean(dxh * xh, axis=-1, keepdims=True))
            dx_ref[...] = dx
            dxb_ref[...] = dx.astype(BF16)

        @pl.when((i == gm - 1) & (kk == nk - 1))
        def _():
            for cp in remote + local:
                cp.wait()

    any_spec = pl.BlockSpec(memory_space=pl.ANY)
    row = pl.BlockSpec((tm, n), lambda i, kk: (i, 0))
    return pl.pallas_call(
        body, name=name, grid=(gm, nk),
        in_specs=[pl.BlockSpec((tm, tk), lambda i, kk: (i, kk)), pl.BlockSpec((tk, n), lambda i, kk: (kk, 0)),
                  row, pl.BlockSpec((1, n), lambda i, kk: (0, 0)), row] + [any_spec] * nx,
        out_specs=[row, row, pl.BlockSpec((SUBLANES, n), lambda i, kk: (0, 0))] + [any_spec] * nx,
        out_shape=[jax.ShapeDtypeStruct((m, n), F32), jax.ShapeDtypeStruct((m, n), BF16),
                   jax.ShapeDtypeStruct((SUBLANES, n), F32)] + _exchange_shapes(arrays, scatter),
        scratch_shapes=[pltpu.VMEM((tm, n), F32)] + _exchange_sems(nx),
        compiler_params=_cparams("arbitrary", "arbitrary"),
    )(a, b, x, w, dres, *arrays)


MM_TM = 1024


def _mm_tn(a, b, name, tn, tk=2 * MM_TM, carry=((), ())):
    t, m = a.shape
    _, n = b.shape
    tk = min(tk, t)
    assert t % tk == 0 and n % tn == 0
    gn, gs = n // tn, t // tk
    arrays, scatter = carry
    nx = len(arrays)

    def body(*refs):
        a_ref, b_ref = refs[:2]
        o_ref = refs[2 + nx]
        j, s = pl.program_id(0), pl.program_id(1)
        if nx:
            local, remote = _exchange_copies(refs[2:2 + nx], refs[3 + nx:3 + 2 * nx], scatter, *refs[3 + 2 * nx:])

            @pl.when((j == 0) & (s == 0))
            def _():
                for cp in local + remote:
                    cp.start()

        @pl.when(s == 0)
        def _():
            o_ref[...] = jnp.zeros_like(o_ref)

        o_ref[...] += _dot_tn(a_ref[...], b_ref[...])

        if nx:
            @pl.when((j == gn - 1) & (s == gs - 1))
            def _():
                for cp in remote + local:
                    cp.wait()

    any_spec = pl.BlockSpec(memory_space=pl.ANY)
    out = pl.pallas_call(
        body, name=name, grid=(gn, gs),
        in_specs=[pl.BlockSpec((tk, m), lambda j, s: (s, 0)),
                  pl.BlockSpec((tk, tn), lambda j, s: (s, j))] + [any_spec] * nx,
        out_specs=[pl.BlockSpec((m, tn), lambda j, s: (0, j))] + [any_spec] * nx,
        out_shape=[jax.ShapeDtypeStruct((m, n), F32)] + _exchange_shapes(arrays, scatter),
        scratch_shapes=_exchange_sems(nx) if nx else [],
        compiler_params=_cparams(*(("arbitrary", "arbitrary") if nx else ("parallel", "arbitrary"))),
    )(a, b, *arrays)
    return out if nx else out[0]


def _rel_index(dist):
    return np.clip(dist, -REL_CLIP, REL_CLIP) + REL_CLIP


def _bias_onehots():
    tw = 3 * LANES
    m = np.arange(ATT_VEC)
    dq = np.where(m <= ATT_KW, 512 - m, 512 - (m - ATT_VEC))
    dk = np.where(m < ATT_KW, m, m - ATT_VEC)
    ohq = np.zeros((tw, ATT_VEC), np.float32)
    ohk = np.zeros((tw, ATT_VEC), np.float32)
    ohq[_rel_index(dq), m] = 1.0
    ohk[_rel_index(dk), m] = 1.0
    return ohq, ohk


def _att_bias(table_pad):
    ohq, ohk = _bias_onehots()
    nslab = ATT_QB // SUBLANES

    def body(t_ref, ohq_ref, ohk_ref, bq_ref, bk_ref):
        tv = jnp.broadcast_to(t_ref[...], (SUBLANES, 3 * LANES))
        lane = lax.broadcasted_iota(jnp.int32, (ATT_QB, ATT_KW), 1)
        row = lax.broadcasted_iota(jnp.int32, (ATT_QB, ATT_KW), 0) // CHUNK
        col = lane // CHUNK
        band = (col >= row) & (col <= row + ATT_BAND - 1)
        for which, (oh_ref, out_ref) in enumerate(((ohq_ref, bq_ref), (ohk_ref, bk_ref))):
            vec = _dot(tv, oh_ref[...], HIGHEST)[0:1, :]
            slab = jnp.concatenate([vec if b == 0 else pltpu.roll(vec, b, 1) for b in range(SUBLANES)], axis=0)
            rows = [slab if a == 0 else pltpu.roll(slab, SUBLANES * a, 1) for a in range(nslab)]
            full = jnp.concatenate(rows, axis=0)[:, :ATT_KW]
            for v in range(3):
                inside = (lane >= (2 - v) * ATT_QB) if which == 0 else (lane < (v + 1) * ATT_QB)
                out_ref[v] = jnp.where(band & inside, full, NEG)

    h = table_pad.shape[0]
    oh_spec = pl.BlockSpec((3 * LANES, ATT_VEC), lambda i: (0, 0))
    out_spec = pl.BlockSpec((3, None, ATT_QB, ATT_KW), lambda i: (0, i, 0, 0))
    return pl.pallas_call(
        body, name="att_bias", grid=(h,),
        in_specs=[pl.BlockSpec((None, 1, 3 * LANES), lambda i: (i, 0, 0)), oh_spec, oh_spec],
        out_specs=[out_spec, out_spec],
        out_shape=[jax.ShapeDtypeStruct((3, h, ATT_QB, ATT_KW), F32)] * 2,
        compiler_params=_cparams("parallel"),
    )(table_pad, jnp.asarray(ohq), jnp.asarray(ohk))


def _head_masks():
    lane = lax.broadcasted_iota(jnp.int32, (1, LANES), 1)
    return [lane < ATT_DH, lane >= ATT_DH]


def _att_fwd(proj, bias_q):
    t = proj.shape[0]
    nb = t // ATT_QB
    scale = ATT_DH ** -0.5

    def body(q_ref, k0_ref, k1_ref, k2_ref, v0_ref, v1_ref, v2_ref, b_ref, o_ref, lse_ref, lset_ref):
        i = pl.program_id(0)
        q = (q_ref[...] * scale).astype(BF16)
        kk = jnp.concatenate([k0_ref[...], k1_ref[...], k2_ref[...]], axis=0).astype(BF16)
        vv = jnp.concatenate([v0_ref[...], v1_ref[...], v2_ref[...]], axis=0).astype(BF16)
        lane = lax.broadcasted_iota(jnp.int32, (1, LANES), 1)
        masks = _head_masks()
        lse_cols = jnp.zeros((ATT_QB, LANES), F32)
        for p in range(ATT_HEADS // 2):
            cs = slice(p * LANES, (p + 1) * LANES)
            qt, kt, vt = q[:, cs], kk[:, cs], vv[:, cs]
            acc = jnp.zeros((ATT_QB, LANES), F32)
            for sub in range(2):
                h = 2 * p + sub
                s = _dot_nt(jnp.where(masks[sub], qt, 0), kt) + b_ref[h]
                mx = jnp.max(s, axis=-1, keepdims=True)
                e = jnp.exp(s - mx)
                l = jnp.sum(e, axis=-1, keepdims=True)
                acc = acc + _dot(e.astype(BF16), jnp.where(masks[sub], vt, 0)) * (1.0 / l)
                lse_cols = lse_cols + jnp.where(lane == h, mx + jnp.log(l), 0.0)
            o_ref[:, cs] = acc.astype(BF16)
        lse_ref[...] = lse_cols
        lset_ref[...] = lse_cols.T[0:SUBLANES, :]

    def kv_spec(off, cb):
        return pl.BlockSpec((ATT_QB, PB), lambda i: (jnp.maximum(i + off, 0), cb))

    return pl.pallas_call(
        body, name="att_fwd", grid=(nb,),
        in_specs=[pl.BlockSpec((ATT_QB, PB), lambda i: (i, CB_QB)),
                  kv_spec(-2, CB_KB), kv_spec(-1, CB_KB), kv_spec(0, CB_KB),
                  kv_spec(-2, CB_VB), kv_spec(-1, CB_VB), kv_spec(0, CB_VB),
                  pl.BlockSpec((None, ATT_HEADS, ATT_QB, ATT_KW), lambda i: (jnp.minimum(i, 2), 0, 0, 0))],
        out_specs=[pl.BlockSpec((ATT_QB, WIDTH_B), lambda i: (i, 0)),
                   pl.BlockSpec((ATT_QB, LANES), lambda i: (i, 0)),
                   pl.BlockSpec((SUBLANES, ATT_QB), lambda i: (0, i))],
        out_shape=[jax.ShapeDtypeStruct((t, WIDTH_B), BF16), jax.ShapeDtypeStruct((t, LANES), F32),
                   jax.ShapeDtypeStruct((SUBLANES, t), F32)],
        compiler_params=_cparams("parallel"),
    )(proj, proj, proj, proj, proj, proj, proj, bias_q)


def _att_dq(proj, bias_q, lse, d_ob, ob, dproj):
    t = proj.shape[0]
    nb = t // ATT_QB
    scale = ATT_DH ** -0.5
    nslab = ATT_QB // SUBLANES

    def body(q_ref, k0_ref, k1_ref, k2_ref, v0_ref, v1_ref, v2_ref, b_ref, lse_ref, do_ref, o_ref, dp_in_ref,
             dq_ref, dlt_ref, slab_ref):
        i = pl.program_id(0)

        @pl.when(i == 0)
        def _():
            slab_ref[...] = jnp.zeros_like(slab_ref)

        q = (q_ref[...] * scale).astype(BF16)
        kk = jnp.concatenate([k0_ref[...], k1_ref[...], k2_ref[...]], axis=0).astype(BF16)
        vv = jnp.concatenate([v0_ref[...], v1_ref[...], v2_ref[...]], axis=0).astype(BF16)
        do = do_ref[...].astype(BF16)
        do_o = do_ref[...] * o_ref[...].astype(F32)
        lane = lax.broadcasted_iota(jnp.int32, (1, LANES), 1)
        masks = _head_masks()
        lse_all = lse_ref[...]
        dlt_cols = jnp.zeros((ATT_QB, LANES), F32)
        zpad = jnp.zeros((SUBLANES, ATT_VEC - ATT_KW), F32)
        for p in range(ATT_HEADS // 2):
            cs = slice(p * LANES, (p + 1) * LANES)
            qt, kt, vt, dot_ = q[:, cs], kk[:, cs], vv[:, cs], do[:, cs]
            acc = jnp.zeros((ATT_QB, LANES), F32)
            for sub in range(2):
                h = 2 * p + sub
                s = _dot_nt(jnp.where(masks[sub], qt, 0), kt) + b_ref[h]
                pr = jnp.exp(s - lse_all[:, h:h + 1])
                dp = _dot_nt(jnp.where(masks[sub], dot_, 0), vt)
                dl = jnp.sum(jnp.where(masks[sub], do_o[:, cs], 0.0), axis=-1, keepdims=True)
                ds = pr * (dp - dl)
                acc = acc + _dot(ds.astype(BF16), jnp.where(masks[sub], kt, 0)) * scale
                dlt_cols = dlt_cols + jnp.where(lane == h, dl, 0.0)
                sl = jnp.zeros((SUBLANES, ATT_VEC), F32)
                for a in range(nslab):
                    piece = jnp.concatenate([ds[a * SUBLANES:(a + 1) * SUBLANES, :], zpad], axis=1)
                    sl = sl + (piece if a == 0 else pltpu.roll(piece, ATT_VEC - SUBLANES * a, 1))
                slab_ref[h] += sl
            dq_ref[:, cs] = acc.astype(BF16)
        dlt_ref[...] = dlt_cols.T[0:SUBLANES, :]

    def kv_spec(off, cb):
        return pl.BlockSpec((ATT_QB, PB), lambda i: (jnp.maximum(i + off, 0), cb))

    return pl.pallas_call(
        body, name="att_dq", grid=(nb,),
        in_specs=[pl.BlockSpec((ATT_QB, PB), lambda i: (i, CB_QB)),
                  kv_spec(-2, CB_KB), kv_spec(-1, CB_KB), kv_spec(0, CB_KB),
                  kv_spec(-2, CB_VB), kv_spec(-1, CB_VB), kv_spec(0, CB_VB),
                  pl.BlockSpec((None, ATT_HEADS, ATT_QB, ATT_KW), lambda i: (jnp.minimum(i, 2), 0, 0, 0)),
                  pl.BlockSpec((ATT_QB, LANES), lambda i: (i, 0)),
                  pl.BlockSpec((ATT_QB, WIDTH_B), lambda i: (i, 0)), pl.BlockSpec((ATT_QB, WIDTH_B), lambda i: (i, 0)),
                  pl.BlockSpec(memory_space=pl.ANY)],
        out_specs=[_dp_spec(ATT_QB, DP_QB),
                   pl.BlockSpec((SUBLANES, ATT_QB), lambda i: (0, i)),
                   pl.BlockSpec((ATT_HEADS, SUBLANES, ATT_VEC), lambda i: (0, 0, 0))],
        out_shape=[jax.ShapeDtypeStruct(dproj.shape, dproj.dtype), jax.ShapeDtypeStruct((SUBLANES, t), F32),
                   jax.ShapeDtypeStruct((ATT_HEADS, SUBLANES, ATT_VEC), F32)],
        input_output_aliases={11: 0},
        compiler_params=_cparams("arbitrary"),
    )(proj, proj, proj, proj, proj, proj, proj, bias_q, lse, d_ob, ob, dproj)


def _att_dkv(proj, bias_k, lse_t, dlt_t, d_ob, dproj):
    t = proj.shape[0]
    nb = t // ATT_QB
    scale = ATT_DH ** -0.5

    def body(k_ref, v_ref, q0_ref, q1_ref, q2_ref, d0_ref, d1_ref, d2_ref, l0_ref, l1_ref, l2_ref,
             e0_ref, e1_ref, e2_ref, b_ref, dp_in_ref, dkv_ref):
        i = pl.program_id(0)
        k = k_ref[...].astype(BF16)
        v = v_ref[...].astype(BF16)
        qq = (jnp.concatenate([q0_ref[...], q1_ref[...], q2_ref[...]], axis=0) * scale).astype(BF16)
        do = jnp.concatenate([d0_ref[...], d1_ref[...], d2_ref[...]], axis=0).astype(BF16)
        lse = jnp.concatenate([l0_ref[...], l1_ref[...], l2_ref[...]], axis=1)
        dlt = jnp.concatenate([e0_ref[...], e1_ref[...], e2_ref[...]], axis=1)
        masks = _head_masks()
        for p in range(ATT_HEADS // 2):
            cs = slice(p * LANES, (p + 1) * LANES)
            kt, vt, qt, dot_ = k[:, cs], v[:, cs], qq[:, cs], do[:, cs]
            acc_k = jnp.zeros((ATT_QB, LANES), F32)
            acc_v = jnp.zeros((ATT_QB, LANES), F32)
            for sub in range(2):
                h = 2 * p + sub
                st = _dot_nt(jnp.where(masks[sub], kt, 0), qt) + b_ref[h]
                pt = jnp.exp(st - lse[h:h + 1, :])
                dot_m = jnp.where(masks[sub], dot_, 0)
                acc_v = acc_v + _dot(pt.astype(BF16), dot_m)
                dpt = _dot_nt(jnp.where(masks[sub], vt, 0), dot_)
                dst = pt * (dpt - dlt[h:h + 1, :])
                acc_k = acc_k + _dot(dst.astype(BF16), jnp.where(masks[sub], qt, 0))
            dkv_ref[:, cs] = acc_k.astype(BF16)
            dkv_ref[:, WIDTH_B + p * LANES:WIDTH_B + (p + 1) * LANES] = acc_v.astype(BF16)

    def q_spec(off, cb):
        return pl.BlockSpec((ATT_QB, PB), lambda i: (jnp.minimum(i + off, nb - 1), cb))

    def d_spec(off):
        return pl.BlockSpec((ATT_QB, WIDTH_B), lambda i: (jnp.minimum(i + off, nb - 1), 0))

    def r_spec(off):
        return pl.BlockSpec((SUBLANES, ATT_QB), lambda i: (0, jnp.minimum(i + off, nb - 1)))

    row = pl.BlockSpec((ATT_QB, WIDTH_B), lambda i: (i, 0))
    return pl.pallas_call(
        body, name="att_dkv", grid=(nb,),
        in_specs=[pl.BlockSpec((ATT_QB, PB), lambda i: (i, CB_KB)), pl.BlockSpec((ATT_QB, PB), lambda i: (i, CB_VB)),
                  q_spec(0, CB_QB), q_spec(1, CB_QB), q_spec(2, CB_QB),
                  d_spec(0), d_spec(1), d_spec(2), r_spec(0), r_spec(1), r_spec(2),
                  r_spec(0), r_spec(1), r_spec(2),
                  pl.BlockSpec((None, ATT_HEADS, ATT_QB, ATT_KW), lambda i: (jnp.minimum(nb - 1 - i, 2), 0, 0, 0)),
                  pl.BlockSpec(memory_space=pl.ANY)],
        out_specs=_dp_spec(ATT_QB, DP_KVB),
        out_shape=jax.ShapeDtypeStruct(dproj.shape, dproj.dtype),
        input_output_aliases={15: 0},
        compiler_params=_cparams("parallel"),
    )(proj, proj, proj, proj, proj, d_ob, d_ob, d_ob, lse_t, lse_t, lse_t, dlt_t, dlt_t, dlt_t, bias_k, dproj)


def _relbias_grad(slabs):
    ohq, _ = _bias_onehots()

    def body(s_ref, oh_ref, o_ref):
        sv = s_ref[...]
        vec = sv[0:1, :]
        for b in range(1, SUBLANES):
            vec = vec + pltpu.roll(sv[b:b + 1, :], ATT_VEC - b, 1)
        o_ref[...] = _dot_nt(jnp.broadcast_to(vec, (SUBLANES, ATT_VEC)), oh_ref[...], HIGHEST)[0:1, :]

    h = slabs.shape[0]
    return pl.pallas_call(
        body, name="att_dbias", grid=(h,),
        in_specs=[pl.BlockSpec((None, SUBLANES, ATT_VEC), lambda i: (i, 0, 0)),
                  pl.BlockSpec((3 * LANES, ATT_VEC), lambda i: (0, 0))],
        out_specs=pl.BlockSpec((None, 1, 3 * LANES), lambda i: (i, 0, 0)),
        out_shape=jax.ShapeDtypeStruct((h, 1, 3 * LANES), F32),
        compiler_params=_cparams("parallel"),
    )(slabs, jnp.asarray(ohq))


GDN_TM = 512
GDN_CB = 8
HALO = SUBLANES


def _conv_taps(ext, width, lead, n):
    return [(ext if k == width - 1 else pltpu.roll(ext, width - 1 - k, 0))[lead:lead + n] for k in range(width)]


def _prev_halo_spec(tm, width, cb):
    return pl.BlockSpec((HALO, width), lambda i: (jnp.maximum(i * (tm // HALO) - 1, 0), cb))


def _next_halo_spec(tm, width, cb, t):
    return pl.BlockSpec((HALO, width), lambda i: (jnp.minimum((i + 1) * (tm // HALO), t // HALO - 1), cb))


def _gdn_prep_fwd(proj, conv_w):
    t = proj.shape[0]
    tm = GDN_TM

    def body(q_ref, k_ref, v_ref, hq_ref, hk_ref, hv_ref, w_ref, qn_ref, kn_ref, vo_ref):
        first = pl.program_id(0) == 0
        for idx, (x_ref, h_ref, o_ref) in enumerate(((q_ref, hq_ref, qn_ref), (k_ref, hk_ref, kn_ref),
                                                      (v_ref, hv_ref, vo_ref))):
            halo = jnp.where(first, 0.0, h_ref[...])
            ext = jnp.concatenate([halo, x_ref[...]], axis=0)
            w = w_ref[:, idx * KEY_A:(idx + 1) * KEY_A]
            taps = _conv_taps(ext, GDN_CONV, HALO, tm)
            y = sum(w[k:k + 1, :] * taps[k] for k in range(GDN_CONV))
            a = y * _sigmoid(y)
            if idx < 2:
                for h in range(GDN_HEADS):
                    cs = slice(h * GDN_DK, (h + 1) * GDN_DK)
                    seg = a[:, cs]
                    o_ref[:, cs] = seg * lax.rsqrt(jnp.sum(seg * seg, axis=-1, keepdims=True) + EPS)
            else:
                o_ref[...] = a

    row = pl.BlockSpec((tm, KEY_A), lambda i: (i, 0))
    return pl.pallas_call(
        body, name="gdn_prep_fwd", grid=(t // tm,),
        in_specs=[pl.BlockSpec((tm, PB), lambda i: (i, CB_QA)), pl.BlockSpec((tm, PB), lambda i: (i, CB_KA)),
                  pl.BlockSpec((tm, PB), lambda i: (i, CB_VA)),
                  _prev_halo_spec(tm, PB, CB_QA), _prev_halo_spec(tm, PB, CB_KA), _prev_halo_spec(tm, PB, CB_VA),
                  pl.BlockSpec((GDN_CONV, 3 * KEY_A), lambda i: (0, 0))],
        out_specs=[row, row, row],
        out_shape=[jax.ShapeDtypeStruct((t, KEY_A), F32)] * 3,
        compiler_params=_cparams("parallel"),
    )(proj, proj, proj, proj, proj, proj, conv_w)


def _gdn_prep_bwd(proj, conv_w, dqn, dkn, dv, dproj):
    t = proj.shape[0]
    tm = GDN_TM
    nt = t // tm
    n_ext = tm + HALO

    def body(q_ref, k_ref, v_ref, pq_ref, pk_ref, pv_ref, nq_ref, nk_ref, nv_ref,
             dq_ref, dk_ref, dv_ref, ndq_ref, ndk_ref, ndv_ref, w_ref, dp_in_ref, out_ref, dw_ref):
        i = pl.program_id(0)
        first, last = i == 0, i == nt - 1

        @pl.when(first)
        def _():
            dw_ref[...] = jnp.zeros_like(dw_ref)

        groups = ((q_ref, pq_ref, nq_ref, dq_ref, ndq_ref), (k_ref, pk_ref, nk_ref, dk_ref, ndk_ref),
                  (v_ref, pv_ref, nv_ref, dv_ref, ndv_ref))
        for idx, (x_ref, p_ref, n_ref, d_ref, nd_ref) in enumerate(groups):
            cs_all = slice(idx * KEY_A, (idx + 1) * KEY_A)
            ext = jnp.concatenate([jnp.where(first, 0.0, p_ref[...]), x_ref[...], jnp.where(last, 0.0, n_ref[...])], axis=0)
            w = w_ref[:, cs_all]
            taps = _conv_taps(ext, GDN_CONV, HALO, n_ext)
            y = sum(w[k:k + 1, :] * taps[k] for k in range(GDN_CONV))
            sg = _sigmoid(y)
            a = y * sg
            dup = jnp.concatenate([d_ref[...], jnp.where(last, 0.0, nd_ref[...])], axis=0)
            if idx < 2:
                segs = []
                for h in range(GDN_HEADS):
                    cs = slice(h * GDN_DK, (h + 1) * GDN_DK)
                    seg = a[:, cs]
                    r = lax.rsqrt(jnp.sum(seg * seg, axis=-1, keepdims=True) + EPS)
                    nrm = seg * r
                    dn = dup[:, cs]
                    segs.append(r * (dn - nrm * jnp.sum(dn * nrm, axis=-1, keepdims=True)))
                da = jnp.concatenate(segs, axis=1)
            else:
                da = dup
            dy = da * sg * (1.0 + y * (1.0 - sg))
            dx = sum(w[k:k + 1, :] * (dy if k == GDN_CONV - 1 else pltpu.roll(dy, n_ext - (GDN_CONV - 1 - k), 0))[:tm]
                     for k in range(GDN_CONV))
            out_ref[:, cs_all] = dx.astype(BF16)
            for k in range(GDN_CONV):
                dw_ref[k:k + 1, cs_all] += jnp.sum(dy[:tm] * taps[k][:tm], axis=0, keepdims=True)

    row = pl.BlockSpec((tm, KEY_A), lambda i: (i, 0))
    nrow = _next_halo_spec(tm, KEY_A, 0, t)
    return pl.pallas_call(
        body, name="gdn_prep_bwd", grid=(nt,),
        in_specs=[pl.BlockSpec((tm, PB), lambda i: (i, CB_QA)), pl.BlockSpec((tm, PB), lambda i: (i, CB_KA)),
                  pl.BlockSpec((tm, PB), lambda i: (i, CB_VA)),
                  _prev_halo_spec(tm, PB, CB_QA), _prev_halo_spec(tm, PB, CB_KA), _prev_halo_spec(tm, PB, CB_VA),
                  _next_halo_spec(tm, PB, CB_QA, t), _next_halo_spec(tm, PB, CB_KA, t), _next_halo_spec(tm, PB, CB_VA, t),
                  row, row, row, nrow, nrow, nrow,
                  pl.BlockSpec((GDN_CONV, 3 * KEY_A), lambda i: (0, 0)), pl.BlockSpec(memory_space=pl.ANY)],
        out_specs=[_dp_spec(tm, DP_QKVA), pl.BlockSpec((SUBLANES, 3 * KEY_A), lambda i: (0, 0))],
        out_shape=[jax.ShapeDtypeStruct(dproj.shape, dproj.dtype), jax.ShapeDtypeStruct((SUBLANES, 3 * KEY_A), F32)],
        input_output_aliases={16: 0},
        compiler_params=_cparams("arbitrary"),
    )(proj, proj, proj, proj, proj, proj, proj, proj, proj, dqn, dkn, dv, dqn, dkn, dv, conv_w, dproj)


class _Pair(dict):
    __getattr__ = dict.__getitem__
    __setattr__ = dict.__setitem__


def _pairs_to_lanes(cols):
    lane = lax.broadcasted_iota(jnp.int32, (1, LANES), 1)
    out = jnp.zeros((cols[0].shape[0], LANES), F32)
    for p, col in enumerate(cols):
        out = out + jnp.where(lane == p, col, 0.0)
    return out


def _gdn_terms(bd, par, kn_ref, qn_ref):
    c = CHUNK
    ii = lax.broadcasted_iota(jnp.int32, (c, c), 0)
    jj = lax.broadcasted_iota(jnp.int32, (c, c), 1)
    strict, incl = ii > jj, ii >= jj
    ltri = incl.astype(F32)
    ts = []
    for cc in range(GDN_CB):
        for h in range(GDN_HEADS):
            t = _Pair(cc=cc, h=h, rows=slice(cc * c, (cc + 1) * c), cs=slice(h * GDN_DK, (h + 1) * GDN_DK),
                      strict=strict, incl=incl)
            t.beta = _sigmoid(bd[t.rows, h:h + 1])
            t.ea = jnp.exp(par[0:1, h:h + 1])
            t.sp_arg = bd[t.rows, GDN_HEADS + h:GDN_HEADS + h + 1] + par[1:2, h:h + 1]
            t.g = -t.ea * _softplus(t.sp_arg)
            t.k = kn_ref[t.rows, t.cs]
            t.q = qn_ref[t.rows, t.cs] * (GDN_DK ** -0.5)
            t.kb, t.qb = t.k.astype(BF16), t.q.astype(BF16)
            ts.append(t)
    gall = _dot(ltri, _pairs_to_lanes([t.g for t in ts]), HIGHEST)
    gall_t = gall.T
    for p, t in enumerate(ts):
        t.gb = jnp.broadcast_to(gall[:, p:p + 1], (c, GDN_DK))
    for t in ts:
        t.kk = _dot_nt(t.kb, t.kb)
        t.qk = _dot_nt(t.qb, t.kb)
    for p, t in enumerate(ts):
        diff = t.gb[:, :c] - gall_t[p:p + 1, :]
        t.dec_s = jnp.exp(jnp.where(strict, diff, NEG))
        t.dec_i = jnp.exp(jnp.where(incl, diff, NEG))
        t.gam = jnp.exp(t.gb)
        glast = t.gb[c - 1:c, :]
        t.e_rest = jnp.exp(glast - t.gb)
        t.gl = jnp.exp(glast)
        t.p = t.qk * t.dec_i
    return ts


def _gdn_fwd(qn, kn, v, proj, par, gnw):
    t = qn.shape[0]
    c = CHUNK
    nc = t // c
    r_ = GDN_CB * c

    def body(qn_ref, kn_ref, v_ref, bd_ref, z_ref, par_ref, gnw_ref,
             oan_ref, o_ref, sp_ref, w_ref, u_ref, tm_ref, s_ref):
        @pl.when(pl.program_id(0) == 0)
        def _():
            s_ref[...] = jnp.zeros_like(s_ref)

        bd, par, gnw_v = bd_ref[...], par_ref[...], gnw_ref[...]
        eye = (lax.broadcasted_iota(jnp.int32, (c, c), 0) == lax.broadcasted_iota(jnp.int32, (c, c), 1)).astype(F32)
        ts = _gdn_terms(bd, par, kn_ref, qn_ref)
        for t in ts:
            t.vv = v_ref[t.rows, t.cs]
            t.x = -(t.beta * t.kk * t.dec_s)
            t.tinv = eye + t.x
        for t in ts:
            t.xs = _split(t.x)
        for _ in range(5):
            for t in ts:
                t.xs = _split(_dot3s(t.xs, t.xs))
            for t in ts:
                t.tinv = t.tinv + _dot3s(_split(t.tinv), t.xs)
        for t in ts:
            tsp = _split(t.tinv)
            t.wm = _dot3s(tsp, _split((t.beta * t.gam) * t.k))
            t.uv = _dot3s(tsp, _split(t.beta * t.vv))
        for t in ts:
            w_ref[t.rows, t.cs] = t.wm
            tm_ref[t.cc, t.h] = t.tinv.T
            t.wb = t.wm.astype(BF16)
            t.qgb = (t.q * t.gam).astype(BF16)
            t.kdb = (t.k * t.e_rest).astype(BF16)
            t.pb = t.p.astype(BF16)
        state = [s_ref[h] for h in range(GDN_HEADS)]
        for cc in range(GDN_CB):
            tc = [t for t in ts if t.cc == cc]
            for t in tc:
                t.sh = state[t.h]
                t.sb = t.sh.astype(BF16)
            for t in tc:
                t.ws = _dot(t.wb, t.sb)
            for t in tc:
                t.u = t.uv - t.ws
                t.ub = t.u.astype(BF16)
            for t in tc:
                state[t.h] = t.gl * t.sh + _dot_tn(t.kdb, t.ub)
            for t in tc:
                t.o = _dot(t.qgb, t.sb) + _dot(t.pb, t.ub)
                sp_ref[cc, t.h] = t.sh
                u_ref[t.rows, t.cs] = t.u
                o_ref[t.rows, t.cs] = t.o
        for h in range(GDN_HEADS):
            s_ref[h] = state[h]
        for t in ts:
            zz = z_ref[t.rows, t.cs]
            rr = lax.rsqrt(jnp.mean(t.o * t.o, axis=-1, keepdims=True) + EPS)
            oan_ref[t.rows, t.cs] = ((t.o * rr) * gnw_v * (zz * _sigmoid(zz))).astype(BF16)

    row = pl.BlockSpec((r_, KEY_A), lambda i: (i, 0))
    return pl.pallas_call(
        body, name="gdn_fwd", grid=(nc // GDN_CB,),
        in_specs=[row, row, row, pl.BlockSpec((r_, LANES), lambda i: (i, CB_BD)),
                  pl.BlockSpec((r_, PB), lambda i: (i, CB_ZA)),
                  pl.BlockSpec((SUBLANES, LANES), lambda i: (0, 0)), pl.BlockSpec((1, GDN_DK), lambda i: (0, 0))],
        out_specs=[row, row, pl.BlockSpec((GDN_CB, GDN_HEADS, GDN_DK, GDN_DK), lambda i: (i, 0, 0, 0)),
                   row, row, pl.BlockSpec((GDN_CB, GDN_HEADS, c, c), lambda i: (i, 0, 0, 0))],
        out_shape=[jax.ShapeDtypeStruct((t, KEY_A), BF16), jax.ShapeDtypeStruct((t, KEY_A), F32),
                   jax.ShapeDtypeStruct((nc, GDN_HEADS, GDN_DK, GDN_DK), F32),
                   jax.ShapeDtypeStruct((t, KEY_A), F32), jax.ShapeDtypeStruct((t, KEY_A), F32),
                   jax.ShapeDtypeStruct((nc, GDN_HEADS, c, c), F32)],
        scratch_shapes=[pltpu.VMEM((GDN_HEADS, GDN_DK, GDN_DK), F32)],
        compiler_params=_cparams("arbitrary"),
    )(qn, kn, v, proj, proj, par, gnw)


def _gdn_bwd(qn, kn, v, proj, par, gnw, o, sprev, wst, ust, tst, d_oan, dproj):
    t = qn.shape[0]
    c = CHUNK
    nc = t // c
    nb = nc // GDN_CB
    r_ = GDN_CB * c

    def body(qn_ref, kn_ref, v_ref, bd_ref, z_ref, par_ref, gnw_ref, o_ref, sp_ref, w_ref, u_ref, tm_ref, do_ref,
             dp_in_ref, dqn_ref, dkn_ref, dv_ref, dzb_ref, acc_ref, ds_ref):
        @pl.when(pl.program_id(0) == 0)
        def _():
            ds_ref[...] = jnp.zeros_like(ds_ref)
            acc_ref[...] = jnp.zeros_like(acc_ref)

        bd, par, gnw_v = bd_ref[...], par_ref[...], gnw_ref[...]
        lane = lax.broadcasted_iota(jnp.int32, (1, LANES), 1)
        rix = lax.broadcasted_iota(jnp.int32, (c, 1), 0)
        ii = lax.broadcasted_iota(jnp.int32, (c, c), 0)
        jj = lax.broadcasted_iota(jnp.int32, (c, c), 1)
        upper = (jj >= ii).astype(F32)
        acc_a = jnp.zeros((1, LANES), F32)
        acc_d = jnp.zeros((1, LANES), F32)
        acc_g = jnp.zeros((1, LANES), F32)
        ts = _gdn_terms(bd, par, kn_ref, qn_ref)
        for t in ts:
            t.vv = v_ref[t.rows, t.cs]
            t.sh = sp_ref[t.cc, t.h]
            t.sb = t.sh.astype(BF16)
            t.wm, t.u, t.tinv_t = w_ref[t.rows, t.cs], u_ref[t.rows, t.cs], tm_ref[t.cc, t.h]
            t.wb, t.ub = t.wm.astype(BF16), t.u.astype(BF16)
            ov, zz, dout = o_ref[t.rows, t.cs], z_ref[t.rows, t.cs], do_ref[t.rows, t.cs]
            sg = _sigmoid(zz)
            sil = zz * sg
            rr = lax.rsqrt(jnp.mean(ov * ov, axis=-1, keepdims=True) + EPS)
            on = ov * rr
            dzb_ref[t.rows, t.cs] = (dout * on * gnw_v * (sg * (1.0 + zz * (1.0 - sg)))).astype(BF16)
            acc_g = acc_g + jnp.sum(dout * on * sil, axis=0, keepdims=True)
            don = dout * gnw_v * sil
            t.dob = (rr * (don - on * jnp.mean(don * on, axis=-1, keepdims=True))).astype(BF16)
            t.qg = t.q * t.gam
            t.kd = t.k * t.e_rest
            t.qgb, t.kdb = t.qg.astype(BF16), t.kd.astype(BF16)
            t.ptb = t.p.T.astype(BF16)
        for t in ts:
            t.du0 = _dot(t.ptb, t.dob)
            t.ds0 = _dot_tn(t.qgb, t.dob)
            t.dqg = _dot_nt(t.dob, t.sb)
            t.dp = _dot_nt(t.dob, t.ub)
            t.uv = t.u + _dot(t.wb, t.sb)
        dstate = [ds_ref[h] for h in range(GDN_HEADS)]
        for cc in reversed(range(GDN_CB)):
            tc = [t for t in ts if t.cc == cc]
            for t in tc:
                t.dsn = dstate[t.h]
                t.dsnb = t.dsn.astype(BF16)
            for t in tc:
                t.du = t.du0 + _dot(t.kdb, t.dsnb)
            for t in tc:
                t.dub = t.du.astype(BF16)
            for t in tc:
                dstate[t.h] = t.gl * t.dsn + t.ds0 - _dot_tn(t.wb, t.dub)
            for t in tc:
                t.dkd = _dot_nt(t.ub, t.dsnb)
                t.dgl = jnp.sum(jnp.sum(t.dsn * t.sh, axis=1, keepdims=True), axis=0, keepdims=True)
                t.dwm = -_dot_nt(t.dub, t.sb)
        for h in range(GDN_HEADS):
            ds_ref[h] = dstate[h]
        for t in ts:
            tsp = _split(t.tinv_t)
            t.dbk = _dot3s(tsp, _split(t.dwm))
            t.dbv = _dot3s(tsp, _split(t.du))
        for t in ts:
            d_a = -(_dot_nt(t.dbk.astype(BF16), t.wb) + _dot_nt(t.dbv.astype(BF16), t.uv.astype(BF16)))
            t.d_a = jnp.where(t.strict, d_a, 0.0)
        for t in ts:
            t.dkk = t.d_a * t.beta * t.dec_s
            t.dqk = t.dp * t.dec_i
            t.dqkb = t.dqk.astype(BF16)
        for t in ts:
            t.dq = _dot(t.dqkb, t.kb) + t.dqg * t.gam
            t.dk = (t.dbk * (t.beta * t.gam) + _dot_tn(t.dqkb, t.qb) + _dot((t.dkk + t.dkk.T).astype(BF16), t.kb)
                    + t.dkd * t.e_rest)
        for t in ts:
            dbeta = (jnp.sum(t.d_a * t.kk * t.dec_s, axis=-1, keepdims=True)
                     + jnp.sum(t.dbk * t.k * t.gam, axis=-1, keepdims=True) + jnp.sum(t.dbv * t.vv, axis=-1, keepdims=True))
            t.dbl = dbeta * t.beta * (1.0 - t.beta)
            dv_ref[t.rows, t.cs] = t.dbv * t.beta
            bk = (t.beta * t.gam) * t.k
            zc = jnp.sum(t.dkd * t.kd, axis=-1, keepdims=True)
            xs = t.dkk * t.kk + t.dp * t.p
            dgc = (jnp.sum(xs, axis=-1, keepdims=True) - jnp.sum(xs.T, axis=-1, keepdims=True)
                   + jnp.sum(t.dbk * bk, axis=-1, keepdims=True) + jnp.sum(t.dqg * t.qg, axis=-1, keepdims=True) - zc)
            dglast = jnp.sum(zc, axis=0, keepdims=True) + t.dgl * t.gl[:, 0:1]
            t.dgc = dgc + jnp.where(rix == c - 1, dglast, 0.0)
        dgall = _dot(upper, _pairs_to_lanes([t.dgc for t in ts]), HIGHEST)
        for p, t in enumerate(ts):
            t.dg = dgall[:, p:p + 1]
        dbd_tiles = [jnp.zeros((c, LANES), F32) for _ in range(GDN_CB)]
        for t in ts:
            ddl = t.dg * (-t.ea) * _sigmoid(t.sp_arg)
            acc_a = acc_a + jnp.where(lane == t.h, jnp.sum(t.dg * t.g, axis=0, keepdims=True), 0.0)
            acc_d = acc_d + jnp.where(lane == t.h, jnp.sum(ddl, axis=0, keepdims=True), 0.0)
            dbd_tiles[t.cc] = (dbd_tiles[t.cc] + jnp.where(lane == t.h, t.dbl, 0.0)
                               + jnp.where(lane == GDN_HEADS + t.h, ddl, 0.0))
            dqn_ref[t.rows, t.cs] = t.dq * (GDN_DK ** -0.5)
            dkn_ref[t.rows, t.cs] = t.dk
        for cc in range(GDN_CB):
            dzb_ref[cc * c:(cc + 1) * c, KEY_A:KEY_A + LANES] = dbd_tiles[cc].astype(BF16)
        acc_ref[0:1, :] += acc_a
        acc_ref[1:2, :] += acc_d
        acc_ref[2:3, :] += acc_g

    def rev(i):
        return nb - 1 - i

    row = pl.BlockSpec((r_, KEY_A), lambda i: (rev(i), 0))
    st = pl.BlockSpec((GDN_CB, GDN_HEADS, GDN_DK, GDN_DK), lambda i: (rev(i), 0, 0, 0))
    tt_spec = pl.BlockSpec((GDN_CB, GDN_HEADS, c, c), lambda i: (rev(i), 0, 0, 0))
    return pl.pallas_call(
        body, name="gdn_bwd", grid=(nb,),
        in_specs=[row, row, row, pl.BlockSpec((r_, LANES), lambda i: (rev(i), CB_BD)),
                  pl.BlockSpec((r_, PB), lambda i: (rev(i), CB_ZA)),
                  pl.BlockSpec((SUBLANES, LANES), lambda i: (0, 0)), pl.BlockSpec((1, GDN_DK), lambda i: (0, 0)),
                  row, st, row, row, tt_spec, row, pl.BlockSpec(memory_space=pl.ANY)],
        out_specs=[row, row, row, _dp_spec(r_, DP_ZBD, rev), pl.BlockSpec((SUBLANES, LANES), lambda i: (0, 0))],
        out_shape=[jax.ShapeDtypeStruct((t, KEY_A), F32)] * 3 + [jax.ShapeDtypeStruct(dproj.shape, dproj.dtype),
                                                                jax.ShapeDtypeStruct((SUBLANES, LANES), F32)],
        input_output_aliases={13: 3},
        scratch_shapes=[pltpu.VMEM((GDN_HEADS, GDN_DK, GDN_DK), F32)],
        compiler_params=_cparams("arbitrary"),
    )(qn, kn, v, proj, proj, par, gnw, o, sprev, wst, ust, tst, d_oan, dproj)


def _merge_fwd(oan, ob, proj, x, wba, wbb, wout, w2, tm=512):
    t = x.shape[0]

    def body(oa_ref, ob_ref, ga_ref, gb_ref, x_ref, wba_ref, wbb_ref, wout_ref, w2_ref, x2_ref, h2_ref):
        ya = _dot(oa_ref[...], wba_ref[...])
        yb = _dot(ob_ref[...], wbb_ref[...])
        mix = _sigmoid(ga_ref[...]) * ya + _sigmoid(gb_ref[...]) * yb
        x2 = x_ref[...] + _dot(mix.astype(BF16), wout_ref[...])
        x2_ref[...] = x2
        r = lax.rsqrt(jnp.mean(x2 * x2, axis=-1, keepdims=True) + EPS)
        h2_ref[...] = (x2 * r * w2_ref[...]).astype(BF16)

    half = pl.BlockSpec((tm, KEY_A), lambda i: (i, 0))
    row = pl.BlockSpec((tm, D_MODEL), lambda i: (i, 0))
    wsmall = pl.BlockSpec((KEY_A, D_MODEL), lambda i: (0, 0))
    return pl.pallas_call(
        body, name="merge_fwd", grid=(t // tm,),
        in_specs=[half, half, pl.BlockSpec((tm, D_MODEL), lambda i: (i, CB_GA)),
                  pl.BlockSpec((tm, D_MODEL), lambda i: (i, CB_GB)), row, wsmall, wsmall,
                  pl.BlockSpec((D_MODEL, D_MODEL), lambda i: (0, 0)), pl.BlockSpec((1, D_MODEL), lambda i: (0, 0))],
        out_specs=[row, row],
        out_shape=[jax.ShapeDtypeStruct((t, D_MODEL), F32), jax.ShapeDtypeStruct((t, D_MODEL), BF16)],
        compiler_params=_cparams("parallel"),
    )(oan, ob, proj, proj, x, wba, wbb, wout, w2)


def _merge_bwd(dx2b, oan, ob, proj, wba, wbb, wout_t, wba_t, wbb_t, tm=512):
    t = dx2b.shape[0]

    def body(dx_ref, oa_ref, ob_ref, ga_ref, gb_ref, wba_ref, wbb_ref, woutt_ref, wbat_ref, wbbt_ref,
             dg_ref, doa_ref, dob_ref, gout_ref, gba_ref, gbb_ref):
        @pl.when(pl.program_id(0) == 0)
        def _():
            gout_ref[...] = jnp.zeros_like(gout_ref)
            gba_ref[...] = jnp.zeros_like(gba_ref)
            gbb_ref[...] = jnp.zeros_like(gbb_ref)

        dx, oa, ob = dx_ref[...], oa_ref[...], ob_ref[...]
        dmix = _dot(dx, woutt_ref[...])
        ya = _dot(oa, wba_ref[...])
        yb = _dot(ob, wbb_ref[...])
        sa, sb = _sigmoid(ga_ref[...]), _sigmoid(gb_ref[...])
        gout_ref[...] += _dot_tn((sa * ya + sb * yb).astype(BF16), dx)
        dg_ref[:, :D_MODEL] = (dmix * ya * sa * (1.0 - sa)).astype(BF16)
        dg_ref[:, D_MODEL:] = (dmix * yb * sb * (1.0 - sb)).astype(BF16)
        dya = (dmix * sa).astype(BF16)
        dyb = (dmix * sb).astype(BF16)
        gba_ref[...] += _dot_tn(oa, dya)
        gbb_ref[...] += _dot_tn(ob, dyb)
        doa_ref[...] = _dot(dya, wbat_ref[...])
        dob_ref[...] = _dot(dyb, wbbt_ref[...])

    half = pl.BlockSpec((tm, KEY_A), lambda i: (i, 0))
    row = pl.BlockSpec((tm, D_MODEL), lambda i: (i, 0))
    wsmall = pl.BlockSpec((KEY_A, D_MODEL), lambda i: (0, 0))
    wsmall_t = pl.BlockSpec((D_MODEL, KEY_A), lambda i: (0, 0))
    wfull = pl.BlockSpec((D_MODEL, D_MODEL), lambda i: (0, 0))
    return pl.pallas_call(
        body, name="merge_bwd", grid=(t // tm,),
        in_specs=[row, half, half, pl.BlockSpec((tm, D_MODEL), lambda i: (i, CB_GA)),
                  pl.BlockSpec((tm, D_MODEL), lambda i: (i, CB_GB)), wsmall, wsmall, wfull, wsmall_t, wsmall_t],
        out_specs=[_dp_spec(tm, DP_GATES), half, half, wfull, wsmall, wsmall],
        out_shape=[jax.ShapeDtypeStruct((t, PROJ_W), BF16), jax.ShapeDtypeStruct((t, KEY_A), F32),
                   jax.ShapeDtypeStruct((t, KEY_A), F32), jax.ShapeDtypeStruct((D_MODEL, D_MODEL), F32),
                   jax.ShapeDtypeStruct((KEY_A, D_MODEL), F32), jax.ShapeDtypeStruct((WIDTH_B, D_MODEL), F32)],
        compiler_params=_cparams("arbitrary"),
    )(dx2b, oan, ob, proj, proj, wba, wbb, wout_t, wba_t, wbb_t)


FFN_TM = 128
FFN_W = 2 * D_FF


def _ffn_conv(up_ref, halo_ref, cw_ref, cb_ref, first):
    ext = jnp.concatenate([jnp.where(first, 0.0, halo_ref[...]), up_ref[...]], axis=0)
    taps = _conv_taps(ext, FFN_CONV, HALO, FFN_TM)
    cw = cw_ref[...]
    u = sum(cw[k:k + 1, :] * taps[k] for k in range(FFN_CONV)) + cb_ref[...]
    return u, taps


def _resident(shape):
    return pl.BlockSpec(shape, lambda i: (0,) * len(shape), pipeline_mode=pl.Buffered(1))


def _ffn_fwd(h2, wup, cw, cb, wdown, x2, tgt, w3):
    t = x2.shape[0]
    tm = FFN_TM

    def body(h2_ref, wup_ref, cw_ref, cb_ref, wd_ref, x2_ref, tgt_ref, w3_ref, up_ref, dx_ref, dxb_ref, act_ref, acc_ref,
             prev_ref):
        @pl.when(pl.program_id(0) == 0)
        def _():
            acc_ref[...] = jnp.zeros_like(acc_ref)
            prev_ref[...] = jnp.zeros_like(prev_ref)

        up = _dot(h2_ref[...], wup_ref[...])
        up_ref[...] = up
        ext = jnp.concatenate([prev_ref[...], up], axis=0)
        prev_ref[...] = up[tm - HALO:tm]
        taps = _conv_taps(ext, FFN_CONV, HALO, tm)
        cw_v = cw_ref[...]
        u = sum(cw_v[k:k + 1, :] * taps[k] for k in range(FFN_CONV)) + cb_ref[...]
        gate, upp = u[:, :D_FF], u[:, D_FF:]
        act = (gate * _sigmoid(gate) * upp).astype(BF16)
        act_ref[...] = act
        x3 = x2_ref[...] + _dot(act, wd_ref[...])
        r = lax.rsqrt(jnp.mean(x3 * x3, axis=-1, keepdims=True) + EPS)
        xh = x3 * r
        w3v = w3_ref[...]
        err = xh * w3v - tgt_ref[...]
        loss = 0.5 * jnp.sum(jnp.mean(err * err, axis=-1, keepdims=True), axis=0, keepdims=True)
        dy = err * (1.0 / D_MODEL)
        acc_ref[0:1, :] += jnp.sum(dy * xh, axis=0, keepdims=True)
        acc_ref[1:2, :] += jnp.broadcast_to(loss, (1, D_MODEL))
        dxh = dy * w3v
        dx = r * (dxh - xh * jnp.mean(dxh * xh, axis=-1, keepdims=True))
        dx_ref[...] = dx
        dxb_ref[...] = dx.astype(BF16)

    row = pl.BlockSpec((tm, D_MODEL), lambda i: (i, 0))
    return pl.pallas_call(
        body, name="ffn_fwd", grid=(t // tm,),
        in_specs=[row, _resident((D_MODEL, FFN_W)), _resident((SUBLANES, FFN_W)), _resident((1, FFN_W)),
                  _resident((D_FF, D_MODEL)), row, row, _resident((1, D_MODEL))],
        out_specs=[pl.BlockSpec((tm, FFN_W), lambda i: (i, 0)), row, row, pl.BlockSpec((tm, D_FF), lambda i: (i, 0)),
                   pl.BlockSpec((SUBLANES, D_MODEL), lambda i: (0, 0))],
        out_shape=[jax.ShapeDtypeStruct((t, FFN_W), F32), jax.ShapeDtypeStruct((t, D_MODEL), F32),
                   jax.ShapeDtypeStruct((t, D_MODEL), BF16), jax.ShapeDtypeStruct((t, D_FF), BF16),
                   jax.ShapeDtypeStruct((SUBLANES, D_MODEL), F32)],
        scratch_shapes=[pltpu.VMEM((HALO, FFN_W), F32)],
        compiler_params=_cparams("arbitrary"),
    )(h2, wup, cw, cb, wdown, x2, tgt, w3)


def _ffn_bwd(dx3b, wdown_t, up, cw, cb, wup_t, x2, w2, dx3, carry):
    t = up.shape[0]
    tm = FFN_TM
    nt = t // tm
    n_ext = tm + HALO
    arrays, scatter = carry
    nx = len(arrays)

    def rev(i):
        return nt - 1 - i

    def body(*refs):
        dx_ref, wdt_ref, up_ref, halo_ref, cw_ref, cb_ref, wupt_ref, x2_ref, w2_ref, dres_ref = refs[:10]
        srcs = refs[10:10 + nx]
        dup_ref, acc_ref, dx2_ref, dx2b_ref, dw2_ref = refs[10 + nx:15 + nx]
        dsts = refs[15 + nx:15 + 2 * nx]
        nxt_ref = refs[15 + 2 * nx]
        i = pl.program_id(0)
        local, remote = _exchange_copies(srcs, dsts, scatter, *refs[16 + 2 * nx:])

        @pl.when(i == 0)
        def _():
            for cp in local + remote:
                cp.start()
            acc_ref[...] = jnp.zeros_like(acc_ref)
            dw2_ref[...] = jnp.zeros_like(dw2_ref)
            nxt_ref[...] = jnp.zeros_like(nxt_ref)

        dact = _dot(dx_ref[...], wdt_ref[...])
        u, taps = _ffn_conv(up_ref, halo_ref, cw_ref, cb_ref, i == nt - 1)
        gate, upp = u[:, :D_FF], u[:, D_FF:]
        sg = _sigmoid(gate)
        du = jnp.concatenate([dact * upp * (sg * (1.0 + gate * (1.0 - sg))), dact * (gate * sg)], axis=1)
        for k in range(FFN_CONV):
            acc_ref[k:k + 1, :] += jnp.sum(du * taps[k], axis=0, keepdims=True)
        acc_ref[FFN_CONV:FFN_CONV + 1, :] += jnp.sum(du, axis=0, keepdims=True)
        ext = jnp.concatenate([du, nxt_ref[...]], axis=0)
        cw_v = cw_ref[...]
        dup = cw_v[FFN_CONV - 1:FFN_CONV, :] * du
        for k in range(FFN_CONV - 1):
            dup = dup + cw_v[k:k + 1, :] * pltpu.roll(ext, n_ext - (FFN_CONV - 1 - k), 0)[:tm]
        dupb = dup.astype(BF16)
        dup_ref[...] = dupb
        nxt_ref[...] = du[0:HALO]
        dhv = _dot(dupb, wupt_ref[...])
        xv = x2_ref[...]
        r = lax.rsqrt(jnp.mean(xv * xv, axis=-1, keepdims=True) + EPS)
        xh = xv * r
        dw2_ref[0:1, :] += jnp.sum(dhv * xh, axis=0, keepdims=True)
        dxh = dhv * w2_ref[...]
        dx2 = dres_ref[...] + r * (dxh - xh * jnp.mean(dxh * xh, axis=-1, keepdims=True))
        dx2_ref[...] = dx2
        dx2b_ref[...] = dx2.astype(BF16)

        @pl.when(i == nt - 1)
        def _():
            for cp in remote + local:
                cp.wait()

    wide = pl.BlockSpec((tm, FFN_W), lambda i: (rev(i), 0))
    row = pl.BlockSpec((tm, D_MODEL), lambda i: (rev(i), 0))
    any_spec = pl.BlockSpec(memory_space=pl.ANY)
    return pl.pallas_call(
        body, name="ffn_bwd", grid=(nt,),
        in_specs=[row, _resident((D_MODEL, D_FF)),
                  wide, pl.BlockSpec((HALO, FFN_W), lambda i: (jnp.maximum(rev(i) * (tm // HALO) - 1, 0), 0)),
                  _resident((SUBLANES, FFN_W)), _resident((1, FFN_W)), _resident((FFN_W, D_MODEL)), row,
                  _resident((1, D_MODEL)), row] + [any_spec] * nx,
        out_specs=[wide, pl.BlockSpec((SUBLANES, FFN_W), lambda i: (0, 0)), row, row,
                   pl.BlockSpec((SUBLANES, D_MODEL), lambda i: (0, 0))] + [any_spec] * nx,
        out_shape=[jax.ShapeDtypeStruct((t, FFN_W), BF16), jax.ShapeDtypeStruct((SUBLANES, FFN_W), F32),
                   jax.ShapeDtypeStruct((t, D_MODEL), F32), jax.ShapeDtypeStruct((t, D_MODEL), BF16),
                   jax.ShapeDtypeStruct((SUBLANES, D_MODEL), F32)] + _exchange_shapes(arrays, scatter),
        scratch_shapes=[pltpu.VMEM((HALO, FFN_W), F32)] + _exchange_sems(nx),
        compiler_params=_cparams("arbitrary"),
    )(dx3b, wdown_t, up, up, cw, cb, wup_t, x2, w2, dx3, *arrays)


def _adamw(parts, w, m, v, name, tr):
    r, cols = w.shape

    def body(p_ref, w_ref, m_ref, v_ref, g_ref, d_ref, mo_ref, vo_ref):
        g = p_ref[0].astype(F32)
        for s in range(1, N_DEV):
            g = g + p_ref[s].astype(F32)
        mm = ADAM_B1 * m_ref[...] + (1.0 - ADAM_B1) * g
        vv = ADAM_B2 * v_ref[...] + (1.0 - ADAM_B2) * (g * g)
        m_hat = mm / (1.0 - ADAM_B1 ** ADAM_STEP)
        v_hat = vv / (1.0 - ADAM_B2 ** ADAM_STEP)
        g_ref[...] = g
        d_ref[...] = -ADAM_LR * (m_hat / (jnp.sqrt(v_hat) + ADAM_EPS) + ADAM_WD * w_ref[...])
        mo_ref[...] = mm
        vo_ref[...] = vv

    assert r % tr == 0
    row = pl.BlockSpec((tr, cols), lambda i: (i, 0))
    return pl.pallas_call(
        body, name=name, grid=(r // tr,),
        in_specs=[pl.BlockSpec((N_DEV, tr, cols), lambda i: (0, i, 0)), row, row, row],
        out_specs=[row, row, row, row],
        out_shape=[jax.ShapeDtypeStruct((r, cols), F32)] * 4,
        compiler_params=_cparams("parallel"),
    )(parts, w, m, v)


def _mesh_pos():
    return lax.axis_index("x"), lax.axis_index("y"), lax.axis_index("c")


def _peer(pos, k):
    x, y, c = pos
    return (x ^ ((k >> 2) & 1), y ^ ((k >> 1) & 1), c ^ (k & 1))


def _flat_id(pos):
    return 4 * pos[0] + 2 * pos[1] + pos[2]


def _exchange_copies(srcs, dsts, scatter, send_sems, recv_sems, loc_sems):
    pos = _mesh_pos()
    me = _flat_id(pos)
    local, remote = [], []
    for j, (src, dst) in enumerate(zip(srcs, dsts)):
        local.append(pltpu.make_async_copy(src.at[me] if scatter[j] else src, dst.at[me], loc_sems.at[j]))
        for k in range(1, N_DEV):
            to = _peer(pos, k)
            remote.append(pltpu.make_async_remote_copy(
                src_ref=src.at[_flat_id(to)] if scatter[j] else src, dst_ref=dst.at[me],
                send_sem=send_sems.at[j, k - 1], recv_sem=recv_sems.at[j, k - 1],
                device_id=to, device_id_type=pl.DeviceIdType.MESH))
    return local, remote


def _exchange_shapes(arrays, scatter):
    return [jax.ShapeDtypeStruct(a.shape if s else (N_DEV,) + a.shape, a.dtype) for a, s in zip(arrays, scatter)]


def _exchange_sems(n):
    return [pltpu.SemaphoreType.DMA((n, N_DEV - 1)), pltpu.SemaphoreType.DMA((n, N_DEV - 1)), pltpu.SemaphoreType.DMA((n,))]


def _exchange(arrays, scatter, name):
    n = len(arrays)
    any_spec = pl.BlockSpec(memory_space=pl.ANY)

    def body(*refs):
        local, remote = _exchange_copies(refs[:n], refs[n:2 * n], scatter, *refs[2 * n:])
        for cp in local + remote:
            cp.start()
        for cp in remote:
            cp.wait()
        for cp in local:
            cp.wait()

    return pl.pallas_call(
        body, name=name, in_specs=[any_spec] * n, out_specs=[any_spec] * n,
        out_shape=_exchange_shapes(arrays, scatter), scratch_shapes=_exchange_sems(n),
    )(*arrays)


def _pad_rows(a, rows):
    return jnp.pad(a, ((0, rows - a.shape[0]),) + ((0, 0),) * (a.ndim - 1))


PACK_UNIT = SUBLANES * LANES


def _pack_lanes(parts, rows):
    out = []
    for a in parts:
        f = a.reshape(-1)
        out.append(jnp.pad(f, (0, (-f.shape[0]) % PACK_UNIT)).reshape(-1, LANES))
    packed = jnp.concatenate(out, axis=0)
    assert packed.shape[0] == rows, (packed.shape, rows)
    return packed


def _unpack_lanes(buf, shapes):
    out, r0 = [], 0
    for shp in shapes:
        n = math.prod(shp)
        nr = -(-n // PACK_UNIT) * SUBLANES
        out.append(buf[r0:r0 + nr].reshape(-1)[:n].reshape(shp))
        r0 += nr
    return out


def _col_shards(g):
    r, n = g.shape
    return g.reshape(r, N_DEV, n // N_DEV).transpose(1, 0, 2)


def _col_unshard(s):
    _, r, w = s.shape
    return s.transpose(1, 0, 2).reshape(r, N_DEV * w)


def _lane_rows(flat):
    n = flat.shape[1]
    return jnp.pad(flat, ((0, 0), (0, (-n) % PACK_UNIT))).reshape(N_DEV, -1, LANES)


SMALL_ROWS = 128
WS_ROWS = 32


def kernel(x, norm_mix_w, w_in, conv_qkv_w, a_log, dt_bias, gdn_norm_w, w_branch_a, w_branch_b, rel_bias, w_out, norm_ffn_w, w_up, conv_ffn_w, conv_ffn_b, w_down, norm_final_w, loss_target, m_norm_mix_w, m_w_in, m_conv_qkv_w, m_a_log, m_dt_bias, m_gdn_norm_w, m_w_branch_a, m_w_branch_b, m_rel_bias, m_w_out, m_norm_ffn_w, m_w_up, m_conv_ffn_w, m_conv_ffn_b, m_w_down, m_norm_final_w, v_norm_mix_w, v_w_in, v_conv_qkv_w, v_a_log, v_dt_bias, v_gdn_norm_w, v_w_branch_a, v_w_branch_b, v_rel_bias, v_w_out, v_norm_ffn_w, v_w_up, v_conv_ffn_w, v_conv_ffn_b, v_w_down, v_norm_final_w):
    big_w = (w_in, w_branch_a, w_branch_b, w_out, w_up, w_down, conv_qkv_w, conv_ffn_w)
    big_m = (m_w_in, m_w_branch_a, m_w_branch_b, m_w_out, m_w_up, m_w_down, m_conv_qkv_w, m_conv_ffn_w)
    big_v = (v_w_in, v_w_branch_a, v_w_branch_b, v_w_out, v_w_up, v_w_down, v_conv_qkv_w, v_conv_ffn_w)
    small_w = (norm_mix_w, a_log, dt_bias, gdn_norm_w, rel_bias, norm_ffn_w, conv_ffn_b, norm_final_w)
    small_m = (m_norm_mix_w, m_a_log, m_dt_bias, m_gdn_norm_w, m_rel_bias, m_norm_ffn_w, m_conv_ffn_b, m_norm_final_w)
    small_v = (v_norm_mix_w, v_a_log, v_dt_bias, v_gdn_norm_w, v_rel_bias, v_norm_ffn_w, v_conv_ffn_b, v_norm_final_w)

    xs, tgt = x[0], loss_target[0]
    ws = _pack_lanes(big_w[6:], WS_ROWS)
    h1, g_in, gs = _rmsnorm_cast(xs, norm_mix_w, "norm_mix", carry=([w_in[0].astype(BF16), ws], (False, False)))
    win = _col_unshard(g_in)
    gs = gs.reshape(N_DEV, -1)
    cqkv = gs[:, :GDN_CONV * 192].reshape(N_DEV, GDN_CONV, 192).transpose(1, 0, 2).reshape(GDN_CONV, 3 * KEY_A)
    cffn = gs[:, PACK_UNIT:PACK_UNIT + FFN_CONV * 704].reshape(N_DEV, FFN_CONV, 704).transpose(1, 0, 2).reshape(FFN_CONV, FFN_W)
    cffn = _pad_rows(cffn, SUBLANES)
    w_all = jnp.concatenate([win[:, a:b] for a, b in W_IN_ORDER] + [jnp.zeros((D_MODEL, PROJ_W - D_IN), BF16)], axis=1)
    par = _pad_rows(jnp.pad(jnp.concatenate([a_log, dt_bias], axis=0), ((0, 0), (0, LANES - GDN_HEADS))), SUBLANES)
    table = jnp.pad(rel_bias[0], ((0, 0), (0, 3 * LANES - rel_bias.shape[-1]))).reshape(ATT_HEADS, 1, 3 * LANES)

    proj, g_ba, g_bb, g_out, g_up, g_down = _mm_nn(
        h1, w_all, F32, "in_proj", 2 * MM_TM, 1152, D_MODEL, carry=([w[0].astype(BF16) for w in big_w[1:6]], (False,) * 5))
    wba, wbb, wup = _col_unshard(g_ba), _col_unshard(g_bb), _col_unshard(g_up)
    wout = g_out.reshape(D_MODEL, D_MODEL)
    wdown = g_down.reshape(D_FF, D_MODEL)
    qn, kn, va = _gdn_prep_fwd(proj, cqkv)
    oan, o_gdn, sprev, wst, ust, tst = _gdn_fwd(qn, kn, va, proj, par, gdn_norm_w)
    bias_q, bias_k = _att_bias(table)
    ob, lse, lse_t = _att_fwd(proj, bias_q)
    x2, h2 = _merge_fwd(oan, ob, proj, xs, wba, wbb, wout, norm_ffn_w)
    up, dx3, dx3b, act, tail_sums = _ffn_fwd(h2, wup, cffn, conv_ffn_b, wdown, x2, tgt, norm_final_w.reshape(1, D_MODEL))

    g_wdown = _mm_tn(act, dx3b, "dw_down", 512)
    dup, ffn_sums, dx2, dx2b, nffn_sums, r_down = _ffn_bwd(
        dx3b, wdown.T, up, cffn, conv_ffn_b, wup.T, x2, norm_ffn_w, dx3,
        carry=([g_wdown.reshape(N_DEV, -1, D_MODEL).astype(BF16)], (True,)))
    g_wup = _mm_tn(h2, dup, "dw_up", 1408)
    dproj, d_oan, d_ob, g_wout, g_wba, g_wbb = _merge_bwd(dx2b, oan, ob, proj, wba, wbb, wout.T, wba.T, wbb.T)
    dproj, dlt_t, slabs = _att_dq(proj, bias_q, lse, d_ob, ob, dproj)
    dproj = _att_dkv(proj, bias_k, lse_t, dlt_t, d_ob, dproj)
    g_rel = _relbias_grad(slabs)[:, 0, :rel_bias.shape[-1]]
    dqn, dkn, dva, dproj, gdn_sums = _gdn_bwd(qn, kn, va, proj, par, gdn_norm_w, o_gdn, sprev, wst, ust, tst, d_oan, dproj)
    dproj, cq_sums = _gdn_prep_bwd(proj, cqkv, dqn, dkn, dva, dproj)
    g_wall, r_up = _mm_tn(h1, dproj, "dw_in", 1152, carry=([_col_shards(g_wup).astype(BF16)], (True,)))
    starts = np.cumsum([0] + [b - a for a, b in W_IN_ORDER])
    g_win = jnp.concatenate([g_wall[:, starts[i]:starts[i + 1]] for i in np.argsort([a for a, _ in W_IN_ORDER])], axis=1)
    g_conv = jnp.concatenate([_lane_rows(_col_shards(cq_sums[:GDN_CONV]).reshape(N_DEV, -1)),
                              _lane_rows(_col_shards(ffn_sums[:FFN_CONV]).reshape(N_DEV, -1))], axis=1)
    grad_x, _, nmix_sums, r_in, r_ba, r_bb, r_out, r_conv = _mm_rms_bwd(
        dproj, w_all.T, xs, norm_mix_w, dx2, "in_proj_bwd", MM_TM, 1152, carry=(
            [_col_shards(g_win).astype(BF16), _col_shards(g_wba).astype(BF16), _col_shards(g_wbb).astype(BF16),
             g_wout.reshape(N_DEV, -1, D_MODEL).astype(BF16), g_conv], (True,) * 5))

    small_g = (nmix_sums[0:1], gdn_sums[0:1, :GDN_HEADS], gdn_sums[1:2, :GDN_HEADS], gdn_sums[2:3], g_rel,
               nffn_sums[0:1], ffn_sums[FFN_CONV:FFN_CONV + 1], tail_sums[0:1], tail_sums[1:2, 0:1])
    r_small, = _exchange([_pack_lanes(small_g, SMALL_ROWS)], (False,), "all_gather_small_grads")
    recv = (r_in, r_ba, r_bb, r_out, r_up, r_down, r_conv, r_small)

    res = {}
    for i, (nm, tr) in enumerate((("w_in", 128), ("w_branch_a", KEY_A), ("w_branch_b", WIDTH_B), ("w_out", 128),
                                  ("w_up", 128), ("w_down", 176))):
        res[nm] = [o[None] for o in _adamw(recv[i], big_w[i][0], big_m[i][0], big_v[i][0], "adamw_" + nm, tr)]
    conv = _adamw(recv[6], _pack_lanes(big_w[6:], WS_ROWS), _pack_lanes(big_m[6:], WS_ROWS), _pack_lanes(big_v[6:], WS_ROWS),
                  "adamw_conv", WS_ROWS)
    conv = [_unpack_lanes(o, [w.shape for w in big_w[6:]]) for o in conv]
    res["conv_qkv_w"] = [o[0] for o in conv]
    res["conv_ffn_w"] = [o[1] for o in conv]
    small_shapes = [w.shape for w in small_w]
    zero = jnp.zeros((1,), F32)
    small = _adamw(recv[7], _pack_lanes(small_w + (zero,), SMALL_ROWS), _pack_lanes(small_m + (zero,), SMALL_ROWS),
                   _pack_lanes(small_v + (zero,), SMALL_ROWS), "adamw_replicated", SMALL_ROWS)
    small = [_unpack_lanes(o, small_shapes + [()]) for o in small]
    loss = small[0][-1]
    for j, nm in enumerate(("norm_mix_w", "a_log", "dt_bias", "gdn_norm_w", "rel_bias", "norm_ffn_w", "conv_ffn_b",
                            "norm_final_w")):
        res[nm] = [o[j] for o in small]

    names = ("norm_mix_w", "w_in", "conv_qkv_w", "a_log", "dt_bias", "gdn_norm_w", "w_branch_a", "w_branch_b", "rel_bias",
             "w_out", "norm_ffn_w", "w_up", "conv_ffn_w", "conv_ffn_b", "w_down", "norm_final_w")
    outs = [res[n][kind] for kind in range(4) for n in names]
    return (loss, grad_x[None], *outs)
```

```python
import functools
import math

import numpy as np
import jax
import jax.numpy as jnp
from jax import lax
from jax.experimental import pallas as pl
from jax.experimental.pallas import tpu as pltpu

F32, BF16 = jnp.float32, jnp.bfloat16
HIGHEST = lax.Precision.HIGHEST

N_DEV = 8
D_MODEL = 1024
CHUNK = 64
EPS = 1e-6
GDN_HEADS, GDN_DK = 4, 128
KEY_A = GDN_HEADS * GDN_DK
GDN_CONV = 4
ATT_HEADS, ATT_DH = 8, 64
WIDTH_B = ATT_HEADS * ATT_DH
ATT_BAND = 9
REL_CLIP = 128
D_FF = 2816
FFN_CONV = 3
D_IN = 5640
ADAM_LR, ADAM_B1, ADAM_B2, ADAM_EPS, ADAM_WD, ADAM_STEP = 0.001, 0.9, 0.999, 1e-08, 0.01, 10

LANES = 128
SUBLANES = 8
NEG = -1e30

PROJ_W = 5760
PB = 512
CB_GA, CB_GB = 0, 1
CB_KB, CB_VB, CB_QA, CB_KA, CB_VA, CB_QB, CB_ZA = 4, 5, 6, 7, 8, 9, 10
CB_BD = 44
DP_GATES, DP_KVB, DP_QKVA, DP_QB, DP_ZBD = (2048, 0), (1024, 2), (1536, 2), (512, 9), (640, 8)
W_IN_ORDER = ((3592, 5640), (2568, 3592), (0, 1536), (2056, 2568), (1536, 2048), (2048, 2056))

ATT_QB = 256
ATT_KW = 768
ATT_VEC = 1024


def _dot(a, b, precision=None):
    return jnp.dot(a, b, preferred_element_type=F32, precision=precision)


def _dot_nt(a, b, precision=None):
    return lax.dot_general(a, b, (((1,), (1,)), ((), ())), preferred_element_type=F32, precision=precision)


def _dot_tn(a, b):
    return lax.dot_general(a, b, (((0,), (0,)), ((), ())), preferred_element_type=F32)


def _split(a):
    hi = a.astype(BF16)
    return hi, (a - hi.astype(F32)).astype(BF16)


def _dot3s(a, b):
    return _dot(a[0], b[0]) + (_dot(a[0], b[1]) + _dot(a[1], b[0]))


def _sigmoid(x):
    return 0.5 * jnp.tanh(0.5 * x) + 0.5


def _softplus(x):
    return jnp.maximum(x, 0.0) + jnp.log(1.0 + jnp.exp(-jnp.abs(x)))


def _cparams(*sem):
    return pltpu.CompilerParams(dimension_semantics=tuple(sem))


def _dp_spec(tm, region, index=lambda i: i):
    width, cb = region
    return pl.BlockSpec((tm, width), lambda i: (index(i), cb))


def _rmsnorm_cast(x, w, name, tm=512, carry=((), ())):
    t, d = x.shape
    nt = t // tm
    arrays, scatter = carry
    nx = len(arrays)

    def body(*refs):
        x_ref, w_ref = refs[:2]
        o_ref = refs[2 + nx]
        i = pl.program_id(0)
        if nx:
            local, remote = _exchange_copies(refs[2:2 + nx], refs[3 + nx:3 + 2 * nx], scatter, *refs[3 + 2 * nx:])

            @pl.when(i == 0)
            def _():
                for cp in local + remote:
                    cp.start()

        xv = x_ref[...]
        r = lax.rsqrt(jnp.mean(xv * xv, axis=-1, keepdims=True) + EPS)
        o_ref[...] = (xv * r * w_ref[...]).astype(BF16)

        if nx:
            @pl.when(i == nt - 1)
            def _():
                for cp in remote + local:
                    cp.wait()

    any_spec = pl.BlockSpec(memory_space=pl.ANY)
    out = pl.pallas_call(
        body, name=name, grid=(nt,),
        in_specs=[pl.BlockSpec((tm, d), lambda i: (i, 0)), pl.BlockSpec((1, d), lambda i: (0, 0))] + [any_spec] * nx,
        out_specs=[pl.BlockSpec((tm, d), lambda i: (i, 0))] + [any_spec] * nx,
        out_shape=[jax.ShapeDtypeStruct((t, d), BF16)] + _exchange_shapes(arrays, scatter),
        scratch_shapes=_exchange_sems(nx) if nx else [],
        compiler_params=_cparams("arbitrary" if nx else "parallel"),
    )(x, w, *arrays)
    return out if nx else out[0]


def _mm_nn(a, b, out_dtype, name, tm, tn, tk, carry=((), ())):
    m, k = a.shape
    _, n = b.shape
    tm = min(tm, m)
    nk = k // tk
    assert m % tm == 0 and n % tn == 0 and k % tk == 0
    arrays, scatter = carry
    nx = len(arrays)
    gm, gn = m // tm, n // tn

    def body(*refs):
        a_ref, b_ref = refs[:2]
        srcs = refs[2:2 + nx]
        o_ref = refs[2 + nx]
        dsts = refs[3 + nx:3 + 2 * nx]
        rest = refs[3 + 2 * nx:]
        i, j, kk = pl.program_id(0), pl.program_id(1), pl.program_id(2)
        if nx:
            local, remote = _exchange_copies(srcs, dsts, scatter, *rest[-3:])

            @pl.when((i == 0) & (j == 0) & (kk == 0))
            def _():
                for cp in local + remote:
                    cp.start()

        if nk == 1:
            o_ref[...] = _dot(a_ref[...], b_ref[...]).astype(out_dtype)
        else:
            acc_ref = rest[0]

            @pl.when(kk == 0)
            def _():
                acc_ref[...] = jnp.zeros_like(acc_ref)

            acc_ref[...] += _dot(a_ref[...], b_ref[...])

            @pl.when(kk == nk - 1)
            def _():
                o_ref[...] = acc_ref[...].astype(out_dtype)

        if nx:
            @pl.when((i == gm - 1) & (j == gn - 1) & (kk == nk - 1))
            def _():
                for cp in remote + local:
                    cp.wait()

    any_spec = pl.BlockSpec(memory_space=pl.ANY)
    scratch = ([pltpu.VMEM((tm, tn), F32)] if nk > 1 else []) + (_exchange_sems(nx) if nx else [])
    out = pl.pallas_call(
        body, name=name, grid=(gm, gn, nk),
        in_specs=[pl.BlockSpec((tm, tk), lambda i, j, kk: (i, kk)),
                  pl.BlockSpec((tk, tn), lambda i, j, kk: (kk, j))] + [any_spec] * nx,
        out_specs=[pl.BlockSpec((tm, tn), lambda i, j, kk: (i, j))] + [any_spec] * nx,
        out_shape=[jax.ShapeDtypeStruct((m, n), out_dtype)] + _exchange_shapes(arrays, scatter),
        scratch_shapes=scratch,
        compiler_params=_cparams(*(("arbitrary",) * 3 if nx else ("parallel", "parallel", "arbitrary"))),
    )(a, b, *arrays)
    return out if nx else out[0]


def _mm_rms_bwd(a, b, x, w, dres, name, tm, tk, carry):
    m, k = a.shape
    _, n = b.shape
    nk = k // tk
    gm = m // tm
    assert m % tm == 0 and k % tk == 0
    arrays, scatter = carry
    nx = len(arrays)

    def body(*refs):
        a_ref, b_ref, x_ref, w_ref, dres_ref = refs[:5]
        srcs = refs[5:5 + nx]
        dx_ref, dxb_ref, dw_ref = refs[5 + nx:8 + nx]
        dsts = refs[8 + nx:8 + 2 * nx]
        acc_ref = refs[8 + 2 * nx]
        i, kk = pl.program_id(0), pl.program_id(1)
        local, remote = _exchange_copies(srcs, dsts, scatter, *refs[9 + 2 * nx:])

        @pl.when((i == 0) & (kk == 0))
        def _():
            for cp in local + remote:
                cp.start()
            dw_ref[...] = jnp.zeros_like(dw_ref)

        @pl.when(kk == 0)
        def _():
            acc_ref[...] = jnp.zeros_like(acc_ref)

        acc_ref[...] += _dot(a_ref[...], b_ref[...])

        @pl.when(kk == nk - 1)
        def _():
            dhv = acc_ref[...]
            xv = x_ref[...]
            r = lax.rsqrt(jnp.mean(xv * xv, axis=-1, keepdims=True) + EPS)
            xh = xv * r
            dw_ref[0:1, :] += jnp.sum(dhv * xh, axis=0, keepdims=True)
            dxh = dhv * w_ref[...]
            dx = dres_ref[...] + r * (dxh - xh * jnp.mean(dxh * xh, axis=-1, keepdims=True))
            dx_ref[...] = dx
            dxb_ref[...] = dx.astype(BF16)

        @pl.when((i == gm - 1) & (kk == nk - 1))
        def _():
            for cp in remote + local:
                cp.wait()

    any_spec = pl.BlockSpec(memory_space=pl.ANY)
    row = pl.BlockSpec((tm, n), lambda i, kk: (i, 0))
    return pl.pallas_call(
        body, name=name, grid=(gm, nk),
        in_specs=[pl.BlockSpec((tm, tk), lambda i, kk: (i, kk)), pl.BlockSpec((tk, n), lambda i, kk: (kk, 0)),
                  row, pl.BlockSpec((1, n), lambda i, kk: (0, 0)), row] + [any_spec] * nx,
        out_specs=[row, row, pl.BlockSpec((SUBLANES, n), lambda i, kk: (0, 0))] + [any_spec] * nx,
        out_shape=[jax.ShapeDtypeStruct((m, n), F32), jax.ShapeDtypeStruct((m, n), BF16),
                   jax.ShapeDtypeStruct((SUBLANES, n), F32)] + _exchange_shapes(arrays, scatter),
        scratch_shapes=[pltpu.VMEM((tm, n), F32)] + _exchange_sems(nx),
        compiler_params=_cparams("arbitrary", "arbitrary"),
    )(a, b, x, w, dres, *arrays)


MM_TM = 1024


def _mm_tn(a, b, name, tn, tk=2 * MM_TM, carry=((), ())):
    t, m = a.shape
    _, n = b.shape
    tk = min(tk, t)
    assert t % tk == 0 and n % tn == 0
    gn, gs = n // tn, t // tk
    arrays, scatter = carry
    nx = len(arrays)

    def body(*refs):
        a_ref, b_ref = refs[:2]
        o_ref = refs[2 + nx]
        j, s = pl.program_id(0), pl.program_id(1)
        if nx:
            local, remote = _exchange_copies(refs[2:2 + nx], refs[3 + nx:3 + 2 * nx], scatter, *refs[3 + 2 * nx:])

            @pl.when((j == 0) & (s == 0))
            def _():
                for cp in local + remote:
                    cp.start()

        @pl.when(s == 0)
        def _():
            o_ref[...] = jnp.zeros_like(o_ref)

        o_ref[...] += _dot_tn(a_ref[...], b_ref[...])

        if nx:
            @pl.when((j == gn - 1) & (s == gs - 1))
            def _():
                for cp in remote + local:
                    cp.wait()

    any_spec = pl.BlockSpec(memory_space=pl.ANY)
    out = pl.pallas_call(
        body, name=name, grid=(gn, gs),
        in_specs=[pl.BlockSpec((tk, m), lambda j, s: (s, 0)),
                  pl.BlockSpec((tk, tn), lambda j, s: (s, j))] + [any_spec] * nx,
        out_specs=[pl.BlockSpec((m, tn), lambda j, s: (0, j))] + [any_spec] * nx,
        out_shape=[jax.ShapeDtypeStruct((m, n), F32)] + _exchange_shapes(arrays, scatter),
        scratch_shapes=_exchange_sems(nx) if nx else [],
        compiler_params=_cparams(*(("arbitrary", "arbitrary") if nx else ("parallel", "arbitrary"))),
    )(a, b, *arrays)
    return out if nx else out[0]


def _rel_index(dist):
    return np.clip(dist, -REL_CLIP, REL_CLIP) + REL_CLIP


def _bias_onehots():
    tw = 3 * LANES
    m = np.arange(ATT_VEC)
    dq = np.where(m <= ATT_KW, 512 - m, 512 - (m - ATT_VEC))
    dk = np.where(m < ATT_KW, m, m - ATT_VEC)
    ohq = np.zeros((tw, ATT_VEC), np.float32)
    ohk = np.zeros((tw, ATT_VEC), np.float32)
    ohq[_rel_index(dq), m] = 1.0
    ohk[_rel_index(dk), m] = 1.0
    return ohq, ohk


def _att_bias(table_pad):
    ohq, ohk = _bias_onehots()
    nslab = ATT_QB // SUBLANES

    def body(t_ref, ohq_ref, ohk_ref, bq_ref, bk_ref):
        tv = jnp.broadcast_to(t_ref[...], (SUBLANES, 3 * LANES))
        lane = lax.broadcasted_iota(jnp.int32, (ATT_QB, ATT_KW), 1)
        row = lax.broadcasted_iota(jnp.int32, (ATT_QB, ATT_KW), 0) // CHUNK
        col = lane // CHUNK
        band = (col >= row) & (col <= row + ATT_BAND - 1)
        for which, (oh_ref, out_ref) in enumerate(((ohq_ref, bq_ref), (ohk_ref, bk_ref))):
            vec = _dot(tv, oh_ref[...], HIGHEST)[0:1, :]
            slab = jnp.concatenate([vec if b == 0 else pltpu.roll(vec, b, 1) for b in range(SUBLANES)], axis=0)
            rows = [slab if a == 0 else pltpu.roll(slab, SUBLANES * a, 1) for a in range(nslab)]
            full = jnp.concatenate(rows, axis=0)[:, :ATT_KW]
            for v in range(3):
                inside = (lane >= (2 - v) * ATT_QB) if which == 0 else (lane < (v + 1) * ATT_QB)
                out_ref[v] = jnp.where(band & inside, full, NEG)

    h = table_pad.shape[0]
    oh_spec = pl.BlockSpec((3 * LANES, ATT_VEC), lambda i: (0, 0))
    out_spec = pl.BlockSpec((3, None, ATT_QB, ATT_KW), lambda i: (0, i, 0, 0))
    return pl.pallas_call(
        body, name="att_bias", grid=(h,),
        in_specs=[pl.BlockSpec((None, 1, 3 * LANES), lambda i: (i, 0, 0)), oh_spec, oh_spec],
        out_specs=[out_spec, out_spec],
        out_shape=[jax.ShapeDtypeStruct((3, h, ATT_QB, ATT_KW), F32)] * 2,
        compiler_params=_cparams("parallel"),
    )(table_pad, jnp.asarray(ohq), jnp.asarray(ohk))


def _head_masks():
    lane = lax.broadcasted_iota(jnp.int32, (1, LANES), 1)
    return [lane < ATT_DH, lane >= ATT_DH]


def _att_fwd(proj, bias_q):
    t = proj.shape[0]
    nb = t // ATT_QB
    scale = ATT_DH ** -0.5

    def body(q_ref, k0_ref, k1_ref, k2_ref, v0_ref, v1_ref, v2_ref, b_ref, o_ref, lse_ref, lset_ref):
        i = pl.program_id(0)
        q = (q_ref[...] * scale).astype(BF16)
        kk = jnp.concatenate([k0_ref[...], k1_ref[...], k2_ref[...]], axis=0).astype(BF16)
        vv = jnp.concatenate([v0_ref[...], v1_ref[...], v2_ref[...]], axis=0).astype(BF16)
        lane = lax.broadcasted_iota(jnp.int32, (1, LANES), 1)
        masks = _head_masks()
        lse_cols = jnp.zeros((ATT_QB, LANES), F32)
        for p in range(ATT_HEADS // 2):
            cs = slice(p * LANES, (p + 1) * LANES)
            qt, kt, vt = q[:, cs], kk[:, cs], vv[:, cs]
            acc = jnp.zeros((ATT_QB, LANES), F32)
            for sub in range(2):
                h = 2 * p + sub
                s = _dot_nt(jnp.where(masks[sub], qt, 0), kt) + b_ref[h]
                mx = jnp.max(s, axis=-1, keepdims=True)
                e = jnp.exp(s - mx)
                l = jnp.sum(e, axis=-1, keepdims=True)
                acc = acc + _dot(e.astype(BF16), jnp.where(masks[sub], vt, 0)) * (1.0 / l)
                lse_cols = lse_cols + jnp.where(lane == h, mx + jnp.log(l), 0.0)
            o_ref[:, cs] = acc.astype(BF16)
        lse_ref[...] = lse_cols
        lset_ref[...] = lse_cols.T[0:SUBLANES, :]

    def kv_spec(off, cb):
        return pl.BlockSpec((ATT_QB, PB), lambda i: (jnp.maximum(i + off, 0), cb))

    return pl.pallas_call(
        body, name="att_fwd", grid=(nb,),
        in_specs=[pl.BlockSpec((ATT_QB, PB), lambda i: (i, CB_QB)),
                  kv_spec(-2, CB_KB), kv_spec(-1, CB_KB), kv_spec(0, CB_KB),
                  kv_spec(-2, CB_VB), kv_spec(-1, CB_VB), kv_spec(0, CB_VB),
                  pl.BlockSpec((None, ATT_HEADS, ATT_QB, ATT_KW), lambda i: (jnp.minimum(i, 2), 0, 0, 0))],
        out_specs=[pl.BlockSpec((ATT_QB, WIDTH_B), lambda i: (i, 0)),
                   pl.BlockSpec((ATT_QB, LANES), lambda i: (i, 0)),
                   pl.BlockSpec((SUBLANES, ATT_QB), lambda i: (0, i))],
        out_shape=[jax.ShapeDtypeStruct((t, WIDTH_B), BF16), jax.ShapeDtypeStruct((t, LANES), F32),
                   jax.ShapeDtypeStruct((SUBLANES, t), F32)],
        compiler_params=_cparams("parallel"),
    )(proj, proj, proj, proj, proj, proj, proj, bias_q)


def _att_dq(proj, bias_q, lse, d_ob, ob, dproj):
    t = proj.shape[0]
    nb = t // ATT_QB
    scale = ATT_DH ** -0.5
    nslab = ATT_QB // SUBLANES

    def body(q_ref, k0_ref, k1_ref, k2_ref, v0_ref, v1_ref, v2_ref, b_ref, lse_ref, do_ref, o_ref, dp_in_ref,
             dq_ref, dlt_ref, slab_ref):
        i = pl.program_id(0)

        @pl.when(i == 0)
        def _():
            slab_ref[...] = jnp.zeros_like(slab_ref)

        q = (q_ref[...] * scale).astype(BF16)
        kk = jnp.concatenate([k0_ref[...], k1_ref[...], k2_ref[...]], axis=0).astype(BF16)
        vv = jnp.concatenate([v0_ref[...], v1_ref[...], v2_ref[...]], axis=0).astype(BF16)
        do = do_ref[...].astype(BF16)
        do_o = do_ref[...] * o_ref[...].astype(F32)
        lane = lax.broadcasted_iota(jnp.int32, (1, LANES), 1)
        masks = _head_masks()
        lse_all = lse_ref[...]
        dlt_cols = jnp.zeros((ATT_QB, LANES), F32)
        zpad = jnp.zeros((SUBLANES, ATT_VEC - ATT_KW), F32)
        for p in range(ATT_HEADS // 2):
            cs = slice(p * LANES, (p + 1) * LANES)
            qt, kt, vt, dot_ = q[:, cs], kk[:, cs], vv[:, cs], do[:, cs]
            acc = jnp.zeros((ATT_QB, LANES), F32)
            for sub in range(2):
                h = 2 * p + sub
                s = _dot_nt(jnp.where(masks[sub], qt, 0), kt) + b_ref[h]
                pr = jnp.exp(s - lse_all[:, h:h + 1])
                dp = _dot_nt(jnp.where(masks[sub], dot_, 0), vt)
                dl = jnp.sum(jnp.where(masks[sub], do_o[:, cs], 0.0), axis=-1, keepdims=True)
                ds = pr * (dp - dl)
                acc = acc + _dot(ds.astype(BF16), jnp.where(masks[sub], kt, 0)) * scale
                dlt_cols = dlt_cols + jnp.where(lane == h, dl, 0.0)
                sl = jnp.zeros((SUBLANES, ATT_VEC), F32)
                for a in range(nslab):
                    piece = jnp.concatenate([ds[a * SUBLANES:(a + 1) * SUBLANES, :], zpad], axis=1)
                    sl = sl + (piece if a == 0 else pltpu.roll(piece, ATT_VEC - SUBLANES * a, 1))
                slab_ref[h] += sl
            dq_ref[:, cs] = acc.astype(BF16)
        dlt_ref[...] = dlt_cols.T[0:SUBLANES, :]

    def kv_spec(off, cb):
        return pl.BlockSpec((ATT_QB, PB), lambda i: (jnp.maximum(i + off, 0), cb))

    return pl.pallas_call(
        body, name="att_dq", grid=(nb,),
        in_specs=[pl.BlockSpec((ATT_QB, PB), lambda i: (i, CB_QB)),
                  kv_spec(-2, CB_KB), kv_spec(-1, CB_KB), kv_spec(0, CB_KB),
                  kv_spec(-2, CB_VB), kv_spec(-1, CB_VB), kv_spec(0, CB_VB),
                  pl.BlockSpec((None, ATT_HEADS, ATT_QB, ATT_KW), lambda i: (jnp.minimum(i, 2), 0, 0, 0)),
                  pl.BlockSpec((ATT_QB, LANES), lambda i: (i, 0)),
                  pl.BlockSpec((ATT_QB, WIDTH_B), lambda i: (i, 0)), pl.BlockSpec((ATT_QB, WIDTH_B), lambda i: (i, 0)),
                  pl.BlockSpec(memory_space=pl.ANY)],
        out_specs=[_dp_spec(ATT_QB, DP_QB),
                   pl.BlockSpec((SUBLANES, ATT_QB), lambda i: (0, i)),
                   pl.BlockSpec((ATT_HEADS, SUBLANES, ATT_VEC), lambda i: (0, 0, 0))],
        out_shape=[jax.ShapeDtypeStruct(dproj.shape, dproj.dtype), jax.ShapeDtypeStruct((SUBLANES, t), F32),
                   jax.ShapeDtypeStruct((ATT_HEADS, SUBLANES, ATT_VEC), F32)],
        input_output_aliases={11: 0},
        compiler_params=_cparams("arbitrary"),
    )(proj, proj, proj, proj, proj, proj, proj, bias_q, lse, d_ob, ob, dproj)


def _att_dkv(proj, bias_k, lse_t, dlt_t, d_ob, dproj):
    t = proj.shape[0]
    nb = t // ATT_QB
    scale = ATT_DH ** -0.5

    def body(k_ref, v_ref, q0_ref, q1_ref, q2_ref, d0_ref, d1_ref, d2_ref, l0_ref, l1_ref, l2_ref,
             e0_ref, e1_ref, e2_ref, b_ref, dp_in_ref, dkv_ref):
        i = pl.program_id(0)
        k = k_ref[...].astype(BF16)
        v = v_ref[...].astype(BF16)
        qq = (jnp.concatenate([q0_ref[...], q1_ref[...], q2_ref[...]], axis=0) * scale).astype(BF16)
        do = jnp.concatenate([d0_ref[...], d1_ref[...], d2_ref[...]], axis=0).astype(BF16)
        lse = jnp.concatenate([l0_ref[...], l1_ref[...], l2_ref[...]], axis=1)
        dlt = jnp.concatenate([e0_ref[...], e1_ref[...], e2_ref[...]], axis=1)
        masks = _head_masks()
        for p in range(ATT_HEADS // 2):
            cs = slice(p * LANES, (p + 1) * LANES)
            kt, vt, qt, dot_ = k[:, cs], v[:, cs], qq[:, cs], do[:, cs]
            acc_k = jnp.zeros((ATT_QB, LANES), F32)
            acc_v = jnp.zeros((ATT_QB, LANES), F32)
            for sub in range(2):
                h = 2 * p + sub
                st = _dot_nt(jnp.where(masks[sub], kt, 0), qt) + b_ref[h]
                pt = jnp.exp(st - lse[h:h + 1, :])
                dot_m = jnp.where(masks[sub], dot_, 0)
                acc_v = acc_v + _dot(pt.astype(BF16), dot_m)
                dpt = _dot_nt(jnp.where(masks[sub], vt, 0), dot_)
                dst = pt * (dpt - dlt[h:h + 1, :])
                acc_k = acc_k + _dot(dst.astype(BF16), jnp.where(masks[sub], qt, 0))
            dkv_ref[:, cs] = acc_k.astype(BF16)
            dkv_ref[:, WIDTH_B + p * LANES:WIDTH_B + (p + 1) * LANES] = acc_v.astype(BF16)

    def q_spec(off, cb):
        return pl.BlockSpec((ATT_QB, PB), lambda i: (jnp.minimum(i + off, nb - 1), cb))

    def d_spec(off):
        return pl.BlockSpec((ATT_QB, WIDTH_B), lambda i: (jnp.minimum(i + off, nb - 1), 0))

    def r_spec(off):
        return pl.BlockSpec((SUBLANES, ATT_QB), lambda i: (0, jnp.minimum(i + off, nb - 1)))

    row = pl.BlockSpec((ATT_QB, WIDTH_B), lambda i: (i, 0))
    return pl.pallas_call(
        body, name="att_dkv", grid=(nb,),
        in_specs=[pl.BlockSpec((ATT_QB, PB), lambda i: (i, CB_KB)), pl.BlockSpec((ATT_QB, PB), lambda i: (i, CB_VB)),
                  q_spec(0, CB_QB), q_spec(1, CB_QB), q_spec(2, CB_QB),
                  d_spec(0), d_spec(1), d_spec(2), r_spec(0), r_spec(1), r_spec(2),
                  r_spec(0), r_spec(1), r_spec(2),
                  pl.BlockSpec((None, ATT_HEADS, ATT_QB, ATT_KW), lambda i: (jnp.minimum(nb - 1 - i, 2), 0, 0, 0)),
                  pl.BlockSpec(memory_space=pl.ANY)],
        out_specs=_dp_spec(ATT_QB, DP_KVB),
        out_shape=jax.ShapeDtypeStruct(dproj.shape, dproj.dtype),
        input_output_aliases={15: 0},
        compiler_params=_cparams("parallel"),
    )(proj, proj, proj, proj, proj, d_ob, d_ob, d_ob, lse_t, lse_t, lse_t, dlt_t, dlt_t, dlt_t, bias_k, dproj)


def _relbias_grad(slabs):
    ohq, _ = _bias_onehots()

    def body(s_ref, oh_ref, o_ref):
        sv = s_ref[...]
        vec = sv[0:1, :]
        for b in range(1, SUBLANES):
            vec = vec + pltpu.roll(sv[b:b + 1, :], ATT_VEC - b, 1)
        o_ref[...] = _dot_nt(jnp.broadcast_to(vec, (SUBLANES, ATT_VEC)), oh_ref[...], HIGHEST)[0:1, :]

    h = slabs.shape[0]
    return pl.pallas_call(
        body, name="att_dbias", grid=(h,),
        in_specs=[pl.BlockSpec((None, SUBLANES, ATT_VEC), lambda i: (i, 0, 0)),
                  pl.BlockSpec((3 * LANES, ATT_VEC), lambda i: (0, 0))],
        out_specs=pl.BlockSpec((None, 1, 3 * LANES), lambda i: (i, 0, 0)),
        out_shape=jax.ShapeDtypeStruct((h, 1, 3 * LANES), F32),
        compiler_params=_cparams("parallel"),
    )(slabs, jnp.asarray(ohq))


GDN_TM = 512
GDN_CB = 8
HALO = SUBLANES


def _conv_taps(ext, width, lead, n):
    return [(ext if k == width - 1 else pltpu.roll(ext, width - 1 - k, 0))[lead:lead + n] for k in range(width)]


def _prev_halo_spec(tm, width, cb):
    return pl.BlockSpec((HALO, width), lambda i: (jnp.maximum(i * (tm // HALO) - 1, 0), cb))


def _next_halo_spec(tm, width, cb, t):
    return pl.BlockSpec((HALO, width), lambda i: (jnp.minimum((i + 1) * (tm // HALO), t // HALO - 1), cb))


def _gdn_prep_fwd(proj, conv_w):
    t = proj.shape[0]
    tm = GDN_TM

    def body(q_ref, k_ref, v_ref, hq_ref, hk_ref, hv_ref, w_ref, qn_ref, kn_ref, vo_ref):
        first = pl.program_id(0) == 0
        for idx, (x_ref, h_ref, o_ref) in enumerate(((q_ref, hq_ref, qn_ref), (k_ref, hk_ref, kn_ref),
                                                      (v_ref, hv_ref, vo_ref))):
            halo = jnp.where(first, 0.0, h_ref[...])
            ext = jnp.concatenate([halo, x_ref[...]], axis=0)
            w = w_ref[:, idx * KEY_A:(idx + 1) * KEY_A]
            taps = _conv_taps(ext, GDN_CONV, HALO, tm)
            y = sum(w[k:k + 1, :] * taps[k] for k in range(GDN_CONV))
            a = y * _sigmoid(y)
            if idx < 2:
                for h in range(GDN_HEADS):
                    cs = slice(h * GDN_DK, (h + 1) * GDN_DK)
                    seg = a[:, cs]
                    o_ref[:, cs] = seg * lax.rsqrt(jnp.sum(seg * seg, axis=-1, keepdims=True) + EPS)
            else:
                o_ref[...] = a

    row = pl.BlockSpec((tm, KEY_A), lambda i: (i, 0))
    return pl.pallas_call(
        body, name="gdn_prep_fwd", grid=(t // tm,),
        in_specs=[pl.BlockSpec((tm, PB), lambda i: (i, CB_QA)), pl.BlockSpec((tm, PB), lambda i: (i, CB_KA)),
                  pl.BlockSpec((tm, PB), lambda i: (i, CB_VA)),
                  _prev_halo_spec(tm, PB, CB_QA), _prev_halo_spec(tm, PB, CB_KA), _prev_halo_spec(tm, PB, CB_VA),
                  pl.BlockSpec((GDN_CONV, 3 * KEY_A), lambda i: (0, 0))],
        out_specs=[row, row, row],
        out_shape=[jax.ShapeDtypeStruct((t, KEY_A), F32)] * 3,
        compiler_params=_cparams("parallel"),
    )(proj, proj, proj, proj, proj, proj, conv_w)


def _gdn_prep_bwd(proj, conv_w, dqn, dkn, dv, dproj):
    t = proj.shape[0]
    tm = GDN_TM
    nt = t // tm
    n_ext = tm + HALO

    def body(q_ref, k_ref, v_ref, pq_ref, pk_ref, pv_ref, nq_ref, nk_ref, nv_ref,
             dq_ref, dk_ref, dv_ref, ndq_ref, ndk_ref, ndv_ref, w_ref, dp_in_ref, out_ref, dw_ref):
        i = pl.program_id(0)
        first, last = i == 0, i == nt - 1

        @pl.when(first)
        def _():
            dw_ref[...] = jnp.zeros_like(dw_ref)

        groups = ((q_ref, pq_ref, nq_ref, dq_ref, ndq_ref), (k_ref, pk_ref, nk_ref, dk_ref, ndk_ref),
                  (v_ref, pv_ref, nv_ref, dv_ref, ndv_ref))
        for idx, (x_ref, p_ref, n_ref, d_ref, nd_ref) in enumerate(groups):
            cs_all = slice(idx * KEY_A, (idx + 1) * KEY_A)
            ext = jnp.concatenate([jnp.where(first, 0.0, p_ref[...]), x_ref[...], jnp.where(last, 0.0, n_ref[...])], axis=0)
            w = w_ref[:, cs_all]
            taps = _conv_taps(ext, GDN_CONV, HALO, n_ext)
            y = sum(w[k:k + 1, :] * taps[k] for k in range(GDN_CONV))
            sg = _sigmoid(y)
            a = y * sg
            dup = jnp.concatenate([d_ref[...], jnp.where(last, 0.0, nd_ref[...])], axis=0)
            if idx < 2:
                segs = []
                for h in range(GDN_HEADS):
                    cs = slice(h * GDN_DK, (h + 1) * GDN_DK)
                    seg = a[:, cs]
                    r = lax.rsqrt(jnp.sum(seg * seg, axis=-1, keepdims=True) + EPS)
                    nrm = seg * r
                    dn = dup[:, cs]
                    segs.append(r * (dn - nrm * jnp.sum(dn * nrm, axis=-1, keepdims=True)))
                da = jnp.concatenate(segs, axis=1)
            else:
                da = dup
            dy = da * sg * (1.0 + y * (1.0 - sg))
            dx = sum(w[k:k + 1, :] * (dy if k == GDN_CONV - 1 else pltpu.roll(dy, n_ext - (GDN_CONV - 1 - k), 0))[:tm]
                     for k in range(GDN_CONV))
            out_ref[:, cs_all] = dx.astype(BF16)
            for k in range(GDN_CONV):
                dw_ref[k:k + 1, cs_all] += jnp.sum(dy[:tm] * taps[k][:tm], axis=0, keepdims=True)

    row = pl.BlockSpec((tm, KEY_A), lambda i: (i, 0))
    nrow = _next_halo_spec(tm, KEY_A, 0, t)
    return pl.pallas_call(
        body, name="gdn_prep_bwd", grid=(nt,),
        in_specs=[pl.BlockSpec((tm, PB), lambda i: (i, CB_QA)), pl.BlockSpec((tm, PB), lambda i: (i, CB_KA)),
                  pl.BlockSpec((tm, PB), lambda i: (i, CB_VA)),
                  _prev_halo_spec(tm, PB, CB_QA), _prev_halo_spec(tm, PB, CB_KA), _prev_halo_spec(tm, PB, CB_VA),
                  _next_halo_spec(tm, PB, CB_QA, t), _next_halo_spec(tm, PB, CB_KA, t), _next_halo_spec(tm, PB, CB_VA, t),
                  row, row, row, nrow, nrow, nrow,
                  pl.BlockSpec((GDN_CONV, 3 * KEY_A), lambda i: (0, 0)), pl.BlockSpec(memory_space=pl.ANY)],
        out_specs=[_dp_spec(tm, DP_QKVA), pl.BlockSpec((SUBLANES, 3 * KEY_A), lambda i: (0, 0))],
        out_shape=[jax.ShapeDtypeStruct(dproj.shape, dproj.dtype), jax.ShapeDtypeStruct((SUBLANES, 3 * KEY_A), F32)],
        input_output_aliases={16: 0},
        compiler_params=_cparams("arbitrary"),
    )(proj, proj, proj, proj, proj, proj, proj, proj, proj, dqn, dkn, dv, dqn, dkn, dv, conv_w, dproj)


class _Pair(dict):
    __getattr__ = dict.__getitem__
    __setattr__ = dict.__setitem__


def _pairs_to_lanes(cols):
    lane = lax.broadcasted_iota(jnp.int32, (1, LANES), 1)
    out = jnp.zeros((cols[0].shape[0], LANES), F32)
    for p, col in enumerate(cols):
        out = out + jnp.where(lane == p, col, 0.0)
    return out


def _gdn_terms(bd, par, kn_ref, qn_ref):
    c = CHUNK
    ii = lax.broadcasted_iota(jnp.int32, (c, c), 0)
    jj = lax.broadcasted_iota(jnp.int32, (c, c), 1)
    strict, incl = ii > jj, ii >= jj
    ltri = incl.astype(F32)
    ts = []
    for cc in range(GDN_CB):
        for h in range(GDN_HEADS):
            t = _Pair(cc=cc, h=h, rows=slice(cc * c, (cc + 1) * c), cs=slice(h * GDN_DK, (h + 1) * GDN_DK),
                      strict=strict, incl=incl)
            t.beta = _sigmoid(bd[t.rows, h:h + 1])
            t.ea = jnp.exp(par[0:1, h:h + 1])
            t.sp_arg = bd[t.rows, GDN_HEADS + h:GDN_HEADS + h + 1] + par[1:2, h:h + 1]
            t.g = -t.ea * _softplus(t.sp_arg)
            t.k = kn_ref[t.rows, t.cs]
            t.q = qn_ref[t.rows, t.cs] * (GDN_DK ** -0.5)
            t.kb, t.qb = t.k.astype(BF16), t.q.astype(BF16)
            ts.append(t)
    gall = _dot(ltri, _pairs_to_lanes([t.g for t in ts]), HIGHEST)
    gall_t = gall.T
    for p, t in enumerate(ts):
        t.gb = jnp.broadcast_to(gall[:, p:p + 1], (c, GDN_DK))
    for t in ts:
        t.kk = _dot_nt(t.kb, t.kb)
        t.qk = _dot_nt(t.qb, t.kb)
    for p, t in enumerate(ts):
        diff = t.gb[:, :c] - gall_t[p:p + 1, :]
        t.dec_s = jnp.exp(jnp.where(strict, diff, NEG))
        t.dec_i = jnp.exp(jnp.where(incl, diff, NEG))
        t.gam = jnp.exp(t.gb)
        glast = t.gb[c - 1:c, :]
        t.e_rest = jnp.exp(glast - t.gb)
        t.gl = jnp.exp(glast)
        t.p = t.qk * t.dec_i
    return ts


def _gdn_fwd(qn, kn, v, proj, par, gnw):
    t = qn.shape[0]
    c = CHUNK
    nc = t // c
    r_ = GDN_CB * c

    def body(qn_ref, kn_ref, v_ref, bd_ref, z_ref, par_ref, gnw_ref,
             oan_ref, o_ref, sp_ref, w_ref, u_ref, tm_ref, s_ref):
        @pl.when(pl.program_id(0) == 0)
        def _():
            s_ref[...] = jnp.zeros_like(s_ref)

        bd, par, gnw_v = bd_ref[...], par_ref[...], gnw_ref[...]
        eye = (lax.broadcasted_iota(jnp.int32, (c, c), 0) == lax.broadcasted_iota(jnp.int32, (c, c), 1)).astype(F32)
        ts = _gdn_terms(bd, par, kn_ref, qn_ref)
        for t in ts:
            t.vv = v_ref[t.rows, t.cs]
            t.x = -(t.beta * t.kk * t.dec_s)
            t.tinv = eye + t.x
        for t in ts:
            t.xs = _split(t.x)
        for _ in range(5):
            for t in ts:
                t.xs = _split(_dot3s(t.xs, t.xs))
            for t in ts:
                t.tinv = t.tinv + _dot3s(_split(t.tinv), t.xs)
        for t in ts:
            tsp = _split(t.tinv)
            t.wm = _dot3s(tsp, _split((t.beta * t.gam) * t.k))
            t.uv = _dot3s(tsp, _split(t.beta * t.vv))
        for t in ts:
            w_ref[t.rows, t.cs] = t.wm
            tm_ref[t.cc, t.h] = t.tinv.T
            t.wb = t.wm.astype(BF16)
            t.qgb = (t.q * t.gam).astype(BF16)
            t.kdb = (t.k * t.e_rest).astype(BF16)
            t.pb = t.p.astype(BF16)
        state = [s_ref[h] for h in range(GDN_HEADS)]
        for cc in range(GDN_CB):
            tc = [t for t in ts if t.cc == cc]
            for t in tc:
                t.sh = state[t.h]
                t.sb = t.sh.astype(BF16)
            for t in tc:
                t.ws = _dot(t.wb, t.sb)
            for t in tc:
                t.u = t.uv - t.ws
                t.ub = t.u.astype(BF16)
            for t in tc:
                state[t.h] = t.gl * t.sh + _dot_tn(t.kdb, t.ub)
            for t in tc:
                t.o = _dot(t.qgb, t.sb) + _dot(t.pb, t.ub)
                sp_ref[cc, t.h] = t.sh
                u_ref[t.rows, t.cs] = t.u
                o_ref[t.rows, t.cs] = t.o
        for h in range(GDN_HEADS):
            s_ref[h] = state[h]
        for t in ts:
            zz = z_ref[t.rows, t.cs]
            rr = lax.rsqrt(jnp.mean(t.o * t.o, axis=-1, keepdims=True) + EPS)
            oan_ref[t.rows, t.cs] = ((t.o * rr) * gnw_v * (zz * _sigmoid(zz))).astype(BF16)

    row = pl.BlockSpec((r_, KEY_A), lambda i: (i, 0))
    return pl.pallas_call(
        body, name="gdn_fwd", grid=(nc // GDN_CB,),
        in_specs=[row, row, row, pl.BlockSpec((r_, LANES), lambda i: (i, CB_BD)),
                  pl.BlockSpec((r_, PB), lambda i: (i, CB_ZA)),
                  pl.BlockSpec((SUBLANES, LANES), lambda i: (0, 0)), pl.BlockSpec((1, GDN_DK), lambda i: (0, 0))],
        out_specs=[row, row, pl.BlockSpec((GDN_CB, GDN_HEADS, GDN_DK, GDN_DK), lambda i: (i, 0, 0, 0)),
                   row, row, pl.BlockSpec((GDN_CB, GDN_HEADS, c, c), lambda i: (i, 0, 0, 0))],
        out_shape=[jax.ShapeDtypeStruct((t, KEY_A), BF16), jax.ShapeDtypeStruct((t, KEY_A), F32),
                   jax.ShapeDtypeStruct((nc, GDN_HEADS, GDN_DK, GDN_DK), F32),
                   jax.ShapeDtypeStruct((t, KEY_A), F32), jax.ShapeDtypeStruct((t, KEY_A), F32),
                   jax.ShapeDtypeStruct((nc, GDN_HEADS, c, c), F32)],
        scratch_shapes=[pltpu.VMEM((GDN_HEADS, GDN_DK, GDN_DK), F32)],
        compiler_params=_cparams("arbitrary"),
    )(qn, kn, v, proj, proj, par, gnw)


def _gdn_bwd(qn, kn, v, proj, par, gnw, o, sprev, wst, ust, tst, d_oan, dproj):
    t = qn.shape[0]
    c = CHUNK
    nc = t // c
    nb = nc // GDN_CB
    r_ = GDN_CB * c

    def body(qn_ref, kn_ref, v_ref, bd_ref, z_ref, par_ref, gnw_ref, o_ref, sp_ref, w_ref, u_ref, tm_ref, do_ref,
             dp_in_ref, dqn_ref, dkn_ref, dv_ref, dzb_ref, acc_ref, ds_ref):
        @pl.when(pl.program_id(0) == 0)
        def _():
            ds_ref[...] = jnp.zeros_like(ds_ref)
            acc_ref[...] = jnp.zeros_like(acc_ref)

        bd, par, gnw_v = bd_ref[...], par_ref[...], gnw_ref[...]
        lane = lax.broadcasted_iota(jnp.int32, (1, LANES), 1)
        rix = lax.broadcasted_iota(jnp.int32, (c, 1), 0)
        ii = lax.broadcasted_iota(jnp.int32, (c, c), 0)
        jj = lax.broadcasted_iota(jnp.int32, (c, c), 1)
        upper = (jj >= ii).astype(F32)
        acc_a = jnp.zeros((1, LANES), F32)
        acc_d = jnp.zeros((1, LANES), F32)
        acc_g = jnp.zeros((1, LANES), F32)
        ts = _gdn_terms(bd, par, kn_ref, qn_ref)
        for t in ts:
            t.vv = v_ref[t.rows, t.cs]
            t.sh = sp_ref[t.cc, t.h]
            t.sb = t.sh.astype(BF16)
            t.wm, t.u, t.tinv_t = w_ref[t.rows, t.cs], u_ref[t.rows, t.cs], tm_ref[t.cc, t.h]
            t.wb, t.ub = t.wm.astype(BF16), t.u.astype(BF16)
            ov, zz, dout = o_ref[t.rows, t.cs], z_ref[t.rows, t.cs], do_ref[t.rows, t.cs]
            sg = _sigmoid(zz)
            sil = zz * sg
            rr = lax.rsqrt(jnp.mean(ov * ov, axis=-1, keepdims=True) + EPS)
            on = ov * rr
            dzb_ref[t.rows, t.cs] = (dout * on * gnw_v * (sg * (1.0 + zz * (1.0 - sg)))).astype(BF16)
            acc_g = acc_g + jnp.sum(dout * on * sil, axis=0, keepdims=True)
            don = dout * gnw_v * sil
            t.dob = (rr * (don - on * jnp.mean(don * on, axis=-1, keepdims=True))).astype(BF16)
            t.qg = t.q * t.gam
            t.kd = t.k * t.e_rest
            t.qgb, t.kdb = t.qg.astype(BF16), t.kd.astype(BF16)
            t.ptb = t.p.T.astype(BF16)
        for t in ts:
            t.du0 = _dot(t.ptb, t.dob)
            t.ds0 = _dot_tn(t.qgb, t.dob)
            t.dqg = _dot_nt(t.dob, t.sb)
            t.dp = _dot_nt(t.dob, t.ub)
            t.uv = t.u + _dot(t.wb, t.sb)
        dstate = [ds_ref[h] for h in range(GDN_HEADS)]
        for cc in reversed(range(GDN_CB)):
            tc = [t for t in ts if t.cc == cc]
            for t in tc:
                t.dsn = dstate[t.h]
                t.dsnb = t.dsn.astype(BF16)
            for t in tc:
                t.du = t.du0 + _dot(t.kdb, t.dsnb)
            for t in tc:
                t.dub = t.du.astype(BF16)
            for t in tc:
                dstate[t.h] = t.gl * t.dsn + t.ds0 - _dot_tn(t.wb, t.dub)
            for t in tc:
                t.dkd = _dot_nt(t.ub, t.dsnb)
                t.dgl = jnp.sum(jnp.sum(t.dsn * t.sh, axis=1, keepdims=True), axis=0, keepdims=True)
                t.dwm = -_dot_nt(t.dub, t.sb)
        for h in range(GDN_HEADS):
            ds_ref[h] = dstate[h]
        for t in ts:
            tsp = _split(t.tinv_t)
            t.dbk = _dot3s(tsp, _split(t.dwm))
            t.dbv = _dot3s(tsp, _split(t.du))
        for t in ts:
            d_a = -(_dot_nt(t.dbk.astype(BF16), t.wb) + _dot_nt(t.dbv.astype(BF16), t.uv.astype(BF16)))
            t.d_a = jnp.where(t.strict, d_a, 0.0)
        for t in ts:
            t.dkk = t.d_a * t.beta * t.dec_s
            t.dqk = t.dp * t.dec_i
            t.dqkb = t.dqk.astype(BF16)
        for t in ts:
            t.dq = _dot(t.dqkb, t.kb) + t.dqg * t.gam
            t.dk = (t.dbk * (t.beta * t.gam) + _dot_tn(t.dqkb, t.qb) + _dot((t.dkk + t.dkk.T).astype(BF16), t.kb)
                    + t.dkd * t.e_rest)
        for t in ts:
            dbeta = (jnp.sum(t.d_a * t.kk * t.dec_s, axis=-1, keepdims=True)
                     + jnp.sum(t.dbk * t.k * t.gam, axis=-1, keepdims=True) + jnp.sum(t.dbv * t.vv, axis=-1, keepdims=True))
            t.dbl = dbeta * t.beta * (1.0 - t.beta)
            dv_ref[t.rows, t.cs] = t.dbv * t.beta
            bk = (t.beta * t.gam) * t.k
            zc = jnp.sum(t.dkd * t.kd, axis=-1, keepdims=True)
            xs = t.dkk * t.kk + t.dp * t.p
            dgc = (jnp.sum(xs, axis=-1, keepdims=True) - jnp.sum(xs.T, axis=-1, keepdims=True)
                   + jnp.sum(t.dbk * bk, axis=-1, keepdims=True) + jnp.sum(t.dqg * t.qg, axis=-1, keepdims=True) - zc)
            dglast = jnp.sum(zc, axis=0, keepdims=True) + t.dgl * t.gl[:, 0:1]
            t.dgc = dgc + jnp.where(rix == c - 1, dglast, 0.0)
        dgall = _dot(upper, _pairs_to_lanes([t.dgc for t in ts]), HIGHEST)
        for p, t in enumerate(ts):
            t.dg = dgall[:, p:p + 1]
        dbd_tiles = [jnp.zeros((c, LANES), F32) for _ in range(GDN_CB)]
        for t in ts:
            ddl = t.dg * (-t.ea) * _sigmoid(t.sp_arg)
            acc_a = acc_a + jnp.where(lane == t.h, jnp.sum(t.dg * t.g, axis=0, keepdims=True), 0.0)
            acc_d = acc_d + jnp.where(lane == t.h, jnp.sum(ddl, axis=0, keepdims=True), 0.0)
            dbd_tiles[t.cc] = (dbd_tiles[t.cc] + jnp.where(lane == t.h, t.dbl, 0.0)
                               + jnp.where(lane == GDN_HEADS + t.h, ddl, 0.0))
            dqn_ref[t.rows, t.cs] = t.dq * (GDN_DK ** -0.5)
            dkn_ref[t.rows, t.cs] = t.dk
        for cc in range(GDN_CB):
            dzb_ref[cc * c:(cc + 1) * c, KEY_A:KEY_A + LANES] = dbd_tiles[cc].astype(BF16)
        acc_ref[0:1, :] += acc_a
        acc_ref[1:2, :] += acc_d
        acc_ref[2:3, :] += acc_g

    def rev(i):
        return nb - 1 - i

    row = pl.BlockSpec((r_, KEY_A), lambda i: (rev(i), 0))
    st = pl.BlockSpec((GDN_CB, GDN_HEADS, GDN_DK, GDN_DK), lambda i: (rev(i), 0, 0, 0))
    tt_spec = pl.BlockSpec((GDN_CB, GDN_HEADS, c, c), lambda i: (rev(i), 0, 0, 0))
    return pl.pallas_call(
        body, name="gdn_bwd", grid=(nb,),
        in_specs=[row, row, row, pl.BlockSpec((r_, LANES), lambda i: (rev(i), CB_BD)),
                  pl.BlockSpec((r_, PB), lambda i: (rev(i), CB_ZA)),
                  pl.BlockSpec((SUBLANES, LANES), lambda i: (0, 0)), pl.BlockSpec((1, GDN_DK), lambda i: (0, 0)),
                  row, st, row, row, tt_spec, row, pl.BlockSpec(memory_space=pl.ANY)],
        out_specs=[row, row, row, _dp_spec(r_, DP_ZBD, rev), pl.BlockSpec((SUBLANES, LANES), lambda i: (0, 0))],
        out_shape=[jax.ShapeDtypeStruct((t, KEY_A), F32)] * 3 + [jax.ShapeDtypeStruct(dproj.shape, dproj.dtype),
                                                                jax.ShapeDtypeStruct((SUBLANES, LANES), F32)],
        input_output_aliases={13: 3},
        scratch_shapes=[pltpu.VMEM((GDN_HEADS, GDN_DK, GDN_DK), F32)],
        compiler_params=_cparams("arbitrary"),
    )(qn, kn, v, proj, proj, par, gnw, o, sprev, wst, ust, tst, d_oan, dproj)


def _merge_fwd(oan, ob, proj, x, wba, wbb, wout, w2, tm=512):
    t = x.shape[0]

    def body(oa_ref, ob_ref, ga_ref, gb_ref, x_ref, wba_ref, wbb_ref, wout_ref, w2_ref, x2_ref, h2_ref):
        ya = _dot(oa_ref[...], wba_ref[...])
        yb = _dot(ob_ref[...], wbb_ref[...])
        mix = _sigmoid(ga_ref[...]) * ya + _sigmoid(gb_ref[...]) * yb
        x2 = x_ref[...] + _dot(mix.astype(BF16), wout_ref[...])
        x2_ref[...] = x2
        r = lax.rsqrt(jnp.mean(x2 * x2, axis=-1, keepdims=True) + EPS)
        h2_ref[...] = (x2 * r * w2_ref[...]).astype(BF16)

    half = pl.BlockSpec((tm, KEY_A), lambda i: (i, 0))
    row = pl.BlockSpec((tm, D_MODEL), lambda i: (i, 0))
    wsmall = pl.BlockSpec((KEY_A, D_MODEL), lambda i: (0, 0))
    return pl.pallas_call(
        body, name="merge_fwd", grid=(t // tm,),
        in_specs=[half, half, pl.BlockSpec((tm, D_MODEL), lambda i: (i, CB_GA)),
                  pl.BlockSpec((tm, D_MODEL), lambda i: (i, CB_GB)), row, wsmall, wsmall,
                  pl.BlockSpec((D_MODEL, D_MODEL), lambda i: (0, 0)), pl.BlockSpec((1, D_MODEL), lambda i: (0, 0))],
        out_specs=[row, row],
        out_shape=[jax.ShapeDtypeStruct((t, D_MODEL), F32), jax.ShapeDtypeStruct((t, D_MODEL), BF16)],
        compiler_params=_cparams("parallel"),
    )(oan, ob, proj, proj, x, wba, wbb, wout, w2)


def _merge_bwd(dx2b, oan, ob, proj, wba, wbb, wout_t, wba_t, wbb_t, tm=512):
    t = dx2b.shape[0]

    def body(dx_ref, oa_ref, ob_ref, ga_ref, gb_ref, wba_ref, wbb_ref, woutt_ref, wbat_ref, wbbt_ref,
             dg_ref, doa_ref, dob_ref, gout_ref, gba_ref, gbb_ref):
        @pl.when(pl.program_id(0) == 0)
        def _():
            gout_ref[...] = jnp.zeros_like(gout_ref)
            gba_ref[...] = jnp.zeros_like(gba_ref)
            gbb_ref[...] = jnp.zeros_like(gbb_ref)

        dx, oa, ob = dx_ref[...], oa_ref[...], ob_ref[...]
        dmix = _dot(dx, woutt_ref[...])
        ya = _dot(oa, wba_ref[...])
        yb = _dot(ob, wbb_ref[...])
        sa, sb = _sigmoid(ga_ref[...]), _sigmoid(gb_ref[...])
        gout_ref[...] += _dot_tn((sa * ya + sb * yb).astype(BF16), dx)
        dg_ref[:, :D_MODEL] = (dmix * ya * sa * (1.0 - sa)).astype(BF16)
        dg_ref[:, D_MODEL:] = (dmix * yb * sb * (1.0 - sb)).astype(BF16)
        dya = (dmix * sa).astype(BF16)
        dyb = (dmix * sb).astype(BF16)
        gba_ref[...] += _dot_tn(oa, dya)
        gbb_ref[...] += _dot_tn(ob, dyb)
        doa_ref[...] = _dot(dya, wbat_ref[...])
        dob_ref[...] = _dot(dyb, wbbt_ref[...])

    half = pl.BlockSpec((tm, KEY_A), lambda i: (i, 0))
    row = pl.BlockSpec((tm, D_MODEL), lambda i: (i, 0))
    wsmall = pl.BlockSpec((KEY_A, D_MODEL), lambda i: (0, 0))
    wsmall_t = pl.BlockSpec((D_MODEL, KEY_A), lambda i: (0, 0))
    wfull = pl.BlockSpec((D_MODEL, D_MODEL), lambda i: (0, 0))
    return pl.pallas_call(
        body, name="merge_bwd", grid=(t // tm,),
        in_specs=[row, half, half, pl.BlockSpec((tm, D_MODEL), lambda i: (i, CB_GA)),
                  pl.BlockSpec((tm, D_MODEL), lambda i: (i, CB_GB)), wsmall, wsmall, wfull, wsmall_t, wsmall_t],
        out_specs=[_dp_spec(tm, DP_GATES), half, half, wfull, wsmall, wsmall],
        out_shape=[jax.ShapeDtypeStruct((t, PROJ_W), BF16), jax.ShapeDtypeStruct((t, KEY_A), F32),
                   jax.ShapeDtypeStruct((t, KEY_A), F32), jax.ShapeDtypeStruct((D_MODEL, D_MODEL), F32),
                   jax.ShapeDtypeStruct((KEY_A, D_MODEL), F32), jax.ShapeDtypeStruct((WIDTH_B, D_MODEL), F32)],
        compiler_params=_cparams("arbitrary"),
    )(dx2b, oan, ob, proj, proj, wba, wbb, wout_t, wba_t, wbb_t)


FFN_TM = 128
FFN_W = 2 * D_FF


def _resident(shape):
    return pl.BlockSpec(shape, lambda i: (0,) * len(shape), pipeline_mode=pl.Buffered(1))


def _ffn_fwd(h2, wup, cw, cb, wdown, x2, tgt, w3):
    t = x2.shape[0]
    tm = FFN_TM

    def body(h2_ref, wup_ref, cw_ref, cb_ref, wd_ref, x2_ref, tgt_ref, w3_ref, up_ref, u_ref, dx_ref, dxb_ref, act_ref,
             acc_ref, prev_ref):
        @pl.when(pl.program_id(0) == 0)
        def _():
            acc_ref[...] = jnp.zeros_like(acc_ref)
            prev_ref[...] = jnp.zeros_like(prev_ref)

        up = _dot(h2_ref[...], wup_ref[...])
        up_ref[...] = up
        ext = jnp.concatenate([prev_ref[...], up], axis=0)
        prev_ref[...] = up[tm - HALO:tm]
        taps = _conv_taps(ext, FFN_CONV, HALO, tm)
        cw_v = cw_ref[...]
        u = sum(cw_v[k:k + 1, :] * taps[k] for k in range(FFN_CONV)) + cb_ref[...]
        u_ref[...] = u
        gate, upp = u[:, :D_FF], u[:, D_FF:]
        act = (gate * _sigmoid(gate) * upp).astype(BF16)
        act_ref[...] = act
        x3 = x2_ref[...] + _dot(act, wd_ref[...])
        r = lax.rsqrt(jnp.mean(x3 * x3, axis=-1, keepdims=True) + EPS)
        xh = x3 * r
        w3v = w3_ref[...]
        err = xh * w3v - tgt_ref[...]
        loss = 0.5 * jnp.sum(jnp.mean(err * err, axis=-1, keepdims=True), axis=0, keepdims=True)
        dy = err * (1.0 / D_MODEL)
        acc_ref[0:1, :] += jnp.sum(dy * xh, axis=0, keepdims=True)
        acc_ref[1:2, :] += jnp.broadcast_to(loss, (1, D_MODEL))
        dxh = dy * w3v
        dx = r * (dxh - xh * jnp.mean(dxh * xh, axis=-1, keepdims=True))
        dx_ref[...] = dx
        dxb_ref[...] = dx.astype(BF16)

    row = pl.BlockSpec((tm, D_MODEL), lambda i: (i, 0))
    return pl.pallas_call(
        body, name="ffn_fwd", grid=(t // tm,),
        in_specs=[row, _resident((D_MODEL, FFN_W)), _resident((SUBLANES, FFN_W)), _resident((1, FFN_W)),
                  _resident((D_FF, D_MODEL)), row, row, _resident((1, D_MODEL))],
        out_specs=[pl.BlockSpec((tm, FFN_W), lambda i: (i, 0)), pl.BlockSpec((tm, FFN_W), lambda i: (i, 0)), row, row,
                   pl.BlockSpec((tm, D_FF), lambda i: (i, 0)), pl.BlockSpec((SUBLANES, D_MODEL), lambda i: (0, 0))],
        out_shape=[jax.ShapeDtypeStruct((t, FFN_W), F32), jax.ShapeDtypeStruct((t, FFN_W), F32),
                   jax.ShapeDtypeStruct((t, D_MODEL), F32),
                   jax.ShapeDtypeStruct((t, D_MODEL), BF16), jax.ShapeDtypeStruct((t, D_FF), BF16),
                   jax.ShapeDtypeStruct((SUBLANES, D_MODEL), F32)],
        scratch_shapes=[pltpu.VMEM((HALO, FFN_W), F32)],
        compiler_params=_cparams("arbitrary"),
    )(h2, wup, cw, cb, wdown, x2, tgt, w3)


def _ffn_bwd(dx3b, wdown_t, up, u, cw, wup_t, x2, w2, dx3, carry):
    t = up.shape[0]
    tm = FFN_TM
    nt = t // tm
    n_ext = tm + HALO
    arrays, scatter = carry
    nx = len(arrays)

    def rev(i):
        return nt - 1 - i

    def body(*refs):
        dx_ref, wdt_ref, up_ref, u_ref, cw_ref, wupt_ref, x2_ref, w2_ref, dres_ref = refs[:9]
        srcs = refs[9:9 + nx]
        dup_ref, acc_ref, dx2_ref, dx2b_ref, dw2_ref = refs[9 + nx:14 + nx]
        dsts = refs[14 + nx:14 + 2 * nx]
        nxt_ref = refs[14 + 2 * nx]
        i = pl.program_id(0)
        local, remote = _exchange_copies(srcs, dsts, scatter, *refs[15 + 2 * nx:])

        @pl.when(i == 0)
        def _():
            for cp in local + remote:
                cp.start()
            acc_ref[...] = jnp.zeros_like(acc_ref)
            dw2_ref[...] = jnp.zeros_like(dw2_ref)
            nxt_ref[...] = jnp.zeros_like(nxt_ref)

        dact = _dot(dx_ref[...], wdt_ref[...])
        gate, upp = u_ref[:, :D_FF], u_ref[:, D_FF:]
        sg = _sigmoid(gate)
        du = jnp.concatenate([dact * upp * (sg * (1.0 + gate * (1.0 - sg))), dact * (gate * sg)], axis=1)
        acc_ref[FFN_CONV:FFN_CONV + 1, :] += jnp.sum(du, axis=0, keepdims=True)
        ext = jnp.concatenate([du, nxt_ref[...]], axis=0)
        cw_v = cw_ref[...]
        upv = up_ref[...]
        dup = None
        for k in range(FFN_CONV):
            shift = FFN_CONV - 1 - k
            tap = du if shift == 0 else pltpu.roll(ext, n_ext - shift, 0)[:tm]
            term = cw_v[k:k + 1, :] * tap
            dup = term if dup is None else dup + term
            acc_ref[k:k + 1, :] += jnp.sum(tap * upv, axis=0, keepdims=True)
        dupb = dup.astype(BF16)
        dup_ref[...] = dupb
        nxt_ref[...] = du[0:HALO]
        dhv = _dot(dupb, wupt_ref[...])
        xv = x2_ref[...]
        r = lax.rsqrt(jnp.mean(xv * xv, axis=-1, keepdims=True) + EPS)
        xh = xv * r
        dw2_ref[0:1, :] += jnp.sum(dhv * xh, axis=0, keepdims=True)
        dxh = dhv * w2_ref[...]
        dx2 = dres_ref[...] + r * (dxh - xh * jnp.mean(dxh * xh, axis=-1, keepdims=True))
        dx2_ref[...] = dx2
        dx2b_ref[...] = dx2.astype(BF16)

        @pl.when(i == nt - 1)
        def _():
            for cp in remote + local:
                cp.wait()

    wide = pl.BlockSpec((tm, FFN_W), lambda i: (rev(i), 0))
    row = pl.BlockSpec((tm, D_MODEL), lambda i: (rev(i), 0))
    any_spec = pl.BlockSpec(memory_space=pl.ANY)
    return pl.pallas_call(
        body, name="ffn_bwd", grid=(nt,),
        in_specs=[row, _resident((D_MODEL, D_FF)), wide, wide,
                  _resident((SUBLANES, FFN_W)), _resident((FFN_W, D_MODEL)), row,
                  _resident((1, D_MODEL)), row] + [any_spec] * nx,
        out_specs=[wide, pl.BlockSpec((SUBLANES, FFN_W), lambda i: (0, 0)), row, row,
                   pl.BlockSpec((SUBLANES, D_MODEL), lambda i: (0, 0))] + [any_spec] * nx,
        out_shape=[jax.ShapeDtypeStruct((t, FFN_W), BF16), jax.ShapeDtypeStruct((SUBLANES, FFN_W), F32),
                   jax.ShapeDtypeStruct((t, D_MODEL), F32), jax.ShapeDtypeStruct((t, D_MODEL), BF16),
                   jax.ShapeDtypeStruct((SUBLANES, D_MODEL), F32)] + _exchange_shapes(arrays, scatter),
        scratch_shapes=[pltpu.VMEM((HALO, FFN_W), F32)] + _exchange_sems(nx),
        compiler_params=_cparams("arbitrary"),
    )(dx3b, wdown_t, up, u, cw, wup_t, x2, w2, dx3, *arrays)


def _adamw(parts, w, m, v, name, tr):
    r, cols = w.shape

    def body(p_ref, w_ref, m_ref, v_ref, g_ref, d_ref, mo_ref, vo_ref):
        g = p_ref[0].astype(F32)
        for s in range(1, N_DEV):
            g = g + p_ref[s].astype(F32)
        mm = ADAM_B1 * m_ref[...] + (1.0 - ADAM_B1) * g
        vv = ADAM_B2 * v_ref[...] + (1.0 - ADAM_B2) * (g * g)
        m_hat = mm / (1.0 - ADAM_B1 ** ADAM_STEP)
        v_hat = vv / (1.0 - ADAM_B2 ** ADAM_STEP)
        g_ref[...] = g
        d_ref[...] = -ADAM_LR * (m_hat / (jnp.sqrt(v_hat) + ADAM_EPS) + ADAM_WD * w_ref[...])
        mo_ref[...] = mm
        vo_ref[...] = vv

    assert r % tr == 0
    row = pl.BlockSpec((tr, cols), lambda i: (i, 0))
    return pl.pallas_call(
        body, name=name, grid=(r // tr,),
        in_specs=[pl.BlockSpec((N_DEV, tr, cols), lambda i: (0, i, 0)), row, row, row],
        out_specs=[row, row, row, row],
        out_shape=[jax.ShapeDtypeStruct((r, cols), F32)] * 4,
        compiler_params=_cparams("parallel"),
    )(parts, w, m, v)


def _mesh_pos():
    return lax.axis_index("x"), lax.axis_index("y"), lax.axis_index("c")


def _peer(pos, k):
    x, y, c = pos
    return (x ^ ((k >> 2) & 1), y ^ ((k >> 1) & 1), c ^ (k & 1))


def _flat_id(pos):
    return 4 * pos[0] + 2 * pos[1] + pos[2]


def _exchange_copies(srcs, dsts, scatter, send_sems, recv_sems, loc_sems):
    pos = _mesh_pos()
    me = _flat_id(pos)
    local, remote = [], []
    for j, (src, dst) in enumerate(zip(srcs, dsts)):
        local.append(pltpu.make_async_copy(src.at[me] if scatter[j] else src, dst.at[me], loc_sems.at[j]))
        for k in range(1, N_DEV):
            to = _peer(pos, k)
            remote.append(pltpu.make_async_remote_copy(
                src_ref=src.at[_flat_id(to)] if scatter[j] else src, dst_ref=dst.at[me],
                send_sem=send_sems.at[j, k - 1], recv_sem=recv_sems.at[j, k - 1],
                device_id=to, device_id_type=pl.DeviceIdType.MESH))
    return local, remote


def _exchange_shapes(arrays, scatter):
    return [jax.ShapeDtypeStruct(a.shape if s else (N_DEV,) + a.shape, a.dtype) for a, s in zip(arrays, scatter)]


def _exchange_sems(n):
    return [pltpu.SemaphoreType.DMA((n, N_DEV - 1)), pltpu.SemaphoreType.DMA((n, N_DEV - 1)), pltpu.SemaphoreType.DMA((n,))]


def _exchange(arrays, scatter, name):
    n = len(arrays)
    any_spec = pl.BlockSpec(memory_space=pl.ANY)

    def body(*refs):
        local, remote = _exchange_copies(refs[:n], refs[n:2 * n], scatter, *refs[2 * n:])
        for cp in local + remote:
            cp.start()
        for cp in remote:
            cp.wait()
        for cp in local:
            cp.wait()

    return pl.pallas_call(
        body, name=name, in_specs=[any_spec] * n, out_specs=[any_spec] * n,
        out_shape=_exchange_shapes(arrays, scatter), scratch_shapes=_exchange_sems(n),
    )(*arrays)


def _pad_rows(a, rows):
    return jnp.pad(a, ((0, rows - a.shape[0]),) + ((0, 0),) * (a.ndim - 1))


PACK_UNIT = SUBLANES * LANES


def _pack_lanes(parts, rows):
    out = []
    for a in parts:
        f = a.reshape(-1)
        out.append(jnp.pad(f, (0, (-f.shape[0]) % PACK_UNIT)).reshape(-1, LANES))
    packed = jnp.concatenate(out, axis=0)
    assert packed.shape[0] == rows, (packed.shape, rows)
    return packed


def _unpack_lanes(buf, shapes):
    out, r0 = [], 0
    for shp in shapes:
        n = math.prod(shp)
        nr = -(-n // PACK_UNIT) * SUBLANES
        out.append(buf[r0:r0 + nr].reshape(-1)[:n].reshape(shp))
        r0 += nr
    return out


def _col_shards(g):
    r, n = g.shape
    return g.reshape(r, N_DEV, n // N_DEV).transpose(1, 0, 2)


def _col_unshard(s):
    _, r, w = s.shape
    return s.transpose(1, 0, 2).reshape(r, N_DEV * w)


def _lane_rows(flat):
    n = flat.shape[1]
    return jnp.pad(flat, ((0, 0), (0, (-n) % PACK_UNIT))).reshape(N_DEV, -1, LANES)


SMALL_ROWS = 128
WS_ROWS = 32


def kernel(x, norm_mix_w, w_in, conv_qkv_w, a_log, dt_bias, gdn_norm_w, w_branch_a, w_branch_b, rel_bias, w_out, norm_ffn_w, w_up, conv_ffn_w, conv_ffn_b, w_down, norm_final_w, loss_target, m_norm_mix_w, m_w_in, m_conv_qkv_w, m_a_log, m_dt_bias, m_gdn_norm_w, m_w_branch_a, m_w_branch_b, m_rel_bias, m_w_out, m_norm_ffn_w, m_w_up, m_conv_ffn_w, m_conv_ffn_b, m_w_down, m_norm_final_w, v_norm_mix_w, v_w_in, v_conv_qkv_w, v_a_log, v_dt_bias, v_gdn_norm_w, v_w_branch_a, v_w_branch_b, v_rel_bias, v_w_out, v_norm_ffn_w, v_w_up, v_conv_ffn_w, v_conv_ffn_b, v_w_down, v_norm_final_w):
    big_w = (w_in, w_branch_a, w_branch_b, w_out, w_up, w_down, conv_qkv_w, conv_ffn_w)
    big_m = (m_w_in, m_w_branch_a, m_w_branch_b, m_w_out, m_w_up, m_w_down, m_conv_qkv_w, m_conv_ffn_w)
    big_v = (v_w_in, v_w_branch_a, v_w_branch_b, v_w_out, v_w_up, v_w_down, v_conv_qkv_w, v_conv_ffn_w)
    small_w = (norm_mix_w, a_log, dt_bias, gdn_norm_w, rel_bias, norm_ffn_w, conv_ffn_b, norm_final_w)
    small_m = (m_norm_mix_w, m_a_log, m_dt_bias, m_gdn_norm_w, m_rel_bias, m_norm_ffn_w, m_conv_ffn_b, m_norm_final_w)
    small_v = (v_norm_mix_w, v_a_log, v_dt_bias, v_gdn_norm_w, v_rel_bias, v_norm_ffn_w, v_conv_ffn_b, v_norm_final_w)

    xs, tgt = x[0], loss_target[0]
    ws = _pack_lanes(big_w[6:], WS_ROWS)
    h1, g_in, gs = _rmsnorm_cast(xs, norm_mix_w, "norm_mix", carry=([w_in[0].astype(BF16), ws], (False, False)))
    win = _col_unshard(g_in)
    gs = gs.reshape(N_DEV, -1)
    cqkv = gs[:, :GDN_CONV * 192].reshape(N_DEV, GDN_CONV, 192).transpose(1, 0, 2).reshape(GDN_CONV, 3 * KEY_A)
    cffn = gs[:, PACK_UNIT:PACK_UNIT + FFN_CONV * 704].reshape(N_DEV, FFN_CONV, 704).transpose(1, 0, 2).reshape(FFN_CONV, FFN_W)
    cffn = _pad_rows(cffn, SUBLANES)
    w_all = jnp.concatenate([win[:, a:b] for a, b in W_IN_ORDER] + [jnp.zeros((D_MODEL, PROJ_W - D_IN), BF16)], axis=1)
    par = _pad_rows(jnp.pad(jnp.concatenate([a_log, dt_bias], axis=0), ((0, 0), (0, LANES - GDN_HEADS))), SUBLANES)
    table = jnp.pad(rel_bias[0], ((0, 0), (0, 3 * LANES - rel_bias.shape[-1]))).reshape(ATT_HEADS, 1, 3 * LANES)

    proj, g_ba, g_bb, g_out, g_up, g_down = _mm_nn(
        h1, w_all, F32, "in_proj", 2 * MM_TM, 1152, D_MODEL, carry=([w[0].astype(BF16) for w in big_w[1:6]], (False,) * 5))
    wba, wbb, wup = _col_unshard(g_ba), _col_unshard(g_bb), _col_unshard(g_up)
    wout = g_out.reshape(D_MODEL, D_MODEL)
    wdown = g_down.reshape(D_FF, D_MODEL)
    qn, kn, va = _gdn_prep_fwd(proj, cqkv)
    oan, o_gdn, sprev, wst, ust, tst = _gdn_fwd(qn, kn, va, proj, par, gdn_norm_w)
    bias_q, bias_k = _att_bias(table)
    ob, lse, lse_t = _att_fwd(proj, bias_q)
    x2, h2 = _merge_fwd(oan, ob, proj, xs, wba, wbb, wout, norm_ffn_w)
    up, u_ffn, dx3, dx3b, act, tail_sums = _ffn_fwd(h2, wup, cffn, conv_ffn_b, wdown, x2, tgt,
                                                    norm_final_w.reshape(1, D_MODEL))

    g_wdown = _mm_tn(act, dx3b, "dw_down", 512)
    dup, ffn_sums, dx2, dx2b, nffn_sums, r_down = _ffn_bwd(
        dx3b, wdown.T, up, u_ffn, cffn, wup.T, x2, norm_ffn_w, dx3,
        carry=([g_wdown.reshape(N_DEV, -1, D_MODEL).astype(BF16)], (True,)))
    g_wup = _mm_tn(h2, dup, "dw_up", 1408)
    dproj, d_oan, d_ob, g_wout, g_wba, g_wbb = _merge_bwd(dx2b, oan, ob, proj, wba, wbb, wout.T, wba.T, wbb.T)
    dproj, dlt_t, slabs = _att_dq(proj, bias_q, lse, d_ob, ob, dproj)
    dproj = _att_dkv(proj, bias_k, lse_t, dlt_t, d_ob, dproj)
    g_rel = _relbias_grad(slabs)[:, 0, :rel_bias.shape[-1]]
    dqn, dkn, dva, dproj, gdn_sums = _gdn_bwd(qn, kn, va, proj, par, gdn_norm_w, o_gdn, sprev, wst, ust, tst, d_oan, dproj)
    dproj, cq_sums = _gdn_prep_bwd(proj, cqkv, dqn, dkn, dva, dproj)
    g_wall, r_up = _mm_tn(h1, dproj, "dw_in", 1152, carry=([_col_shards(g_wup).astype(BF16)], (True,)))
    starts = np.cumsum([0] + [b - a for a, b in W_IN_ORDER])
    g_win = jnp.concatenate([g_wall[:, starts[i]:starts[i + 1]] for i in np.argsort([a for a, _ in W_IN_ORDER])], axis=1)
    g_conv = jnp.concatenate([_lane_rows(_col_shards(cq_sums[:GDN_CONV]).reshape(N_DEV, -1)),
                              _lane_rows(_col_shards(ffn_sums[:FFN_CONV]).reshape(N_DEV, -1))], axis=1)
    grad_x, _, nmix_sums, r_in, r_ba, r_bb, r_out, r_conv = _mm_rms_bwd(
        dproj, w_all.T, xs, norm_mix_w, dx2, "in_proj_bwd", MM_TM, 1152, carry=(
            [_col_shards(g_win).astype(BF16), _col_shards(g_wba).astype(BF16), _col_shards(g_wbb).astype(BF16),
             g_wout.reshape(N_DEV, -1, D_MODEL).astype(BF16), g_conv], (True,) * 5))

    small_g = (nmix_sums[0:1], gdn_sums[0:1, :GDN_HEADS], gdn_sums[1:2, :GDN_HEADS], gdn_sums[2:3], g_rel,
               nffn_sums[0:1], ffn_sums[FFN_CONV:FFN_CONV + 1], tail_sums[0:1], tail_sums[1:2, 0:1])
    r_small, = _exchange([_pack_lanes(small_g, SMALL_ROWS)], (False,), "all_gather_small_grads")
    recv = (r_in, r_ba, r_bb, r_out, r_up, r_down, r_conv, r_small)

    res = {}
    for i, (nm, tr) in enumerate((("w_in", 128), ("w_branch_a", KEY_A), ("w_branch_b", WIDTH_B), ("w_out", 128),
                                  ("w_up", 128), ("w_down", 176))):
        res[nm] = [o[None] for o in _adamw(recv[i], big_w[i][0], big_m[i][0], big_v[i][0], "adamw_" + nm, tr)]
    conv = _adamw(recv[6], _pack_lanes(big_w[6:], WS_ROWS), _pack_lanes(big_m[6:], WS_ROWS), _pack_lanes(big_v[6:], WS_ROWS),
                  "adamw_conv", WS_ROWS)
    conv = [_unpack_lanes(o, [w.shape for w in big_w[6:]]) for o in conv]
    res["conv_qkv_w"] = [o[0] for o in conv]
    res["conv_ffn_w"] = [o[1] for o in conv]
    small_shapes = [w.shape for w in small_w]
    zero = jnp.zeros((1,), F32)
    small = _adamw(recv[7], _pack_lanes(small_w + (zero,), SMALL_ROWS), _pack_lanes(small_m + (zero,), SMALL_ROWS),
                   _pack_lanes(small_v + (zero,), SMALL_ROWS), "adamw_replicated", SMALL_ROWS)
    small = [_unpack_lanes(o, small_shapes + [()]) for o in small]
    loss = small[0][-1]
    for j, nm in enumerate(("norm_mix_w", "a_log", "dt_bias", "gdn_norm_w", "rel_bias", "norm_ffn_w", "conv_ffn_b",
                            "norm_final_w")):
        res[nm] = [o[j] for o in small]

    names = ("norm_mix_w", "w_in", "conv_qkv_w", "a_log", "dt_bias", "gdn_norm_w", "w_branch_a", "w_branch_b", "rel_bias",
             "w_out", "norm_ffn_w", "w_up", "conv_ffn_w", "conv_ffn_b", "w_down", "norm_final_w")
    outs = [res[n][kind] for kind in range(4) for n in names]
    return (loss, grad_x[None], *outs)
```

```python
import functools
import math

import numpy as np
import jax
import jax.numpy as jnp
from jax import lax
from jax.experimental import pallas as pl
from jax.experimental.pallas import tpu as pltpu

F32, BF16 = jnp.float32, jnp.bfloat16
HIGHEST = lax.Precision.HIGHEST

N_DEV = 8
D_MODEL = 1024
CHUNK = 64
EPS = 1e-6
GDN_HEADS, GDN_DK = 4, 128
KEY_A = GDN_HEADS * GDN_DK
GDN_CONV = 4
ATT_HEADS, ATT_DH = 8, 64
WIDTH_B = ATT_HEADS * ATT_DH
ATT_BAND = 9
REL_CLIP = 128
D_FF = 2816
FFN_CONV = 3
D_IN = 5640
ADAM_LR, ADAM_B1, ADAM_B2, ADAM_EPS, ADAM_WD, ADAM_STEP = 0.001, 0.9, 0.999, 1e-08, 0.01, 10

LANES = 128
SUBLANES = 8
NEG = -1e30

PROJ_W = 5760
PB = 512
CB_GA, CB_GB = 0, 1
CB_KB, CB_VB, CB_QA, CB_KA, CB_VA, CB_QB, CB_ZA = 4, 5, 6, 7, 8, 9, 10
CB_BD = 44
DP_GATES, DP_KVB, DP_QKVA, DP_QB, DP_ZBD = (2048, 0), (1024, 2), (1536, 2), (512, 9), (640, 8)
W_IN_ORDER = ((3592, 5640), (2568, 3592), (0, 1536), (2056, 2568), (1536, 2048), (2048, 2056))

ATT_QB = 256
ATT_KW = 768
ATT_VEC = 1024


def _dot(a, b, precision=None):
    return jnp.dot(a, b, preferred_element_type=F32, precision=precision)


def _dot_nt(a, b, precision=None):
    return lax.dot_general(a, b, (((1,), (1,)), ((), ())), preferred_element_type=F32, precision=precision)


def _dot_tn(a, b):
    return lax.dot_general(a, b, (((0,), (0,)), ((), ())), preferred_element_type=F32)


def _split(a):
    hi = a.astype(BF16)
    return hi, (a - hi.astype(F32)).astype(BF16)


def _dot3s(a, b):
    return _dot(a[0], b[0]) + (_dot(a[0], b[1]) + _dot(a[1], b[0]))


def _sigmoid(x):
    return 0.5 * jnp.tanh(0.5 * x) + 0.5


def _softplus(x):
    return jnp.maximum(x, 0.0) + jnp.log(1.0 + jnp.exp(-jnp.abs(x)))


def _cparams(*sem):
    return pltpu.CompilerParams(dimension_semantics=tuple(sem))


def _dp_spec(tm, region, index=lambda i: i):
    width, cb = region
    return pl.BlockSpec((tm, width), lambda i: (index(i), cb))


def _rmsnorm_cast(x, w, name, tm=512, carry=((), ())):
    t, d = x.shape
    nt = t // tm
    arrays, scatter = carry
    nx = len(arrays)

    def body(*refs):
        x_ref, w_ref = refs[:2]
        o_ref = refs[2 + nx]
        i = pl.program_id(0)
        if nx:
            local, remote = _exchange_copies(refs[2:2 + nx], refs[3 + nx:3 + 2 * nx], scatter, *refs[3 + 2 * nx:])

            @pl.when(i == 0)
            def _():
                for cp in local + remote:
                    cp.start()

        xv = x_ref[...]
        r = lax.rsqrt(jnp.mean(xv * xv, axis=-1, keepdims=True) + EPS)
        o_ref[...] = (xv * r * w_ref[...]).astype(BF16)

        if nx:
            @pl.when(i == nt - 1)
            def _():
                for cp in remote + local:
                    cp.wait()

    any_spec = pl.BlockSpec(memory_space=pl.ANY)
    out = pl.pallas_call(
        body, name=name, grid=(nt,),
        in_specs=[pl.BlockSpec((tm, d), lambda i: (i, 0)), pl.BlockSpec((1, d), lambda i: (0, 0))] + [any_spec] * nx,
        out_specs=[pl.BlockSpec((tm, d), lambda i: (i, 0))] + [any_spec] * nx,
        out_shape=[jax.ShapeDtypeStruct((t, d), BF16)] + _exchange_shapes(arrays, scatter),
        scratch_shapes=_exchange_sems(nx) if nx else [],
        compiler_params=_cparams("arbitrary" if nx else "parallel"),
    )(x, w, *arrays)
    return out if nx else out[0]


def _mm_nn(a, b, out_dtype, name, tm, tn, tk, carry=((), ())):
    m, k = a.shape
    _, n = b.shape
    tm = min(tm, m)
    nk = k // tk
    assert m % tm == 0 and n % tn == 0 and k % tk == 0
    arrays, scatter = carry
    nx = len(arrays)
    gm, gn = m // tm, n // tn

    def body(*refs):
        a_ref, b_ref = refs[:2]
        srcs = refs[2:2 + nx]
        o_ref = refs[2 + nx]
        dsts = refs[3 + nx:3 + 2 * nx]
        rest = refs[3 + 2 * nx:]
        i, j, kk = pl.program_id(0), pl.program_id(1), pl.program_id(2)
        if nx:
            local, remote = _exchange_copies(srcs, dsts, scatter, *rest[-3:])

            @pl.when((i == 0) & (j == 0) & (kk == 0))
            def _():
                for cp in local + remote:
                    cp.start()

        if nk == 1:
            o_ref[...] = _dot(a_ref[...], b_ref[...]).astype(out_dtype)
        else:
            acc_ref = rest[0]

            @pl.when(kk == 0)
            def _():
                acc_ref[...] = jnp.zeros_like(acc_ref)

            acc_ref[...] += _dot(a_ref[...], b_ref[...])

            @pl.when(kk == nk - 1)
            def _():
                o_ref[...] = acc_ref[...].astype(out_dtype)

        if nx:
            @pl.when((i == gm - 1) & (j == gn - 1) & (kk == nk - 1))
            def _():
                for cp in remote + local:
                    cp.wait()

    any_spec = pl.BlockSpec(memory_space=pl.ANY)
    scratch = ([pltpu.VMEM((tm, tn), F32)] if nk > 1 else []) + (_exchange_sems(nx) if nx else [])
    out = pl.pallas_call(
        body, name=name, grid=(gm, gn, nk),
        in_specs=[pl.BlockSpec((tm, tk), lambda i, j, kk: (i, kk)),
                  pl.BlockSpec((tk, tn), lambda i, j, kk: (kk, j))] + [any_spec] * nx,
        out_specs=[pl.BlockSpec((tm, tn), lambda i, j, kk: (i, j))] + [any_spec] * nx,
        out_shape=[jax.ShapeDtypeStruct((m, n), out_dtype)] + _exchange_shapes(arrays, scatter),
        scratch_shapes=scratch,
        compiler_params=_cparams(*(("arbitrary",) * 3 if nx else ("parallel", "parallel", "arbitrary"))),
    )(a, b, *arrays)
    return out if nx else out[0]


def _mm_rms_bwd(a, b, x, w, dres, name, tm, tk, carry):
    m, k = a.shape
    _, n = b.shape
    nk = k // tk
    gm = m // tm
    assert m % tm == 0 and k % tk == 0
    arrays, scatter = carry
    nx = len(arrays)

    def body(*refs):
        a_ref, b_ref, x_ref, w_ref, dres_ref = refs[:5]
        srcs = refs[5:5 + nx]
        dx_ref, dxb_ref, dw_ref = refs[5 + nx:8 + nx]
        dsts = refs[8 + nx:8 + 2 * nx]
        acc_ref = refs[8 + 2 * nx]
        i, kk = pl.program_id(0), pl.program_id(1)
        local, remote = _exchange_copies(srcs, dsts, scatter, *refs[9 + 2 * nx:])

        @pl.when((i == 0) & (kk == 0))
        def _():
            for cp in local + remote:
                cp.start()
            dw_ref[...] = jnp.zeros_like(dw_ref)

        @pl.when(kk == 0)
        def _():
            acc_ref[...] = jnp.zeros_like(acc_ref)

        acc_ref[...] += _dot(a_ref[...], b_ref[...])

        @pl.when(kk == nk - 1)
        def _():
            dhv = acc_ref[...]
            xv = x_ref[...]
            r = lax.rsqrt(jnp.mean(xv * xv, axis=-1, keepdims=True) + EPS)
            xh = xv * r
            dw_ref[0:1, :] += jnp.sum(dhv * xh, axis=0, keepdims=True)
            dxh = dhv * w_ref[...]
            dx = dres_ref[...] + r * (dxh - xh * jnp.mean(dxh * xh, axis=-1, keepdims=True))
            dx_ref[...] = dx
            dxb_ref[...] = dx.astype(BF16)

        @pl.when((i == gm - 1) & (kk == nk - 1))
        def _():
            for cp in remote + local:
                cp.wait()

    any_spec = pl.BlockSpec(memory_space=pl.ANY)
    row = pl.BlockSpec((tm, n), lambda i, kk: (i, 0))
    return pl.pallas_call(
        body, name=name, grid=(gm, nk),
        in_specs=[pl.BlockSpec((tm, tk), lambda i, kk: (i, kk)), pl.BlockSpec((tk, n), lambda i, kk: (kk, 0)),
                  row, pl.BlockSpec((1, n), lambda i, kk: (0, 0)), row] + [any_spec] * nx,
        out_specs=[row, row, pl.BlockSpec((SUBLANES, n), lambda i, kk: (0, 0))] + [any_spec] * nx,
        out_shape=[jax.ShapeDtypeStruct((m, n), F32), jax.ShapeDtypeStruct((m, n), BF16),
                   jax.ShapeDtypeStruct((SUBLANES, n), F32)] + _exchange_shapes(arrays, scatter),
        scratch_shapes=[pltpu.VMEM((tm, n), F32)] + _exchange_sems(nx),
        compiler_params=_cparams("arbitrary", "arbitrary"),
    )(a, b, x, w, dres, *arrays)


MM_TM = 1024


def _mm_tn(a, b, name, tn, tk=2 * MM_TM, carry=((), ())):
    t, m = a.shape
    _, n = b.shape
    tk = min(tk, t)
    assert t % tk == 0 and n % tn == 0
    gn, gs = n // tn, t // tk
    arrays, scatter = carry
    nx = len(arrays)

    def body(*refs):
        a_ref, b_ref = refs[:2]
        o_ref = refs[2 + nx]
        j, s = pl.program_id(0), pl.program_id(1)
        if nx:
            local, remote = _exchange_copies(refs[2:2 + nx], refs[3 + nx:3 + 2 * nx], scatter, *refs[3 + 2 * nx:])

            @pl.when((j == 0) & (s == 0))
            def _():
                for cp in local + remote:
                    cp.start()

        @pl.when(s == 0)
        def _():
            o_ref[...] = jnp.zeros_like(o_ref)

        o_ref[...] += _dot_tn(a_ref[...], b_ref[...])

        if nx:
            @pl.when((j == gn - 1) & (s == gs - 1))
            def _():
                for cp in remote + local:
                    cp.wait()

    any_spec = pl.BlockSpec(memory_space=pl.ANY)
    out = pl.pallas_call(
        body, name=name, grid=(gn, gs),
        in_specs=[pl.BlockSpec((tk, m), lambda j, s: (s, 0)),
                  pl.BlockSpec((tk, tn), lambda j, s: (s, j))] + [any_spec] * nx,
        out_specs=[pl.BlockSpec((m, tn), lambda j, s: (0, j))] + [any_spec] * nx,
        out_shape=[jax.ShapeDtypeStruct((m, n), F32)] + _exchange_shapes(arrays, scatter),
        scratch_shapes=_exchange_sems(nx) if nx else [],
        compiler_params=_cparams(*(("arbitrary", "arbitrary") if nx else ("parallel", "arbitrary"))),
    )(a, b, *arrays)
    return out if nx else out[0]


def _rel_index(dist):
    return np.clip(dist, -REL_CLIP, REL_CLIP) + REL_CLIP


def _bias_onehots():
    tw = 3 * LANES
    m = np.arange(ATT_VEC)
    dq = np.where(m <= ATT_KW, 512 - m, 512 - (m - ATT_VEC))
    dk = np.where(m < ATT_KW, m, m - ATT_VEC)
    ohq = np.zeros((tw, ATT_VEC), np.float32)
    ohk = np.zeros((tw, ATT_VEC), np.float32)
    ohq[_rel_index(dq), m] = 1.0
    ohk[_rel_index(dk), m] = 1.0
    return ohq, ohk


def _att_bias(table_pad):
    ohq, ohk = _bias_onehots()
    nslab = ATT_QB // SUBLANES

    def body(t_ref, ohq_ref, ohk_ref, bq_ref, bk_ref):
        tv = jnp.broadcast_to(t_ref[...], (SUBLANES, 3 * LANES))
        lane = lax.broadcasted_iota(jnp.int32, (ATT_QB, ATT_KW), 1)
        row = lax.broadcasted_iota(jnp.int32, (ATT_QB, ATT_KW), 0) // CHUNK
        col = lane // CHUNK
        band = (col >= row) & (col <= row + ATT_BAND - 1)
        for which, (oh_ref, out_ref) in enumerate(((ohq_ref, bq_ref), (ohk_ref, bk_ref))):
            vec = _dot(tv, oh_ref[...], HIGHEST)[0:1, :]
            slab = jnp.concatenate([vec if b == 0 else pltpu.roll(vec, b, 1) for b in range(SUBLANES)], axis=0)
            rows = [slab if a == 0 else pltpu.roll(slab, SUBLANES * a, 1) for a in range(nslab)]
            full = jnp.concatenate(rows, axis=0)[:, :ATT_KW]
            for v in range(3):
                inside = (lane >= (2 - v) * ATT_QB) if which == 0 else (lane < (v + 1) * ATT_QB)
                out_ref[v] = jnp.where(band & inside, full, NEG)

    h = table_pad.shape[0]
    oh_spec = pl.BlockSpec((3 * LANES, ATT_VEC), lambda i: (0, 0))
    out_spec = pl.BlockSpec((3, None, ATT_QB, ATT_KW), lambda i: (0, i, 0, 0))
    return pl.pallas_call(
        body, name="att_bias", grid=(h,),
        in_specs=[pl.BlockSpec((None, 1, 3 * LANES), lambda i: (i, 0, 0)), oh_spec, oh_spec],
        out_specs=[out_spec, out_spec],
        out_shape=[jax.ShapeDtypeStruct((3, h, ATT_QB, ATT_KW), F32)] * 2,
        compiler_params=_cparams("parallel"),
    )(table_pad, jnp.asarray(ohq), jnp.asarray(ohk))


def _head_masks():
    lane = lax.broadcasted_iota(jnp.int32, (1, LANES), 1)
    return [lane < ATT_DH, lane >= ATT_DH]


def _att_fwd(proj, bias_q):
    t = proj.shape[0]
    nb = t // ATT_QB
    scale = ATT_DH ** -0.5

    def body(q_ref, k0_ref, k1_ref, k2_ref, v0_ref, v1_ref, v2_ref, b_ref, o_ref, lse_ref, lset_ref):
        i = pl.program_id(0)
        q = (q_ref[...] * scale).astype(BF16)
        kk = jnp.concatenate([k0_ref[...], k1_ref[...], k2_ref[...]], axis=0).astype(BF16)
        vv = jnp.concatenate([v0_ref[...], v1_ref[...], v2_ref[...]], axis=0).astype(BF16)
        lane = lax.broadcasted_iota(jnp.int32, (1, LANES), 1)
        masks = _head_masks()
        lse_cols = jnp.zeros((ATT_QB, LANES), F32)
        for p in range(ATT_HEADS // 2):
            cs = slice(p * LANES, (p + 1) * LANES)
            qt, kt, vt = q[:, cs], kk[:, cs], vv[:, cs]
            acc = jnp.zeros((ATT_QB, LANES), F32)
            for sub in range(2):
                h = 2 * p + sub
                s = _dot_nt(jnp.where(masks[sub], qt, 0), kt) + b_ref[h]
                mx = jnp.max(s, axis=-1, keepdims=True)
                e = jnp.exp(s - mx)
                l = jnp.sum(e, axis=-1, keepdims=True)
                acc = acc + _dot(e.astype(BF16), jnp.where(masks[sub], vt, 0)) * (1.0 / l)
                lse_cols = lse_cols + jnp.where(lane == h, mx + jnp.log(l), 0.0)
            o_ref[:, cs] = acc.astype(BF16)
        lse_ref[...] = lse_cols
        lset_ref[...] = lse_cols.T[0:SUBLANES, :]

    def kv_spec(off, cb):
        return pl.BlockSpec((ATT_QB, PB), lambda i: (jnp.maximum(i + off, 0), cb))

    return pl.pallas_call(
        body, name="att_fwd", grid=(nb,),
        in_specs=[pl.BlockSpec((ATT_QB, PB), lambda i: (i, CB_QB)),
                  kv_spec(-2, CB_KB), kv_spec(-1, CB_KB), kv_spec(0, CB_KB),
                  kv_spec(-2, CB_VB), kv_spec(-1, CB_VB), kv_spec(0, CB_VB),
                  pl.BlockSpec((None, ATT_HEADS, ATT_QB, ATT_KW), lambda i: (jnp.minimum(i, 2), 0, 0, 0))],
        out_specs=[pl.BlockSpec((ATT_QB, WIDTH_B), lambda i: (i, 0)),
                   pl.BlockSpec((ATT_QB, LANES), lambda i: (i, 0)),
                   pl.BlockSpec((SUBLANES, ATT_QB), lambda i: (0, i))],
        out_shape=[jax.ShapeDtypeStruct((t, WIDTH_B), BF16), jax.ShapeDtypeStruct((t, LANES), F32),
                   jax.ShapeDtypeStruct((SUBLANES, t), F32)],
        compiler_params=_cparams("parallel"),
    )(proj, proj, proj, proj, proj, proj, proj, bias_q)


def _att_dq(proj, bias_q, lse, d_ob, ob, dproj):
    t = proj.shape[0]
    nb = t // ATT_QB
    scale = ATT_DH ** -0.5
    nslab = ATT_QB // SUBLANES

    def body(q_ref, k0_ref, k1_ref, k2_ref, v0_ref, v1_ref, v2_ref, b_ref, lse_ref, do_ref, o_ref, dp_in_ref,
             dq_ref, dlt_ref, slab_ref):
        i = pl.program_id(0)

        @pl.when(i == 0)
        def _():
            slab_ref[...] = jnp.zeros_like(slab_ref)

        q = (q_ref[...] * scale).astype(BF16)
        kk = jnp.concatenate([k0_ref[...], k1_ref[...], k2_ref[...]], axis=0).astype(BF16)
        vv = jnp.concatenate([v0_ref[...], v1_ref[...], v2_ref[...]], axis=0).astype(BF16)
        do = do_ref[...].astype(BF16)
        do_o = do_ref[...] * o_ref[...].astype(F32)
        lane = lax.broadcasted_iota(jnp.int32, (1, LANES), 1)
        masks = _head_masks()
        lse_all = lse_ref[...]
        dlt_cols = jnp.zeros((ATT_QB, LANES), F32)
        zpad = jnp.zeros((SUBLANES, ATT_VEC - ATT_KW), F32)
        for p in range(ATT_HEADS // 2):
            cs = slice(p * LANES, (p + 1) * LANES)
            qt, kt, vt, dot_ = q[:, cs], kk[:, cs], vv[:, cs], do[:, cs]
            acc = jnp.zeros((ATT_QB, LANES), F32)
            for sub in range(2):
                h = 2 * p + sub
                s = _dot_nt(jnp.where(masks[sub], qt, 0), kt) + b_ref[h]
                pr = jnp.exp(s - lse_all[:, h:h + 1])
                dp = _dot_nt(jnp.where(masks[sub], dot_, 0), vt)
                dl = jnp.sum(jnp.where(masks[sub], do_o[:, cs], 0.0), axis=-1, keepdims=True)
                ds = pr * (dp - dl)
                acc = acc + _dot(ds.astype(BF16), jnp.where(masks[sub], kt, 0)) * scale
                dlt_cols = dlt_cols + jnp.where(lane == h, dl, 0.0)
                sl = jnp.zeros((SUBLANES, ATT_VEC), F32)
                for a in range(nslab):
                    piece = jnp.concatenate([ds[a * SUBLANES:(a + 1) * SUBLANES, :], zpad], axis=1)
                    sl = sl + (piece if a == 0 else pltpu.roll(piece, ATT_VEC - SUBLANES * a, 1))
                slab_ref[h] += sl
            dq_ref[:, cs] = acc.astype(BF16)
        dlt_ref[...] = dlt_cols.T[0:SUBLANES, :]

    def kv_spec(off, cb):
        return pl.BlockSpec((ATT_QB, PB), lambda i: (jnp.maximum(i + off, 0), cb))

    return pl.pallas_call(
        body, name="att_dq", grid=(nb,),
        in_specs=[pl.BlockSpec((ATT_QB, PB), lambda i: (i, CB_QB)),
                  kv_spec(-2, CB_KB), kv_spec(-1, CB_KB), kv_spec(0, CB_KB),
                  kv_spec(-2, CB_VB), kv_spec(-1, CB_VB), kv_spec(0, CB_VB),
                  pl.BlockSpec((None, ATT_HEADS, ATT_QB, ATT_KW), lambda i: (jnp.minimum(i, 2), 0, 0, 0)),
                  pl.BlockSpec((ATT_QB, LANES), lambda i: (i, 0)),
                  pl.BlockSpec((ATT_QB, WIDTH_B), lambda i: (i, 0)), pl.BlockSpec((ATT_QB, WIDTH_B), lambda i: (i, 0)),
                  pl.BlockSpec(memory_space=pl.ANY)],
        out_specs=[_dp_spec(ATT_QB, DP_QB),
                   pl.BlockSpec((SUBLANES, ATT_QB), lambda i: (0, i)),
                   pl.BlockSpec((ATT_HEADS, SUBLANES, ATT_VEC), lambda i: (0, 0, 0))],
        out_shape=[jax.ShapeDtypeStruct(dproj.shape, dproj.dtype), jax.ShapeDtypeStruct((SUBLANES, t), F32),
                   jax.ShapeDtypeStruct((ATT_HEADS, SUBLANES, ATT_VEC), F32)],
        input_output_aliases={11: 0},
        compiler_params=_cparams("arbitrary"),
    )(proj, proj, proj, proj, proj, proj, proj, bias_q, lse, d_ob, ob, dproj)


def _att_dkv(proj, bias_k, lse_t, dlt_t, d_ob, dproj):
    t = proj.shape[0]
    nb = t // ATT_QB
    scale = ATT_DH ** -0.5

    def body(k_ref, v_ref, q0_ref, q1_ref, q2_ref, d0_ref, d1_ref, d2_ref, l0_ref, l1_ref, l2_ref,
             e0_ref, e1_ref, e2_ref, b_ref, dp_in_ref, dkv_ref):
        i = pl.program_id(0)
        k = k_ref[...].astype(BF16)
        v = v_ref[...].astype(BF16)
        qq = (jnp.concatenate([q0_ref[...], q1_ref[...], q2_ref[...]], axis=0) * scale).astype(BF16)
        do = jnp.concatenate([d0_ref[...], d1_ref[...], d2_ref[...]], axis=0).astype(BF16)
        lse = jnp.concatenate([l0_ref[...], l1_ref[...], l2_ref[...]], axis=1)
        dlt = jnp.concatenate([e0_ref[...], e1_ref[...], e2_ref[...]], axis=1)
        masks = _head_masks()
        for p in range(ATT_HEADS // 2):
            cs = slice(p * LANES, (p + 1) * LANES)
            kt, vt, qt, dot_ = k[:, cs], v[:, cs], qq[:, cs], do[:, cs]
            acc_k = jnp.zeros((ATT_QB, LANES), F32)
            acc_v = jnp.zeros((ATT_QB, LANES), F32)
            for sub in range(2):
                h = 2 * p + sub
                st = _dot_nt(jnp.where(masks[sub], kt, 0), qt) + b_ref[h]
                pt = jnp.exp(st - lse[h:h + 1, :])
                dot_m = jnp.where(masks[sub], dot_, 0)
                acc_v = acc_v + _dot(pt.astype(BF16), dot_m)
                dpt = _dot_nt(jnp.where(masks[sub], vt, 0), dot_)
                dst = pt * (dpt - dlt[h:h + 1, :])
                acc_k = acc_k + _dot(dst.astype(BF16), jnp.where(masks[sub], qt, 0))
            dkv_ref[:, cs] = acc_k.astype(BF16)
            dkv_ref[:, WIDTH_B + p * LANES:WIDTH_B + (p + 1) * LANES] = acc_v.astype(BF16)

    def q_spec(off, cb):
        return pl.BlockSpec((ATT_QB, PB), lambda i: (jnp.minimum(i + off, nb - 1), cb))

    def d_spec(off):
        return pl.BlockSpec((ATT_QB, WIDTH_B), lambda i: (jnp.minimum(i + off, nb - 1), 0))

    def r_spec(off):
        return pl.BlockSpec((SUBLANES, ATT_QB), lambda i: (0, jnp.minimum(i + off, nb - 1)))

    row = pl.BlockSpec((ATT_QB, WIDTH_B), lambda i: (i, 0))
    return pl.pallas_call(
        body, name="att_dkv", grid=(nb,),
        in_specs=[pl.BlockSpec((ATT_QB, PB), lambda i: (i, CB_KB)), pl.BlockSpec((ATT_QB, PB), lambda i: (i, CB_VB)),
                  q_spec(0, CB_QB), q_spec(1, CB_QB), q_spec(2, CB_QB),
                  d_spec(0), d_spec(1), d_spec(2), r_spec(0), r_spec(1), r_spec(2),
                  r_spec(0), r_spec(1), r_spec(2),
                  pl.BlockSpec((None, ATT_HEADS, ATT_QB, ATT_KW), lambda i: (jnp.minimum(nb - 1 - i, 2), 0, 0, 0)),
                  pl.BlockSpec(memory_space=pl.ANY)],
        out_specs=_dp_spec(ATT_QB, DP_KVB),
        out_shape=jax.ShapeDtypeStruct(dproj.shape, dproj.dtype),
        input_output_aliases={15: 0},
        compiler_params=_cparams("parallel"),
    )(proj, proj, proj, proj, proj, d_ob, d_ob, d_ob, lse_t, lse_t, lse_t, dlt_t, dlt_t, dlt_t, bias_k, dproj)


def _relbias_grad(slabs):
    ohq, _ = _bias_onehots()

    def body(s_ref, oh_ref, o_ref):
        sv = s_ref[...]
        vec = sv[0:1, :]
        for b in range(1, SUBLANES):
            vec = vec + pltpu.roll(sv[b:b + 1, :], ATT_VEC - b, 1)
        o_ref[...] = _dot_nt(jnp.broadcast_to(vec, (SUBLANES, ATT_VEC)), oh_ref[...], HIGHEST)[0:1, :]

    h = slabs.shape[0]
    return pl.pallas_call(
        body, name="att_dbias", grid=(h,),
        in_specs=[pl.BlockSpec((None, SUBLANES, ATT_VEC), lambda i: (i, 0, 0)),
                  pl.BlockSpec((3 * LANES, ATT_VEC), lambda i: (0, 0))],
        out_specs=pl.BlockSpec((None, 1, 3 * LANES), lambda i: (i, 0, 0)),
        out_shape=jax.ShapeDtypeStruct((h, 1, 3 * LANES), F32),
        compiler_params=_cparams("parallel"),
    )(slabs, jnp.asarray(ohq))


GDN_TM = 512
GDN_CB = 8
HALO = SUBLANES


def _conv_taps(ext, width, lead, n):
    return [(ext if k == width - 1 else pltpu.roll(ext, width - 1 - k, 0))[lead:lead + n] for k in range(width)]


def _prev_halo_spec(tm, width, cb):
    return pl.BlockSpec((HALO, width), lambda i: (jnp.maximum(i * (tm // HALO) - 1, 0), cb))


def _next_halo_spec(tm, width, cb, t):
    return pl.BlockSpec((HALO, width), lambda i: (jnp.minimum((i + 1) * (tm // HALO), t // HALO - 1), cb))


def _gdn_prep_fwd(proj, conv_w):
    t = proj.shape[0]
    tm = GDN_TM

    def body(q_ref, k_ref, v_ref, hq_ref, hk_ref, hv_ref, w_ref, qn_ref, kn_ref, vo_ref, y_ref):
        first = pl.program_id(0) == 0
        for idx, (x_ref, h_ref, o_ref) in enumerate(((q_ref, hq_ref, qn_ref), (k_ref, hk_ref, kn_ref),
                                                      (v_ref, hv_ref, vo_ref))):
            cs_all = slice(idx * KEY_A, (idx + 1) * KEY_A)
            halo = jnp.where(first, 0.0, h_ref[...])
            ext = jnp.concatenate([halo, x_ref[...]], axis=0)
            w = w_ref[:, cs_all]
            taps = _conv_taps(ext, GDN_CONV, HALO, tm)
            y = sum(w[k:k + 1, :] * taps[k] for k in range(GDN_CONV))
            y_ref[:, cs_all] = y
            a = y * _sigmoid(y)
            if idx < 2:
                for h in range(GDN_HEADS):
                    cs = slice(h * GDN_DK, (h + 1) * GDN_DK)
                    seg = a[:, cs]
                    o_ref[:, cs] = seg * lax.rsqrt(jnp.sum(seg * seg, axis=-1, keepdims=True) + EPS)
            else:
                o_ref[...] = a

    row = pl.BlockSpec((tm, KEY_A), lambda i: (i, 0))
    return pl.pallas_call(
        body, name="gdn_prep_fwd", grid=(t // tm,),
        in_specs=[pl.BlockSpec((tm, PB), lambda i: (i, CB_QA)), pl.BlockSpec((tm, PB), lambda i: (i, CB_KA)),
                  pl.BlockSpec((tm, PB), lambda i: (i, CB_VA)),
                  _prev_halo_spec(tm, PB, CB_QA), _prev_halo_spec(tm, PB, CB_KA), _prev_halo_spec(tm, PB, CB_VA),
                  pl.BlockSpec((GDN_CONV, 3 * KEY_A), lambda i: (0, 0))],
        out_specs=[row, row, row, pl.BlockSpec((tm, 3 * KEY_A), lambda i: (i, 0))],
        out_shape=[jax.ShapeDtypeStruct((t, KEY_A), F32)] * 3 + [jax.ShapeDtypeStruct((t, 3 * KEY_A), F32)],
        compiler_params=_cparams("parallel"),
    )(proj, proj, proj, proj, proj, proj, conv_w)


def _gdn_prep_bwd(proj, ycv, conv_w, dqn, dkn, dv, dproj):
    t = proj.shape[0]
    tm = GDN_TM
    nt = t // tm
    n_ext = tm + HALO

    def body(q_ref, k_ref, v_ref, y_ref, ny_ref, dq_ref, dk_ref, dv_ref, ndq_ref, ndk_ref, ndv_ref, w_ref, dp_in_ref,
             out_ref, dw_ref):
        i = pl.program_id(0)
        last = i == nt - 1

        @pl.when(i == 0)
        def _():
            dw_ref[...] = jnp.zeros_like(dw_ref)

        groups = ((q_ref, dq_ref, ndq_ref), (k_ref, dk_ref, ndk_ref), (v_ref, dv_ref, ndv_ref))
        for idx, (x_ref, d_ref, nd_ref) in enumerate(groups):
            cs_all = slice(idx * KEY_A, (idx + 1) * KEY_A)
            w = w_ref[:, cs_all]
            y = jnp.concatenate([y_ref[:, cs_all], jnp.where(last, 0.0, ny_ref[:, cs_all])], axis=0)
            sg = _sigmoid(y)
            a = y * sg
            dup = jnp.concatenate([d_ref[...], jnp.where(last, 0.0, nd_ref[...])], axis=0)
            if idx < 2:
                segs = []
                for h in range(GDN_HEADS):
                    cs = slice(h * GDN_DK, (h + 1) * GDN_DK)
                    seg = a[:, cs]
                    r = lax.rsqrt(jnp.sum(seg * seg, axis=-1, keepdims=True) + EPS)
                    nrm = seg * r
                    dn = dup[:, cs]
                    segs.append(r * (dn - nrm * jnp.sum(dn * nrm, axis=-1, keepdims=True)))
                da = jnp.concatenate(segs, axis=1)
            else:
                da = dup
            dy = da * sg * (1.0 + y * (1.0 - sg))
            xv = x_ref[...]
            dx = None
            for k in range(GDN_CONV):
                shift = GDN_CONV - 1 - k
                tap = (dy if shift == 0 else pltpu.roll(dy, n_ext - shift, 0))[:tm]
                term = w[k:k + 1, :] * tap
                dx = term if dx is None else dx + term
                dw_ref[k:k + 1, cs_all] += jnp.sum(tap * xv, axis=0, keepdims=True)
            out_ref[:, cs_all] = dx.astype(BF16)

    row = pl.BlockSpec((tm, KEY_A), lambda i: (i, 0))
    nrow = _next_halo_spec(tm, KEY_A, 0, t)
    return pl.pallas_call(
        body, name="gdn_prep_bwd", grid=(nt,),
        in_specs=[pl.BlockSpec((tm, PB), lambda i: (i, CB_QA)), pl.BlockSpec((tm, PB), lambda i: (i, CB_KA)),
                  pl.BlockSpec((tm, PB), lambda i: (i, CB_VA)),
                  pl.BlockSpec((tm, 3 * KEY_A), lambda i: (i, 0)), _next_halo_spec(tm, 3 * KEY_A, 0, t),
                  row, row, row, nrow, nrow, nrow,
                  pl.BlockSpec((GDN_CONV, 3 * KEY_A), lambda i: (0, 0)), pl.BlockSpec(memory_space=pl.ANY)],
        out_specs=[_dp_spec(tm, DP_QKVA), pl.BlockSpec((SUBLANES, 3 * KEY_A), lambda i: (0, 0))],
        out_shape=[jax.ShapeDtypeStruct(dproj.shape, dproj.dtype), jax.ShapeDtypeStruct((SUBLANES, 3 * KEY_A), F32)],
        input_output_aliases={12: 0},
        compiler_params=_cparams("arbitrary"),
    )(proj, proj, proj, ycv, ycv, dqn, dkn, dv, dqn, dkn, dv, conv_w, dproj)


class _Pair(dict):
    __getattr__ = dict.__getitem__
    __setattr__ = dict.__setitem__


def _pairs_to_lanes(cols):
    lane = lax.broadcasted_iota(jnp.int32, (1, LANES), 1)
    out = jnp.zeros((cols[0].shape[0], LANES), F32)
    for p, col in enumerate(cols):
        out = out + jnp.where(lane == p, col, 0.0)
    return out


def _gdn_terms(bd, par, kn_ref, qn_ref):
    c = CHUNK
    ii = lax.broadcasted_iota(jnp.int32, (c, c), 0)
    jj = lax.broadcasted_iota(jnp.int32, (c, c), 1)
    strict, incl = ii > jj, ii >= jj
    ltri = incl.astype(F32)
    ts = []
    for cc in range(GDN_CB):
        for h in range(GDN_HEADS):
            t = _Pair(cc=cc, h=h, rows=slice(cc * c, (cc + 1) * c), cs=slice(h * GDN_DK, (h + 1) * GDN_DK),
                      strict=strict, incl=incl)
            t.beta = _sigmoid(bd[t.rows, h:h + 1])
            t.ea = jnp.exp(par[0:1, h:h + 1])
            t.sp_arg = bd[t.rows, GDN_HEADS + h:GDN_HEADS + h + 1] + par[1:2, h:h + 1]
            t.g = -t.ea * _softplus(t.sp_arg)
            t.k = kn_ref[t.rows, t.cs]
            t.q = qn_ref[t.rows, t.cs] * (GDN_DK ** -0.5)
            t.kb, t.qb = t.k.astype(BF16), t.q.astype(BF16)
            ts.append(t)
    gall = _dot(ltri, _pairs_to_lanes([t.g for t in ts]), HIGHEST)
    gall_t = gall.T
    for p, t in enumerate(ts):
        t.gb = jnp.broadcast_to(gall[:, p:p + 1], (c, GDN_DK))
    for t in ts:
        t.kk = _dot_nt(t.kb, t.kb)
        t.qk = _dot_nt(t.qb, t.kb)
    for p, t in enumerate(ts):
        diff = t.gb[:, :c] - gall_t[p:p + 1, :]
        t.dec_s = jnp.exp(jnp.where(strict, diff, NEG))
        t.dec_i = jnp.exp(jnp.where(incl, diff, NEG))
        t.gam = jnp.exp(t.gb)
        glast = t.gb[c - 1:c, :]
        t.e_rest = jnp.exp(glast - t.gb)
        t.gl = jnp.exp(glast)
        t.p = t.qk * t.dec_i
    return ts


def _gdn_fwd(qn, kn, v, proj, par, gnw):
    t = qn.shape[0]
    c = CHUNK
    nc = t // c
    r_ = GDN_CB * c

    def body(qn_ref, kn_ref, v_ref, bd_ref, z_ref, par_ref, gnw_ref,
             oan_ref, o_ref, sp_ref, w_ref, u_ref, tm_ref, s_ref):
        @pl.when(pl.program_id(0) == 0)
        def _():
            s_ref[...] = jnp.zeros_like(s_ref)

        bd, par, gnw_v = bd_ref[...], par_ref[...], gnw_ref[...]
        eye = (lax.broadcasted_iota(jnp.int32, (c, c), 0) == lax.broadcasted_iota(jnp.int32, (c, c), 1)).astype(F32)
        ts = _gdn_terms(bd, par, kn_ref, qn_ref)
        for t in ts:
            t.vv = v_ref[t.rows, t.cs]
            t.x = -(t.beta * t.kk * t.dec_s)
            t.tinv = eye + t.x
        for t in ts:
            t.xs = _split(t.x)
        for _ in range(5):
            for t in ts:
                t.xs = _split(_dot3s(t.xs, t.xs))
            for t in ts:
                t.tinv = t.tinv + _dot3s(_split(t.tinv), t.xs)
        for t in ts:
            tsp = _split(t.tinv)
            t.wm = _dot3s(tsp, _split((t.beta * t.gam) * t.k))
            t.uv = _dot3s(tsp, _split(t.beta * t.vv))
        for t in ts:
            w_ref[t.rows, t.cs] = t.wm
            tm_ref[t.cc, t.h] = t.tinv.T
            t.wb = t.wm.astype(BF16)
            t.qgb = (t.q * t.gam).astype(BF16)
            t.kdb = (t.k * t.e_rest).astype(BF16)
            t.pb = t.p.astype(BF16)
        state = [s_ref[h] for h in range(GDN_HEADS)]
        for cc in range(GDN_CB):
            tc = [t for t in ts if t.cc == cc]
            for t in tc:
                t.sh = state[t.h]
                t.sb = t.sh.astype(BF16)
            for t in tc:
                t.ws = _dot(t.wb, t.sb)
            for t in tc:
                t.u = t.uv - t.ws
                t.ub = t.u.astype(BF16)
            for t in tc:
                state[t.h] = t.gl * t.sh + _dot_tn(t.kdb, t.ub)
            for t in tc:
                t.o = _dot(t.qgb, t.sb) + _dot(t.pb, t.ub)
                sp_ref[cc, t.h] = t.sh
                u_ref[t.rows, t.cs] = t.u
                o_ref[t.rows, t.cs] = t.o
        for h in range(GDN_HEADS):
            s_ref[h] = state[h]
        for t in ts:
            zz = z_ref[t.rows, t.cs]
            rr = lax.rsqrt(jnp.mean(t.o * t.o, axis=-1, keepdims=True) + EPS)
            oan_ref[t.rows, t.cs] = ((t.o * rr) * gnw_v * (zz * _sigmoid(zz))).astype(BF16)

    row = pl.BlockSpec((r_, KEY_A), lambda i: (i, 0))
    return pl.pallas_call(
        body, name="gdn_fwd", grid=(nc // GDN_CB,),
        in_specs=[row, row, row, pl.BlockSpec((r_, LANES), lambda i: (i, CB_BD)),
                  pl.BlockSpec((r_, PB), lambda i: (i, CB_ZA)),
                  pl.BlockSpec((SUBLANES, LANES), lambda i: (0, 0)), pl.BlockSpec((1, GDN_DK), lambda i: (0, 0))],
        out_specs=[row, row, pl.BlockSpec((GDN_CB, GDN_HEADS, GDN_DK, GDN_DK), lambda i: (i, 0, 0, 0)),
                   row, row, pl.BlockSpec((GDN_CB, GDN_HEADS, c, c), lambda i: (i, 0, 0, 0))],
        out_shape=[jax.ShapeDtypeStruct((t, KEY_A), BF16), jax.ShapeDtypeStruct((t, KEY_A), F32),
                   jax.ShapeDtypeStruct((nc, GDN_HEADS, GDN_DK, GDN_DK), F32),
                   jax.ShapeDtypeStruct((t, KEY_A), F32), jax.ShapeDtypeStruct((t, KEY_A), F32),
                   jax.ShapeDtypeStruct((nc, GDN_HEADS, c, c), F32)],
        scratch_shapes=[pltpu.VMEM((GDN_HEADS, GDN_DK, GDN_DK), F32)],
        compiler_params=_cparams("arbitrary"),
    )(qn, kn, v, proj, proj, par, gnw)


def _gdn_bwd(qn, kn, v, proj, par, gnw, o, sprev, wst, ust, tst, d_oan, dproj):
    t = qn.shape[0]
    c = CHUNK
    nc = t // c
    nb = nc // GDN_CB
    r_ = GDN_CB * c

    def body(qn_ref, kn_ref, v_ref, bd_ref, z_ref, par_ref, gnw_ref, o_ref, sp_ref, w_ref, u_ref, tm_ref, do_ref,
             dp_in_ref, dqn_ref, dkn_ref, dv_ref, dzb_ref, acc_ref, ds_ref):
        @pl.when(pl.program_id(0) == 0)
        def _():
            ds_ref[...] = jnp.zeros_like(ds_ref)
            acc_ref[...] = jnp.zeros_like(acc_ref)

        bd, par, gnw_v = bd_ref[...], par_ref[...], gnw_ref[...]
        lane = lax.broadcasted_iota(jnp.int32, (1, LANES), 1)
        rix = lax.broadcasted_iota(jnp.int32, (c, 1), 0)
        ii = lax.broadcasted_iota(jnp.int32, (c, c), 0)
        jj = lax.broadcasted_iota(jnp.int32, (c, c), 1)
        upper = (jj >= ii).astype(F32)
        acc_a = jnp.zeros((1, LANES), F32)
        acc_d = jnp.zeros((1, LANES), F32)
        acc_g = jnp.zeros((1, LANES), F32)
        ts = _gdn_terms(bd, par, kn_ref, qn_ref)
        for t in ts:
            t.vv = v_ref[t.rows, t.cs]
            t.sh = sp_ref[t.cc, t.h]
            t.sb = t.sh.astype(BF16)
            t.wm, t.u, t.tinv_t = w_ref[t.rows, t.cs], u_ref[t.rows, t.cs], tm_ref[t.cc, t.h]
            t.wb, t.ub = t.wm.astype(BF16), t.u.astype(BF16)
            ov, zz, dout = o_ref[t.rows, t.cs], z_ref[t.rows, t.cs], do_ref[t.rows, t.cs]
            sg = _sigmoid(zz)
            sil = zz * sg
            rr = lax.rsqrt(jnp.mean(ov * ov, axis=-1, keepdims=True) + EPS)
            on = ov * rr
            dzb_ref[t.rows, t.cs] = (dout * on * gnw_v * (sg * (1.0 + zz * (1.0 - sg)))).astype(BF16)
            acc_g = acc_g + jnp.sum(dout * on * sil, axis=0, keepdims=True)
            don = dout * gnw_v * sil
            t.dob = (rr * (don - on * jnp.mean(don * on, axis=-1, keepdims=True))).astype(BF16)
            t.qg = t.q * t.gam
            t.kd = t.k * t.e_rest
            t.qgb, t.kdb = t.qg.astype(BF16), t.kd.astype(BF16)
            t.ptb = t.p.T.astype(BF16)
        for t in ts:
            t.du0 = _dot(t.ptb, t.dob)
            t.ds0 = _dot_tn(t.qgb, t.dob)
            t.dqg = _dot_nt(t.dob, t.sb)
            t.dp = _dot_nt(t.dob, t.ub)
            t.uv = t.u + _dot(t.wb, t.sb)
        dstate = [ds_ref[h] for h in range(GDN_HEADS)]
        for cc in reversed(range(GDN_CB)):
            tc = [t for t in ts if t.cc == cc]
            for t in tc:
                t.dsn = dstate[t.h]
                t.dsnb = t.dsn.astype(BF16)
            for t in tc:
                t.du = t.du0 + _dot(t.kdb, t.dsnb)
            for t in tc:
                t.dub = t.du.astype(BF16)
            for t in tc:
                dstate[t.h] = t.gl * t.dsn + t.ds0 - _dot_tn(t.wb, t.dub)
            for t in tc:
                t.dkd = _dot_nt(t.ub, t.dsnb)
                t.dgl = jnp.sum(jnp.sum(t.dsn * t.sh, axis=1, keepdims=True), axis=0, keepdims=True)
                t.dwm = -_dot_nt(t.dub, t.sb)
        for h in range(GDN_HEADS):
            ds_ref[h] = dstate[h]
        for t in ts:
            tsp = _split(t.tinv_t)
            t.dbk = _dot3s(tsp, _split(t.dwm))
            t.dbv = _dot3s(tsp, _split(t.du))
        for t in ts:
            d_a = -(_dot_nt(t.dbk.astype(BF16), t.wb) + _dot_nt(t.dbv.astype(BF16), t.uv.astype(BF16)))
            t.d_a = jnp.where(t.strict, d_a, 0.0)
        for t in ts:
            t.dkk = t.d_a * t.beta * t.dec_s
            t.dqk = t.dp * t.dec_i
            t.dqkb = t.dqk.astype(BF16)
        for t in ts:
            t.dq = _dot(t.dqkb, t.kb) + t.dqg * t.gam
            t.dk = (t.dbk * (t.beta * t.gam) + _dot_tn(t.dqkb, t.qb) + _dot((t.dkk + t.dkk.T).astype(BF16), t.kb)
                    + t.dkd * t.e_rest)
        for t in ts:
            dbeta = (jnp.sum(t.d_a * t.kk * t.dec_s, axis=-1, keepdims=True)
                     + jnp.sum(t.dbk * t.k * t.gam, axis=-1, keepdims=True) + jnp.sum(t.dbv * t.vv, axis=-1, keepdims=True))
            t.dbl = dbeta * t.beta * (1.0 - t.beta)
            dv_ref[t.rows, t.cs] = t.dbv * t.beta
            bk = (t.beta * t.gam) * t.k
            zc = jnp.sum(t.dkd * t.kd, axis=-1, keepdims=True)
            xs = t.dkk * t.kk + t.dp * t.p
            dgc = (jnp.sum(xs, axis=-1, keepdims=True) - jnp.sum(xs.T, axis=-1, keepdims=True)
                   + jnp.sum(t.dbk * bk, axis=-1, keepdims=True) + jnp.sum(t.dqg * t.qg, axis=-1, keepdims=True) - zc)
            dglast = jnp.sum(zc, axis=0, keepdims=True) + t.dgl * t.gl[:, 0:1]
            t.dgc = dgc + jnp.where(rix == c - 1, dglast, 0.0)
        dgall = _dot(upper, _pairs_to_lanes([t.dgc for t in ts]), HIGHEST)
        for p, t in enumerate(ts):
            t.dg = dgall[:, p:p + 1]
        dbd_tiles = [jnp.zeros((c, LANES), F32) for _ in range(GDN_CB)]
        for t in ts:
            ddl = t.dg * (-t.ea) * _sigmoid(t.sp_arg)
            acc_a = acc_a + jnp.where(lane == t.h, jnp.sum(t.dg * t.g, axis=0, keepdims=True), 0.0)
            acc_d = acc_d + jnp.where(lane == t.h, jnp.sum(ddl, axis=0, keepdims=True), 0.0)
            dbd_tiles[t.cc] = (dbd_tiles[t.cc] + jnp.where(lane == t.h, t.dbl, 0.0)
                               + jnp.where(lane == GDN_HEADS + t.h, ddl, 0.0))
            dqn_ref[t.rows, t.cs] = t.dq * (GDN_DK ** -0.5)
            dkn_ref[t.rows, t.cs] = t.dk
        for cc in range(GDN_CB):
            dzb_ref[cc * c:(cc + 1) * c, KEY_A:KEY_A + LANES] = dbd_tiles[cc].astype(BF16)
        acc_ref[0:1, :] += acc_a
        acc_ref[1:2, :] += acc_d
        acc_ref[2:3, :] += acc_g

    def rev(i):
        return nb - 1 - i

    row = pl.BlockSpec((r_, KEY_A), lambda i: (rev(i), 0))
    st = pl.BlockSpec((GDN_CB, GDN_HEADS, GDN_DK, GDN_DK), lambda i: (rev(i), 0, 0, 0))
    tt_spec = pl.BlockSpec((GDN_CB, GDN_HEADS, c, c), lambda i: (rev(i), 0, 0, 0))
    return pl.pallas_call(
        body, name="gdn_bwd", grid=(nb,),
        in_specs=[row, row, row, pl.BlockSpec((r_, LANES), lambda i: (rev(i), CB_BD)),
                  pl.BlockSpec((r_, PB), lambda i: (rev(i), CB_ZA)),
                  pl.BlockSpec((SUBLANES, LANES), lambda i: (0, 0)), pl.BlockSpec((1, GDN_DK), lambda i: (0, 0)),
                  row, st, row, row, tt_spec, row, pl.BlockSpec(memory_space=pl.ANY)],
        out_specs=[row, row, row, _dp_spec(r_, DP_ZBD, rev), pl.BlockSpec((SUBLANES, LANES), lambda i: (0, 0))],
        out_shape=[jax.ShapeDtypeStruct((t, KEY_A), F32)] * 3 + [jax.ShapeDtypeStruct(dproj.shape, dproj.dtype),
                                                                jax.ShapeDtypeStruct((SUBLANES, LANES), F32)],
        input_output_aliases={13: 3},
        scratch_shapes=[pltpu.VMEM((GDN_HEADS, GDN_DK, GDN_DK), F32)],
        compiler_params=_cparams("arbitrary"),
    )(qn, kn, v, proj, proj, par, gnw, o, sprev, wst, ust, tst, d_oan, dproj)


def _merge_fwd(oan, ob, proj, x, wba, wbb, wout, w2, tm=512):
    t = x.shape[0]

    def body(oa_ref, ob_ref, ga_ref, gb_ref, x_ref, wba_ref, wbb_ref, wout_ref, w2_ref, x2_ref, h2_ref):
        ya = _dot(oa_ref[...], wba_ref[...])
        yb = _dot(ob_ref[...], wbb_ref[...])
        mix = _sigmoid(ga_ref[...]) * ya + _sigmoid(gb_ref[...]) * yb
        x2 = x_ref[...] + _dot(mix.astype(BF16), wout_ref[...])
        x2_ref[...] = x2
        r = lax.rsqrt(jnp.mean(x2 * x2, axis=-1, keepdims=True) + EPS)
        h2_ref[...] = (x2 * r * w2_ref[...]).astype(BF16)

    half = pl.BlockSpec((tm, KEY_A), lambda i: (i, 0))
    row = pl.BlockSpec((tm, D_MODEL), lambda i: (i, 0))
    wsmall = pl.BlockSpec((KEY_A, D_MODEL), lambda i: (0, 0))
    return pl.pallas_call(
        body, name="merge_fwd", grid=(t // tm,),
        in_specs=[half, half, pl.BlockSpec((tm, D_MODEL), lambda i: (i, CB_GA)),
                  pl.BlockSpec((tm, D_MODEL), lambda i: (i, CB_GB)), row, wsmall, wsmall,
                  pl.BlockSpec((D_MODEL, D_MODEL), lambda i: (0, 0)), pl.BlockSpec((1, D_MODEL), lambda i: (0, 0))],
        out_specs=[row, row],
        out_shape=[jax.ShapeDtypeStruct((t, D_MODEL), F32), jax.ShapeDtypeStruct((t, D_MODEL), BF16)],
        compiler_params=_cparams("parallel"),
    )(oan, ob, proj, proj, x, wba, wbb, wout, w2)


def _merge_bwd(dx2b, oan, ob, proj, wba, wbb, wout_t, wba_t, wbb_t, tm=512):
    t = dx2b.shape[0]

    def body(dx_ref, oa_ref, ob_ref, ga_ref, gb_ref, wba_ref, wbb_ref, woutt_ref, wbat_ref, wbbt_ref,
             dg_ref, doa_ref, dob_ref, gout_ref, gba_ref, gbb_ref):
        @pl.when(pl.program_id(0) == 0)
        def _():
            gout_ref[...] = jnp.zeros_like(gout_ref)
            gba_ref[...] = jnp.zeros_like(gba_ref)
            gbb_ref[...] = jnp.zeros_like(gbb_ref)

        dx, oa, ob = dx_ref[...], oa_ref[...], ob_ref[...]
        dmix = _dot(dx, woutt_ref[...])
        ya = _dot(oa, wba_ref[...])
        yb = _dot(ob, wbb_ref[...])
        sa, sb = _sigmoid(ga_ref[...]), _sigmoid(gb_ref[...])
        gout_ref[...] += _dot_tn((sa * ya + sb * yb).astype(BF16), dx)
        dg_ref[:, :D_MODEL] = (dmix * ya * sa * (1.0 - sa)).astype(BF16)
        dg_ref[:, D_MODEL:] = (dmix * yb * sb * (1.0 - sb)).astype(BF16)
        dya = (dmix * sa).astype(BF16)
        dyb = (dmix * sb).astype(BF16)
        gba_ref[...] += _dot_tn(oa, dya)
        gbb_ref[...] += _dot_tn(ob, dyb)
        doa_ref[...] = _dot(dya, wbat_ref[...])
        dob_ref[...] = _dot(dyb, wbbt_ref[...])

    half = pl.BlockSpec((tm, KEY_A), lambda i: (i, 0))
    row = pl.BlockSpec((tm, D_MODEL), lambda i: (i, 0))
    wsmall = pl.BlockSpec((KEY_A, D_MODEL), lambda i: (0, 0))
    wsmall_t = pl.BlockSpec((D_MODEL, KEY_A), lambda i: (0, 0))
    wfull = pl.BlockSpec((D_MODEL, D_MODEL), lambda i: (0, 0))
    return pl.pallas_call(
        body, name="merge_bwd", grid=(t // tm,),
        in_specs=[row, half, half, pl.BlockSpec((tm, D_MODEL), lambda i: (i, CB_GA)),
                  pl.BlockSpec((tm, D_MODEL), lambda i: (i, CB_GB)), wsmall, wsmall, wfull, wsmall_t, wsmall_t],
        out_specs=[_dp_spec(tm, DP_GATES), half, half, wfull, wsmall, wsmall],
        out_shape=[jax.ShapeDtypeStruct((t, PROJ_W), BF16), jax.ShapeDtypeStruct((t, KEY_A), F32),
                   jax.ShapeDtypeStruct((t, KEY_A), F32), jax.ShapeDtypeStruct((D_MODEL, D_MODEL), F32),
                   jax.ShapeDtypeStruct((KEY_A, D_MODEL), F32), jax.ShapeDtypeStruct((WIDTH_B, D_MODEL), F32)],
        compiler_params=_cparams("arbitrary"),
    )(dx2b, oan, ob, proj, proj, wba, wbb, wout_t, wba_t, wbb_t)


FFN_TM = 128
FFN_W = 2 * D_FF


def _resident(shape):
    return pl.BlockSpec(shape, lambda i: (0,) * len(shape), pipeline_mode=pl.Buffered(1))


def _ffn_fwd(h2, wup, cw, cb, wdown, x2, tgt, w3):
    t = x2.shape[0]
    tm = FFN_TM

    def body(h2_ref, wup_ref, cw_ref, cb_ref, wd_ref, x2_ref, tgt_ref, w3_ref, up_ref, u_ref, dx_ref, dxb_ref, act_ref,
             acc_ref, prev_ref):
        @pl.when(pl.program_id(0) == 0)
        def _():
            acc_ref[...] = jnp.zeros_like(acc_ref)
            prev_ref[...] = jnp.zeros_like(prev_ref)

        up = _dot(h2_ref[...], wup_ref[...])
        up_ref[...] = up
        ext = jnp.concatenate([prev_ref[...], up], axis=0)
        prev_ref[...] = up[tm - HALO:tm]
        taps = _conv_taps(ext, FFN_CONV, HALO, tm)
        cw_v = cw_ref[...]
        u = sum(cw_v[k:k + 1, :] * taps[k] for k in range(FFN_CONV)) + cb_ref[...]
        u_ref[...] = u
        gate, upp = u[:, :D_FF], u[:, D_FF:]
        act = (gate * _sigmoid(gate) * upp).astype(BF16)
        act_ref[...] = act
        x3 = x2_ref[...] + _dot(act, wd_ref[...])
        r = lax.rsqrt(jnp.mean(x3 * x3, axis=-1, keepdims=True) + EPS)
        xh = x3 * r
        w3v = w3_ref[...]
        err = xh * w3v - tgt_ref[...]
        loss = 0.5 * jnp.sum(jnp.mean(err * err, axis=-1, keepdims=True), axis=0, keepdims=True)
        dy = err * (1.0 / D_MODEL)
        acc_ref[0:1, :] += jnp.sum(dy * xh, axis=0, keepdims=True)
        acc_ref[1:2, :] += jnp.broadcast_to(loss, (1, D_MODEL))
        dxh = dy * w3v
        dx = r * (dxh - xh * jnp.mean(dxh * xh, axis=-1, keepdims=True))
        dx_ref[...] = dx
        dxb_ref[...] = dx.astype(BF16)

    row = pl.BlockSpec((tm, D_MODEL), lambda i: (i, 0))
    return pl.pallas_call(
        body, name="ffn_fwd", grid=(t // tm,),
        in_specs=[row, _resident((D_MODEL, FFN_W)), _resident((SUBLANES, FFN_W)), _resident((1, FFN_W)),
                  _resident((D_FF, D_MODEL)), row, row, _resident((1, D_MODEL))],
        out_specs=[pl.BlockSpec((tm, FFN_W), lambda i: (i, 0)), pl.BlockSpec((tm, FFN_W), lambda i: (i, 0)), row, row,
                   pl.BlockSpec((tm, D_FF), lambda i: (i, 0)), pl.BlockSpec((SUBLANES, D_MODEL), lambda i: (0, 0))],
        out_shape=[jax.ShapeDtypeStruct((t, FFN_W), F32), jax.ShapeDtypeStruct((t, FFN_W), F32),
                   jax.ShapeDtypeStruct((t, D_MODEL), F32),
                   jax.ShapeDtypeStruct((t, D_MODEL), BF16), jax.ShapeDtypeStruct((t, D_FF), BF16),
                   jax.ShapeDtypeStruct((SUBLANES, D_MODEL), F32)],
        scratch_shapes=[pltpu.VMEM((HALO, FFN_W), F32)],
        compiler_params=_cparams("arbitrary"),
    )(h2, wup, cw, cb, wdown, x2, tgt, w3)


def _ffn_bwd(dx3b, wdown_t, up, u, cw, wup_t, x2, w2, dx3, carry):
    t = up.shape[0]
    tm = FFN_TM
    nt = t // tm
    n_ext = tm + HALO
    arrays, scatter = carry
    nx = len(arrays)

    def rev(i):
        return nt - 1 - i

    def body(*refs):
        dx_ref, wdt_ref, up_ref, u_ref, cw_ref, wupt_ref, x2_ref, w2_ref, dres_ref = refs[:9]
        srcs = refs[9:9 + nx]
        dup_ref, acc_ref, dx2_ref, dx2b_ref, dw2_ref = refs[9 + nx:14 + nx]
        dsts = refs[14 + nx:14 + 2 * nx]
        nxt_ref = refs[14 + 2 * nx]
        i = pl.program_id(0)
        local, remote = _exchange_copies(srcs, dsts, scatter, *refs[15 + 2 * nx:])

        @pl.when(i == 0)
        def _():
            for cp in local + remote:
                cp.start()
            acc_ref[...] = jnp.zeros_like(acc_ref)
            dw2_ref[...] = jnp.zeros_like(dw2_ref)
            nxt_ref[...] = jnp.zeros_like(nxt_ref)

        dact = _dot(dx_ref[...], wdt_ref[...])
        gate, upp = u_ref[:, :D_FF], u_ref[:, D_FF:]
        sg = _sigmoid(gate)
        du = jnp.concatenate([dact * upp * (sg * (1.0 + gate * (1.0 - sg))), dact * (gate * sg)], axis=1)
        acc_ref[FFN_CONV:FFN_CONV + 1, :] += jnp.sum(du, axis=0, keepdims=True)
        ext = jnp.concatenate([du, nxt_ref[...]], axis=0)
        cw_v = cw_ref[...]
        upv = up_ref[...]
        dup = None
        for k in range(FFN_CONV):
            shift = FFN_CONV - 1 - k
            tap = du if shift == 0 else pltpu.roll(ext, n_ext - shift, 0)[:tm]
            term = cw_v[k:k + 1, :] * tap
            dup = term if dup is None else dup + term
            acc_ref[k:k + 1, :] += jnp.sum(tap * upv, axis=0, keepdims=True)
        dupb = dup.astype(BF16)
        dup_ref[...] = dupb
        nxt_ref[...] = du[0:HALO]
        dhv = _dot(dupb, wupt_ref[...])
        xv = x2_ref[...]
        r = lax.rsqrt(jnp.mean(xv * xv, axis=-1, keepdims=True) + EPS)
        xh = xv * r
        dw2_ref[0:1, :] += jnp.sum(dhv * xh, axis=0, keepdims=True)
        dxh = dhv * w2_ref[...]
        dx2 = dres_ref[...] + r * (dxh - xh * jnp.mean(dxh * xh, axis=-1, keepdims=True))
        dx2_ref[...] = dx2
        dx2b_ref[...] = dx2.astype(BF16)

        @pl.when(i == nt - 1)
        def _():
            for cp in remote + local:
                cp.wait()

    wide = pl.BlockSpec((tm, FFN_W), lambda i: (rev(i), 0))
    row = pl.BlockSpec((tm, D_MODEL), lambda i: (rev(i), 0))
    any_spec = pl.BlockSpec(memory_space=pl.ANY)
    return pl.pallas_call(
        body, name="ffn_bwd", grid=(nt,),
        in_specs=[row, _resident((D_MODEL, D_FF)), wide, wide,
                  _resident((SUBLANES, FFN_W)), _resident((FFN_W, D_MODEL)), row,
                  _resident((1, D_MODEL)), row] + [any_spec] * nx,
        out_specs=[wide, pl.BlockSpec((SUBLANES, FFN_W), lambda i: (0, 0)), row, row,
                   pl.BlockSpec((SUBLANES, D_MODEL), lambda i: (0, 0))] + [any_spec] * nx,
        out_shape=[jax.ShapeDtypeStruct((t, FFN_W), BF16), jax.ShapeDtypeStruct((SUBLANES, FFN_W), F32),
                   jax.ShapeDtypeStruct((t, D_MODEL), F32), jax.ShapeDtypeStruct((t, D_MODEL), BF16),
                   jax.ShapeDtypeStruct((SUBLANES, D_MODEL), F32)] + _exchange_shapes(arrays, scatter),
        scratch_shapes=[pltpu.VMEM((HALO, FFN_W), F32)] + _exchange_sems(nx),
        compiler_params=_cparams("arbitrary"),
    )(dx3b, wdown_t, up, u, cw, wup_t, x2, w2, dx3, *arrays)


def _adamw(parts, w, m, v, name, tr):
    r, cols = w.shape

    def body(p_ref, w_ref, m_ref, v_ref, g_ref, d_ref, mo_ref, vo_ref):
        g = p_ref[0].astype(F32)
        for s in range(1, N_DEV):
            g = g + p_ref[s].astype(F32)
        mm = ADAM_B1 * m_ref[...] + (1.0 - ADAM_B1) * g
        vv = ADAM_B2 * v_ref[...] + (1.0 - ADAM_B2) * (g * g)
        m_hat = mm / (1.0 - ADAM_B1 ** ADAM_STEP)
        v_hat = vv / (1.0 - ADAM_B2 ** ADAM_STEP)
        g_ref[...] = g
        d_ref[...] = -ADAM_LR * (m_hat / (jnp.sqrt(v_hat) + ADAM_EPS) + ADAM_WD * w_ref[...])
        mo_ref[...] = mm
        vo_ref[...] = vv

    assert r % tr == 0
    row = pl.BlockSpec((tr, cols), lambda i: (i, 0))
    return pl.pallas_call(
        body, name=name, grid=(r // tr,),
        in_specs=[pl.BlockSpec((N_DEV, tr, cols), lambda i: (0, i, 0)), row, row, row],
        out_specs=[row, row, row, row],
        out_shape=[jax.ShapeDtypeStruct((r, cols), F32)] * 4,
        compiler_params=_cparams("parallel"),
    )(parts, w, m, v)


def _mesh_pos():
    return lax.axis_index("x"), lax.axis_index("y"), lax.axis_index("c")


def _peer(pos, k):
    x, y, c = pos
    return (x ^ ((k >> 2) & 1), y ^ ((k >> 1) & 1), c ^ (k & 1))


def _flat_id(pos):
    return 4 * pos[0] + 2 * pos[1] + pos[2]


def _exchange_copies(srcs, dsts, scatter, send_sems, recv_sems, loc_sems):
    pos = _mesh_pos()
    me = _flat_id(pos)
    local, remote = [], []
    for j, (src, dst) in enumerate(zip(srcs, dsts)):
        local.append(pltpu.make_async_copy(src.at[me] if scatter[j] else src, dst.at[me], loc_sems.at[j]))
        for k in range(1, N_DEV):
            to = _peer(pos, k)
            remote.append(pltpu.make_async_remote_copy(
                src_ref=src.at[_flat_id(to)] if scatter[j] else src, dst_ref=dst.at[me],
                send_sem=send_sems.at[j, k - 1], recv_sem=recv_sems.at[j, k - 1],
                device_id=to, device_id_type=pl.DeviceIdType.MESH))
    return local, remote


def _exchange_shapes(arrays, scatter):
    return [jax.ShapeDtypeStruct(a.shape if s else (N_DEV,) + a.shape, a.dtype) for a, s in zip(arrays, scatter)]


def _exchange_sems(n):
    return [pltpu.SemaphoreType.DMA((n, N_DEV - 1)), pltpu.SemaphoreType.DMA((n, N_DEV - 1)), pltpu.SemaphoreType.DMA((n,))]


def _exchange(arrays, scatter, name):
    n = len(arrays)
    any_spec = pl.BlockSpec(memory_space=pl.ANY)

    def body(*refs):
        local, remote = _exchange_copies(refs[:n], refs[n:2 * n], scatter, *refs[2 * n:])
        for cp in local + remote:
            cp.start()
        for cp in remote:
            cp.wait()
        for cp in local:
            cp.wait()

    return pl.pallas_call(
        body, name=name, in_specs=[any_spec] * n, out_specs=[any_spec] * n,
        out_shape=_exchange_shapes(arrays, scatter), scratch_shapes=_exchange_sems(n),
    )(*arrays)


def _pad_rows(a, rows):
    return jnp.pad(a, ((0, rows - a.shape[0]),) + ((0, 0),) * (a.ndim - 1))


PACK_UNIT = SUBLANES * LANES


def _pack_lanes(parts, rows):
    out = []
    for a in parts:
        f = a.reshape(-1)
        out.append(jnp.pad(f, (0, (-f.shape[0]) % PACK_UNIT)).reshape(-1, LANES))
    packed = jnp.concatenate(out, axis=0)
    assert packed.shape[0] == rows, (packed.shape, rows)
    return packed


def _unpack_lanes(buf, shapes):
    out, r0 = [], 0
    for shp in shapes:
        n = math.prod(shp)
        nr = -(-n // PACK_UNIT) * SUBLANES
        out.append(buf[r0:r0 + nr].reshape(-1)[:n].reshape(shp))
        r0 += nr
    return out


def _col_shards(g):
    r, n = g.shape
    return g.reshape(r, N_DEV, n // N_DEV).transpose(1, 0, 2)


def _col_unshard(s):
    _, r, w = s.shape
    return s.transpose(1, 0, 2).reshape(r, N_DEV * w)


def _lane_rows(flat):
    n = flat.shape[1]
    return jnp.pad(flat, ((0, 0), (0, (-n) % PACK_UNIT))).reshape(N_DEV, -1, LANES)


SMALL_ROWS = 128
WS_ROWS = 32


def kernel(x, norm_mix_w, w_in, conv_qkv_w, a_log, dt_bias, gdn_norm_w, w_branch_a, w_branch_b, rel_bias, w_out, norm_ffn_w, w_up, conv_ffn_w, conv_ffn_b, w_down, norm_final_w, loss_target, m_norm_mix_w, m_w_in, m_conv_qkv_w, m_a_log, m_dt_bias, m_gdn_norm_w, m_w_branch_a, m_w_branch_b, m_rel_bias, m_w_out, m_norm_ffn_w, m_w_up, m_conv_ffn_w, m_conv_ffn_b, m_w_down, m_norm_final_w, v_norm_mix_w, v_w_in, v_conv_qkv_w, v_a_log, v_dt_bias, v_gdn_norm_w, v_w_branch_a, v_w_branch_b, v_rel_bias, v_w_out, v_norm_ffn_w, v_w_up, v_conv_ffn_w, v_conv_ffn_b, v_w_down, v_norm_final_w):
    big_w = (w_in, w_branch_a, w_branch_b, w_out, w_up, w_down, conv_qkv_w, conv_ffn_w)
    big_m = (m_w_in, m_w_branch_a, m_w_branch_b, m_w_out, m_w_up, m_w_down, m_conv_qkv_w, m_conv_ffn_w)
    big_v = (v_w_in, v_w_branch_a, v_w_branch_b, v_w_out, v_w_up, v_w_down, v_conv_qkv_w, v_conv_ffn_w)
    small_w = (norm_mix_w, a_log, dt_bias, gdn_norm_w, rel_bias, norm_ffn_w, conv_ffn_b, norm_final_w)
    small_m = (m_norm_mix_w, m_a_log, m_dt_bias, m_gdn_norm_w, m_rel_bias, m_norm_ffn_w, m_conv_ffn_b, m_norm_final_w)
    small_v = (v_norm_mix_w, v_a_log, v_dt_bias, v_gdn_norm_w, v_rel_bias, v_norm_ffn_w, v_conv_ffn_b, v_norm_final_w)

    xs, tgt = x[0], loss_target[0]
    ws = _pack_lanes(big_w[6:], WS_ROWS)
    h1, g_in, gs = _rmsnorm_cast(xs, norm_mix_w, "norm_mix", carry=([w_in[0].astype(BF16), ws], (False, False)))
    win = _col_unshard(g_in)
    gs = gs.reshape(N_DEV, -1)
    cqkv = gs[:, :GDN_CONV * 192].reshape(N_DEV, GDN_CONV, 192).transpose(1, 0, 2).reshape(GDN_CONV, 3 * KEY_A)
    cffn = gs[:, PACK_UNIT:PACK_UNIT + FFN_CONV * 704].reshape(N_DEV, FFN_CONV, 704).transpose(1, 0, 2).reshape(FFN_CONV, FFN_W)
    cffn = _pad_rows(cffn, SUBLANES)
    w_all = jnp.concatenate([win[:, a:b] for a, b in W_IN_ORDER] + [jnp.zeros((D_MODEL, PROJ_W - D_IN), BF16)], axis=1)
    par = _pad_rows(jnp.pad(jnp.concatenate([a_log, dt_bias], axis=0), ((0, 0), (0, LANES - GDN_HEADS))), SUBLANES)
    table = jnp.pad(rel_bias[0], ((0, 0), (0, 3 * LANES - rel_bias.shape[-1]))).reshape(ATT_HEADS, 1, 3 * LANES)

    proj, g_ba, g_bb, g_out, g_up, g_down = _mm_nn(
        h1, w_all, F32, "in_proj", 2 * MM_TM, 1152, D_MODEL, carry=([w[0].astype(BF16) for w in big_w[1:6]], (False,) * 5))
    wba, wbb, wup = _col_unshard(g_ba), _col_unshard(g_bb), _col_unshard(g_up)
    wout = g_out.reshape(D_MODEL, D_MODEL)
    wdown = g_down.reshape(D_FF, D_MODEL)
    qn, kn, va, ycv = _gdn_prep_fwd(proj, cqkv)
    oan, o_gdn, sprev, wst, ust, tst = _gdn_fwd(qn, kn, va, proj, par, gdn_norm_w)
    bias_q, bias_k = _att_bias(table)
    ob, lse, lse_t = _att_fwd(proj, bias_q)
    x2, h2 = _merge_fwd(oan, ob, proj, xs, wba, wbb, wout, norm_ffn_w)
    up, u_ffn, dx3, dx3b, act, tail_sums = _ffn_fwd(h2, wup, cffn, conv_ffn_b, wdown, x2, tgt,
                                                    norm_final_w.reshape(1, D_MODEL))

    g_wdown = _mm_tn(act, dx3b, "dw_down", 512)
    dup, ffn_sums, dx2, dx2b, nffn_sums, r_down = _ffn_bwd(
        dx3b, wdown.T, up, u_ffn, cffn, wup.T, x2, norm_ffn_w, dx3,
        carry=([g_wdown.reshape(N_DEV, -1, D_MODEL).astype(BF16)], (True,)))
    g_wup = _mm_tn(h2, dup, "dw_up", 1408)
    dproj, d_oan, d_ob, g_wout, g_wba, g_wbb = _merge_bwd(dx2b, oan, ob, proj, wba, wbb, wout.T, wba.T, wbb.T)
    dproj, dlt_t, slabs = _att_dq(proj, bias_q, lse, d_ob, ob, dproj)
    dproj = _att_dkv(proj, bias_k, lse_t, dlt_t, d_ob, dproj)
    g_rel = _relbias_grad(slabs)[:, 0, :rel_bias.shape[-1]]
    dqn, dkn, dva, dproj, gdn_sums = _gdn_bwd(qn, kn, va, proj, par, gdn_norm_w, o_gdn, sprev, wst, ust, tst, d_oan, dproj)
    dproj, cq_sums = _gdn_prep_bwd(proj, ycv, cqkv, dqn, dkn, dva, dproj)
    g_wall, r_up = _mm_tn(h1, dproj, "dw_in", 1152, carry=([_col_shards(g_wup).astype(BF16)], (True,)))
    starts = np.cumsum([0] + [b - a for a, b in W_IN_ORDER])
    g_win = jnp.concatenate([g_wall[:, starts[i]:starts[i + 1]] for i in np.argsort([a for a, _ in W_IN_ORDER])], axis=1)
    g_conv = jnp.concatenate([_lane_rows(_col_shards(cq_sums[:GDN_CONV]).reshape(N_DEV, -1)),
                              _lane_rows(_col_shards(ffn_sums[:FFN_CONV]).reshape(N_DEV, -1))], axis=1)
    grad_x, _, nmix_sums, r_in, r_ba, r_bb, r_out, r_conv = _mm_rms_bwd(
        dproj, w_all.T, xs, norm_mix_w, dx2, "in_proj_bwd", MM_TM, 1152, carry=(
            [_col_shards(g_win).astype(BF16), _col_shards(g_wba).astype(BF16), _col_shards(g_wbb).astype(BF16),
             g_wout.reshape(N_DEV, -1, D_MODEL).astype(BF16), g_conv], (True,) * 5))

    small_g = (nmix_sums[0:1], gdn_sums[0:1, :GDN_HEADS], gdn_sums[1:2, :GDN_HEADS], gdn_sums[2:3], g_rel,
               nffn_sums[0:1], ffn_sums[FFN_CONV:FFN_CONV + 1], tail_sums[0:1], tail_sums[1:2, 0:1])
    r_small, = _exchange([_pack_lanes(small_g, SMALL_ROWS)], (False,), "all_gather_small_grads")
    recv = (r_in, r_ba, r_bb, r_out, r_up, r_down, r_conv, r_small)

    res = {}
    for i, (nm, tr) in enumerate((("w_in", 128), ("w_branch_a", KEY_A), ("w_branch_b", WIDTH_B), ("w_out", 128),
                                  ("w_up", 128), ("w_down", 176))):
        res[nm] = [o[None] for o in _adamw(recv[i], big_w[i][0], big_m[i][0], big_v[i][0], "adamw_" + nm, tr)]
    conv = _adamw(recv[6], _pack_lanes(big_w[6:], WS_ROWS), _pack_lanes(big_m[6:], WS_ROWS), _pack_lanes(big_v[6:], WS_ROWS),
                  "adamw_conv", WS_ROWS)
    conv = [_unpack_lanes(o, [w.shape for w in big_w[6:]]) for o in conv]
    res["conv_qkv_w"] = [o[0] for o in conv]
    res["conv_ffn_w"] = [o[1] for o in conv]
    small_shapes = [w.shape for w in small_w]
    zero = jnp.zeros((1,), F32)
    small = _adamw(recv[7], _pack_lanes(small_w + (zero,), SMALL_ROWS), _pack_lanes(small_m + (zero,), SMALL_ROWS),
                   _pack_lanes(small_v + (zero,), SMALL_ROWS), "adamw_replicated", SMALL_ROWS)
    small = [_unpack_lanes(o, small_shapes + [()]) for o in small]
    loss = small[0][-1]
    for j, nm in enumerate(("norm_mix_w", "a_log", "dt_bias", "gdn_norm_w", "rel_bias", "norm_ffn_w", "conv_ffn_b",
                            "norm_final_w")):
        res[nm] = [o[j] for o in small]

    names = ("norm_mix_w", "w_in", "conv_qkv_w", "a_log", "dt_bias", "gdn_norm_w", "w_branch_a", "w_branch_b", "rel_bias",
             "w_out", "norm_ffn_w", "w_up", "conv_ffn_w", "conv_ffn_b", "w_down", "norm_final_w")
    outs = [res[n][kind] for kind in range(4) for n in names]
    return (loss, grad_x[None], *outs)
```

```python
import functools
import math

import numpy as np
import jax
import jax.numpy as jnp
from jax import lax
from jax.experimental import pallas as pl
from jax.experimental.pallas import tpu as pltpu

F32, BF16 = jnp.float32, jnp.bfloat16
HIGHEST = lax.Precision.HIGHEST

N_DEV = 8
D_MODEL = 1024
CHUNK = 64
EPS = 1e-6
GDN_HEADS, GDN_DK = 4, 128
KEY_A = GDN_HEADS * GDN_DK
GDN_CONV = 4
ATT_HEADS, ATT_DH = 8, 64
WIDTH_B = ATT_HEADS * ATT_DH
ATT_BAND = 9
REL_CLIP = 128
D_FF = 2816
FFN_CONV = 3
D_IN = 5640
ADAM_LR, ADAM_B1, ADAM_B2, ADAM_EPS, ADAM_WD, ADAM_STEP = 0.001, 0.9, 0.999, 1e-08, 0.01, 10

LANES = 128
SUBLANES = 8
NEG = -1e30

PROJ_W = 5760
PB = 512
CB_GA, CB_GB = 0, 1
CB_KB, CB_VB, CB_QA, CB_KA, CB_VA, CB_QB, CB_ZA = 4, 5, 6, 7, 8, 9, 10
CB_BD = 44
DP_GATES, DP_KVB, DP_QKVA, DP_QB, DP_ZBD = (2048, 0), (1024, 2), (1536, 2), (512, 9), (640, 8)
W_IN_ORDER = ((3592, 5640), (2568, 3592), (0, 1536), (2056, 2568), (1536, 2048), (2048, 2056))

ATT_QB = 256
ATT_KW = 768
ATT_VEC = 1024


def _dot(a, b, precision=None):
    return jnp.dot(a, b, preferred_element_type=F32, precision=precision)


def _dot_nt(a, b, precision=None):
    return lax.dot_general(a, b, (((1,), (1,)), ((), ())), preferred_element_type=F32, precision=precision)


def _dot_tn(a, b):
    return lax.dot_general(a, b, (((0,), (0,)), ((), ())), preferred_element_type=F32)


def _split(a):
    hi = a.astype(BF16)
    return hi, (a - hi.astype(F32)).astype(BF16)


def _dot3s(a, b):
    return _dot(a[0], b[0]) + (_dot(a[0], b[1]) + _dot(a[1], b[0]))


def _sigmoid(x):
    return 0.5 * jnp.tanh(0.5 * x) + 0.5


def _softplus(x):
    return jnp.maximum(x, 0.0) + jnp.log(1.0 + jnp.exp(-jnp.abs(x)))


def _cparams(*sem):
    return pltpu.CompilerParams(dimension_semantics=tuple(sem))


def _dp_spec(tm, region, index=lambda i: i):
    width, cb = region
    return pl.BlockSpec((tm, width), lambda i: (index(i), cb))


def _rmsnorm_cast(x, w, name, tm=512, carry=((), ())):
    t, d = x.shape
    nt = t // tm
    arrays, scatter = carry
    nx = len(arrays)

    def body(*refs):
        x_ref, w_ref = refs[:2]
        o_ref = refs[2 + nx]
        i = pl.program_id(0)
        if nx:
            local, remote = _exchange_copies(refs[2:2 + nx], refs[3 + nx:3 + 2 * nx], scatter, *refs[3 + 2 * nx:])

            @pl.when(i == 0)
            def _():
                for cp in local + remote:
                    cp.start()

        xv = x_ref[...]
        r = lax.rsqrt(jnp.mean(xv * xv, axis=-1, keepdims=True) + EPS)
        o_ref[...] = (xv * r * w_ref[...]).astype(BF16)

        if nx:
            @pl.when(i == nt - 1)
            def _():
                for cp in remote + local:
                    cp.wait()

    any_spec = pl.BlockSpec(memory_space=pl.ANY)
    out = pl.pallas_call(
        body, name=name, grid=(nt,),
        in_specs=[pl.BlockSpec((tm, d), lambda i: (i, 0)), pl.BlockSpec((1, d), lambda i: (0, 0))] + [any_spec] * nx,
        out_specs=[pl.BlockSpec((tm, d), lambda i: (i, 0))] + [any_spec] * nx,
        out_shape=[jax.ShapeDtypeStruct((t, d), BF16)] + _exchange_shapes(arrays, scatter),
        scratch_shapes=_exchange_sems(nx) if nx else [],
        compiler_params=_cparams("arbitrary" if nx else "parallel"),
    )(x, w, *arrays)
    return out if nx else out[0]


def _mm_rms_bwd(a, b, x, w, dres, name, tm, tk, carry):
    m, k = a.shape
    _, n = b.shape
    nk = k // tk
    gm = m // tm
    assert m % tm == 0 and k % tk == 0
    arrays, scatter = carry
    nx = len(arrays)

    def body(*refs):
        a_ref, b_ref, x_ref, w_ref, dres_ref = refs[:5]
        srcs = refs[5:5 + nx]
        dx_ref, dxb_ref, dw_ref = refs[5 + nx:8 + nx]
        dsts = refs[8 + nx:8 + 2 * nx]
        acc_ref = refs[8 + 2 * nx]
        i, kk = pl.program_id(0), pl.program_id(1)
        local, remote = _exchange_copies(srcs, dsts, scatter, *refs[9 + 2 * nx:])

        @pl.when((i == 0) & (kk == 0))
        def _():
            for cp in local + remote:
                cp.start()
            dw_ref[...] = jnp.zeros_like(dw_ref)

        @pl.when(kk == 0)
        def _():
            acc_ref[...] = jnp.zeros_like(acc_ref)

        acc_ref[...] += _dot(a_ref[...], b_ref[...])

        @pl.when(kk == nk - 1)
        def _():
            dhv = acc_ref[...]
            xv = x_ref[...]
            r = lax.rsqrt(jnp.mean(xv * xv, axis=-1, keepdims=True) + EPS)
            xh = xv * r
            dw_ref[0:1, :] += jnp.sum(dhv * xh, axis=0, keepdims=True)
            dxh = dhv * w_ref[...]
            dx = dres_ref[...] + r * (dxh - xh * jnp.mean(dxh * xh, axis=-1, keepdims=True))
            dx_ref[...] = dx
            dxb_ref[...] = dx.astype(BF16)

        @pl.when((i == gm - 1) & (kk == nk - 1))
        def _():
            for cp in remote + local:
                cp.wait()

    any_spec = pl.BlockSpec(memory_space=pl.ANY)
    row = pl.BlockSpec((tm, n), lambda i, kk: (i, 0))
    return pl.pallas_call(
        body, name=name, grid=(gm, nk),
        in_specs=[pl.BlockSpec((tm, tk), lambda i, kk: (i, kk)), pl.BlockSpec((tk, n), lambda i, kk: (kk, 0)),
                  row, pl.BlockSpec((1, n), lambda i, kk: (0, 0)), row] + [any_spec] * nx,
        out_specs=[row, row, pl.BlockSpec((SUBLANES, n), lambda i, kk: (0, 0))] + [any_spec] * nx,
        out_shape=[jax.ShapeDtypeStruct((m, n), F32), jax.ShapeDtypeStruct((m, n), BF16),
                   jax.ShapeDtypeStruct((SUBLANES, n), F32)] + _exchange_shapes(arrays, scatter),
        scratch_shapes=[pltpu.VMEM((tm, n), F32)] + _exchange_sems(nx),
        compiler_params=_cparams("arbitrary", "arbitrary"),
    )(a, b, x, w, dres, *arrays)


MM_TM = 1024


def _mm_tn(a, b, name, tn, tk=2 * MM_TM, carry=((), ())):
    t, m = a.shape
    _, n = b.shape
    tk = min(tk, t)
    assert t % tk == 0 and n % tn == 0
    gn, gs = n // tn, t // tk
    arrays, scatter = carry
    nx = len(arrays)

    def body(*refs):
        a_ref, b_ref = refs[:2]
        o_ref = refs[2 + nx]
        j, s = pl.program_id(0), pl.program_id(1)
        if nx:
            local, remote = _exchange_copies(refs[2:2 + nx], refs[3 + nx:3 + 2 * nx], scatter, *refs[3 + 2 * nx:])

            @pl.when((j == 0) & (s == 0))
            def _():
                for cp in local + remote:
                    cp.start()

        @pl.when(s == 0)
        def _():
            o_ref[...] = jnp.zeros_like(o_ref)

        o_ref[...] += _dot_tn(a_ref[...], b_ref[...])

        if nx:
            @pl.when((j == gn - 1) & (s == gs - 1))
            def _():
                for cp in remote + local:
                    cp.wait()

    any_spec = pl.BlockSpec(memory_space=pl.ANY)
    out = pl.pallas_call(
        body, name=name, grid=(gn, gs),
        in_specs=[pl.BlockSpec((tk, m), lambda j, s: (s, 0)),
                  pl.BlockSpec((tk, tn), lambda j, s: (s, j))] + [any_spec] * nx,
        out_specs=[pl.BlockSpec((m, tn), lambda j, s: (0, j))] + [any_spec] * nx,
        out_shape=[jax.ShapeDtypeStruct((m, n), F32)] + _exchange_shapes(arrays, scatter),
        scratch_shapes=_exchange_sems(nx) if nx else [],
        compiler_params=_cparams(*(("arbitrary", "arbitrary") if nx else ("parallel", "arbitrary"))),
    )(a, b, *arrays)
    return out if nx else out[0]


def _rel_index(dist):
    return np.clip(dist, -REL_CLIP, REL_CLIP) + REL_CLIP


def _bias_onehots():
    tw = 3 * LANES
    m = np.arange(ATT_VEC)
    dq = np.where(m <= ATT_KW, 512 - m, 512 - (m - ATT_VEC))
    dk = np.where(m < ATT_KW, m, m - ATT_VEC)
    ohq = np.zeros((tw, ATT_VEC), np.float32)
    ohk = np.zeros((tw, ATT_VEC), np.float32)
    ohq[_rel_index(dq), m] = 1.0
    ohk[_rel_index(dk), m] = 1.0
    return ohq, ohk


def _att_bias(table_pad):
    ohq, ohk = _bias_onehots()
    nslab = ATT_QB // SUBLANES

    def body(t_ref, ohq_ref, ohk_ref, bq_ref, bk_ref):
        tv = jnp.broadcast_to(t_ref[...], (SUBLANES, 3 * LANES))
        lane = lax.broadcasted_iota(jnp.int32, (ATT_QB, ATT_KW), 1)
        row = lax.broadcasted_iota(jnp.int32, (ATT_QB, ATT_KW), 0) // CHUNK
        col = lane // CHUNK
        band = (col >= row) & (col <= row + ATT_BAND - 1)
        for which, (oh_ref, out_ref) in enumerate(((ohq_ref, bq_ref), (ohk_ref, bk_ref))):
            vec = _dot(tv, oh_ref[...], HIGHEST)[0:1, :]
            slab = jnp.concatenate([vec if b == 0 else pltpu.roll(vec, b, 1) for b in range(SUBLANES)], axis=0)
            rows = [slab if a == 0 else pltpu.roll(slab, SUBLANES * a, 1) for a in range(nslab)]
            full = jnp.concatenate(rows, axis=0)[:, :ATT_KW]
            for v in range(3):
                inside = (lane >= (2 - v) * ATT_QB) if which == 0 else (lane < (v + 1) * ATT_QB)
                out_ref[v] = jnp.where(band & inside, full, NEG)

    h = table_pad.shape[0]
    oh_spec = pl.BlockSpec((3 * LANES, ATT_VEC), lambda i: (0, 0))
    out_spec = pl.BlockSpec((3, None, ATT_QB, ATT_KW), lambda i: (0, i, 0, 0))
    return pl.pallas_call(
        body, name="att_bias", grid=(h,),
        in_specs=[pl.BlockSpec((None, 1, 3 * LANES), lambda i: (i, 0, 0)), oh_spec, oh_spec],
        out_specs=[out_spec, out_spec],
        out_shape=[jax.ShapeDtypeStruct((3, h, ATT_QB, ATT_KW), F32)] * 2,
        compiler_params=_cparams("parallel"),
    )(table_pad, jnp.asarray(ohq), jnp.asarray(ohk))


def _head_masks():
    lane = lax.broadcasted_iota(jnp.int32, (1, LANES), 1)
    return [lane < ATT_DH, lane >= ATT_DH]


def _att_fwd(proj, bias_q):
    t = proj.shape[0]
    nb = t // ATT_QB
    scale = ATT_DH ** -0.5

    def body(q_ref, k0_ref, k1_ref, k2_ref, v0_ref, v1_ref, v2_ref, b_ref, o_ref, lse_ref, lset_ref):
        i = pl.program_id(0)
        q = (q_ref[...] * scale).astype(BF16)
        kk = jnp.concatenate([k0_ref[...], k1_ref[...], k2_ref[...]], axis=0).astype(BF16)
        vv = jnp.concatenate([v0_ref[...], v1_ref[...], v2_ref[...]], axis=0).astype(BF16)
        lane = lax.broadcasted_iota(jnp.int32, (1, LANES), 1)
        masks = _head_masks()
        lse_cols = jnp.zeros((ATT_QB, LANES), F32)
        for p in range(ATT_HEADS // 2):
            cs = slice(p * LANES, (p + 1) * LANES)
            qt, kt, vt = q[:, cs], kk[:, cs], vv[:, cs]
            acc = jnp.zeros((ATT_QB, LANES), F32)
            for sub in range(2):
                h = 2 * p + sub
                s = _dot_nt(jnp.where(masks[sub], qt, 0), kt) + b_ref[h]
                mx = jnp.max(s, axis=-1, keepdims=True)
                e = jnp.exp(s - mx)
                l = jnp.sum(e, axis=-1, keepdims=True)
                acc = acc + _dot(e.astype(BF16), jnp.where(masks[sub], vt, 0)) * (1.0 / l)
                lse_cols = lse_cols + jnp.where(lane == h, mx + jnp.log(l), 0.0)
            o_ref[:, cs] = acc.astype(BF16)
        lse_ref[...] = lse_cols
        lset_ref[...] = lse_cols.T[0:SUBLANES, :]

    def kv_spec(off, cb):
        return pl.BlockSpec((ATT_QB, PB), lambda i: (jnp.maximum(i + off, 0), cb))

    return pl.pallas_call(
        body, name="att_fwd", grid=(nb,),
        in_specs=[pl.BlockSpec((ATT_QB, PB), lambda i: (i, CB_QB)),
                  kv_spec(-2, CB_KB), kv_spec(-1, CB_KB), kv_spec(0, CB_KB),
                  kv_spec(-2, CB_VB), kv_spec(-1, CB_VB), kv_spec(0, CB_VB),
                  pl.BlockSpec((None, ATT_HEADS, ATT_QB, ATT_KW), lambda i: (jnp.minimum(i, 2), 0, 0, 0))],
        out_specs=[pl.BlockSpec((ATT_QB, WIDTH_B), lambda i: (i, 0)),
                   pl.BlockSpec((ATT_QB, LANES), lambda i: (i, 0)),
                   pl.BlockSpec((SUBLANES, ATT_QB), lambda i: (0, i))],
        out_shape=[jax.ShapeDtypeStruct((t, WIDTH_B), BF16), jax.ShapeDtypeStruct((t, LANES), F32),
                   jax.ShapeDtypeStruct((SUBLANES, t), F32)],
        compiler_params=_cparams("parallel"),
    )(proj, proj, proj, proj, proj, proj, proj, bias_q)


def _att_dq(proj, bias_q, lse, d_ob, ob, dproj):
    t = proj.shape[0]
    nb = t // ATT_QB
    scale = ATT_DH ** -0.5
    nslab = ATT_QB // SUBLANES

    def body(q_ref, k0_ref, k1_ref, k2_ref, v0_ref, v1_ref, v2_ref, b_ref, lse_ref, do_ref, o_ref, dp_in_ref,
             dq_ref, dlt_ref, slab_ref):
        i = pl.program_id(0)

        @pl.when(i == 0)
        def _():
            slab_ref[...] = jnp.zeros_like(slab_ref)

        q = (q_ref[...] * scale).astype(BF16)
        kk = jnp.concatenate([k0_ref[...], k1_ref[...], k2_ref[...]], axis=0).astype(BF16)
        vv = jnp.concatenate([v0_ref[...], v1_ref[...], v2_ref[...]], axis=0).astype(BF16)
        do = do_ref[...].astype(BF16)
        do_o = do_ref[...] * o_ref[...].astype(F32)
        lane = lax.broadcasted_iota(jnp.int32, (1, LANES), 1)
        masks = _head_masks()
        lse_all = lse_ref[...]
        dlt_cols = jnp.zeros((ATT_QB, LANES), F32)
        zpad = jnp.zeros((SUBLANES, ATT_VEC - ATT_KW), F32)
        for p in range(ATT_HEADS // 2):
            cs = slice(p * LANES, (p + 1) * LANES)
            qt, kt, vt, dot_ = q[:, cs], kk[:, cs], vv[:, cs], do[:, cs]
            acc = jnp.zeros((ATT_QB, LANES), F32)
            for sub in range(2):
                h = 2 * p + sub
                s = _dot_nt(jnp.where(masks[sub], qt, 0), kt) + b_ref[h]
                pr = jnp.exp(s - lse_all[:, h:h + 1])
                dp = _dot_nt(jnp.where(masks[sub], dot_, 0), vt)
                dl = jnp.sum(jnp.where(masks[sub], do_o[:, cs], 0.0), axis=-1, keepdims=True)
                ds = pr * (dp - dl)
                acc = acc + _dot(ds.astype(BF16), jnp.where(masks[sub], kt, 0)) * scale
                dlt_cols = dlt_cols + jnp.where(lane == h, dl, 0.0)
                sl = jnp.zeros((SUBLANES, ATT_VEC), F32)
                for a in range(nslab):
                    piece = jnp.concatenate([ds[a * SUBLANES:(a + 1) * SUBLANES, :], zpad], axis=1)
                    sl = sl + (piece if a == 0 else pltpu.roll(piece, ATT_VEC - SUBLANES * a, 1))
                slab_ref[h] += sl
            dq_ref[:, cs] = acc.astype(BF16)
        dlt_ref[...] = dlt_cols.T[0:SUBLANES, :]

    def kv_spec(off, cb):
        return pl.BlockSpec((ATT_QB, PB), lambda i: (jnp.maximum(i + off, 0), cb))

    return pl.pallas_call(
        body, name="att_dq", grid=(nb,),
        in_specs=[pl.BlockSpec((ATT_QB, PB), lambda i: (i, CB_QB)),
                  kv_spec(-2, CB_KB), kv_spec(-1, CB_KB), kv_spec(0, CB_KB),
                  kv_spec(-2, CB_VB), kv_spec(-1, CB_VB), kv_spec(0, CB_VB),
                  pl.BlockSpec((None, ATT_HEADS, ATT_QB, ATT_KW), lambda i: (jnp.minimum(i, 2), 0, 0, 0)),
                  pl.BlockSpec((ATT_QB, LANES), lambda i: (i, 0)),
                  pl.BlockSpec((ATT_QB, WIDTH_B), lambda i: (i, 0)), pl.BlockSpec((ATT_QB, WIDTH_B), lambda i: (i, 0)),
                  pl.BlockSpec(memory_space=pl.ANY)],
        out_specs=[_dp_spec(ATT_QB, DP_QB),
                   pl.BlockSpec((SUBLANES, ATT_QB), lambda i: (0, i)),
                   pl.BlockSpec((ATT_HEADS, SUBLANES, ATT_VEC), lambda i: (0, 0, 0))],
        out_shape=[jax.ShapeDtypeStruct(dproj.shape, dproj.dtype), jax.ShapeDtypeStruct((SUBLANES, t), F32),
                   jax.ShapeDtypeStruct((ATT_HEADS, SUBLANES, ATT_VEC), F32)],
        input_output_aliases={11: 0},
        compiler_params=_cparams("arbitrary"),
    )(proj, proj, proj, proj, proj, proj, proj, bias_q, lse, d_ob, ob, dproj)


def _att_dkv(proj, bias_k, lse_t, dlt_t, d_ob, dproj):
    t = proj.shape[0]
    nb = t // ATT_QB
    scale = ATT_DH ** -0.5

    def body(k_ref, v_ref, q0_ref, q1_ref, q2_ref, d0_ref, d1_ref, d2_ref, l0_ref, l1_ref, l2_ref,
             e0_ref, e1_ref, e2_ref, b_ref, dp_in_ref, dkv_ref):
        i = pl.program_id(0)
        k = k_ref[...].astype(BF16)
        v = v_ref[...].astype(BF16)
        qq = (jnp.concatenate([q0_ref[...], q1_ref[...], q2_ref[...]], axis=0) * scale).astype(BF16)
        do = jnp.concatenate([d0_ref[...], d1_ref[...], d2_ref[...]], axis=0).astype(BF16)
        lse = jnp.concatenate([l0_ref[...], l1_ref[...], l2_ref[...]], axis=1)
        dlt = jnp.concatenate([e0_ref[...], e1_ref[...], e2_ref[...]], axis=1)
        masks = _head_masks()
        for p in range(ATT_HEADS // 2):
            cs = slice(p * LANES, (p + 1) * LANES)
            kt, vt, qt, dot_ = k[:, cs], v[:, cs], qq[:, cs], do[:, cs]
            acc_k = jnp.zeros((ATT_QB, LANES), F32)
            acc_v = jnp.zeros((ATT_QB, LANES), F32)
            for sub in range(2):
                h = 2 * p + sub
                st = _dot_nt(jnp.where(masks[sub], kt, 0), qt) + b_ref[h]
                pt = jnp.exp(st - lse[h:h + 1, :])
                dot_m = jnp.where(masks[sub], dot_, 0)
                acc_v = acc_v + _dot(pt.astype(BF16), dot_m)
                dpt = _dot_nt(jnp.where(masks[sub], vt, 0), dot_)
                dst = pt * (dpt - dlt[h:h + 1, :])
                acc_k = acc_k + _dot(dst.astype(BF16), jnp.where(masks[sub], qt, 0))
            dkv_ref[:, cs] = acc_k.astype(BF16)
            dkv_ref[:, WIDTH_B + p * LANES:WIDTH_B + (p + 1) * LANES] = acc_v.astype(BF16)

    def q_spec(off, cb):
        return pl.BlockSpec((ATT_QB, PB), lambda i: (jnp.minimum(i + off, nb - 1), cb))

    def d_spec(off):
        return pl.BlockSpec((ATT_QB, WIDTH_B), lambda i: (jnp.minimum(i + off, nb - 1), 0))

    def r_spec(off):
        return pl.BlockSpec((SUBLANES, ATT_QB), lambda i: (0, jnp.minimum(i + off, nb - 1)))

    row = pl.BlockSpec((ATT_QB, WIDTH_B), lambda i: (i, 0))
    return pl.pallas_call(
        body, name="att_dkv", grid=(nb,),
        in_specs=[pl.BlockSpec((ATT_QB, PB), lambda i: (i, CB_KB)), pl.BlockSpec((ATT_QB, PB), lambda i: (i, CB_VB)),
                  q_spec(0, CB_QB), q_spec(1, CB_QB), q_spec(2, CB_QB),
                  d_spec(0), d_spec(1), d_spec(2), r_spec(0), r_spec(1), r_spec(2),
                  r_spec(0), r_spec(1), r_spec(2),
                  pl.BlockSpec((None, ATT_HEADS, ATT_QB, ATT_KW), lambda i: (jnp.minimum(nb - 1 - i, 2), 0, 0, 0)),
                  pl.BlockSpec(memory_space=pl.ANY)],
        out_specs=_dp_spec(ATT_QB, DP_KVB),
        out_shape=jax.ShapeDtypeStruct(dproj.shape, dproj.dtype),
        input_output_aliases={15: 0},
        compiler_params=_cparams("parallel"),
    )(proj, proj, proj, proj, proj, d_ob, d_ob, d_ob, lse_t, lse_t, lse_t, dlt_t, dlt_t, dlt_t, bias_k, dproj)


def _relbias_grad(slabs):
    ohq, _ = _bias_onehots()

    def body(s_ref, oh_ref, o_ref):
        sv = s_ref[...]
        vec = sv[0:1, :]
        for b in range(1, SUBLANES):
            vec = vec + pltpu.roll(sv[b:b + 1, :], ATT_VEC - b, 1)
        o_ref[...] = _dot_nt(jnp.broadcast_to(vec, (SUBLANES, ATT_VEC)), oh_ref[...], HIGHEST)[0:1, :]

    h = slabs.shape[0]
    return pl.pallas_call(
        body, name="att_dbias", grid=(h,),
        in_specs=[pl.BlockSpec((None, SUBLANES, ATT_VEC), lambda i: (i, 0, 0)),
                  pl.BlockSpec((3 * LANES, ATT_VEC), lambda i: (0, 0))],
        out_specs=pl.BlockSpec((None, 1, 3 * LANES), lambda i: (i, 0, 0)),
        out_shape=jax.ShapeDtypeStruct((h, 1, 3 * LANES), F32),
        compiler_params=_cparams("parallel"),
    )(slabs, jnp.asarray(ohq))


GDN_TM = 512
GDN_CB = 8
HALO = SUBLANES


def _conv_taps(ext, width, lead, n):
    return [(ext if k == width - 1 else pltpu.roll(ext, width - 1 - k, 0))[lead:lead + n] for k in range(width)]


def _next_halo_spec(tm, width, cb, t):
    return pl.BlockSpec((HALO, width), lambda i: (jnp.minimum((i + 1) * (tm // HALO), t // HALO - 1), cb))


def _in_proj_prep(h1, w_all, conv_w, carry):
    t = h1.shape[0]
    tm = GDN_TM
    nt = t // tm
    arrays, scatter = carry
    nx = len(arrays)
    c0 = CB_QA * PB

    def body(*refs):
        h_ref, w_ref, cw_ref = refs[:3]
        srcs = refs[3:3 + nx]
        proj_ref, qn_ref, kn_ref, vo_ref, y_ref = refs[3 + nx:8 + nx]
        dsts = refs[8 + nx:8 + 2 * nx]
        prev_ref = refs[8 + 2 * nx]
        i = pl.program_id(0)
        local, remote = _exchange_copies(srcs, dsts, scatter, *refs[9 + 2 * nx:])

        @pl.when(i == 0)
        def _():
            for cp in local + remote:
                cp.start()
            prev_ref[...] = jnp.zeros_like(prev_ref)

        proj = _dot(h_ref[...], w_ref[...])
        proj_ref[...] = proj
        xin = proj[:, c0:c0 + 3 * KEY_A]
        ext = jnp.concatenate([prev_ref[...], xin], axis=0)
        prev_ref[...] = xin[tm - HALO:tm]
        taps = _conv_taps(ext, GDN_CONV, HALO, tm)
        cw = cw_ref[...]
        y = sum(cw[k:k + 1, :] * taps[k] for k in range(GDN_CONV))
        y_ref[...] = y
        a = y * _sigmoid(y)
        for idx, o_ref in enumerate((qn_ref, kn_ref)):
            for h in range(GDN_HEADS):
                cs = slice(h * GDN_DK, (h + 1) * GDN_DK)
                seg = a[:, idx * KEY_A + h * GDN_DK:idx * KEY_A + (h + 1) * GDN_DK]
                o_ref[:, cs] = seg * lax.rsqrt(jnp.sum(seg * seg, axis=-1, keepdims=True) + EPS)
        vo_ref[...] = a[:, 2 * KEY_A:]

        @pl.when(i == nt - 1)
        def _():
            for cp in remote + local:
                cp.wait()

    any_spec = pl.BlockSpec(memory_space=pl.ANY)
    row = pl.BlockSpec((tm, KEY_A), lambda i: (i, 0))
    return pl.pallas_call(
        body, name="in_proj", grid=(nt,),
        in_specs=[pl.BlockSpec((tm, D_MODEL), lambda i: (i, 0)), _resident((D_MODEL, PROJ_W)),
                  _resident((GDN_CONV, 3 * KEY_A))] + [any_spec] * nx,
        out_specs=[pl.BlockSpec((tm, PROJ_W), lambda i: (i, 0)), row, row, row,
                   pl.BlockSpec((tm, 3 * KEY_A), lambda i: (i, 0))] + [any_spec] * nx,
        out_shape=[jax.ShapeDtypeStruct((t, PROJ_W), F32)] + [jax.ShapeDtypeStruct((t, KEY_A), F32)] * 3
        + [jax.ShapeDtypeStruct((t, 3 * KEY_A), F32)] + _exchange_shapes(arrays, scatter),
        scratch_shapes=[pltpu.VMEM((HALO, 3 * KEY_A), F32)] + _exchange_sems(nx),
        compiler_params=_cparams("arbitrary"),
    )(h1, w_all, conv_w, *arrays)


def _gdn_prep_bwd(proj, ycv, conv_w, dqn, dkn, dv, dproj):
    t = proj.shape[0]
    tm = GDN_TM
    nt = t // tm
    n_ext = tm + HALO

    def body(q_ref, k_ref, v_ref, y_ref, ny_ref, dq_ref, dk_ref, dv_ref, ndq_ref, ndk_ref, ndv_ref, w_ref, dp_in_ref,
             out_ref, dw_ref):
        i = pl.program_id(0)
        last = i == nt - 1

        @pl.when(i == 0)
        def _():
            dw_ref[...] = jnp.zeros_like(dw_ref)

        groups = ((q_ref, dq_ref, ndq_ref), (k_ref, dk_ref, ndk_ref), (v_ref, dv_ref, ndv_ref))
        for idx, (x_ref, d_ref, nd_ref) in enumerate(groups):
            cs_all = slice(idx * KEY_A, (idx + 1) * KEY_A)
            w = w_ref[:, cs_all]
            y = jnp.concatenate([y_ref[:, cs_all], jnp.where(last, 0.0, ny_ref[:, cs_all])], axis=0)
            sg = _sigmoid(y)
            a = y * sg
            dup = jnp.concatenate([d_ref[...], jnp.where(last, 0.0, nd_ref[...])], axis=0)
            if idx < 2:
                segs = []
                for h in range(GDN_HEADS):
                    cs = slice(h * GDN_DK, (h + 1) * GDN_DK)
                    seg = a[:, cs]
                    r = lax.rsqrt(jnp.sum(seg * seg, axis=-1, keepdims=True) + EPS)
                    nrm = seg * r
                    dn = dup[:, cs]
                    segs.append(r * (dn - nrm * jnp.sum(dn * nrm, axis=-1, keepdims=True)))
                da = jnp.concatenate(segs, axis=1)
            else:
                da = dup
            dy = da * sg * (1.0 + y * (1.0 - sg))
            xv = x_ref[...]
            dx = None
            for k in range(GDN_CONV):
                shift = GDN_CONV - 1 - k
                tap = (dy if shift == 0 else pltpu.roll(dy, n_ext - shift, 0))[:tm]
                term = w[k:k + 1, :] * tap
                dx = term if dx is None else dx + term
                dw_ref[k:k + 1, cs_all] += jnp.sum(tap * xv, axis=0, keepdims=True)
            out_ref[:, cs_all] = dx.astype(BF16)

    row = pl.BlockSpec((tm, KEY_A), lambda i: (i, 0))
    nrow = _next_halo_spec(tm, KEY_A, 0, t)
    return pl.pallas_call(
        body, name="gdn_prep_bwd", grid=(nt,),
        in_specs=[pl.BlockSpec((tm, PB), lambda i: (i, CB_QA)), pl.BlockSpec((tm, PB), lambda i: (i, CB_KA)),
                  pl.BlockSpec((tm, PB), lambda i: (i, CB_VA)),
                  pl.BlockSpec((tm, 3 * KEY_A), lambda i: (i, 0)), _next_halo_spec(tm, 3 * KEY_A, 0, t),
                  row, row, row, nrow, nrow, nrow,
                  pl.BlockSpec((GDN_CONV, 3 * KEY_A), lambda i: (0, 0)), pl.BlockSpec(memory_space=pl.ANY)],
        out_specs=[_dp_spec(tm, DP_QKVA), pl.BlockSpec((SUBLANES, 3 * KEY_A), lambda i: (0, 0))],
        out_shape=[jax.ShapeDtypeStruct(dproj.shape, dproj.dtype), jax.ShapeDtypeStruct((SUBLANES, 3 * KEY_A), F32)],
        input_output_aliases={12: 0},
        compiler_params=_cparams("arbitrary"),
    )(proj, proj, proj, ycv, ycv, dqn, dkn, dv, dqn, dkn, dv, conv_w, dproj)


class _Pair(dict):
    __getattr__ = dict.__getitem__
    __setattr__ = dict.__setitem__


def _pairs_to_lanes(cols):
    lane = lax.broadcasted_iota(jnp.int32, (1, LANES), 1)
    out = jnp.zeros((cols[0].shape[0], LANES), F32)
    for p, col in enumerate(cols):
        out = out + jnp.where(lane == p, col, 0.0)
    return out


def _gdn_terms(bd, par, kn_ref, qn_ref):
    c = CHUNK
    ii = lax.broadcasted_iota(jnp.int32, (c, c), 0)
    jj = lax.broadcasted_iota(jnp.int32, (c, c), 1)
    strict, incl = ii > jj, ii >= jj
    ltri = incl.astype(F32)
    ts = []
    for cc in range(GDN_CB):
        for h in range(GDN_HEADS):
            t = _Pair(cc=cc, h=h, rows=slice(cc * c, (cc + 1) * c), cs=slice(h * GDN_DK, (h + 1) * GDN_DK),
                      strict=strict, incl=incl)
            t.beta = _sigmoid(bd[t.rows, h:h + 1])
            t.ea = jnp.exp(par[0:1, h:h + 1])
            t.sp_arg = bd[t.rows, GDN_HEADS + h:GDN_HEADS + h + 1] + par[1:2, h:h + 1]
            t.g = -t.ea * _softplus(t.sp_arg)
            t.k = kn_ref[t.rows, t.cs]
            t.q = qn_ref[t.rows, t.cs] * (GDN_DK ** -0.5)
            t.kb, t.qb = t.k.astype(BF16), t.q.astype(BF16)
            ts.append(t)
    gall = _dot(ltri, _pairs_to_lanes([t.g for t in ts]), HIGHEST)
    gall_t = gall.T
    for p, t in enumerate(ts):
        t.gb = jnp.broadcast_to(gall[:, p:p + 1], (c, GDN_DK))
    for t in ts:
        t.kk = _dot_nt(t.kb, t.kb)
        t.qk = _dot_nt(t.qb, t.kb)
    for p, t in enumerate(ts):
        diff = t.gb[:, :c] - gall_t[p:p + 1, :]
        t.dec_s = jnp.exp(jnp.where(strict, diff, NEG))
        t.dec_i = jnp.exp(jnp.where(incl, diff, NEG))
        t.gam = jnp.exp(t.gb)
        glast = t.gb[c - 1:c, :]
        t.e_rest = jnp.exp(glast - t.gb)
        t.gl = jnp.exp(glast)
        t.p = t.qk * t.dec_i
    return ts


def _gdn_fwd(qn, kn, v, proj, par, gnw):
    t = qn.shape[0]
    c = CHUNK
    nc = t // c
    r_ = GDN_CB * c

    def body(qn_ref, kn_ref, v_ref, bd_ref, z_ref, par_ref, gnw_ref,
             oan_ref, o_ref, sp_ref, w_ref, u_ref, tm_ref, s_ref):
        @pl.when(pl.program_id(0) == 0)
        def _():
            s_ref[...] = jnp.zeros_like(s_ref)

        bd, par, gnw_v = bd_ref[...], par_ref[...], gnw_ref[...]
        eye = (lax.broadcasted_iota(jnp.int32, (c, c), 0) == lax.broadcasted_iota(jnp.int32, (c, c), 1)).astype(F32)
        ts = _gdn_terms(bd, par, kn_ref, qn_ref)
        for t in ts:
            t.vv = v_ref[t.rows, t.cs]
            t.x = -(t.beta * t.kk * t.dec_s)
            t.tinv = eye + t.x
        for t in ts:
            t.xs = _split(t.x)
        for _ in range(5):
            for t in ts:
                t.xs = _split(_dot3s(t.xs, t.xs))
            for t in ts:
                t.tinv = t.tinv + _dot3s(_split(t.tinv), t.xs)
        for t in ts:
            tsp = _split(t.tinv)
            t.wm = _dot3s(tsp, _split((t.beta * t.gam) * t.k))
            t.uv = _dot3s(tsp, _split(t.beta * t.vv))
        for t in ts:
            w_ref[t.rows, t.cs] = t.wm
            tm_ref[t.cc, t.h] = t.tinv.T
            t.wb = t.wm.astype(BF16)
            t.qgb = (t.q * t.gam).astype(BF16)
            t.kdb = (t.k * t.e_rest).astype(BF16)
            t.pb = t.p.astype(BF16)
        state = [s_ref[h] for h in range(GDN_HEADS)]
        for cc in range(GDN_CB):
            tc = [t for t in ts if t.cc == cc]
            for t in tc:
                t.sh = state[t.h]
                t.sb = t.sh.astype(BF16)
            for t in tc:
                t.ws = _dot(t.wb, t.sb)
            for t in tc:
                t.u = t.uv - t.ws
                t.ub = t.u.astype(BF16)
            for t in tc:
                state[t.h] = t.gl * t.sh + _dot_tn(t.kdb, t.ub)
            for t in tc:
                t.o = _dot(t.qgb, t.sb) + _dot(t.pb, t.ub)
                sp_ref[cc, t.h] = t.sh
                u_ref[t.rows, t.cs] = t.u
                o_ref[t.rows, t.cs] = t.o
        for h in range(GDN_HEADS):
            s_ref[h] = state[h]
        for t in ts:
            zz = z_ref[t.rows, t.cs]
            rr = lax.rsqrt(jnp.mean(t.o * t.o, axis=-1, keepdims=True) + EPS)
            oan_ref[t.rows, t.cs] = ((t.o * rr) * gnw_v * (zz * _sigmoid(zz))).astype(BF16)

    row = pl.BlockSpec((r_, KEY_A), lambda i: (i, 0))
    return pl.pallas_call(
        body, name="gdn_fwd", grid=(nc // GDN_CB,),
        in_specs=[row, row, row, pl.BlockSpec((r_, LANES), lambda i: (i, CB_BD)),
                  pl.BlockSpec((r_, PB), lambda i: (i, CB_ZA)),
                  pl.BlockSpec((SUBLANES, LANES), lambda i: (0, 0)), pl.BlockSpec((1, GDN_DK), lambda i: (0, 0))],
        out_specs=[row, row, pl.BlockSpec((GDN_CB, GDN_HEADS, GDN_DK, GDN_DK), lambda i: (i, 0, 0, 0)),
                   row, row, pl.BlockSpec((GDN_CB, GDN_HEADS, c, c), lambda i: (i, 0, 0, 0))],
        out_shape=[jax.ShapeDtypeStruct((t, KEY_A), BF16), jax.ShapeDtypeStruct((t, KEY_A), F32),
                   jax.ShapeDtypeStruct((nc, GDN_HEADS, GDN_DK, GDN_DK), F32),
                   jax.ShapeDtypeStruct((t, KEY_A), F32), jax.ShapeDtypeStruct((t, KEY_A), F32),
                   jax.ShapeDtypeStruct((nc, GDN_HEADS, c, c), F32)],
        scratch_shapes=[pltpu.VMEM((GDN_HEADS, GDN_DK, GDN_DK), F32)],
        compiler_params=_cparams("arbitrary"),
    )(qn, kn, v, proj, proj, par, gnw)


def _gdn_bwd(qn, kn, v, proj, par, gnw, o, sprev, wst, ust, tst, d_oan, dproj):
    t = qn.shape[0]
    c = CHUNK
    nc = t // c
    nb = nc // GDN_CB
    r_ = GDN_CB * c

    def body(qn_ref, kn_ref, v_ref, bd_ref, z_ref, par_ref, gnw_ref, o_ref, sp_ref, w_ref, u_ref, tm_ref, do_ref,
             dp_in_ref, dqn_ref, dkn_ref, dv_ref, dzb_ref, acc_ref, ds_ref):
        @pl.when(pl.program_id(0) == 0)
        def _():
            ds_ref[...] = jnp.zeros_like(ds_ref)
            acc_ref[...] = jnp.zeros_like(acc_ref)

        bd, par, gnw_v = bd_ref[...], par_ref[...], gnw_ref[...]
        lane = lax.broadcasted_iota(jnp.int32, (1, LANES), 1)
        rix = lax.broadcasted_iota(jnp.int32, (c, 1), 0)
        ii = lax.broadcasted_iota(jnp.int32, (c, c), 0)
        jj = lax.broadcasted_iota(jnp.int32, (c, c), 1)
        upper = (jj >= ii).astype(F32)
        acc_a = jnp.zeros((1, LANES), F32)
        acc_d = jnp.zeros((1, LANES), F32)
        acc_g = jnp.zeros((1, LANES), F32)
        ts = _gdn_terms(bd, par, kn_ref, qn_ref)
        for t in ts:
            t.vv = v_ref[t.rows, t.cs]
            t.sh = sp_ref[t.cc, t.h]
            t.sb = t.sh.astype(BF16)
            t.wm, t.u, t.tinv_t = w_ref[t.rows, t.cs], u_ref[t.rows, t.cs], tm_ref[t.cc, t.h]
            t.wb, t.ub = t.wm.astype(BF16), t.u.astype(BF16)
            ov, zz, dout = o_ref[t.rows, t.cs], z_ref[t.rows, t.cs], do_ref[t.rows, t.cs]
            sg = _sigmoid(zz)
            sil = zz * sg
            rr = lax.rsqrt(jnp.mean(ov * ov, axis=-1, keepdims=True) + EPS)
            on = ov * rr
            dzb_ref[t.rows, t.cs] = (dout * on * gnw_v * (sg * (1.0 + zz * (1.0 - sg)))).astype(BF16)
            acc_g = acc_g + jnp.sum(dout * on * sil, axis=0, keepdims=True)
            don = dout * gnw_v * sil
            t.dob = (rr * (don - on * jnp.mean(don * on, axis=-1, keepdims=True))).astype(BF16)
            t.qg = t.q * t.gam
            t.kd = t.k * t.e_rest
            t.qgb, t.kdb = t.qg.astype(BF16), t.kd.astype(BF16)
            t.ptb = t.p.T.astype(BF16)
        for t in ts:
            t.du0 = _dot(t.ptb, t.dob)
            t.ds0 = _dot_tn(t.qgb, t.dob)
            t.dqg = _dot_nt(t.dob, t.sb)
            t.dp = _dot_nt(t.dob, t.ub)
            t.uv = t.u + _dot(t.wb, t.sb)
        dstate = [ds_ref[h] for h in range(GDN_HEADS)]
        for cc in reversed(range(GDN_CB)):
            tc = [t for t in ts if t.cc == cc]
            for t in tc:
                t.dsn = dstate[t.h]
                t.dsnb = t.dsn.astype(BF16)
            for t in tc:
                t.du = t.du0 + _dot(t.kdb, t.dsnb)
            for t in tc:
                t.dub = t.du.astype(BF16)
            for t in tc:
                dstate[t.h] = t.gl * t.dsn + t.ds0 - _dot_tn(t.wb, t.dub)
            for t in tc:
                t.dkd = _dot_nt(t.ub, t.dsnb)
                t.dgl = jnp.sum(jnp.sum(t.dsn * t.sh, axis=1, keepdims=True), axis=0, keepdims=True)
                t.dwm = -_dot_nt(t.dub, t.sb)
        for h in range(GDN_HEADS):
            ds_ref[h] = dstate[h]
        for t in ts:
            tsp = _split(t.tinv_t)
            t.dbk = _dot3s(tsp, _split(t.dwm))
            t.dbv = _dot3s(tsp, _split(t.du))
        for t in ts:
            d_a = -(_dot_nt(t.dbk.astype(BF16), t.wb) + _dot_nt(t.dbv.astype(BF16), t.uv.astype(BF16)))
            t.d_a = jnp.where(t.strict, d_a, 0.0)
        for t in ts:
            t.dkk = t.d_a * t.beta * t.dec_s
            t.dqk = t.dp * t.dec_i
            t.dqkb = t.dqk.astype(BF16)
        for t in ts:
            t.dq = _dot(t.dqkb, t.kb) + t.dqg * t.gam
            t.dk = (t.dbk * (t.beta * t.gam) + _dot_tn(t.dqkb, t.qb) + _dot((t.dkk + t.dkk.T).astype(BF16), t.kb)
                    + t.dkd * t.e_rest)
        for t in ts:
            dbeta = (jnp.sum(t.d_a * t.kk * t.dec_s, axis=-1, keepdims=True)
                     + jnp.sum(t.dbk * t.k * t.gam, axis=-1, keepdims=True) + jnp.sum(t.dbv * t.vv, axis=-1, keepdims=True))
            t.dbl = dbeta * t.beta * (1.0 - t.beta)
            dv_ref[t.rows, t.cs] = t.dbv * t.beta
            bk = (t.beta * t.gam) * t.k
            zc = jnp.sum(t.dkd * t.kd, axis=-1, keepdims=True)
            xs = t.dkk * t.kk + t.dp * t.p
            dgc = (jnp.sum(xs, axis=-1, keepdims=True) - jnp.sum(xs.T, axis=-1, keepdims=True)
                   + jnp.sum(t.dbk * bk, axis=-1, keepdims=True) + jnp.sum(t.dqg * t.qg, axis=-1, keepdims=True) - zc)
            dglast = jnp.sum(zc, axis=0, keepdims=True) + t.dgl * t.gl[:, 0:1]
            t.dgc = dgc + jnp.where(rix == c - 1, dglast, 0.0)
        dgall = _dot(upper, _pairs_to_lanes([t.dgc for t in ts]), HIGHEST)
        for p, t in enumerate(ts):
            t.dg = dgall[:, p:p + 1]
        dbd_tiles = [jnp.zeros((c, LANES), F32) for _ in range(GDN_CB)]
        for t in ts:
            ddl = t.dg * (-t.ea) * _sigmoid(t.sp_arg)
            acc_a = acc_a + jnp.where(lane == t.h, jnp.sum(t.dg * t.g, axis=0, keepdims=True), 0.0)
            acc_d = acc_d + jnp.where(lane == t.h, jnp.sum(ddl, axis=0, keepdims=True), 0.0)
            dbd_tiles[t.cc] = (dbd_tiles[t.cc] + jnp.where(lane == t.h, t.dbl, 0.0)
                               + jnp.where(lane == GDN_HEADS + t.h, ddl, 0.0))
            dqn_ref[t.rows, t.cs] = t.dq * (GDN_DK ** -0.5)
            dkn_ref[t.rows, t.cs] = t.dk
        for cc in range(GDN_CB):
            dzb_ref[cc * c:(cc + 1) * c, KEY_A:KEY_A + LANES] = dbd_tiles[cc].astype(BF16)
        acc_ref[0:1, :] += acc_a
        acc_ref[1:2, :] += acc_d
        acc_ref[2:3, :] += acc_g

    def rev(i):
        return nb - 1 - i

    row = pl.BlockSpec((r_, KEY_A), lambda i: (rev(i), 0))
    st = pl.BlockSpec((GDN_CB, GDN_HEADS, GDN_DK, GDN_DK), lambda i: (rev(i), 0, 0, 0))
    tt_spec = pl.BlockSpec((GDN_CB, GDN_HEADS, c, c), lambda i: (rev(i), 0, 0, 0))
    return pl.pallas_call(
        body, name="gdn_bwd", grid=(nb,),
        in_specs=[row, row, row, pl.BlockSpec((r_, LANES), lambda i: (rev(i), CB_BD)),
                  pl.BlockSpec((r_, PB), lambda i: (rev(i), CB_ZA)),
                  pl.BlockSpec((SUBLANES, LANES), lambda i: (0, 0)), pl.BlockSpec((1, GDN_DK), lambda i: (0, 0)),
                  row, st, row, row, tt_spec, row, pl.BlockSpec(memory_space=pl.ANY)],
        out_specs=[row, row, row, _dp_spec(r_, DP_ZBD, rev), pl.BlockSpec((SUBLANES, LANES), lambda i: (0, 0))],
        out_shape=[jax.ShapeDtypeStruct((t, KEY_A), F32)] * 3 + [jax.ShapeDtypeStruct(dproj.shape, dproj.dtype),
                                                                jax.ShapeDtypeStruct((SUBLANES, LANES), F32)],
        input_output_aliases={13: 3},
        scratch_shapes=[pltpu.VMEM((GDN_HEADS, GDN_DK, GDN_DK), F32)],
        compiler_params=_cparams("arbitrary"),
    )(qn, kn, v, proj, proj, par, gnw, o, sprev, wst, ust, tst, d_oan, dproj)


def _merge_fwd(oan, ob, proj, x, wba, wbb, wout, w2, tm=512):
    t = x.shape[0]

    def body(oa_ref, ob_ref, ga_ref, gb_ref, x_ref, wba_ref, wbb_ref, wout_ref, w2_ref, x2_ref, h2_ref):
        ya = _dot(oa_ref[...], wba_ref[...])
        yb = _dot(ob_ref[...], wbb_ref[...])
        mix = _sigmoid(ga_ref[...]) * ya + _sigmoid(gb_ref[...]) * yb
        x2 = x_ref[...] + _dot(mix.astype(BF16), wout_ref[...])
        x2_ref[...] = x2
        r = lax.rsqrt(jnp.mean(x2 * x2, axis=-1, keepdims=True) + EPS)
        h2_ref[...] = (x2 * r * w2_ref[...]).astype(BF16)

    half = pl.BlockSpec((tm, KEY_A), lambda i: (i, 0))
    row = pl.BlockSpec((tm, D_MODEL), lambda i: (i, 0))
    wsmall = pl.BlockSpec((KEY_A, D_MODEL), lambda i: (0, 0))
    return pl.pallas_call(
        body, name="merge_fwd", grid=(t // tm,),
        in_specs=[half, half, pl.BlockSpec((tm, D_MODEL), lambda i: (i, CB_GA)),
                  pl.BlockSpec((tm, D_MODEL), lambda i: (i, CB_GB)), row, wsmall, wsmall,
                  pl.BlockSpec((D_MODEL, D_MODEL), lambda i: (0, 0)), pl.BlockSpec((1, D_MODEL), lambda i: (0, 0))],
        out_specs=[row, row],
        out_shape=[jax.ShapeDtypeStruct((t, D_MODEL), F32), jax.ShapeDtypeStruct((t, D_MODEL), BF16)],
        compiler_params=_cparams("parallel"),
    )(oan, ob, proj, proj, x, wba, wbb, wout, w2)


def _merge_bwd(dx2b, oan, ob, proj, wba, wbb, wout_t, wba_t, wbb_t, tm=512):
    t = dx2b.shape[0]

    def body(dx_ref, oa_ref, ob_ref, ga_ref, gb_ref, wba_ref, wbb_ref, woutt_ref, wbat_ref, wbbt_ref,
             dg_ref, doa_ref, dob_ref, gout_ref, gba_ref, gbb_ref):
        @pl.when(pl.program_id(0) == 0)
        def _():
            gout_ref[...] = jnp.zeros_like(gout_ref)
            gba_ref[...] = jnp.zeros_like(gba_ref)
            gbb_ref[...] = jnp.zeros_like(gbb_ref)

        dx, oa, ob = dx_ref[...], oa_ref[...], ob_ref[...]
        dmix = _dot(dx, woutt_ref[...])
        ya = _dot(oa, wba_ref[...])
        yb = _dot(ob, wbb_ref[...])
        sa, sb = _sigmoid(ga_ref[...]), _sigmoid(gb_ref[...])
        gout_ref[...] += _dot_tn((sa * ya + sb * yb).astype(BF16), dx)
        dg_ref[:, :D_MODEL] = (dmix * ya * sa * (1.0 - sa)).astype(BF16)
        dg_ref[:, D_MODEL:] = (dmix * yb * sb * (1.0 - sb)).astype(BF16)
        dya = (dmix * sa).astype(BF16)
        dyb = (dmix * sb).astype(BF16)
        gba_ref[...] += _dot_tn(oa, dya)
        gbb_ref[...] += _dot_tn(ob, dyb)
        doa_ref[...] = _dot(dya, wbat_ref[...])
        dob_ref[...] = _dot(dyb, wbbt_ref[...])

    half = pl.BlockSpec((tm, KEY_A), lambda i: (i, 0))
    row = pl.BlockSpec((tm, D_MODEL), lambda i: (i, 0))
    wsmall = pl.BlockSpec((KEY_A, D_MODEL), lambda i: (0, 0))
    wsmall_t = pl.BlockSpec((D_MODEL, KEY_A), lambda i: (0, 0))
    wfull = pl.BlockSpec((D_MODEL, D_MODEL), lambda i: (0, 0))
    return pl.pallas_call(
        body, name="merge_bwd", grid=(t // tm,),
        in_specs=[row, half, half, pl.BlockSpec((tm, D_MODEL), lambda i: (i, CB_GA)),
                  pl.BlockSpec((tm, D_MODEL), lambda i: (i, CB_GB)), wsmall, wsmall, wfull, wsmall_t, wsmall_t],
        out_specs=[_dp_spec(tm, DP_GATES), half, half, wfull, wsmall, wsmall],
        out_shape=[jax.ShapeDtypeStruct((t, PROJ_W), BF16), jax.ShapeDtypeStruct((t, KEY_A), F32),
                   jax.ShapeDtypeStruct((t, KEY_A), F32), jax.ShapeDtypeStruct((D_MODEL, D_MODEL), F32),
                   jax.ShapeDtypeStruct((KEY_A, D_MODEL), F32), jax.ShapeDtypeStruct((WIDTH_B, D_MODEL), F32)],
        compiler_params=_cparams("arbitrary"),
    )(dx2b, oan, ob, proj, proj, wba, wbb, wout_t, wba_t, wbb_t)


FFN_TM = 128
FFN_W = 2 * D_FF


def _resident(shape):
    return pl.BlockSpec(shape, lambda i: (0,) * len(shape), pipeline_mode=pl.Buffered(1))


def _ffn_fwd(h2, wup, cw, cb, wdown, x2, tgt, w3):
    t = x2.shape[0]
    tm = FFN_TM

    def body(h2_ref, wup_ref, cw_ref, cb_ref, wd_ref, x2_ref, tgt_ref, w3_ref, up_ref, u_ref, dx_ref, dxb_ref, act_ref,
             acc_ref, prev_ref):
        @pl.when(pl.program_id(0) == 0)
        def _():
            acc_ref[...] = jnp.zeros_like(acc_ref)
            prev_ref[...] = jnp.zeros_like(prev_ref)

        up = _dot(h2_ref[...], wup_ref[...])
        up_ref[...] = up
        ext = jnp.concatenate([prev_ref[...], up], axis=0)
        prev_ref[...] = up[tm - HALO:tm]
        taps = _conv_taps(ext, FFN_CONV, HALO, tm)
        cw_v = cw_ref[...]
        u = sum(cw_v[k:k + 1, :] * taps[k] for k in range(FFN_CONV)) + cb_ref[...]
        u_ref[...] = u
        gate, upp = u[:, :D_FF], u[:, D_FF:]
        act = (gate * _sigmoid(gate) * upp).astype(BF16)
        act_ref[...] = act
        x3 = x2_ref[...] + _dot(act, wd_ref[...])
        r = lax.rsqrt(jnp.mean(x3 * x3, axis=-1, keepdims=True) + EPS)
        xh = x3 * r
        w3v = w3_ref[...]
        err = xh * w3v - tgt_ref[...]
        loss = 0.5 * jnp.sum(jnp.mean(err * err, axis=-1, keepdims=True), axis=0, keepdims=True)
        dy = err * (1.0 / D_MODEL)
        acc_ref[0:1, :] += jnp.sum(dy * xh, axis=0, keepdims=True)
        acc_ref[1:2, :] += jnp.broadcast_to(loss, (1, D_MODEL))
        dxh = dy * w3v
        dx = r * (dxh - xh * jnp.mean(dxh * xh, axis=-1, keepdims=True))
        dx_ref[...] = dx
        dxb_ref[...] = dx.astype(BF16)

    row = pl.BlockSpec((tm, D_MODEL), lambda i: (i, 0))
    return pl.pallas_call(
        body, name="ffn_fwd", grid=(t // tm,),
        in_specs=[row, _resident((D_MODEL, FFN_W)), _resident((SUBLANES, FFN_W)), _resident((1, FFN_W)),
                  _resident((D_FF, D_MODEL)), row, row, _resident((1, D_MODEL))],
        out_specs=[pl.BlockSpec((tm, FFN_W), lambda i: (i, 0)), pl.BlockSpec((tm, FFN_W), lambda i: (i, 0)), row, row,
                   pl.BlockSpec((tm, D_FF), lambda i: (i, 0)), pl.BlockSpec((SUBLANES, D_MODEL), lambda i: (0, 0))],
        out_shape=[jax.ShapeDtypeStruct((t, FFN_W), F32), jax.ShapeDtypeStruct((t, FFN_W), F32),
                   jax.ShapeDtypeStruct((t, D_MODEL), F32),
                   jax.ShapeDtypeStruct((t, D_MODEL), BF16), jax.ShapeDtypeStruct((t, D_FF), BF16),
                   jax.ShapeDtypeStruct((SUBLANES, D_MODEL), F32)],
        scratch_shapes=[pltpu.VMEM((HALO, FFN_W), F32)],
        compiler_params=_cparams("arbitrary"),
    )(h2, wup, cw, cb, wdown, x2, tgt, w3)


def _ffn_bwd(dx3b, wdown_t, up, u, cw, wup_t, x2, w2, dx3, carry):
    t = up.shape[0]
    tm = FFN_TM
    nt = t // tm
    n_ext = tm + HALO
    arrays, scatter = carry
    nx = len(arrays)

    def rev(i):
        return nt - 1 - i

    def body(*refs):
        dx_ref, wdt_ref, up_ref, u_ref, cw_ref, wupt_ref, x2_ref, w2_ref, dres_ref = refs[:9]
        srcs = refs[9:9 + nx]
        dup_ref, acc_ref, dx2_ref, dx2b_ref, dw2_ref = refs[9 + nx:14 + nx]
        dsts = refs[14 + nx:14 + 2 * nx]
        nxt_ref = refs[14 + 2 * nx]
        i = pl.program_id(0)
        local, remote = _exchange_copies(srcs, dsts, scatter, *refs[15 + 2 * nx:])

        @pl.when(i == 0)
        def _():
            for cp in local + remote:
                cp.start()
            acc_ref[...] = jnp.zeros_like(acc_ref)
            dw2_ref[...] = jnp.zeros_like(dw2_ref)
            nxt_ref[...] = jnp.zeros_like(nxt_ref)

        dact = _dot(dx_ref[...], wdt_ref[...])
        gate, upp = u_ref[:, :D_FF], u_ref[:, D_FF:]
        sg = _sigmoid(gate)
        du = jnp.concatenate([dact * upp * (sg * (1.0 + gate * (1.0 - sg))), dact * (gate * sg)], axis=1)
        acc_ref[FFN_CONV:FFN_CONV + 1, :] += jnp.sum(du, axis=0, keepdims=True)
        ext = jnp.concatenate([du, nxt_ref[...]], axis=0)
        cw_v = cw_ref[...]
        upv = up_ref[...]
        dup = None
        for k in range(FFN_CONV):
            shift = FFN_CONV - 1 - k
            tap = du if shift == 0 else pltpu.roll(ext, n_ext - shift, 0)[:tm]
            term = cw_v[k:k + 1, :] * tap
            dup = term if dup is None else dup + term
            acc_ref[k:k + 1, :] += jnp.sum(tap * upv, axis=0, keepdims=True)
        dupb = dup.astype(BF16)
        dup_ref[...] = dupb
        nxt_ref[...] = du[0:HALO]
        dhv = _dot(dupb, wupt_ref[...])
        xv = x2_ref[...]
        r = lax.rsqrt(jnp.mean(xv * xv, axis=-1, keepdims=True) + EPS)
        xh = xv * r
        dw2_ref[0:1, :] += jnp.sum(dhv * xh, axis=0, keepdims=True)
        dxh = dhv * w2_ref[...]
        dx2 = dres_ref[...] + r * (dxh - xh * jnp.mean(dxh * xh, axis=-1, keepdims=True))
        dx2_ref[...] = dx2
        dx2b_ref[...] = dx2.astype(BF16)

        @pl.when(i == nt - 1)
        def _():
            for cp in remote + local:
                cp.wait()

    wide = pl.BlockSpec((tm, FFN_W), lambda i: (rev(i), 0))
    row = pl.BlockSpec((tm, D_MODEL), lambda i: (rev(i), 0))
    any_spec = pl.BlockSpec(memory_space=pl.ANY)
    return pl.pallas_call(
        body, name="ffn_bwd", grid=(nt,),
        in_specs=[row, _resident((D_MODEL, D_FF)), wide, wide,
                  _resident((SUBLANES, FFN_W)), _resident((FFN_W, D_MODEL)), row,
                  _resident((1, D_MODEL)), row] + [any_spec] * nx,
        out_specs=[wide, pl.BlockSpec((SUBLANES, FFN_W), lambda i: (0, 0)), row, row,
                   pl.BlockSpec((SUBLANES, D_MODEL), lambda i: (0, 0))] + [any_spec] * nx,
        out_shape=[jax.ShapeDtypeStruct((t, FFN_W), BF16), jax.ShapeDtypeStruct((SUBLANES, FFN_W), F32),
                   jax.ShapeDtypeStruct((t, D_MODEL), F32), jax.ShapeDtypeStruct((t, D_MODEL), BF16),
                   jax.ShapeDtypeStruct((SUBLANES, D_MODEL), F32)] + _exchange_shapes(arrays, scatter),
        scratch_shapes=[pltpu.VMEM((HALO, FFN_W), F32)] + _exchange_sems(nx),
        compiler_params=_cparams("arbitrary"),
    )(dx3b, wdown_t, up, u, cw, wup_t, x2, w2, dx3, *arrays)


def _adamw(parts, w, m, v, name, tr):
    r, cols = w.shape

    def body(p_ref, w_ref, m_ref, v_ref, g_ref, d_ref, mo_ref, vo_ref):
        g = p_ref[0].astype(F32)
        for s in range(1, N_DEV):
            g = g + p_ref[s].astype(F32)
        mm = ADAM_B1 * m_ref[...] + (1.0 - ADAM_B1) * g
        vv = ADAM_B2 * v_ref[...] + (1.0 - ADAM_B2) * (g * g)
        m_hat = mm / (1.0 - ADAM_B1 ** ADAM_STEP)
        v_hat = vv / (1.0 - ADAM_B2 ** ADAM_STEP)
        g_ref[...] = g
        d_ref[...] = -ADAM_LR * (m_hat / (jnp.sqrt(v_hat) + ADAM_EPS) + ADAM_WD * w_ref[...])
        mo_ref[...] = mm
        vo_ref[...] = vv

    assert r % tr == 0
    row = pl.BlockSpec((tr, cols), lambda i: (i, 0))
    return pl.pallas_call(
        body, name=name, grid=(r // tr,),
        in_specs=[pl.BlockSpec((N_DEV, tr, cols), lambda i: (0, i, 0)), row, row, row],
        out_specs=[row, row, row, row],
        out_shape=[jax.ShapeDtypeStruct((r, cols), F32)] * 4,
        compiler_params=_cparams("parallel"),
    )(parts, w, m, v)


def _mesh_pos():
    return lax.axis_index("x"), lax.axis_index("y"), lax.axis_index("c")


def _peer(pos, k):
    x, y, c = pos
    return (x ^ ((k >> 2) & 1), y ^ ((k >> 1) & 1), c ^ (k & 1))


def _flat_id(pos):
    return 4 * pos[0] + 2 * pos[1] + pos[2]


def _exchange_copies(srcs, dsts, scatter, send_sems, recv_sems, loc_sems):
    pos = _mesh_pos()
    me = _flat_id(pos)
    local, remote = [], []
    for j, (src, dst) in enumerate(zip(srcs, dsts)):
        local.append(pltpu.make_async_copy(src.at[me] if scatter[j] else src, dst.at[me], loc_sems.at[j]))
        for k in range(1, N_DEV):
            to = _peer(pos, k)
            remote.append(pltpu.make_async_remote_copy(
                src_ref=src.at[_flat_id(to)] if scatter[j] else src, dst_ref=dst.at[me],
                send_sem=send_sems.at[j, k - 1], recv_sem=recv_sems.at[j, k - 1],
                device_id=to, device_id_type=pl.DeviceIdType.MESH))
    return local, remote


def _exchange_shapes(arrays, scatter):
    return [jax.ShapeDtypeStruct(a.shape if s else (N_DEV,) + a.shape, a.dtype) for a, s in zip(arrays, scatter)]


def _exchange_sems(n):
    return [pltpu.SemaphoreType.DMA((n, N_DEV - 1)), pltpu.SemaphoreType.DMA((n, N_DEV - 1)), pltpu.SemaphoreType.DMA((n,))]


def _exchange(arrays, scatter, name):
    n = len(arrays)
    any_spec = pl.BlockSpec(memory_space=pl.ANY)

    def body(*refs):
        local, remote = _exchange_copies(refs[:n], refs[n:2 * n], scatter, *refs[2 * n:])
        for cp in local + remote:
            cp.start()
        for cp in remote:
            cp.wait()
        for cp in local:
            cp.wait()

    return pl.pallas_call(
        body, name=name, in_specs=[any_spec] * n, out_specs=[any_spec] * n,
        out_shape=_exchange_shapes(arrays, scatter), scratch_shapes=_exchange_sems(n),
    )(*arrays)


def _pad_rows(a, rows):
    return jnp.pad(a, ((0, rows - a.shape[0]),) + ((0, 0),) * (a.ndim - 1))


PACK_UNIT = SUBLANES * LANES


def _pack_lanes(parts, rows):
    out = []
    for a in parts:
        f = a.reshape(-1)
        out.append(jnp.pad(f, (0, (-f.shape[0]) % PACK_UNIT)).reshape(-1, LANES))
    packed = jnp.concatenate(out, axis=0)
    assert packed.shape[0] == rows, (packed.shape, rows)
    return packed


def _unpack_lanes(buf, shapes):
    out, r0 = [], 0
    for shp in shapes:
        n = math.prod(shp)
        nr = -(-n // PACK_UNIT) * SUBLANES
        out.append(buf[r0:r0 + nr].reshape(-1)[:n].reshape(shp))
        r0 += nr
    return out


def _col_shards(g):
    r, n = g.shape
    return g.reshape(r, N_DEV, n // N_DEV).transpose(1, 0, 2)


def _col_unshard(s):
    _, r, w = s.shape
    return s.transpose(1, 0, 2).reshape(r, N_DEV * w)


def _lane_rows(flat):
    n = flat.shape[1]
    return jnp.pad(flat, ((0, 0), (0, (-n) % PACK_UNIT))).reshape(N_DEV, -1, LANES)


SMALL_ROWS = 128
WS_ROWS = 32


def kernel(x, norm_mix_w, w_in, conv_qkv_w, a_log, dt_bias, gdn_norm_w, w_branch_a, w_branch_b, rel_bias, w_out, norm_ffn_w, w_up, conv_ffn_w, conv_ffn_b, w_down, norm_final_w, loss_target, m_norm_mix_w, m_w_in, m_conv_qkv_w, m_a_log, m_dt_bias, m_gdn_norm_w, m_w_branch_a, m_w_branch_b, m_rel_bias, m_w_out, m_norm_ffn_w, m_w_up, m_conv_ffn_w, m_conv_ffn_b, m_w_down, m_norm_final_w, v_norm_mix_w, v_w_in, v_conv_qkv_w, v_a_log, v_dt_bias, v_gdn_norm_w, v_w_branch_a, v_w_branch_b, v_rel_bias, v_w_out, v_norm_ffn_w, v_w_up, v_conv_ffn_w, v_conv_ffn_b, v_w_down, v_norm_final_w):
    big_w = (w_in, w_branch_a, w_branch_b, w_out, w_up, w_down, conv_qkv_w, conv_ffn_w)
    big_m = (m_w_in, m_w_branch_a, m_w_branch_b, m_w_out, m_w_up, m_w_down, m_conv_qkv_w, m_conv_ffn_w)
    big_v = (v_w_in, v_w_branch_a, v_w_branch_b, v_w_out, v_w_up, v_w_down, v_conv_qkv_w, v_conv_ffn_w)
    small_w = (norm_mix_w, a_log, dt_bias, gdn_norm_w, rel_bias, norm_ffn_w, conv_ffn_b, norm_final_w)
    small_m = (m_norm_mix_w, m_a_log, m_dt_bias, m_gdn_norm_w, m_rel_bias, m_norm_ffn_w, m_conv_ffn_b, m_norm_final_w)
    small_v = (v_norm_mix_w, v_a_log, v_dt_bias, v_gdn_norm_w, v_rel_bias, v_norm_ffn_w, v_conv_ffn_b, v_norm_final_w)

    xs, tgt = x[0], loss_target[0]
    ws = _pack_lanes(big_w[6:], WS_ROWS)
    h1, g_in, gs = _rmsnorm_cast(xs, norm_mix_w, "norm_mix", carry=([w_in[0].astype(BF16), ws], (False, False)))
    win = _col_unshard(g_in)
    gs = gs.reshape(N_DEV, -1)
    cqkv = gs[:, :GDN_CONV * 192].reshape(N_DEV, GDN_CONV, 192).transpose(1, 0, 2).reshape(GDN_CONV, 3 * KEY_A)
    cffn = gs[:, PACK_UNIT:PACK_UNIT + FFN_CONV * 704].reshape(N_DEV, FFN_CONV, 704).transpose(1, 0, 2).reshape(FFN_CONV, FFN_W)
    cffn = _pad_rows(cffn, SUBLANES)
    w_all = jnp.concatenate([win[:, a:b] for a, b in W_IN_ORDER] + [jnp.zeros((D_MODEL, PROJ_W - D_IN), BF16)], axis=1)
    par = _pad_rows(jnp.pad(jnp.concatenate([a_log, dt_bias], axis=0), ((0, 0), (0, LANES - GDN_HEADS))), SUBLANES)
    table = jnp.pad(rel_bias[0], ((0, 0), (0, 3 * LANES - rel_bias.shape[-1]))).reshape(ATT_HEADS, 1, 3 * LANES)

    proj, qn, kn, va, ycv, g_ba, g_bb, g_out, g_up, g_down = _in_proj_prep(
        h1, w_all, cqkv, carry=([w[0].astype(BF16) for w in big_w[1:6]], (False,) * 5))
    wba, wbb, wup = _col_unshard(g_ba), _col_unshard(g_bb), _col_unshard(g_up)
    wout = g_out.reshape(D_MODEL, D_MODEL)
    wdown = g_down.reshape(D_FF, D_MODEL)
    oan, o_gdn, sprev, wst, ust, tst = _gdn_fwd(qn, kn, va, proj, par, gdn_norm_w)
    bias_q, bias_k = _att_bias(table)
    ob, lse, lse_t = _att_fwd(proj, bias_q)
    x2, h2 = _merge_fwd(oan, ob, proj, xs, wba, wbb, wout, norm_ffn_w)
    up, u_ffn, dx3, dx3b, act, tail_sums = _ffn_fwd(h2, wup, cffn, conv_ffn_b, wdown, x2, tgt,
                                                    norm_final_w.reshape(1, D_MODEL))

    g_wdown = _mm_tn(act, dx3b, "dw_down", 512)
    dup, ffn_sums, dx2, dx2b, nffn_sums, r_down = _ffn_bwd(
        dx3b, wdown.T, up, u_ffn, cffn, wup.T, x2, norm_ffn_w, dx3,
        carry=([g_wdown.reshape(N_DEV, -1, D_MODEL).astype(BF16)], (True,)))
    g_wup = _mm_tn(h2, dup, "dw_up", 1408)
    dproj, d_oan, d_ob, g_wout, g_wba, g_wbb = _merge_bwd(dx2b, oan, ob, proj, wba, wbb, wout.T, wba.T, wbb.T)
    dproj, dlt_t, slabs = _att_dq(proj, bias_q, lse, d_ob, ob, dproj)
    dproj = _att_dkv(proj, bias_k, lse_t, dlt_t, d_ob, dproj)
    g_rel = _relbias_grad(slabs)[:, 0, :rel_bias.shape[-1]]
    dqn, dkn, dva, dproj, gdn_sums = _gdn_bwd(qn, kn, va, proj, par, gdn_norm_w, o_gdn, sprev, wst, ust, tst, d_oan, dproj)
    dproj, cq_sums = _gdn_prep_bwd(proj, ycv, cqkv, dqn, dkn, dva, dproj)
    g_wall, r_up = _mm_tn(h1, dproj, "dw_in", 1152, carry=([_col_shards(g_wup).astype(BF16)], (True,)))
    starts = np.cumsum([0] + [b - a for a, b in W_IN_ORDER])
    g_win = jnp.concatenate([g_wall[:, starts[i]:starts[i + 1]] for i in np.argsort([a for a, _ in W_IN_ORDER])], axis=1)
    g_conv = jnp.concatenate([_lane_rows(_col_shards(cq_sums[:GDN_CONV]).reshape(N_DEV, -1)),
                              _lane_rows(_col_shards(ffn_sums[:FFN_CONV]).reshape(N_DEV, -1))], axis=1)
    grad_x, _, nmix_sums, r_in, r_ba, r_bb, r_out, r_conv = _mm_rms_bwd(
        dproj, w_all.T, xs, norm_mix_w, dx2, "in_proj_bwd", MM_TM, 1152, carry=(
            [_col_shards(g_win).astype(BF16), _col_shards(g_wba).astype(BF16), _col_shards(g_wbb).astype(BF16),
             g_wout.reshape(N_DEV, -1, D_MODEL).astype(BF16), g_conv], (True,) * 5))

    small_g = (nmix_sums[0:1], gdn_sums[0:1, :GDN_HEADS], gdn_sums[1:2, :GDN_HEADS], gdn_sums[2:3], g_rel,
               nffn_sums[0:1], ffn_sums[FFN_CONV:FFN_CONV + 1], tail_sums[0:1], tail_sums[1:2, 0:1])
    r_small, = _exchange([_pack_lanes(small_g, SMALL_ROWS)], (False,), "all_gather_small_grads")
    recv = (r_in, r_ba, r_bb, r_out, r_up, r_down, r_conv, r_small)

    res = {}
    for i, (nm, tr) in enumerate((("w_in", 128), ("w_branch_a", KEY_A), ("w_branch_b", WIDTH_B), ("w_out", 128),
                                  ("w_up", 128), ("w_down", 176))):
        res[nm] = [o[None] for o in _adamw(recv[i], big_w[i][0], big_m[i][0], big_v[i][0], "adamw_" + nm, tr)]
    conv = _adamw(recv[6], _pack_lanes(big_w[6:], WS_ROWS), _pack_lanes(big_m[6:], WS_ROWS), _pack_lanes(big_v[6:], WS_ROWS),
                  "adamw_conv", WS_ROWS)
    conv = [_unpack_lanes(o, [w.shape for w in big_w[6:]]) for o in conv]
    res["conv_qkv_w"] = [o[0] for o in conv]
    res["conv_ffn_w"] = [o[1] for o in conv]
    small_shapes = [w.shape for w in small_w]
    zero = jnp.zeros((1,), F32)
    small = _adamw(recv[7], _pack_lanes(small_w + (zero,), SMALL_ROWS), _pack_lanes(small_m + (zero,), SMALL_ROWS),
                   _pack_lanes(small_v + (zero,), SMALL_ROWS), "adamw_replicated", SMALL_ROWS)
    small = [_unpack_lanes(o, small_shapes + [()]) for o in small]
    loss = small[0][-1]
    for j, nm in enumerate(("norm_mix_w", "a_log", "dt_bias", "gdn_norm_w", "rel_bias", "norm_ffn_w", "conv_ffn_b",
                            "norm_final_w")):
        res[nm] = [o[j] for o in small]

    names = ("norm_mix_w", "w_in", "conv_qkv_w", "a_log", "dt_bias", "gdn_norm_w", "w_branch_a", "w_branch_b", "rel_bias",
             "w_out", "norm_ffn_w", "w_up", "conv_ffn_w", "conv_ffn_b", "w_down", "norm_final_w")
    outs = [res[n][kind] for kind in range(4) for n in names]
    return (loss, grad_x[None], *outs)
```

```python
import functools
import math

import numpy as np
import jax
import jax.numpy as jnp
from jax import lax
from jax.experimental import pallas as pl
from jax.experimental.pallas import tpu as pltpu

F32, BF16 = jnp.float32, jnp.bfloat16
HIGHEST = lax.Precision.HIGHEST

N_DEV = 8
D_MODEL = 1024
CHUNK = 64
EPS = 1e-6
GDN_HEADS, GDN_DK = 4, 128
KEY_A = GDN_HEADS * GDN_DK
GDN_CONV = 4
ATT_HEADS, ATT_DH = 8, 64
WIDTH_B = ATT_HEADS * ATT_DH
ATT_BAND = 9
REL_CLIP = 128
D_FF = 2816
FFN_CONV = 3
D_IN = 5640
ADAM_LR, ADAM_B1, ADAM_B2, ADAM_EPS, ADAM_WD, ADAM_STEP = 0.001, 0.9, 0.999, 1e-08, 0.01, 10

LANES = 128
SUBLANES = 8
NEG = -1e30

PROJ_W = 5760
PB = 512
CB_GA, CB_GB = 0, 1
CB_KB, CB_VB, CB_QA, CB_KA, CB_VA, CB_QB, CB_ZA = 4, 5, 6, 7, 8, 9, 10
CB_BD = 44
DP_GATES, DP_KVB, DP_QKVA, DP_QB, DP_ZBD = (2048, 0), (1024, 2), (1536, 2), (512, 9), (640, 8)
W_IN_ORDER = ((3592, 5640), (2568, 3592), (0, 1536), (2056, 2568), (1536, 2048), (2048, 2056))

ATT_QB = 256
ATT_KW = 768
ATT_VEC = 1024


def _dot(a, b, precision=None):
    return jnp.dot(a, b, preferred_element_type=F32, precision=precision)


def _dot_nt(a, b, precision=None):
    return lax.dot_general(a, b, (((1,), (1,)), ((), ())), preferred_element_type=F32, precision=precision)


def _dot_tn(a, b):
    return lax.dot_general(a, b, (((0,), (0,)), ((), ())), preferred_element_type=F32)


def _split(a):
    hi = a.astype(BF16)
    return hi, (a - hi.astype(F32)).astype(BF16)


def _dot3s(a, b):
    return _dot(a[0], b[0]) + (_dot(a[0], b[1]) + _dot(a[1], b[0]))


def _sigmoid(x):
    return 0.5 * jnp.tanh(0.5 * x) + 0.5


def _softplus(x):
    return jnp.maximum(x, 0.0) + jnp.log(1.0 + jnp.exp(-jnp.abs(x)))


def _cparams(*sem):
    return pltpu.CompilerParams(dimension_semantics=tuple(sem))


def _dp_spec(tm, region, index=lambda i: i):
    width, cb = region
    return pl.BlockSpec((tm, width), lambda i: (index(i), cb))


def _rmsnorm_cast(x, w, name, tm=512, carry=((), ())):
    t, d = x.shape
    nt = t // tm
    arrays, scatter = carry
    nx = len(arrays)

    def body(*refs):
        x_ref, w_ref = refs[:2]
        o_ref = refs[2 + nx]
        i = pl.program_id(0)
        if nx:
            local, remote = _exchange_copies(refs[2:2 + nx], refs[3 + nx:3 + 2 * nx], scatter, *refs[3 + 2 * nx:])

            @pl.when(i == 0)
            def _():
                for cp in local + remote:
                    cp.start()

        xv = x_ref[...]
        r = lax.rsqrt(jnp.mean(xv * xv, axis=-1, keepdims=True) + EPS)
        o_ref[...] = (xv * r * w_ref[...]).astype(BF16)

        if nx:
            @pl.when(i == nt - 1)
            def _():
                for cp in remote + local:
                    cp.wait()

    any_spec = pl.BlockSpec(memory_space=pl.ANY)
    out = pl.pallas_call(
        body, name=name, grid=(nt,),
        in_specs=[pl.BlockSpec((tm, d), lambda i: (i, 0)), pl.BlockSpec((1, d), lambda i: (0, 0))] + [any_spec] * nx,
        out_specs=[pl.BlockSpec((tm, d), lambda i: (i, 0))] + [any_spec] * nx,
        out_shape=[jax.ShapeDtypeStruct((t, d), BF16)] + _exchange_shapes(arrays, scatter),
        scratch_shapes=_exchange_sems(nx) if nx else [],
        compiler_params=_cparams("arbitrary" if nx else "parallel"),
    )(x, w, *arrays)
    return out if nx else out[0]


def _mm_rms_bwd(a, b, x, w, dres, name, tm, tk, carry):
    m, k = a.shape
    _, n = b.shape
    nk = k // tk
    gm = m // tm
    assert m % tm == 0 and k % tk == 0
    arrays, scatter = carry
    nx = len(arrays)

    def body(*refs):
        a_ref, b_ref, x_ref, w_ref, dres_ref = refs[:5]
        srcs = refs[5:5 + nx]
        dx_ref, dxb_ref, dw_ref = refs[5 + nx:8 + nx]
        dsts = refs[8 + nx:8 + 2 * nx]
        acc_ref = refs[8 + 2 * nx]
        i, kk = pl.program_id(0), pl.program_id(1)
        local, remote = _exchange_copies(srcs, dsts, scatter, *refs[9 + 2 * nx:])

        @pl.when((i == 0) & (kk == 0))
        def _():
            for cp in local + remote:
                cp.start()
            dw_ref[...] = jnp.zeros_like(dw_ref)

        @pl.when(kk == 0)
        def _():
            acc_ref[...] = jnp.zeros_like(acc_ref)

        acc_ref[...] += _dot(a_ref[...], b_ref[...])

        @pl.when(kk == nk - 1)
        def _():
            dhv = acc_ref[...]
            xv = x_ref[...]
            r = lax.rsqrt(jnp.mean(xv * xv, axis=-1, keepdims=True) + EPS)
            xh = xv * r
            dw_ref[0:1, :] += jnp.sum(dhv * xh, axis=0, keepdims=True)
            dxh = dhv * w_ref[...]
            dx = dres_ref[...] + r * (dxh - xh * jnp.mean(dxh * xh, axis=-1, keepdims=True))
            dx_ref[...] = dx
            dxb_ref[...] = dx.astype(BF16)

        @pl.when((i == gm - 1) & (kk == nk - 1))
        def _():
            for cp in remote + local:
                cp.wait()

    any_spec = pl.BlockSpec(memory_space=pl.ANY)
    row = pl.BlockSpec((tm, n), lambda i, kk: (i, 0))
    return pl.pallas_call(
        body, name=name, grid=(gm, nk),
        in_specs=[pl.BlockSpec((tm, tk), lambda i, kk: (i, kk)), pl.BlockSpec((tk, n), lambda i, kk: (kk, 0)),
                  row, pl.BlockSpec((1, n), lambda i, kk: (0, 0)), row] + [any_spec] * nx,
        out_specs=[row, row, pl.BlockSpec((SUBLANES, n), lambda i, kk: (0, 0))] + [any_spec] * nx,
        out_shape=[jax.ShapeDtypeStruct((m, n), F32), jax.ShapeDtypeStruct((m, n), BF16),
                   jax.ShapeDtypeStruct((SUBLANES, n), F32)] + _exchange_shapes(arrays, scatter),
        scratch_shapes=[pltpu.VMEM((tm, n), F32)] + _exchange_sems(nx),
        compiler_params=_cparams("arbitrary", "arbitrary"),
    )(a, b, x, w, dres, *arrays)


MM_TM = 1024


def _mm_tn(a, b, name, tn, tk=2 * MM_TM, carry=((), ())):
    t, m = a.shape
    _, n = b.shape
    tk = min(tk, t)
    assert t % tk == 0 and n % tn == 0
    gn, gs = n // tn, t // tk
    arrays, scatter = carry
    nx = len(arrays)

    def body(*refs):
        a_ref, b_ref = refs[:2]
        o_ref = refs[2 + nx]
        j, s = pl.program_id(0), pl.program_id(1)
        if nx:
            local, remote = _exchange_copies(refs[2:2 + nx], refs[3 + nx:3 + 2 * nx], scatter, *refs[3 + 2 * nx:])

            @pl.when((j == 0) & (s == 0))
            def _():
                for cp in local + remote:
                    cp.start()

        @pl.when(s == 0)
        def _():
            o_ref[...] = jnp.zeros_like(o_ref)

        o_ref[...] += _dot_tn(a_ref[...], b_ref[...])

        if nx:
            @pl.when((j == gn - 1) & (s == gs - 1))
            def _():
                for cp in remote + local:
                    cp.wait()

    any_spec = pl.BlockSpec(memory_space=pl.ANY)
    out = pl.pallas_call(
        body, name=name, grid=(gn, gs),
        in_specs=[pl.BlockSpec((tk, m), lambda j, s: (s, 0)),
                  pl.BlockSpec((tk, tn), lambda j, s: (s, j))] + [any_spec] * nx,
        out_specs=[pl.BlockSpec((m, tn), lambda j, s: (0, j))] + [any_spec] * nx,
        out_shape=[jax.ShapeDtypeStruct((m, n), F32)] + _exchange_shapes(arrays, scatter),
        scratch_shapes=_exchange_sems(nx) if nx else [],
        compiler_params=_cparams(*(("arbitrary", "arbitrary") if nx else ("parallel", "arbitrary"))),
    )(a, b, *arrays)
    return out if nx else out[0]


def _rel_index(dist):
    return np.clip(dist, -REL_CLIP, REL_CLIP) + REL_CLIP


def _bias_onehots():
    tw = 3 * LANES
    m = np.arange(ATT_VEC)
    dq = np.where(m <= ATT_KW, 512 - m, 512 - (m - ATT_VEC))
    dk = np.where(m < ATT_KW, m, m - ATT_VEC)
    ohq = np.zeros((tw, ATT_VEC), np.float32)
    ohk = np.zeros((tw, ATT_VEC), np.float32)
    ohq[_rel_index(dq), m] = 1.0
    ohk[_rel_index(dk), m] = 1.0
    return ohq, ohk


def _att_bias(table_pad):
    ohq, ohk = _bias_onehots()
    nslab = ATT_QB // SUBLANES

    def body(t_ref, ohq_ref, ohk_ref, bq_ref, bk_ref):
        tv = jnp.broadcast_to(t_ref[...], (SUBLANES, 3 * LANES))
        lane = lax.broadcasted_iota(jnp.int32, (ATT_QB, ATT_KW), 1)
        row = lax.broadcasted_iota(jnp.int32, (ATT_QB, ATT_KW), 0) // CHUNK
        col = lane // CHUNK
        band = (col >= row) & (col <= row + ATT_BAND - 1)
        for which, (oh_ref, out_ref) in enumerate(((ohq_ref, bq_ref), (ohk_ref, bk_ref))):
            vec = _dot(tv, oh_ref[...], HIGHEST)[0:1, :]
            slab = jnp.concatenate([vec if b == 0 else pltpu.roll(vec, b, 1) for b in range(SUBLANES)], axis=0)
            rows = [slab if a == 0 else pltpu.roll(slab, SUBLANES * a, 1) for a in range(nslab)]
            full = jnp.concatenate(rows, axis=0)[:, :ATT_KW]
            for v in range(3):
                inside = (lane >= (2 - v) * ATT_QB) if which == 0 else (lane < (v + 1) * ATT_QB)
                out_ref[v] = jnp.where(band & inside, full, NEG)

    h = table_pad.shape[0]
    oh_spec = pl.BlockSpec((3 * LANES, ATT_VEC), lambda i: (0, 0))
    out_spec = pl.BlockSpec((3, None, ATT_QB, ATT_KW), lambda i: (0, i, 0, 0))
    return pl.pallas_call(
        body, name="att_bias", grid=(h,),
        in_specs=[pl.BlockSpec((None, 1, 3 * LANES), lambda i: (i, 0, 0)), oh_spec, oh_spec],
        out_specs=[out_spec, out_spec],
        out_shape=[jax.ShapeDtypeStruct((3, h, ATT_QB, ATT_KW), F32)] * 2,
        compiler_params=_cparams("parallel"),
    )(table_pad, jnp.asarray(ohq), jnp.asarray(ohk))


def _head_masks():
    lane = lax.broadcasted_iota(jnp.int32, (1, LANES), 1)
    return [lane < ATT_DH, lane >= ATT_DH]


def _att_merge_fwd(proj, bias_q, oan, x, wba, wbb, wout, w2):
    t = proj.shape[0]
    nb = t // ATT_QB
    scale = ATT_DH ** -0.5

    def body(q_ref, k0_ref, k1_ref, k2_ref, v0_ref, v1_ref, v2_ref, b_ref, oa_ref, ga_ref, gb_ref, x_ref,
             wba_ref, wbb_ref, wout_ref, w2_ref, o_ref, lse_ref, lset_ref, x2_ref, h2_ref):
        i = pl.program_id(0)
        q = (q_ref[...] * scale).astype(BF16)
        kk = jnp.concatenate([k0_ref[...], k1_ref[...], k2_ref[...]], axis=0).astype(BF16)
        vv = jnp.concatenate([v0_ref[...], v1_ref[...], v2_ref[...]], axis=0).astype(BF16)
        lane = lax.broadcasted_iota(jnp.int32, (1, LANES), 1)
        masks = _head_masks()
        lse_cols = jnp.zeros((ATT_QB, LANES), F32)
        for p in range(ATT_HEADS // 2):
            cs = slice(p * LANES, (p + 1) * LANES)
            qt, kt, vt = q[:, cs], kk[:, cs], vv[:, cs]
            acc = jnp.zeros((ATT_QB, LANES), F32)
            for sub in range(2):
                h = 2 * p + sub
                s = _dot_nt(jnp.where(masks[sub], qt, 0), kt) + b_ref[h]
                mx = jnp.max(s, axis=-1, keepdims=True)
                e = jnp.exp(s - mx)
                l = jnp.sum(e, axis=-1, keepdims=True)
                acc = acc + _dot(e.astype(BF16), jnp.where(masks[sub], vt, 0)) * (1.0 / l)
                lse_cols = lse_cols + jnp.where(lane == h, mx + jnp.log(l), 0.0)
            o_ref[:, cs] = acc.astype(BF16)
        lse_ref[...] = lse_cols
        lset_ref[...] = lse_cols.T[0:SUBLANES, :]
        ya = _dot(oa_ref[...], wba_ref[...])
        yb = _dot(o_ref[...], wbb_ref[...])
        mix = _sigmoid(ga_ref[...]) * ya + _sigmoid(gb_ref[...]) * yb
        x2 = x_ref[...] + _dot(mix.astype(BF16), wout_ref[...])
        x2_ref[...] = x2
        r = lax.rsqrt(jnp.mean(x2 * x2, axis=-1, keepdims=True) + EPS)
        h2_ref[...] = (x2 * r * w2_ref[...]).astype(BF16)

    def kv_spec(off, cb):
        return pl.BlockSpec((ATT_QB, PB), lambda i: (jnp.maximum(i + off, 0), cb))

    half = pl.BlockSpec((ATT_QB, WIDTH_B), lambda i: (i, 0))
    row = pl.BlockSpec((ATT_QB, D_MODEL), lambda i: (i, 0))
    return pl.pallas_call(
        body, name="att_merge_fwd", grid=(nb,),
        in_specs=[pl.BlockSpec((ATT_QB, PB), lambda i: (i, CB_QB)),
                  kv_spec(-2, CB_KB), kv_spec(-1, CB_KB), kv_spec(0, CB_KB),
                  kv_spec(-2, CB_VB), kv_spec(-1, CB_VB), kv_spec(0, CB_VB),
                  pl.BlockSpec((None, ATT_HEADS, ATT_QB, ATT_KW), lambda i: (jnp.minimum(i, 2), 0, 0, 0)),
                  half, pl.BlockSpec((ATT_QB, D_MODEL), lambda i: (i, CB_GA)),
                  pl.BlockSpec((ATT_QB, D_MODEL), lambda i: (i, CB_GB)), row,
                  _resident((KEY_A, D_MODEL)), _resident((WIDTH_B, D_MODEL)), _resident((D_MODEL, D_MODEL)),
                  _resident((1, D_MODEL))],
        out_specs=[half, pl.BlockSpec((ATT_QB, LANES), lambda i: (i, 0)),
                   pl.BlockSpec((SUBLANES, ATT_QB), lambda i: (0, i)), row, row],
        out_shape=[jax.ShapeDtypeStruct((t, WIDTH_B), BF16), jax.ShapeDtypeStruct((t, LANES), F32),
                   jax.ShapeDtypeStruct((SUBLANES, t), F32), jax.ShapeDtypeStruct((t, D_MODEL), F32),
                   jax.ShapeDtypeStruct((t, D_MODEL), BF16)],
        compiler_params=_cparams("parallel"),
    )(proj, proj, proj, proj, proj, proj, proj, bias_q, oan, proj, proj, x, wba, wbb, wout, w2)


def _att_dq(proj, bias_q, lse, d_ob, ob, dproj):
    t = proj.shape[0]
    nb = t // ATT_QB
    scale = ATT_DH ** -0.5
    nslab = ATT_QB // SUBLANES

    def body(q_ref, k0_ref, k1_ref, k2_ref, v0_ref, v1_ref, v2_ref, b_ref, lse_ref, do_ref, o_ref, dp_in_ref,
             dq_ref, dlt_ref, slab_ref):
        i = pl.program_id(0)

        @pl.when(i == 0)
        def _():
            slab_ref[...] = jnp.zeros_like(slab_ref)

        q = (q_ref[...] * scale).astype(BF16)
        kk = jnp.concatenate([k0_ref[...], k1_ref[...], k2_ref[...]], axis=0).astype(BF16)
        vv = jnp.concatenate([v0_ref[...], v1_ref[...], v2_ref[...]], axis=0).astype(BF16)
        do = do_ref[...].astype(BF16)
        do_o = do_ref[...] * o_ref[...].astype(F32)
        lane = lax.broadcasted_iota(jnp.int32, (1, LANES), 1)
        masks = _head_masks()
        lse_all = lse_ref[...]
        dlt_cols = jnp.zeros((ATT_QB, LANES), F32)
        zpad = jnp.zeros((SUBLANES, ATT_VEC - ATT_KW), F32)
        for p in range(ATT_HEADS // 2):
            cs = slice(p * LANES, (p + 1) * LANES)
            qt, kt, vt, dot_ = q[:, cs], kk[:, cs], vv[:, cs], do[:, cs]
            acc = jnp.zeros((ATT_QB, LANES), F32)
            for sub in range(2):
                h = 2 * p + sub
                s = _dot_nt(jnp.where(masks[sub], qt, 0), kt) + b_ref[h]
                pr = jnp.exp(s - lse_all[:, h:h + 1])
                dp = _dot_nt(jnp.where(masks[sub], dot_, 0), vt)
                dl = jnp.sum(jnp.where(masks[sub], do_o[:, cs], 0.0), axis=-1, keepdims=True)
                ds = pr * (dp - dl)
                acc = acc + _dot(ds.astype(BF16), jnp.where(masks[sub], kt, 0)) * scale
                dlt_cols = dlt_cols + jnp.where(lane == h, dl, 0.0)
                sl = jnp.zeros((SUBLANES, ATT_VEC), F32)
                for a in range(nslab):
                    piece = jnp.concatenate([ds[a * SUBLANES:(a + 1) * SUBLANES, :], zpad], axis=1)
                    sl = sl + (piece if a == 0 else pltpu.roll(piece, ATT_VEC - SUBLANES * a, 1))
                slab_ref[h] += sl
            dq_ref[:, cs] = acc.astype(BF16)
        dlt_ref[...] = dlt_cols.T[0:SUBLANES, :]

    def kv_spec(off, cb):
        return pl.BlockSpec((ATT_QB, PB), lambda i: (jnp.maximum(i + off, 0), cb))

    return pl.pallas_call(
        body, name="att_dq", grid=(nb,),
        in_specs=[pl.BlockSpec((ATT_QB, PB), lambda i: (i, CB_QB)),
                  kv_spec(-2, CB_KB), kv_spec(-1, CB_KB), kv_spec(0, CB_KB),
                  kv_spec(-2, CB_VB), kv_spec(-1, CB_VB), kv_spec(0, CB_VB),
                  pl.BlockSpec((None, ATT_HEADS, ATT_QB, ATT_KW), lambda i: (jnp.minimum(i, 2), 0, 0, 0)),
                  pl.BlockSpec((ATT_QB, LANES), lambda i: (i, 0)),
                  pl.BlockSpec((ATT_QB, WIDTH_B), lambda i: (i, 0)), pl.BlockSpec((ATT_QB, WIDTH_B), lambda i: (i, 0)),
                  pl.BlockSpec(memory_space=pl.ANY)],
        out_specs=[_dp_spec(ATT_QB, DP_QB),
                   pl.BlockSpec((SUBLANES, ATT_QB), lambda i: (0, i)),
                   pl.BlockSpec((ATT_HEADS, SUBLANES, ATT_VEC), lambda i: (0, 0, 0))],
        out_shape=[jax.ShapeDtypeStruct(dproj.shape, dproj.dtype), jax.ShapeDtypeStruct((SUBLANES, t), F32),
                   jax.ShapeDtypeStruct((ATT_HEADS, SUBLANES, ATT_VEC), F32)],
        input_output_aliases={11: 0},
        compiler_params=_cparams("arbitrary"),
    )(proj, proj, proj, proj, proj, proj, proj, bias_q, lse, d_ob, ob, dproj)


def _att_dkv(proj, bias_k, lse_t, dlt_t, d_ob, dproj):
    t = proj.shape[0]
    nb = t // ATT_QB
    scale = ATT_DH ** -0.5

    def body(k_ref, v_ref, q0_ref, q1_ref, q2_ref, d0_ref, d1_ref, d2_ref, l0_ref, l1_ref, l2_ref,
             e0_ref, e1_ref, e2_ref, b_ref, dp_in_ref, dkv_ref):
        i = pl.program_id(0)
        k = k_ref[...].astype(BF16)
        v = v_ref[...].astype(BF16)
        qq = (jnp.concatenate([q0_ref[...], q1_ref[...], q2_ref[...]], axis=0) * scale).astype(BF16)
        do = jnp.concatenate([d0_ref[...], d1_ref[...], d2_ref[...]], axis=0).astype(BF16)
        lse = jnp.concatenate([l0_ref[...], l1_ref[...], l2_ref[...]], axis=1)
        dlt = jnp.concatenate([e0_ref[...], e1_ref[...], e2_ref[...]], axis=1)
        masks = _head_masks()
        for p in range(ATT_HEADS // 2):
            cs = slice(p * LANES, (p + 1) * LANES)
            kt, vt, qt, dot_ = k[:, cs], v[:, cs], qq[:, cs], do[:, cs]
            acc_k = jnp.zeros((ATT_QB, LANES), F32)
            acc_v = jnp.zeros((ATT_QB, LANES), F32)
            for sub in range(2):
                h = 2 * p + sub
                st = _dot_nt(jnp.where(masks[sub], kt, 0), qt) + b_ref[h]
                pt = jnp.exp(st - lse[h:h + 1, :])
                dot_m = jnp.where(masks[sub], dot_, 0)
                acc_v = acc_v + _dot(pt.astype(BF16), dot_m)
                dpt = _dot_nt(jnp.where(masks[sub], vt, 0), dot_)
                dst = pt * (dpt - dlt[h:h + 1, :])
                acc_k = acc_k + _dot(dst.astype(BF16), jnp.where(masks[sub], qt, 0))
            dkv_ref[:, cs] = acc_k.astype(BF16)
            dkv_ref[:, WIDTH_B + p * LANES:WIDTH_B + (p + 1) * LANES] = acc_v.astype(BF16)

    def q_spec(off, cb):
        return pl.BlockSpec((ATT_QB, PB), lambda i: (jnp.minimum(i + off, nb - 1), cb))

    def d_spec(off):
        return pl.BlockSpec((ATT_QB, WIDTH_B), lambda i: (jnp.minimum(i + off, nb - 1), 0))

    def r_spec(off):
        return pl.BlockSpec((SUBLANES, ATT_QB), lambda i: (0, jnp.minimum(i + off, nb - 1)))

    row = pl.BlockSpec((ATT_QB, WIDTH_B), lambda i: (i, 0))
    return pl.pallas_call(
        body, name="att_dkv", grid=(nb,),
        in_specs=[pl.BlockSpec((ATT_QB, PB), lambda i: (i, CB_KB)), pl.BlockSpec((ATT_QB, PB), lambda i: (i, CB_VB)),
                  q_spec(0, CB_QB), q_spec(1, CB_QB), q_spec(2, CB_QB),
                  d_spec(0), d_spec(1), d_spec(2), r_spec(0), r_spec(1), r_spec(2),
                  r_spec(0), r_spec(1), r_spec(2),
                  pl.BlockSpec((None, ATT_HEADS, ATT_QB, ATT_KW), lambda i: (jnp.minimum(nb - 1 - i, 2), 0, 0, 0)),
                  pl.BlockSpec(memory_space=pl.ANY)],
        out_specs=_dp_spec(ATT_QB, DP_KVB),
        out_shape=jax.ShapeDtypeStruct(dproj.shape, dproj.dtype),
        input_output_aliases={15: 0},
        compiler_params=_cparams("parallel"),
    )(proj, proj, proj, proj, proj, d_ob, d_ob, d_ob, lse_t, lse_t, lse_t, dlt_t, dlt_t, dlt_t, bias_k, dproj)


def _relbias_grad(slabs):
    ohq, _ = _bias_onehots()

    def body(s_ref, oh_ref, o_ref):
        sv = s_ref[...]
        vec = sv[0:1, :]
        for b in range(1, SUBLANES):
            vec = vec + pltpu.roll(sv[b:b + 1, :], ATT_VEC - b, 1)
        o_ref[...] = _dot_nt(jnp.broadcast_to(vec, (SUBLANES, ATT_VEC)), oh_ref[...], HIGHEST)[0:1, :]

    h = slabs.shape[0]
    return pl.pallas_call(
        body, name="att_dbias", grid=(h,),
        in_specs=[pl.BlockSpec((None, SUBLANES, ATT_VEC), lambda i: (i, 0, 0)),
                  pl.BlockSpec((3 * LANES, ATT_VEC), lambda i: (0, 0))],
        out_specs=pl.BlockSpec((None, 1, 3 * LANES), lambda i: (i, 0, 0)),
        out_shape=jax.ShapeDtypeStruct((h, 1, 3 * LANES), F32),
        compiler_params=_cparams("parallel"),
    )(slabs, jnp.asarray(ohq))


GDN_TM = 512
GDN_CB = 8
HALO = SUBLANES


def _conv_taps(ext, width, lead, n):
    return [(ext if k == width - 1 else pltpu.roll(ext, width - 1 - k, 0))[lead:lead + n] for k in range(width)]


def _next_halo_spec(tm, width, cb, t):
    return pl.BlockSpec((HALO, width), lambda i: (jnp.minimum((i + 1) * (tm // HALO), t // HALO - 1), cb))


def _in_proj_prep(h1, w_all, conv_w, carry):
    t = h1.shape[0]
    tm = GDN_TM
    nt = t // tm
    arrays, scatter = carry
    nx = len(arrays)
    c0 = CB_QA * PB

    def body(*refs):
        h_ref, w_ref, cw_ref = refs[:3]
        srcs = refs[3:3 + nx]
        proj_ref, qn_ref, kn_ref, vo_ref, y_ref = refs[3 + nx:8 + nx]
        dsts = refs[8 + nx:8 + 2 * nx]
        prev_ref = refs[8 + 2 * nx]
        i = pl.program_id(0)
        local, remote = _exchange_copies(srcs, dsts, scatter, *refs[9 + 2 * nx:])

        @pl.when(i == 0)
        def _():
            for cp in local + remote:
                cp.start()
            prev_ref[...] = jnp.zeros_like(prev_ref)

        proj = _dot(h_ref[...], w_ref[...])
        proj_ref[...] = proj
        xin = proj[:, c0:c0 + 3 * KEY_A]
        ext = jnp.concatenate([prev_ref[...], xin], axis=0)
        prev_ref[...] = xin[tm - HALO:tm]
        taps = _conv_taps(ext, GDN_CONV, HALO, tm)
        cw = cw_ref[...]
        y = sum(cw[k:k + 1, :] * taps[k] for k in range(GDN_CONV))
        y_ref[...] = y
        a = y * _sigmoid(y)
        for idx, o_ref in enumerate((qn_ref, kn_ref)):
            for h in range(GDN_HEADS):
                cs = slice(h * GDN_DK, (h + 1) * GDN_DK)
                seg = a[:, idx * KEY_A + h * GDN_DK:idx * KEY_A + (h + 1) * GDN_DK]
                o_ref[:, cs] = seg * lax.rsqrt(jnp.sum(seg * seg, axis=-1, keepdims=True) + EPS)
        vo_ref[...] = a[:, 2 * KEY_A:]

        @pl.when(i == nt - 1)
        def _():
            for cp in remote + local:
                cp.wait()

    any_spec = pl.BlockSpec(memory_space=pl.ANY)
    row = pl.BlockSpec((tm, KEY_A), lambda i: (i, 0))
    return pl.pallas_call(
        body, name="in_proj", grid=(nt,),
        in_specs=[pl.BlockSpec((tm, D_MODEL), lambda i: (i, 0)), _resident((D_MODEL, PROJ_W)),
                  _resident((GDN_CONV, 3 * KEY_A))] + [any_spec] * nx,
        out_specs=[pl.BlockSpec((tm, PROJ_W), lambda i: (i, 0)), row, row, row,
                   pl.BlockSpec((tm, 3 * KEY_A), lambda i: (i, 0))] + [any_spec] * nx,
        out_shape=[jax.ShapeDtypeStruct((t, PROJ_W), F32)] + [jax.ShapeDtypeStruct((t, KEY_A), F32)] * 3
        + [jax.ShapeDtypeStruct((t, 3 * KEY_A), F32)] + _exchange_shapes(arrays, scatter),
        scratch_shapes=[pltpu.VMEM((HALO, 3 * KEY_A), F32)] + _exchange_sems(nx),
        compiler_params=_cparams("arbitrary"),
    )(h1, w_all, conv_w, *arrays)


def _gdn_prep_bwd(proj, ycv, conv_w, dqn, dkn, dv, dproj):
    t = proj.shape[0]
    tm = GDN_TM
    nt = t // tm
    n_ext = tm + HALO

    def body(q_ref, k_ref, v_ref, y_ref, ny_ref, dq_ref, dk_ref, dv_ref, ndq_ref, ndk_ref, ndv_ref, w_ref, dp_in_ref,
             out_ref, dw_ref):
        i = pl.program_id(0)
        last = i == nt - 1

        @pl.when(i == 0)
        def _():
            dw_ref[...] = jnp.zeros_like(dw_ref)

        groups = ((q_ref, dq_ref, ndq_ref), (k_ref, dk_ref, ndk_ref), (v_ref, dv_ref, ndv_ref))
        for idx, (x_ref, d_ref, nd_ref) in enumerate(groups):
            cs_all = slice(idx * KEY_A, (idx + 1) * KEY_A)
            w = w_ref[:, cs_all]
            y = jnp.concatenate([y_ref[:, cs_all], jnp.where(last, 0.0, ny_ref[:, cs_all])], axis=0)
            sg = _sigmoid(y)
            a = y * sg
            dup = jnp.concatenate([d_ref[...], jnp.where(last, 0.0, nd_ref[...])], axis=0)
            if idx < 2:
                segs = []
                for h in range(GDN_HEADS):
                    cs = slice(h * GDN_DK, (h + 1) * GDN_DK)
                    seg = a[:, cs]
                    r = lax.rsqrt(jnp.sum(seg * seg, axis=-1, keepdims=True) + EPS)
                    nrm = seg * r
                    dn = dup[:, cs]
                    segs.append(r * (dn - nrm * jnp.sum(dn * nrm, axis=-1, keepdims=True)))
                da = jnp.concatenate(segs, axis=1)
            else:
                da = dup
            dy = da * sg * (1.0 + y * (1.0 - sg))
            xv = x_ref[...]
            dx = None
            for k in range(GDN_CONV):
                shift = GDN_CONV - 1 - k
                tap = (dy if shift == 0 else pltpu.roll(dy, n_ext - shift, 0))[:tm]
                term = w[k:k + 1, :] * tap
                dx = term if dx is None else dx + term
                dw_ref[k:k + 1, cs_all] += jnp.sum(tap * xv, axis=0, keepdims=True)
            out_ref[:, cs_all] = dx.astype(BF16)

    row = pl.BlockSpec((tm, KEY_A), lambda i: (i, 0))
    nrow = _next_halo_spec(tm, KEY_A, 0, t)
    return pl.pallas_call(
        body, name="gdn_prep_bwd", grid=(nt,),
        in_specs=[pl.BlockSpec((tm, PB), lambda i: (i, CB_QA)), pl.BlockSpec((tm, PB), lambda i: (i, CB_KA)),
                  pl.BlockSpec((tm, PB), lambda i: (i, CB_VA)),
                  pl.BlockSpec((tm, 3 * KEY_A), lambda i: (i, 0)), _next_halo_spec(tm, 3 * KEY_A, 0, t),
                  row, row, row, nrow, nrow, nrow,
                  pl.BlockSpec((GDN_CONV, 3 * KEY_A), lambda i: (0, 0)), pl.BlockSpec(memory_space=pl.ANY)],
        out_specs=[_dp_spec(tm, DP_QKVA), pl.BlockSpec((SUBLANES, 3 * KEY_A), lambda i: (0, 0))],
        out_shape=[jax.ShapeDtypeStruct(dproj.shape, dproj.dtype), jax.ShapeDtypeStruct((SUBLANES, 3 * KEY_A), F32)],
        input_output_aliases={12: 0},
        compiler_params=_cparams("arbitrary"),
    )(proj, proj, proj, ycv, ycv, dqn, dkn, dv, dqn, dkn, dv, conv_w, dproj)


class _Pair(dict):
    __getattr__ = dict.__getitem__
    __setattr__ = dict.__setitem__


def _pairs_to_lanes(cols):
    lane = lax.broadcasted_iota(jnp.int32, (1, LANES), 1)
    out = jnp.zeros((cols[0].shape[0], LANES), F32)
    for p, col in enumerate(cols):
        out = out + jnp.where(lane == p, col, 0.0)
    return out


def _gdn_terms(bd, par, kn_ref, qn_ref):
    c = CHUNK
    ii = lax.broadcasted_iota(jnp.int32, (c, c), 0)
    jj = lax.broadcasted_iota(jnp.int32, (c, c), 1)
    strict, incl = ii > jj, ii >= jj
    ltri = incl.astype(F32)
    ts = []
    for cc in range(GDN_CB):
        for h in range(GDN_HEADS):
            t = _Pair(cc=cc, h=h, rows=slice(cc * c, (cc + 1) * c), cs=slice(h * GDN_DK, (h + 1) * GDN_DK),
                      strict=strict, incl=incl)
            t.beta = _sigmoid(bd[t.rows, h:h + 1])
            t.ea = jnp.exp(par[0:1, h:h + 1])
            t.sp_arg = bd[t.rows, GDN_HEADS + h:GDN_HEADS + h + 1] + par[1:2, h:h + 1]
            t.g = -t.ea * _softplus(t.sp_arg)
            t.k = kn_ref[t.rows, t.cs]
            t.q = qn_ref[t.rows, t.cs] * (GDN_DK ** -0.5)
            t.kb, t.qb = t.k.astype(BF16), t.q.astype(BF16)
            ts.append(t)
    gall = _dot(ltri, _pairs_to_lanes([t.g for t in ts]), HIGHEST)
    gall_t = gall.T
    for p, t in enumerate(ts):
        t.gb = jnp.broadcast_to(gall[:, p:p + 1], (c, GDN_DK))
    for t in ts:
        t.kk = _dot_nt(t.kb, t.kb)
        t.qk = _dot_nt(t.qb, t.kb)
    for p, t in enumerate(ts):
        diff = t.gb[:, :c] - gall_t[p:p + 1, :]
        t.dec_s = jnp.exp(jnp.where(strict, diff, NEG))
        t.dec_i = jnp.exp(jnp.where(incl, diff, NEG))
        t.gam = jnp.exp(t.gb)
        glast = t.gb[c - 1:c, :]
        t.e_rest = jnp.exp(glast - t.gb)
        t.gl = jnp.exp(glast)
        t.p = t.qk * t.dec_i
    return ts


def _gdn_fwd(qn, kn, v, proj, par, gnw):
    t = qn.shape[0]
    c = CHUNK
    nc = t // c
    r_ = GDN_CB * c

    def body(qn_ref, kn_ref, v_ref, bd_ref, z_ref, par_ref, gnw_ref,
             oan_ref, o_ref, sp_ref, w_ref, u_ref, tm_ref, s_ref):
        @pl.when(pl.program_id(0) == 0)
        def _():
            s_ref[...] = jnp.zeros_like(s_ref)

        bd, par, gnw_v = bd_ref[...], par_ref[...], gnw_ref[...]
        eye = (lax.broadcasted_iota(jnp.int32, (c, c), 0) == lax.broadcasted_iota(jnp.int32, (c, c), 1)).astype(F32)
        ts = _gdn_terms(bd, par, kn_ref, qn_ref)
        for t in ts:
            t.vv = v_ref[t.rows, t.cs]
            t.x = -(t.beta * t.kk * t.dec_s)
            t.tinv = eye + t.x
        for t in ts:
            t.xs = _split(t.x)
        for _ in range(5):
            for t in ts:
                t.xs = _split(_dot3s(t.xs, t.xs))
            for t in ts:
                t.tinv = t.tinv + _dot3s(_split(t.tinv), t.xs)
        for t in ts:
            tsp = _split(t.tinv)
            t.wm = _dot3s(tsp, _split((t.beta * t.gam) * t.k))
            t.uv = _dot3s(tsp, _split(t.beta * t.vv))
        for t in ts:
            w_ref[t.rows, t.cs] = t.wm
            tm_ref[t.cc, t.h] = t.tinv.T
            t.wb = t.wm.astype(BF16)
            t.qgb = (t.q * t.gam).astype(BF16)
            t.kdb = (t.k * t.e_rest).astype(BF16)
            t.pb = t.p.astype(BF16)
        state = [s_ref[h] for h in range(GDN_HEADS)]
        for cc in range(GDN_CB):
            tc = [t for t in ts if t.cc == cc]
            for t in tc:
                t.sh = state[t.h]
                t.sb = t.sh.astype(BF16)
            for t in tc:
                t.ws = _dot(t.wb, t.sb)
            for t in tc:
                t.u = t.uv - t.ws
                t.ub = t.u.astype(BF16)
            for t in tc:
                state[t.h] = t.gl * t.sh + _dot_tn(t.kdb, t.ub)
            for t in tc:
                t.o = _dot(t.qgb, t.sb) + _dot(t.pb, t.ub)
                sp_ref[cc, t.h] = t.sh
                u_ref[t.rows, t.cs] = t.u
                o_ref[t.rows, t.cs] = t.o
        for h in range(GDN_HEADS):
            s_ref[h] = state[h]
        for t in ts:
            zz = z_ref[t.rows, t.cs]
            rr = lax.rsqrt(jnp.mean(t.o * t.o, axis=-1, keepdims=True) + EPS)
            oan_ref[t.rows, t.cs] = ((t.o * rr) * gnw_v * (zz * _sigmoid(zz))).astype(BF16)

    row = pl.BlockSpec((r_, KEY_A), lambda i: (i, 0))
    return pl.pallas_call(
        body, name="gdn_fwd", grid=(nc // GDN_CB,),
        in_specs=[row, row, row, pl.BlockSpec((r_, LANES), lambda i: (i, CB_BD)),
                  pl.BlockSpec((r_, PB), lambda i: (i, CB_ZA)),
                  pl.BlockSpec((SUBLANES, LANES), lambda i: (0, 0)), pl.BlockSpec((1, GDN_DK), lambda i: (0, 0))],
        out_specs=[row, row, pl.BlockSpec((GDN_CB, GDN_HEADS, GDN_DK, GDN_DK), lambda i: (i, 0, 0, 0)),
                   row, row, pl.BlockSpec((GDN_CB, GDN_HEADS, c, c), lambda i: (i, 0, 0, 0))],
        out_shape=[jax.ShapeDtypeStruct((t, KEY_A), BF16), jax.ShapeDtypeStruct((t, KEY_A), F32),
                   jax.ShapeDtypeStruct((nc, GDN_HEADS, GDN_DK, GDN_DK), F32),
                   jax.ShapeDtypeStruct((t, KEY_A), F32), jax.ShapeDtypeStruct((t, KEY_A), F32),
                   jax.ShapeDtypeStruct((nc, GDN_HEADS, c, c), F32)],
        scratch_shapes=[pltpu.VMEM((GDN_HEADS, GDN_DK, GDN_DK), F32)],
        compiler_params=_cparams("arbitrary"),
    )(qn, kn, v, proj, proj, par, gnw)


def _gdn_bwd(qn, kn, v, proj, par, gnw, o, sprev, wst, ust, tst, d_oan, dproj):
    t = qn.shape[0]
    c = CHUNK
    nc = t // c
    nb = nc // GDN_CB
    r_ = GDN_CB * c

    def body(qn_ref, kn_ref, v_ref, bd_ref, z_ref, par_ref, gnw_ref, o_ref, sp_ref, w_ref, u_ref, tm_ref, do_ref,
             dp_in_ref, dqn_ref, dkn_ref, dv_ref, dzb_ref, acc_ref, ds_ref):
        @pl.when(pl.program_id(0) == 0)
        def _():
            ds_ref[...] = jnp.zeros_like(ds_ref)
            acc_ref[...] = jnp.zeros_like(acc_ref)

        bd, par, gnw_v = bd_ref[...], par_ref[...], gnw_ref[...]
        lane = lax.broadcasted_iota(jnp.int32, (1, LANES), 1)
        rix = lax.broadcasted_iota(jnp.int32, (c, 1), 0)
        ii = lax.broadcasted_iota(jnp.int32, (c, c), 0)
        jj = lax.broadcasted_iota(jnp.int32, (c, c), 1)
        upper = (jj >= ii).astype(F32)
        acc_a = jnp.zeros((1, LANES), F32)
        acc_d = jnp.zeros((1, LANES), F32)
        acc_g = jnp.zeros((1, LANES), F32)
        ts = _gdn_terms(bd, par, kn_ref, qn_ref)
        for t in ts:
            t.vv = v_ref[t.rows, t.cs]
            t.sh = sp_ref[t.cc, t.h]
            t.sb = t.sh.astype(BF16)
            t.wm, t.u, t.tinv_t = w_ref[t.rows, t.cs], u_ref[t.rows, t.cs], tm_ref[t.cc, t.h]
            t.wb, t.ub = t.wm.astype(BF16), t.u.astype(BF16)
            ov, zz, dout = o_ref[t.rows, t.cs], z_ref[t.rows, t.cs], do_ref[t.rows, t.cs]
            sg = _sigmoid(zz)
            sil = zz * sg
            rr = lax.rsqrt(jnp.mean(ov * ov, axis=-1, keepdims=True) + EPS)
            on = ov * rr
            dzb_ref[t.rows, t.cs] = (dout * on * gnw_v * (sg * (1.0 + zz * (1.0 - sg)))).astype(BF16)
            acc_g = acc_g + jnp.sum(dout * on * sil, axis=0, keepdims=True)
            don = dout * gnw_v * sil
            t.dob = (rr * (don - on * jnp.mean(don * on, axis=-1, keepdims=True))).astype(BF16)
            t.qg = t.q * t.gam
            t.kd = t.k * t.e_rest
            t.qgb, t.kdb = t.qg.astype(BF16), t.kd.astype(BF16)
            t.ptb = t.p.T.astype(BF16)
        for t in ts:
            t.du0 = _dot(t.ptb, t.dob)
            t.ds0 = _dot_tn(t.qgb, t.dob)
            t.dqg = _dot_nt(t.dob, t.sb)
            t.dp = _dot_nt(t.dob, t.ub)
            t.uv = t.u + _dot(t.wb, t.sb)
        dstate = [ds_ref[h] for h in range(GDN_HEADS)]
        for cc in reversed(range(GDN_CB)):
            tc = [t for t in ts if t.cc == cc]
            for t in tc:
                t.dsn = dstate[t.h]
                t.dsnb = t.dsn.astype(BF16)
            for t in tc:
                t.du = t.du0 + _dot(t.kdb, t.dsnb)
            for t in tc:
                t.dub = t.du.astype(BF16)
            for t in tc:
                dstate[t.h] = t.gl * t.dsn + t.ds0 - _dot_tn(t.wb, t.dub)
            for t in tc:
                t.dkd = _dot_nt(t.ub, t.dsnb)
                t.dgl = jnp.sum(jnp.sum(t.dsn * t.sh, axis=1, keepdims=True), axis=0, keepdims=True)
                t.dwm = -_dot_nt(t.dub, t.sb)
        for h in range(GDN_HEADS):
            ds_ref[h] = dstate[h]
        for t in ts:
            tsp = _split(t.tinv_t)
            t.dbk = _dot3s(tsp, _split(t.dwm))
            t.dbv = _dot3s(tsp, _split(t.du))
        for t in ts:
            d_a = -(_dot_nt(t.dbk.astype(BF16), t.wb) + _dot_nt(t.dbv.astype(BF16), t.uv.astype(BF16)))
            t.d_a = jnp.where(t.strict, d_a, 0.0)
        for t in ts:
            t.dkk = t.d_a * t.beta * t.dec_s
            t.dqk = t.dp * t.dec_i
            t.dqkb = t.dqk.astype(BF16)
        for t in ts:
            t.dq = _dot(t.dqkb, t.kb) + t.dqg * t.gam
            t.dk = (t.dbk * (t.beta * t.gam) + _dot_tn(t.dqkb, t.qb) + _dot((t.dkk + t.dkk.T).astype(BF16), t.kb)
                    + t.dkd * t.e_rest)
        for t in ts:
            dbeta = (jnp.sum(t.d_a * t.kk * t.dec_s, axis=-1, keepdims=True)
                     + jnp.sum(t.dbk * t.k * t.gam, axis=-1, keepdims=True) + jnp.sum(t.dbv * t.vv, axis=-1, keepdims=True))
            t.dbl = dbeta * t.beta * (1.0 - t.beta)
            dv_ref[t.rows, t.cs] = t.dbv * t.beta
            bk = (t.beta * t.gam) * t.k
            zc = jnp.sum(t.dkd * t.kd, axis=-1, keepdims=True)
            xs = t.dkk * t.kk + t.dp * t.p
            dgc = (jnp.sum(xs, axis=-1, keepdims=True) - jnp.sum(xs.T, axis=-1, keepdims=True)
                   + jnp.sum(t.dbk * bk, axis=-1, keepdims=True) + jnp.sum(t.dqg * t.qg, axis=-1, keepdims=True) - zc)
            dglast = jnp.sum(zc, axis=0, keepdims=True) + t.dgl * t.gl[:, 0:1]
            t.dgc = dgc + jnp.where(rix == c - 1, dglast, 0.0)
        dgall = _dot(upper, _pairs_to_lanes([t.dgc for t in ts]), HIGHEST)
        for p, t in enumerate(ts):
            t.dg = dgall[:, p:p + 1]
        dbd_tiles = [jnp.zeros((c, LANES), F32) for _ in range(GDN_CB)]
        for t in ts:
            ddl = t.dg * (-t.ea) * _sigmoid(t.sp_arg)
            acc_a = acc_a + jnp.where(lane == t.h, jnp.sum(t.dg * t.g, axis=0, keepdims=True), 0.0)
            acc_d = acc_d + jnp.where(lane == t.h, jnp.sum(ddl, axis=0, keepdims=True), 0.0)
            dbd_tiles[t.cc] = (dbd_tiles[t.cc] + jnp.where(lane == t.h, t.dbl, 0.0)
                               + jnp.where(lane == GDN_HEADS + t.h, ddl, 0.0))
            dqn_ref[t.rows, t.cs] = t.dq * (GDN_DK ** -0.5)
            dkn_ref[t.rows, t.cs] = t.dk
        for cc in range(GDN_CB):
            dzb_ref[cc * c:(cc + 1) * c, KEY_A:KEY_A + LANES] = dbd_tiles[cc].astype(BF16)
        acc_ref[0:1, :] += acc_a
        acc_ref[1:2, :] += acc_d
        acc_ref[2:3, :] += acc_g

    def rev(i):
        return nb - 1 - i

    row = pl.BlockSpec((r_, KEY_A), lambda i: (rev(i), 0))
    st = pl.BlockSpec((GDN_CB, GDN_HEADS, GDN_DK, GDN_DK), lambda i: (rev(i), 0, 0, 0))
    tt_spec = pl.BlockSpec((GDN_CB, GDN_HEADS, c, c), lambda i: (rev(i), 0, 0, 0))
    return pl.pallas_call(
        body, name="gdn_bwd", grid=(nb,),
        in_specs=[row, row, row, pl.BlockSpec((r_, LANES), lambda i: (rev(i), CB_BD)),
                  pl.BlockSpec((r_, PB), lambda i: (rev(i), CB_ZA)),
                  pl.BlockSpec((SUBLANES, LANES), lambda i: (0, 0)), pl.BlockSpec((1, GDN_DK), lambda i: (0, 0)),
                  row, st, row, row, tt_spec, row, pl.BlockSpec(memory_space=pl.ANY)],
        out_specs=[row, row, row, _dp_spec(r_, DP_ZBD, rev), pl.BlockSpec((SUBLANES, LANES), lambda i: (0, 0))],
        out_shape=[jax.ShapeDtypeStruct((t, KEY_A), F32)] * 3 + [jax.ShapeDtypeStruct(dproj.shape, dproj.dtype),
                                                                jax.ShapeDtypeStruct((SUBLANES, LANES), F32)],
        input_output_aliases={13: 3},
        scratch_shapes=[pltpu.VMEM((GDN_HEADS, GDN_DK, GDN_DK), F32)],
        compiler_params=_cparams("arbitrary"),
    )(qn, kn, v, proj, proj, par, gnw, o, sprev, wst, ust, tst, d_oan, dproj)


def _merge_bwd(dx2b, oan, ob, proj, wba, wbb, wout_t, wba_t, wbb_t, tm=512):
    t = dx2b.shape[0]

    def body(dx_ref, oa_ref, ob_ref, ga_ref, gb_ref, wba_ref, wbb_ref, woutt_ref, wbat_ref, wbbt_ref,
             dg_ref, doa_ref, dob_ref, gout_ref, gba_ref, gbb_ref):
        @pl.when(pl.program_id(0) == 0)
        def _():
            gout_ref[...] = jnp.zeros_like(gout_ref)
            gba_ref[...] = jnp.zeros_like(gba_ref)
            gbb_ref[...] = jnp.zeros_like(gbb_ref)

        dx, oa, ob = dx_ref[...], oa_ref[...], ob_ref[...]
        dmix = _dot(dx, woutt_ref[...])
        ya = _dot(oa, wba_ref[...])
        yb = _dot(ob, wbb_ref[...])
        sa, sb = _sigmoid(ga_ref[...]), _sigmoid(gb_ref[...])
        gout_ref[...] += _dot_tn((sa * ya + sb * yb).astype(BF16), dx)
        dg_ref[:, :D_MODEL] = (dmix * ya * sa * (1.0 - sa)).astype(BF16)
        dg_ref[:, D_MODEL:] = (dmix * yb * sb * (1.0 - sb)).astype(BF16)
        dya = (dmix * sa).astype(BF16)
        dyb = (dmix * sb).astype(BF16)
        gba_ref[...] += _dot_tn(oa, dya)
        gbb_ref[...] += _dot_tn(ob, dyb)
        doa_ref[...] = _dot(dya, wbat_ref[...])
        dob_ref[...] = _dot(dyb, wbbt_ref[...])

    half = pl.BlockSpec((tm, KEY_A), lambda i: (i, 0))
    row = pl.BlockSpec((tm, D_MODEL), lambda i: (i, 0))
    wsmall = pl.BlockSpec((KEY_A, D_MODEL), lambda i: (0, 0))
    wsmall_t = pl.BlockSpec((D_MODEL, KEY_A), lambda i: (0, 0))
    wfull = pl.BlockSpec((D_MODEL, D_MODEL), lambda i: (0, 0))
    return pl.pallas_call(
        body, name="merge_bwd", grid=(t // tm,),
        in_specs=[row, half, half, pl.BlockSpec((tm, D_MODEL), lambda i: (i, CB_GA)),
                  pl.BlockSpec((tm, D_MODEL), lambda i: (i, CB_GB)), wsmall, wsmall, wfull, wsmall_t, wsmall_t],
        out_specs=[_dp_spec(tm, DP_GATES), half, half, wfull, wsmall, wsmall],
        out_shape=[jax.ShapeDtypeStruct((t, PROJ_W), BF16), jax.ShapeDtypeStruct((t, KEY_A), F32),
                   jax.ShapeDtypeStruct((t, KEY_A), F32), jax.ShapeDtypeStruct((D_MODEL, D_MODEL), F32),
                   jax.ShapeDtypeStruct((KEY_A, D_MODEL), F32), jax.ShapeDtypeStruct((WIDTH_B, D_MODEL), F32)],
        compiler_params=_cparams("arbitrary"),
    )(dx2b, oan, ob, proj, proj, wba, wbb, wout_t, wba_t, wbb_t)


FFN_TM = 128
FFN_W = 2 * D_FF


def _resident(shape):
    return pl.BlockSpec(shape, lambda i: (0,) * len(shape), pipeline_mode=pl.Buffered(1))


def _ffn_fwd(h2, wup, cw, cb, wdown, x2, tgt, w3):
    t = x2.shape[0]
    tm = FFN_TM

    def body(h2_ref, wup_ref, cw_ref, cb_ref, wd_ref, x2_ref, tgt_ref, w3_ref, up_ref, u_ref, dx_ref, dxb_ref, act_ref,
             acc_ref, prev_ref):
        @pl.when(pl.program_id(0) == 0)
        def _():
            acc_ref[...] = jnp.zeros_like(acc_ref)
            prev_ref[...] = jnp.zeros_like(prev_ref)

        up = _dot(h2_ref[...], wup_ref[...])
        up_ref[...] = up
        ext = jnp.concatenate([prev_ref[...], up], axis=0)
        prev_ref[...] = up[tm - HALO:tm]
        taps = _conv_taps(ext, FFN_CONV, HALO, tm)
        cw_v = cw_ref[...]
        u = sum(cw_v[k:k + 1, :] * taps[k] for k in range(FFN_CONV)) + cb_ref[...]
        u_ref[...] = u
        gate, upp = u[:, :D_FF], u[:, D_FF:]
        act = (gate * _sigmoid(gate) * upp).astype(BF16)
        act_ref[...] = act
        x3 = x2_ref[...] + _dot(act, wd_ref[...])
        r = lax.rsqrt(jnp.mean(x3 * x3, axis=-1, keepdims=True) + EPS)
        xh = x3 * r
        w3v = w3_ref[...]
        err = xh * w3v - tgt_ref[...]
        loss = 0.5 * jnp.sum(jnp.mean(err * err, axis=-1, keepdims=True), axis=0, keepdims=True)
        dy = err * (1.0 / D_MODEL)
        acc_ref[0:1, :] += jnp.sum(dy * xh, axis=0, keepdims=True)
        acc_ref[1:2, :] += jnp.broadcast_to(loss, (1, D_MODEL))
        dxh = dy * w3v
        dx = r * (dxh - xh * jnp.mean(dxh * xh, axis=-1, keepdims=True))
        dx_ref[...] = dx
        dxb_ref[...] = dx.astype(BF16)

    row = pl.BlockSpec((tm, D_MODEL), lambda i: (i, 0))
    return pl.pallas_call(
        body, name="ffn_fwd", grid=(t // tm,),
        in_specs=[row, _resident((D_MODEL, FFN_W)), _resident((SUBLANES, FFN_W)), _resident((1, FFN_W)),
                  _resident((D_FF, D_MODEL)), row, row, _resident((1, D_MODEL))],
        out_specs=[pl.BlockSpec((tm, FFN_W), lambda i: (i, 0)), pl.BlockSpec((tm, FFN_W), lambda i: (i, 0)), row, row,
                   pl.BlockSpec((tm, D_FF), lambda i: (i, 0)), pl.BlockSpec((SUBLANES, D_MODEL), lambda i: (0, 0))],
        out_shape=[jax.ShapeDtypeStruct((t, FFN_W), F32), jax.ShapeDtypeStruct((t, FFN_W), F32),
                   jax.ShapeDtypeStruct((t, D_MODEL), F32),
                   jax.ShapeDtypeStruct((t, D_MODEL), BF16), jax.ShapeDtypeStruct((t, D_FF), BF16),
                   jax.ShapeDtypeStruct((SUBLANES, D_MODEL), F32)],
        scratch_shapes=[pltpu.VMEM((HALO, FFN_W), F32)],
        compiler_params=_cparams("arbitrary"),
    )(h2, wup, cw, cb, wdown, x2, tgt, w3)


def _ffn_bwd(dx3b, wdown_t, up, u, cw, wup_t, x2, w2, dx3, carry):
    t = up.shape[0]
    tm = FFN_TM
    nt = t // tm
    n_ext = tm + HALO
    arrays, scatter = carry
    nx = len(arrays)

    def rev(i):
        return nt - 1 - i

    def body(*refs):
        dx_ref, wdt_ref, up_ref, u_ref, cw_ref, wupt_ref, x2_ref, w2_ref, dres_ref = refs[:9]
        srcs = refs[9:9 + nx]
        dup_ref, acc_ref, dx2_ref, dx2b_ref, dw2_ref = refs[9 + nx:14 + nx]
        dsts = refs[14 + nx:14 + 2 * nx]
        nxt_ref = refs[14 + 2 * nx]
        i = pl.program_id(0)
        local, remote = _exchange_copies(srcs, dsts, scatter, *refs[15 + 2 * nx:])

        @pl.when(i == 0)
        def _():
            for cp in local + remote:
                cp.start()
            acc_ref[...] = jnp.zeros_like(acc_ref)
            dw2_ref[...] = jnp.zeros_like(dw2_ref)
            nxt_ref[...] = jnp.zeros_like(nxt_ref)

        dact = _dot(dx_ref[...], wdt_ref[...])
        gate, upp = u_ref[:, :D_FF], u_ref[:, D_FF:]
        sg = _sigmoid(gate)
        du = jnp.concatenate([dact * upp * (sg * (1.0 + gate * (1.0 - sg))), dact * (gate * sg)], axis=1)
        acc_ref[FFN_CONV:FFN_CONV + 1, :] += jnp.sum(du, axis=0, keepdims=True)
        ext = jnp.concatenate([du, nxt_ref[...]], axis=0)
        cw_v = cw_ref[...]
        upv = up_ref[...]
        dup = None
        for k in range(FFN_CONV):
            shift = FFN_CONV - 1 - k
            tap = du if shift == 0 else pltpu.roll(ext, n_ext - shift, 0)[:tm]
            term = cw_v[k:k + 1, :] * tap
            dup = term if dup is None else dup + term
            acc_ref[k:k + 1, :] += jnp.sum(tap * upv, axis=0, keepdims=True)
        dupb = dup.astype(BF16)
        dup_ref[...] = dupb
        nxt_ref[...] = du[0:HALO]
        dhv = _dot(dupb, wupt_ref[...])
        xv = x2_ref[...]
        r = lax.rsqrt(jnp.mean(xv * xv, axis=-1, keepdims=True) + EPS)
        xh = xv * r
        dw2_ref[0:1, :] += jnp.sum(dhv * xh, axis=0, keepdims=True)
        dxh = dhv * w2_ref[...]
        dx2 = dres_ref[...] + r * (dxh - xh * jnp.mean(dxh * xh, axis=-1, keepdims=True))
        dx2_ref[...] = dx2
        dx2b_ref[...] = dx2.astype(BF16)

        @pl.when(i == nt - 1)
        def _():
            for cp in remote + local:
                cp.wait()

    wide = pl.BlockSpec((tm, FFN_W), lambda i: (rev(i), 0))
    row = pl.BlockSpec((tm, D_MODEL), lambda i: (rev(i), 0))
    any_spec = pl.BlockSpec(memory_space=pl.ANY)
    return pl.pallas_call(
        body, name="ffn_bwd", grid=(nt,),
        in_specs=[row, _resident((D_MODEL, D_FF)), wide, wide,
                  _resident((SUBLANES, FFN_W)), _resident((FFN_W, D_MODEL)), row,
                  _resident((1, D_MODEL)), row] + [any_spec] * nx,
        out_specs=[wide, pl.BlockSpec((SUBLANES, FFN_W), lambda i: (0, 0)), row, row,
                   pl.BlockSpec((SUBLANES, D_MODEL), lambda i: (0, 0))] + [any_spec] * nx,
        out_shape=[jax.ShapeDtypeStruct((t, FFN_W), BF16), jax.ShapeDtypeStruct((SUBLANES, FFN_W), F32),
                   jax.ShapeDtypeStruct((t, D_MODEL), F32), jax.ShapeDtypeStruct((t, D_MODEL), BF16),
                   jax.ShapeDtypeStruct((SUBLANES, D_MODEL), F32)] + _exchange_shapes(arrays, scatter),
        scratch_shapes=[pltpu.VMEM((HALO, FFN_W), F32)] + _exchange_sems(nx),
        compiler_params=_cparams("arbitrary"),
    )(dx3b, wdown_t, up, u, cw, wup_t, x2, w2, dx3, *arrays)


def _adamw(parts, w, m, v, name, tr):
    r, cols = w.shape

    def body(p_ref, w_ref, m_ref, v_ref, g_ref, d_ref, mo_ref, vo_ref):
        g = p_ref[0].astype(F32)
        for s in range(1, N_DEV):
            g = g + p_ref[s].astype(F32)
        mm = ADAM_B1 * m_ref[...] + (1.0 - ADAM_B1) * g
        vv = ADAM_B2 * v_ref[...] + (1.0 - ADAM_B2) * (g * g)
        m_hat = mm / (1.0 - ADAM_B1 ** ADAM_STEP)
        v_hat = vv / (1.0 - ADAM_B2 ** ADAM_STEP)
        g_ref[...] = g
        d_ref[...] = -ADAM_LR * (m_hat / (jnp.sqrt(v_hat) + ADAM_EPS) + ADAM_WD * w_ref[...])
        mo_ref[...] = mm
        vo_ref[...] = vv

    assert r % tr == 0
    row = pl.BlockSpec((tr, cols), lambda i: (i, 0))
    return pl.pallas_call(
        body, name=name, grid=(r // tr,),
        in_specs=[pl.BlockSpec((N_DEV, tr, cols), lambda i: (0, i, 0)), row, row, row],
        out_specs=[row, row, row, row],
        out_shape=[jax.ShapeDtypeStruct((r, cols), F32)] * 4,
        compiler_params=_cparams("parallel"),
    )(parts, w, m, v)


def _mesh_pos():
    return lax.axis_index("x"), lax.axis_index("y"), lax.axis_index("c")


def _peer(pos, k):
    x, y, c = pos
    return (x ^ ((k >> 2) & 1), y ^ ((k >> 1) & 1), c ^ (k & 1))


def _flat_id(pos):
    return 4 * pos[0] + 2 * pos[1] + pos[2]


def _exchange_copies(srcs, dsts, scatter, send_sems, recv_sems, loc_sems):
    pos = _mesh_pos()
    me = _flat_id(pos)
    local, remote = [], []
    for j, (src, dst) in enumerate(zip(srcs, dsts)):
        local.append(pltpu.make_async_copy(src.at[me] if scatter[j] else src, dst.at[me], loc_sems.at[j]))
        for k in range(1, N_DEV):
            to = _peer(pos, k)
            remote.append(pltpu.make_async_remote_copy(
                src_ref=src.at[_flat_id(to)] if scatter[j] else src, dst_ref=dst.at[me],
                send_sem=send_sems.at[j, k - 1], recv_sem=recv_sems.at[j, k - 1],
                device_id=to, device_id_type=pl.DeviceIdType.MESH))
    return local, remote


def _exchange_shapes(arrays, scatter):
    return [jax.ShapeDtypeStruct(a.shape if s else (N_DEV,) + a.shape, a.dtype) for a, s in zip(arrays, scatter)]


def _exchange_sems(n):
    return [pltpu.SemaphoreType.DMA((n, N_DEV - 1)), pltpu.SemaphoreType.DMA((n, N_DEV - 1)), pltpu.SemaphoreType.DMA((n,))]


def _exchange(arrays, scatter, name):
    n = len(arrays)
    any_spec = pl.BlockSpec(memory_space=pl.ANY)

    def body(*refs):
        local, remote = _exchange_copies(refs[:n], refs[n:2 * n], scatter, *refs[2 * n:])
        for cp in local + remote:
            cp.start()
        for cp in remote:
            cp.wait()
        for cp in local:
            cp.wait()

    return pl.pallas_call(
        body, name=name, in_specs=[any_spec] * n, out_specs=[any_spec] * n,
        out_shape=_exchange_shapes(arrays, scatter), scratch_shapes=_exchange_sems(n),
    )(*arrays)


def _pad_rows(a, rows):
    return jnp.pad(a, ((0, rows - a.shape[0]),) + ((0, 0),) * (a.ndim - 1))


PACK_UNIT = SUBLANES * LANES


def _pack_lanes(parts, rows):
    out = []
    for a in parts:
        f = a.reshape(-1)
        out.append(jnp.pad(f, (0, (-f.shape[0]) % PACK_UNIT)).reshape(-1, LANES))
    packed = jnp.concatenate(out, axis=0)
    assert packed.shape[0] == rows, (packed.shape, rows)
    return packed


def _unpack_lanes(buf, shapes):
    out, r0 = [], 0
    for shp in shapes:
        n = math.prod(shp)
        nr = -(-n // PACK_UNIT) * SUBLANES
        out.append(buf[r0:r0 + nr].reshape(-1)[:n].reshape(shp))
        r0 += nr
    return out


def _col_shards(g):
    r, n = g.shape
    return g.reshape(r, N_DEV, n // N_DEV).transpose(1, 0, 2)


def _col_unshard(s):
    _, r, w = s.shape
    return s.transpose(1, 0, 2).reshape(r, N_DEV * w)


def _lane_rows(flat):
    n = flat.shape[1]
    return jnp.pad(flat, ((0, 0), (0, (-n) % PACK_UNIT))).reshape(N_DEV, -1, LANES)


SMALL_ROWS = 128
WS_ROWS = 32


def kernel(x, norm_mix_w, w_in, conv_qkv_w, a_log, dt_bias, gdn_norm_w, w_branch_a, w_branch_b, rel_bias, w_out, norm_ffn_w, w_up, conv_ffn_w, conv_ffn_b, w_down, norm_final_w, loss_target, m_norm_mix_w, m_w_in, m_conv_qkv_w, m_a_log, m_dt_bias, m_gdn_norm_w, m_w_branch_a, m_w_branch_b, m_rel_bias, m_w_out, m_norm_ffn_w, m_w_up, m_conv_ffn_w, m_conv_ffn_b, m_w_down, m_norm_final_w, v_norm_mix_w, v_w_in, v_conv_qkv_w, v_a_log, v_dt_bias, v_gdn_norm_w, v_w_branch_a, v_w_branch_b, v_rel_bias, v_w_out, v_norm_ffn_w, v_w_up, v_conv_ffn_w, v_conv_ffn_b, v_w_down, v_norm_final_w):
    big_w = (w_in, w_branch_a, w_branch_b, w_out, w_up, w_down, conv_qkv_w, conv_ffn_w)
    big_m = (m_w_in, m_w_branch_a, m_w_branch_b, m_w_out, m_w_up, m_w_down, m_conv_qkv_w, m_conv_ffn_w)
    big_v = (v_w_in, v_w_branch_a, v_w_branch_b, v_w_out, v_w_up, v_w_down, v_conv_qkv_w, v_conv_ffn_w)
    small_w = (norm_mix_w, a_log, dt_bias, gdn_norm_w, rel_bias, norm_ffn_w, conv_ffn_b, norm_final_w)
    small_m = (m_norm_mix_w, m_a_log, m_dt_bias, m_gdn_norm_w, m_rel_bias, m_norm_ffn_w, m_conv_ffn_b, m_norm_final_w)
    small_v = (v_norm_mix_w, v_a_log, v_dt_bias, v_gdn_norm_w, v_rel_bias, v_norm_ffn_w, v_conv_ffn_b, v_norm_final_w)

    xs, tgt = x[0], loss_target[0]
    ws = _pack_lanes(big_w[6:], WS_ROWS)
    h1, g_in, gs = _rmsnorm_cast(xs, norm_mix_w, "norm_mix", carry=([w_in[0].astype(BF16), ws], (False, False)))
    win = _col_unshard(g_in)
    gs = gs.reshape(N_DEV, -1)
    cqkv = gs[:, :GDN_CONV * 192].reshape(N_DEV, GDN_CONV, 192).transpose(1, 0, 2).reshape(GDN_CONV, 3 * KEY_A)
    cffn = gs[:, PACK_UNIT:PACK_UNIT + FFN_CONV * 704].reshape(N_DEV, FFN_CONV, 704).transpose(1, 0, 2).reshape(FFN_CONV, FFN_W)
    cffn = _pad_rows(cffn, SUBLANES)
    w_all = jnp.concatenate([win[:, a:b] for a, b in W_IN_ORDER] + [jnp.zeros((D_MODEL, PROJ_W - D_IN), BF16)], axis=1)
    par = _pad_rows(jnp.pad(jnp.concatenate([a_log, dt_bias], axis=0), ((0, 0), (0, LANES - GDN_HEADS))), SUBLANES)
    table = jnp.pad(rel_bias[0], ((0, 0), (0, 3 * LANES - rel_bias.shape[-1]))).reshape(ATT_HEADS, 1, 3 * LANES)

    proj, qn, kn, va, ycv, g_ba, g_bb, g_out, g_up, g_down = _in_proj_prep(
        h1, w_all, cqkv, carry=([w[0].astype(BF16) for w in big_w[1:6]], (False,) * 5))
    wba, wbb, wup = _col_unshard(g_ba), _col_unshard(g_bb), _col_unshard(g_up)
    wout = g_out.reshape(D_MODEL, D_MODEL)
    wdown = g_down.reshape(D_FF, D_MODEL)
    oan, o_gdn, sprev, wst, ust, tst = _gdn_fwd(qn, kn, va, proj, par, gdn_norm_w)
    bias_q, bias_k = _att_bias(table)
    ob, lse, lse_t, x2, h2 = _att_merge_fwd(proj, bias_q, oan, xs, wba, wbb, wout, norm_ffn_w)
    up, u_ffn, dx3, dx3b, act, tail_sums = _ffn_fwd(h2, wup, cffn, conv_ffn_b, wdown, x2, tgt,
                                                    norm_final_w.reshape(1, D_MODEL))

    g_wdown = _mm_tn(act, dx3b, "dw_down", 512)
    dup, ffn_sums, dx2, dx2b, nffn_sums, r_down = _ffn_bwd(
        dx3b, wdown.T, up, u_ffn, cffn, wup.T, x2, norm_ffn_w, dx3,
        carry=([g_wdown.reshape(N_DEV, -1, D_MODEL).astype(BF16)], (True,)))
    g_wup = _mm_tn(h2, dup, "dw_up", 1408)
    dproj, d_oan, d_ob, g_wout, g_wba, g_wbb = _merge_bwd(dx2b, oan, ob, proj, wba, wbb, wout.T, wba.T, wbb.T)
    dproj, dlt_t, slabs = _att_dq(proj, bias_q, lse, d_ob, ob, dproj)
    dproj = _att_dkv(proj, bias_k, lse_t, dlt_t, d_ob, dproj)
    g_rel = _relbias_grad(slabs)[:, 0, :rel_bias.shape[-1]]
    dqn, dkn, dva, dproj, gdn_sums = _gdn_bwd(qn, kn, va, proj, par, gdn_norm_w, o_gdn, sprev, wst, ust, tst, d_oan, dproj)
    dproj, cq_sums = _gdn_prep_bwd(proj, ycv, cqkv, dqn, dkn, dva, dproj)
    g_wall, r_up = _mm_tn(h1, dproj, "dw_in", 1152, carry=([_col_shards(g_wup).astype(BF16)], (True,)))
    starts = np.cumsum([0] + [b - a for a, b in W_IN_ORDER])
    g_win = jnp.concatenate([g_wall[:, starts[i]:starts[i + 1]] for i in np.argsort([a for a, _ in W_IN_ORDER])], axis=1)
    g_conv = jnp.concatenate([_lane_rows(_col_shards(cq_sums[:GDN_CONV]).reshape(N_DEV, -1)),
                              _lane_rows(_col_shards(ffn_sums[:FFN_CONV]).reshape(N_DEV, -1))], axis=1)
    grad_x, _, nmix_sums, r_in, r_ba, r_bb, r_out, r_conv = _mm_rms_bwd(
        dproj, w_all.T, xs, norm_mix_w, dx2, "in_proj_bwd", MM_TM, 1152, carry=(
            [_col_shards(g_win).astype(BF16), _col_shards(g_wba).astype(BF16), _col_shards(g_wbb).astype(BF16),
             g_wout.reshape(N_DEV, -1, D_MODEL).astype(BF16), g_conv], (True,) * 5))

    small_g = (nmix_sums[0:1], gdn_sums[0:1, :GDN_HEADS], gdn_sums[1:2, :GDN_HEADS], gdn_sums[2:3], g_rel,
               nffn_sums[0:1], ffn_sums[FFN_CONV:FFN_CONV + 1], tail_sums[0:1], tail_sums[1:2, 0:1])
    r_small, = _exchange([_pack_lanes(small_g, SMALL_ROWS)], (False,), "all_gather_small_grads")
    recv = (r_in, r_ba, r_bb, r_out, r_up, r_down, r_conv, r_small)

    res = {}
    for i, (nm, tr) in enumerate((("w_in", 128), ("w_branch_a", KEY_A), ("w_branch_b", WIDTH_B), ("w_out", 128),
                                  ("w_up", 128), ("w_down", 176))):
        res[nm] = [o[None] for o in _adamw(recv[i], big_w[i][0], big_m[i][0], big_v[i][0], "adamw_" + nm, tr)]
    conv = _adamw(recv[6], _pack_lanes(big_w[6:], WS_ROWS), _pack_lanes(big_m[6:], WS_ROWS), _pack_lanes(big_v[6:], WS_ROWS),
                  "adamw_conv", WS_ROWS)
    conv = [_unpack_lanes(o, [w.shape for w in big_w[6:]]) for o in conv]
    res["conv_qkv_w"] = [o[0] for o in conv]
    res["conv_ffn_w"] = [o[1] for o in conv]
    small_shapes = [w.shape for w in small_w]
    zero = jnp.zeros((1,), F32)
    small = _adamw(recv[7], _pack_lanes(small_w + (zero,), SMALL_ROWS), _pack_lanes(small_m + (zero,), SMALL_ROWS),
                   _pack_lanes(small_v + (zero,), SMALL_ROWS), "adamw_replicated", SMALL_ROWS)
    small = [_unpack_lanes(o, small_shapes + [()]) for o in small]
    loss = small[0][-1]
    for j, nm in enumerate(("norm_mix_w", "a_log", "dt_bias", "gdn_norm_w", "rel_bias", "norm_ffn_w", "conv_ffn_b",
                            "norm_final_w")):
        res[nm] = [o[j] for o in small]

    names = ("norm_mix_w", "w_in", "conv_qkv_w", "a_log", "dt_bias", "gdn_norm_w", "w_branch_a", "w_branch_b", "rel_bias",
             "w_out", "norm_ffn_w", "w_up", "conv_ffn_w", "conv_ffn_b", "w_down", "norm_final_w")
    outs = [res[n][kind] for kind in range(4) for n in names]
    return (loss, grad_x[None], *outs)
```

```python
import functools
import math

import numpy as np
import jax
import jax.numpy as jnp
from jax import lax
from jax.experimental import pallas as pl
from jax.experimental.pallas import tpu as pltpu

F32, BF16 = jnp.float32, jnp.bfloat16
HIGHEST = lax.Precision.HIGHEST

N_DEV = 8
D_MODEL = 1024
CHUNK = 64
EPS = 1e-6
GDN_HEADS, GDN_DK = 4, 128
KEY_A = GDN_HEADS * GDN_DK
GDN_CONV = 4
ATT_HEADS, ATT_DH = 8, 64
WIDTH_B = ATT_HEADS * ATT_DH
ATT_BAND = 9
REL_CLIP = 128
D_FF = 2816
FFN_CONV = 3
D_IN = 5640
ADAM_LR, ADAM_B1, ADAM_B2, ADAM_EPS, ADAM_WD, ADAM_STEP = 0.001, 0.9, 0.999, 1e-08, 0.01, 10

LANES = 128
SUBLANES = 8
NEG = -1e30

PROJ_W = 5760
PB = 512
CB_GA, CB_GB = 0, 1
CB_KB, CB_VB, CB_QA, CB_KA, CB_VA, CB_QB, CB_ZA = 4, 5, 6, 7, 8, 9, 10
CB_BD = 44
DP_GATES, DP_KVB, DP_QKVA, DP_QB, DP_ZBD = (2048, 0), (1024, 2), (1536, 2), (512, 9), (640, 8)
W_IN_ORDER = ((3592, 5640), (2568, 3592), (0, 1536), (2056, 2568), (1536, 2048), (2048, 2056))

ATT_QB = 256
ATT_KW = 768
ATT_VEC = 1024


def _dot(a, b, precision=None):
    return jnp.dot(a, b, preferred_element_type=F32, precision=precision)


def _dot_nt(a, b, precision=None):
    return lax.dot_general(a, b, (((1,), (1,)), ((), ())), preferred_element_type=F32, precision=precision)


def _dot_tn(a, b):
    return lax.dot_general(a, b, (((0,), (0,)), ((), ())), preferred_element_type=F32)


def _split(a):
    hi = a.astype(BF16)
    return hi, (a - hi.astype(F32)).astype(BF16)


def _dot3s(a, b):
    return _dot(a[0], b[0]) + (_dot(a[0], b[1]) + _dot(a[1], b[0]))


def _sigmoid(x):
    return 0.5 * jnp.tanh(0.5 * x) + 0.5


def _softplus(x):
    return jnp.maximum(x, 0.0) + jnp.log(1.0 + jnp.exp(-jnp.abs(x)))


def _cparams(*sem):
    return pltpu.CompilerParams(dimension_semantics=tuple(sem))


def _dp_spec(tm, region, index=lambda i: i):
    width, cb = region
    return pl.BlockSpec((tm, width), lambda i: (index(i), cb))


def _rmsnorm_gather(x, w, arrays, name, tm=512):
    t, d = x.shape
    nt = t // tm
    nx = len(arrays)

    def body(*refs):
        x_ref, w_ref = refs[:2]
        srcs = refs[2:2 + nx]
        o_ref = refs[2 + nx]
        dsts = refs[3 + nx:3 + 2 * nx]
        send_sems, recv_sems, loc_sems = refs[3 + 2 * nx:]
        i = pl.program_id(0)
        px, py, pc = _mesh_pos()
        me, sibling = (px, py, pc), (px, py, 1 - pc)
        chips = [(1 - px, py), (px, 1 - py), (1 - px, 1 - py)]

        def copy(j, k, src, block, to):
            return pltpu.make_async_remote_copy(
                src_ref=src, dst_ref=dsts[j].at[_flat_id(block)], send_sem=send_sems.at[j, k], recv_sem=recv_sems.at[j, k],
                device_id=to, device_id_type=pl.DeviceIdType.MESH)

        local = [pltpu.make_async_copy(srcs[j], dsts[j].at[_flat_id(me)], loc_sems.at[j]) for j in range(nx)]
        first = [copy(j, 0, srcs[j], me, sibling) for j in range(nx)]
        first += [copy(j, 1 + m, srcs[j], me, (*chip, pc)) for j in range(nx) for m, chip in enumerate(chips)]
        arrived = [copy(j, 1 + m, srcs[j], (*chip, pc), me) for j in range(nx) for m, chip in enumerate(chips)]
        passed = [copy(j, 4 + m, dsts[j].at[_flat_id((*chip, pc))], (*chip, pc), sibling)
                  for j in range(nx) for m, chip in enumerate(chips)]
        from_sibling = [copy(j, 0, srcs[j], sibling, me) for j in range(nx)]
        from_sibling += [copy(j, 4 + m, srcs[j], (*chip, 1 - pc), me) for j in range(nx) for m, chip in enumerate(chips)]

        @pl.when(i == 0)
        def _():
            for cp in local + first:
                cp.start()

        xv = x_ref[...]
        r = lax.rsqrt(jnp.mean(xv * xv, axis=-1, keepdims=True) + EPS)
        o_ref[...] = (xv * r * w_ref[...]).astype(BF16)

        @pl.when(i == nt - 1)
        def _():
            for got, fwd in zip(arrived, passed):
                got.wait_recv()
                fwd.start()
            for cp in from_sibling:
                cp.wait_recv()
            for cp in first + passed:
                cp.wait_send()
            for cp in local:
                cp.wait()

    any_spec = pl.BlockSpec(memory_space=pl.ANY)
    return pl.pallas_call(
        body, name=name, grid=(nt,),
        in_specs=[pl.BlockSpec((tm, d), lambda i: (i, 0)), pl.BlockSpec((1, d), lambda i: (0, 0))] + [any_spec] * nx,
        out_specs=[pl.BlockSpec((tm, d), lambda i: (i, 0))] + [any_spec] * nx,
        out_shape=[jax.ShapeDtypeStruct((t, d), BF16)] + _exchange_shapes(arrays, (False,) * nx),
        scratch_shapes=_exchange_sems(nx),
        compiler_params=_cparams("arbitrary"),
    )(x, w, *arrays)


def _mm_rms_bwd(a, b, x, w, dres, name, tm, tk, carry):
    m, k = a.shape
    _, n = b.shape
    nk = k // tk
    gm = m // tm
    assert m % tm == 0 and k % tk == 0
    arrays, scatter = carry
    nx = len(arrays)

    def body(*refs):
        a_ref, b_ref, x_ref, w_ref, dres_ref = refs[:5]
        srcs = refs[5:5 + nx]
        dx_ref, dxb_ref, dw_ref = refs[5 + nx:8 + nx]
        dsts = refs[8 + nx:8 + 2 * nx]
        acc_ref = refs[8 + 2 * nx]
        i, kk = pl.program_id(0), pl.program_id(1)
        local, remote = _exchange_copies(srcs, dsts, scatter, *refs[9 + 2 * nx:])

        @pl.when((i == 0) & (kk == 0))
        def _():
            for cp in local + remote:
                cp.start()
            dw_ref[...] = jnp.zeros_like(dw_ref)

        @pl.when(kk == 0)
        def _():
            acc_ref[...] = jnp.zeros_like(acc_ref)

        acc_ref[...] += _dot(a_ref[...], b_ref[...])

        @pl.when(kk == nk - 1)
        def _():
            dhv = acc_ref[...]
            xv = x_ref[...]
            r = lax.rsqrt(jnp.mean(xv * xv, axis=-1, keepdims=True) + EPS)
            xh = xv * r
            dw_ref[0:1, :] += jnp.sum(dhv * xh, axis=0, keepdims=True)
            dxh = dhv * w_ref[...]
            dx = dres_ref[...] + r * (dxh - xh * jnp.mean(dxh * xh, axis=-1, keepdims=True))
            dx_ref[...] = dx
            dxb_ref[...] = dx.astype(BF16)

        @pl.when((i == gm - 1) & (kk == nk - 1))
        def _():
            for cp in remote + local:
                cp.wait()

    any_spec = pl.BlockSpec(memory_space=pl.ANY)
    row = pl.BlockSpec((tm, n), lambda i, kk: (i, 0))
    return pl.pallas_call(
        body, name=name, grid=(gm, nk),
        in_specs=[pl.BlockSpec((tm, tk), lambda i, kk: (i, kk)), pl.BlockSpec((tk, n), lambda i, kk: (kk, 0)),
                  row, pl.BlockSpec((1, n), lambda i, kk: (0, 0)), row] + [any_spec] * nx,
        out_specs=[row, row, pl.BlockSpec((SUBLANES, n), lambda i, kk: (0, 0))] + [any_spec] * nx,
        out_shape=[jax.ShapeDtypeStruct((m, n), F32), jax.ShapeDtypeStruct((m, n), BF16),
                   jax.ShapeDtypeStruct((SUBLANES, n), F32)] + _exchange_shapes(arrays, scatter),
        scratch_shapes=[pltpu.VMEM((tm, n), F32)] + _exchange_sems(nx),
        compiler_params=_cparams("arbitrary", "arbitrary"),
    )(a, b, x, w, dres, *arrays)


MM_TM = 1024


def _mm_tn(a, b, name, tn, tk=2 * MM_TM, carry=((), ())):
    t, m = a.shape
    _, n = b.shape
    tk = min(tk, t)
    assert t % tk == 0 and n % tn == 0
    gn, gs = n // tn, t // tk
    arrays, scatter = carry
    nx = len(arrays)

    def body(*refs):
        a_ref, b_ref = refs[:2]
        o_ref = refs[2 + nx]
        j, s = pl.program_id(0), pl.program_id(1)
        if nx:
            local, remote = _exchange_copies(refs[2:2 + nx], refs[3 + nx:3 + 2 * nx], scatter, *refs[3 + 2 * nx:])

            @pl.when((j == 0) & (s == 0))
            def _():
                for cp in local + remote:
                    cp.start()

        @pl.when(s == 0)
        def _():
            o_ref[...] = jnp.zeros_like(o_ref)

        o_ref[...] += _dot_tn(a_ref[...], b_ref[...])

        if nx:
            @pl.when((j == gn - 1) & (s == gs - 1))
            def _():
                for cp in remote + local:
                    cp.wait()

    any_spec = pl.BlockSpec(memory_space=pl.ANY)
    out = pl.pallas_call(
        body, name=name, grid=(gn, gs),
        in_specs=[pl.BlockSpec((tk, m), lambda j, s: (s, 0)),
                  pl.BlockSpec((tk, tn), lambda j, s: (s, j))] + [any_spec] * nx,
        out_specs=[pl.BlockSpec((m, tn), lambda j, s: (0, j))] + [any_spec] * nx,
        out_shape=[jax.ShapeDtypeStruct((m, n), F32)] + _exchange_shapes(arrays, scatter),
        scratch_shapes=_exchange_sems(nx) if nx else [],
        compiler_params=_cparams(*(("arbitrary", "arbitrary") if nx else ("parallel", "arbitrary"))),
    )(a, b, *arrays)
    return out if nx else out[0]


def _rel_index(dist):
    return np.clip(dist, -REL_CLIP, REL_CLIP) + REL_CLIP


def _bias_onehots():
    tw = 3 * LANES
    m = np.arange(ATT_VEC)
    dq = np.where(m <= ATT_KW, 512 - m, 512 - (m - ATT_VEC))
    dk = np.where(m < ATT_KW, m, m - ATT_VEC)
    ohq = np.zeros((tw, ATT_VEC), np.float32)
    ohk = np.zeros((tw, ATT_VEC), np.float32)
    ohq[_rel_index(dq), m] = 1.0
    ohk[_rel_index(dk), m] = 1.0
    return ohq, ohk


def _att_bias(table_pad):
    ohq, ohk = _bias_onehots()
    nslab = ATT_QB // SUBLANES

    def body(t_ref, ohq_ref, ohk_ref, bq_ref, bk_ref):
        tv = jnp.broadcast_to(t_ref[...], (SUBLANES, 3 * LANES))
        lane = lax.broadcasted_iota(jnp.int32, (ATT_QB, ATT_KW), 1)
        row = lax.broadcasted_iota(jnp.int32, (ATT_QB, ATT_KW), 0) // CHUNK
        col = lane // CHUNK
        band = (col >= row) & (col <= row + ATT_BAND - 1)
        for which, (oh_ref, out_ref) in enumerate(((ohq_ref, bq_ref), (ohk_ref, bk_ref))):
            vec = _dot(tv, oh_ref[...], HIGHEST)[0:1, :]
            slab = jnp.concatenate([vec if b == 0 else pltpu.roll(vec, b, 1) for b in range(SUBLANES)], axis=0)
            rows = [slab if a == 0 else pltpu.roll(slab, SUBLANES * a, 1) for a in range(nslab)]
            full = jnp.concatenate(rows, axis=0)[:, :ATT_KW]
            for v in range(3):
                inside = (lane >= (2 - v) * ATT_QB) if which == 0 else (lane < (v + 1) * ATT_QB)
                out_ref[v] = jnp.where(band & inside, full, NEG)

    h = table_pad.shape[0]
    oh_spec = pl.BlockSpec((3 * LANES, ATT_VEC), lambda i: (0, 0))
    out_spec = pl.BlockSpec((3, None, ATT_QB, ATT_KW), lambda i: (0, i, 0, 0))
    return pl.pallas_call(
        body, name="att_bias", grid=(h,),
        in_specs=[pl.BlockSpec((None, 1, 3 * LANES), lambda i: (i, 0, 0)), oh_spec, oh_spec],
        out_specs=[out_spec, out_spec],
        out_shape=[jax.ShapeDtypeStruct((3, h, ATT_QB, ATT_KW), F32)] * 2,
        compiler_params=_cparams("parallel"),
    )(table_pad, jnp.asarray(ohq), jnp.asarray(ohk))


def _head_masks():
    lane = lax.broadcasted_iota(jnp.int32, (1, LANES), 1)
    return [lane < ATT_DH, lane >= ATT_DH]


def _att_merge_fwd(proj, bias_q, oan, x, wba, wbb, wout, w2):
    t = proj.shape[0]
    nb = t // ATT_QB
    scale = ATT_DH ** -0.5

    def body(q_ref, k0_ref, k1_ref, k2_ref, v0_ref, v1_ref, v2_ref, b_ref, oa_ref, ga_ref, gb_ref, x_ref,
             wba_ref, wbb_ref, wout_ref, w2_ref, o_ref, lse_ref, lset_ref, x2_ref, h2_ref):
        i = pl.program_id(0)
        q = (q_ref[...] * scale).astype(BF16)
        kk = jnp.concatenate([k0_ref[...], k1_ref[...], k2_ref[...]], axis=0).astype(BF16)
        vv = jnp.concatenate([v0_ref[...], v1_ref[...], v2_ref[...]], axis=0).astype(BF16)
        lane = lax.broadcasted_iota(jnp.int32, (1, LANES), 1)
        masks = _head_masks()
        lse_cols = jnp.zeros((ATT_QB, LANES), F32)
        for p in range(ATT_HEADS // 2):
            cs = slice(p * LANES, (p + 1) * LANES)
            qt, kt, vt = q[:, cs], kk[:, cs], vv[:, cs]
            acc = jnp.zeros((ATT_QB, LANES), F32)
            for sub in range(2):
                h = 2 * p + sub
                s = _dot_nt(jnp.where(masks[sub], qt, 0), kt) + b_ref[h]
                mx = jnp.max(s, axis=-1, keepdims=True)
                e = jnp.exp(s - mx)
                l = jnp.sum(e, axis=-1, keepdims=True)
                acc = acc + _dot(e.astype(BF16), jnp.where(masks[sub], vt, 0)) * (1.0 / l)
                lse_cols = lse_cols + jnp.where(lane == h, mx + jnp.log(l), 0.0)
            o_ref[:, cs] = acc.astype(BF16)
        lse_ref[...] = lse_cols
        lset_ref[...] = lse_cols.T[0:SUBLANES, :]
        ya = _dot(oa_ref[...], wba_ref[...])
        yb = _dot(o_ref[...], wbb_ref[...])
        mix = _sigmoid(ga_ref[...]) * ya + _sigmoid(gb_ref[...]) * yb
        x2 = x_ref[...] + _dot(mix.astype(BF16), wout_ref[...])
        x2_ref[...] = x2
        r = lax.rsqrt(jnp.mean(x2 * x2, axis=-1, keepdims=True) + EPS)
        h2_ref[...] = (x2 * r * w2_ref[...]).astype(BF16)

    def kv_spec(off, cb):
        return pl.BlockSpec((ATT_QB, PB), lambda i: (jnp.maximum(i + off, 0), cb))

    half = pl.BlockSpec((ATT_QB, WIDTH_B), lambda i: (i, 0))
    row = pl.BlockSpec((ATT_QB, D_MODEL), lambda i: (i, 0))
    return pl.pallas_call(
        body, name="att_merge_fwd", grid=(nb,),
        in_specs=[pl.BlockSpec((ATT_QB, PB), lambda i: (i, CB_QB)),
                  kv_spec(-2, CB_KB), kv_spec(-1, CB_KB), kv_spec(0, CB_KB),
                  kv_spec(-2, CB_VB), kv_spec(-1, CB_VB), kv_spec(0, CB_VB),
                  pl.BlockSpec((None, ATT_HEADS, ATT_QB, ATT_KW), lambda i: (jnp.minimum(i, 2), 0, 0, 0)),
                  half, pl.BlockSpec((ATT_QB, D_MODEL), lambda i: (i, CB_GA)),
                  pl.BlockSpec((ATT_QB, D_MODEL), lambda i: (i, CB_GB)), row,
                  _resident((KEY_A, D_MODEL)), _resident((WIDTH_B, D_MODEL)), _resident((D_MODEL, D_MODEL)),
                  _resident((1, D_MODEL))],
        out_specs=[half, pl.BlockSpec((ATT_QB, LANES), lambda i: (i, 0)),
                   pl.BlockSpec((SUBLANES, ATT_QB), lambda i: (0, i)), row, row],
        out_shape=[jax.ShapeDtypeStruct((t, WIDTH_B), BF16), jax.ShapeDtypeStruct((t, LANES), F32),
                   jax.ShapeDtypeStruct((SUBLANES, t), F32), jax.ShapeDtypeStruct((t, D_MODEL), F32),
                   jax.ShapeDtypeStruct((t, D_MODEL), BF16)],
        compiler_params=_cparams("parallel"),
    )(proj, proj, proj, proj, proj, proj, proj, bias_q, oan, proj, proj, x, wba, wbb, wout, w2)


def _att_dq(proj, bias_q, lse, d_ob, ob, dproj):
    t = proj.shape[0]
    nb = t // ATT_QB
    scale = ATT_DH ** -0.5
    nslab = ATT_QB // SUBLANES

    def body(q_ref, k0_ref, k1_ref, k2_ref, v0_ref, v1_ref, v2_ref, b_ref, lse_ref, do_ref, o_ref, dp_in_ref,
             dq_ref, dlt_ref, slab_ref):
        i = pl.program_id(0)

        @pl.when(i == 0)
        def _():
            slab_ref[...] = jnp.zeros_like(slab_ref)

        q = (q_ref[...] * scale).astype(BF16)
        kk = jnp.concatenate([k0_ref[...], k1_ref[...], k2_ref[...]], axis=0).astype(BF16)
        vv = jnp.concatenate([v0_ref[...], v1_ref[...], v2_ref[...]], axis=0).astype(BF16)
        do = do_ref[...].astype(BF16)
        do_o = do_ref[...] * o_ref[...].astype(F32)
        lane = lax.broadcasted_iota(jnp.int32, (1, LANES), 1)
        masks = _head_masks()
        lse_all = lse_ref[...]
        dlt_cols = jnp.zeros((ATT_QB, LANES), F32)
        zpad = jnp.zeros((SUBLANES, ATT_VEC - ATT_KW), F32)
        for p in range(ATT_HEADS // 2):
            cs = slice(p * LANES, (p + 1) * LANES)
            qt, kt, vt, dot_ = q[:, cs], kk[:, cs], vv[:, cs], do[:, cs]
            acc = jnp.zeros((ATT_QB, LANES), F32)
            for sub in range(2):
                h = 2 * p + sub
                s = _dot_nt(jnp.where(masks[sub], qt, 0), kt) + b_ref[h]
                pr = jnp.exp(s - lse_all[:, h:h + 1])
                dp = _dot_nt(jnp.where(masks[sub], dot_, 0), vt)
                dl = jnp.sum(jnp.where(masks[sub], do_o[:, cs], 0.0), axis=-1, keepdims=True)
                ds = pr * (dp - dl)
                acc = acc + _dot(ds.astype(BF16), jnp.where(masks[sub], kt, 0)) * scale
                dlt_cols = dlt_cols + jnp.where(lane == h, dl, 0.0)
                sl = jnp.zeros((SUBLANES, ATT_VEC), F32)
                for a in range(nslab):
                    piece = jnp.concatenate([ds[a * SUBLANES:(a + 1) * SUBLANES, :], zpad], axis=1)
                    sl = sl + (piece if a == 0 else pltpu.roll(piece, ATT_VEC - SUBLANES * a, 1))
                slab_ref[h] += sl
            dq_ref[:, cs] = acc.astype(BF16)
        dlt_ref[...] = dlt_cols.T[0:SUBLANES, :]

    def kv_spec(off, cb):
        return pl.BlockSpec((ATT_QB, PB), lambda i: (jnp.maximum(i + off, 0), cb))

    return pl.pallas_call(
        body, name="att_dq", grid=(nb,),
        in_specs=[pl.BlockSpec((ATT_QB, PB), lambda i: (i, CB_QB)),
                  kv_spec(-2, CB_KB), kv_spec(-1, CB_KB), kv_spec(0, CB_KB),
                  kv_spec(-2, CB_VB), kv_spec(-1, CB_VB), kv_spec(0, CB_VB),
                  pl.BlockSpec((None, ATT_HEADS, ATT_QB, ATT_KW), lambda i: (jnp.minimum(i, 2), 0, 0, 0)),
                  pl.BlockSpec((ATT_QB, LANES), lambda i: (i, 0)),
                  pl.BlockSpec((ATT_QB, WIDTH_B), lambda i: (i, 0)), pl.BlockSpec((ATT_QB, WIDTH_B), lambda i: (i, 0)),
                  pl.BlockSpec(memory_space=pl.ANY)],
        out_specs=[_dp_spec(ATT_QB, DP_QB),
                   pl.BlockSpec((SUBLANES, ATT_QB), lambda i: (0, i)),
                   pl.BlockSpec((ATT_HEADS, SUBLANES, ATT_VEC), lambda i: (0, 0, 0))],
        out_shape=[jax.ShapeDtypeStruct(dproj.shape, dproj.dtype), jax.ShapeDtypeStruct((SUBLANES, t), F32),
                   jax.ShapeDtypeStruct((ATT_HEADS, SUBLANES, ATT_VEC), F32)],
        input_output_aliases={11: 0},
        compiler_params=_cparams("arbitrary"),
    )(proj, proj, proj, proj, proj, proj, proj, bias_q, lse, d_ob, ob, dproj)


def _att_dkv(proj, bias_k, lse_t, dlt_t, d_ob, dproj):
    t = proj.shape[0]
    nb = t // ATT_QB
    scale = ATT_DH ** -0.5

    def body(k_ref, v_ref, q0_ref, q1_ref, q2_ref, d0_ref, d1_ref, d2_ref, l0_ref, l1_ref, l2_ref,
             e0_ref, e1_ref, e2_ref, b_ref, dp_in_ref, dkv_ref):
        i = pl.program_id(0)
        k = k_ref[...].astype(BF16)
        v = v_ref[...].astype(BF16)
        qq = (jnp.concatenate([q0_ref[...], q1_ref[...], q2_ref[...]], axis=0) * scale).astype(BF16)
        do = jnp.concatenate([d0_ref[...], d1_ref[...], d2_ref[...]], axis=0).astype(BF16)
        lse = jnp.concatenate([l0_ref[...], l1_ref[...], l2_ref[...]], axis=1)
        dlt = jnp.concatenate([e0_ref[...], e1_ref[...], e2_ref[...]], axis=1)
        masks = _head_masks()
        for p in range(ATT_HEADS // 2):
            cs = slice(p * LANES, (p + 1) * LANES)
            kt, vt, qt, dot_ = k[:, cs], v[:, cs], qq[:, cs], do[:, cs]
            acc_k = jnp.zeros((ATT_QB, LANES), F32)
            acc_v = jnp.zeros((ATT_QB, LANES), F32)
            for sub in range(2):
                h = 2 * p + sub
                st = _dot_nt(jnp.where(masks[sub], kt, 0), qt) + b_ref[h]
                pt = jnp.exp(st - lse[h:h + 1, :])
                dot_m = jnp.where(masks[sub], dot_, 0)
                acc_v = acc_v + _dot(pt.astype(BF16), dot_m)
                dpt = _dot_nt(jnp.where(masks[sub], vt, 0), dot_)
                dst = pt * (dpt - dlt[h:h + 1, :])
                acc_k = acc_k + _dot(dst.astype(BF16), jnp.where(masks[sub], qt, 0))
            dkv_ref[:, cs] = acc_k.astype(BF16)
            dkv_ref[:, WIDTH_B + p * LANES:WIDTH_B + (p + 1) * LANES] = acc_v.astype(BF16)

    def q_spec(off, cb):
        return pl.BlockSpec((ATT_QB, PB), lambda i: (jnp.minimum(i + off, nb - 1), cb))

    def d_spec(off):
        return pl.BlockSpec((ATT_QB, WIDTH_B), lambda i: (jnp.minimum(i + off, nb - 1), 0))

    def r_spec(off):
        return pl.BlockSpec((SUBLANES, ATT_QB), lambda i: (0, jnp.minimum(i + off, nb - 1)))

    row = pl.BlockSpec((ATT_QB, WIDTH_B), lambda i: (i, 0))
    return pl.pallas_call(
        body, name="att_dkv", grid=(nb,),
        in_specs=[pl.BlockSpec((ATT_QB, PB), lambda i: (i, CB_KB)), pl.BlockSpec((ATT_QB, PB), lambda i: (i, CB_VB)),
                  q_spec(0, CB_QB), q_spec(1, CB_QB), q_spec(2, CB_QB),
                  d_spec(0), d_spec(1), d_spec(2), r_spec(0), r_spec(1), r_spec(2),
                  r_spec(0), r_spec(1), r_spec(2),
                  pl.BlockSpec((None, ATT_HEADS, ATT_QB, ATT_KW), lambda i: (jnp.minimum(nb - 1 - i, 2), 0, 0, 0)),
                  pl.BlockSpec(memory_space=pl.ANY)],
        out_specs=_dp_spec(ATT_QB, DP_KVB),
        out_shape=jax.ShapeDtypeStruct(dproj.shape, dproj.dtype),
        input_output_aliases={15: 0},
        compiler_params=_cparams("parallel"),
    )(proj, proj, proj, proj, proj, d_ob, d_ob, d_ob, lse_t, lse_t, lse_t, dlt_t, dlt_t, dlt_t, bias_k, dproj)


def _relbias_grad(slabs):
    ohq, _ = _bias_onehots()

    def body(s_ref, oh_ref, o_ref):
        sv = s_ref[...]
        vec = sv[0:1, :]
        for b in range(1, SUBLANES):
            vec = vec + pltpu.roll(sv[b:b + 1, :], ATT_VEC - b, 1)
        o_ref[...] = _dot_nt(jnp.broadcast_to(vec, (SUBLANES, ATT_VEC)), oh_ref[...], HIGHEST)[0:1, :]

    h = slabs.shape[0]
    return pl.pallas_call(
        body, name="att_dbias", grid=(h,),
        in_specs=[pl.BlockSpec((None, SUBLANES, ATT_VEC), lambda i: (i, 0, 0)),
                  pl.BlockSpec((3 * LANES, ATT_VEC), lambda i: (0, 0))],
        out_specs=pl.BlockSpec((None, 1, 3 * LANES), lambda i: (i, 0, 0)),
        out_shape=jax.ShapeDtypeStruct((h, 1, 3 * LANES), F32),
        compiler_params=_cparams("parallel"),
    )(slabs, jnp.asarray(ohq))


GDN_TM = 512
GDN_CB = 8
HALO = SUBLANES


def _conv_taps(ext, width, lead, n):
    return [(ext if k == width - 1 else pltpu.roll(ext, width - 1 - k, 0))[lead:lead + n] for k in range(width)]


def _next_halo_spec(tm, width, cb, t):
    return pl.BlockSpec((HALO, width), lambda i: (jnp.minimum((i + 1) * (tm // HALO), t // HALO - 1), cb))


def _in_proj_prep(h1, w_all, conv_w, carry):
    t = h1.shape[0]
    tm = GDN_TM
    nt = t // tm
    arrays, scatter = carry
    nx = len(arrays)
    c0 = CB_QA * PB

    def body(*refs):
        h_ref, w_ref, cw_ref = refs[:3]
        srcs = refs[3:3 + nx]
        proj_ref, qn_ref, kn_ref, vo_ref, y_ref = refs[3 + nx:8 + nx]
        dsts = refs[8 + nx:8 + 2 * nx]
        prev_ref = refs[8 + 2 * nx]
        i = pl.program_id(0)
        local, remote = _exchange_copies(srcs, dsts, scatter, *refs[9 + 2 * nx:])

        @pl.when(i == 0)
        def _():
            for cp in local + remote:
                cp.start()
            prev_ref[...] = jnp.zeros_like(prev_ref)

        proj = _dot(h_ref[...], w_ref[...])
        proj_ref[...] = proj
        xin = proj[:, c0:c0 + 3 * KEY_A]
        ext = jnp.concatenate([prev_ref[...], xin], axis=0)
        prev_ref[...] = xin[tm - HALO:tm]
        taps = _conv_taps(ext, GDN_CONV, HALO, tm)
        cw = cw_ref[...]
        y = sum(cw[k:k + 1, :] * taps[k] for k in range(GDN_CONV))
        y_ref[...] = y
        a = y * _sigmoid(y)
        for idx, o_ref in enumerate((qn_ref, kn_ref)):
            for h in range(GDN_HEADS):
                cs = slice(h * GDN_DK, (h + 1) * GDN_DK)
                seg = a[:, idx * KEY_A + h * GDN_DK:idx * KEY_A + (h + 1) * GDN_DK]
                o_ref[:, cs] = seg * lax.rsqrt(jnp.sum(seg * seg, axis=-1, keepdims=True) + EPS)
        vo_ref[...] = a[:, 2 * KEY_A:]

        @pl.when(i == nt - 1)
        def _():
            for cp in remote + local:
                cp.wait()

    any_spec = pl.BlockSpec(memory_space=pl.ANY)
    row = pl.BlockSpec((tm, KEY_A), lambda i: (i, 0))
    return pl.pallas_call(
        body, name="in_proj", grid=(nt,),
        in_specs=[pl.BlockSpec((tm, D_MODEL), lambda i: (i, 0)), _resident((D_MODEL, PROJ_W)),
                  _resident((GDN_CONV, 3 * KEY_A))] + [any_spec] * nx,
        out_specs=[pl.BlockSpec((tm, PROJ_W), lambda i: (i, 0)), row, row, row,
                   pl.BlockSpec((tm, 3 * KEY_A), lambda i: (i, 0))] + [any_spec] * nx,
        out_shape=[jax.ShapeDtypeStruct((t, PROJ_W), F32)] + [jax.ShapeDtypeStruct((t, KEY_A), F32)] * 3
        + [jax.ShapeDtypeStruct((t, 3 * KEY_A), F32)] + _exchange_shapes(arrays, scatter),
        scratch_shapes=[pltpu.VMEM((HALO, 3 * KEY_A), F32)] + _exchange_sems(nx),
        compiler_params=_cparams("arbitrary"),
    )(h1, w_all, conv_w, *arrays)


def _gdn_prep_bwd(proj, ycv, conv_w, dqn, dkn, dv, dproj):
    t = proj.shape[0]
    tm = GDN_TM
    nt = t // tm
    n_ext = tm + HALO

    def body(q_ref, k_ref, v_ref, y_ref, ny_ref, dq_ref, dk_ref, dv_ref, ndq_ref, ndk_ref, ndv_ref, w_ref, dp_in_ref,
             out_ref, dw_ref):
        i = pl.program_id(0)
        last = i == nt - 1

        @pl.when(i == 0)
        def _():
            dw_ref[...] = jnp.zeros_like(dw_ref)

        groups = ((q_ref, dq_ref, ndq_ref), (k_ref, dk_ref, ndk_ref), (v_ref, dv_ref, ndv_ref))
        for idx, (x_ref, d_ref, nd_ref) in enumerate(groups):
            cs_all = slice(idx * KEY_A, (idx + 1) * KEY_A)
            w = w_ref[:, cs_all]
            y = jnp.concatenate([y_ref[:, cs_all], jnp.where(last, 0.0, ny_ref[:, cs_all])], axis=0)
            sg = _sigmoid(y)
            a = y * sg
            dup = jnp.concatenate([d_ref[...], jnp.where(last, 0.0, nd_ref[...])], axis=0)
            if idx < 2:
                segs = []
                for h in range(GDN_HEADS):
                    cs = slice(h * GDN_DK, (h + 1) * GDN_DK)
                    seg = a[:, cs]
                    r = lax.rsqrt(jnp.sum(seg * seg, axis=-1, keepdims=True) + EPS)
                    nrm = seg * r
                    dn = dup[:, cs]
                    segs.append(r * (dn - nrm * jnp.sum(dn * nrm, axis=-1, keepdims=True)))
                da = jnp.concatenate(segs, axis=1)
            else:
                da = dup
            dy = da * sg * (1.0 + y * (1.0 - sg))
            xv = x_ref[...]
            dx = None
            for k in range(GDN_CONV):
                shift = GDN_CONV - 1 - k
                tap = (dy if shift == 0 else pltpu.roll(dy, n_ext - shift, 0))[:tm]
                term = w[k:k + 1, :] * tap
                dx = term if dx is None else dx + term
                dw_ref[k:k + 1, cs_all] += jnp.sum(tap * xv, axis=0, keepdims=True)
            out_ref[:, cs_all] = dx.astype(BF16)

    row = pl.BlockSpec((tm, KEY_A), lambda i: (i, 0))
    nrow = _next_halo_spec(tm, KEY_A, 0, t)
    return pl.pallas_call(
        body, name="gdn_prep_bwd", grid=(nt,),
        in_specs=[pl.BlockSpec((tm, PB), lambda i: (i, CB_QA)), pl.BlockSpec((tm, PB), lambda i: (i, CB_KA)),
                  pl.BlockSpec((tm, PB), lambda i: (i, CB_VA)),
                  pl.BlockSpec((tm, 3 * KEY_A), lambda i: (i, 0)), _next_halo_spec(tm, 3 * KEY_A, 0, t),
                  row, row, row, nrow, nrow, nrow,
                  pl.BlockSpec((GDN_CONV, 3 * KEY_A), lambda i: (0, 0)), pl.BlockSpec(memory_space=pl.ANY)],
        out_specs=[_dp_spec(tm, DP_QKVA), pl.BlockSpec((SUBLANES, 3 * KEY_A), lambda i: (0, 0))],
        out_shape=[jax.ShapeDtypeStruct(dproj.shape, dproj.dtype), jax.ShapeDtypeStruct((SUBLANES, 3 * KEY_A), F32)],
        input_output_aliases={12: 0},
        compiler_params=_cparams("arbitrary"),
    )(proj, proj, proj, ycv, ycv, dqn, dkn, dv, dqn, dkn, dv, conv_w, dproj)


class _Pair(dict):
    __getattr__ = dict.__getitem__
    __setattr__ = dict.__setitem__


def _pairs_to_lanes(cols):
    lane = lax.broadcasted_iota(jnp.int32, (1, LANES), 1)
    out = jnp.zeros((cols[0].shape[0], LANES), F32)
    for p, col in enumerate(cols):
        out = out + jnp.where(lane == p, col, 0.0)
    return out


def _gdn_terms(bd, par, kn_ref, qn_ref):
    c = CHUNK
    ii = lax.broadcasted_iota(jnp.int32, (c, c), 0)
    jj = lax.broadcasted_iota(jnp.int32, (c, c), 1)
    strict, incl = ii > jj, ii >= jj
    ltri = incl.astype(F32)
    ts = []
    for cc in range(GDN_CB):
        for h in range(GDN_HEADS):
            t = _Pair(cc=cc, h=h, rows=slice(cc * c, (cc + 1) * c), cs=slice(h * GDN_DK, (h + 1) * GDN_DK),
                      strict=strict, incl=incl)
            t.beta = _sigmoid(bd[t.rows, h:h + 1])
            t.ea = jnp.exp(par[0:1, h:h + 1])
            t.sp_arg = bd[t.rows, GDN_HEADS + h:GDN_HEADS + h + 1] + par[1:2, h:h + 1]
            t.g = -t.ea * _softplus(t.sp_arg)
            t.k = kn_ref[t.rows, t.cs]
            t.q = qn_ref[t.rows, t.cs] * (GDN_DK ** -0.5)
            t.kb, t.qb = t.k.astype(BF16), t.q.astype(BF16)
            ts.append(t)
    gall = _dot(ltri, _pairs_to_lanes([t.g for t in ts]), HIGHEST)
    gall_t = gall.T
    for p, t in enumerate(ts):
        t.gb = jnp.broadcast_to(gall[:, p:p + 1], (c, GDN_DK))
    for t in ts:
        t.kk = _dot_nt(t.kb, t.kb)
        t.qk = _dot_nt(t.qb, t.kb)
    for p, t in enumerate(ts):
        diff = t.gb[:, :c] - gall_t[p:p + 1, :]
        t.dec_s = jnp.exp(jnp.where(strict, diff, NEG))
        t.dec_i = jnp.exp(jnp.where(incl, diff, NEG))
        t.gam = jnp.exp(t.gb)
        glast = t.gb[c - 1:c, :]
        t.e_rest = jnp.exp(glast - t.gb)
        t.gl = jnp.exp(glast)
        t.p = t.qk * t.dec_i
    return ts


def _gdn_fwd(qn, kn, v, proj, par, gnw):
    t = qn.shape[0]
    c = CHUNK
    nc = t // c
    r_ = GDN_CB * c

    def body(qn_ref, kn_ref, v_ref, bd_ref, z_ref, par_ref, gnw_ref,
             oan_ref, o_ref, sp_ref, w_ref, u_ref, tm_ref, s_ref):
        @pl.when(pl.program_id(0) == 0)
        def _():
            s_ref[...] = jnp.zeros_like(s_ref)

        bd, par, gnw_v = bd_ref[...], par_ref[...], gnw_ref[...]
        eye = (lax.broadcasted_iota(jnp.int32, (c, c), 0) == lax.broadcasted_iota(jnp.int32, (c, c), 1)).astype(F32)
        ts = _gdn_terms(bd, par, kn_ref, qn_ref)
        for t in ts:
            t.vv = v_ref[t.rows, t.cs]
            t.x = -(t.beta * t.kk * t.dec_s)
            t.tinv = eye + t.x
        for t in ts:
            t.xs = _split(t.x)
        for _ in range(5):
            for t in ts:
                t.xs = _split(_dot3s(t.xs, t.xs))
            for t in ts:
                t.tinv = t.tinv + _dot3s(_split(t.tinv), t.xs)
        for t in ts:
            tsp = _split(t.tinv)
            t.wm = _dot3s(tsp, _split((t.beta * t.gam) * t.k))
            t.uv = _dot3s(tsp, _split(t.beta * t.vv))
        for t in ts:
            w_ref[t.rows, t.cs] = t.wm
            tm_ref[t.cc, t.h] = t.tinv.T
            t.wb = t.wm.astype(BF16)
            t.qgb = (t.q * t.gam).astype(BF16)
            t.kdb = (t.k * t.e_rest).astype(BF16)
            t.pb = t.p.astype(BF16)
        state = [s_ref[h] for h in range(GDN_HEADS)]
        for cc in range(GDN_CB):
            tc = [t for t in ts if t.cc == cc]
            for t in tc:
                t.sh = state[t.h]
                t.sb = t.sh.astype(BF16)
            for t in tc:
                t.ws = _dot(t.wb, t.sb)
            for t in tc:
                t.u = t.uv - t.ws
                t.ub = t.u.astype(BF16)
            for t in tc:
                state[t.h] = t.gl * t.sh + _dot_tn(t.kdb, t.ub)
            for t in tc:
                t.o = _dot(t.qgb, t.sb) + _dot(t.pb, t.ub)
                sp_ref[cc, t.h] = t.sh
                u_ref[t.rows, t.cs] = t.u
                o_ref[t.rows, t.cs] = t.o
        for h in range(GDN_HEADS):
            s_ref[h] = state[h]
        for t in ts:
            zz = z_ref[t.rows, t.cs]
            rr = lax.rsqrt(jnp.mean(t.o * t.o, axis=-1, keepdims=True) + EPS)
            oan_ref[t.rows, t.cs] = ((t.o * rr) * gnw_v * (zz * _sigmoid(zz))).astype(BF16)

    row = pl.BlockSpec((r_, KEY_A), lambda i: (i, 0))
    return pl.pallas_call(
        body, name="gdn_fwd", grid=(nc // GDN_CB,),
        in_specs=[row, row, row, pl.BlockSpec((r_, LANES), lambda i: (i, CB_BD)),
                  pl.BlockSpec((r_, PB), lambda i: (i, CB_ZA)),
                  pl.BlockSpec((SUBLANES, LANES), lambda i: (0, 0)), pl.BlockSpec((1, GDN_DK), lambda i: (0, 0))],
        out_specs=[row, row, pl.BlockSpec((GDN_CB, GDN_HEADS, GDN_DK, GDN_DK), lambda i: (i, 0, 0, 0)),
                   row, row, pl.BlockSpec((GDN_CB, GDN_HEADS, c, c), lambda i: (i, 0, 0, 0))],
        out_shape=[jax.ShapeDtypeStruct((t, KEY_A), BF16), jax.ShapeDtypeStruct((t, KEY_A), F32),
                   jax.ShapeDtypeStruct((nc, GDN_HEADS, GDN_DK, GDN_DK), F32),
                   jax.ShapeDtypeStruct((t, KEY_A), F32), jax.ShapeDtypeStruct((t, KEY_A), F32),
                   jax.ShapeDtypeStruct((nc, GDN_HEADS, c, c), F32)],
        scratch_shapes=[pltpu.VMEM((GDN_HEADS, GDN_DK, GDN_DK), F32)],
        compiler_params=_cparams("arbitrary"),
    )(qn, kn, v, proj, proj, par, gnw)


def _gdn_bwd(qn, kn, v, proj, par, gnw, o, sprev, wst, ust, tst, d_oan, dproj):
    t = qn.shape[0]
    c = CHUNK
    nc = t // c
    nb = nc // GDN_CB
    r_ = GDN_CB * c

    def body(qn_ref, kn_ref, v_ref, bd_ref, z_ref, par_ref, gnw_ref, o_ref, sp_ref, w_ref, u_ref, tm_ref, do_ref,
             dp_in_ref, dqn_ref, dkn_ref, dv_ref, dzb_ref, acc_ref, ds_ref):
        @pl.when(pl.program_id(0) == 0)
        def _():
            ds_ref[...] = jnp.zeros_like(ds_ref)
            acc_ref[...] = jnp.zeros_like(acc_ref)

        bd, par, gnw_v = bd_ref[...], par_ref[...], gnw_ref[...]
        lane = lax.broadcasted_iota(jnp.int32, (1, LANES), 1)
        rix = lax.broadcasted_iota(jnp.int32, (c, 1), 0)
        ii = lax.broadcasted_iota(jnp.int32, (c, c), 0)
        jj = lax.broadcasted_iota(jnp.int32, (c, c), 1)
        upper = (jj >= ii).astype(F32)
        acc_a = jnp.zeros((1, LANES), F32)
        acc_d = jnp.zeros((1, LANES), F32)
        acc_g = jnp.zeros((1, LANES), F32)
        ts = _gdn_terms(bd, par, kn_ref, qn_ref)
        for t in ts:
            t.vv = v_ref[t.rows, t.cs]
            t.sh = sp_ref[t.cc, t.h]
            t.sb = t.sh.astype(BF16)
            t.wm, t.u, t.tinv_t = w_ref[t.rows, t.cs], u_ref[t.rows, t.cs], tm_ref[t.cc, t.h]
            t.wb, t.ub = t.wm.astype(BF16), t.u.astype(BF16)
            ov, zz, dout = o_ref[t.rows, t.cs], z_ref[t.rows, t.cs], do_ref[t.rows, t.cs]
            sg = _sigmoid(zz)
            sil = zz * sg
            rr = lax.rsqrt(jnp.mean(ov * ov, axis=-1, keepdims=True) + EPS)
            on = ov * rr
            dzb_ref[t.rows, t.cs] = (dout * on * gnw_v * (sg * (1.0 + zz * (1.0 - sg)))).astype(BF16)
            acc_g = acc_g + jnp.sum(dout * on * sil, axis=0, keepdims=True)
            don = dout * gnw_v * sil
            t.dob = (rr * (don - on * jnp.mean(don * on, axis=-1, keepdims=True))).astype(BF16)
            t.qg = t.q * t.gam
            t.kd = t.k * t.e_rest
            t.qgb, t.kdb = t.qg.astype(BF16), t.kd.astype(BF16)
            t.ptb = t.p.T.astype(BF16)
        for t in ts:
            t.du0 = _dot(t.ptb, t.dob)
            t.ds0 = _dot_tn(t.qgb, t.dob)
            t.dqg = _dot_nt(t.dob, t.sb)
            t.dp = _dot_nt(t.dob, t.ub)
            t.uv = t.u + _dot(t.wb, t.sb)
        dstate = [ds_ref[h] for h in range(GDN_HEADS)]
        for cc in reversed(range(GDN_CB)):
            tc = [t for t in ts if t.cc == cc]
            for t in tc:
                t.dsn = dstate[t.h]
                t.dsnb = t.dsn.astype(BF16)
            for t in tc:
                t.du = t.du0 + _dot(t.kdb, t.dsnb)
            for t in tc:
                t.dub = t.du.astype(BF16)
            for t in tc:
                dstate[t.h] = t.gl * t.dsn + t.ds0 - _dot_tn(t.wb, t.dub)
            for t in tc:
                t.dkd = _dot_nt(t.ub, t.dsnb)
                t.dgl = jnp.sum(jnp.sum(t.dsn * t.sh, axis=1, keepdims=True), axis=0, keepdims=True)
                t.dwm = -_dot_nt(t.dub, t.sb)
        for h in range(GDN_HEADS):
            ds_ref[h] = dstate[h]
        for t in ts:
            tsp = _split(t.tinv_t)
            t.dbk = _dot3s(tsp, _split(t.dwm))
            t.dbv = _dot3s(tsp, _split(t.du))
        for t in ts:
            d_a = -(_dot_nt(t.dbk.astype(BF16), t.wb) + _dot_nt(t.dbv.astype(BF16), t.uv.astype(BF16)))
            t.d_a = jnp.where(t.strict, d_a, 0.0)
        for t in ts:
            t.dkk = t.d_a * t.beta * t.dec_s
            t.dqk = t.dp * t.dec_i
            t.dqkb = t.dqk.astype(BF16)
        for t in ts:
            t.dq = _dot(t.dqkb, t.kb) + t.dqg * t.gam
            t.dk = (t.dbk * (t.beta * t.gam) + _dot_tn(t.dqkb, t.qb) + _dot((t.dkk + t.dkk.T).astype(BF16), t.kb)
                    + t.dkd * t.e_rest)
        for t in ts:
            dbeta = (jnp.sum(t.d_a * t.kk * t.dec_s, axis=-1, keepdims=True)
                     + jnp.sum(t.dbk * t.k * t.gam, axis=-1, keepdims=True) + jnp.sum(t.dbv * t.vv, axis=-1, keepdims=True))
            t.dbl = dbeta * t.beta * (1.0 - t.beta)
            dv_ref[t.rows, t.cs] = t.dbv * t.beta
            bk = (t.beta * t.gam) * t.k
            zc = jnp.sum(t.dkd * t.kd, axis=-1, keepdims=True)
            xs = t.dkk * t.kk + t.dp * t.p
            dgc = (jnp.sum(xs, axis=-1, keepdims=True) - jnp.sum(xs.T, axis=-1, keepdims=True)
                   + jnp.sum(t.dbk * bk, axis=-1, keepdims=True) + jnp.sum(t.dqg * t.qg, axis=-1, keepdims=True) - zc)
            dglast = jnp.sum(zc, axis=0, keepdims=True) + t.dgl * t.gl[:, 0:1]
            t.dgc = dgc + jnp.where(rix == c - 1, dglast, 0.0)
        dgall = _dot(upper, _pairs_to_lanes([t.dgc for t in ts]), HIGHEST)
        for p, t in enumerate(ts):
            t.dg = dgall[:, p:p + 1]
        dbd_tiles = [jnp.zeros((c, LANES), F32) for _ in range(GDN_CB)]
        for t in ts:
            ddl = t.dg * (-t.ea) * _sigmoid(t.sp_arg)
            acc_a = acc_a + jnp.where(lane == t.h, jnp.sum(t.dg * t.g, axis=0, keepdims=True), 0.0)
            acc_d = acc_d + jnp.where(lane == t.h, jnp.sum(ddl, axis=0, keepdims=True), 0.0)
            dbd_tiles[t.cc] = (dbd_tiles[t.cc] + jnp.where(lane == t.h, t.dbl, 0.0)
                               + jnp.where(lane == GDN_HEADS + t.h, ddl, 0.0))
            dqn_ref[t.rows, t.cs] = t.dq * (GDN_DK ** -0.5)
            dkn_ref[t.rows, t.cs] = t.dk
        for cc in range(GDN_CB):
            dzb_ref[cc * c:(cc + 1) * c, KEY_A:KEY_A + LANES] = dbd_tiles[cc].astype(BF16)
        acc_ref[0:1, :] += acc_a
        acc_ref[1:2, :] += acc_d
        acc_ref[2:3, :] += acc_g

    def rev(i):
        return nb - 1 - i

    row = pl.BlockSpec((r_, KEY_A), lambda i: (rev(i), 0))
    st = pl.BlockSpec((GDN_CB, GDN_HEADS, GDN_DK, GDN_DK), lambda i: (rev(i), 0, 0, 0))
    tt_spec = pl.BlockSpec((GDN_CB, GDN_HEADS, c, c), lambda i: (rev(i), 0, 0, 0))
    return pl.pallas_call(
        body, name="gdn_bwd", grid=(nb,),
        in_specs=[row, row, row, pl.BlockSpec((r_, LANES), lambda i: (rev(i), CB_BD)),
                  pl.BlockSpec((r_, PB), lambda i: (rev(i), CB_ZA)),
                  pl.BlockSpec((SUBLANES, LANES), lambda i: (0, 0)), pl.BlockSpec((1, GDN_DK), lambda i: (0, 0)),
                  row, st, row, row, tt_spec, row, pl.BlockSpec(memory_space=pl.ANY)],
        out_specs=[row, row, row, _dp_spec(r_, DP_ZBD, rev), pl.BlockSpec((SUBLANES, LANES), lambda i: (0, 0))],
        out_shape=[jax.ShapeDtypeStruct((t, KEY_A), F32)] * 3 + [jax.ShapeDtypeStruct(dproj.shape, dproj.dtype),
                                                                jax.ShapeDtypeStruct((SUBLANES, LANES), F32)],
        input_output_aliases={13: 3},
        scratch_shapes=[pltpu.VMEM((GDN_HEADS, GDN_DK, GDN_DK), F32)],
        compiler_params=_cparams("arbitrary"),
    )(qn, kn, v, proj, proj, par, gnw, o, sprev, wst, ust, tst, d_oan, dproj)


def _merge_bwd(dx2b, oan, ob, proj, wba, wbb, wout_t, wba_t, wbb_t, tm=512):
    t = dx2b.shape[0]

    def body(dx_ref, oa_ref, ob_ref, ga_ref, gb_ref, wba_ref, wbb_ref, woutt_ref, wbat_ref, wbbt_ref,
             dg_ref, doa_ref, dob_ref, gout_ref, gba_ref, gbb_ref):
        @pl.when(pl.program_id(0) == 0)
        def _():
            gout_ref[...] = jnp.zeros_like(gout_ref)
            gba_ref[...] = jnp.zeros_like(gba_ref)
            gbb_ref[...] = jnp.zeros_like(gbb_ref)

        dx, oa, ob = dx_ref[...], oa_ref[...], ob_ref[...]
        dmix = _dot(dx, woutt_ref[...])
        ya = _dot(oa, wba_ref[...])
        yb = _dot(ob, wbb_ref[...])
        sa, sb = _sigmoid(ga_ref[...]), _sigmoid(gb_ref[...])
        gout_ref[...] += _dot_tn((sa * ya + sb * yb).astype(BF16), dx)
        dg_ref[:, :D_MODEL] = (dmix * ya * sa * (1.0 - sa)).astype(BF16)
        dg_ref[:, D_MODEL:] = (dmix * yb * sb * (1.0 - sb)).astype(BF16)
        dya = (dmix * sa).astype(BF16)
        dyb = (dmix * sb).astype(BF16)
        gba_ref[...] += _dot_tn(oa, dya)
        gbb_ref[...] += _dot_tn(ob, dyb)
        doa_ref[...] = _dot(dya, wbat_ref[...])
        dob_ref[...] = _dot(dyb, wbbt_ref[...])

    half = pl.BlockSpec((tm, KEY_A), lambda i: (i, 0))
    row = pl.BlockSpec((tm, D_MODEL), lambda i: (i, 0))
    wsmall = pl.BlockSpec((KEY_A, D_MODEL), lambda i: (0, 0))
    wsmall_t = pl.BlockSpec((D_MODEL, KEY_A), lambda i: (0, 0))
    wfull = pl.BlockSpec((D_MODEL, D_MODEL), lambda i: (0, 0))
    return pl.pallas_call(
        body, name="merge_bwd", grid=(t // tm,),
        in_specs=[row, half, half, pl.BlockSpec((tm, D_MODEL), lambda i: (i, CB_GA)),
                  pl.BlockSpec((tm, D_MODEL), lambda i: (i, CB_GB)), wsmall, wsmall, wfull, wsmall_t, wsmall_t],
        out_specs=[_dp_spec(tm, DP_GATES), half, half, wfull, wsmall, wsmall],
        out_shape=[jax.ShapeDtypeStruct((t, PROJ_W), BF16), jax.ShapeDtypeStruct((t, KEY_A), F32),
                   jax.ShapeDtypeStruct((t, KEY_A), F32), jax.ShapeDtypeStruct((D_MODEL, D_MODEL), F32),
                   jax.ShapeDtypeStruct((KEY_A, D_MODEL), F32), jax.ShapeDtypeStruct((WIDTH_B, D_MODEL), F32)],
        compiler_params=_cparams("arbitrary"),
    )(dx2b, oan, ob, proj, proj, wba, wbb, wout_t, wba_t, wbb_t)


FFN_TM = 128
FFN_W = 2 * D_FF


def _resident(shape):
    return pl.BlockSpec(shape, lambda i: (0,) * len(shape), pipeline_mode=pl.Buffered(1))


def _ffn_fwd(h2, wup, cw, cb, wdown, x2, tgt, w3):
    t = x2.shape[0]
    tm = FFN_TM

    def body(h2_ref, wup_ref, cw_ref, cb_ref, wd_ref, x2_ref, tgt_ref, w3_ref, up_ref, u_ref, dx_ref, dxb_ref, act_ref,
             acc_ref, prev_ref):
        @pl.when(pl.program_id(0) == 0)
        def _():
            acc_ref[...] = jnp.zeros_like(acc_ref)
            prev_ref[...] = jnp.zeros_like(prev_ref)

        up = _dot(h2_ref[...], wup_ref[...])
        up_ref[...] = up
        ext = jnp.concatenate([prev_ref[...], up], axis=0)
        prev_ref[...] = up[tm - HALO:tm]
        taps = _conv_taps(ext, FFN_CONV, HALO, tm)
        cw_v = cw_ref[...]
        u = sum(cw_v[k:k + 1, :] * taps[k] for k in range(FFN_CONV)) + cb_ref[...]
        u_ref[...] = u
        gate, upp = u[:, :D_FF], u[:, D_FF:]
        act = (gate * _sigmoid(gate) * upp).astype(BF16)
        act_ref[...] = act
        x3 = x2_ref[...] + _dot(act, wd_ref[...])
        r = lax.rsqrt(jnp.mean(x3 * x3, axis=-1, keepdims=True) + EPS)
        xh = x3 * r
        w3v = w3_ref[...]
        err = xh * w3v - tgt_ref[...]
        loss = 0.5 * jnp.sum(jnp.mean(err * err, axis=-1, keepdims=True), axis=0, keepdims=True)
        dy = err * (1.0 / D_MODEL)
        acc_ref[0:1, :] += jnp.sum(dy * xh, axis=0, keepdims=True)
        acc_ref[1:2, :] += jnp.broadcast_to(loss, (1, D_MODEL))
        dxh = dy * w3v
        dx = r * (dxh - xh * jnp.mean(dxh * xh, axis=-1, keepdims=True))
        dx_ref[...] = dx
        dxb_ref[...] = dx.astype(BF16)

    row = pl.BlockSpec((tm, D_MODEL), lambda i: (i, 0))
    return pl.pallas_call(
        body, name="ffn_fwd", grid=(t // tm,),
        in_specs=[row, _resident((D_MODEL, FFN_W)), _resident((SUBLANES, FFN_W)), _resident((1, FFN_W)),
                  _resident((D_FF, D_MODEL)), row, row, _resident((1, D_MODEL))],
        out_specs=[pl.BlockSpec((tm, FFN_W), lambda i: (i, 0)), pl.BlockSpec((tm, FFN_W), lambda i: (i, 0)), row, row,
                   pl.BlockSpec((tm, D_FF), lambda i: (i, 0)), pl.BlockSpec((SUBLANES, D_MODEL), lambda i: (0, 0))],
        out_shape=[jax.ShapeDtypeStruct((t, FFN_W), F32), jax.ShapeDtypeStruct((t, FFN_W), F32),
                   jax.ShapeDtypeStruct((t, D_MODEL), F32),
                   jax.ShapeDtypeStruct((t, D_MODEL), BF16), jax.ShapeDtypeStruct((t, D_FF), BF16),
                   jax.ShapeDtypeStruct((SUBLANES, D_MODEL), F32)],
        scratch_shapes=[pltpu.VMEM((HALO, FFN_W), F32)],
        compiler_params=_cparams("arbitrary"),
    )(h2, wup, cw, cb, wdown, x2, tgt, w3)


def _ffn_bwd(dx3b, wdown_t, up, u, cw, wup_t, x2, w2, dx3, carry):
    t = up.shape[0]
    tm = FFN_TM
    nt = t // tm
    n_ext = tm + HALO
    arrays, scatter = carry
    nx = len(arrays)

    def rev(i):
        return nt - 1 - i

    def body(*refs):
        dx_ref, wdt_ref, up_ref, u_ref, cw_ref, wupt_ref, x2_ref, w2_ref, dres_ref = refs[:9]
        srcs = refs[9:9 + nx]
        dup_ref, acc_ref, dx2_ref, dx2b_ref, dw2_ref = refs[9 + nx:14 + nx]
        dsts = refs[14 + nx:14 + 2 * nx]
        nxt_ref = refs[14 + 2 * nx]
        i = pl.program_id(0)
        local, remote = _exchange_copies(srcs, dsts, scatter, *refs[15 + 2 * nx:])

        @pl.when(i == 0)
        def _():
            for cp in local + remote:
                cp.start()
            acc_ref[...] = jnp.zeros_like(acc_ref)
            dw2_ref[...] = jnp.zeros_like(dw2_ref)
            nxt_ref[...] = jnp.zeros_like(nxt_ref)

        dact = _dot(dx_ref[...], wdt_ref[...])
        gate, upp = u_ref[:, :D_FF], u_ref[:, D_FF:]
        sg = _sigmoid(gate)
        du = jnp.concatenate([dact * upp * (sg * (1.0 + gate * (1.0 - sg))), dact * (gate * sg)], axis=1)
        acc_ref[FFN_CONV:FFN_CONV + 1, :] += jnp.sum(du, axis=0, keepdims=True)
        ext = jnp.concatenate([du, nxt_ref[...]], axis=0)
        cw_v = cw_ref[...]
        upv = up_ref[...]
        dup = None
        for k in range(FFN_CONV):
            shift = FFN_CONV - 1 - k
            tap = du if shift == 0 else pltpu.roll(ext, n_ext - shift, 0)[:tm]
            term = cw_v[k:k + 1, :] * tap
            dup = term if dup is None else dup + term
            acc_ref[k:k + 1, :] += jnp.sum(tap * upv, axis=0, keepdims=True)
        dupb = dup.astype(BF16)
        dup_ref[...] = dupb
        nxt_ref[...] = du[0:HALO]
        dhv = _dot(dupb, wupt_ref[...])
        xv = x2_ref[...]
        r = lax.rsqrt(jnp.mean(xv * xv, axis=-1, keepdims=True) + EPS)
        xh = xv * r
        dw2_ref[0:1, :] += jnp.sum(dhv * xh, axis=0, keepdims=True)
        dxh = dhv * w2_ref[...]
        dx2 = dres_ref[...] + r * (dxh - xh * jnp.mean(dxh * xh, axis=-1, keepdims=True))
        dx2_ref[...] = dx2
        dx2b_ref[...] = dx2.astype(BF16)

        @pl.when(i == nt - 1)
        def _():
            for cp in remote + local:
                cp.wait()

    wide = pl.BlockSpec((tm, FFN_W), lambda i: (rev(i), 0))
    row = pl.BlockSpec((tm, D_MODEL), lambda i: (rev(i), 0))
    any_spec = pl.BlockSpec(memory_space=pl.ANY)
    return pl.pallas_call(
        body, name="ffn_bwd", grid=(nt,),
        in_specs=[row, _resident((D_MODEL, D_FF)), wide, wide,
                  _resident((SUBLANES, FFN_W)), _resident((FFN_W, D_MODEL)), row,
                  _resident((1, D_MODEL)), row] + [any_spec] * nx,
        out_specs=[wide, pl.BlockSpec((SUBLANES, FFN_W), lambda i: (0, 0)), row, row,
                   pl.BlockSpec((SUBLANES, D_MODEL), lambda i: (0, 0))] + [any_spec] * nx,
        out_shape=[jax.ShapeDtypeStruct((t, FFN_W), BF16), jax.ShapeDtypeStruct((SUBLANES, FFN_W), F32),
                   jax.ShapeDtypeStruct((t, D_MODEL), F32), jax.ShapeDtypeStruct((t, D_MODEL), BF16),
                   jax.ShapeDtypeStruct((SUBLANES, D_MODEL), F32)] + _exchange_shapes(arrays, scatter),
        scratch_shapes=[pltpu.VMEM((HALO, FFN_W), F32)] + _exchange_sems(nx),
        compiler_params=_cparams("arbitrary"),
    )(dx3b, wdown_t, up, u, cw, wup_t, x2, w2, dx3, *arrays)


def _adamw(parts, w, m, v, name, tr):
    r, cols = w.shape

    def body(p_ref, w_ref, m_ref, v_ref, g_ref, d_ref, mo_ref, vo_ref):
        g = p_ref[0].astype(F32)
        for s in range(1, N_DEV):
            g = g + p_ref[s].astype(F32)
        mm = ADAM_B1 * m_ref[...] + (1.0 - ADAM_B1) * g
        vv = ADAM_B2 * v_ref[...] + (1.0 - ADAM_B2) * (g * g)
        m_hat = mm / (1.0 - ADAM_B1 ** ADAM_STEP)
        v_hat = vv / (1.0 - ADAM_B2 ** ADAM_STEP)
        g_ref[...] = g
        d_ref[...] = -ADAM_LR * (m_hat / (jnp.sqrt(v_hat) + ADAM_EPS) + ADAM_WD * w_ref[...])
        mo_ref[...] = mm
        vo_ref[...] = vv

    assert r % tr == 0
    row = pl.BlockSpec((tr, cols), lambda i: (i, 0))
    return pl.pallas_call(
        body, name=name, grid=(r // tr,),
        in_specs=[pl.BlockSpec((N_DEV, tr, cols), lambda i: (0, i, 0)), row, row, row],
        out_specs=[row, row, row, row],
        out_shape=[jax.ShapeDtypeStruct((r, cols), F32)] * 4,
        compiler_params=_cparams("parallel"),
    )(parts, w, m, v)


def _mesh_pos():
    return lax.axis_index("x"), lax.axis_index("y"), lax.axis_index("c")


def _peer(pos, k):
    x, y, c = pos
    return (x ^ ((k >> 2) & 1), y ^ ((k >> 1) & 1), c ^ (k & 1))


def _flat_id(pos):
    return 4 * pos[0] + 2 * pos[1] + pos[2]


def _exchange_copies(srcs, dsts, scatter, send_sems, recv_sems, loc_sems):
    pos = _mesh_pos()
    me = _flat_id(pos)
    local, remote = [], []
    for j, (src, dst) in enumerate(zip(srcs, dsts)):
        local.append(pltpu.make_async_copy(src.at[me] if scatter[j] else src, dst.at[me], loc_sems.at[j]))
        for k in range(1, N_DEV):
            to = _peer(pos, k)
            remote.append(pltpu.make_async_remote_copy(
                src_ref=src.at[_flat_id(to)] if scatter[j] else src, dst_ref=dst.at[me],
                send_sem=send_sems.at[j, k - 1], recv_sem=recv_sems.at[j, k - 1],
                device_id=to, device_id_type=pl.DeviceIdType.MESH))
    return local, remote


def _exchange_shapes(arrays, scatter):
    return [jax.ShapeDtypeStruct(a.shape if s else (N_DEV,) + a.shape, a.dtype) for a, s in zip(arrays, scatter)]


def _exchange_sems(n):
    return [pltpu.SemaphoreType.DMA((n, N_DEV - 1)), pltpu.SemaphoreType.DMA((n, N_DEV - 1)), pltpu.SemaphoreType.DMA((n,))]


def _exchange(arrays, scatter, name):
    n = len(arrays)
    any_spec = pl.BlockSpec(memory_space=pl.ANY)

    def body(*refs):
        local, remote = _exchange_copies(refs[:n], refs[n:2 * n], scatter, *refs[2 * n:])
        for cp in local + remote:
            cp.start()
        for cp in remote:
            cp.wait()
        for cp in local:
            cp.wait()

    return pl.pallas_call(
        body, name=name, in_specs=[any_spec] * n, out_specs=[any_spec] * n,
        out_shape=_exchange_shapes(arrays, scatter), scratch_shapes=_exchange_sems(n),
    )(*arrays)


def _pad_rows(a, rows):
    return jnp.pad(a, ((0, rows - a.shape[0]),) + ((0, 0),) * (a.ndim - 1))


PACK_UNIT = SUBLANES * LANES


def _pack_lanes(parts, rows):
    out = []
    for a in parts:
        f = a.reshape(-1)
        out.append(jnp.pad(f, (0, (-f.shape[0]) % PACK_UNIT)).reshape(-1, LANES))
    packed = jnp.concatenate(out, axis=0)
    assert packed.shape[0] == rows, (packed.shape, rows)
    return packed


def _unpack_lanes(buf, shapes):
    out, r0 = [], 0
    for shp in shapes:
        n = math.prod(shp)
        nr = -(-n // PACK_UNIT) * SUBLANES
        out.append(buf[r0:r0 + nr].reshape(-1)[:n].reshape(shp))
        r0 += nr
    return out


def _col_shards(g):
    r, n = g.shape
    return g.reshape(r, N_DEV, n // N_DEV).transpose(1, 0, 2)


def _col_unshard(s):
    _, r, w = s.shape
    return s.transpose(1, 0, 2).reshape(r, N_DEV * w)


def _lane_rows(flat):
    n = flat.shape[1]
    return jnp.pad(flat, ((0, 0), (0, (-n) % PACK_UNIT))).reshape(N_DEV, -1, LANES)


SMALL_ROWS = 128
WS_ROWS = 32


def kernel(x, norm_mix_w, w_in, conv_qkv_w, a_log, dt_bias, gdn_norm_w, w_branch_a, w_branch_b, rel_bias, w_out, norm_ffn_w, w_up, conv_ffn_w, conv_ffn_b, w_down, norm_final_w, loss_target, m_norm_mix_w, m_w_in, m_conv_qkv_w, m_a_log, m_dt_bias, m_gdn_norm_w, m_w_branch_a, m_w_branch_b, m_rel_bias, m_w_out, m_norm_ffn_w, m_w_up, m_conv_ffn_w, m_conv_ffn_b, m_w_down, m_norm_final_w, v_norm_mix_w, v_w_in, v_conv_qkv_w, v_a_log, v_dt_bias, v_gdn_norm_w, v_w_branch_a, v_w_branch_b, v_rel_bias, v_w_out, v_norm_ffn_w, v_w_up, v_conv_ffn_w, v_conv_ffn_b, v_w_down, v_norm_final_w):
    big_w = (w_in, w_branch_a, w_branch_b, w_out, w_up, w_down, conv_qkv_w, conv_ffn_w)
    big_m = (m_w_in, m_w_branch_a, m_w_branch_b, m_w_out, m_w_up, m_w_down, m_conv_qkv_w, m_conv_ffn_w)
    big_v = (v_w_in, v_w_branch_a, v_w_branch_b, v_w_out, v_w_up, v_w_down, v_conv_qkv_w, v_conv_ffn_w)
    small_w = (norm_mix_w, a_log, dt_bias, gdn_norm_w, rel_bias, norm_ffn_w, conv_ffn_b, norm_final_w)
    small_m = (m_norm_mix_w, m_a_log, m_dt_bias, m_gdn_norm_w, m_rel_bias, m_norm_ffn_w, m_conv_ffn_b, m_norm_final_w)
    small_v = (v_norm_mix_w, v_a_log, v_dt_bias, v_gdn_norm_w, v_rel_bias, v_norm_ffn_w, v_conv_ffn_b, v_norm_final_w)

    xs, tgt = x[0], loss_target[0]
    ws = _pack_lanes(big_w[6:], WS_ROWS)
    h1, g_in, gs = _rmsnorm_gather(xs, norm_mix_w, [w_in[0].astype(BF16), ws], "norm_mix")
    win = _col_unshard(g_in)
    gs = gs.reshape(N_DEV, -1)
    cqkv = gs[:, :GDN_CONV * 192].reshape(N_DEV, GDN_CONV, 192).transpose(1, 0, 2).reshape(GDN_CONV, 3 * KEY_A)
    cffn = gs[:, PACK_UNIT:PACK_UNIT + FFN_CONV * 704].reshape(N_DEV, FFN_CONV, 704).transpose(1, 0, 2).reshape(FFN_CONV, FFN_W)
    cffn = _pad_rows(cffn, SUBLANES)
    w_all = jnp.concatenate([win[:, a:b] for a, b in W_IN_ORDER] + [jnp.zeros((D_MODEL, PROJ_W - D_IN), BF16)], axis=1)
    par = _pad_rows(jnp.pad(jnp.concatenate([a_log, dt_bias], axis=0), ((0, 0), (0, LANES - GDN_HEADS))), SUBLANES)
    table = jnp.pad(rel_bias[0], ((0, 0), (0, 3 * LANES - rel_bias.shape[-1]))).reshape(ATT_HEADS, 1, 3 * LANES)

    proj, qn, kn, va, ycv, g_ba, g_bb, g_out, g_up, g_down = _in_proj_prep(
        h1, w_all, cqkv, carry=([w[0].astype(BF16) for w in big_w[1:6]], (False,) * 5))
    wba, wbb, wup = _col_unshard(g_ba), _col_unshard(g_bb), _col_unshard(g_up)
    wout = g_out.reshape(D_MODEL, D_MODEL)
    wdown = g_down.reshape(D_FF, D_MODEL)
    oan, o_gdn, sprev, wst, ust, tst = _gdn_fwd(qn, kn, va, proj, par, gdn_norm_w)
    bias_q, bias_k = _att_bias(table)
    ob, lse, lse_t, x2, h2 = _att_merge_fwd(proj, bias_q, oan, xs, wba, wbb, wout, norm_ffn_w)
    up, u_ffn, dx3, dx3b, act, tail_sums = _ffn_fwd(h2, wup, cffn, conv_ffn_b, wdown, x2, tgt,
                                                    norm_final_w.reshape(1, D_MODEL))

    g_wdown = _mm_tn(act, dx3b, "dw_down", 512)
    dup, ffn_sums, dx2, dx2b, nffn_sums, r_down = _ffn_bwd(
        dx3b, wdown.T, up, u_ffn, cffn, wup.T, x2, norm_ffn_w, dx3,
        carry=([g_wdown.reshape(N_DEV, -1, D_MODEL).astype(BF16)], (True,)))
    g_wup = _mm_tn(h2, dup, "dw_up", 1408)
    dproj, d_oan, d_ob, g_wout, g_wba, g_wbb = _merge_bwd(dx2b, oan, ob, proj, wba, wbb, wout.T, wba.T, wbb.T)
    dproj, dlt_t, slabs = _att_dq(proj, bias_q, lse, d_ob, ob, dproj)
    dproj = _att_dkv(proj, bias_k, lse_t, dlt_t, d_ob, dproj)
    g_rel = _relbias_grad(slabs)[:, 0, :rel_bias.shape[-1]]
    dqn, dkn, dva, dproj, gdn_sums = _gdn_bwd(qn, kn, va, proj, par, gdn_norm_w, o_gdn, sprev, wst, ust, tst, d_oan, dproj)
    dproj, cq_sums = _gdn_prep_bwd(proj, ycv, cqkv, dqn, dkn, dva, dproj)
    g_wall, r_up = _mm_tn(h1, dproj, "dw_in", 1152, carry=([_col_shards(g_wup).astype(BF16)], (True,)))
    starts = np.cumsum([0] + [b - a for a, b in W_IN_ORDER])
    g_win = jnp.concatenate([g_wall[:, starts[i]:starts[i + 1]] for i in np.argsort([a for a, _ in W_IN_ORDER])], axis=1)
    g_conv = jnp.concatenate([_lane_rows(_col_shards(cq_sums[:GDN_CONV]).reshape(N_DEV, -1)),
                              _lane_rows(_col_shards(ffn_sums[:FFN_CONV]).reshape(N_DEV, -1))], axis=1)
    grad_x, _, nmix_sums, r_in, r_ba, r_bb, r_out, r_conv = _mm_rms_bwd(
        dproj, w_all.T, xs, norm_mix_w, dx2, "in_proj_bwd", MM_TM, 1152, carry=(
            [_col_shards(g_win).astype(BF16), _col_shards(g_wba).astype(BF16), _col_shards(g_wbb).astype(BF16),
             g_wout.reshape(N_DEV, -1, D_MODEL).astype(BF16), g_conv], (True,) * 5))

    small_g = (nmix_sums[0:1], gdn_sums[0:1, :GDN_HEADS], gdn_sums[1:2, :GDN_HEADS], gdn_sums[2:3], g_rel,
               nffn_sums[0:1], ffn_sums[FFN_CONV:FFN_CONV + 1], tail_sums[0:1], tail_sums[1:2, 0:1])
    r_small, = _exchange([_pack_lanes(small_g, SMALL_ROWS)], (False,), "all_gather_small_grads")
    recv = (r_in, r_ba, r_bb, r_out, r_up, r_down, r_conv, r_small)

    res = {}
    for i, (nm, tr) in enumerate((("w_in", 128), ("w_branch_a", KEY_A), ("w_branch_b", WIDTH_B), ("w_out", 128),
                                  ("w_up", 128), ("w_down", 176))):
        res[nm] = [o[None] for o in _adamw(recv[i], big_w[i][0], big_m[i][0], big_v[i][0], "adamw_" + nm, tr)]
    conv = _adamw(recv[6], _pack_lanes(big_w[6:], WS_ROWS), _pack_lanes(big_m[6:], WS_ROWS), _pack_lanes(big_v[6:], WS_ROWS),
                  "adamw_conv", WS_ROWS)
    conv = [_unpack_lanes(o, [w.shape for w in big_w[6:]]) for o in conv]
    res["conv_qkv_w"] = [o[0] for o in conv]
    res["conv_ffn_w"] = [o[1] for o in conv]
    small_shapes = [w.shape for w in small_w]
    zero = jnp.zeros((1,), F32)
    small = _adamw(recv[7], _pack_lanes(small_w + (zero,), SMALL_ROWS), _pack_lanes(small_m + (zero,), SMALL_ROWS),
                   _pack_lanes(small_v + (zero,), SMALL_ROWS), "adamw_replicated", SMALL_ROWS)
    small = [_unpack_lanes(o, small_shapes + [()]) for o in small]
    loss = small[0][-1]
    for j, nm in enumerate(("norm_mix_w", "a_log", "dt_bias", "gdn_norm_w", "rel_bias", "norm_ffn_w", "conv_ffn_b",
                            "norm_final_w")):
        res[nm] = [o[j] for o in small]

    names = ("norm_mix_w", "w_in", "conv_qkv_w", "a_log", "dt_bias", "gdn_norm_w", "w_branch_a", "w_branch_b", "rel_bias",
             "w_out", "norm_ffn_w", "w_up", "conv_ffn_w", "conv_ffn_b", "w_down", "norm_final_w")
    outs = [res[n][kind] for kind in range(4) for n in names]
    return (loss, grad_x[None], *outs)
```

```python
import functools
import math

import numpy as np
import jax
import jax.numpy as jnp
from jax import lax
from jax.experimental import pallas as pl
from jax.experimental.pallas import tpu as pltpu

F32, BF16 = jnp.float32, jnp.bfloat16
HIGHEST = lax.Precision.HIGHEST

N_DEV = 8
D_MODEL = 1024
CHUNK = 64
EPS = 1e-6
GDN_HEADS, GDN_DK = 4, 128
KEY_A = GDN_HEADS * GDN_DK
GDN_CONV = 4
ATT_HEADS, ATT_DH = 8, 64
WIDTH_B = ATT_HEADS * ATT_DH
ATT_BAND = 9
REL_CLIP = 128
D_FF = 2816
FFN_CONV = 3
D_IN = 5640
ADAM_LR, ADAM_B1, ADAM_B2, ADAM_EPS, ADAM_WD, ADAM_STEP = 0.001, 0.9, 0.999, 1e-08, 0.01, 10

LANES = 128
SUBLANES = 8
NEG = -1e30

PROJ_W = 5760
PB = 512
CB_GA, CB_GB = 0, 1
CB_KB, CB_VB, CB_QA, CB_KA, CB_VA, CB_QB, CB_ZA = 4, 5, 6, 7, 8, 9, 10
CB_BD = 44
DP_GATES, DP_KVB, DP_QKVA, DP_QB, DP_ZBD = (2048, 0), (1024, 2), (1536, 2), (512, 9), (640, 8)
W_IN_ORDER = ((3592, 5640), (2568, 3592), (0, 1536), (2056, 2568), (1536, 2048), (2048, 2056))

ATT_QB = 256
ATT_KW = 768
ATT_VEC = 1024


def _dot(a, b, precision=None):
    return jnp.dot(a, b, preferred_element_type=F32, precision=precision)


def _dot_nt(a, b, precision=None):
    return lax.dot_general(a, b, (((1,), (1,)), ((), ())), preferred_element_type=F32, precision=precision)


def _dot_tn(a, b):
    return lax.dot_general(a, b, (((0,), (0,)), ((), ())), preferred_element_type=F32)


def _split(a):
    hi = a.astype(BF16)
    return hi, (a - hi.astype(F32)).astype(BF16)


def _dot3s(a, b):
    return _dot(a[0], b[0]) + (_dot(a[0], b[1]) + _dot(a[1], b[0]))


def _sigmoid(x):
    return 0.5 * jnp.tanh(0.5 * x) + 0.5


def _softplus(x):
    return jnp.maximum(x, 0.0) + jnp.log(1.0 + jnp.exp(-jnp.abs(x)))


def _cparams(*sem):
    return pltpu.CompilerParams(dimension_semantics=tuple(sem))


def _dp_spec(tm, region, index=lambda i: i):
    width, cb = region
    return pl.BlockSpec((tm, width), lambda i: (index(i), cb))


def _rmsnorm_gather(x, w, arrays, name, tm=512):
    t, d = x.shape
    nt = t // tm
    nx = len(arrays)

    def body(*refs):
        x_ref, w_ref = refs[:2]
        srcs = refs[2:2 + nx]
        o_ref = refs[2 + nx]
        dsts = refs[3 + nx:3 + 2 * nx]
        send_sems, recv_sems, loc_sems = refs[3 + 2 * nx:]
        i = pl.program_id(0)
        px, py, pc = _mesh_pos()
        me, sibling = (px, py, pc), (px, py, 1 - pc)
        chips = [(1 - px, py), (px, 1 - py), (1 - px, 1 - py)]

        def copy(j, k, src, block, to):
            return pltpu.make_async_remote_copy(
                src_ref=src, dst_ref=dsts[j].at[_flat_id(block)], send_sem=send_sems.at[j, k], recv_sem=recv_sems.at[j, k],
                device_id=to, device_id_type=pl.DeviceIdType.MESH)

        local = [pltpu.make_async_copy(srcs[j], dsts[j].at[_flat_id(me)], loc_sems.at[j]) for j in range(nx)]
        first = [copy(j, 0, srcs[j], me, sibling) for j in range(nx)]
        first += [copy(j, 1 + m, srcs[j], me, (*chip, pc)) for j in range(nx) for m, chip in enumerate(chips)]
        arrived = [copy(j, 1 + m, srcs[j], (*chip, pc), me) for j in range(nx) for m, chip in enumerate(chips)]
        passed = [copy(j, 4 + m, dsts[j].at[_flat_id((*chip, pc))], (*chip, pc), sibling)
                  for j in range(nx) for m, chip in enumerate(chips)]
        from_sibling = [copy(j, 0, srcs[j], sibling, me) for j in range(nx)]
        from_sibling += [copy(j, 4 + m, srcs[j], (*chip, 1 - pc), me) for j in range(nx) for m, chip in enumerate(chips)]

        @pl.when(i == 0)
        def _():
            for cp in local + first:
                cp.start()

        xv = x_ref[...]
        r = lax.rsqrt(jnp.mean(xv * xv, axis=-1, keepdims=True) + EPS)
        o_ref[...] = (xv * r * w_ref[...]).astype(BF16)

        @pl.when(i == nt - 1)
        def _():
            for got, fwd in zip(arrived, passed):
                got.wait_recv()
                fwd.start()
            for cp in from_sibling:
                cp.wait_recv()
            for cp in first + passed:
                cp.wait_send()
            for cp in local:
                cp.wait()

    any_spec = pl.BlockSpec(memory_space=pl.ANY)
    return pl.pallas_call(
        body, name=name, grid=(nt,),
        in_specs=[pl.BlockSpec((tm, d), lambda i: (i, 0)), pl.BlockSpec((1, d), lambda i: (0, 0))] + [any_spec] * nx,
        out_specs=[pl.BlockSpec((tm, d), lambda i: (i, 0))] + [any_spec] * nx,
        out_shape=[jax.ShapeDtypeStruct((t, d), BF16)] + _exchange_shapes(arrays, (False,) * nx),
        scratch_shapes=_exchange_sems(nx),
        compiler_params=_cparams("arbitrary"),
    )(x, w, *arrays)


def _mm_rms_bwd(a, b, x, w, dres, name, tm, tk, carry):
    m, k = a.shape
    _, n = b.shape
    nk = k // tk
    gm = m // tm
    assert m % tm == 0 and k % tk == 0
    arrays, scatter = carry
    nx = len(arrays)

    def body(*refs):
        a_ref, b_ref, x_ref, w_ref, dres_ref = refs[:5]
        srcs = refs[5:5 + nx]
        dx_ref, dxb_ref, dw_ref = refs[5 + nx:8 + nx]
        dsts = refs[8 + nx:8 + 2 * nx]
        acc_ref = refs[8 + 2 * nx]
        i, kk = pl.program_id(0), pl.program_id(1)
        local, remote = _exchange_copies(srcs, dsts, scatter, *refs[9 + 2 * nx:])

        @pl.when((i == 0) & (kk == 0))
        def _():
            for cp in local + remote:
                cp.start()
            dw_ref[...] = jnp.zeros_like(dw_ref)

        @pl.when(kk == 0)
        def _():
            acc_ref[...] = jnp.zeros_like(acc_ref)

        acc_ref[...] += _dot(a_ref[...], b_ref[...])

        @pl.when(kk == nk - 1)
        def _():
            dhv = acc_ref[...]
            xv = x_ref[...]
            r = lax.rsqrt(jnp.mean(xv * xv, axis=-1, keepdims=True) + EPS)
            xh = xv * r
            dw_ref[0:1, :] += jnp.sum(dhv * xh, axis=0, keepdims=True)
            dxh = dhv * w_ref[...]
            dx = dres_ref[...] + r * (dxh - xh * jnp.mean(dxh * xh, axis=-1, keepdims=True))
            dx_ref[...] = dx
            dxb_ref[...] = dx.astype(BF16)

        @pl.when((i == gm - 1) & (kk == nk - 1))
        def _():
            for cp in remote + local:
                cp.wait()

    any_spec = pl.BlockSpec(memory_space=pl.ANY)
    row = pl.BlockSpec((tm, n), lambda i, kk: (i, 0))
    return pl.pallas_call(
        body, name=name, grid=(gm, nk),
        in_specs=[pl.BlockSpec((tm, tk), lambda i, kk: (i, kk)), pl.BlockSpec((tk, n), lambda i, kk: (kk, 0)),
                  row, pl.BlockSpec((1, n), lambda i, kk: (0, 0)), row] + [any_spec] * nx,
        out_specs=[row, row, pl.BlockSpec((SUBLANES, n), lambda i, kk: (0, 0))] + [any_spec] * nx,
        out_shape=[jax.ShapeDtypeStruct((m, n), F32), jax.ShapeDtypeStruct((m, n), BF16),
                   jax.ShapeDtypeStruct((SUBLANES, n), F32)] + _exchange_shapes(arrays, scatter),
        scratch_shapes=[pltpu.VMEM((tm, n), F32)] + _exchange_sems(nx),
        compiler_params=_cparams("arbitrary", "arbitrary"),
    )(a, b, x, w, dres, *arrays)


MM_TM = 1024


def _mm_tn(a, b, name, tn, tk=2 * MM_TM, carry=((), ())):
    t, m = a.shape
    _, n = b.shape
    tk = min(tk, t)
    assert t % tk == 0 and n % tn == 0
    gn, gs = n // tn, t // tk
    arrays, scatter = carry
    nx = len(arrays)

    def body(*refs):
        a_ref, b_ref = refs[:2]
        o_ref = refs[2 + nx]
        j, s = pl.program_id(0), pl.program_id(1)
        if nx:
            local, remote = _exchange_copies(refs[2:2 + nx], refs[3 + nx:3 + 2 * nx], scatter, *refs[3 + 2 * nx:])

            @pl.when((j == 0) & (s == 0))
            def _():
                for cp in local + remote:
                    cp.start()

        @pl.when(s == 0)
        def _():
            o_ref[...] = jnp.zeros_like(o_ref)

        o_ref[...] += _dot_tn(a_ref[...], b_ref[...])

        if nx:
            @pl.when((j == gn - 1) & (s == gs - 1))
            def _():
                for cp in remote + local:
                    cp.wait()

    any_spec = pl.BlockSpec(memory_space=pl.ANY)
    out = pl.pallas_call(
        body, name=name, grid=(gn, gs),
        in_specs=[pl.BlockSpec((tk, m), lambda j, s: (s, 0)),
                  pl.BlockSpec((tk, tn), lambda j, s: (s, j))] + [any_spec] * nx,
        out_specs=[pl.BlockSpec((m, tn), lambda j, s: (0, j))] + [any_spec] * nx,
        out_shape=[jax.ShapeDtypeStruct((m, n), F32)] + _exchange_shapes(arrays, scatter),
        scratch_shapes=_exchange_sems(nx) if nx else [],
        compiler_params=_cparams(*(("arbitrary", "arbitrary") if nx else ("parallel", "arbitrary"))),
    )(a, b, *arrays)
    return out if nx else out[0]


def _rel_index(dist):
    return np.clip(dist, -REL_CLIP, REL_CLIP) + REL_CLIP


def _bias_onehots():
    tw = 3 * LANES
    m = np.arange(ATT_VEC)
    dq = np.where(m <= ATT_KW, 512 - m, 512 - (m - ATT_VEC))
    dk = np.where(m < ATT_KW, m, m - ATT_VEC)
    ohq = np.zeros((tw, ATT_VEC), np.float32)
    ohk = np.zeros((tw, ATT_VEC), np.float32)
    ohq[_rel_index(dq), m] = 1.0
    ohk[_rel_index(dk), m] = 1.0
    return ohq, ohk


def _att_bias(table_pad):
    ohq, ohk = _bias_onehots()
    nslab = ATT_QB // SUBLANES

    def body(t_ref, ohq_ref, ohk_ref, bq_ref, bk_ref):
        tv = jnp.broadcast_to(t_ref[...], (SUBLANES, 3 * LANES))
        lane = lax.broadcasted_iota(jnp.int32, (ATT_QB, ATT_KW), 1)
        row = lax.broadcasted_iota(jnp.int32, (ATT_QB, ATT_KW), 0) // CHUNK
        col = lane // CHUNK
        band = (col >= row) & (col <= row + ATT_BAND - 1)
        for which, (oh_ref, out_ref) in enumerate(((ohq_ref, bq_ref), (ohk_ref, bk_ref))):
            vec = _dot(tv, oh_ref[...], HIGHEST)[0:1, :]
            slab = jnp.concatenate([vec if b == 0 else pltpu.roll(vec, b, 1) for b in range(SUBLANES)], axis=0)
            rows = [slab if a == 0 else pltpu.roll(slab, SUBLANES * a, 1) for a in range(nslab)]
            full = jnp.concatenate(rows, axis=0)[:, :ATT_KW]
            for v in range(3):
                inside = (lane >= (2 - v) * ATT_QB) if which == 0 else (lane < (v + 1) * ATT_QB)
                out_ref[v] = jnp.where(band & inside, full, NEG)

    h = table_pad.shape[0]
    oh_spec = pl.BlockSpec((3 * LANES, ATT_VEC), lambda i: (0, 0))
    out_spec = pl.BlockSpec((3, None, ATT_QB, ATT_KW), lambda i: (0, i, 0, 0))
    return pl.pallas_call(
        body, name="att_bias", grid=(h,),
        in_specs=[pl.BlockSpec((None, 1, 3 * LANES), lambda i: (i, 0, 0)), oh_spec, oh_spec],
        out_specs=[out_spec, out_spec],
        out_shape=[jax.ShapeDtypeStruct((3, h, ATT_QB, ATT_KW), F32)] * 2,
        compiler_params=_cparams("parallel"),
    )(table_pad, jnp.asarray(ohq), jnp.asarray(ohk))


def _head_masks():
    lane = lax.broadcasted_iota(jnp.int32, (1, LANES), 1)
    return [lane < ATT_DH, lane >= ATT_DH]


def _att_merge_fwd(proj, bias_q, oan, x, wba, wbb, wout, w2):
    t = proj.shape[0]
    nb = t // ATT_QB
    scale = ATT_DH ** -0.5

    def body(q_ref, k0_ref, k1_ref, k2_ref, v0_ref, v1_ref, v2_ref, b_ref, oa_ref, ga_ref, gb_ref, x_ref,
             wba_ref, wbb_ref, wout_ref, w2_ref, o_ref, lse_ref, lset_ref, x2_ref, h2_ref):
        i = pl.program_id(0)
        q = (q_ref[...] * scale).astype(BF16)
        kk = jnp.concatenate([k0_ref[...], k1_ref[...], k2_ref[...]], axis=0).astype(BF16)
        vv = jnp.concatenate([v0_ref[...], v1_ref[...], v2_ref[...]], axis=0).astype(BF16)
        lane = lax.broadcasted_iota(jnp.int32, (1, LANES), 1)
        masks = _head_masks()
        lse_cols = jnp.zeros((ATT_QB, LANES), F32)
        for p in range(ATT_HEADS // 2):
            cs = slice(p * LANES, (p + 1) * LANES)
            qt, kt, vt = q[:, cs], kk[:, cs], vv[:, cs]
            acc = jnp.zeros((ATT_QB, LANES), F32)
            for sub in range(2):
                h = 2 * p + sub
                s = _dot_nt(jnp.where(masks[sub], qt, 0), kt) + b_ref[h]
                mx = jnp.max(s, axis=-1, keepdims=True)
                e = jnp.exp(s - mx)
                l = jnp.sum(e, axis=-1, keepdims=True)
                acc = acc + _dot(e.astype(BF16), jnp.where(masks[sub], vt, 0)) * (1.0 / l)
                lse_cols = lse_cols + jnp.where(lane == h, mx + jnp.log(l), 0.0)
            o_ref[:, cs] = acc.astype(BF16)
        lse_ref[...] = lse_cols
        lset_ref[...] = lse_cols.T[0:SUBLANES, :]
        ya = _dot(oa_ref[...], wba_ref[...])
        yb = _dot(o_ref[...], wbb_ref[...])
        mix = _sigmoid(ga_ref[...]) * ya + _sigmoid(gb_ref[...]) * yb
        x2 = x_ref[...] + _dot(mix.astype(BF16), wout_ref[...])
        x2_ref[...] = x2
        r = lax.rsqrt(jnp.mean(x2 * x2, axis=-1, keepdims=True) + EPS)
        h2_ref[...] = (x2 * r * w2_ref[...]).astype(BF16)

    def kv_spec(off, cb):
        return pl.BlockSpec((ATT_QB, PB), lambda i: (jnp.maximum(i + off, 0), cb))

    half = pl.BlockSpec((ATT_QB, WIDTH_B), lambda i: (i, 0))
    row = pl.BlockSpec((ATT_QB, D_MODEL), lambda i: (i, 0))
    return pl.pallas_call(
        body, name="att_merge_fwd", grid=(nb,),
        in_specs=[pl.BlockSpec((ATT_QB, PB), lambda i: (i, CB_QB)),
                  kv_spec(-2, CB_KB), kv_spec(-1, CB_KB), kv_spec(0, CB_KB),
                  kv_spec(-2, CB_VB), kv_spec(-1, CB_VB), kv_spec(0, CB_VB),
                  pl.BlockSpec((None, ATT_HEADS, ATT_QB, ATT_KW), lambda i: (jnp.minimum(i, 2), 0, 0, 0)),
                  half, pl.BlockSpec((ATT_QB, D_MODEL), lambda i: (i, CB_GA)),
                  pl.BlockSpec((ATT_QB, D_MODEL), lambda i: (i, CB_GB)), row,
                  _resident((KEY_A, D_MODEL)), _resident((WIDTH_B, D_MODEL)), _resident((D_MODEL, D_MODEL)),
                  _resident((1, D_MODEL))],
        out_specs=[half, pl.BlockSpec((ATT_QB, LANES), lambda i: (i, 0)),
                   pl.BlockSpec((SUBLANES, ATT_QB), lambda i: (0, i)), row, row],
        out_shape=[jax.ShapeDtypeStruct((t, WIDTH_B), BF16), jax.ShapeDtypeStruct((t, LANES), F32),
                   jax.ShapeDtypeStruct((SUBLANES, t), F32), jax.ShapeDtypeStruct((t, D_MODEL), F32),
                   jax.ShapeDtypeStruct((t, D_MODEL), BF16)],
        compiler_params=_cparams("parallel"),
    )(proj, proj, proj, proj, proj, proj, proj, bias_q, oan, proj, proj, x, wba, wbb, wout, w2)


def _att_dq(proj, bias_q, lse, d_ob, ob, dproj):
    t = proj.shape[0]
    nb = t // ATT_QB
    scale = ATT_DH ** -0.5
    nslab = ATT_QB // SUBLANES

    def body(q_ref, k0_ref, k1_ref, k2_ref, v0_ref, v1_ref, v2_ref, b_ref, lse_ref, do_ref, o_ref, dp_in_ref,
             dq_ref, dlt_ref, slab_ref):
        i = pl.program_id(0)

        @pl.when(i == 0)
        def _():
            slab_ref[...] = jnp.zeros_like(slab_ref)

        q = (q_ref[...] * scale).astype(BF16)
        kk = jnp.concatenate([k0_ref[...], k1_ref[...], k2_ref[...]], axis=0).astype(BF16)
        vv = jnp.concatenate([v0_ref[...], v1_ref[...], v2_ref[...]], axis=0).astype(BF16)
        do = do_ref[...].astype(BF16)
        do_o = do_ref[...] * o_ref[...].astype(F32)
        lane = lax.broadcasted_iota(jnp.int32, (1, LANES), 1)
        masks = _head_masks()
        lse_all = lse_ref[...]
        dlt_cols = jnp.zeros((ATT_QB, LANES), F32)
        zpad = jnp.zeros((SUBLANES, ATT_VEC - ATT_KW), F32)
        for p in range(ATT_HEADS // 2):
            cs = slice(p * LANES, (p + 1) * LANES)
            qt, kt, vt, dot_ = q[:, cs], kk[:, cs], vv[:, cs], do[:, cs]
            acc = jnp.zeros((ATT_QB, LANES), F32)
            for sub in range(2):
                h = 2 * p + sub
                s = _dot_nt(jnp.where(masks[sub], qt, 0), kt) + b_ref[h]
                pr = jnp.exp(s - lse_all[:, h:h + 1])
                dp = _dot_nt(jnp.where(masks[sub], dot_, 0), vt)
                dl = jnp.sum(jnp.where(masks[sub], do_o[:, cs], 0.0), axis=-1, keepdims=True)
                ds = pr * (dp - dl)
                acc = acc + _dot(ds.astype(BF16), jnp.where(masks[sub], kt, 0)) * scale
                dlt_cols = dlt_cols + jnp.where(lane == h, dl, 0.0)
                sl = jnp.zeros((SUBLANES, ATT_VEC), F32)
                for a in range(nslab):
                    piece = jnp.concatenate([ds[a * SUBLANES:(a + 1) * SUBLANES, :], zpad], axis=1)
                    sl = sl + (piece if a == 0 else pltpu.roll(piece, ATT_VEC - SUBLANES * a, 1))
                slab_ref[h] += sl
            dq_ref[:, cs] = acc.astype(BF16)
        dlt_ref[...] = dlt_cols.T[0:SUBLANES, :]

    def kv_spec(off, cb):
        return pl.BlockSpec((ATT_QB, PB), lambda i: (jnp.maximum(i + off, 0), cb))

    return pl.pallas_call(
        body, name="att_dq", grid=(nb,),
        in_specs=[pl.BlockSpec((ATT_QB, PB), lambda i: (i, CB_QB)),
                  kv_spec(-2, CB_KB), kv_spec(-1, CB_KB), kv_spec(0, CB_KB),
                  kv_spec(-2, CB_VB), kv_spec(-1, CB_VB), kv_spec(0, CB_VB),
                  pl.BlockSpec((None, ATT_HEADS, ATT_QB, ATT_KW), lambda i: (jnp.minimum(i, 2), 0, 0, 0)),
                  pl.BlockSpec((ATT_QB, LANES), lambda i: (i, 0)),
                  pl.BlockSpec((ATT_QB, WIDTH_B), lambda i: (i, 0)), pl.BlockSpec((ATT_QB, WIDTH_B), lambda i: (i, 0)),
                  pl.BlockSpec(memory_space=pl.ANY)],
        out_specs=[_dp_spec(ATT_QB, DP_QB),
                   pl.BlockSpec((SUBLANES, ATT_QB), lambda i: (0, i)),
                   pl.BlockSpec((ATT_HEADS, SUBLANES, ATT_VEC), lambda i: (0, 0, 0))],
        out_shape=[jax.ShapeDtypeStruct(dproj.shape, dproj.dtype), jax.ShapeDtypeStruct((SUBLANES, t), F32),
                   jax.ShapeDtypeStruct((ATT_HEADS, SUBLANES, ATT_VEC), F32)],
        input_output_aliases={11: 0},
        compiler_params=_cparams("arbitrary"),
    )(proj, proj, proj, proj, proj, proj, proj, bias_q, lse, d_ob, ob, dproj)


def _att_dkv(proj, bias_k, lse_t, dlt_t, d_ob, dproj):
    t = proj.shape[0]
    nb = t // ATT_QB
    scale = ATT_DH ** -0.5

    def body(k_ref, v_ref, q0_ref, q1_ref, q2_ref, d0_ref, d1_ref, d2_ref, l0_ref, l1_ref, l2_ref,
             e0_ref, e1_ref, e2_ref, b_ref, dp_in_ref, dkv_ref):
        i = pl.program_id(0)
        k = k_ref[...].astype(BF16)
        v = v_ref[...].astype(BF16)
        qq = (jnp.concatenate([q0_ref[...], q1_ref[...], q2_ref[...]], axis=0) * scale).astype(BF16)
        do = jnp.concatenate([d0_ref[...], d1_ref[...], d2_ref[...]], axis=0).astype(BF16)
        lse = jnp.concatenate([l0_ref[...], l1_ref[...], l2_ref[...]], axis=1)
        dlt = jnp.concatenate([e0_ref[...], e1_ref[...], e2_ref[...]], axis=1)
        masks = _head_masks()
        for p in range(ATT_HEADS // 2):
            cs = slice(p * LANES, (p + 1) * LANES)
            kt, vt, qt, dot_ = k[:, cs], v[:, cs], qq[:, cs], do[:, cs]
            acc_k = jnp.zeros((ATT_QB, LANES), F32)
            acc_v = jnp.zeros((ATT_QB, LANES), F32)
            for sub in range(2):
                h = 2 * p + sub
                st = _dot_nt(jnp.where(masks[sub], kt, 0), qt) + b_ref[h]
                pt = jnp.exp(st - lse[h:h + 1, :])
                dot_m = jnp.where(masks[sub], dot_, 0)
                acc_v = acc_v + _dot(pt.astype(BF16), dot_m)
                dpt = _dot_nt(jnp.where(masks[sub], vt, 0), dot_)
                dst = pt * (dpt - dlt[h:h + 1, :])
                acc_k = acc_k + _dot(dst.astype(BF16), jnp.where(masks[sub], qt, 0))
            dkv_ref[:, cs] = acc_k.astype(BF16)
            dkv_ref[:, WIDTH_B + p * LANES:WIDTH_B + (p + 1) * LANES] = acc_v.astype(BF16)

    def q_spec(off, cb):
        return pl.BlockSpec((ATT_QB, PB), lambda i: (jnp.minimum(i + off, nb - 1), cb))

    def d_spec(off):
        return pl.BlockSpec((ATT_QB, WIDTH_B), lambda i: (jnp.minimum(i + off, nb - 1), 0))

    def r_spec(off):
        return pl.BlockSpec((SUBLANES, ATT_QB), lambda i: (0, jnp.minimum(i + off, nb - 1)))

    row = pl.BlockSpec((ATT_QB, WIDTH_B), lambda i: (i, 0))
    return pl.pallas_call(
        body, name="att_dkv", grid=(nb,),
        in_specs=[pl.BlockSpec((ATT_QB, PB), lambda i: (i, CB_KB)), pl.BlockSpec((ATT_QB, PB), lambda i: (i, CB_VB)),
                  q_spec(0, CB_QB), q_spec(1, CB_QB), q_spec(2, CB_QB),
                  d_spec(0), d_spec(1), d_spec(2), r_spec(0), r_spec(1), r_spec(2),
                  r_spec(0), r_spec(1), r_spec(2),
                  pl.BlockSpec((None, ATT_HEADS, ATT_QB, ATT_KW), lambda i: (jnp.minimum(nb - 1 - i, 2), 0, 0, 0)),
                  pl.BlockSpec(memory_space=pl.ANY)],
        out_specs=_dp_spec(ATT_QB, DP_KVB),
        out_shape=jax.ShapeDtypeStruct(dproj.shape, dproj.dtype),
        input_output_aliases={15: 0},
        compiler_params=_cparams("parallel"),
    )(proj, proj, proj, proj, proj, d_ob, d_ob, d_ob, lse_t, lse_t, lse_t, dlt_t, dlt_t, dlt_t, bias_k, dproj)


def _relbias_grad(slabs):
    ohq, _ = _bias_onehots()

    def body(s_ref, oh_ref, o_ref):
        sv = s_ref[...]
        vec = sv[0:1, :]
        for b in range(1, SUBLANES):
            vec = vec + pltpu.roll(sv[b:b + 1, :], ATT_VEC - b, 1)
        o_ref[...] = _dot_nt(jnp.broadcast_to(vec, (SUBLANES, ATT_VEC)), oh_ref[...], HIGHEST)[0:1, :]

    h = slabs.shape[0]
    return pl.pallas_call(
        body, name="att_dbias", grid=(h,),
        in_specs=[pl.BlockSpec((None, SUBLANES, ATT_VEC), lambda i: (i, 0, 0)),
                  pl.BlockSpec((3 * LANES, ATT_VEC), lambda i: (0, 0))],
        out_specs=pl.BlockSpec((None, 1, 3 * LANES), lambda i: (i, 0, 0)),
        out_shape=jax.ShapeDtypeStruct((h, 1, 3 * LANES), F32),
        compiler_params=_cparams("parallel"),
    )(slabs, jnp.asarray(ohq))


GDN_TM = 512
GDN_CB = 8
HALO = SUBLANES


def _conv_taps(ext, width, lead, n):
    return [(ext if k == width - 1 else pltpu.roll(ext, width - 1 - k, 0))[lead:lead + n] for k in range(width)]


def _next_halo_spec(tm, width, cb, t):
    return pl.BlockSpec((HALO, width), lambda i: (jnp.minimum((i + 1) * (tm // HALO), t // HALO - 1), cb))


def _in_proj_prep(h1, w_all, conv_w, carry):
    t = h1.shape[0]
    tm = GDN_TM
    nt = t // tm
    arrays, scatter = carry
    nx = len(arrays)
    c0 = CB_QA * PB

    def body(*refs):
        h_ref, w_ref, cw_ref = refs[:3]
        srcs = refs[3:3 + nx]
        proj_ref, qn_ref, kn_ref, vo_ref, y_ref = refs[3 + nx:8 + nx]
        dsts = refs[8 + nx:8 + 2 * nx]
        prev_ref = refs[8 + 2 * nx]
        i = pl.program_id(0)
        local, remote = _exchange_copies(srcs, dsts, scatter, *refs[9 + 2 * nx:])

        @pl.when(i == 0)
        def _():
            for cp in local + remote:
                cp.start()
            prev_ref[...] = jnp.zeros_like(prev_ref)

        proj = _dot(h_ref[...], w_ref[...])
        proj_ref[...] = proj
        xin = proj[:, c0:c0 + 3 * KEY_A]
        ext = jnp.concatenate([prev_ref[...], xin], axis=0)
        prev_ref[...] = xin[tm - HALO:tm]
        taps = _conv_taps(ext, GDN_CONV, HALO, tm)
        cw = cw_ref[...]
        y = sum(cw[k:k + 1, :] * taps[k] for k in range(GDN_CONV))
        y_ref[...] = y
        a = y * _sigmoid(y)
        for idx, o_ref in enumerate((qn_ref, kn_ref)):
            for h in range(GDN_HEADS):
                cs = slice(h * GDN_DK, (h + 1) * GDN_DK)
                seg = a[:, idx * KEY_A + h * GDN_DK:idx * KEY_A + (h + 1) * GDN_DK]
                o_ref[:, cs] = seg * lax.rsqrt(jnp.sum(seg * seg, axis=-1, keepdims=True) + EPS)
        vo_ref[...] = a[:, 2 * KEY_A:]

        @pl.when(i == nt - 1)
        def _():
            for cp in remote + local:
                cp.wait()

    any_spec = pl.BlockSpec(memory_space=pl.ANY)
    row = pl.BlockSpec((tm, KEY_A), lambda i: (i, 0))
    return pl.pallas_call(
        body, name="in_proj", grid=(nt,),
        in_specs=[pl.BlockSpec((tm, D_MODEL), lambda i: (i, 0)), _resident((D_MODEL, PROJ_W)),
                  _resident((GDN_CONV, 3 * KEY_A))] + [any_spec] * nx,
        out_specs=[pl.BlockSpec((tm, PROJ_W), lambda i: (i, 0)), row, row, row,
                   pl.BlockSpec((tm, 3 * KEY_A), lambda i: (i, 0))] + [any_spec] * nx,
        out_shape=[jax.ShapeDtypeStruct((t, PROJ_W), F32)] + [jax.ShapeDtypeStruct((t, KEY_A), F32)] * 3
        + [jax.ShapeDtypeStruct((t, 3 * KEY_A), F32)] + _exchange_shapes(arrays, scatter),
        scratch_shapes=[pltpu.VMEM((HALO, 3 * KEY_A), F32)] + _exchange_sems(nx),
        compiler_params=_cparams("arbitrary"),
    )(h1, w_all, conv_w, *arrays)


def _gdn_prep_bwd(proj, ycv, conv_w, dqn, dkn, dv, dproj):
    t = proj.shape[0]
    tm = GDN_TM
    nt = t // tm
    n_ext = tm + HALO

    def body(q_ref, k_ref, v_ref, y_ref, ny_ref, dq_ref, dk_ref, dv_ref, ndq_ref, ndk_ref, ndv_ref, w_ref, dp_in_ref,
             out_ref, dw_ref):
        i = pl.program_id(0)
        last = i == nt - 1

        @pl.when(i == 0)
        def _():
            dw_ref[...] = jnp.zeros_like(dw_ref)

        groups = ((q_ref, dq_ref, ndq_ref), (k_ref, dk_ref, ndk_ref), (v_ref, dv_ref, ndv_ref))
        for idx, (x_ref, d_ref, nd_ref) in enumerate(groups):
            cs_all = slice(idx * KEY_A, (idx + 1) * KEY_A)
            w = w_ref[:, cs_all]
            y = jnp.concatenate([y_ref[:, cs_all], jnp.where(last, 0.0, ny_ref[:, cs_all])], axis=0)
            sg = _sigmoid(y)
            a = y * sg
            dup = jnp.concatenate([d_ref[...], jnp.where(last, 0.0, nd_ref[...])], axis=0)
            if idx < 2:
                segs = []
                for h in range(GDN_HEADS):
                    cs = slice(h * GDN_DK, (h + 1) * GDN_DK)
                    seg = a[:, cs]
                    r = lax.rsqrt(jnp.sum(seg * seg, axis=-1, keepdims=True) + EPS)
                    nrm = seg * r
                    dn = dup[:, cs]
                    segs.append(r * (dn - nrm * jnp.sum(dn * nrm, axis=-1, keepdims=True)))
                da = jnp.concatenate(segs, axis=1)
            else:
                da = dup
            dy = da * sg * (1.0 + y * (1.0 - sg))
            xv = x_ref[...]
            dx = None
            for k in range(GDN_CONV):
                shift = GDN_CONV - 1 - k
                tap = (dy if shift == 0 else pltpu.roll(dy, n_ext - shift, 0))[:tm]
                term = w[k:k + 1, :] * tap
                dx = term if dx is None else dx + term
                dw_ref[k:k + 1, cs_all] += jnp.sum(tap * xv, axis=0, keepdims=True)
            out_ref[:, cs_all] = dx.astype(BF16)

    row = pl.BlockSpec((tm, KEY_A), lambda i: (i, 0))
    nrow = _next_halo_spec(tm, KEY_A, 0, t)
    return pl.pallas_call(
        body, name="gdn_prep_bwd", grid=(nt,),
        in_specs=[pl.BlockSpec((tm, PB), lambda i: (i, CB_QA)), pl.BlockSpec((tm, PB), lambda i: (i, CB_KA)),
                  pl.BlockSpec((tm, PB), lambda i: (i, CB_VA)),
                  pl.BlockSpec((tm, 3 * KEY_A), lambda i: (i, 0)), _next_halo_spec(tm, 3 * KEY_A, 0, t),
                  row, row, row, nrow, nrow, nrow,
                  pl.BlockSpec((GDN_CONV, 3 * KEY_A), lambda i: (0, 0)), pl.BlockSpec(memory_space=pl.ANY)],
        out_specs=[_dp_spec(tm, DP_QKVA), pl.BlockSpec((SUBLANES, 3 * KEY_A), lambda i: (0, 0))],
        out_shape=[jax.ShapeDtypeStruct(dproj.shape, dproj.dtype), jax.ShapeDtypeStruct((SUBLANES, 3 * KEY_A), F32)],
        input_output_aliases={12: 0},
        compiler_params=_cparams("arbitrary"),
    )(proj, proj, proj, ycv, ycv, dqn, dkn, dv, dqn, dkn, dv, conv_w, dproj)


class _Pair(dict):
    __getattr__ = dict.__getitem__
    __setattr__ = dict.__setitem__


def _pairs_to_lanes(cols):
    lane = lax.broadcasted_iota(jnp.int32, (1, LANES), 1)
    out = jnp.zeros((cols[0].shape[0], LANES), F32)
    for p, col in enumerate(cols):
        out = out + jnp.where(lane == p, col, 0.0)
    return out


def _gdn_terms(bd, par, kn_ref, qn_ref):
    c = CHUNK
    ii = lax.broadcasted_iota(jnp.int32, (c, c), 0)
    jj = lax.broadcasted_iota(jnp.int32, (c, c), 1)
    strict, incl = ii > jj, ii >= jj
    ltri = incl.astype(F32)
    ts = []
    for cc in range(GDN_CB):
        for h in range(GDN_HEADS):
            t = _Pair(cc=cc, h=h, rows=slice(cc * c, (cc + 1) * c), cs=slice(h * GDN_DK, (h + 1) * GDN_DK),
                      strict=strict, incl=incl)
            t.beta = _sigmoid(bd[t.rows, h:h + 1])
            t.ea = jnp.exp(par[0:1, h:h + 1])
            t.sp_arg = bd[t.rows, GDN_HEADS + h:GDN_HEADS + h + 1] + par[1:2, h:h + 1]
            t.g = -t.ea * _softplus(t.sp_arg)
            t.k = kn_ref[t.rows, t.cs]
            t.q = qn_ref[t.rows, t.cs] * (GDN_DK ** -0.5)
            t.kb, t.qb = t.k.astype(BF16), t.q.astype(BF16)
            ts.append(t)
    gall = _dot(ltri, _pairs_to_lanes([t.g for t in ts]), HIGHEST)
    gall_t = gall.T
    for p, t in enumerate(ts):
        t.gb = jnp.broadcast_to(gall[:, p:p + 1], (c, GDN_DK))
    for t in ts:
        t.kk = _dot_nt(t.kb, t.kb)
        t.qk = _dot_nt(t.qb, t.kb)
    for p, t in enumerate(ts):
        diff = t.gb[:, :c] - gall_t[p:p + 1, :]
        t.dec_s = jnp.exp(jnp.where(strict, diff, NEG))
        t.dec_i = jnp.exp(jnp.where(incl, diff, NEG))
        t.gam = jnp.exp(t.gb)
        glast = t.gb[c - 1:c, :]
        t.e_rest = jnp.exp(glast - t.gb)
        t.gl = jnp.exp(glast)
        t.p = t.qk * t.dec_i
    return ts


def _gdn_fwd(qn, kn, v, proj, par, gnw):
    t = qn.shape[0]
    c = CHUNK
    nc = t // c
    r_ = GDN_CB * c

    def body(qn_ref, kn_ref, v_ref, bd_ref, z_ref, par_ref, gnw_ref,
             oan_ref, o_ref, sp_ref, w_ref, u_ref, tm_ref, s_ref):
        @pl.when(pl.program_id(0) == 0)
        def _():
            s_ref[...] = jnp.zeros_like(s_ref)

        bd, par, gnw_v = bd_ref[...], par_ref[...], gnw_ref[...]
        eye = (lax.broadcasted_iota(jnp.int32, (c, c), 0) == lax.broadcasted_iota(jnp.int32, (c, c), 1)).astype(F32)
        ts = _gdn_terms(bd, par, kn_ref, qn_ref)
        for t in ts:
            t.vv = v_ref[t.rows, t.cs]
            t.x = -(t.beta * t.kk * t.dec_s)
            t.tinv = eye + t.x
        for t in ts:
            t.xs = _split(t.x)
        for _ in range(5):
            for t in ts:
                t.xs = _split(_dot3s(t.xs, t.xs))
            for t in ts:
                t.tinv = t.tinv + _dot3s(_split(t.tinv), t.xs)
        for t in ts:
            tsp = _split(t.tinv)
            t.wm = _dot3s(tsp, _split((t.beta * t.gam) * t.k))
            t.uv = _dot3s(tsp, _split(t.beta * t.vv))
        for t in ts:
            w_ref[t.rows, t.cs] = t.wm
            tm_ref[t.cc, t.h] = t.tinv.T
            t.wb = t.wm.astype(BF16)
            t.qgb = (t.q * t.gam).astype(BF16)
            t.kdb = (t.k * t.e_rest).astype(BF16)
            t.pb = t.p.astype(BF16)
        state = [s_ref[h] for h in range(GDN_HEADS)]
        for cc in range(GDN_CB):
            tc = [t for t in ts if t.cc == cc]
            for t in tc:
                t.sh = state[t.h]
                t.sb = t.sh.astype(BF16)
            for t in tc:
                t.ws = _dot(t.wb, t.sb)
            for t in tc:
                t.u = t.uv - t.ws
                t.ub = t.u.astype(BF16)
            for t in tc:
                state[t.h] = t.gl * t.sh + _dot_tn(t.kdb, t.ub)
            for t in tc:
                t.o = _dot(t.qgb, t.sb) + _dot(t.pb, t.ub)
                sp_ref[cc, t.h] = t.sh
                u_ref[t.rows, t.cs] = t.u
                o_ref[t.rows, t.cs] = t.o
        for h in range(GDN_HEADS):
            s_ref[h] = state[h]
        for t in ts:
            zz = z_ref[t.rows, t.cs]
            rr = lax.rsqrt(jnp.mean(t.o * t.o, axis=-1, keepdims=True) + EPS)
            oan_ref[t.rows, t.cs] = ((t.o * rr) * gnw_v * (zz * _sigmoid(zz))).astype(BF16)

    row = pl.BlockSpec((r_, KEY_A), lambda i: (i, 0))
    return pl.pallas_call(
        body, name="gdn_fwd", grid=(nc // GDN_CB,),
        in_specs=[row, row, row, pl.BlockSpec((r_, LANES), lambda i: (i, CB_BD)),
                  pl.BlockSpec((r_, PB), lambda i: (i, CB_ZA)),
                  pl.BlockSpec((SUBLANES, LANES), lambda i: (0, 0)), pl.BlockSpec((1, GDN_DK), lambda i: (0, 0))],
        out_specs=[row, row, pl.BlockSpec((GDN_CB, GDN_HEADS, GDN_DK, GDN_DK), lambda i: (i, 0, 0, 0)),
                   row, row, pl.BlockSpec((GDN_CB, GDN_HEADS, c, c), lambda i: (i, 0, 0, 0))],
        out_shape=[jax.ShapeDtypeStruct((t, KEY_A), BF16), jax.ShapeDtypeStruct((t, KEY_A), F32),
                   jax.ShapeDtypeStruct((nc, GDN_HEADS, GDN_DK, GDN_DK), F32),
                   jax.ShapeDtypeStruct((t, KEY_A), F32), jax.ShapeDtypeStruct((t, KEY_A), F32),
                   jax.ShapeDtypeStruct((nc, GDN_HEADS, c, c), F32)],
        scratch_shapes=[pltpu.VMEM((GDN_HEADS, GDN_DK, GDN_DK), F32)],
        compiler_params=_cparams("arbitrary"),
    )(qn, kn, v, proj, proj, par, gnw)


def _gdn_bwd(qn, kn, v, proj, par, gnw, o, sprev, wst, ust, tst, d_oan, dproj):
    t = qn.shape[0]
    c = CHUNK
    nc = t // c
    nb = nc // GDN_CB
    r_ = GDN_CB * c

    def body(qn_ref, kn_ref, v_ref, bd_ref, z_ref, par_ref, gnw_ref, o_ref, sp_ref, w_ref, u_ref, tm_ref, do_ref,
             dp_in_ref, dqn_ref, dkn_ref, dv_ref, dzb_ref, acc_ref, ds_ref):
        @pl.when(pl.program_id(0) == 0)
        def _():
            ds_ref[...] = jnp.zeros_like(ds_ref)
            acc_ref[...] = jnp.zeros_like(acc_ref)

        bd, par, gnw_v = bd_ref[...], par_ref[...], gnw_ref[...]
        lane = lax.broadcasted_iota(jnp.int32, (1, LANES), 1)
        rix = lax.broadcasted_iota(jnp.int32, (c, 1), 0)
        ii = lax.broadcasted_iota(jnp.int32, (c, c), 0)
        jj = lax.broadcasted_iota(jnp.int32, (c, c), 1)
        upper = (jj >= ii).astype(F32)
        acc_a = jnp.zeros((1, LANES), F32)
        acc_d = jnp.zeros((1, LANES), F32)
        acc_g = jnp.zeros((1, LANES), F32)
        ts = _gdn_terms(bd, par, kn_ref, qn_ref)
        for t in ts:
            t.vv = v_ref[t.rows, t.cs]
            t.sh = sp_ref[t.cc, t.h]
            t.sb = t.sh.astype(BF16)
            t.wm, t.u, t.tinv_t = w_ref[t.rows, t.cs], u_ref[t.rows, t.cs], tm_ref[t.cc, t.h]
            t.wb, t.ub = t.wm.astype(BF16), t.u.astype(BF16)
            ov, zz, dout = o_ref[t.rows, t.cs], z_ref[t.rows, t.cs], do_ref[t.rows, t.cs]
            sg = _sigmoid(zz)
            sil = zz * sg
            rr = lax.rsqrt(jnp.mean(ov * ov, axis=-1, keepdims=True) + EPS)
            on = ov * rr
            dzb_ref[t.rows, t.cs] = (dout * on * gnw_v * (sg * (1.0 + zz * (1.0 - sg)))).astype(BF16)
            acc_g = acc_g + jnp.sum(dout * on * sil, axis=0, keepdims=True)
            don = dout * gnw_v * sil
            t.dob = (rr * (don - on * jnp.mean(don * on, axis=-1, keepdims=True))).astype(BF16)
            t.qg = t.q * t.gam
            t.kd = t.k * t.e_rest
            t.qgb, t.kdb = t.qg.astype(BF16), t.kd.astype(BF16)
            t.ptb = t.p.T.astype(BF16)
        for t in ts:
            t.du0 = _dot(t.ptb, t.dob)
            t.ds0 = _dot_tn(t.qgb, t.dob)
            t.dqg = _dot_nt(t.dob, t.sb)
            t.dp = _dot_nt(t.dob, t.ub)
            t.uv = t.u + _dot(t.wb, t.sb)
        dstate = [ds_ref[h] for h in range(GDN_HEADS)]
        for cc in reversed(range(GDN_CB)):
            tc = [t for t in ts if t.cc == cc]
            for t in tc:
                t.dsn = dstate[t.h]
                t.dsnb = t.dsn.astype(BF16)
            for t in tc:
                t.du = t.du0 + _dot(t.kdb, t.dsnb)
            for t in tc:
                t.dub = t.du.astype(BF16)
            for t in tc:
                dstate[t.h] = t.gl * t.dsn + t.ds0 - _dot_tn(t.wb, t.dub)
            for t in tc:
                t.dkd = _dot_nt(t.ub, t.dsnb)
                t.dgl = jnp.sum(jnp.sum(t.dsn * t.sh, axis=1, keepdims=True), axis=0, keepdims=True)
                t.dwm = -_dot_nt(t.dub, t.sb)
        for h in range(GDN_HEADS):
            ds_ref[h] = dstate[h]
        for t in ts:
            tsp = _split(t.tinv_t)
            t.dbk = _dot3s(tsp, _split(t.dwm))
            t.dbv = _dot3s(tsp, _split(t.du))
        for t in ts:
            d_a = -(_dot_nt(t.dbk.astype(BF16), t.wb) + _dot_nt(t.dbv.astype(BF16), t.uv.astype(BF16)))
            t.d_a = jnp.where(t.strict, d_a, 0.0)
        for t in ts:
            t.dkk = t.d_a * t.beta * t.dec_s
            t.dqk = t.dp * t.dec_i
            t.dqkb = t.dqk.astype(BF16)
        for t in ts:
            t.dq = _dot(t.dqkb, t.kb) + t.dqg * t.gam
            t.dk = (t.dbk * (t.beta * t.gam) + _dot_tn(t.dqkb, t.qb) + _dot((t.dkk + t.dkk.T).astype(BF16), t.kb)
                    + t.dkd * t.e_rest)
        for t in ts:
            dbeta = (jnp.sum(t.d_a * t.kk * t.dec_s, axis=-1, keepdims=True)
                     + jnp.sum(t.dbk * t.k * t.gam, axis=-1, keepdims=True) + jnp.sum(t.dbv * t.vv, axis=-1, keepdims=True))
            t.dbl = dbeta * t.beta * (1.0 - t.beta)
            dv_ref[t.rows, t.cs] = t.dbv * t.beta
            bk = (t.beta * t.gam) * t.k
            zc = jnp.sum(t.dkd * t.kd, axis=-1, keepdims=True)
            xs = t.dkk * t.kk + t.dp * t.p
            dgc = (jnp.sum(xs, axis=-1, keepdims=True) - jnp.sum(xs.T, axis=-1, keepdims=True)
                   + jnp.sum(t.dbk * bk, axis=-1, keepdims=True) + jnp.sum(t.dqg * t.qg, axis=-1, keepdims=True) - zc)
            dglast = jnp.sum(zc, axis=0, keepdims=True) + t.dgl * t.gl[:, 0:1]
            t.dgc = dgc + jnp.where(rix == c - 1, dglast, 0.0)
        dgall = _dot(upper, _pairs_to_lanes([t.dgc for t in ts]), HIGHEST)
        for p, t in enumerate(ts):
            t.dg = dgall[:, p:p + 1]
        dbd_tiles = [jnp.zeros((c, LANES), F32) for _ in range(GDN_CB)]
        for t in ts:
            ddl = t.dg * (-t.ea) * _sigmoid(t.sp_arg)
            acc_a = acc_a + jnp.where(lane == t.h, jnp.sum(t.dg * t.g, axis=0, keepdims=True), 0.0)
            acc_d = acc_d + jnp.where(lane == t.h, jnp.sum(ddl, axis=0, keepdims=True), 0.0)
            dbd_tiles[t.cc] = (dbd_tiles[t.cc] + jnp.where(lane == t.h, t.dbl, 0.0)
                               + jnp.where(lane == GDN_HEADS + t.h, ddl, 0.0))
            dqn_ref[t.rows, t.cs] = t.dq * (GDN_DK ** -0.5)
            dkn_ref[t.rows, t.cs] = t.dk
        for cc in range(GDN_CB):
            dzb_ref[cc * c:(cc + 1) * c, KEY_A:KEY_A + LANES] = dbd_tiles[cc].astype(BF16)
        acc_ref[0:1, :] += acc_a
        acc_ref[1:2, :] += acc_d
        acc_ref[2:3, :] += acc_g

    def rev(i):
        return nb - 1 - i

    row = pl.BlockSpec((r_, KEY_A), lambda i: (rev(i), 0))
    st = pl.BlockSpec((GDN_CB, GDN_HEADS, GDN_DK, GDN_DK), lambda i: (rev(i), 0, 0, 0))
    tt_spec = pl.BlockSpec((GDN_CB, GDN_HEADS, c, c), lambda i: (rev(i), 0, 0, 0))
    return pl.pallas_call(
        body, name="gdn_bwd", grid=(nb,),
        in_specs=[row, row, row, pl.BlockSpec((r_, LANES), lambda i: (rev(i), CB_BD)),
                  pl.BlockSpec((r_, PB), lambda i: (rev(i), CB_ZA)),
                  pl.BlockSpec((SUBLANES, LANES), lambda i: (0, 0)), pl.BlockSpec((1, GDN_DK), lambda i: (0, 0)),
                  row, st, row, row, tt_spec, row, pl.BlockSpec(memory_space=pl.ANY)],
        out_specs=[row, row, row, _dp_spec(r_, DP_ZBD, rev), pl.BlockSpec((SUBLANES, LANES), lambda i: (0, 0))],
        out_shape=[jax.ShapeDtypeStruct((t, KEY_A), F32)] * 3 + [jax.ShapeDtypeStruct(dproj.shape, dproj.dtype),
                                                                jax.ShapeDtypeStruct((SUBLANES, LANES), F32)],
        input_output_aliases={13: 3},
        scratch_shapes=[pltpu.VMEM((GDN_HEADS, GDN_DK, GDN_DK), F32)],
        compiler_params=_cparams("arbitrary"),
    )(qn, kn, v, proj, proj, par, gnw, o, sprev, wst, ust, tst, d_oan, dproj)


def _merge_bwd(dx2b, oan, ob, proj, wba, wbb, wout_t, wba_t, wbb_t, tm=512):
    t = dx2b.shape[0]

    def body(dx_ref, oa_ref, ob_ref, ga_ref, gb_ref, wba_ref, wbb_ref, woutt_ref, wbat_ref, wbbt_ref,
             dg_ref, doa_ref, dob_ref, gout_ref, gba_ref, gbb_ref):
        @pl.when(pl.program_id(0) == 0)
        def _():
            gout_ref[...] = jnp.zeros_like(gout_ref)
            gba_ref[...] = jnp.zeros_like(gba_ref)
            gbb_ref[...] = jnp.zeros_like(gbb_ref)

        dx, oa, ob = dx_ref[...], oa_ref[...], ob_ref[...]
        dmix = _dot(dx, woutt_ref[...])
        ya = _dot(oa, wba_ref[...])
        yb = _dot(ob, wbb_ref[...])
        sa, sb = _sigmoid(ga_ref[...]), _sigmoid(gb_ref[...])
        gout_ref[...] += _dot_tn((sa * ya + sb * yb).astype(BF16), dx)
        dg_ref[:, :D_MODEL] = (dmix * ya * sa * (1.0 - sa)).astype(BF16)
        dg_ref[:, D_MODEL:] = (dmix * yb * sb * (1.0 - sb)).astype(BF16)
        dya = (dmix * sa).astype(BF16)
        dyb = (dmix * sb).astype(BF16)
        gba_ref[...] += _dot_tn(oa, dya)
        gbb_ref[...] += _dot_tn(ob, dyb)
        doa_ref[...] = _dot(dya, wbat_ref[...])
        dob_ref[...] = _dot(dyb, wbbt_ref[...])

    half = pl.BlockSpec((tm, KEY_A), lambda i: (i, 0))
    row = pl.BlockSpec((tm, D_MODEL), lambda i: (i, 0))
    wsmall = pl.BlockSpec((KEY_A, D_MODEL), lambda i: (0, 0))
    wsmall_t = pl.BlockSpec((D_MODEL, KEY_A), lambda i: (0, 0))
    wfull = pl.BlockSpec((D_MODEL, D_MODEL), lambda i: (0, 0))
    return pl.pallas_call(
        body, name="merge_bwd", grid=(t // tm,),
        in_specs=[row, half, half, pl.BlockSpec((tm, D_MODEL), lambda i: (i, CB_GA)),
                  pl.BlockSpec((tm, D_MODEL), lambda i: (i, CB_GB)), wsmall, wsmall, wfull, wsmall_t, wsmall_t],
        out_specs=[_dp_spec(tm, DP_GATES), half, half, wfull, wsmall, wsmall],
        out_shape=[jax.ShapeDtypeStruct((t, PROJ_W), BF16), jax.ShapeDtypeStruct((t, KEY_A), F32),
                   jax.ShapeDtypeStruct((t, KEY_A), F32), jax.ShapeDtypeStruct((D_MODEL, D_MODEL), F32),
                   jax.ShapeDtypeStruct((KEY_A, D_MODEL), F32), jax.ShapeDtypeStruct((WIDTH_B, D_MODEL), F32)],
        compiler_params=_cparams("arbitrary"),
    )(dx2b, oan, ob, proj, proj, wba, wbb, wout_t, wba_t, wbb_t)


FFN_TM = 128
FFN_W = 2 * D_FF


def _resident(shape):
    return pl.BlockSpec(shape, lambda i: (0,) * len(shape), pipeline_mode=pl.Buffered(1))


def _ffn_fwd(h2, wup, cw, cb, wdown, x2, tgt, w3):
    t = x2.shape[0]
    tm = FFN_TM

    def body(h2_ref, wup_ref, cw_ref, cb_ref, wd_ref, x2_ref, tgt_ref, w3_ref, up_ref, u_ref, dx_ref, dxb_ref, act_ref,
             acc_ref, prev_ref):
        @pl.when(pl.program_id(0) == 0)
        def _():
            acc_ref[...] = jnp.zeros_like(acc_ref)
            prev_ref[...] = jnp.zeros_like(prev_ref)

        up = _dot(h2_ref[...], wup_ref[...])
        up_ref[...] = up
        ext = jnp.concatenate([prev_ref[...], up], axis=0)
        prev_ref[...] = up[tm - HALO:tm]
        taps = _conv_taps(ext, FFN_CONV, HALO, tm)
        cw_v = cw_ref[...]
        u = sum(cw_v[k:k + 1, :] * taps[k] for k in range(FFN_CONV)) + cb_ref[...]
        u_ref[...] = u
        gate, upp = u[:, :D_FF], u[:, D_FF:]
        act = (gate * _sigmoid(gate) * upp).astype(BF16)
        act_ref[...] = act
        x3 = x2_ref[...] + _dot(act, wd_ref[...])
        r = lax.rsqrt(jnp.mean(x3 * x3, axis=-1, keepdims=True) + EPS)
        xh = x3 * r
        w3v = w3_ref[...]
        err = xh * w3v - tgt_ref[...]
        loss = 0.5 * jnp.sum(jnp.mean(err * err, axis=-1, keepdims=True), axis=0, keepdims=True)
        dy = err * (1.0 / D_MODEL)
        acc_ref[0:1, :] += jnp.sum(dy * xh, axis=0, keepdims=True)
        acc_ref[1:2, :] += jnp.broadcast_to(loss, (1, D_MODEL))
        dxh = dy * w3v
        dx = r * (dxh - xh * jnp.mean(dxh * xh, axis=-1, keepdims=True))
        dx_ref[...] = dx
        dxb_ref[...] = dx.astype(BF16)

    row = pl.BlockSpec((tm, D_MODEL), lambda i: (i, 0))
    return pl.pallas_call(
        body, name="ffn_fwd", grid=(t // tm,),
        in_specs=[row, _resident((D_MODEL, FFN_W)), _resident((SUBLANES, FFN_W)), _resident((1, FFN_W)),
                  _resident((D_FF, D_MODEL)), row, row, _resident((1, D_MODEL))],
        out_specs=[pl.BlockSpec((tm, FFN_W), lambda i: (i, 0)), pl.BlockSpec((tm, FFN_W), lambda i: (i, 0)), row, row,
                   pl.BlockSpec((tm, D_FF), lambda i: (i, 0)), pl.BlockSpec((SUBLANES, D_MODEL), lambda i: (0, 0))],
        out_shape=[jax.ShapeDtypeStruct((t, FFN_W), F32), jax.ShapeDtypeStruct((t, FFN_W), F32),
                   jax.ShapeDtypeStruct((t, D_MODEL), F32),
                   jax.ShapeDtypeStruct((t, D_MODEL), BF16), jax.ShapeDtypeStruct((t, D_FF), BF16),
                   jax.ShapeDtypeStruct((SUBLANES, D_MODEL), F32)],
        scratch_shapes=[pltpu.VMEM((HALO, FFN_W), F32)],
        compiler_params=_cparams("arbitrary"),
    )(h2, wup, cw, cb, wdown, x2, tgt, w3)


def _ffn_bwd(dx3b, wdown_t, up, u, cw, wup_t, x2, w2, dx3, carry):
    t = up.shape[0]
    tm = FFN_TM
    nt = t // tm
    n_ext = tm + HALO
    arrays, scatter = carry
    nx = len(arrays)

    def rev(i):
        return nt - 1 - i

    def body(*refs):
        dx_ref, wdt_ref, up_ref, u_ref, cw_ref, wupt_ref, x2_ref, w2_ref, dres_ref = refs[:9]
        srcs = refs[9:9 + nx]
        dup_ref, acc_ref, dx2_ref, dx2b_ref, dw2_ref = refs[9 + nx:14 + nx]
        dsts = refs[14 + nx:14 + 2 * nx]
        nxt_ref = refs[14 + 2 * nx]
        i = pl.program_id(0)
        local, remote = _exchange_copies(srcs, dsts, scatter, *refs[15 + 2 * nx:])

        @pl.when(i == 0)
        def _():
            for cp in local + remote:
                cp.start()
            acc_ref[...] = jnp.zeros_like(acc_ref)
            dw2_ref[...] = jnp.zeros_like(dw2_ref)
            nxt_ref[...] = jnp.zeros_like(nxt_ref)

        dact = _dot(dx_ref[...], wdt_ref[...])
        gate, upp = u_ref[:, :D_FF], u_ref[:, D_FF:]
        sg = _sigmoid(gate)
        du = jnp.concatenate([dact * upp * (sg * (1.0 + gate * (1.0 - sg))), dact * (gate * sg)], axis=1)
        acc_ref[FFN_CONV:FFN_CONV + 1, :] += jnp.sum(du, axis=0, keepdims=True)
        ext = jnp.concatenate([du, nxt_ref[...]], axis=0)
        cw_v = cw_ref[...]
        upv = up_ref[...]
        dup = None
        for k in range(FFN_CONV):
            shift = FFN_CONV - 1 - k
            tap = du if shift == 0 else pltpu.roll(ext, n_ext - shift, 0)[:tm]
            term = cw_v[k:k + 1, :] * tap
            dup = term if dup is None else dup + term
            acc_ref[k:k + 1, :] += jnp.sum(tap * upv, axis=0, keepdims=True)
        dupb = dup.astype(BF16)
        dup_ref[...] = dupb
        nxt_ref[...] = du[0:HALO]
        dhv = _dot(dupb, wupt_ref[...])
        xv = x2_ref[...]
        r = lax.rsqrt(jnp.mean(xv * xv, axis=-1, keepdims=True) + EPS)
        xh = xv * r
        dw2_ref[0:1, :] += jnp.sum(dhv * xh, axis=0, keepdims=True)
        dxh = dhv * w2_ref[...]
        dx2 = dres_ref[...] + r * (dxh - xh * jnp.mean(dxh * xh, axis=-1, keepdims=True))
        dx2_ref[...] = dx2
        dx2b_ref[...] = dx2.astype(BF16)

        @pl.when(i == nt - 1)
        def _():
            for cp in remote + local:
                cp.wait()

    wide = pl.BlockSpec((tm, FFN_W), lambda i: (rev(i), 0))
    row = pl.BlockSpec((tm, D_MODEL), lambda i: (rev(i), 0))
    any_spec = pl.BlockSpec(memory_space=pl.ANY)
    return pl.pallas_call(
        body, name="ffn_bwd", grid=(nt,),
        in_specs=[row, _resident((D_MODEL, D_FF)), wide, wide,
                  _resident((SUBLANES, FFN_W)), _resident((FFN_W, D_MODEL)), row,
                  _resident((1, D_MODEL)), row] + [any_spec] * nx,
        out_specs=[wide, pl.BlockSpec((SUBLANES, FFN_W), lambda i: (0, 0)), row, row,
                   pl.BlockSpec((SUBLANES, D_MODEL), lambda i: (0, 0))] + [any_spec] * nx,
        out_shape=[jax.ShapeDtypeStruct((t, FFN_W), BF16), jax.ShapeDtypeStruct((SUBLANES, FFN_W), F32),
                   jax.ShapeDtypeStruct((t, D_MODEL), F32), jax.ShapeDtypeStruct((t, D_MODEL), BF16),
                   jax.ShapeDtypeStruct((SUBLANES, D_MODEL), F32)] + _exchange_shapes(arrays, scatter),
        scratch_shapes=[pltpu.VMEM((HALO, FFN_W), F32)] + _exchange_sems(nx),
        compiler_params=_cparams("arbitrary"),
    )(dx3b, wdown_t, up, u, cw, wup_t, x2, w2, dx3, *arrays)


def _adamw(parts, w, m, v, name, tr):
    r, cols = w.shape

    def body(p_ref, w_ref, m_ref, v_ref, g_ref, d_ref, mo_ref, vo_ref):
        g = p_ref[0].astype(F32)
        for s in range(1, N_DEV):
            g = g + p_ref[s].astype(F32)
        mm = ADAM_B1 * m_ref[...] + (1.0 - ADAM_B1) * g
        vv = ADAM_B2 * v_ref[...] + (1.0 - ADAM_B2) * (g * g)
        m_hat = mm / (1.0 - ADAM_B1 ** ADAM_STEP)
        v_hat = vv / (1.0 - ADAM_B2 ** ADAM_STEP)
        g_ref[...] = g
        d_ref[...] = -ADAM_LR * (m_hat / (jnp.sqrt(v_hat) + ADAM_EPS) + ADAM_WD * w_ref[...])
        mo_ref[...] = mm
        vo_ref[...] = vv

    assert r % tr == 0
    row = pl.BlockSpec((tr, cols), lambda i: (i, 0))
    return pl.pallas_call(
        body, name=name, grid=(r // tr,),
        in_specs=[pl.BlockSpec((N_DEV, tr, cols), lambda i: (0, i, 0)), row, row, row],
        out_specs=[row, row, row, row],
        out_shape=[jax.ShapeDtypeStruct((r, cols), F32)] * 4,
        compiler_params=_cparams("parallel"),
    )(parts, w, m, v)


def _mesh_pos():
    return lax.axis_index("x"), lax.axis_index("y"), lax.axis_index("c")


def _peer(pos, k):
    x, y, c = pos
    return (x ^ ((k >> 2) & 1), y ^ ((k >> 1) & 1), c ^ (k & 1))


def _flat_id(pos):
    return 4 * pos[0] + 2 * pos[1] + pos[2]


def _exchange_copies(srcs, dsts, scatter, send_sems, recv_sems, loc_sems):
    pos = _mesh_pos()
    me = _flat_id(pos)
    local, remote = [], []
    for j, (src, dst) in enumerate(zip(srcs, dsts)):
        local.append(pltpu.make_async_copy(src.at[me] if scatter[j] else src, dst.at[me], loc_sems.at[j]))
        for k in range(1, N_DEV):
            to = _peer(pos, k)
            remote.append(pltpu.make_async_remote_copy(
                src_ref=src.at[_flat_id(to)] if scatter[j] else src, dst_ref=dst.at[me],
                send_sem=send_sems.at[j, k - 1], recv_sem=recv_sems.at[j, k - 1],
                device_id=to, device_id_type=pl.DeviceIdType.MESH))
    return local, remote


def _exchange_shapes(arrays, scatter):
    return [jax.ShapeDtypeStruct(a.shape if s else (N_DEV,) + a.shape, a.dtype) for a, s in zip(arrays, scatter)]


def _exchange_sems(n):
    return [pltpu.SemaphoreType.DMA((n, N_DEV - 1)), pltpu.SemaphoreType.DMA((n, N_DEV - 1)), pltpu.SemaphoreType.DMA((n,))]


def _exchange(arrays, scatter, name):
    n = len(arrays)
    any_spec = pl.BlockSpec(memory_space=pl.ANY)

    def body(*refs):
        local, remote = _exchange_copies(refs[:n], refs[n:2 * n], scatter, *refs[2 * n:])
        for cp in local + remote:
            cp.start()
        for cp in remote:
            cp.wait()
        for cp in local:
            cp.wait()

    return pl.pallas_call(
        body, name=name, in_specs=[any_spec] * n, out_specs=[any_spec] * n,
        out_shape=_exchange_shapes(arrays, scatter), scratch_shapes=_exchange_sems(n),
    )(*arrays)


def _pad_rows(a, rows):
    return jnp.pad(a, ((0, rows - a.shape[0]),) + ((0, 0),) * (a.ndim - 1))


PACK_UNIT = SUBLANES * LANES


def _pack_lanes(parts, rows):
    out = []
    for a in parts:
        f = a.reshape(-1)
        out.append(jnp.pad(f, (0, (-f.shape[0]) % PACK_UNIT)).reshape(-1, LANES))
    packed = jnp.concatenate(out, axis=0)
    assert packed.shape[0] == rows, (packed.shape, rows)
    return packed


def _unpack_lanes(buf, shapes):
    out, r0 = [], 0
    for shp in shapes:
        n = math.prod(shp)
        nr = -(-n // PACK_UNIT) * SUBLANES
        out.append(buf[r0:r0 + nr].reshape(-1)[:n].reshape(shp))
        r0 += nr
    return out


def _col_shards(g):
    r, n = g.shape
    return g.reshape(r, N_DEV, n // N_DEV).transpose(1, 0, 2)


def _col_unshard(s):
    _, r, w = s.shape
    return s.transpose(1, 0, 2).reshape(r, N_DEV * w)


def _lane_rows(flat):
    n = flat.shape[1]
    return jnp.pad(flat, ((0, 0), (0, (-n) % PACK_UNIT))).reshape(N_DEV, -1, LANES)


SMALL_ROWS = 128
WS_ROWS = 32


def kernel(x, norm_mix_w, w_in, conv_qkv_w, a_log, dt_bias, gdn_norm_w, w_branch_a, w_branch_b, rel_bias, w_out, norm_ffn_w, w_up, conv_ffn_w, conv_ffn_b, w_down, norm_final_w, loss_target, m_norm_mix_w, m_w_in, m_conv_qkv_w, m_a_log, m_dt_bias, m_gdn_norm_w, m_w_branch_a, m_w_branch_b, m_rel_bias, m_w_out, m_norm_ffn_w, m_w_up, m_conv_ffn_w, m_conv_ffn_b, m_w_down, m_norm_final_w, v_norm_mix_w, v_w_in, v_conv_qkv_w, v_a_log, v_dt_bias, v_gdn_norm_w, v_w_branch_a, v_w_branch_b, v_rel_bias, v_w_out, v_norm_ffn_w, v_w_up, v_conv_ffn_w, v_conv_ffn_b, v_w_down, v_norm_final_w):
    big_w = (w_in, w_branch_a, w_branch_b, w_out, w_up, w_down, conv_qkv_w, conv_ffn_w)
    big_m = (m_w_in, m_w_branch_a, m_w_branch_b, m_w_out, m_w_up, m_w_down, m_conv_qkv_w, m_conv_ffn_w)
    big_v = (v_w_in, v_w_branch_a, v_w_branch_b, v_w_out, v_w_up, v_w_down, v_conv_qkv_w, v_conv_ffn_w)
    small_w = (norm_mix_w, a_log, dt_bias, gdn_norm_w, rel_bias, norm_ffn_w, conv_ffn_b, norm_final_w)
    small_m = (m_norm_mix_w, m_a_log, m_dt_bias, m_gdn_norm_w, m_rel_bias, m_norm_ffn_w, m_conv_ffn_b, m_norm_final_w)
    small_v = (v_norm_mix_w, v_a_log, v_dt_bias, v_gdn_norm_w, v_rel_bias, v_norm_ffn_w, v_conv_ffn_b, v_norm_final_w)

    xs, tgt = x[0], loss_target[0]
    ws = _pack_lanes(big_w[6:], WS_ROWS)
    h1, g_in, gs = _rmsnorm_gather(xs, norm_mix_w, [w_in[0].astype(BF16), ws], "norm_mix")
    win = _col_unshard(g_in)
    gs = gs.reshape(N_DEV, -1)
    cqkv = gs[:, :GDN_CONV * 192].reshape(N_DEV, GDN_CONV, 192).transpose(1, 0, 2).reshape(GDN_CONV, 3 * KEY_A)
    cffn = gs[:, PACK_UNIT:PACK_UNIT + FFN_CONV * 704].reshape(N_DEV, FFN_CONV, 704).transpose(1, 0, 2).reshape(FFN_CONV, FFN_W)
    cffn = _pad_rows(cffn, SUBLANES)
    w_all = jnp.concatenate([win[:, a:b] for a, b in W_IN_ORDER] + [jnp.zeros((D_MODEL, PROJ_W - D_IN), BF16)], axis=1)
    par = _pad_rows(jnp.pad(jnp.concatenate([a_log, dt_bias], axis=0), ((0, 0), (0, LANES - GDN_HEADS))), SUBLANES)
    table = jnp.pad(rel_bias[0], ((0, 0), (0, 3 * LANES - rel_bias.shape[-1]))).reshape(ATT_HEADS, 1, 3 * LANES)

    proj, qn, kn, va, ycv, g_ba, g_bb, g_out, g_up, g_down = _in_proj_prep(
        h1, w_all, cqkv, carry=([w[0].astype(BF16) for w in big_w[1:6]], (False,) * 5))
    wba, wbb, wup = _col_unshard(g_ba), _col_unshard(g_bb), _col_unshard(g_up)
    wout = g_out.reshape(D_MODEL, D_MODEL)
    wdown = g_down.reshape(D_FF, D_MODEL)
    oan, o_gdn, sprev, wst, ust, tst = _gdn_fwd(qn, kn, va, proj, par, gdn_norm_w)
    bias_q, bias_k = _att_bias(table)
    ob, lse, lse_t, x2, h2 = _att_merge_fwd(proj, bias_q, oan, xs, wba, wbb, wout, norm_ffn_w)
    up, u_ffn, dx3, dx3b, act, tail_sums = _ffn_fwd(h2, wup, cffn, conv_ffn_b, wdown, x2, tgt,
                                                    norm_final_w.reshape(1, D_MODEL))

    g_wdown = _mm_tn(act, dx3b, "dw_down", 512)
    dup, ffn_sums, dx2, dx2b, nffn_sums, r_down = _ffn_bwd(
        dx3b, wdown.T, up, u_ffn, cffn, wup.T, x2, norm_ffn_w, dx3,
        carry=([g_wdown.reshape(N_DEV, -1, D_MODEL).astype(BF16)], (True,)))
    g_wup = _mm_tn(h2, dup, "dw_up", 1408)
    dproj, d_oan, d_ob, g_wout, g_wba, g_wbb = _merge_bwd(dx2b, oan, ob, proj, wba, wbb, wout.T, wba.T, wbb.T)
    dproj, dlt_t, slabs = _att_dq(proj, bias_q, lse, d_ob, ob, dproj)
    dproj = _att_dkv(proj, bias_k, lse_t, dlt_t, d_ob, dproj)
    g_rel = _relbias_grad(slabs)[:, 0, :rel_bias.shape[-1]]
    dqn, dkn, dva, dproj, gdn_sums = _gdn_bwd(qn, kn, va, proj, par, gdn_norm_w, o_gdn, sprev, wst, ust, tst, d_oan, dproj)
    dproj, cq_sums = _gdn_prep_bwd(proj, ycv, cqkv, dqn, dkn, dva, dproj)
    g_wall, r_up = _mm_tn(h1, dproj, "dw_in", 1920, carry=([_col_shards(g_wup).astype(BF16)], (True,)))
    starts = np.cumsum([0] + [b - a for a, b in W_IN_ORDER])
    g_win = jnp.concatenate([g_wall[:, starts[i]:starts[i + 1]] for i in np.argsort([a for a, _ in W_IN_ORDER])], axis=1)
    g_conv = jnp.concatenate([_lane_rows(_col_shards(cq_sums[:GDN_CONV]).reshape(N_DEV, -1)),
                              _lane_rows(_col_shards(ffn_sums[:FFN_CONV]).reshape(N_DEV, -1))], axis=1)
    grad_x, _, nmix_sums, r_in, r_ba, r_bb, r_out, r_conv = _mm_rms_bwd(
        dproj, w_all.T, xs, norm_mix_w, dx2, "in_proj_bwd", MM_TM, 1152, carry=(
            [_col_shards(g_win).astype(BF16), _col_shards(g_wba).astype(BF16), _col_shards(g_wbb).astype(BF16),
             g_wout.reshape(N_DEV, -1, D_MODEL).astype(BF16), g_conv], (True,) * 5))

    small_g = (nmix_sums[0:1], gdn_sums[0:1, :GDN_HEADS], gdn_sums[1:2, :GDN_HEADS], gdn_sums[2:3], g_rel,
               nffn_sums[0:1], ffn_sums[FFN_CONV:FFN_CONV + 1], tail_sums[0:1], tail_sums[1:2, 0:1])
    r_small, = _exchange([_pack_lanes(small_g, SMALL_ROWS)], (False,), "all_gather_small_grads")
    recv = (r_in, r_ba, r_bb, r_out, r_up, r_down, r_conv, r_small)

    res = {}
    for i, (nm, tr) in enumerate((("w_in", 128), ("w_branch_a", KEY_A), ("w_branch_b", WIDTH_B), ("w_out", 128),
                                  ("w_up", 128), ("w_down", 176))):
        res[nm] = [o[None] for o in _adamw(recv[i], big_w[i][0], big_m[i][0], big_v[i][0], "adamw_" + nm, tr)]
    conv = _adamw(recv[6], _pack_lanes(big_w[6:], WS_ROWS), _pack_lanes(big_m[6:], WS_ROWS), _pack_lanes(big_v[6:], WS_ROWS),
                  "adamw_conv", WS_ROWS)
    conv = [_unpack_lanes(o, [w.shape for w in big_w[6:]]) for o in conv]
    res["conv_qkv_w"] = [o[0] for o in conv]
    res["conv_ffn_w"] = [o[1] for o in conv]
    small_shapes = [w.shape for w in small_w]
    zero = jnp.zeros((1,), F32)
    small = _adamw(recv[7], _pack_lanes(small_w + (zero,), SMALL_ROWS), _pack_lanes(small_m + (zero,), SMALL_ROWS),
                   _pack_lanes(small_v + (zero,), SMALL_ROWS), "adamw_replicated", SMALL_ROWS)
    small = [_unpack_lanes(o, small_shapes + [()]) for o in small]
    loss = small[0][-1]
    for j, nm in enumerate(("norm_mix_w", "a_log", "dt_bias", "gdn_norm_w", "rel_bias", "norm_ffn_w", "conv_ffn_b",
                            "norm_final_w")):
        res[nm] = [o[j] for o in small]

    names = ("norm_mix_w", "w_in", "conv_qkv_w", "a_log", "dt_bias", "gdn_norm_w", "w_branch_a", "w_branch_b", "rel_bias",
             "w_out", "norm_ffn_w", "w_up", "conv_ffn_w", "conv_ffn_b", "w_down", "norm_final_w")
    outs = [res[n][kind] for kind in range(4) for n in names]
    return (loss, grad_x[None], *outs)
```
